```python
import math
import jax, jax.numpy as jnp
from jax import lax
import numpy as np

D_MODEL = 1024
BATCH = 16
SEQ = 256
DEPTH = 1
DEC_BATCH = 4
DEC_SEQ = 1024
PAST_LEN = 512

GRID_W = 64
RET_HEADS = 4
RET_KD = D_MODEL // 8
RET_VD = D_MODEL // 4
RET_CHUNK = 128
DIFF_HEADS = 8
DIFF_HD = D_MODEL // 16
Q_BLOCK = 128
ROPE_BASE = 10000.0
N_GROUPS = 4
EXPERTS_PER_GROUP = 8
N_EXPERTS = N_GROUPS * EXPERTS_PER_GROUP
EXPERT_TOP_K = 2
EXPERT_FF = D_MODEL // 4
NORM_EPS = 1e-6

RET_QK_W = RET_HEADS * RET_KD
RET_V_W = RET_HEADS * RET_VD
DIFF_QK_W = DIFF_HEADS * 2 * DIFF_HD
DIFF_V_W = DIFF_HEADS * 2 * DIFF_HD
IN_WIDTHS = (RET_QK_W, RET_QK_W, RET_V_W, RET_V_W, DIFF_QK_W, DIFF_QK_W, DIFF_V_W, D_MODEL, D_MODEL)
IN_SPLITS = tuple(sum(IN_WIDTHS[:i + 1]) for i in range(len(IN_WIDTHS) - 1))
IN_W = sum(IN_WIDTHS)

kernel_name = 'hybrid_retention_diffattn_hmoe_diffusion_step'

F32 = jnp.float32


def rmsnorm(x, g):
    xf = x.astype(F32)
    y = xf * lax.rsqrt(jnp.mean(xf * xf, axis=-1, keepdims=True) + NORM_EPS)
    return (y * g.astype(F32)).astype(x.dtype)


def head_groupnorm(o, g):
    B, T = o.shape[0], o.shape[1]
    mu = jnp.mean(o, axis=-1, keepdims=True)
    d = o - mu
    y = d * lax.rsqrt(jnp.mean(d * d, axis=-1, keepdims=True) + NORM_EPS)
    return y.reshape(B, T, -1) * g.astype(F32)


def axial_rope(x):
    T, dh = x.shape[1], x.shape[-1]
    n_rows = T // GRID_W
    row = jnp.repeat(jnp.arange(n_rows), GRID_W).astype(F32)
    col = jnp.tile(jnp.arange(GRID_W), n_rows).astype(F32)
    axis_dim = dh // 2
    n_freq = axis_dim // 2
    inv = ROPE_BASE ** (-jnp.arange(n_freq, dtype=F32) / n_freq)

    def rot(part, pos):
        ang = pos[:, None] * inv[None, :]
        cos = jnp.cos(ang)[None, :, None, None, :]
        sin = jnp.sin(ang)[None, :, None, None, :]
        p1, p2 = part[..., :n_freq], part[..., n_freq:]
        return jnp.concatenate([p1 * cos - p2 * sin, p1 * sin + p2 * cos], axis=-1)

    xf = x.astype(F32)
    out = jnp.concatenate([rot(xf[..., :axis_dim], row), rot(xf[..., axis_dim:], col)], axis=-1)
    return out.astype(x.dtype)


def retention_chunkwise(q, k, v, log_gamma, s0):
    B, T, H, _ = q.shape
    dv = v.shape[-1]
    n = T // RET_CHUNK
    lg = log_gamma.astype(F32)
    pos = jnp.arange(RET_CHUNK, dtype=F32)
    rel = pos[:, None] - pos[None, :]
    intra = jnp.where(rel >= 0, jnp.exp(lg[:, None, None] * jnp.maximum(rel, 0.0)), 0.0)
    q_dec = jnp.exp(lg[:, None] * (pos + 1.0))[None]
    k_dec = jnp.exp(lg[:, None] * (RET_CHUNK - 1.0 - pos))[None]
    c_dec = jnp.exp(lg * RET_CHUNK)[None, :, None, None]

    def to_chunks(a):
        return a.astype(F32).reshape(B, n, RET_CHUNK, H, a.shape[-1]).swapaxes(0, 1)

    def step(s, qkv):
        qc, kc, vc = qkv
        scores = jnp.einsum('bihd,bjhd->bhij', qc, kc) * intra
        o = (jnp.einsum('bhij,bjhe->bihe', scores, vc)
             + jnp.einsum('bihd,bhi,bhde->bihe', qc, jnp.broadcast_to(q_dec, (B, H, RET_CHUNK)), s))
        s = c_dec * s + jnp.einsum('bjhd,bhj,bjhe->bhde', kc, jnp.broadcast_to(k_dec, (B, H, RET_CHUNK)), vc)
        return s, o

    s_final, o = lax.scan(step, s0.astype(F32), (to_chunks(q), to_chunks(k), to_chunks(v)))
    return o.swapaxes(0, 1).reshape(B, T, H, dv), s_final


def diff_attention(q, k, v, lam):
    B, Tq, H, _, dh = q.shape
    nb = Tq // Q_BLOCK
    qb = q.reshape(B, nb, Q_BLOCK, H, 2, dh).swapaxes(0, 1)
    vf = v.astype(F32)
    scale = dh ** -0.5

    def block(qblk):
        s = jnp.einsum('bqhcd,bkhcd->bchqk', qblk, k).astype(F32) * scale
        p = jax.nn.softmax(s, axis=-1)
        a = p[:, 0] - lam * p[:, 1]
        return jnp.einsum('bhqk,bkhe->bqhe', a, vf)

    o = lax.map(block, qb)
    return o.swapaxes(0, 1).reshape(B, Tq, H, v.shape[-1])


def mixer(h, ctx, lam, lam_init, w_in, lg_f, lg_b, ret_norm_g, subln_g, w_ret_o, w_diff_o, w_o):
    B, T, _ = h.shape
    rq, rk, rv, rg, dq, dk, dv, gr, gd = jnp.split(h @ w_in, IN_SPLITS, axis=-1)
    rq = rq.reshape(B, T, RET_HEADS, RET_KD)
    rk = rk.reshape(B, T, RET_HEADS, RET_KD) * (RET_KD ** -0.5)
    rv = rv.reshape(B, T, RET_HEADS, RET_VD)
    dq = dq.reshape(B, T, DIFF_HEADS, 2, DIFF_HD)
    dk = dk.reshape(B, T, DIFF_HEADS, 2, DIFF_HD)
    dv = dv.reshape(B, T, DIFF_HEADS, 2 * DIFF_HD)
    if ctx is None:
        s0_f = jnp.zeros((B, RET_HEADS, RET_KD, RET_VD), F32)
        s0_b = s0_f
        q_att, k_att, v_att = dq, dk, dv
    else:
        ctx_k, ctx_v, s0_f, s0_b = ctx
        q_att = axial_rope(dq)
        k_att = jnp.concatenate([axial_rope(dk), ctx_k.astype(dk.dtype)], axis=1)
        v_att = jnp.concatenate([dv, ctx_v.astype(dv.dtype)], axis=1)
    o_f, s_f = retention_chunkwise(rq, rk, rv, lg_f, s0_f)
    o_b, s_b = retention_chunkwise(rq[:, ::-1], rk[:, ::-1], rv[:, ::-1], lg_b, s0_b)
    o_r = head_groupnorm(o_f + o_b[:, ::-1], ret_norm_g)
    ret_out = (jax.nn.silu(rg.astype(F32)) * o_r).astype(h.dtype) @ w_ret_o
    o_d = diff_attention(q_att, k_att, v_att, lam)
    o_d = (o_d * lax.rsqrt(jnp.mean(o_d * o_d, axis=-1, keepdims=True) + NORM_EPS)
           * subln_g.astype(F32) * (1.0 - lam_init))
    diff_out = o_d.reshape(B, T, DIFF_V_W).astype(h.dtype) @ w_diff_o
    merged = jax.nn.sigmoid(gr) * ret_out + jax.nn.sigmoid(gd) * diff_out
    return merged @ w_o, dk, dv, s_f, s_b


def hier_moe(h, rgw, rgb, rew, reb, wg, wu, wd):
    B, T, D = h.shape
    hf = h.reshape(B * T, D)
    p_grp = jax.nn.softmax((hf @ rgw + rgb).astype(F32), axis=-1)
    p_top, g_idx = lax.top_k(p_grp, 1)
    e_logits = (hf @ rew + reb).astype(F32).reshape(-1, N_GROUPS, EXPERTS_PER_GROUP)
    e_logits = jnp.take_along_axis(e_logits, g_idx[:, :, None], axis=1)[:, 0]
    e_top, e_idx = lax.top_k(jax.nn.softmax(e_logits, axis=-1), EXPERT_TOP_K)
    w = p_top * e_top / jnp.sum(e_top, axis=-1, keepdims=True)
    gidx = g_idx * EXPERTS_PER_GROUP + e_idx
    combine = jnp.einsum('nk,nke->ne', w, jax.nn.one_hot(gidx, N_EXPERTS, dtype=F32))
    a = jax.nn.silu(jnp.einsum('nd,edf->nef', hf, wg)) * jnp.einsum('nd,edf->nef', hf, wu)
    y = jnp.einsum('nef,efd->nd', a * combine[:, :, None].astype(a.dtype), wd)
    return y.reshape(B, T, D)


def trunk_layer(x, mod, ctx, lam, lam_init, n1, n2, w_in, lg_f, lg_b, ret_norm_g, subln_g,
                w_ret_o, w_diff_o, w_o, rgw, rgb, rew, reb, wg, wu, wd):
    sh1, sc1, g1, sh2, sc2, g2 = jnp.split(mod, 6, axis=-1)
    h = rmsnorm(x, n1) * (1.0 + sc1) + sh1
    mix, k_c, v_c, s_f, s_b = mixer(h, ctx, lam, lam_init, w_in, lg_f, lg_b, ret_norm_g, subln_g,
                                    w_ret_o, w_diff_o, w_o)
    x = x + g1 * mix
    h2 = rmsnorm(x, n2) * (1.0 + sc2) + sh2
    x = x + g2 * hier_moe(h2, rgw, rgb, rew, reb, wg, wu, wd)
    return x, k_c, v_c, s_f, s_b


def setup_inputs(seed: int = 0) -> dict:
    key = jax.random.key(seed)
    ks = jax.random.split(key, 32)

    def nrm(k, shape, s):
        return jax.random.normal(k, shape, F32) * s

    base_logit = jnp.log(2.0 ** (5.0 + jnp.arange(RET_HEADS, dtype=F32)) - 1.0)
    return {
        'x_prompt': nrm(ks[0], (BATCH, SEQ, D_MODEL), 1.0),
        'x_sample': nrm(ks[1], (DEC_BATCH, DEC_SEQ, D_MODEL), 1.0),
        'cache_diff_k': nrm(ks[2], (DEC_BATCH, DEPTH, PAST_LEN, DIFF_HEADS, 2, DIFF_HD), 1.0),
        'cache_diff_v': nrm(ks[3], (DEC_BATCH, DEPTH, PAST_LEN, DIFF_HEADS, 2 * DIFF_HD), 1.0),
        'state_ret_fwd': nrm(ks[4], (DEC_BATCH, DEPTH, RET_HEADS, RET_KD, RET_VD), 0.5),
        'state_ret_bwd': nrm(ks[5], (DEC_BATCH, DEPTH, RET_HEADS, RET_KD, RET_VD), 0.5),
        'c': nrm(ks[6], (DEC_BATCH, D_MODEL), 1.0),
        'c_ctx': nrm(ks[7], (D_MODEL,), 1.0),
        'w_ada': nrm(ks[8], (DEPTH, D_MODEL, 6 * D_MODEL), 0.5 * D_MODEL ** -0.5),
        'b_ada': nrm(ks[9], (DEPTH, 6 * D_MODEL), 0.01),
        'norm1_g': 1.0 + nrm(ks[10], (DEPTH, D_MODEL), 0.01),
        'norm2_g': 1.0 + nrm(ks[11], (DEPTH, D_MODEL), 0.01),
        'w_in': nrm(ks[12], (DEPTH, D_MODEL, IN_W), D_MODEL ** -0.5),
        'ret_decay_fwd': base_logit[None] + nrm(ks[13], (DEPTH, RET_HEADS), 0.1),
        'ret_decay_bwd': base_logit[None] + nrm(ks[14], (DEPTH, RET_HEADS), 0.1),
        'ret_norm_g': 1.0 + nrm(ks[15], (DEPTH, RET_V_W), 0.01),
        'diff_lambda_q1': nrm(ks[16], (DEPTH, DIFF_HD), 0.1),
        'diff_lambda_k1': nrm(ks[17], (DEPTH, DIFF_HD), 0.1),
        'diff_lambda_q2': nrm(ks[18], (DEPTH, DIFF_HD), 0.1),
        'diff_lambda_k2': nrm(ks[19], (DEPTH, DIFF_HD), 0.1),
        'diff_subln_g': 1.0 + nrm(ks[20], (DEPTH, 2 * DIFF_HD), 0.01),
        'w_ret_o': nrm(ks[21], (DEPTH, RET_V_W, D_MODEL), RET_V_W ** -0.5),
        'w_diff_o': nrm(ks[22], (DEPTH, DIFF_V_W, D_MODEL), DIFF_V_W ** -0.5),
        'w_o': nrm(ks[23], (DEPTH, D_MODEL, D_MODEL), D_MODEL ** -0.5),
        'router_group_w': nrm(ks[24], (DEPTH, D_MODEL, N_GROUPS), D_MODEL ** -0.5),
        'router_group_b': nrm(ks[25], (DEPTH, N_GROUPS), 0.01),
        'router_expert_w': nrm(ks[26], (DEPTH, D_MODEL, N_EXPERTS), D_MODEL ** -0.5),
        'router_expert_b': nrm(ks[27], (DEPTH, N_EXPERTS), 0.01),
        'moe_w_gate': nrm(ks[28], (DEPTH, N_EXPERTS, D_MODEL, EXPERT_FF), D_MODEL ** -0.5),
        'moe_w_up': nrm(ks[29], (DEPTH, N_EXPERTS, D_MODEL, EXPERT_FF), D_MODEL ** -0.5),
        'moe_w_down': nrm(ks[30], (DEPTH, N_EXPERTS, EXPERT_FF, D_MODEL), EXPERT_FF ** -0.5),
        'final_norm_g': 1.0 + nrm(ks[31], (D_MODEL,), 0.01),
    }


def reference(x_prompt, x_sample, cache_diff_k, cache_diff_v, state_ret_fwd, state_ret_bwd, c, c_ctx,
              w_ada, b_ada, norm1_g, norm2_g, w_in, ret_decay_fwd, ret_decay_bwd, ret_norm_g,
              diff_lambda_q1, diff_lambda_k1, diff_lambda_q2, diff_lambda_k2, diff_subln_g,
              w_ret_o, w_diff_o, w_o, router_group_w, router_group_b, router_expert_w, router_expert_b,
              moe_w_gate, moe_w_up, moe_w_down, final_norm_g):
    xp, xs = x_prompt, x_sample
    silu_ctx = jax.nn.silu(c_ctx)
    silu_c = jax.nn.silu(c)
    ks_l, vs_l, sf_l, sb_l = [], [], [], []
    for l in range(DEPTH):
        lam_init = 0.8 - 0.6 * math.exp(-0.3 * l)
        lam = (jnp.exp(jnp.sum(diff_lambda_q1[l].astype(F32) * diff_lambda_k1[l].astype(F32)))
               - jnp.exp(jnp.sum(diff_lambda_q2[l].astype(F32) * diff_lambda_k2[l].astype(F32))) + lam_init)
        lg_f = jax.nn.log_sigmoid(ret_decay_fwd[l].astype(F32))
        lg_b = jax.nn.log_sigmoid(ret_decay_bwd[l].astype(F32))
        shared = (lam, lam_init, norm1_g[l], norm2_g[l], w_in[l], lg_f, lg_b, ret_norm_g[l], diff_subln_g[l],
                  w_ret_o[l], w_diff_o[l], w_o[l], router_group_w[l], router_group_b[l],
                  router_expert_w[l], router_expert_b[l], moe_w_gate[l], moe_w_up[l], moe_w_down[l])
        mod_p = silu_ctx @ w_ada[l] + b_ada[l]
        mod_s = (silu_c @ w_ada[l] + b_ada[l])[:, None, :]
        xp, k_c, v_c, s_f, s_b = trunk_layer(xp, mod_p, None, *shared)
        ks_l.append(k_c)
        vs_l.append(v_c)
        sf_l.append(s_f.astype(xp.dtype))
        sb_l.append(s_b.astype(xp.dtype))
        ctx = (cache_diff_k[:, l], cache_diff_v[:, l], state_ret_fwd[:, l], state_ret_bwd[:, l])
        xs, _, _, _, _ = trunk_layer(xs, mod_s, ctx, *shared)
    y_prompt = rmsnorm(xp, final_norm_g)
    y_sample = rmsnorm(xs, final_norm_g)
    new_diff_k = jnp.stack(ks_l, axis=1)
    new_diff_v = jnp.stack(vs_l, axis=1)
    new_ret_fwd = jnp.stack(sf_l, axis=1)
    new_ret_bwd = jnp.stack(sb_l, axis=1)
    return (y_prompt, y_sample, new_diff_k, new_diff_v, new_ret_fwd, new_ret_bwd)
```

```python
import functools
import math

import jax
import jax.numpy as jnp
from jax import lax
from jax.experimental import pallas as pl
from jax.experimental.pallas import tpu as pltpu

F32 = jnp.float32
BF16 = jnp.bfloat16

D = 1024
N_PROMPT = 16 * 256
N_SAMPLE = 4 * 1024
N_TOK = N_PROMPT + N_SAMPLE
T_P = 256
T_S = 1024
PAST = 512
GRID_W = 64
RET_H = 4
RET_KD = 128
RET_VD = 256
DIFF_H = 8
DIFF_HD = 64
ROPE_BASE = 10000.0
N_GROUPS = 4
EPG = 8
N_EXP = 32
FF = 256
EPS = 1e-6
IN_W = 8192
LANE = 128
N_CHUNK = IN_W // LANE
C_RQ, C_RK, C_RV, C_RG, C_DQ, C_DK, C_DV, C_GR, C_GD = 0, 4, 8, 16, 24, 32, 40, 48, 56
ROUTE_W = 128
SEQ_BLK = 1024
N_PBLK = N_PROMPT // SEQ_BLK
VMEM_LIMIT = 56 * 1024 * 1024


def _cparams(sem):
    return pltpu.CompilerParams(dimension_semantics=sem, vmem_limit_bytes=VMEM_LIMIT)


def _mod_row(tile_rows):
    def f(i):
        start = i * tile_rows
        return jnp.where(start < N_PROMPT, 0, 1 + (start - N_PROMPT) // T_S)
    return f


def _rms(x):
    return x * lax.rsqrt(jnp.mean(x * x, axis=-1, keepdims=True) + EPS)


def _mod_kernel(c_ref, w_ref, b_ref, o_ref):
    s = jax.nn.silu(c_ref[...])
    o_ref[...] = jnp.dot(s.astype(BF16), w_ref[...].astype(BF16),
                         preferred_element_type=F32) + b_ref[...]


def _modulation(cvec, w_ada, b_ada):
    tn = 1536
    return pl.pallas_call(
        _mod_kernel,
        out_shape=jax.ShapeDtypeStruct((8, 6 * D), F32),
        grid=(6 * D // tn,),
        in_specs=[pl.BlockSpec((8, D), lambda j: (0, 0)),
                  pl.BlockSpec((D, tn), lambda j: (0, j)),
                  pl.BlockSpec((1, tn), lambda j: (0, j))],
        out_specs=pl.BlockSpec((8, tn), lambda j: (0, j)),
        compiler_params=_cparams(("arbitrary",)),
        name="mod",
    )(cvec, w_ada, b_ada)


IP_TM = 512
IP_TN = 2048
IP_NPT = N_PROMPT // IP_TM


def _inproj_kernel(xp_ref, xs_ref, mod_ref, n1_ref, w_ref, proj_ref, k32_ref, v32_ref, h_scr):
    i = pl.program_id(0)
    j = pl.program_id(1)

    def norm(x_ref):
        mod = mod_ref[...]
        y = _rms(x_ref[...]) * n1_ref[...]
        h_scr[...] = (y * (1.0 + mod[:, D:2 * D]) + mod[:, 0:D]).astype(BF16)

    @pl.when(jnp.logical_and(j == 0, i < IP_NPT))
    def _():
        norm(xp_ref)

    @pl.when(jnp.logical_and(j == 0, i >= IP_NPT))
    def _():
        norm(xs_ref)

    acc = jnp.dot(h_scr[...], w_ref[...], preferred_element_type=F32)
    for c in range(IP_TN // LANE):
        proj_ref[c] = acc[:, c * LANE:(c + 1) * LANE].astype(BF16)

    @pl.when(jnp.logical_and(j == 2, i < IP_NPT))
    def _():
        k32_ref[...] = acc[:, :D]
        v32_ref[...] = acc[:, D:]


def _inproj(xp, xs, mod3, n1, w_in_bf):
    npt = IP_NPT
    cpt = IP_TN // LANE
    return pl.pallas_call(
        _inproj_kernel,
        out_shape=(jax.ShapeDtypeStruct((N_CHUNK, N_TOK, LANE), BF16),
                   jax.ShapeDtypeStruct((N_PROMPT, D), F32),
                   jax.ShapeDtypeStruct((N_PROMPT, D), F32)),
        grid=(N_TOK // IP_TM, IN_W // IP_TN),
        in_specs=[pl.BlockSpec((IP_TM, D), lambda i, j: (jnp.minimum(i, npt - 1), 0)),
                  pl.BlockSpec((IP_TM, D), lambda i, j: (jnp.maximum(i - npt, 0), 0)),
                  pl.BlockSpec((None, 1, 6 * D), lambda i, j: (_mod_row(IP_TM)(i), 0, 0)),
                  pl.BlockSpec((1, D), lambda i, j: (0, 0)),
                  pl.BlockSpec((D, IP_TN), lambda i, j: (0, j))],
        out_specs=(pl.BlockSpec((cpt, IP_TM, LANE), lambda i, j: (j, i, 0)),
                   pl.BlockSpec((IP_TM, D), lambda i, j: (jnp.minimum(i, npt - 1), 0)),
                   pl.BlockSpec((IP_TM, D), lambda i, j: (jnp.minimum(i, npt - 1), 0))),
        scratch_shapes=[pltpu.VMEM((IP_TM, D), BF16)],
        compiler_params=_cparams(("arbitrary", "arbitrary")),
        name="inproj",
    )(xp, xs, mod3, n1, w_in_bf)


def _decay_mask(t, lgf, lgb):
    ii = lax.broadcasted_iota(jnp.int32, (t, t), 0)
    jj = lax.broadcasted_iota(jnp.int32, (t, t), 1)
    rel = (ii - jj).astype(F32)
    e = jnp.exp(jnp.where(rel >= 0.0, lgf, -lgb) * rel)
    return jnp.where(rel == 0.0, 2.0, e) * (RET_KD ** -0.5)


def _ret_kernel(lg_ref, q_ref, k_ref, v_ref, rg_ref, s0f_ref, s0b_ref, g_ref,
                o_ref, sf_ref, sb_ref):
    i = pl.program_id(0)
    h = pl.program_id(1)
    lgf = lg_ref[0, h]
    lgb = lg_ref[1, h]
    gain = g_ref[...]
    nt = (((1,), (1,)), ((), ()))
    tn = (((0,), (0,)), ((), ()))

    def finish(o, rg):
        d = o - jnp.mean(o, axis=-1, keepdims=True)
        y = d * lax.rsqrt(jnp.mean(d * d, axis=-1, keepdims=True) + EPS) * gain
        return (jax.nn.silu(rg.astype(F32)) * y).astype(BF16)

    def vcat(r):
        return jnp.concatenate([v_ref[0, r, :], v_ref[1, r, :]], axis=1)

    def gcat(r):
        return jnp.concatenate([rg_ref[0, r, :], rg_ref[1, r, :]], axis=1)

    @pl.when(i < N_PBLK)
    def _():
        dm = _decay_mask(T_P, lgf, lgb)
        t = lax.broadcasted_iota(jnp.int32, (T_P, 1), 0).astype(F32)
        kdf = jnp.exp(lgf * (T_P - 1.0 - t)) * (RET_KD ** -0.5)
        kdb = jnp.exp(lgb * t) * (RET_KD ** -0.5)
        for s in range(SEQ_BLK // T_P):
            r = slice(s * T_P, (s + 1) * T_P)
            q = q_ref[r, :]
            k = k_ref[r, :]
            v = vcat(r)
            sc = lax.dot_general(q, k, nt, preferred_element_type=F32)
            o = jnp.dot((sc * dm).astype(BF16), v, preferred_element_type=F32)
            o_ref[r, :] = finish(o, gcat(r))
            kf = k.astype(F32)
            sf_ref[s] = lax.dot_general((kf * kdf).astype(BF16), v, tn, preferred_element_type=F32)
            sb_ref[s] = lax.dot_general((kf * kdb).astype(BF16), v, tn, preferred_element_type=F32)

    @pl.when(i >= N_PBLK)
    def _():
        r = slice(0, T_S)
        dm = _decay_mask(T_S, lgf, lgb)
        t = lax.broadcasted_iota(jnp.int32, (T_S, 1), 0).astype(F32)
        q = q_ref[...]
        k = k_ref[...]
        v = vcat(r)
        sc = lax.dot_general(q, k, nt, preferred_element_type=F32)
        o = jnp.dot((sc * dm).astype(BF16), v, preferred_element_type=F32)
        qf = q.astype(F32)
        o = o + jnp.dot((qf * jnp.exp(lgf * (t + 1.0))).astype(BF16), s0f_ref[...].astype(BF16),
                        preferred_element_type=F32)
        o = o + jnp.dot((qf * jnp.exp(lgb * (T_S - t))).astype(BF16), s0b_ref[...].astype(BF16),
                        preferred_element_type=F32)
        o_ref[...] = finish(o, gcat(r))


def _retention(lg, proj, s0f, s0b, gnorm):
    nb = N_TOK // SEQ_BLK
    spb = SEQ_BLK // T_P
    smp = lambda i: jnp.maximum(i - N_PBLK, 0)
    pmt = lambda i: jnp.minimum(i, N_PBLK - 1)
    pmh = lambda i, h: jnp.where(i < N_PBLK, h, RET_H - 1)
    return pl.pallas_call(
        _ret_kernel,
        out_shape=(jax.ShapeDtypeStruct((N_TOK, RET_H * RET_VD), BF16),
                   jax.ShapeDtypeStruct((16, RET_H, RET_KD, RET_VD), F32),
                   jax.ShapeDtypeStruct((16, RET_H, RET_KD, RET_VD), F32)),
        grid=(nb, RET_H),
        in_specs=[pl.BlockSpec(memory_space=pltpu.SMEM),
                  pl.BlockSpec((None, SEQ_BLK, LANE), lambda i, h: (C_RQ + h, i, 0)),
                  pl.BlockSpec((None, SEQ_BLK, LANE), lambda i, h: (C_RK + h, i, 0)),
                  pl.BlockSpec((2, SEQ_BLK, LANE), lambda i, h: (C_RV // 2 + h, i, 0)),
                  pl.BlockSpec((2, SEQ_BLK, LANE), lambda i, h: (C_RG // 2 + h, i, 0)),
                  pl.BlockSpec((None, None, RET_KD, RET_VD), lambda i, h: (smp(i), h, 0, 0)),
                  pl.BlockSpec((None, None, RET_KD, RET_VD), lambda i, h: (smp(i), h, 0, 0)),
                  pl.BlockSpec((1, RET_VD), lambda i, h: (0, h))],
        out_specs=(pl.BlockSpec((SEQ_BLK, RET_VD), lambda i, h: (i, h)),
                   pl.BlockSpec((spb, None, RET_KD, RET_VD), lambda i, h: (pmt(i), pmh(i, h), 0, 0)),
                   pl.BlockSpec((spb, None, RET_KD, RET_VD), lambda i, h: (pmt(i), pmh(i, h), 0, 0))),
        compiler_params=_cparams(("arbitrary", "arbitrary")),
        name="ret",
    )(lg, proj, proj, proj, proj, s0f, s0b, gnorm)


ATT_TQ = 256


def _attn_kernel(sc_ref, q_ref, k_ref, v_ref, ck_ref, cv_ref, cos_ref, sin_ref, g_ref,
                 o_ref, q_scr, k_scr, v_scr):
    i = pl.program_id(0)
    lam = sc_ref[0]
    out_scale = sc_ref[1]
    gain = g_ref[...] * out_scale
    lane = lax.broadcasted_iota(jnp.int32, (1, LANE), 1)
    first = lane < DIFF_HD
    nt = (((1,), (1,)), ((), ()))

    def attend(q, k, v):
        zero = jnp.zeros_like(q)
        s0 = lax.dot_general(jnp.where(first, q, zero), k, nt, preferred_element_type=F32)
        s1 = lax.dot_general(jnp.where(first, zero, q), k, nt, preferred_element_type=F32)
        e0 = jnp.exp(s0 - jnp.max(s0, axis=-1, keepdims=True))
        e1 = jnp.exp(s1 - jnp.max(s1, axis=-1, keepdims=True))
        r0 = 1.0 / jnp.sum(e0, axis=-1, keepdims=True)
        r1 = lam / jnp.sum(e1, axis=-1, keepdims=True)
        a = (e0 * r0 - e1 * r1).astype(BF16)
        o = jnp.dot(a, v, preferred_element_type=F32)
        return _rms(o) * gain

    @pl.when(i < N_PBLK)
    def _():
        for s in range(SEQ_BLK // T_P):
            r = slice(s * T_P, (s + 1) * T_P)
            q = (q_ref[r, :].astype(F32) * (DIFF_HD ** -0.5)).astype(BF16)
            o_ref[r, :] = attend(q, k_ref[r, :], v_ref[r, :]).astype(BF16)

    @pl.when(i >= N_PBLK)
    def _():
        cos = cos_ref[...]
        sin = sin_ref[...]
        low = (lax.broadcasted_iota(jnp.int32, (T_S, LANE), 1) & 16) == 0

        def rope(x):
            xs = jnp.where(low, pltpu.roll(x, LANE - 16, 1), pltpu.roll(x, 16, 1))
            return x * cos + xs * sin

        q_scr[...] = (rope(q_ref[...].astype(F32)) * (DIFF_HD ** -0.5)).astype(BF16)
        k_scr[0:T_S, :] = rope(k_ref[...].astype(F32)).astype(BF16)
        k_scr[T_S:T_S + PAST, :] = ck_ref[...].astype(BF16)
        v_scr[0:T_S, :] = v_ref[...]
        v_scr[T_S:T_S + PAST, :] = cv_ref[...].astype(BF16)

        def body(b, carry):
            rows = pl.ds(pl.multiple_of(b * ATT_TQ, ATT_TQ), ATT_TQ)
            o_ref[rows, :] = attend(q_scr[rows, :], k_scr[...], v_scr[...]).astype(BF16)
            return carry

        lax.fori_loop(0, T_S // ATT_TQ, body, 0)


def _attention(scal, proj, cache_k, cache_v, cos_t, sin_t, subln_g):
    nb = N_TOK // SEQ_BLK
    smp = lambda i: jnp.maximum(i - N_PBLK, 0)
    return pl.pallas_call(
        _attn_kernel,
        out_shape=jax.ShapeDtypeStruct((N_TOK, DIFF_H * 2 * DIFF_HD), BF16),
        grid=(nb, DIFF_H),
        in_specs=[pl.BlockSpec(memory_space=pltpu.SMEM),
                  pl.BlockSpec((None, SEQ_BLK, LANE), lambda i, h: (C_DQ + h, i, 0)),
                  pl.BlockSpec((None, SEQ_BLK, LANE), lambda i, h: (C_DK + h, i, 0)),
                  pl.BlockSpec((None, SEQ_BLK, LANE), lambda i, h: (C_DV + h, i, 0)),
                  pl.BlockSpec((None, PAST, LANE), lambda i, h: (smp(i), 0, h)),
                  pl.BlockSpec((None, PAST, LANE), lambda i, h: (smp(i), 0, h)),
                  pl.BlockSpec((T_S, LANE), lambda i, h: (0, 0)),
                  pl.BlockSpec((T_S, LANE), lambda i, h: (0, 0)),
                  pl.BlockSpec((1, LANE), lambda i, h: (0, 0))],
        out_specs=pl.BlockSpec((SEQ_BLK, LANE), lambda i, h: (i, h)),
        scratch_shapes=[pltpu.VMEM((T_S, LANE), BF16),
                        pltpu.VMEM((T_S + PAST, LANE), BF16),
                        pltpu.VMEM((T_S + PAST, LANE), BF16)],
        compiler_params=_cparams(("arbitrary", "arbitrary")),
        name="attn",
    )(scal, proj, proj, proj, cache_k, cache_v, cos_t, sin_t, subln_g)


def _rope_tables():
    n_rows = T_S // GRID_W
    row = jnp.repeat(jnp.arange(n_rows), GRID_W).astype(F32)
    col = jnp.tile(jnp.arange(GRID_W), n_rows).astype(F32)
    n_freq = DIFF_HD // 4
    inv = ROPE_BASE ** (-jnp.arange(n_freq, dtype=F32) / n_freq)

    def axis_tables(pos):
        ang = pos[:, None] * inv[None, :]
        c = jnp.cos(ang)
        s = jnp.sin(ang)
        return jnp.concatenate([c, c], axis=-1), jnp.concatenate([-s, s], axis=-1)

    cr, sr = axis_tables(row)
    cc, sc = axis_tables(col)
    cos_h = jnp.concatenate([cr, cc], axis=-1)
    sin_h = jnp.concatenate([sr, sc], axis=-1)
    return jnp.concatenate([cos_h, cos_h], axis=-1), jnp.concatenate([sin_h, sin_h], axis=-1)


OP_TM = 512
OP_NPT = N_PROMPT // OP_TM


def _outproj_kernel(ra_ref, da_ref, gr_ref, gd_ref, xp_ref, xs_ref, mod_ref, n2_ref,
                    wro_ref, wdo_ref, wo_ref, wr_ref, br_ref,
                    x1_ref, h2_ref, lg_ref, m_scr):
    i = pl.program_id(0)
    ret_out = jnp.dot(ra_ref[...], wro_ref[...], preferred_element_type=F32)
    diff_out = jnp.dot(da_ref[...], wdo_ref[...], preferred_element_type=F32)
    for c in range(D // LANE):
        sl = slice(c * LANE, (c + 1) * LANE)
        m = (jax.nn.sigmoid(gr_ref[c].astype(F32)) * ret_out[:, sl]
             + jax.nn.sigmoid(gd_ref[c].astype(F32)) * diff_out[:, sl])
        m_scr[:, sl] = m.astype(BF16)
    mix = jnp.dot(m_scr[...], wo_ref[...], preferred_element_type=F32)
    mod = mod_ref[...]

    def fin(x_ref):
        x1 = x_ref[...] + mod[:, 2 * D:3 * D] * mix
        x1_ref[...] = x1
        h2 = (_rms(x1) * n2_ref[...] * (1.0 + mod[:, 4 * D:5 * D]) + mod[:, 3 * D:4 * D]).astype(BF16)
        h2_ref[...] = h2
        lg_ref[...] = jnp.dot(h2, wr_ref[...], preferred_element_type=F32) + br_ref[...]

    @pl.when(i < OP_NPT)
    def _():
        fin(xp_ref)

    @pl.when(i >= OP_NPT)
    def _():
        fin(xs_ref)


def _outproj(ret_act, diff_act, proj, xp, xs, mod3, n2, wro, wdo, wo, wr, br):
    npt = OP_NPT
    full = lambda i: (0, 0)
    return pl.pallas_call(
        _outproj_kernel,
        out_shape=(jax.ShapeDtypeStruct((N_TOK, D), F32),
                   jax.ShapeDtypeStruct((N_TOK, D), BF16),
                   jax.ShapeDtypeStruct((N_TOK, ROUTE_W), F32)),
        grid=(N_TOK // OP_TM,),
        in_specs=[pl.BlockSpec((OP_TM, D), lambda i: (i, 0)),
                  pl.BlockSpec((OP_TM, D), lambda i: (i, 0)),
                  pl.BlockSpec((8, OP_TM, LANE), lambda i: (C_GR // 8, i, 0)),
                  pl.BlockSpec((8, OP_TM, LANE), lambda i: (C_GD // 8, i, 0)),
                  pl.BlockSpec((OP_TM, D), lambda i: (jnp.minimum(i, npt - 1), 0)),
                  pl.BlockSpec((OP_TM, D), lambda i: (jnp.maximum(i - npt, 0), 0)),
                  pl.BlockSpec((None, 1, 6 * D), lambda i: (_mod_row(OP_TM)(i), 0, 0)),
                  pl.BlockSpec((1, D), full),
                  pl.BlockSpec((D, D), full),
                  pl.BlockSpec((D, D), full),
                  pl.BlockSpec((D, D), full),
                  pl.BlockSpec((D, ROUTE_W), full),
                  pl.BlockSpec((1, ROUTE_W), full)],
        out_specs=(pl.BlockSpec((OP_TM, D), lambda i: (i, 0)),
                   pl.BlockSpec((OP_TM, D), lambda i: (i, 0)),
                   pl.BlockSpec((OP_TM, ROUTE_W), lambda i: (i, 0))),
        scratch_shapes=[pltpu.VMEM((OP_TM, D), BF16)],
        compiler_params=_cparams(("arbitrary",)),
        name="outproj",
    )(ret_act, diff_act, proj, proj, xp, xs, mod3, n2, wro, wdo, wo, wr, br)


def _route_kernel(lg_ref, cw_ref):
    lg = lg_ref[...]
    lane = lax.broadcasted_iota(jnp.int32, lg.shape, 1)
    neg = jnp.float32(-jnp.inf)
    big = jnp.int32(ROUTE_W)

    def first_where(cond):
        return jnp.min(jnp.where(cond, lane, big), axis=-1, keepdims=True)

    is_g = lane < N_GROUPS
    gl = jnp.where(is_g, lg, neg)
    gmax = jnp.max(gl, axis=-1, keepdims=True)
    gsum = jnp.sum(jnp.where(is_g, jnp.exp(lg - gmax), 0.0), axis=-1, keepdims=True)
    p_top = 1.0 / gsum
    g_idx = first_where(gl == gmax)
    in_grp = jnp.logical_and(lane >= N_GROUPS, ((lane - N_GROUPS) >> 3) == g_idx)
    el = jnp.where(in_grp, lg, neg)
    emax = jnp.max(el, axis=-1, keepdims=True)
    ee = jnp.where(in_grp, jnp.exp(lg - emax), 0.0)
    ep = ee / jnp.sum(ee, axis=-1, keepdims=True)
    ep = jnp.where(in_grp, ep, -1.0)
    e1 = jnp.max(ep, axis=-1, keepdims=True)
    i1 = first_where(ep == e1)
    ep2 = jnp.where(lane == i1, -1.0, ep)
    e2 = jnp.max(ep2, axis=-1, keepdims=True)
    i2 = first_where(ep2 == e2)
    den = e1 + e2
    cw_ref[...] = (jnp.where(lane == i1, p_top * e1 / den, 0.0)
                   + jnp.where(lane == i2, p_top * e2 / den, 0.0))


def _route(logits):
    tm = 1024
    return pl.pallas_call(
        _route_kernel,
        out_shape=jax.ShapeDtypeStruct((N_TOK, ROUTE_W), F32),
        grid=(N_TOK // tm,),
        in_specs=[pl.BlockSpec((tm, ROUTE_W), lambda i: (i, 0))],
        out_specs=pl.BlockSpec((tm, ROUTE_W), lambda i: (i, 0)),
        compiler_params=_cparams(("arbitrary",)),
        name="route",
    )(logits)


MOE_TM = 1024
MOE_EB = 4


def _moe_kernel(h_ref, cw_ref, wg_ref, wu_ref, wd_ref, y_ref):
    eb = pl.program_id(1)

    @pl.when(eb == 0)
    def _():
        y_ref[...] = jnp.zeros_like(y_ref)

    h = h_ref[...]
    cw = cw_ref[...]
    lane = lax.broadcasted_iota(jnp.int32, cw.shape, 1)
    acc = jnp.zeros(y_ref.shape, F32)
    for k in range(MOE_EB):
        e = eb * MOE_EB + k
        w = jnp.sum(jnp.where(lane == e + N_GROUPS, cw, 0.0), axis=-1, keepdims=True)
        a = (jax.nn.silu(jnp.dot(h, wg_ref[k], preferred_element_type=F32))
             * jnp.dot(h, wu_ref[k], preferred_element_type=F32))
        acc = acc + jnp.dot((a * w).astype(BF16), wd_ref[k], preferred_element_type=F32)
    y_ref[...] += acc


def _moe(h2, cw, wg, wu, wd):
    return pl.pallas_call(
        _moe_kernel,
        out_shape=jax.ShapeDtypeStruct((N_TOK, D), F32),
        grid=(N_TOK // MOE_TM, N_EXP // MOE_EB),
        in_specs=[pl.BlockSpec((MOE_TM, D), lambda i, e: (i, 0)),
                  pl.BlockSpec((MOE_TM, ROUTE_W), lambda i, e: (i, 0)),
                  pl.BlockSpec((MOE_EB, D, FF), lambda i, e: (e, 0, 0)),
                  pl.BlockSpec((MOE_EB, D, FF), lambda i, e: (e, 0, 0)),
                  pl.BlockSpec((MOE_EB, FF, D), lambda i, e: (e, 0, 0))],
        out_specs=pl.BlockSpec((MOE_TM, D), lambda i, e: (i, 0)),
        compiler_params=_cparams(("arbitrary", "arbitrary")),
        name="moe",
    )(h2, cw, wg, wu, wd)


FN_TM = 512
FN_NPT = N_PROMPT // FN_TM


def _final_kernel(x1_ref, y_ref, mod_ref, g_ref, yp_ref, ys_ref):
    i = pl.program_id(0)
    mod = mod_ref[...]
    out = _rms(x1_ref[...] + mod[:, 5 * D:6 * D] * y_ref[...]) * g_ref[...]

    @pl.when(i < FN_NPT)
    def _():
        yp_ref[...] = out

    @pl.when(i >= FN_NPT)
    def _():
        ys_ref[...] = out


def _final(x1, y, mod3, fg):
    npt = FN_NPT
    return pl.pallas_call(
        _final_kernel,
        out_shape=(jax.ShapeDtypeStruct((N_PROMPT, D), F32),
                   jax.ShapeDtypeStruct((N_SAMPLE, D), F32)),
        grid=(N_TOK // FN_TM,),
        in_specs=[pl.BlockSpec((FN_TM, D), lambda i: (i, 0)),
                  pl.BlockSpec((FN_TM, D), lambda i: (i, 0)),
                  pl.BlockSpec((None, 1, 6 * D), lambda i: (_mod_row(FN_TM)(i), 0, 0)),
                  pl.BlockSpec((1, D), lambda i: (0, 0))],
        out_specs=(pl.BlockSpec((FN_TM, D), lambda i: (jnp.minimum(i, npt - 1), 0)),
                   pl.BlockSpec((FN_TM, D), lambda i: (jnp.maximum(i - npt, 0), 0))),
        compiler_params=_cparams(("arbitrary",)),
        name="final",
    )(x1, y, mod3, fg)


def kernel(x_prompt, x_sample, cache_diff_k, cache_diff_v, state_ret_fwd, state_ret_bwd, c, c_ctx,
           w_ada, b_ada, norm1_g, norm2_g, w_in, ret_decay_fwd, ret_decay_bwd, ret_norm_g,
           diff_lambda_q1, diff_lambda_k1, diff_lambda_q2, diff_lambda_k2, diff_subln_g,
           w_ret_o, w_diff_o, w_o, router_group_w, router_group_b, router_expert_w, router_expert_b,
           moe_w_gate, moe_w_up, moe_w_down, final_norm_g):
    l = 0
    lam_init = 0.8 - 0.6 * math.exp(-0.3 * l)
    lam = (jnp.exp(jnp.sum(diff_lambda_q1[l].astype(F32) * diff_lambda_k1[l].astype(F32)))
           - jnp.exp(jnp.sum(diff_lambda_q2[l].astype(F32) * diff_lambda_k2[l].astype(F32))) + lam_init)
    attn_scal = jnp.stack([lam, jnp.float32(1.0 - lam_init)]).astype(F32)
    lg = jnp.stack([jax.nn.log_sigmoid(ret_decay_fwd[l].astype(F32)),
                    jax.nn.log_sigmoid(ret_decay_bwd[l].astype(F32))])

    xp = x_prompt.reshape(N_PROMPT, D)
    xs = x_sample.reshape(N_SAMPLE, D)
    cvec = jnp.concatenate([c_ctx[None, :], c, jnp.zeros((3, D), F32)], axis=0)
    mod3 = _modulation(cvec, w_ada[l], b_ada[l][None, :]).reshape(8, 1, 6 * D)

    proj, k32, v32 = _inproj(xp, xs, mod3, norm1_g[l][None, :], w_in[l].astype(BF16))

    ret_act, s_f, s_b = _retention(lg, proj, state_ret_fwd[:, l], state_ret_bwd[:, l],
                                   ret_norm_g[l][None, :])
    cos_t, sin_t = _rope_tables()
    diff_act = _attention(attn_scal, proj,
                          cache_diff_k[:, l].reshape(4, PAST, D), cache_diff_v[:, l].reshape(4, PAST, D),
                          cos_t, sin_t, diff_subln_g[l][None, :])

    wr = jnp.concatenate([router_group_w[l], router_expert_w[l],
                          jnp.zeros((D, ROUTE_W - N_GROUPS - N_EXP), F32)], axis=1).astype(BF16)
    br = jnp.concatenate([router_group_b[l], router_expert_b[l],
                          jnp.zeros((ROUTE_W - N_GROUPS - N_EXP,), F32)])[None, :]
    x1, h2, logits = _outproj(ret_act, diff_act, proj, xp, xs, mod3, norm2_g[l][None, :],
                              w_ret_o[l].astype(BF16), w_diff_o[l].astype(BF16), w_o[l].astype(BF16),
                              wr, br)
    cw = _route(logits)
    y = _moe(h2, cw, moe_w_gate[l].astype(BF16), moe_w_up[l].astype(BF16), moe_w_down[l].astype(BF16))
    yp, ys = _final(x1, y, mod3, final_norm_g[None, :])

    return (yp.reshape(16, T_P, D), ys.reshape(4, T_S, D),
            k32.reshape(16, 1, T_P, DIFF_H, 2, DIFF_HD), v32.reshape(16, 1, T_P, DIFF_H, 2 * DIFF_HD),
            s_f.reshape(16, 1, RET_H, RET_KD, RET_VD), s_b.reshape(16, 1, RET_H, RET_KD, RET_VD))
```

```python
import functools
import math

import jax
import jax.numpy as jnp
from jax import lax
from jax.experimental import pallas as pl
from jax.experimental.pallas import tpu as pltpu

F32 = jnp.float32
BF16 = jnp.bfloat16

D = 1024
N_PROMPT = 16 * 256
N_SAMPLE = 4 * 1024
N_TOK = N_PROMPT + N_SAMPLE
T_P = 256
T_S = 1024
PAST = 512
GRID_W = 64
RET_H = 4
RET_KD = 128
RET_VD = 256
DIFF_H = 8
DIFF_HD = 64
ROPE_BASE = 10000.0
N_GROUPS = 4
EPG = 8
N_EXP = 32
FF = 256
EPS = 1e-6
IN_W = 8192
LANE = 128
N_CHUNK = IN_W // LANE
C_RQ, C_RK, C_RV, C_RG, C_DQ, C_DK, C_DV, C_GR, C_GD = 0, 4, 8, 16, 24, 32, 40, 48, 56
ROUTE_W = 128
SEQ_BLK = 1024
N_PBLK = N_PROMPT // SEQ_BLK
VMEM_LIMIT = 56 * 1024 * 1024


def _cparams(sem):
    return pltpu.CompilerParams(dimension_semantics=sem, vmem_limit_bytes=VMEM_LIMIT)


def _mod_row(tile_rows):
    def f(i):
        start = i * tile_rows
        return jnp.where(start < N_PROMPT, 0, 1 + (start - N_PROMPT) // T_S)
    return f


def _rms(x):
    return x * lax.rsqrt(jnp.mean(x * x, axis=-1, keepdims=True) + EPS)


def _mod_kernel(c_ref, w_ref, b_ref, o_ref):
    s = jax.nn.silu(c_ref[...])
    o_ref[...] = jnp.dot(s.astype(BF16), w_ref[...].astype(BF16),
                         preferred_element_type=F32) + b_ref[...]


def _modulation(cvec, w_ada, b_ada):
    tn = 1536
    return pl.pallas_call(
        _mod_kernel,
        out_shape=jax.ShapeDtypeStruct((8, 6 * D), F32),
        grid=(6 * D // tn,),
        in_specs=[pl.BlockSpec((8, D), lambda j: (0, 0)),
                  pl.BlockSpec((D, tn), lambda j: (0, j)),
                  pl.BlockSpec((1, tn), lambda j: (0, j))],
        out_specs=pl.BlockSpec((8, tn), lambda j: (0, j)),
        compiler_params=_cparams(("arbitrary",)),
        name="mod",
    )(cvec, w_ada, b_ada)


IP_TM = 512
IP_TN = 2048
IP_NPT = N_PROMPT // IP_TM


def _inproj_kernel(xp_ref, xs_ref, mod_ref, n1_ref, w_ref, proj_ref, k32_ref, v32_ref, h_scr):
    i = pl.program_id(0)
    j = pl.program_id(1)

    def norm(x_ref):
        mod = mod_ref[...]
        y = _rms(x_ref[...]) * n1_ref[...]
        h_scr[...] = (y * (1.0 + mod[:, D:2 * D]) + mod[:, 0:D]).astype(BF16)

    @pl.when(jnp.logical_and(j == 0, i < IP_NPT))
    def _():
        norm(xp_ref)

    @pl.when(jnp.logical_and(j == 0, i >= IP_NPT))
    def _():
        norm(xs_ref)

    acc = jnp.dot(h_scr[...], w_ref[...], preferred_element_type=F32)
    for c in range(IP_TN // LANE):
        proj_ref[c] = acc[:, c * LANE:(c + 1) * LANE].astype(BF16)

    @pl.when(jnp.logical_and(j == 2, i < IP_NPT))
    def _():
        k32_ref[...] = acc[:, :D]
        v32_ref[...] = acc[:, D:]


def _inproj(xp, xs, mod3, n1, w_in_bf):
    npt = IP_NPT
    cpt = IP_TN // LANE
    return pl.pallas_call(
        _inproj_kernel,
        out_shape=(jax.ShapeDtypeStruct((N_CHUNK, N_TOK, LANE), BF16),
                   jax.ShapeDtypeStruct((N_PROMPT, D), F32),
                   jax.ShapeDtypeStruct((N_PROMPT, D), F32)),
        grid=(N_TOK // IP_TM, IN_W // IP_TN),
        in_specs=[pl.BlockSpec((IP_TM, D), lambda i, j: (jnp.minimum(i, npt - 1), 0)),
                  pl.BlockSpec((IP_TM, D), lambda i, j: (jnp.maximum(i - npt, 0), 0)),
                  pl.BlockSpec((None, 1, 6 * D), lambda i, j: (_mod_row(IP_TM)(i), 0, 0)),
                  pl.BlockSpec((1, D), lambda i, j: (0, 0)),
                  pl.BlockSpec((D, IP_TN), lambda i, j: (0, j))],
        out_specs=(pl.BlockSpec((cpt, IP_TM, LANE), lambda i, j: (j, i, 0)),
                   pl.BlockSpec((IP_TM, D), lambda i, j: (jnp.minimum(i, npt - 1), 0)),
                   pl.BlockSpec((IP_TM, D), lambda i, j: (jnp.minimum(i, npt - 1), 0))),
        scratch_shapes=[pltpu.VMEM((IP_TM, D), BF16)],
        compiler_params=_cparams(("arbitrary", "arbitrary")),
        name="inproj",
    )(xp, xs, mod3, n1, w_in_bf)


def _decay_mask(t, lgf, lgb):
    ii = lax.broadcasted_iota(jnp.int32, (t, t), 0)
    jj = lax.broadcasted_iota(jnp.int32, (t, t), 1)
    rel = (ii - jj).astype(F32)
    e = jnp.exp(jnp.where(rel >= 0.0, lgf, -lgb) * rel)
    return jnp.where(rel == 0.0, 2.0, e) * (RET_KD ** -0.5)


def _ret_kernel(lg_ref, q_ref, k_ref, v_ref, rg_ref, s0f_ref, s0b_ref, g_ref,
                o_ref, sf_ref, sb_ref):
    i = pl.program_id(0)
    h = pl.program_id(1)
    lgf = lg_ref[0, h]
    lgb = lg_ref[1, h]
    gain = g_ref[...]
    nt = (((1,), (1,)), ((), ()))
    tn = (((0,), (0,)), ((), ()))

    def finish(o, rg):
        d = o - jnp.mean(o, axis=-1, keepdims=True)
        y = d * lax.rsqrt(jnp.mean(d * d, axis=-1, keepdims=True) + EPS) * gain
        return (jax.nn.silu(rg.astype(F32)) * y).astype(BF16)

    def vcat(r):
        return jnp.concatenate([v_ref[0, r, :], v_ref[1, r, :]], axis=1)

    def gcat(r):
        return jnp.concatenate([rg_ref[0, r, :], rg_ref[1, r, :]], axis=1)

    @pl.when(i < N_PBLK)
    def _():
        dm = _decay_mask(T_P, lgf, lgb)
        t = lax.broadcasted_iota(jnp.int32, (T_P, 1), 0).astype(F32)
        kdf = jnp.exp(lgf * (T_P - 1.0 - t)) * (RET_KD ** -0.5)
        kdb = jnp.exp(lgb * t) * (RET_KD ** -0.5)
        for s in range(SEQ_BLK // T_P):
            r = slice(s * T_P, (s + 1) * T_P)
            q = q_ref[r, :]
            k = k_ref[r, :]
            v = vcat(r)
            sc = lax.dot_general(q, k, nt, preferred_element_type=F32)
            o = jnp.dot((sc * dm).astype(BF16), v, preferred_element_type=F32)
            o_ref[r, :] = finish(o, gcat(r))
            kf = k.astype(F32)
            sf_ref[s] = lax.dot_general((kf * kdf).astype(BF16), v, tn, preferred_element_type=F32)
            sb_ref[s] = lax.dot_general((kf * kdb).astype(BF16), v, tn, preferred_element_type=F32)

    @pl.when(i >= N_PBLK)
    def _():
        r = slice(0, T_S)
        dm = _decay_mask(T_S, lgf, lgb)
        t = lax.broadcasted_iota(jnp.int32, (T_S, 1), 0).astype(F32)
        q = q_ref[...]
        k = k_ref[...]
        v = vcat(r)
        sc = lax.dot_general(q, k, nt, preferred_element_type=F32)
        o = jnp.dot((sc * dm).astype(BF16), v, preferred_element_type=F32)
        qf = q.astype(F32)
        o = o + jnp.dot((qf * jnp.exp(lgf * (t + 1.0))).astype(BF16), s0f_ref[...].astype(BF16),
                        preferred_element_type=F32)
        o = o + jnp.dot((qf * jnp.exp(lgb * (T_S - t))).astype(BF16), s0b_ref[...].astype(BF16),
                        preferred_element_type=F32)
        o_ref[...] = finish(o, gcat(r))


def _retention(lg, proj, s0f, s0b, gnorm):
    nb = N_TOK // SEQ_BLK
    spb = SEQ_BLK // T_P
    smp = lambda i: jnp.maximum(i - N_PBLK, 0)
    pmt = lambda i: jnp.minimum(i, N_PBLK - 1)
    pmh = lambda i, h: jnp.where(i < N_PBLK, h, RET_H - 1)
    return pl.pallas_call(
        _ret_kernel,
        out_shape=(jax.ShapeDtypeStruct((N_TOK, RET_H * RET_VD), BF16),
                   jax.ShapeDtypeStruct((16, RET_H, RET_KD, RET_VD), F32),
                   jax.ShapeDtypeStruct((16, RET_H, RET_KD, RET_VD), F32)),
        grid=(nb, RET_H),
        in_specs=[pl.BlockSpec(memory_space=pltpu.SMEM),
                  pl.BlockSpec((None, SEQ_BLK, LANE), lambda i, h: (C_RQ + h, i, 0)),
                  pl.BlockSpec((None, SEQ_BLK, LANE), lambda i, h: (C_RK + h, i, 0)),
                  pl.BlockSpec((2, SEQ_BLK, LANE), lambda i, h: (C_RV // 2 + h, i, 0)),
                  pl.BlockSpec((2, SEQ_BLK, LANE), lambda i, h: (C_RG // 2 + h, i, 0)),
                  pl.BlockSpec((None, None, RET_KD, RET_VD), lambda i, h: (smp(i), h, 0, 0)),
                  pl.BlockSpec((None, None, RET_KD, RET_VD), lambda i, h: (smp(i), h, 0, 0)),
                  pl.BlockSpec((1, RET_VD), lambda i, h: (0, h))],
        out_specs=(pl.BlockSpec((SEQ_BLK, RET_VD), lambda i, h: (i, h)),
                   pl.BlockSpec((spb, None, RET_KD, RET_VD), lambda i, h: (pmt(i), pmh(i, h), 0, 0)),
                   pl.BlockSpec((spb, None, RET_KD, RET_VD), lambda i, h: (pmt(i), pmh(i, h), 0, 0))),
        compiler_params=_cparams(("arbitrary", "arbitrary")),
        name="ret",
    )(lg, proj, proj, proj, proj, s0f, s0b, gnorm)


ATT_TQ = 256


def _attn_kernel(sc_ref, q_ref, k_ref, v_ref, ck_ref, cv_ref, cos_ref, sin_ref, g_ref,
                 o_ref, q_scr, k_scr, v_scr):
    i = pl.program_id(0)
    lam = sc_ref[0]
    out_scale = sc_ref[1]
    gain = g_ref[...] * out_scale
    lane = lax.broadcasted_iota(jnp.int32, (1, LANE), 1)
    first = lane < DIFF_HD
    nt = (((1,), (1,)), ((), ()))

    def attend(q, k, v):
        zero = jnp.zeros_like(q)
        s0 = lax.dot_general(jnp.where(first, q, zero), k, nt, preferred_element_type=F32)
        s1 = lax.dot_general(jnp.where(first, zero, q), k, nt, preferred_element_type=F32)
        e0 = jnp.exp(s0 - jnp.max(s0, axis=-1, keepdims=True))
        e1 = jnp.exp(s1 - jnp.max(s1, axis=-1, keepdims=True))
        r0 = 1.0 / jnp.sum(e0, axis=-1, keepdims=True)
        r1 = lam / jnp.sum(e1, axis=-1, keepdims=True)
        a = (e0 * r0 - e1 * r1).astype(BF16)
        o = jnp.dot(a, v, preferred_element_type=F32)
        return _rms(o) * gain

    @pl.when(i < N_PBLK)
    def _():
        for s in range(SEQ_BLK // T_P):
            r = slice(s * T_P, (s + 1) * T_P)
            q = (q_ref[r, :].astype(F32) * (DIFF_HD ** -0.5)).astype(BF16)
            o_ref[r, :] = attend(q, k_ref[r, :], v_ref[r, :]).astype(BF16)

    @pl.when(i >= N_PBLK)
    def _():
        cos = cos_ref[...]
        sin = sin_ref[...]
        low = (lax.broadcasted_iota(jnp.int32, (T_S, LANE), 1) & 16) == 0

        def rope(x):
            xs = jnp.where(low, pltpu.roll(x, LANE - 16, 1), pltpu.roll(x, 16, 1))
            return x * cos + xs * sin

        q_scr[...] = (rope(q_ref[...].astype(F32)) * (DIFF_HD ** -0.5)).astype(BF16)
        k_scr[0:T_S, :] = rope(k_ref[...].astype(F32)).astype(BF16)
        k_scr[T_S:T_S + PAST, :] = ck_ref[...].astype(BF16)
        v_scr[0:T_S, :] = v_ref[...]
        v_scr[T_S:T_S + PAST, :] = cv_ref[...].astype(BF16)

        def body(b, carry):
            rows = pl.ds(pl.multiple_of(b * ATT_TQ, ATT_TQ), ATT_TQ)
            o_ref[rows, :] = attend(q_scr[rows, :], k_scr[...], v_scr[...]).astype(BF16)
            return carry

        lax.fori_loop(0, T_S // ATT_TQ, body, 0)


def _attention(scal, proj, cache_k, cache_v, cos_t, sin_t, subln_g):
    nb = N_TOK // SEQ_BLK
    smp = lambda i: jnp.maximum(i - N_PBLK, 0)
    return pl.pallas_call(
        _attn_kernel,
        out_shape=jax.ShapeDtypeStruct((N_TOK, DIFF_H * 2 * DIFF_HD), BF16),
        grid=(nb, DIFF_H),
        in_specs=[pl.BlockSpec(memory_space=pltpu.SMEM),
                  pl.BlockSpec((None, SEQ_BLK, LANE), lambda i, h: (C_DQ + h, i, 0)),
                  pl.BlockSpec((None, SEQ_BLK, LANE), lambda i, h: (C_DK + h, i, 0)),
                  pl.BlockSpec((None, SEQ_BLK, LANE), lambda i, h: (C_DV + h, i, 0)),
                  pl.BlockSpec((None, PAST, LANE), lambda i, h: (smp(i), 0, h)),
                  pl.BlockSpec((None, PAST, LANE), lambda i, h: (smp(i), 0, h)),
                  pl.BlockSpec((T_S, LANE), lambda i, h: (0, 0)),
                  pl.BlockSpec((T_S, LANE), lambda i, h: (0, 0)),
                  pl.BlockSpec((1, LANE), lambda i, h: (0, 0))],
        out_specs=pl.BlockSpec((SEQ_BLK, LANE), lambda i, h: (i, h)),
        scratch_shapes=[pltpu.VMEM((T_S, LANE), BF16),
                        pltpu.VMEM((T_S + PAST, LANE), BF16),
                        pltpu.VMEM((T_S + PAST, LANE), BF16)],
        compiler_params=_cparams(("arbitrary", "arbitrary")),
        name="attn",
    )(scal, proj, proj, proj, cache_k, cache_v, cos_t, sin_t, subln_g)


def _rope_tables():
    n_rows = T_S // GRID_W
    row = jnp.repeat(jnp.arange(n_rows), GRID_W).astype(F32)
    col = jnp.tile(jnp.arange(GRID_W), n_rows).astype(F32)
    n_freq = DIFF_HD // 4
    inv = ROPE_BASE ** (-jnp.arange(n_freq, dtype=F32) / n_freq)

    def axis_tables(pos):
        ang = pos[:, None] * inv[None, :]
        c = jnp.cos(ang)
        s = jnp.sin(ang)
        return jnp.concatenate([c, c], axis=-1), jnp.concatenate([-s, s], axis=-1)

    cr, sr = axis_tables(row)
    cc, sc = axis_tables(col)
    cos_h = jnp.concatenate([cr, cc], axis=-1)
    sin_h = jnp.concatenate([sr, sc], axis=-1)
    return jnp.concatenate([cos_h, cos_h], axis=-1), jnp.concatenate([sin_h, sin_h], axis=-1)


OP_TM = 512
OP_NPT = N_PROMPT // OP_TM


def _outproj_kernel(ra_ref, da_ref, gr_ref, gd_ref, xp_ref, xs_ref, mod_ref, n2_ref,
                    wro_ref, wdo_ref, wo_ref, wr_ref, br_ref,
                    x1_ref, h2_ref, lg_ref, m_scr):
    i = pl.program_id(0)
    ret_out = jnp.dot(ra_ref[...], wro_ref[...], preferred_element_type=F32)
    diff_out = jnp.dot(da_ref[...], wdo_ref[...], preferred_element_type=F32)
    for c in range(D // LANE):
        sl = slice(c * LANE, (c + 1) * LANE)
        m = (jax.nn.sigmoid(gr_ref[c].astype(F32)) * ret_out[:, sl]
             + jax.nn.sigmoid(gd_ref[c].astype(F32)) * diff_out[:, sl])
        m_scr[:, sl] = m.astype(BF16)
    mix = jnp.dot(m_scr[...], wo_ref[...], preferred_element_type=F32)
    mod = mod_ref[...]

    def fin(x_ref):
        x1 = x_ref[...] + mod[:, 2 * D:3 * D] * mix
        x1_ref[...] = x1
        h2 = (_rms(x1) * n2_ref[...] * (1.0 + mod[:, 4 * D:5 * D]) + mod[:, 3 * D:4 * D]).astype(BF16)
        h2_ref[...] = h2
        lg_ref[...] = jnp.dot(h2, wr_ref[...], preferred_element_type=F32) + br_ref[...]

    @pl.when(i < OP_NPT)
    def _():
        fin(xp_ref)

    @pl.when(i >= OP_NPT)
    def _():
        fin(xs_ref)


def _outproj(ret_act, diff_act, proj, xp, xs, mod3, n2, wro, wdo, wo, wr, br):
    npt = OP_NPT
    full = lambda i: (0, 0)
    return pl.pallas_call(
        _outproj_kernel,
        out_shape=(jax.ShapeDtypeStruct((N_TOK, D), F32),
                   jax.ShapeDtypeStruct((N_TOK, D), BF16),
                   jax.ShapeDtypeStruct((N_TOK, ROUTE_W), F32)),
        grid=(N_TOK // OP_TM,),
        in_specs=[pl.BlockSpec((OP_TM, D), lambda i: (i, 0)),
                  pl.BlockSpec((OP_TM, D), lambda i: (i, 0)),
                  pl.BlockSpec((8, OP_TM, LANE), lambda i: (C_GR // 8, i, 0)),
                  pl.BlockSpec((8, OP_TM, LANE), lambda i: (C_GD // 8, i, 0)),
                  pl.BlockSpec((OP_TM, D), lambda i: (jnp.minimum(i, npt - 1), 0)),
                  pl.BlockSpec((OP_TM, D), lambda i: (jnp.maximum(i - npt, 0), 0)),
                  pl.BlockSpec((None, 1, 6 * D), lambda i: (_mod_row(OP_TM)(i), 0, 0)),
                  pl.BlockSpec((1, D), full),
                  pl.BlockSpec((D, D), full),
                  pl.BlockSpec((D, D), full),
                  pl.BlockSpec((D, D), full),
                  pl.BlockSpec((D, ROUTE_W), full),
                  pl.BlockSpec((1, ROUTE_W), full)],
        out_specs=(pl.BlockSpec((OP_TM, D), lambda i: (i, 0)),
                   pl.BlockSpec((OP_TM, D), lambda i: (i, 0)),
                   pl.BlockSpec((OP_TM, ROUTE_W), lambda i: (i, 0))),
        scratch_shapes=[pltpu.VMEM((OP_TM, D), BF16)],
        compiler_params=_cparams(("arbitrary",)),
        name="outproj",
    )(ret_act, diff_act, proj, proj, xp, xs, mod3, n2, wro, wdo, wo, wr, br)


RT_TM = 512
RT_NT = N_TOK // RT_TM
PIECE = 16
R_LOC = RT_TM + N_GROUPS * PIECE
R_STAGE = 640
EX_TM = 512
R_MAX = 11264
EX_NT = R_MAX // EX_TM


def _route_kernel(lg_ref, cw8_ref, info_ref, pc_ref):
    lg = lg_ref[...]
    lane = lax.broadcasted_iota(jnp.int32, lg.shape, 1)
    neg = jnp.float32(-jnp.inf)
    big = jnp.int32(ROUTE_W)

    def first_where(cond):
        return jnp.min(jnp.where(cond, lane, big), axis=-1, keepdims=True)

    is_g = lane < N_GROUPS
    gl = jnp.where(is_g, lg, neg)
    gmax = jnp.max(gl, axis=-1, keepdims=True)
    gsum = jnp.sum(jnp.where(is_g, jnp.exp(lg - gmax), 0.0), axis=-1, keepdims=True)
    p_top = 1.0 / gsum
    g_idx = first_where(gl == gmax)
    in_grp = jnp.logical_and(lane >= N_GROUPS, ((lane - N_GROUPS) >> 3) == g_idx)
    el = jnp.where(in_grp, lg, neg)
    emax = jnp.max(el, axis=-1, keepdims=True)
    ee = jnp.where(in_grp, jnp.exp(lg - emax), 0.0)
    ep = ee / jnp.sum(ee, axis=-1, keepdims=True)
    ep = jnp.where(in_grp, ep, -1.0)
    e1 = jnp.max(ep, axis=-1, keepdims=True)
    i1 = first_where(ep == e1)
    ep2 = jnp.where(lane == i1, -1.0, ep)
    e2 = jnp.max(ep2, axis=-1, keepdims=True)
    i2 = first_where(ep2 == e2)
    den = e1 + e2
    cw = (jnp.where(lane == i1, p_top * e1 / den, 0.0)
          + jnp.where(lane == i2, p_top * e2 / den, 0.0))
    cw8 = jnp.zeros_like(cw)
    for g in range(N_GROUPS):
        cw8 = cw8 + jnp.where(g_idx == g, pltpu.roll(cw, LANE - (N_GROUPS + EPG * g), 1), 0.0)
    cw8_ref[...] = cw8

    onehot = (lane == g_idx).astype(F32)
    ii = lax.broadcasted_iota(jnp.int32, (RT_TM, RT_TM), 0)
    jj = lax.broadcasted_iota(jnp.int32, (RT_TM, RT_TM), 1)
    earlier = (jj < ii).astype(BF16)
    prefix = jnp.dot(earlier, onehot.astype(BF16), preferred_element_type=F32)
    cnt = jnp.sum(onehot, axis=0, keepdims=True)
    pc = jnp.floor((cnt + (PIECE - 1.0)) * (1.0 / PIECE)) * PIECE
    lo = pltpu.roll(pc, 1, 1) + pltpu.roll(pc, 2, 1) + pltpu.roll(pc, 3, 1)
    dest = jnp.sum(onehot * (prefix + lo), axis=-1, keepdims=True)
    info_ref[...] = jnp.where(lane == 0, g_idx.astype(F32), jnp.where(lane == 1, dest, 0.0))
    pc_ref[...] = pc


def _route(logits):
    return pl.pallas_call(
        _route_kernel,
        out_shape=(jax.ShapeDtypeStruct((N_TOK, ROUTE_W), F32),
                   jax.ShapeDtypeStruct((N_TOK, ROUTE_W), F32),
                   jax.ShapeDtypeStruct((RT_NT, 1, ROUTE_W), F32)),
        grid=(RT_NT,),
        in_specs=[pl.BlockSpec((RT_TM, ROUTE_W), lambda i: (i, 0))],
        out_specs=(pl.BlockSpec((RT_TM, ROUTE_W), lambda i: (i, 0)),
                   pl.BlockSpec((RT_TM, ROUTE_W), lambda i: (i, 0)),
                   pl.BlockSpec((None, 1, ROUTE_W), lambda i: (i, 0, 0))),
        compiler_params=_cparams(("arbitrary",)),
        name="route",
    )(logits)


def _dispatch_plan(pc_arr):
    pc = pc_arr[:, 0, :N_GROUPS].astype(jnp.int32)
    seg_len = jnp.sum(pc, axis=0)
    seg_pad = (seg_len + (EX_TM - 1)) // EX_TM * EX_TM
    seg_end_pad = jnp.cumsum(seg_pad)
    seg_start = seg_end_pad - seg_pad
    chunk_off = seg_start[None, :] + jnp.cumsum(pc, axis=0) - pc
    tile_start = jnp.arange(EX_NT, dtype=jnp.int32) * EX_TM
    tile_group = jnp.minimum(jnp.sum((tile_start[:, None] >= seg_end_pad[None, :]).astype(jnp.int32), axis=1),
                             N_GROUPS - 1)
    tile_valid = (tile_start < (seg_start + seg_len)[tile_group]).astype(jnp.int32)
    return chunk_off.astype(jnp.int32), pc, tile_group.astype(jnp.int32), tile_valid


def _piece_copies(off_ref, pc_ref, i, make):
    lo = 0
    for g in range(N_GROUPS):
        n = pc_ref[i, g] // PIECE
        base = off_ref[i, g]

        def body(j, carry, lo=lo, base=base):
            make(pl.multiple_of(lo + j * PIECE, PIECE), pl.multiple_of(base + j * PIECE, PIECE))
            return carry

        lax.fori_loop(0, n, body, 0)
        lo = lo + pc_ref[i, g]
    return lo // PIECE


def _dispatch_kernel(off_ref, pc_ref, h_ref, info_ref, cw8_ref, xs_in, cws_in,
                     xs_out, cws_out, x_scr, c_scr, sem):
    del xs_in, cws_in
    i = pl.program_id(0)
    dest = info_ref[...].T[1:2, :]
    row = lax.broadcasted_iota(jnp.int32, (R_LOC, RT_TM), 0).astype(F32)
    sel = row == dest
    x_scr[...] = jnp.dot(sel.astype(BF16), h_ref[...], preferred_element_type=F32).astype(BF16)
    c_scr[...] = jnp.dot(sel.astype(F32), cw8_ref[...], preferred_element_type=F32,
                         precision=lax.Precision.HIGHEST)

    def x_copy(src, dst):
        return pltpu.make_async_copy(x_scr.at[pl.ds(src, PIECE)], xs_out.at[pl.ds(dst, PIECE)], sem.at[0])

    def c_copy(src, dst):
        return pltpu.make_async_copy(c_scr.at[pl.ds(src, PIECE)], cws_out.at[pl.ds(dst, PIECE)], sem.at[1])

    def start(src, dst):
        x_copy(src, dst).start()
        c_copy(src, dst).start()

    n_pieces = _piece_copies(off_ref, pc_ref, i, start)

    def wait(j, carry):
        x_copy(0, 0).wait()
        c_copy(0, 0).wait()
        return carry

    lax.fori_loop(0, n_pieces, wait, 0)


def _dispatch(chunk_off, pc, h2, info, cw8):
    xs0 = jnp.zeros((R_MAX, D), BF16)
    cws0 = jnp.zeros((R_MAX, ROUTE_W), F32)
    grid_spec = pltpu.PrefetchScalarGridSpec(
        num_scalar_prefetch=2,
        grid=(RT_NT,),
        in_specs=[pl.BlockSpec((RT_TM, D), lambda i, o, p: (i, 0)),
                  pl.BlockSpec((RT_TM, ROUTE_W), lambda i, o, p: (i, 0)),
                  pl.BlockSpec((RT_TM, ROUTE_W), lambda i, o, p: (i, 0)),
                  pl.BlockSpec(memory_space=pl.ANY),
                  pl.BlockSpec(memory_space=pl.ANY)],
        out_specs=(pl.BlockSpec(memory_space=pl.ANY), pl.BlockSpec(memory_space=pl.ANY)),
        scratch_shapes=[pltpu.VMEM((R_LOC, D), BF16),
                        pltpu.VMEM((R_LOC, ROUTE_W), F32),
                        pltpu.SemaphoreType.DMA((2,))])
    return pl.pallas_call(
        _dispatch_kernel,
        out_shape=(jax.ShapeDtypeStruct((R_MAX, D), BF16),
                   jax.ShapeDtypeStruct((R_MAX, ROUTE_W), F32)),
        grid_spec=grid_spec,
        input_output_aliases={5: 0, 6: 1},
        compiler_params=_cparams(("arbitrary",)),
        name="dispatch",
    )(chunk_off, pc, h2, info, cw8, xs0, cws0)


def _expert_kernel(tg_ref, tv_ref, x_ref, cw_ref, wg_ref, wu_ref, wd_ref, y_ref):
    del tg_ref
    k = pl.program_id(0)

    @pl.when(tv_ref[k] == 0)
    def _():
        y_ref[...] = jnp.zeros_like(y_ref)

    @pl.when(tv_ref[k] != 0)
    def _():
        x = x_ref[...]
        cw = cw_ref[...]
        lane = lax.broadcasted_iota(jnp.int32, cw.shape, 1)
        acc = jnp.zeros(y_ref.shape, F32)
        for j in range(EPG):
            w = jnp.sum(jnp.where(lane == j, cw, 0.0), axis=-1, keepdims=True)
            a = (jax.nn.silu(jnp.dot(x, wg_ref[j], preferred_element_type=F32))
                 * jnp.dot(x, wu_ref[j], preferred_element_type=F32))
            acc = acc + jnp.dot((a * w).astype(BF16), wd_ref[j], preferred_element_type=F32)
        y_ref[...] = acc.astype(BF16)


def _experts(tile_group, tile_valid, xs, cws, wg, wu, wd):
    grid_spec = pltpu.PrefetchScalarGridSpec(
        num_scalar_prefetch=2,
        grid=(EX_NT,),
        in_specs=[pl.BlockSpec((EX_TM, D), lambda k, tg, tv: (k, 0)),
                  pl.BlockSpec((EX_TM, ROUTE_W), lambda k, tg, tv: (k, 0)),
                  pl.BlockSpec((EPG, D, FF), lambda k, tg, tv: (tg[k], 0, 0)),
                  pl.BlockSpec((EPG, D, FF), lambda k, tg, tv: (tg[k], 0, 0)),
                  pl.BlockSpec((EPG, FF, D), lambda k, tg, tv: (tg[k], 0, 0))],
        out_specs=pl.BlockSpec((EX_TM, D), lambda k, tg, tv: (k, 0)))
    return pl.pallas_call(
        _expert_kernel,
        out_shape=jax.ShapeDtypeStruct((R_MAX, D), BF16),
        grid_spec=grid_spec,
        compiler_params=_cparams(("arbitrary",)),
        name="experts",
    )(tile_group, tile_valid, xs, cws, wg, wu, wd)


CB_NPT = N_PROMPT // RT_TM


def _combine_kernel(off_ref, pc_ref, info_ref, x1_ref, mod_ref, g_ref, ys_hbm,
                    yp_ref, ysm_ref, stage, sem):
    i = pl.program_id(0)

    @pl.when(i == 0)
    def _():
        stage[...] = jnp.zeros_like(stage)

    def copy(dst, src):
        return pltpu.make_async_copy(ys_hbm.at[pl.ds(src, PIECE)], stage.at[pl.ds(dst, PIECE)], sem.at[0])

    n_pieces = _piece_copies(off_ref, pc_ref, i, lambda loc, glob: copy(loc, glob).start())

    def wait(j, carry):
        copy(0, 0).wait()
        return carry

    lax.fori_loop(0, n_pieces, wait, 0)

    dest = info_ref[...][:, 1:2]
    col = lax.broadcasted_iota(jnp.int32, (RT_TM, R_STAGE), 1).astype(F32)
    moe = jnp.dot((col == dest).astype(BF16), stage[...], preferred_element_type=F32)
    mod = mod_ref[...]
    out = _rms(x1_ref[...] + mod[:, 5 * D:6 * D] * moe) * g_ref[...]

    @pl.when(i < CB_NPT)
    def _():
        yp_ref[...] = out

    @pl.when(i >= CB_NPT)
    def _():
        ysm_ref[...] = out


def _combine(chunk_off, pc, info, x1, mod3, fg, ys):
    npt = CB_NPT
    grid_spec = pltpu.PrefetchScalarGridSpec(
        num_scalar_prefetch=2,
        grid=(RT_NT,),
        in_specs=[pl.BlockSpec((RT_TM, ROUTE_W), lambda i, o, p: (i, 0)),
                  pl.BlockSpec((RT_TM, D), lambda i, o, p: (i, 0)),
                  pl.BlockSpec((None, 1, 6 * D), lambda i, o, p: (_mod_row(RT_TM)(i), 0, 0)),
                  pl.BlockSpec((1, D), lambda i, o, p: (0, 0)),
                  pl.BlockSpec(memory_space=pl.ANY)],
        out_specs=(pl.BlockSpec((RT_TM, D), lambda i, o, p: (jnp.minimum(i, npt - 1), 0)),
                   pl.BlockSpec((RT_TM, D), lambda i, o, p: (jnp.maximum(i - npt, 0), 0))),
        scratch_shapes=[pltpu.VMEM((R_STAGE, D), BF16),
                        pltpu.SemaphoreType.DMA((1,))])
    return pl.pallas_call(
        _combine_kernel,
        out_shape=(jax.ShapeDtypeStruct((N_PROMPT, D), F32),
                   jax.ShapeDtypeStruct((N_SAMPLE, D), F32)),
        grid_spec=grid_spec,
        compiler_params=_cparams(("arbitrary",)),
        name="combine",
    )(chunk_off, pc, info, x1, mod3, fg, ys)


def kernel(x_prompt, x_sample, cache_diff_k, cache_diff_v, state_ret_fwd, state_ret_bwd, c, c_ctx,
           w_ada, b_ada, norm1_g, norm2_g, w_in, ret_decay_fwd, ret_decay_bwd, ret_norm_g,
           diff_lambda_q1, diff_lambda_k1, diff_lambda_q2, diff_lambda_k2, diff_subln_g,
           w_ret_o, w_diff_o, w_o, router_group_w, router_group_b, router_expert_w, router_expert_b,
           moe_w_gate, moe_w_up, moe_w_down, final_norm_g):
    l = 0
    lam_init = 0.8 - 0.6 * math.exp(-0.3 * l)
    lam = (jnp.exp(jnp.sum(diff_lambda_q1[l].astype(F32) * diff_lambda_k1[l].astype(F32)))
           - jnp.exp(jnp.sum(diff_lambda_q2[l].astype(F32) * diff_lambda_k2[l].astype(F32))) + lam_init)
    attn_scal = jnp.stack([lam, jnp.float32(1.0 - lam_init)]).astype(F32)
    lg = jnp.stack([jax.nn.log_sigmoid(ret_decay_fwd[l].astype(F32)),
                    jax.nn.log_sigmoid(ret_decay_bwd[l].astype(F32))])

    xp = x_prompt.reshape(N_PROMPT, D)
    xs = x_sample.reshape(N_SAMPLE, D)
    cvec = jnp.concatenate([c_ctx[None, :], c, jnp.zeros((3, D), F32)], axis=0)
    mod3 = _modulation(cvec, w_ada[l], b_ada[l][None, :]).reshape(8, 1, 6 * D)

    proj, k32, v32 = _inproj(xp, xs, mod3, norm1_g[l][None, :], w_in[l].astype(BF16))

    ret_act, s_f, s_b = _retention(lg, proj, state_ret_fwd[:, l], state_ret_bwd[:, l],
                                   ret_norm_g[l][None, :])
    cos_t, sin_t = _rope_tables()
    diff_act = _attention(attn_scal, proj,
                          cache_diff_k[:, l].reshape(4, PAST, D), cache_diff_v[:, l].reshape(4, PAST, D),
                          cos_t, sin_t, diff_subln_g[l][None, :])

    wr = jnp.concatenate([router_group_w[l], router_expert_w[l],
                          jnp.zeros((D, ROUTE_W - N_GROUPS - N_EXP), F32)], axis=1).astype(BF16)
    br = jnp.concatenate([router_group_b[l], router_expert_b[l],
                          jnp.zeros((ROUTE_W - N_GROUPS - N_EXP,), F32)])[None, :]
    x1, h2, logits = _outproj(ret_act, diff_act, proj, xp, xs, mod3, norm2_g[l][None, :],
                              w_ret_o[l].astype(BF16), w_diff_o[l].astype(BF16), w_o[l].astype(BF16),
                              wr, br)
    cw8, info, pc_arr = _route(logits)
    chunk_off, pc, tile_group, tile_valid = _dispatch_plan(pc_arr)
    xs_sorted, cw_sorted = _dispatch(chunk_off, pc, h2, info, cw8)
    y_sorted = _experts(tile_group, tile_valid, xs_sorted, cw_sorted,
                        moe_w_gate[l].astype(BF16), moe_w_up[l].astype(BF16), moe_w_down[l].astype(BF16))
    yp, ys = _combine(chunk_off, pc, info, x1, mod3, final_norm_g[None, :], y_sorted)

    return (yp.reshape(16, T_P, D), ys.reshape(4, T_S, D),
            k32.reshape(16, 1, T_P, DIFF_H, 2, DIFF_HD), v32.reshape(16, 1, T_P, DIFF_H, 2 * DIFF_HD),
            s_f.reshape(16, 1, RET_H, RET_KD, RET_VD), s_b.reshape(16, 1, RET_H, RET_KD, RET_VD))
```

```python
import functools
import math

import jax
import jax.numpy as jnp
from jax import lax
from jax.experimental import pallas as pl
from jax.experimental.pallas import tpu as pltpu

F32 = jnp.float32
BF16 = jnp.bfloat16

D = 1024
N_PROMPT = 16 * 256
N_SAMPLE = 4 * 1024
N_TOK = N_PROMPT + N_SAMPLE
T_P = 256
T_S = 1024
PAST = 512
GRID_W = 64
RET_H = 4
RET_KD = 128
RET_VD = 256
DIFF_H = 8
DIFF_HD = 64
ROPE_BASE = 10000.0
N_GROUPS = 4
EPG = 8
N_EXP = 32
FF = 256
EPS = 1e-6
IN_W = 8192
LANE = 128
N_CHUNK = IN_W // LANE
C_RQ, C_RK, C_RV, C_RG, C_DQ, C_DK, C_DV, C_GR, C_GD = 0, 4, 8, 16, 24, 32, 40, 48, 56
ROUTE_W = 128
SEQ_BLK = 1024
N_PBLK = N_PROMPT // SEQ_BLK
VMEM_LIMIT = 56 * 1024 * 1024


def _cparams(sem):
    return pltpu.CompilerParams(dimension_semantics=sem, vmem_limit_bytes=VMEM_LIMIT)


def _mod_row(tile_rows):
    def f(i):
        start = i * tile_rows
        return jnp.where(start < N_PROMPT, 0, 1 + (start - N_PROMPT) // T_S)
    return f


def _rms(x):
    return x * lax.rsqrt(jnp.mean(x * x, axis=-1, keepdims=True) + EPS)


def _mod_kernel(c_ref, w_ref, b_ref, o_ref):
    s = jax.nn.silu(c_ref[...])
    o_ref[...] = jnp.dot(s.astype(BF16), w_ref[...].astype(BF16),
                         preferred_element_type=F32) + b_ref[...]


def _modulation(cvec, w_ada, b_ada):
    tn = 1536
    return pl.pallas_call(
        _mod_kernel,
        out_shape=jax.ShapeDtypeStruct((8, 6 * D), F32),
        grid=(6 * D // tn,),
        in_specs=[pl.BlockSpec((8, D), lambda j: (0, 0)),
                  pl.BlockSpec((D, tn), lambda j: (0, j)),
                  pl.BlockSpec((1, tn), lambda j: (0, j))],
        out_specs=pl.BlockSpec((8, tn), lambda j: (0, j)),
        compiler_params=_cparams(("arbitrary",)),
        name="mod",
    )(cvec, w_ada, b_ada)


IP_TM = 512
IP_TN = 2048
IP_NPT = N_PROMPT // IP_TM


def _inproj_kernel(xp_ref, xs_ref, mod_ref, n1_ref, w_ref, proj_ref, k32_ref, v32_ref, h_scr):
    i = pl.program_id(0)
    j = pl.program_id(1)

    def norm(x_ref):
        mod = mod_ref[...]
        y = _rms(x_ref[...]) * n1_ref[...]
        h_scr[...] = (y * (1.0 + mod[:, D:2 * D]) + mod[:, 0:D]).astype(BF16)

    @pl.when(jnp.logical_and(j == 0, i < IP_NPT))
    def _():
        norm(xp_ref)

    @pl.when(jnp.logical_and(j == 0, i >= IP_NPT))
    def _():
        norm(xs_ref)

    acc = jnp.dot(h_scr[...], w_ref[...], preferred_element_type=F32)
    for c in range(IP_TN // LANE):
        proj_ref[c] = acc[:, c * LANE:(c + 1) * LANE].astype(BF16)

    @pl.when(jnp.logical_and(j == 2, i < IP_NPT))
    def _():
        k32_ref[...] = acc[:, :D]
        v32_ref[...] = acc[:, D:]


def _inproj(xp, xs, mod3, n1, w_in_bf):
    npt = IP_NPT
    cpt = IP_TN // LANE
    return pl.pallas_call(
        _inproj_kernel,
        out_shape=(jax.ShapeDtypeStruct((N_CHUNK, N_TOK, LANE), BF16),
                   jax.ShapeDtypeStruct((N_PROMPT, D), F32),
                   jax.ShapeDtypeStruct((N_PROMPT, D), F32)),
        grid=(N_TOK // IP_TM, IN_W // IP_TN),
        in_specs=[pl.BlockSpec((IP_TM, D), lambda i, j: (jnp.minimum(i, npt - 1), 0)),
                  pl.BlockSpec((IP_TM, D), lambda i, j: (jnp.maximum(i - npt, 0), 0)),
                  pl.BlockSpec((None, 1, 6 * D), lambda i, j: (_mod_row(IP_TM)(i), 0, 0)),
                  pl.BlockSpec((1, D), lambda i, j: (0, 0)),
                  pl.BlockSpec((D, IP_TN), lambda i, j: (0, j))],
        out_specs=(pl.BlockSpec((cpt, IP_TM, LANE), lambda i, j: (j, i, 0)),
                   pl.BlockSpec((IP_TM, D), lambda i, j: (jnp.minimum(i, npt - 1), 0)),
                   pl.BlockSpec((IP_TM, D), lambda i, j: (jnp.minimum(i, npt - 1), 0))),
        scratch_shapes=[pltpu.VMEM((IP_TM, D), BF16)],
        compiler_params=_cparams(("arbitrary", "arbitrary")),
        name="inproj",
    )(xp, xs, mod3, n1, w_in_bf)


def _decay_mask(t, lgf, lgb):
    ii = lax.broadcasted_iota(jnp.int32, (t, t), 0)
    jj = lax.broadcasted_iota(jnp.int32, (t, t), 1)
    rel = (ii - jj).astype(F32)
    e = jnp.exp(jnp.where(rel >= 0.0, lgf, -lgb) * rel)
    return jnp.where(rel == 0.0, 2.0, e) * (RET_KD ** -0.5)


def _ret_kernel(lg_ref, q_ref, k_ref, v_ref, rg_ref, s0f_ref, s0b_ref, g_ref,
                o_ref, sf_ref, sb_ref):
    i = pl.program_id(0)
    h = pl.program_id(1)
    lgf = lg_ref[0, h]
    lgb = lg_ref[1, h]
    gain = g_ref[...]
    nt = (((1,), (1,)), ((), ()))
    tn = (((0,), (0,)), ((), ()))

    def finish(o, rg):
        d = o - jnp.mean(o, axis=-1, keepdims=True)
        y = d * lax.rsqrt(jnp.mean(d * d, axis=-1, keepdims=True) + EPS) * gain
        return (jax.nn.silu(rg.astype(F32)) * y).astype(BF16)

    def vcat(r):
        return jnp.concatenate([v_ref[0, r, :], v_ref[1, r, :]], axis=1)

    def gcat(r):
        return jnp.concatenate([rg_ref[0, r, :], rg_ref[1, r, :]], axis=1)

    @pl.when(i < N_PBLK)
    def _():
        dm = _decay_mask(T_P, lgf, lgb)
        t = lax.broadcasted_iota(jnp.int32, (T_P, 1), 0).astype(F32)
        kdf = jnp.exp(lgf * (T_P - 1.0 - t)) * (RET_KD ** -0.5)
        kdb = jnp.exp(lgb * t) * (RET_KD ** -0.5)
        for s in range(SEQ_BLK // T_P):
            r = slice(s * T_P, (s + 1) * T_P)
            q = q_ref[r, :]
            k = k_ref[r, :]
            v = vcat(r)
            sc = lax.dot_general(q, k, nt, preferred_element_type=F32)
            o = jnp.dot((sc * dm).astype(BF16), v, preferred_element_type=F32)
            o_ref[r, :] = finish(o, gcat(r))
            kf = k.astype(F32)
            sf_ref[s] = lax.dot_general((kf * kdf).astype(BF16), v, tn, preferred_element_type=F32)
            sb_ref[s] = lax.dot_general((kf * kdb).astype(BF16), v, tn, preferred_element_type=F32)

    @pl.when(i >= N_PBLK)
    def _():
        r = slice(0, T_S)
        dm = _decay_mask(T_S, lgf, lgb)
        t = lax.broadcasted_iota(jnp.int32, (T_S, 1), 0).astype(F32)
        q = q_ref[...]
        k = k_ref[...]
        v = vcat(r)
        sc = lax.dot_general(q, k, nt, preferred_element_type=F32)
        o = jnp.dot((sc * dm).astype(BF16), v, preferred_element_type=F32)
        qf = q.astype(F32)
        o = o + jnp.dot((qf * jnp.exp(lgf * (t + 1.0))).astype(BF16), s0f_ref[...].astype(BF16),
                        preferred_element_type=F32)
        o = o + jnp.dot((qf * jnp.exp(lgb * (T_S - t))).astype(BF16), s0b_ref[...].astype(BF16),
                        preferred_element_type=F32)
        o_ref[...] = finish(o, gcat(r))


def _retention(lg, proj, s0f, s0b, gnorm):
    nb = N_TOK // SEQ_BLK
    spb = SEQ_BLK // T_P
    smp = lambda i: jnp.maximum(i - N_PBLK, 0)
    pmt = lambda i: jnp.minimum(i, N_PBLK - 1)
    pmh = lambda i, h: jnp.where(i < N_PBLK, h, RET_H - 1)
    return pl.pallas_call(
        _ret_kernel,
        out_shape=(jax.ShapeDtypeStruct((N_TOK, RET_H * RET_VD), BF16),
                   jax.ShapeDtypeStruct((16, RET_H, RET_KD, RET_VD), F32),
                   jax.ShapeDtypeStruct((16, RET_H, RET_KD, RET_VD), F32)),
        grid=(nb, RET_H),
        in_specs=[pl.BlockSpec(memory_space=pltpu.SMEM),
                  pl.BlockSpec((None, SEQ_BLK, LANE), lambda i, h: (C_RQ + h, i, 0)),
                  pl.BlockSpec((None, SEQ_BLK, LANE), lambda i, h: (C_RK + h, i, 0)),
                  pl.BlockSpec((2, SEQ_BLK, LANE), lambda i, h: (C_RV // 2 + h, i, 0)),
                  pl.BlockSpec((2, SEQ_BLK, LANE), lambda i, h: (C_RG // 2 + h, i, 0)),
                  pl.BlockSpec((None, None, RET_KD, RET_VD), lambda i, h: (smp(i), h, 0, 0)),
                  pl.BlockSpec((None, None, RET_KD, RET_VD), lambda i, h: (smp(i), h, 0, 0)),
                  pl.BlockSpec((1, RET_VD), lambda i, h: (0, h))],
        out_specs=(pl.BlockSpec((SEQ_BLK, RET_VD), lambda i, h: (i, h)),
                   pl.BlockSpec((spb, None, RET_KD, RET_VD), lambda i, h: (pmt(i), pmh(i, h), 0, 0)),
                   pl.BlockSpec((spb, None, RET_KD, RET_VD), lambda i, h: (pmt(i), pmh(i, h), 0, 0))),
        compiler_params=_cparams(("arbitrary", "arbitrary")),
        name="ret",
    )(lg, proj, proj, proj, proj, s0f, s0b, gnorm)


ATT_TQ = 256


def _attn_kernel(sc_ref, q_ref, k_ref, v_ref, ck_ref, cv_ref, cos_ref, sin_ref, g_ref,
                 o_ref, q_scr, k_scr, v_scr):
    i = pl.program_id(0)
    lam = sc_ref[0]
    out_scale = sc_ref[1]
    gain = g_ref[...] * out_scale
    lane = lax.broadcasted_iota(jnp.int32, (1, LANE), 1)
    first = lane < DIFF_HD
    nt = (((1,), (1,)), ((), ()))

    def halves(q):
        zero = jnp.zeros_like(q)
        return jnp.where(first, q, zero), jnp.where(first, zero, q)

    def weights(s):
        return jnp.exp(s - jnp.max(s, axis=-1, keepdims=True)).astype(BF16)

    def finish(of0, of1):
        o = of0[:, :LANE] / of0[:, LANE:] - lam * (of1[:, :LANE] / of1[:, LANE:])
        return (_rms(o) * gain).astype(BF16)

    @pl.when(i < N_PBLK)
    def _():
        n_seq = SEQ_BLK // T_P
        rows = [slice(s * T_P, (s + 1) * T_P) for s in range(n_seq)]
        q0, q1 = halves((q_ref[...].astype(F32) * (DIFF_HD ** -0.5)).astype(BF16))
        s0 = jnp.concatenate([lax.dot_general(q0[r], k_ref[r, :], nt, preferred_element_type=F32)
                              for r in rows], axis=0)
        s1 = jnp.concatenate([lax.dot_general(q1[r], k_ref[r, :], nt, preferred_element_type=F32)
                              for r in rows], axis=0)
        e0 = weights(s0)
        e1 = weights(s1)
        ones = jnp.ones((T_P, LANE), BF16)
        for r in rows:
            v1 = jnp.concatenate([v_ref[r, :], ones], axis=1)
            o_ref[r, :] = finish(jnp.dot(e0[r], v1, preferred_element_type=F32),
                                 jnp.dot(e1[r], v1, preferred_element_type=F32))

    @pl.when(i >= N_PBLK)
    def _():
        cos = cos_ref[...]
        sin = sin_ref[...]
        low = (lax.broadcasted_iota(jnp.int32, (T_S, LANE), 1) & 16) == 0

        def rope(x):
            xs = jnp.where(low, pltpu.roll(x, LANE - 16, 1), pltpu.roll(x, 16, 1))
            return x * cos + xs * sin

        q_scr[...] = (rope(q_ref[...].astype(F32)) * (DIFF_HD ** -0.5)).astype(BF16)
        k_scr[0:T_S, :] = rope(k_ref[...].astype(F32)).astype(BF16)
        k_scr[T_S:T_S + PAST, :] = ck_ref[...].astype(BF16)
        v_scr[0:T_S, 0:LANE] = v_ref[...]
        v_scr[T_S:T_S + PAST, 0:LANE] = cv_ref[...].astype(BF16)
        v_scr[:, LANE:2 * LANE] = jnp.ones((T_S + PAST, LANE), BF16)

        for b in range(T_S // ATT_TQ):
            r = slice(b * ATT_TQ, (b + 1) * ATT_TQ)
            q0, q1 = halves(q_scr[r, :])
            e0 = weights(lax.dot_general(q0, k_scr[...], nt, preferred_element_type=F32))
            e1 = weights(lax.dot_general(q1, k_scr[...], nt, preferred_element_type=F32))
            o_ref[r, :] = finish(jnp.dot(e0, v_scr[...], preferred_element_type=F32),
                                 jnp.dot(e1, v_scr[...], preferred_element_type=F32))


def _attention(scal, proj, cache_k, cache_v, cos_t, sin_t, subln_g):
    nb = N_TOK // SEQ_BLK
    smp = lambda i: jnp.maximum(i - N_PBLK, 0)
    return pl.pallas_call(
        _attn_kernel,
        out_shape=jax.ShapeDtypeStruct((N_TOK, DIFF_H * 2 * DIFF_HD), BF16),
        grid=(nb, DIFF_H),
        in_specs=[pl.BlockSpec(memory_space=pltpu.SMEM),
                  pl.BlockSpec((None, SEQ_BLK, LANE), lambda i, h: (C_DQ + h, i, 0)),
                  pl.BlockSpec((None, SEQ_BLK, LANE), lambda i, h: (C_DK + h, i, 0)),
                  pl.BlockSpec((None, SEQ_BLK, LANE), lambda i, h: (C_DV + h, i, 0)),
                  pl.BlockSpec((None, PAST, LANE), lambda i, h: (smp(i), 0, h)),
                  pl.BlockSpec((None, PAST, LANE), lambda i, h: (smp(i), 0, h)),
                  pl.BlockSpec((T_S, LANE), lambda i, h: (0, 0)),
                  pl.BlockSpec((T_S, LANE), lambda i, h: (0, 0)),
                  pl.BlockSpec((1, LANE), lambda i, h: (0, 0))],
        out_specs=pl.BlockSpec((SEQ_BLK, LANE), lambda i, h: (i, h)),
        scratch_shapes=[pltpu.VMEM((T_S, LANE), BF16),
                        pltpu.VMEM((T_S + PAST, LANE), BF16),
                        pltpu.VMEM((T_S + PAST, 2 * LANE), BF16)],
        compiler_params=_cparams(("arbitrary", "arbitrary")),
        name="attn",
    )(scal, proj, proj, proj, cache_k, cache_v, cos_t, sin_t, subln_g)


def _rope_tables():
    n_rows = T_S // GRID_W
    row = jnp.repeat(jnp.arange(n_rows), GRID_W).astype(F32)
    col = jnp.tile(jnp.arange(GRID_W), n_rows).astype(F32)
    n_freq = DIFF_HD // 4
    inv = ROPE_BASE ** (-jnp.arange(n_freq, dtype=F32) / n_freq)

    def axis_tables(pos):
        ang = pos[:, None] * inv[None, :]
        c = jnp.cos(ang)
        s = jnp.sin(ang)
        return jnp.concatenate([c, c], axis=-1), jnp.concatenate([-s, s], axis=-1)

    cr, sr = axis_tables(row)
    cc, sc = axis_tables(col)
    cos_h = jnp.concatenate([cr, cc], axis=-1)
    sin_h = jnp.concatenate([sr, sc], axis=-1)
    return jnp.concatenate([cos_h, cos_h], axis=-1), jnp.concatenate([sin_h, sin_h], axis=-1)


OP_TM = 512
OP_NPT = N_PROMPT // OP_TM


def _outproj_kernel(ra_ref, da_ref, gr_ref, gd_ref, xp_ref, xs_ref, mod_ref, n2_ref,
                    wro_ref, wdo_ref, wo_ref, wr_ref, br_ref,
                    x1_ref, h2_ref, lg_ref, m_scr):
    i = pl.program_id(0)
    ret_out = jnp.dot(ra_ref[...], wro_ref[...], preferred_element_type=F32)
    diff_out = jnp.dot(da_ref[...], wdo_ref[...], preferred_element_type=F32)
    for c in range(D // LANE):
        sl = slice(c * LANE, (c + 1) * LANE)
        m = (jax.nn.sigmoid(gr_ref[c].astype(F32)) * ret_out[:, sl]
             + jax.nn.sigmoid(gd_ref[c].astype(F32)) * diff_out[:, sl])
        m_scr[:, sl] = m.astype(BF16)
    mix = jnp.dot(m_scr[...], wo_ref[...], preferred_element_type=F32)
    mod = mod_ref[...]

    def fin(x_ref):
        x1 = x_ref[...] + mod[:, 2 * D:3 * D] * mix
        x1_ref[...] = x1
        h2 = (_rms(x1) * n2_ref[...] * (1.0 + mod[:, 4 * D:5 * D]) + mod[:, 3 * D:4 * D]).astype(BF16)
        h2_ref[...] = h2
        lg_ref[...] = jnp.dot(h2, wr_ref[...], preferred_element_type=F32) + br_ref[...]

    @pl.when(i < OP_NPT)
    def _():
        fin(xp_ref)

    @pl.when(i >= OP_NPT)
    def _():
        fin(xs_ref)


def _outproj(ret_act, diff_act, proj, xp, xs, mod3, n2, wro, wdo, wo, wr, br):
    npt = OP_NPT
    full = lambda i: (0, 0)
    return pl.pallas_call(
        _outproj_kernel,
        out_shape=(jax.ShapeDtypeStruct((N_TOK, D), F32),
                   jax.ShapeDtypeStruct((N_TOK, D), BF16),
                   jax.ShapeDtypeStruct((N_TOK, ROUTE_W), F32)),
        grid=(N_TOK // OP_TM,),
        in_specs=[pl.BlockSpec((OP_TM, D), lambda i: (i, 0)),
                  pl.BlockSpec((OP_TM, D), lambda i: (i, 0)),
                  pl.BlockSpec((8, OP_TM, LANE), lambda i: (C_GR // 8, i, 0)),
                  pl.BlockSpec((8, OP_TM, LANE), lambda i: (C_GD // 8, i, 0)),
                  pl.BlockSpec((OP_TM, D), lambda i: (jnp.minimum(i, npt - 1), 0)),
                  pl.BlockSpec((OP_TM, D), lambda i: (jnp.maximum(i - npt, 0), 0)),
                  pl.BlockSpec((None, 1, 6 * D), lambda i: (_mod_row(OP_TM)(i), 0, 0)),
                  pl.BlockSpec((1, D), full),
                  pl.BlockSpec((D, D), full),
                  pl.BlockSpec((D, D), full),
                  pl.BlockSpec((D, D), full),
                  pl.BlockSpec((D, ROUTE_W), full),
                  pl.BlockSpec((1, ROUTE_W), full)],
        out_specs=(pl.BlockSpec((OP_TM, D), lambda i: (i, 0)),
                   pl.BlockSpec((OP_TM, D), lambda i: (i, 0)),
                   pl.BlockSpec((OP_TM, ROUTE_W), lambda i: (i, 0))),
        scratch_shapes=[pltpu.VMEM((OP_TM, D), BF16)],
        compiler_params=_cparams(("arbitrary",)),
        name="outproj",
    )(ret_act, diff_act, proj, proj, xp, xs, mod3, n2, wro, wdo, wo, wr, br)


RT_TM = 512
RT_NT = N_TOK // RT_TM
PIECE = 16
R_LOC = RT_TM + N_GROUPS * PIECE
R_STAGE = 640
EX_TM = 512
R_MAX = 11264
EX_NT = R_MAX // EX_TM


def _route_kernel(lg_ref, cw8_ref, info_ref, pc_ref):
    lg = lg_ref[...]
    lane = lax.broadcasted_iota(jnp.int32, lg.shape, 1)
    neg = jnp.float32(-jnp.inf)
    big = jnp.int32(ROUTE_W)

    def first_where(cond):
        return jnp.min(jnp.where(cond, lane, big), axis=-1, keepdims=True)

    is_g = lane < N_GROUPS
    gl = jnp.where(is_g, lg, neg)
    gmax = jnp.max(gl, axis=-1, keepdims=True)
    gsum = jnp.sum(jnp.where(is_g, jnp.exp(lg - gmax), 0.0), axis=-1, keepdims=True)
    p_top = 1.0 / gsum
    g_idx = first_where(gl == gmax)
    in_grp = jnp.logical_and(lane >= N_GROUPS, ((lane - N_GROUPS) >> 3) == g_idx)
    el = jnp.where(in_grp, lg, neg)
    emax = jnp.max(el, axis=-1, keepdims=True)
    ee = jnp.where(in_grp, jnp.exp(lg - emax), 0.0)
    ep = ee / jnp.sum(ee, axis=-1, keepdims=True)
    ep = jnp.where(in_grp, ep, -1.0)
    e1 = jnp.max(ep, axis=-1, keepdims=True)
    i1 = first_where(ep == e1)
    ep2 = jnp.where(lane == i1, -1.0, ep)
    e2 = jnp.max(ep2, axis=-1, keepdims=True)
    i2 = first_where(ep2 == e2)
    den = e1 + e2
    cw = (jnp.where(lane == i1, p_top * e1 / den, 0.0)
          + jnp.where(lane == i2, p_top * e2 / den, 0.0))
    cw8 = jnp.zeros_like(cw)
    for g in range(N_GROUPS):
        cw8 = cw8 + jnp.where(g_idx == g, pltpu.roll(cw, LANE - (N_GROUPS + EPG * g), 1), 0.0)
    cw8_ref[...] = cw8

    onehot = (lane == g_idx).astype(F32)
    ii = lax.broadcasted_iota(jnp.int32, (RT_TM, RT_TM), 0)
    jj = lax.broadcasted_iota(jnp.int32, (RT_TM, RT_TM), 1)
    earlier = (jj < ii).astype(BF16)
    prefix = jnp.dot(earlier, onehot.astype(BF16), preferred_element_type=F32)
    cnt = jnp.sum(onehot, axis=0, keepdims=True)
    pc = jnp.floor((cnt + (PIECE - 1.0)) * (1.0 / PIECE)) * PIECE
    lo = pltpu.roll(pc, 1, 1) + pltpu.roll(pc, 2, 1) + pltpu.roll(pc, 3, 1)
    dest = jnp.sum(onehot * (prefix + lo), axis=-1, keepdims=True)
    info_ref[...] = jnp.where(lane == 0, g_idx.astype(F32), jnp.where(lane == 1, dest, 0.0))
    pc_ref[...] = pc


def _route(logits):
    return pl.pallas_call(
        _route_kernel,
        out_shape=(jax.ShapeDtypeStruct((N_TOK, ROUTE_W), F32),
                   jax.ShapeDtypeStruct((N_TOK, ROUTE_W), F32),
                   jax.ShapeDtypeStruct((RT_NT, 1, ROUTE_W), F32)),
        grid=(RT_NT,),
        in_specs=[pl.BlockSpec((RT_TM, ROUTE_W), lambda i: (i, 0))],
        out_specs=(pl.BlockSpec((RT_TM, ROUTE_W), lambda i: (i, 0)),
                   pl.BlockSpec((RT_TM, ROUTE_W), lambda i: (i, 0)),
                   pl.BlockSpec((None, 1, ROUTE_W), lambda i: (i, 0, 0))),
        compiler_params=_cparams(("arbitrary",)),
        name="route",
    )(logits)


def _dispatch_plan(pc_arr):
    pc = pc_arr[:, 0, :N_GROUPS].astype(jnp.int32)
    seg_len = jnp.sum(pc, axis=0)
    seg_pad = (seg_len + (EX_TM - 1)) // EX_TM * EX_TM
    seg_end_pad = jnp.cumsum(seg_pad)
    seg_start = seg_end_pad - seg_pad
    chunk_off = seg_start[None, :] + jnp.cumsum(pc, axis=0) - pc
    tile_start = jnp.arange(EX_NT, dtype=jnp.int32) * EX_TM
    tile_group = jnp.minimum(jnp.sum((tile_start[:, None] >= seg_end_pad[None, :]).astype(jnp.int32), axis=1),
                             N_GROUPS - 1)
    tile_valid = (tile_start < (seg_start + seg_len)[tile_group]).astype(jnp.int32)
    return chunk_off.astype(jnp.int32), pc, tile_group.astype(jnp.int32), tile_valid


def _piece_copies(off_ref, pc_ref, tile, make):
    lo = 0
    for g in range(N_GROUPS):
        n = pc_ref[tile, g] // PIECE
        base = off_ref[tile, g]

        def body(j, carry, lo=lo, base=base):
            make(pl.multiple_of(lo + j * PIECE, PIECE), pl.multiple_of(base + j * PIECE, PIECE))
            return carry

        lax.fori_loop(0, n, body, 0)
        lo = lo + pc_ref[tile, g]


def _piece_count(pc_ref, tile):
    n = 0
    for g in range(N_GROUPS):
        n = n + pc_ref[tile, g] // PIECE
    return n


def _dispatch_kernel(off_ref, pc_ref, h_ref, info_ref, cw8_ref, xs_in, cws_in,
                     xs_out, cws_out, x_scr, c_scr, sem):
    del xs_in, cws_in
    i = pl.program_id(0)
    slot = i % 2
    dest = info_ref[...].T[1:2, :]
    row = lax.broadcasted_iota(jnp.int32, (R_LOC, RT_TM), 0).astype(F32)
    sel = row == dest
    x_scr[slot] = jnp.dot(sel.astype(BF16), h_ref[...], preferred_element_type=F32).astype(BF16)
    c_scr[slot] = jnp.dot(sel.astype(F32), cw8_ref[...], preferred_element_type=F32,
                          precision=lax.Precision.HIGHEST)

    def x_copy(s, src, dst):
        return pltpu.make_async_copy(x_scr.at[s, pl.ds(src, PIECE)], xs_out.at[pl.ds(dst, PIECE)],
                                     sem.at[0, s])

    def c_copy(s, src, dst):
        return pltpu.make_async_copy(c_scr.at[s, pl.ds(src, PIECE)], cws_out.at[pl.ds(dst, PIECE)],
                                     sem.at[1, s])

    def start(src, dst):
        x_copy(slot, src, dst).start()
        c_copy(slot, src, dst).start()

    _piece_copies(off_ref, pc_ref, i, start)

    def wait_tile(tile, s):
        def wait(j, carry):
            x_copy(s, 0, 0).wait()
            c_copy(s, 0, 0).wait()
            return carry

        lax.fori_loop(0, _piece_count(pc_ref, tile), wait, 0)

    @pl.when(i > 0)
    def _():
        wait_tile(i - 1, 1 - slot)

    @pl.when(i == RT_NT - 1)
    def _():
        wait_tile(i, slot)


def _dispatch(chunk_off, pc, h2, info, cw8):
    xs0 = jnp.zeros((R_MAX, D), BF16)
    cws0 = jnp.zeros((R_MAX, ROUTE_W), F32)
    grid_spec = pltpu.PrefetchScalarGridSpec(
        num_scalar_prefetch=2,
        grid=(RT_NT,),
        in_specs=[pl.BlockSpec((RT_TM, D), lambda i, o, p: (i, 0)),
                  pl.BlockSpec((RT_TM, ROUTE_W), lambda i, o, p: (i, 0)),
                  pl.BlockSpec((RT_TM, ROUTE_W), lambda i, o, p: (i, 0)),
                  pl.BlockSpec(memory_space=pl.ANY),
                  pl.BlockSpec(memory_space=pl.ANY)],
        out_specs=(pl.BlockSpec(memory_space=pl.ANY), pl.BlockSpec(memory_space=pl.ANY)),
        scratch_shapes=[pltpu.VMEM((2, R_LOC, D), BF16),
                        pltpu.VMEM((2, R_LOC, ROUTE_W), F32),
                        pltpu.SemaphoreType.DMA((2, 2))])
    return pl.pallas_call(
        _dispatch_kernel,
        out_shape=(jax.ShapeDtypeStruct((R_MAX, D), BF16),
                   jax.ShapeDtypeStruct((R_MAX, ROUTE_W), F32)),
        grid_spec=grid_spec,
        input_output_aliases={5: 0, 6: 1},
        compiler_params=_cparams(("arbitrary",)),
        name="dispatch",
    )(chunk_off, pc, h2, info, cw8, xs0, cws0)


def _expert_kernel(tg_ref, tv_ref, x_ref, cw_ref, wg_ref, wu_ref, wd_ref, y_ref):
    del tg_ref
    k = pl.program_id(0)

    @pl.when(tv_ref[k] == 0)
    def _():
        y_ref[...] = jnp.zeros_like(y_ref)

    @pl.when(tv_ref[k] != 0)
    def _():
        x = x_ref[...]
        cw = cw_ref[...]
        lane = lax.broadcasted_iota(jnp.int32, cw.shape, 1)
        acc = jnp.zeros(y_ref.shape, F32)
        for j in range(EPG):
            w = jnp.sum(jnp.where(lane == j, cw, 0.0), axis=-1, keepdims=True)
            a = (jax.nn.silu(jnp.dot(x, wg_ref[j], preferred_element_type=F32))
                 * jnp.dot(x, wu_ref[j], preferred_element_type=F32))
            acc = acc + jnp.dot((a * w).astype(BF16), wd_ref[j], preferred_element_type=F32)
        y_ref[...] = acc.astype(BF16)


def _experts(tile_group, tile_valid, xs, cws, wg, wu, wd):
    grid_spec = pltpu.PrefetchScalarGridSpec(
        num_scalar_prefetch=2,
        grid=(EX_NT,),
        in_specs=[pl.BlockSpec((EX_TM, D), lambda k, tg, tv: (k, 0)),
                  pl.BlockSpec((EX_TM, ROUTE_W), lambda k, tg, tv: (k, 0)),
                  pl.BlockSpec((EPG, D, FF), lambda k, tg, tv: (tg[k], 0, 0)),
                  pl.BlockSpec((EPG, D, FF), lambda k, tg, tv: (tg[k], 0, 0)),
                  pl.BlockSpec((EPG, FF, D), lambda k, tg, tv: (tg[k], 0, 0))],
        out_specs=pl.BlockSpec((EX_TM, D), lambda k, tg, tv: (k, 0)))
    return pl.pallas_call(
        _expert_kernel,
        out_shape=jax.ShapeDtypeStruct((R_MAX, D), BF16),
        grid_spec=grid_spec,
        compiler_params=_cparams(("arbitrary",)),
        name="experts",
    )(tile_group, tile_valid, xs, cws, wg, wu, wd)


CB_NPT = N_PROMPT // RT_TM


def _combine_kernel(off_ref, pc_ref, info_ref, x1_ref, mod_ref, g_ref, ys_hbm,
                    yp_ref, ysm_ref, stage, sem):
    i = pl.program_id(0)
    slot = i % 2

    def copy(s, dst, src):
        return pltpu.make_async_copy(ys_hbm.at[pl.ds(src, PIECE)], stage.at[s, pl.ds(dst, PIECE)],
                                     sem.at[s])

    def fetch(tile, s):
        _piece_copies(off_ref, pc_ref, tile, lambda loc, glob: copy(s, loc, glob).start())

    @pl.when(i == 0)
    def _():
        stage[...] = jnp.zeros_like(stage)
        fetch(0, 0)

    @pl.when(i + 1 < RT_NT)
    def _():
        fetch(i + 1, 1 - slot)

    def wait(j, carry):
        copy(slot, 0, 0).wait()
        return carry

    lax.fori_loop(0, _piece_count(pc_ref, i), wait, 0)

    dest = info_ref[...][:, 1:2]
    col = lax.broadcasted_iota(jnp.int32, (RT_TM, R_STAGE), 1).astype(F32)
    moe = jnp.dot((col == dest).astype(BF16), stage[slot], preferred_element_type=F32)
    mod = mod_ref[...]
    out = _rms(x1_ref[...] + mod[:, 5 * D:6 * D] * moe) * g_ref[...]

    @pl.when(i < CB_NPT)
    def _():
        yp_ref[...] = out

    @pl.when(i >= CB_NPT)
    def _():
        ysm_ref[...] = out


def _combine(chunk_off, pc, info, x1, mod3, fg, ys):
    npt = CB_NPT
    grid_spec = pltpu.PrefetchScalarGridSpec(
        num_scalar_prefetch=2,
        grid=(RT_NT,),
        in_specs=[pl.BlockSpec((RT_TM, ROUTE_W), lambda i, o, p: (i, 0)),
                  pl.BlockSpec((RT_TM, D), lambda i, o, p: (i, 0)),
                  pl.BlockSpec((None, 1, 6 * D), lambda i, o, p: (_mod_row(RT_TM)(i), 0, 0)),
                  pl.BlockSpec((1, D), lambda i, o, p: (0, 0)),
                  pl.BlockSpec(memory_space=pl.ANY)],
        out_specs=(pl.BlockSpec((RT_TM, D), lambda i, o, p: (jnp.minimum(i, npt - 1), 0)),
                   pl.BlockSpec((RT_TM, D), lambda i, o, p: (jnp.maximum(i - npt, 0), 0))),
        scratch_shapes=[pltpu.VMEM((2, R_STAGE, D), BF16),
                        pltpu.SemaphoreType.DMA((2,))])
    return pl.pallas_call(
        _combine_kernel,
        out_shape=(jax.ShapeDtypeStruct((N_PROMPT, D), F32),
                   jax.ShapeDtypeStruct((N_SAMPLE, D), F32)),
        grid_spec=grid_spec,
        compiler_params=_cparams(("arbitrary",)),
        name="combine",
    )(chunk_off, pc, info, x1, mod3, fg, ys)


def kernel(x_prompt, x_sample, cache_diff_k, cache_diff_v, state_ret_fwd, state_ret_bwd, c, c_ctx,
           w_ada, b_ada, norm1_g, norm2_g, w_in, ret_decay_fwd, ret_decay_bwd, ret_norm_g,
           diff_lambda_q1, diff_lambda_k1, diff_lambda_q2, diff_lambda_k2, diff_subln_g,
           w_ret_o, w_diff_o, w_o, router_group_w, router_group_b, router_expert_w, router_expert_b,
           moe_w_gate, moe_w_up, moe_w_down, final_norm_g):
    l = 0
    lam_init = 0.8 - 0.6 * math.exp(-0.3 * l)
    lam = (jnp.exp(jnp.sum(diff_lambda_q1[l].astype(F32) * diff_lambda_k1[l].astype(F32)))
           - jnp.exp(jnp.sum(diff_lambda_q2[l].astype(F32) * diff_lambda_k2[l].astype(F32))) + lam_init)
    attn_scal = jnp.stack([lam, jnp.float32(1.0 - lam_init)]).astype(F32)
    lg = jnp.stack([jax.nn.log_sigmoid(ret_decay_fwd[l].astype(F32)),
                    jax.nn.log_sigmoid(ret_decay_bwd[l].astype(F32))])

    xp = x_prompt.reshape(N_PROMPT, D)
    xs = x_sample.reshape(N_SAMPLE, D)
    cvec = jnp.concatenate([c_ctx[None, :], c, jnp.zeros((3, D), F32)], axis=0)
    mod3 = _modulation(cvec, w_ada[l], b_ada[l][None, :]).reshape(8, 1, 6 * D)

    proj, k32, v32 = _inproj(xp, xs, mod3, norm1_g[l][None, :], w_in[l].astype(BF16))

    ret_act, s_f, s_b = _retention(lg, proj, state_ret_fwd[:, l], state_ret_bwd[:, l],
                                   ret_norm_g[l][None, :])
    cos_t, sin_t = _rope_tables()
    diff_act = _attention(attn_scal, proj,
                          cache_diff_k[:, l].reshape(4, PAST, D), cache_diff_v[:, l].reshape(4, PAST, D),
                          cos_t, sin_t, diff_subln_g[l][None, :])

    wr = jnp.concatenate([router_group_w[l], router_expert_w[l],
                          jnp.zeros((D, ROUTE_W - N_GROUPS - N_EXP), F32)], axis=1).astype(BF16)
    br = jnp.concatenate([router_group_b[l], router_expert_b[l],
                          jnp.zeros((ROUTE_W - N_GROUPS - N_EXP,), F32)])[None, :]
    x1, h2, logits = _outproj(ret_act, diff_act, proj, xp, xs, mod3, norm2_g[l][None, :],
                              w_ret_o[l].astype(BF16), w_diff_o[l].astype(BF16), w_o[l].astype(BF16),
                              wr, br)
    cw8, info, pc_arr = _route(logits)
    chunk_off, pc, tile_group, tile_valid = _dispatch_plan(pc_arr)
    xs_sorted, cw_sorted = _dispatch(chunk_off, pc, h2, info, cw8)
    y_sorted = _experts(tile_group, tile_valid, xs_sorted, cw_sorted,
                        moe_w_gate[l].astype(BF16), moe_w_up[l].astype(BF16), moe_w_down[l].astype(BF16))
    yp, ys = _combine(chunk_off, pc, info, x1, mod3, final_norm_g[None, :], y_sorted)

    return (yp.reshape(16, T_P, D), ys.reshape(4, T_S, D),
            k32.reshape(16, 1, T_P, DIFF_H, 2, DIFF_HD), v32.reshape(16, 1, T_P, DIFF_H, 2 * DIFF_HD),
            s_f.reshape(16, 1, RET_H, RET_KD, RET_VD), s_b.reshape(16, 1, RET_H, RET_KD, RET_VD))
```

```python
import functools
import math

import jax
import jax.numpy as jnp
from jax import lax
from jax.experimental import pallas as pl
from jax.experimental.pallas import tpu as pltpu

F32 = jnp.float32
BF16 = jnp.bfloat16

D = 1024
N_PROMPT = 16 * 256
N_SAMPLE = 4 * 1024
N_TOK = N_PROMPT + N_SAMPLE
T_P = 256
T_S = 1024
PAST = 512
GRID_W = 64
RET_H = 4
RET_KD = 128
RET_VD = 256
DIFF_H = 8
DIFF_HD = 64
ROPE_BASE = 10000.0
N_GROUPS = 4
EPG = 8
N_EXP = 32
FF = 256
EPS = 1e-6
IN_W = 8192
LANE = 128
N_CHUNK = IN_W // LANE
C_RQ, C_RK, C_RV, C_RG, C_DQ, C_DK, C_DV, C_GR, C_GD = 0, 4, 8, 16, 24, 32, 40, 48, 56
ROUTE_W = 128
SEQ_BLK = 1024
N_PBLK = N_PROMPT // SEQ_BLK
VMEM_LIMIT = 56 * 1024 * 1024


def _cparams(sem):
    return pltpu.CompilerParams(dimension_semantics=sem, vmem_limit_bytes=VMEM_LIMIT)


def _mod_row(tile_rows):
    def f(i):
        start = i * tile_rows
        return jnp.where(start < N_PROMPT, 0, 1 + (start - N_PROMPT) // T_S)
    return f


def _rms(x):
    return x * lax.rsqrt(jnp.mean(x * x, axis=-1, keepdims=True) + EPS)


def _mod_kernel(c_ref, w_ref, b_ref, o_ref):
    s = jax.nn.silu(c_ref[...])
    o_ref[...] = jnp.dot(s.astype(BF16), w_ref[...].astype(BF16),
                         preferred_element_type=F32) + b_ref[...]


def _modulation(cvec, w_ada, b_ada):
    tn = 1536
    return pl.pallas_call(
        _mod_kernel,
        out_shape=jax.ShapeDtypeStruct((8, 6 * D), F32),
        grid=(6 * D // tn,),
        in_specs=[pl.BlockSpec((8, D), lambda j: (0, 0)),
                  pl.BlockSpec((D, tn), lambda j: (0, j)),
                  pl.BlockSpec((1, tn), lambda j: (0, j))],
        out_specs=pl.BlockSpec((8, tn), lambda j: (0, j)),
        compiler_params=_cparams(("arbitrary",)),
        name="mod",
    )(cvec, w_ada, b_ada)


IP_TM = 512
IP_TN = 2048
IP_NPT = N_PROMPT // IP_TM


IP_KV_TILE = C_DK * LANE // IP_TN
IP_SPT = IP_TM // T_P


def _inproj_kernel(xp_ref, xs_ref, mod_ref, n1_ref, w_ref, proj_ref, kt_ref, v32_ref, w_scr):
    j = pl.program_id(0)
    i = pl.program_id(1)

    @pl.when(i == 0)
    def _():
        w_scr[...] = w_ref[...].astype(BF16)

    x = jnp.where(i < IP_NPT, xp_ref[...], xs_ref[...])
    mod = mod_ref[...]
    h = (_rms(x) * n1_ref[...] * (1.0 + mod[:, D:2 * D]) + mod[:, 0:D]).astype(BF16)
    acc = jnp.dot(h, w_scr[...], preferred_element_type=F32)
    for c in range(IP_TN // LANE):
        proj_ref[c] = acc[:, c * LANE:(c + 1) * LANE].astype(BF16)

    @pl.when(jnp.logical_and(j == IP_KV_TILE, i < IP_NPT))
    def _():
        for s in range(IP_SPT):
            kt_ref[s] = acc[s * T_P:(s + 1) * T_P, :D].T
        v32_ref[...] = acc[:, D:]


def _inproj(xp, xs, mod3, n1, w_in):
    npt = IP_NPT
    cpt = IP_TN // LANE

    def kv_tile(j, i):
        return jnp.where(j < IP_KV_TILE, 0, jnp.where(j == IP_KV_TILE, jnp.minimum(i, npt - 1), npt - 1))

    return pl.pallas_call(
        _inproj_kernel,
        out_shape=(jax.ShapeDtypeStruct((N_CHUNK, N_TOK, LANE), BF16),
                   jax.ShapeDtypeStruct((N_PROMPT // T_P, D, T_P), F32),
                   jax.ShapeDtypeStruct((N_PROMPT, D), F32)),
        grid=(IN_W // IP_TN, N_TOK // IP_TM),
        in_specs=[pl.BlockSpec((IP_TM, D), lambda j, i: (jnp.minimum(i, npt - 1), 0)),
                  pl.BlockSpec((IP_TM, D), lambda j, i: (jnp.maximum(i - npt, 0), 0)),
                  pl.BlockSpec((None, 1, 6 * D), lambda j, i: (_mod_row(IP_TM)(i), 0, 0)),
                  pl.BlockSpec((1, D), lambda j, i: (0, 0)),
                  pl.BlockSpec((D, IP_TN), lambda j, i: (0, j))],
        out_specs=(pl.BlockSpec((cpt, IP_TM, LANE), lambda j, i: (j, i, 0)),
                   pl.BlockSpec((IP_SPT, D, T_P), lambda j, i: (kv_tile(j, i), 0, 0)),
                   pl.BlockSpec((IP_TM, D), lambda j, i: (kv_tile(j, i), 0))),
        scratch_shapes=[pltpu.VMEM((D, IP_TN), BF16)],
        compiler_params=_cparams(("arbitrary", "arbitrary")),
        name="inproj",
    )(xp, xs, mod3, n1, w_in)


def _decay_mask(t, lgf, lgb):
    ii = lax.broadcasted_iota(jnp.int32, (t, t), 0)
    jj = lax.broadcasted_iota(jnp.int32, (t, t), 1)
    rel = (ii - jj).astype(F32)
    e = jnp.exp(jnp.where(rel >= 0.0, lgf, -lgb) * rel)
    return jnp.where(rel == 0.0, 2.0, e) * (RET_KD ** -0.5)


def _ret_kernel(lg_ref, q_ref, k_ref, v_ref, rg_ref, s0f_ref, s0b_ref, g_ref,
                o_ref, sf_ref, sb_ref):
    i = pl.program_id(0)
    h = pl.program_id(1)
    lgf = lg_ref[0, h]
    lgb = lg_ref[1, h]
    gain = g_ref[...]
    nt = (((1,), (1,)), ((), ()))
    tn = (((0,), (0,)), ((), ()))

    def finish(o, rg):
        d = o - jnp.mean(o, axis=-1, keepdims=True)
        y = d * lax.rsqrt(jnp.mean(d * d, axis=-1, keepdims=True) + EPS) * gain
        return (jax.nn.silu(rg.astype(F32)) * y).astype(BF16)

    def vcat(r):
        return jnp.concatenate([v_ref[0, r, :], v_ref[1, r, :]], axis=1)

    def gcat(r):
        return jnp.concatenate([rg_ref[0, r, :], rg_ref[1, r, :]], axis=1)

    @pl.when(i < N_PBLK)
    def _():
        dm = _decay_mask(T_P, lgf, lgb)
        t = lax.broadcasted_iota(jnp.int32, (T_P, 1), 0).astype(F32)
        kdf = jnp.exp(lgf * (T_P - 1.0 - t)) * (RET_KD ** -0.5)
        kdb = jnp.exp(lgb * t) * (RET_KD ** -0.5)
        for s in range(SEQ_BLK // T_P):
            r = slice(s * T_P, (s + 1) * T_P)
            q = q_ref[r, :]
            k = k_ref[r, :]
            v = vcat(r)
            sc = lax.dot_general(q, k, nt, preferred_element_type=F32)
            o = jnp.dot((sc * dm).astype(BF16), v, preferred_element_type=F32)
            o_ref[r, :] = finish(o, gcat(r))
            kf = k.astype(F32)
            sf_ref[s] = lax.dot_general((kf * kdf).astype(BF16), v, tn, preferred_element_type=F32)
            sb_ref[s] = lax.dot_general((kf * kdb).astype(BF16), v, tn, preferred_element_type=F32)

    @pl.when(i >= N_PBLK)
    def _():
        r = slice(0, T_S)
        dm = _decay_mask(T_S, lgf, lgb)
        t = lax.broadcasted_iota(jnp.int32, (T_S, 1), 0).astype(F32)
        q = q_ref[...]
        k = k_ref[...]
        v = vcat(r)
        sc = lax.dot_general(q, k, nt, preferred_element_type=F32)
        o = jnp.dot((sc * dm).astype(BF16), v, preferred_element_type=F32)
        qf = q.astype(F32)
        o = o + jnp.dot((qf * jnp.exp(lgf * (t + 1.0))).astype(BF16), s0f_ref[...].astype(BF16),
                        preferred_element_type=F32)
        o = o + jnp.dot((qf * jnp.exp(lgb * (T_S - t))).astype(BF16), s0b_ref[...].astype(BF16),
                        preferred_element_type=F32)
        o_ref[...] = finish(o, gcat(r))


def _retention(lg, proj, s0f, s0b, gnorm):
    nb = N_TOK // SEQ_BLK
    spb = SEQ_BLK // T_P
    smp = lambda i: jnp.maximum(i - N_PBLK, 0)
    pmt = lambda i: jnp.minimum(i, N_PBLK - 1)
    pmh = lambda i, h: jnp.where(i < N_PBLK, h, RET_H - 1)
    return pl.pallas_call(
        _ret_kernel,
        out_shape=(jax.ShapeDtypeStruct((N_TOK, RET_H * RET_VD), BF16),
                   jax.ShapeDtypeStruct((16, RET_H, RET_KD, RET_VD), F32),
                   jax.ShapeDtypeStruct((16, RET_H, RET_KD, RET_VD), F32)),
        grid=(nb, RET_H),
        in_specs=[pl.BlockSpec(memory_space=pltpu.SMEM),
                  pl.BlockSpec((None, SEQ_BLK, LANE), lambda i, h: (C_RQ + h, i, 0)),
                  pl.BlockSpec((None, SEQ_BLK, LANE), lambda i, h: (C_RK + h, i, 0)),
                  pl.BlockSpec((2, SEQ_BLK, LANE), lambda i, h: (C_RV // 2 + h, i, 0)),
                  pl.BlockSpec((2, SEQ_BLK, LANE), lambda i, h: (C_RG // 2 + h, i, 0)),
                  pl.BlockSpec((None, None, RET_KD, RET_VD), lambda i, h: (smp(i), h, 0, 0)),
                  pl.BlockSpec((None, None, RET_KD, RET_VD), lambda i, h: (smp(i), h, 0, 0)),
                  pl.BlockSpec((1, RET_VD), lambda i, h: (0, h))],
        out_specs=(pl.BlockSpec((SEQ_BLK, RET_VD), lambda i, h: (i, h)),
                   pl.BlockSpec((spb, None, RET_KD, RET_VD), lambda i, h: (pmt(i), pmh(i, h), 0, 0)),
                   pl.BlockSpec((spb, None, RET_KD, RET_VD), lambda i, h: (pmt(i), pmh(i, h), 0, 0))),
        compiler_params=_cparams(("arbitrary", "arbitrary")),
        name="ret",
    )(lg, proj, proj, proj, proj, s0f, s0b, gnorm)


ATT_TQ = 256


def _attn_kernel(sc_ref, q_ref, k_ref, v_ref, ckt_ref, cv_ref, cos_ref, sin_ref, g_ref,
                 o_ref, q_scr, k_scr, v_scr):
    i = pl.program_id(0)
    lam = sc_ref[0]
    out_scale = sc_ref[1]
    gain = g_ref[...] * out_scale
    lane = lax.broadcasted_iota(jnp.int32, (1, LANE), 1)
    first = lane < DIFF_HD
    nt = (((1,), (1,)), ((), ()))

    def halves(q):
        zero = jnp.zeros_like(q)
        return jnp.where(first, q, zero), jnp.where(first, zero, q)

    def weights(s):
        return jnp.exp(s - jnp.max(s, axis=-1, keepdims=True)).astype(BF16)

    def finish(of0, of1):
        o = of0[:, :LANE] / of0[:, LANE:] - lam * (of1[:, :LANE] / of1[:, LANE:])
        return (_rms(o) * gain).astype(BF16)

    @pl.when(i < N_PBLK)
    def _():
        n_seq = SEQ_BLK // T_P
        rows = [slice(s * T_P, (s + 1) * T_P) for s in range(n_seq)]
        q0, q1 = halves((q_ref[...].astype(F32) * (DIFF_HD ** -0.5)).astype(BF16))
        s0 = jnp.concatenate([lax.dot_general(q0[r], k_ref[r, :], nt, preferred_element_type=F32)
                              for r in rows], axis=0)
        s1 = jnp.concatenate([lax.dot_general(q1[r], k_ref[r, :], nt, preferred_element_type=F32)
                              for r in rows], axis=0)
        e0 = weights(s0)
        e1 = weights(s1)
        ones = jnp.ones((T_P, LANE), BF16)
        for r in rows:
            v1 = jnp.concatenate([v_ref[r, :], ones], axis=1)
            o_ref[r, :] = finish(jnp.dot(e0[r], v1, preferred_element_type=F32),
                                 jnp.dot(e1[r], v1, preferred_element_type=F32))

    @pl.when(i >= N_PBLK)
    def _():
        cos = cos_ref[...]
        sin = sin_ref[...]
        low = (lax.broadcasted_iota(jnp.int32, (T_S, LANE), 1) & 16) == 0

        def rope(x):
            xs = jnp.where(low, pltpu.roll(x, LANE - 16, 1), pltpu.roll(x, 16, 1))
            return x * cos + xs * sin

        q_scr[...] = (rope(q_ref[...].astype(F32)) * (DIFF_HD ** -0.5)).astype(BF16)
        k_scr[...] = rope(k_ref[...].astype(F32)).astype(BF16)
        ckt = ckt_ref[...].astype(BF16)
        v_scr[0:T_S, 0:LANE] = v_ref[...]
        v_scr[T_S:T_S + PAST, 0:LANE] = cv_ref[:, pl.program_id(1), :].astype(BF16)
        v_scr[:, LANE:2 * LANE] = jnp.ones((T_S + PAST, LANE), BF16)

        def scores(qh):
            return jnp.concatenate([lax.dot_general(qh, k_scr[...], nt, preferred_element_type=F32),
                                    jnp.dot(qh, ckt, preferred_element_type=F32)], axis=1)

        for b in range(T_S // ATT_TQ):
            r = slice(b * ATT_TQ, (b + 1) * ATT_TQ)
            q0, q1 = halves(q_scr[r, :])
            e0 = weights(scores(q0))
            e1 = weights(scores(q1))
            o_ref[r, :] = finish(jnp.dot(e0, v_scr[...], preferred_element_type=F32),
                                 jnp.dot(e1, v_scr[...], preferred_element_type=F32))


def _attention(scal, proj, cache_k, cache_v, cos_t, sin_t, subln_g):
    nb = N_TOK // SEQ_BLK
    smp = lambda i: jnp.maximum(i - N_PBLK, 0)
    return pl.pallas_call(
        _attn_kernel,
        out_shape=jax.ShapeDtypeStruct((N_TOK, DIFF_H * 2 * DIFF_HD), BF16),
        grid=(nb, DIFF_H),
        in_specs=[pl.BlockSpec(memory_space=pltpu.SMEM),
                  pl.BlockSpec((None, SEQ_BLK, LANE), lambda i, h: (C_DQ + h, i, 0)),
                  pl.BlockSpec((None, SEQ_BLK, LANE), lambda i, h: (C_DK + h, i, 0)),
                  pl.BlockSpec((None, SEQ_BLK, LANE), lambda i, h: (C_DV + h, i, 0)),
                  pl.BlockSpec((None, LANE, PAST), lambda i, h: (smp(i), h, 0)),
                  pl.BlockSpec((None, None, PAST, DIFF_H, LANE), lambda i, h: (smp(i), 0, 0, 0, 0)),
                  pl.BlockSpec((T_S, LANE), lambda i, h: (0, 0)),
                  pl.BlockSpec((T_S, LANE), lambda i, h: (0, 0)),
                  pl.BlockSpec((1, LANE), lambda i, h: (0, 0))],
        out_specs=pl.BlockSpec((SEQ_BLK, LANE), lambda i, h: (i, h)),
        scratch_shapes=[pltpu.VMEM((T_S, LANE), BF16),
                        pltpu.VMEM((T_S, LANE), BF16),
                        pltpu.VMEM((T_S + PAST, 2 * LANE), BF16)],
        compiler_params=_cparams(("arbitrary", "arbitrary")),
        name="attn",
    )(scal, proj, proj, proj, cache_k, cache_v, cos_t, sin_t, subln_g)


def _rope_tables():
    n_rows = T_S // GRID_W
    row = jnp.repeat(jnp.arange(n_rows), GRID_W).astype(F32)
    col = jnp.tile(jnp.arange(GRID_W), n_rows).astype(F32)
    n_freq = DIFF_HD // 4
    inv = ROPE_BASE ** (-jnp.arange(n_freq, dtype=F32) / n_freq)

    def axis_tables(pos):
        ang = pos[:, None] * inv[None, :]
        c = jnp.cos(ang)
        s = jnp.sin(ang)
        return jnp.concatenate([c, c], axis=-1), jnp.concatenate([-s, s], axis=-1)

    cr, sr = axis_tables(row)
    cc, sc = axis_tables(col)
    cos_h = jnp.concatenate([cr, cc], axis=-1)
    sin_h = jnp.concatenate([sr, sc], axis=-1)
    return jnp.concatenate([cos_h, cos_h], axis=-1), jnp.concatenate([sin_h, sin_h], axis=-1)


OP_TM = 512
OP_NPT = N_PROMPT // OP_TM


def _outproj_kernel(ra_ref, da_ref, gr_ref, gd_ref, xp_ref, xs_ref, mod_ref, n2_ref,
                    wro_ref, wdo_ref, wo_ref, wr_ref, br_ref,
                    x1_ref, h2_ref, lg_ref, m_scr):
    i = pl.program_id(0)
    ret_out = jnp.dot(ra_ref[...], wro_ref[...], preferred_element_type=F32)
    diff_out = jnp.dot(da_ref[...], wdo_ref[...], preferred_element_type=F32)
    for c in range(D // LANE):
        sl = slice(c * LANE, (c + 1) * LANE)
        m = (jax.nn.sigmoid(gr_ref[c].astype(F32)) * ret_out[:, sl]
             + jax.nn.sigmoid(gd_ref[c].astype(F32)) * diff_out[:, sl])
        m_scr[:, sl] = m.astype(BF16)
    mix = jnp.dot(m_scr[...], wo_ref[...], preferred_element_type=F32)
    mod = mod_ref[...]

    def fin(x_ref):
        x1 = x_ref[...] + mod[:, 2 * D:3 * D] * mix
        x1_ref[...] = x1
        h2 = (_rms(x1) * n2_ref[...] * (1.0 + mod[:, 4 * D:5 * D]) + mod[:, 3 * D:4 * D]).astype(BF16)
        h2_ref[...] = h2
        lg_ref[...] = jnp.dot(h2, wr_ref[...], preferred_element_type=F32) + br_ref[...]

    @pl.when(i < OP_NPT)
    def _():
        fin(xp_ref)

    @pl.when(i >= OP_NPT)
    def _():
        fin(xs_ref)


def _outproj(ret_act, diff_act, proj, xp, xs, mod3, n2, wro, wdo, wo, wr, br):
    npt = OP_NPT
    full = lambda i: (0, 0)
    return pl.pallas_call(
        _outproj_kernel,
        out_shape=(jax.ShapeDtypeStruct((N_TOK, D), F32),
                   jax.ShapeDtypeStruct((N_TOK, D), BF16),
                   jax.ShapeDtypeStruct((N_TOK, ROUTE_W), F32)),
        grid=(N_TOK // OP_TM,),
        in_specs=[pl.BlockSpec((OP_TM, D), lambda i: (i, 0)),
                  pl.BlockSpec((OP_TM, D), lambda i: (i, 0)),
                  pl.BlockSpec((8, OP_TM, LANE), lambda i: (C_GR // 8, i, 0)),
                  pl.BlockSpec((8, OP_TM, LANE), lambda i: (C_GD // 8, i, 0)),
                  pl.BlockSpec((OP_TM, D), lambda i: (jnp.minimum(i, npt - 1), 0)),
                  pl.BlockSpec((OP_TM, D), lambda i: (jnp.maximum(i - npt, 0), 0)),
                  pl.BlockSpec((None, 1, 6 * D), lambda i: (_mod_row(OP_TM)(i), 0, 0)),
                  pl.BlockSpec((1, D), full),
                  pl.BlockSpec((D, D), full),
                  pl.BlockSpec((D, D), full),
                  pl.BlockSpec((D, D), full),
                  pl.BlockSpec((D, ROUTE_W), full),
                  pl.BlockSpec((1, ROUTE_W), full)],
        out_specs=(pl.BlockSpec((OP_TM, D), lambda i: (i, 0)),
                   pl.BlockSpec((OP_TM, D), lambda i: (i, 0)),
                   pl.BlockSpec((OP_TM, ROUTE_W), lambda i: (i, 0))),
        scratch_shapes=[pltpu.VMEM((OP_TM, D), BF16)],
        compiler_params=_cparams(("arbitrary",)),
        name="outproj",
    )(ret_act, diff_act, proj, proj, xp, xs, mod3, n2, wro, wdo, wo, wr, br)


RT_TM = 512
RT_NT = N_TOK // RT_TM
PIECE = 16
R_LOC = RT_TM + N_GROUPS * PIECE
R_STAGE = 640
EX_TM = 512
R_MAX = 11264
EX_NT = R_MAX // EX_TM


def _route_kernel(lg_ref, cw8_ref, info_ref, pc_ref):
    lg = lg_ref[...]
    lane = lax.broadcasted_iota(jnp.int32, lg.shape, 1)
    neg = jnp.float32(-jnp.inf)
    big = jnp.int32(ROUTE_W)

    def first_where(cond):
        return jnp.min(jnp.where(cond, lane, big), axis=-1, keepdims=True)

    is_g = lane < N_GROUPS
    gl = jnp.where(is_g, lg, neg)
    gmax = jnp.max(gl, axis=-1, keepdims=True)
    gsum = jnp.sum(jnp.where(is_g, jnp.exp(lg - gmax), 0.0), axis=-1, keepdims=True)
    p_top = 1.0 / gsum
    g_idx = first_where(gl == gmax)
    in_grp = jnp.logical_and(lane >= N_GROUPS, ((lane - N_GROUPS) >> 3) == g_idx)
    el = jnp.where(in_grp, lg, neg)
    emax = jnp.max(el, axis=-1, keepdims=True)
    ee = jnp.where(in_grp, jnp.exp(lg - emax), 0.0)
    ep = ee / jnp.sum(ee, axis=-1, keepdims=True)
    ep = jnp.where(in_grp, ep, -1.0)
    e1 = jnp.max(ep, axis=-1, keepdims=True)
    i1 = first_where(ep == e1)
    ep2 = jnp.where(lane == i1, -1.0, ep)
    e2 = jnp.max(ep2, axis=-1, keepdims=True)
    i2 = first_where(ep2 == e2)
    den = e1 + e2
    cw = (jnp.where(lane == i1, p_top * e1 / den, 0.0)
          + jnp.where(lane == i2, p_top * e2 / den, 0.0))
    cw8 = jnp.zeros_like(cw)
    for g in range(N_GROUPS):
        cw8 = cw8 + jnp.where(g_idx == g, pltpu.roll(cw, LANE - (N_GROUPS + EPG * g), 1), 0.0)
    cw8_ref[...] = cw8

    onehot = (lane == g_idx).astype(F32)
    ii = lax.broadcasted_iota(jnp.int32, (RT_TM, RT_TM), 0)
    jj = lax.broadcasted_iota(jnp.int32, (RT_TM, RT_TM), 1)
    earlier = (jj < ii).astype(BF16)
    prefix = jnp.dot(earlier, onehot.astype(BF16), preferred_element_type=F32)
    cnt = jnp.sum(onehot, axis=0, keepdims=True)
    pc = jnp.floor((cnt + (PIECE - 1.0)) * (1.0 / PIECE)) * PIECE
    lo = pltpu.roll(pc, 1, 1) + pltpu.roll(pc, 2, 1) + pltpu.roll(pc, 3, 1)
    dest = jnp.sum(onehot * (prefix + lo), axis=-1, keepdims=True)
    info_ref[...] = jnp.where(lane == 0, g_idx.astype(F32), jnp.where(lane == 1, dest, 0.0))
    pc_ref[...] = pc


def _route(logits):
    return pl.pallas_call(
        _route_kernel,
        out_shape=(jax.ShapeDtypeStruct((N_TOK, ROUTE_W), F32),
                   jax.ShapeDtypeStruct((N_TOK, ROUTE_W), F32),
                   jax.ShapeDtypeStruct((RT_NT, 1, ROUTE_W), F32)),
        grid=(RT_NT,),
        in_specs=[pl.BlockSpec((RT_TM, ROUTE_W), lambda i: (i, 0))],
        out_specs=(pl.BlockSpec((RT_TM, ROUTE_W), lambda i: (i, 0)),
                   pl.BlockSpec((RT_TM, ROUTE_W), lambda i: (i, 0)),
                   pl.BlockSpec((None, 1, ROUTE_W), lambda i: (i, 0, 0))),
        compiler_params=_cparams(("arbitrary",)),
        name="route",
    )(logits)


def _dispatch_plan(pc_arr):
    pc = pc_arr[:, 0, :N_GROUPS].astype(jnp.int32)
    seg_len = jnp.sum(pc, axis=0)
    seg_pad = (seg_len + (EX_TM - 1)) // EX_TM * EX_TM
    seg_end_pad = jnp.cumsum(seg_pad)
    seg_start = seg_end_pad - seg_pad
    chunk_off = seg_start[None, :] + jnp.cumsum(pc, axis=0) - pc
    tile_start = jnp.arange(EX_NT, dtype=jnp.int32) * EX_TM
    tile_group = jnp.minimum(jnp.sum((tile_start[:, None] >= seg_end_pad[None, :]).astype(jnp.int32), axis=1),
                             N_GROUPS - 1)
    tile_valid = (tile_start < (seg_start + seg_len)[tile_group]).astype(jnp.int32)
    return chunk_off.astype(jnp.int32), pc, tile_group.astype(jnp.int32), tile_valid


def _piece_copies(off_ref, pc_ref, tile, make):
    lo = 0
    for g in range(N_GROUPS):
        n = pc_ref[tile, g] // PIECE
        base = off_ref[tile, g]

        def body(j, carry, lo=lo, base=base):
            make(pl.multiple_of(lo + j * PIECE, PIECE), pl.multiple_of(base + j * PIECE, PIECE))
            return carry

        lax.fori_loop(0, n, body, 0)
        lo = lo + pc_ref[tile, g]


def _piece_count(pc_ref, tile):
    n = 0
    for g in range(N_GROUPS):
        n = n + pc_ref[tile, g] // PIECE
    return n


def _dispatch_kernel(off_ref, pc_ref, h_ref, info_ref, cw8_ref, xs_in, cws_in,
                     xs_out, cws_out, x_scr, c_scr, sem):
    del xs_in, cws_in
    i = pl.program_id(0)
    slot = i % 2
    dest = info_ref[...].T[1:2, :]
    row = lax.broadcasted_iota(jnp.int32, (R_LOC, RT_TM), 0).astype(F32)
    sel = row == dest
    x_scr[slot] = jnp.dot(sel.astype(BF16), h_ref[...], preferred_element_type=F32).astype(BF16)
    c_scr[slot] = jnp.dot(sel.astype(F32), cw8_ref[...], preferred_element_type=F32,
                          precision=lax.Precision.HIGHEST)

    def x_copy(s, src, dst):
        return pltpu.make_async_copy(x_scr.at[s, pl.ds(src, PIECE)], xs_out.at[pl.ds(dst, PIECE)],
                                     sem.at[0, s])

    def c_copy(s, src, dst):
        return pltpu.make_async_copy(c_scr.at[s, pl.ds(src, PIECE)], cws_out.at[pl.ds(dst, PIECE)],
                                     sem.at[1, s])

    def start(src, dst):
        x_copy(slot, src, dst).start()
        c_copy(slot, src, dst).start()

    _piece_copies(off_ref, pc_ref, i, start)

    def wait_tile(tile, s):
        def wait(j, carry):
            x_copy(s, 0, 0).wait()
            c_copy(s, 0, 0).wait()
            return carry

        lax.fori_loop(0, _piece_count(pc_ref, tile), wait, 0)

    @pl.when(i > 0)
    def _():
        wait_tile(i - 1, 1 - slot)

    @pl.when(i == RT_NT - 1)
    def _():
        wait_tile(i, slot)


def _dispatch(chunk_off, pc, h2, info, cw8):
    xs0 = jnp.zeros((R_MAX, D), BF16)
    cws0 = jnp.zeros((R_MAX, ROUTE_W), F32)
    grid_spec = pltpu.PrefetchScalarGridSpec(
        num_scalar_prefetch=2,
        grid=(RT_NT,),
        in_specs=[pl.BlockSpec((RT_TM, D), lambda i, o, p: (i, 0)),
                  pl.BlockSpec((RT_TM, ROUTE_W), lambda i, o, p: (i, 0)),
                  pl.BlockSpec((RT_TM, ROUTE_W), lambda i, o, p: (i, 0)),
                  pl.BlockSpec(memory_space=pl.ANY),
                  pl.BlockSpec(memory_space=pl.ANY)],
        out_specs=(pl.BlockSpec(memory_space=pl.ANY), pl.BlockSpec(memory_space=pl.ANY)),
        scratch_shapes=[pltpu.VMEM((2, R_LOC, D), BF16),
                        pltpu.VMEM((2, R_LOC, ROUTE_W), F32),
                        pltpu.SemaphoreType.DMA((2, 2))])
    return pl.pallas_call(
        _dispatch_kernel,
        out_shape=(jax.ShapeDtypeStruct((R_MAX, D), BF16),
                   jax.ShapeDtypeStruct((R_MAX, ROUTE_W), F32)),
        grid_spec=grid_spec,
        input_output_aliases={5: 0, 6: 1},
        compiler_params=_cparams(("arbitrary",)),
        name="dispatch",
    )(chunk_off, pc, h2, info, cw8, xs0, cws0)


def _expert_kernel(tg_ref, tv_ref, x_ref, cw_ref, wg_ref, wu_ref, wd_ref, y_ref):
    del tg_ref
    k = pl.program_id(0)

    @pl.when(tv_ref[k] == 0)
    def _():
        y_ref[...] = jnp.zeros_like(y_ref)

    @pl.when(tv_ref[k] != 0)
    def _():
        x = x_ref[...]
        cw = cw_ref[...]
        lane = lax.broadcasted_iota(jnp.int32, cw.shape, 1)
        acc = jnp.zeros(y_ref.shape, F32)
        for j in range(EPG):
            w = jnp.sum(jnp.where(lane == j, cw, 0.0), axis=-1, keepdims=True)
            a = (jax.nn.silu(jnp.dot(x, wg_ref[j], preferred_element_type=F32))
                 * jnp.dot(x, wu_ref[j], preferred_element_type=F32))
            acc = acc + jnp.dot((a * w).astype(BF16), wd_ref[j], preferred_element_type=F32)
        y_ref[...] = acc.astype(BF16)


def _experts(tile_group, tile_valid, xs, cws, wg, wu, wd):
    grid_spec = pltpu.PrefetchScalarGridSpec(
        num_scalar_prefetch=2,
        grid=(EX_NT,),
        in_specs=[pl.BlockSpec((EX_TM, D), lambda k, tg, tv: (k, 0)),
                  pl.BlockSpec((EX_TM, ROUTE_W), lambda k, tg, tv: (k, 0)),
                  pl.BlockSpec((EPG, D, FF), lambda k, tg, tv: (tg[k], 0, 0)),
                  pl.BlockSpec((EPG, D, FF), lambda k, tg, tv: (tg[k], 0, 0)),
                  pl.BlockSpec((EPG, FF, D), lambda k, tg, tv: (tg[k], 0, 0))],
        out_specs=pl.BlockSpec((EX_TM, D), lambda k, tg, tv: (k, 0)))
    return pl.pallas_call(
        _expert_kernel,
        out_shape=jax.ShapeDtypeStruct((R_MAX, D), BF16),
        grid_spec=grid_spec,
        compiler_params=_cparams(("arbitrary",)),
        name="experts",
    )(tile_group, tile_valid, xs, cws, wg, wu, wd)


CB_NPT = N_PROMPT // RT_TM


def _combine_kernel(off_ref, pc_ref, info_ref, x1_ref, mod_ref, g_ref, ys_hbm,
                    yp_ref, ysm_ref, stage, sem):
    i = pl.program_id(0)
    slot = i % 2

    def copy(s, dst, src):
        return pltpu.make_async_copy(ys_hbm.at[pl.ds(src, PIECE)], stage.at[s, pl.ds(dst, PIECE)],
                                     sem.at[s])

    def fetch(tile, s):
        _piece_copies(off_ref, pc_ref, tile, lambda loc, glob: copy(s, loc, glob).start())

    @pl.when(i == 0)
    def _():
        stage[...] = jnp.zeros_like(stage)
        fetch(0, 0)

    @pl.when(i + 1 < RT_NT)
    def _():
        fetch(i + 1, 1 - slot)

    def wait(j, carry):
        copy(slot, 0, 0).wait()
        return carry

    lax.fori_loop(0, _piece_count(pc_ref, i), wait, 0)

    dest = info_ref[...][:, 1:2]
    col = lax.broadcasted_iota(jnp.int32, (RT_TM, R_STAGE), 1).astype(F32)
    moe = jnp.dot((col == dest).astype(BF16), stage[slot], preferred_element_type=F32)
    mod = mod_ref[...]
    out = _rms(x1_ref[...] + mod[:, 5 * D:6 * D] * moe) * g_ref[...]

    @pl.when(i < CB_NPT)
    def _():
        yp_ref[...] = out

    @pl.when(i >= CB_NPT)
    def _():
        ysm_ref[...] = out


def _combine(chunk_off, pc, info, x1, mod3, fg, ys):
    npt = CB_NPT
    grid_spec = pltpu.PrefetchScalarGridSpec(
        num_scalar_prefetch=2,
        grid=(RT_NT,),
        in_specs=[pl.BlockSpec((RT_TM, ROUTE_W), lambda i, o, p: (i, 0)),
                  pl.BlockSpec((RT_TM, D), lambda i, o, p: (i, 0)),
                  pl.BlockSpec((None, 1, 6 * D), lambda i, o, p: (_mod_row(RT_TM)(i), 0, 0)),
                  pl.BlockSpec((1, D), lambda i, o, p: (0, 0)),
                  pl.BlockSpec(memory_space=pl.ANY)],
        out_specs=(pl.BlockSpec((RT_TM, D), lambda i, o, p: (jnp.minimum(i, npt - 1), 0)),
                   pl.BlockSpec((RT_TM, D), lambda i, o, p: (jnp.maximum(i - npt, 0), 0))),
        scratch_shapes=[pltpu.VMEM((2, R_STAGE, D), BF16),
                        pltpu.SemaphoreType.DMA((2,))])
    return pl.pallas_call(
        _combine_kernel,
        out_shape=(jax.ShapeDtypeStruct((N_PROMPT, D), F32),
                   jax.ShapeDtypeStruct((N_SAMPLE, D), F32)),
        grid_spec=grid_spec,
        compiler_params=_cparams(("arbitrary",)),
        name="combine",
    )(chunk_off, pc, info, x1, mod3, fg, ys)


def kernel(x_prompt, x_sample, cache_diff_k, cache_diff_v, state_ret_fwd, state_ret_bwd, c, c_ctx,
           w_ada, b_ada, norm1_g, norm2_g, w_in, ret_decay_fwd, ret_decay_bwd, ret_norm_g,
           diff_lambda_q1, diff_lambda_k1, diff_lambda_q2, diff_lambda_k2, diff_subln_g,
           w_ret_o, w_diff_o, w_o, router_group_w, router_group_b, router_expert_w, router_expert_b,
           moe_w_gate, moe_w_up, moe_w_down, final_norm_g):
    l = 0
    lam_init = 0.8 - 0.6 * math.exp(-0.3 * l)
    lam = (jnp.exp(jnp.sum(diff_lambda_q1[l].astype(F32) * diff_lambda_k1[l].astype(F32)))
           - jnp.exp(jnp.sum(diff_lambda_q2[l].astype(F32) * diff_lambda_k2[l].astype(F32))) + lam_init)
    attn_scal = jnp.stack([lam, jnp.float32(1.0 - lam_init)]).astype(F32)
    lg = jnp.stack([jax.nn.log_sigmoid(ret_decay_fwd[l].astype(F32)),
                    jax.nn.log_sigmoid(ret_decay_bwd[l].astype(F32))])

    xp = x_prompt.reshape(N_PROMPT, D)
    xs = x_sample.reshape(N_SAMPLE, D)
    cvec = jnp.concatenate([c_ctx[None, :], c, jnp.zeros((3, D), F32)], axis=0)
    mod3 = _modulation(cvec, w_ada[l], b_ada[l][None, :]).reshape(8, 1, 6 * D)

    proj, kt32, v32 = _inproj(xp, xs, mod3, norm1_g[l][None, :], w_in[l])

    ret_act, s_f, s_b = _retention(lg, proj, state_ret_fwd[:, l], state_ret_bwd[:, l],
                                   ret_norm_g[l][None, :])
    cos_t, sin_t = _rope_tables()
    cache_kt = jnp.transpose(cache_diff_k[:, l], (0, 2, 3, 4, 1)).reshape(4, D, PAST)
    diff_act = _attention(attn_scal, proj, cache_kt, cache_diff_v,
                          cos_t, sin_t, diff_subln_g[l][None, :])

    wr = jnp.concatenate([router_group_w[l], router_expert_w[l],
                          jnp.zeros((D, ROUTE_W - N_GROUPS - N_EXP), F32)], axis=1).astype(BF16)
    br = jnp.concatenate([router_group_b[l], router_expert_b[l],
                          jnp.zeros((ROUTE_W - N_GROUPS - N_EXP,), F32)])[None, :]
    x1, h2, logits = _outproj(ret_act, diff_act, proj, xp, xs, mod3, norm2_g[l][None, :],
                              w_ret_o[l].astype(BF16), w_diff_o[l].astype(BF16), w_o[l].astype(BF16),
                              wr, br)
    cw8, info, pc_arr = _route(logits)
    chunk_off, pc, tile_group, tile_valid = _dispatch_plan(pc_arr)
    xs_sorted, cw_sorted = _dispatch(chunk_off, pc, h2, info, cw8)
    y_sorted = _experts(tile_group, tile_valid, xs_sorted, cw_sorted,
                        moe_w_gate[l].astype(BF16), moe_w_up[l].astype(BF16), moe_w_down[l].astype(BF16))
    yp, ys = _combine(chunk_off, pc, info, x1, mod3, final_norm_g[None, :], y_sorted)

    return (yp.reshape(16, T_P, D), ys.reshape(4, T_S, D),
            jnp.transpose(kt32.reshape(16, DIFF_H, 2, DIFF_HD, T_P), (0, 4, 1, 2, 3))[:, None],
            v32.reshape(16, 1, T_P, DIFF_H, 2 * DIFF_HD),
            s_f.reshape(16, 1, RET_H, RET_KD, RET_VD), s_b.reshape(16, 1, RET_H, RET_KD, RET_VD))
```

```python
import functools
import math

import jax
import jax.numpy as jnp
from jax import lax
from jax.experimental import pallas as pl
from jax.experimental.pallas import tpu as pltpu

F32 = jnp.float32
BF16 = jnp.bfloat16

D = 1024
N_PROMPT = 16 * 256
N_SAMPLE = 4 * 1024
N_TOK = N_PROMPT + N_SAMPLE
T_P = 256
T_S = 1024
PAST = 512
GRID_W = 64
RET_H = 4
RET_KD = 128
RET_VD = 256
DIFF_H = 8
DIFF_HD = 64
ROPE_BASE = 10000.0
N_GROUPS = 4
EPG = 8
N_EXP = 32
FF = 256
EPS = 1e-6
IN_W = 8192
LANE = 128
N_CHUNK = IN_W // LANE
C_RQ, C_RK, C_RV, C_RG, C_DQ, C_DK, C_DV, C_GR, C_GD = 0, 4, 8, 16, 24, 32, 40, 48, 56
ROUTE_W = 128
SEQ_BLK = 1024
N_PBLK = N_PROMPT // SEQ_BLK
VMEM_LIMIT = 56 * 1024 * 1024


def _cparams(sem):
    return pltpu.CompilerParams(dimension_semantics=sem, vmem_limit_bytes=VMEM_LIMIT)


def _mod_row(tile_rows):
    def f(i):
        start = i * tile_rows
        return jnp.where(start < N_PROMPT, 0, 1 + (start - N_PROMPT) // T_S)
    return f


def _rms(x):
    return x * lax.rsqrt(jnp.mean(x * x, axis=-1, keepdims=True) + EPS)


def _mod_kernel(c_ref, w_ref, b_ref, o_ref):
    s = jax.nn.silu(c_ref[...])
    o_ref[...] = jnp.dot(s.astype(BF16), w_ref[...].astype(BF16),
                         preferred_element_type=F32) + b_ref[...]


def _modulation(cvec, w_ada, b_ada):
    tn = 1536
    return pl.pallas_call(
        _mod_kernel,
        out_shape=jax.ShapeDtypeStruct((8, 6 * D), F32),
        grid=(6 * D // tn,),
        in_specs=[pl.BlockSpec((8, D), lambda j: (0, 0)),
                  pl.BlockSpec((D, tn), lambda j: (0, j)),
                  pl.BlockSpec((1, tn), lambda j: (0, j))],
        out_specs=pl.BlockSpec((8, tn), lambda j: (0, j)),
        compiler_params=_cparams(("arbitrary",)),
        name="mod",
    )(cvec, w_ada, b_ada)


IP_TM = 512
IP_TN = 2048
IP_NPT = N_PROMPT // IP_TM


IP_KV_TILE = C_DK * LANE // IP_TN
IP_SPT = IP_TM // T_P


def _inproj_kernel(xp_ref, xs_ref, mod_ref, n1_ref, w_ref, proj_ref, kt_ref, v32_ref, w_scr):
    j = pl.program_id(0)
    i = pl.program_id(1)

    @pl.when(i == 0)
    def _():
        w_scr[...] = w_ref[...].astype(BF16)

    x = jnp.where(i < IP_NPT, xp_ref[...], xs_ref[...])
    mod = mod_ref[...]
    h = (_rms(x) * n1_ref[...] * (1.0 + mod[:, D:2 * D]) + mod[:, 0:D]).astype(BF16)
    acc = jnp.dot(h, w_scr[...], preferred_element_type=F32)
    for c in range(IP_TN // LANE):
        proj_ref[c] = acc[:, c * LANE:(c + 1) * LANE].astype(BF16)

    @pl.when(jnp.logical_and(j == IP_KV_TILE, i < IP_NPT))
    def _():
        for s in range(IP_SPT):
            kt_ref[s] = acc[s * T_P:(s + 1) * T_P, :D].T
        v32_ref[...] = acc[:, D:]


def _inproj(xp, xs, mod3, n1, w_in):
    npt = IP_NPT
    cpt = IP_TN // LANE

    def kv_tile(j, i):
        return jnp.where(j < IP_KV_TILE, 0, jnp.where(j == IP_KV_TILE, jnp.minimum(i, npt - 1), npt - 1))

    return pl.pallas_call(
        _inproj_kernel,
        out_shape=(jax.ShapeDtypeStruct((N_CHUNK, N_TOK, LANE), BF16),
                   jax.ShapeDtypeStruct((N_PROMPT // T_P, D, T_P), F32),
                   jax.ShapeDtypeStruct((N_PROMPT, D), F32)),
        grid=(IN_W // IP_TN, N_TOK // IP_TM),
        in_specs=[pl.BlockSpec((IP_TM, D), lambda j, i: (jnp.minimum(i, npt - 1), 0)),
                  pl.BlockSpec((IP_TM, D), lambda j, i: (jnp.maximum(i - npt, 0), 0)),
                  pl.BlockSpec((None, 1, 6 * D), lambda j, i: (_mod_row(IP_TM)(i), 0, 0)),
                  pl.BlockSpec((1, D), lambda j, i: (0, 0)),
                  pl.BlockSpec((D, IP_TN), lambda j, i: (0, j))],
        out_specs=(pl.BlockSpec((cpt, IP_TM, LANE), lambda j, i: (j, i, 0)),
                   pl.BlockSpec((IP_SPT, D, T_P), lambda j, i: (kv_tile(j, i), 0, 0)),
                   pl.BlockSpec((IP_TM, D), lambda j, i: (kv_tile(j, i), 0))),
        scratch_shapes=[pltpu.VMEM((D, IP_TN), BF16)],
        compiler_params=_cparams(("arbitrary", "arbitrary")),
        name="inproj",
    )(xp, xs, mod3, n1, w_in)


def _decay_mask(t, lgf, lgb):
    ii = lax.broadcasted_iota(jnp.int32, (t, t), 0)
    jj = lax.broadcasted_iota(jnp.int32, (t, t), 1)
    rel = (ii - jj).astype(F32)
    e = jnp.exp(jnp.where(rel >= 0.0, lgf, -lgb) * rel)
    return jnp.where(rel == 0.0, 2.0, e) * (RET_KD ** -0.5)


def _ret_kernel(lg_ref, q_ref, k_ref, v_ref, rg_ref, s0f_ref, s0b_ref, g_ref,
                o_ref, sf_ref, sb_ref, dmp_scr, dms_scr):
    h = pl.program_id(0)
    i = pl.program_id(1)
    lgf = lg_ref[0, h]
    lgb = lg_ref[1, h]
    gain = g_ref[...]
    nt = (((1,), (1,)), ((), ()))
    tn = (((0,), (0,)), ((), ()))

    def finish(o, rg):
        d = o - jnp.mean(o, axis=-1, keepdims=True)
        y = d * lax.rsqrt(jnp.mean(d * d, axis=-1, keepdims=True) + EPS) * gain
        return (jax.nn.silu(rg.astype(F32)) * y).astype(BF16)

    def vcat(r):
        return jnp.concatenate([v_ref[0, r, :], v_ref[1, r, :]], axis=1)

    def gcat(r):
        return jnp.concatenate([rg_ref[0, r, :], rg_ref[1, r, :]], axis=1)

    @pl.when(i == 0)
    def _():
        dmp_scr[...] = _decay_mask(T_P, lgf, lgb)

    @pl.when(i == N_PBLK)
    def _():
        dms_scr[...] = _decay_mask(T_S, lgf, lgb)

    @pl.when(i < N_PBLK)
    def _():
        dm = dmp_scr[...]
        t = lax.broadcasted_iota(jnp.int32, (T_P, 1), 0).astype(F32)
        kdf = jnp.exp(lgf * (T_P - 1.0 - t)) * (RET_KD ** -0.5)
        kdb = jnp.exp(lgb * t) * (RET_KD ** -0.5)
        for s in range(SEQ_BLK // T_P):
            r = slice(s * T_P, (s + 1) * T_P)
            q = q_ref[r, :]
            k = k_ref[r, :]
            v = vcat(r)
            sc = lax.dot_general(q, k, nt, preferred_element_type=F32)
            o = jnp.dot((sc * dm).astype(BF16), v, preferred_element_type=F32)
            o_ref[r, :] = finish(o, gcat(r))
            kf = k.astype(F32)
            sf_ref[s] = lax.dot_general((kf * kdf).astype(BF16), v, tn, preferred_element_type=F32)
            sb_ref[s] = lax.dot_general((kf * kdb).astype(BF16), v, tn, preferred_element_type=F32)

    @pl.when(i >= N_PBLK)
    def _():
        r = slice(0, T_S)
        t = lax.broadcasted_iota(jnp.int32, (T_S, 1), 0).astype(F32)
        q = q_ref[...]
        k = k_ref[...]
        v = vcat(r)
        sc = lax.dot_general(q, k, nt, preferred_element_type=F32)
        o = jnp.dot((sc * dms_scr[...]).astype(BF16), v, preferred_element_type=F32)
        qf = q.astype(F32)
        o = o + jnp.dot((qf * jnp.exp(lgf * (t + 1.0))).astype(BF16), s0f_ref[...].astype(BF16),
                        preferred_element_type=F32)
        o = o + jnp.dot((qf * jnp.exp(lgb * (T_S - t))).astype(BF16), s0b_ref[...].astype(BF16),
                        preferred_element_type=F32)
        o_ref[...] = finish(o, gcat(r))


def _retention(lg, proj, s0f, s0b, gnorm):
    nb = N_TOK // SEQ_BLK
    spb = SEQ_BLK // T_P
    smp = lambda i: jnp.maximum(i - N_PBLK, 0)
    pmt = lambda i: jnp.minimum(i, N_PBLK - 1)
    return pl.pallas_call(
        _ret_kernel,
        out_shape=(jax.ShapeDtypeStruct((N_TOK, RET_H * RET_VD), BF16),
                   jax.ShapeDtypeStruct((16, RET_H, RET_KD, RET_VD), F32),
                   jax.ShapeDtypeStruct((16, RET_H, RET_KD, RET_VD), F32)),
        grid=(RET_H, nb),
        in_specs=[pl.BlockSpec(memory_space=pltpu.SMEM),
                  pl.BlockSpec((None, SEQ_BLK, LANE), lambda h, i: (C_RQ + h, i, 0)),
                  pl.BlockSpec((None, SEQ_BLK, LANE), lambda h, i: (C_RK + h, i, 0)),
                  pl.BlockSpec((2, SEQ_BLK, LANE), lambda h, i: (C_RV // 2 + h, i, 0)),
                  pl.BlockSpec((2, SEQ_BLK, LANE), lambda h, i: (C_RG // 2 + h, i, 0)),
                  pl.BlockSpec((None, None, RET_KD, RET_VD), lambda h, i: (smp(i), h, 0, 0)),
                  pl.BlockSpec((None, None, RET_KD, RET_VD), lambda h, i: (smp(i), h, 0, 0)),
                  pl.BlockSpec((1, RET_VD), lambda h, i: (0, h))],
        out_specs=(pl.BlockSpec((SEQ_BLK, RET_VD), lambda h, i: (i, h)),
                   pl.BlockSpec((spb, None, RET_KD, RET_VD), lambda h, i: (pmt(i), h, 0, 0)),
                   pl.BlockSpec((spb, None, RET_KD, RET_VD), lambda h, i: (pmt(i), h, 0, 0))),
        scratch_shapes=[pltpu.VMEM((T_P, T_P), F32), pltpu.VMEM((T_S, T_S), F32)],
        compiler_params=_cparams(("arbitrary", "arbitrary")),
        name="ret",
    )(lg, proj, proj, proj, proj, s0f, s0b, gnorm)


ATT_TQ = 256


def _attn_kernel(sc_ref, q_ref, k_ref, v_ref, ckt_ref, cv_ref, cos_ref, sin_ref, g_ref,
                 o_ref, q_scr, k_scr, v_scr):
    i = pl.program_id(0)
    lam = sc_ref[0]
    out_scale = sc_ref[1]
    gain = g_ref[...] * out_scale
    lane = lax.broadcasted_iota(jnp.int32, (1, LANE), 1)
    first = lane < DIFF_HD
    nt = (((1,), (1,)), ((), ()))

    def halves(q):
        zero = jnp.zeros_like(q)
        return jnp.where(first, q, zero), jnp.where(first, zero, q)

    def weights(s):
        return jnp.exp(s - jnp.max(s, axis=-1, keepdims=True)).astype(BF16)

    def finish(of0, of1):
        o = of0[:, :LANE] / of0[:, LANE:] - lam * (of1[:, :LANE] / of1[:, LANE:])
        return (_rms(o) * gain).astype(BF16)

    @pl.when(i < N_PBLK)
    def _():
        n_seq = SEQ_BLK // T_P
        rows = [slice(s * T_P, (s + 1) * T_P) for s in range(n_seq)]
        q0, q1 = halves((q_ref[...].astype(F32) * (DIFF_HD ** -0.5)).astype(BF16))
        s0 = jnp.concatenate([lax.dot_general(q0[r], k_ref[r, :], nt, preferred_element_type=F32)
                              for r in rows], axis=0)
        s1 = jnp.concatenate([lax.dot_general(q1[r], k_ref[r, :], nt, preferred_element_type=F32)
                              for r in rows], axis=0)
        e0 = weights(s0)
        e1 = weights(s1)
        ones = jnp.ones((T_P, LANE), BF16)
        for r in rows:
            v1 = jnp.concatenate([v_ref[r, :], ones], axis=1)
            o_ref[r, :] = finish(jnp.dot(e0[r], v1, preferred_element_type=F32),
                                 jnp.dot(e1[r], v1, preferred_element_type=F32))

    @pl.when(i >= N_PBLK)
    def _():
        cos = cos_ref[...]
        sin = sin_ref[...]
        low = (lax.broadcasted_iota(jnp.int32, (T_S, LANE), 1) & 16) == 0

        def rope(x):
            xs = jnp.where(low, pltpu.roll(x, LANE - 16, 1), pltpu.roll(x, 16, 1))
            return x * cos + xs * sin

        q_scr[...] = (rope(q_ref[...].astype(F32)) * (DIFF_HD ** -0.5)).astype(BF16)
        k_scr[...] = rope(k_ref[...].astype(F32)).astype(BF16)
        ckt = ckt_ref[...].astype(BF16)
        v_scr[0:T_S, 0:LANE] = v_ref[...]
        v_scr[T_S:T_S + PAST, 0:LANE] = cv_ref[:, pl.program_id(1), :].astype(BF16)
        v_scr[:, LANE:2 * LANE] = jnp.ones((T_S + PAST, LANE), BF16)

        def scores(qh):
            return jnp.concatenate([lax.dot_general(qh, k_scr[...], nt, preferred_element_type=F32),
                                    jnp.dot(qh, ckt, preferred_element_type=F32)], axis=1)

        for b in range(T_S // ATT_TQ):
            r = slice(b * ATT_TQ, (b + 1) * ATT_TQ)
            q0, q1 = halves(q_scr[r, :])
            e0 = weights(scores(q0))
            e1 = weights(scores(q1))
            o_ref[r, :] = finish(jnp.dot(e0, v_scr[...], preferred_element_type=F32),
                                 jnp.dot(e1, v_scr[...], preferred_element_type=F32))


def _attention(scal, proj, cache_k, cache_v, cos_t, sin_t, subln_g):
    nb = N_TOK // SEQ_BLK
    smp = lambda i: jnp.maximum(i - N_PBLK, 0)
    return pl.pallas_call(
        _attn_kernel,
        out_shape=jax.ShapeDtypeStruct((N_TOK, DIFF_H * 2 * DIFF_HD), BF16),
        grid=(nb, DIFF_H),
        in_specs=[pl.BlockSpec(memory_space=pltpu.SMEM),
                  pl.BlockSpec((None, SEQ_BLK, LANE), lambda i, h: (C_DQ + h, i, 0)),
                  pl.BlockSpec((None, SEQ_BLK, LANE), lambda i, h: (C_DK + h, i, 0)),
                  pl.BlockSpec((None, SEQ_BLK, LANE), lambda i, h: (C_DV + h, i, 0)),
                  pl.BlockSpec((None, LANE, PAST), lambda i, h: (smp(i), h, 0)),
                  pl.BlockSpec((None, None, PAST, DIFF_H, LANE), lambda i, h: (smp(i), 0, 0, 0, 0)),
                  pl.BlockSpec((T_S, LANE), lambda i, h: (0, 0)),
                  pl.BlockSpec((T_S, LANE), lambda i, h: (0, 0)),
                  pl.BlockSpec((1, LANE), lambda i, h: (0, 0))],
        out_specs=pl.BlockSpec((SEQ_BLK, LANE), lambda i, h: (i, h)),
        scratch_shapes=[pltpu.VMEM((T_S, LANE), BF16),
                        pltpu.VMEM((T_S, LANE), BF16),
                        pltpu.VMEM((T_S + PAST, 2 * LANE), BF16)],
        compiler_params=_cparams(("arbitrary", "arbitrary")),
        name="attn",
    )(scal, proj, proj, proj, cache_k, cache_v, cos_t, sin_t, subln_g)


def _rope_tables():
    n_rows = T_S // GRID_W
    row = jnp.repeat(jnp.arange(n_rows), GRID_W).astype(F32)
    col = jnp.tile(jnp.arange(GRID_W), n_rows).astype(F32)
    n_freq = DIFF_HD // 4
    inv = ROPE_BASE ** (-jnp.arange(n_freq, dtype=F32) / n_freq)

    def axis_tables(pos):
        ang = pos[:, None] * inv[None, :]
        c = jnp.cos(ang)
        s = jnp.sin(ang)
        return jnp.concatenate([c, c], axis=-1), jnp.concatenate([-s, s], axis=-1)

    cr, sr = axis_tables(row)
    cc, sc = axis_tables(col)
    cos_h = jnp.concatenate([cr, cc], axis=-1)
    sin_h = jnp.concatenate([sr, sc], axis=-1)
    return jnp.concatenate([cos_h, cos_h], axis=-1), jnp.concatenate([sin_h, sin_h], axis=-1)


OP_TM = 512
OP_NPT = N_PROMPT // OP_TM


def _outproj_kernel(ra_ref, da_ref, gr_ref, gd_ref, xp_ref, xs_ref, mod_ref, n2_ref,
                    wro_ref, wdo_ref, wo_ref, wr_ref, br_ref,
                    x1_ref, h2_ref, cw8_ref, info_ref, pc_ref, m_scr):
    i = pl.program_id(0)
    ret_out = jnp.dot(ra_ref[...], wro_ref[...], preferred_element_type=F32)
    diff_out = jnp.dot(da_ref[...], wdo_ref[...], preferred_element_type=F32)
    for c in range(D // LANE):
        sl = slice(c * LANE, (c + 1) * LANE)
        m = (jax.nn.sigmoid(gr_ref[c].astype(F32)) * ret_out[:, sl]
             + jax.nn.sigmoid(gd_ref[c].astype(F32)) * diff_out[:, sl])
        m_scr[:, sl] = m.astype(BF16)
    mix = jnp.dot(m_scr[...], wo_ref[...], preferred_element_type=F32)
    mod = mod_ref[...]

    x1 = jnp.where(i < OP_NPT, xp_ref[...], xs_ref[...]) + mod[:, 2 * D:3 * D] * mix
    x1_ref[...] = x1
    h2 = (_rms(x1) * n2_ref[...] * (1.0 + mod[:, 4 * D:5 * D]) + mod[:, 3 * D:4 * D]).astype(BF16)
    h2_ref[...] = h2
    logits = jnp.dot(h2, wr_ref[...], preferred_element_type=F32) + br_ref[...]
    cw8_ref[...], info_ref[...], pc_ref[...] = _route_rows(logits)


def _outproj(ret_act, diff_act, proj, xp, xs, mod3, n2, wro, wdo, wo, wr, br):
    assert OP_TM == RT_TM
    npt = OP_NPT
    full = lambda i: (0, 0)
    return pl.pallas_call(
        _outproj_kernel,
        out_shape=(jax.ShapeDtypeStruct((N_TOK, D), F32),
                   jax.ShapeDtypeStruct((N_TOK, D), BF16),
                   jax.ShapeDtypeStruct((N_TOK, ROUTE_W), F32),
                   jax.ShapeDtypeStruct((N_TOK, ROUTE_W), F32),
                   jax.ShapeDtypeStruct((RT_NT, 1, ROUTE_W), F32)),
        grid=(N_TOK // OP_TM,),
        in_specs=[pl.BlockSpec((OP_TM, D), lambda i: (i, 0)),
                  pl.BlockSpec((OP_TM, D), lambda i: (i, 0)),
                  pl.BlockSpec((8, OP_TM, LANE), lambda i: (C_GR // 8, i, 0)),
                  pl.BlockSpec((8, OP_TM, LANE), lambda i: (C_GD // 8, i, 0)),
                  pl.BlockSpec((OP_TM, D), lambda i: (jnp.minimum(i, npt - 1), 0)),
                  pl.BlockSpec((OP_TM, D), lambda i: (jnp.maximum(i - npt, 0), 0)),
                  pl.BlockSpec((None, 1, 6 * D), lambda i: (_mod_row(OP_TM)(i), 0, 0)),
                  pl.BlockSpec((1, D), full),
                  pl.BlockSpec((D, D), full),
                  pl.BlockSpec((D, D), full),
                  pl.BlockSpec((D, D), full),
                  pl.BlockSpec((D, ROUTE_W), full),
                  pl.BlockSpec((1, ROUTE_W), full)],
        out_specs=(pl.BlockSpec((OP_TM, D), lambda i: (i, 0)),
                   pl.BlockSpec((OP_TM, D), lambda i: (i, 0)),
                   pl.BlockSpec((OP_TM, ROUTE_W), lambda i: (i, 0)),
                   pl.BlockSpec((OP_TM, ROUTE_W), lambda i: (i, 0)),
                   pl.BlockSpec((None, 1, ROUTE_W), lambda i: (i, 0, 0))),
        scratch_shapes=[pltpu.VMEM((OP_TM, D), BF16)],
        compiler_params=_cparams(("arbitrary",)),
        name="outproj",
    )(ret_act, diff_act, proj, proj, xp, xs, mod3, n2, wro, wdo, wo, wr, br)


RT_TM = 512
RT_NT = N_TOK // RT_TM
PIECE = 16
R_LOC = RT_TM + N_GROUPS * PIECE
R_STAGE = 640
EX_TM = 512
R_MAX = 11264
EX_NT = R_MAX // EX_TM


def _route_rows(lg):
    lane = lax.broadcasted_iota(jnp.int32, lg.shape, 1)
    neg = jnp.float32(-jnp.inf)
    big = jnp.int32(ROUTE_W)

    def first_where(cond):
        return jnp.min(jnp.where(cond, lane, big), axis=-1, keepdims=True)

    is_g = lane < N_GROUPS
    gl = jnp.where(is_g, lg, neg)
    gmax = jnp.max(gl, axis=-1, keepdims=True)
    gsum = jnp.sum(jnp.where(is_g, jnp.exp(lg - gmax), 0.0), axis=-1, keepdims=True)
    p_top = 1.0 / gsum
    g_idx = first_where(gl == gmax)
    in_grp = jnp.logical_and(lane >= N_GROUPS, ((lane - N_GROUPS) >> 3) == g_idx)
    el = jnp.where(in_grp, lg, neg)
    emax = jnp.max(el, axis=-1, keepdims=True)
    ee = jnp.where(in_grp, jnp.exp(lg - emax), 0.0)
    ep = ee / jnp.sum(ee, axis=-1, keepdims=True)
    ep = jnp.where(in_grp, ep, -1.0)
    e1 = jnp.max(ep, axis=-1, keepdims=True)
    i1 = first_where(ep == e1)
    ep2 = jnp.where(lane == i1, -1.0, ep)
    e2 = jnp.max(ep2, axis=-1, keepdims=True)
    i2 = first_where(ep2 == e2)
    den = e1 + e2
    cw = (jnp.where(lane == i1, p_top * e1 / den, 0.0)
          + jnp.where(lane == i2, p_top * e2 / den, 0.0))
    cw8 = jnp.zeros_like(cw)
    for g in range(N_GROUPS):
        cw8 = cw8 + jnp.where(g_idx == g, pltpu.roll(cw, LANE - (N_GROUPS + EPG * g), 1), 0.0)

    onehot = (lane == g_idx).astype(F32)
    ii = lax.broadcasted_iota(jnp.int32, (RT_TM, RT_TM), 0)
    jj = lax.broadcasted_iota(jnp.int32, (RT_TM, RT_TM), 1)
    earlier = (jj < ii).astype(BF16)
    prefix = jnp.dot(earlier, onehot.astype(BF16), preferred_element_type=F32)
    cnt = jnp.sum(onehot, axis=0, keepdims=True)
    pc = jnp.floor((cnt + (PIECE - 1.0)) * (1.0 / PIECE)) * PIECE
    lo = pltpu.roll(pc, 1, 1) + pltpu.roll(pc, 2, 1) + pltpu.roll(pc, 3, 1)
    dest = jnp.sum(onehot * (prefix + lo), axis=-1, keepdims=True)
    info = jnp.where(lane == 0, g_idx.astype(F32), jnp.where(lane == 1, dest, 0.0))
    return cw8, info, pc


def _dispatch_plan(pc_arr):
    pc = pc_arr[:, 0, :N_GROUPS].astype(jnp.int32)
    seg_len = jnp.sum(pc, axis=0)
    seg_pad = (seg_len + (EX_TM - 1)) // EX_TM * EX_TM
    seg_end_pad = jnp.cumsum(seg_pad)
    seg_start = seg_end_pad - seg_pad
    chunk_off = seg_start[None, :] + jnp.cumsum(pc, axis=0) - pc
    tile_start = jnp.arange(EX_NT, dtype=jnp.int32) * EX_TM
    tile_group = jnp.minimum(jnp.sum((tile_start[:, None] >= seg_end_pad[None, :]).astype(jnp.int32), axis=1),
                             N_GROUPS - 1)
    tile_valid = (tile_start < (seg_start + seg_len)[tile_group]).astype(jnp.int32)
    return chunk_off.astype(jnp.int32), pc, tile_group.astype(jnp.int32), tile_valid


def _piece_copies(off_ref, pc_ref, tile, make):
    lo = 0
    for g in range(N_GROUPS):
        n = pc_ref[tile, g] // PIECE
        base = off_ref[tile, g]

        def body(j, carry, lo=lo, base=base):
            make(pl.multiple_of(lo + j * PIECE, PIECE), pl.multiple_of(base + j * PIECE, PIECE))
            return carry

        lax.fori_loop(0, n, body, 0)
        lo = lo + pc_ref[tile, g]


def _piece_count(pc_ref, tile):
    n = 0
    for g in range(N_GROUPS):
        n = n + pc_ref[tile, g] // PIECE
    return n


def _dispatch_kernel(off_ref, pc_ref, h_ref, info_ref, cw8_ref, xs_in, cws_in,
                     xs_out, cws_out, x_scr, c_scr, sem):
    del xs_in, cws_in
    i = pl.program_id(0)
    slot = i % 2
    dest = info_ref[...].T[1:2, :]
    row = lax.broadcasted_iota(jnp.int32, (R_LOC, RT_TM), 0).astype(F32)
    sel = row == dest
    sel = sel.astype(BF16)
    x_scr[slot] = jnp.dot(sel, h_ref[...], preferred_element_type=F32).astype(BF16)
    cw = cw8_ref[...]
    hi = cw.astype(BF16).astype(F32)
    mid = (cw - hi).astype(BF16).astype(F32)
    low = (cw - hi - mid).astype(BF16).astype(F32)
    pieces = (hi + pltpu.roll(mid, EPG, 1) + pltpu.roll(low, 2 * EPG, 1)).astype(BF16)
    c_scr[slot] = jnp.dot(sel, pieces, preferred_element_type=F32)

    def x_copy(s, src, dst):
        return pltpu.make_async_copy(x_scr.at[s, pl.ds(src, PIECE)], xs_out.at[pl.ds(dst, PIECE)],
                                     sem.at[0, s])

    def c_copy(s, src, dst):
        return pltpu.make_async_copy(c_scr.at[s, pl.ds(src, PIECE)], cws_out.at[pl.ds(dst, PIECE)],
                                     sem.at[1, s])

    def start(src, dst):
        x_copy(slot, src, dst).start()
        c_copy(slot, src, dst).start()

    _piece_copies(off_ref, pc_ref, i, start)

    def wait_tile(tile, s):
        def wait(j, carry):
            x_copy(s, 0, 0).wait()
            c_copy(s, 0, 0).wait()
            return carry

        lax.fori_loop(0, _piece_count(pc_ref, tile), wait, 0)

    @pl.when(i > 0)
    def _():
        wait_tile(i - 1, 1 - slot)

    @pl.when(i == RT_NT - 1)
    def _():
        wait_tile(i, slot)


def _dispatch(chunk_off, pc, h2, info, cw8):
    xs0 = jnp.zeros((R_MAX, D), BF16)
    cws0 = jnp.zeros((R_MAX, ROUTE_W), F32)
    grid_spec = pltpu.PrefetchScalarGridSpec(
        num_scalar_prefetch=2,
        grid=(RT_NT,),
        in_specs=[pl.BlockSpec((RT_TM, D), lambda i, o, p: (i, 0)),
                  pl.BlockSpec((RT_TM, ROUTE_W), lambda i, o, p: (i, 0)),
                  pl.BlockSpec((RT_TM, ROUTE_W), lambda i, o, p: (i, 0)),
                  pl.BlockSpec(memory_space=pl.ANY),
                  pl.BlockSpec(memory_space=pl.ANY)],
        out_specs=(pl.BlockSpec(memory_space=pl.ANY), pl.BlockSpec(memory_space=pl.ANY)),
        scratch_shapes=[pltpu.VMEM((2, R_LOC, D), BF16),
                        pltpu.VMEM((2, R_LOC, ROUTE_W), F32),
                        pltpu.SemaphoreType.DMA((2, 2))])
    return pl.pallas_call(
        _dispatch_kernel,
        out_shape=(jax.ShapeDtypeStruct((R_MAX, D), BF16),
                   jax.ShapeDtypeStruct((R_MAX, ROUTE_W), F32)),
        grid_spec=grid_spec,
        input_output_aliases={5: 0, 6: 1},
        compiler_params=_cparams(("arbitrary",)),
        name="dispatch",
    )(chunk_off, pc, h2, info, cw8, xs0, cws0)


def _expert_kernel(tg_ref, tv_ref, x_ref, cw_ref, wg_ref, wu_ref, wd_ref, y_ref):
    del tg_ref
    k = pl.program_id(0)

    @pl.when(tv_ref[k] == 0)
    def _():
        y_ref[...] = jnp.zeros_like(y_ref)

    @pl.when(tv_ref[k] != 0)
    def _():
        x = x_ref[...]
        cw = cw_ref[...]
        lane = lax.broadcasted_iota(jnp.int32, cw.shape, 1)
        acc = jnp.zeros(y_ref.shape, F32)
        for j in range(EPG):
            mine = jnp.logical_and((lane & (EPG - 1)) == j, lane < 3 * EPG)
            w = jnp.sum(jnp.where(mine, cw, 0.0), axis=-1, keepdims=True)
            a = (jax.nn.silu(jnp.dot(x, wg_ref[j], preferred_element_type=F32))
                 * jnp.dot(x, wu_ref[j], preferred_element_type=F32))
            acc = acc + jnp.dot((a * w).astype(BF16), wd_ref[j], preferred_element_type=F32)
        y_ref[...] = acc.astype(BF16)


def _experts(tile_group, tile_valid, xs, cws, wg, wu, wd):
    grid_spec = pltpu.PrefetchScalarGridSpec(
        num_scalar_prefetch=2,
        grid=(EX_NT,),
        in_specs=[pl.BlockSpec((EX_TM, D), lambda k, tg, tv: (k, 0)),
                  pl.BlockSpec((EX_TM, ROUTE_W), lambda k, tg, tv: (k, 0)),
                  pl.BlockSpec((EPG, D, FF), lambda k, tg, tv: (tg[k], 0, 0)),
                  pl.BlockSpec((EPG, D, FF), lambda k, tg, tv: (tg[k], 0, 0)),
                  pl.BlockSpec((EPG, FF, D), lambda k, tg, tv: (tg[k], 0, 0))],
        out_specs=pl.BlockSpec((EX_TM, D), lambda k, tg, tv: (k, 0)))
    return pl.pallas_call(
        _expert_kernel,
        out_shape=jax.ShapeDtypeStruct((R_MAX, D), BF16),
        grid_spec=grid_spec,
        compiler_params=_cparams(("arbitrary",)),
        name="experts",
    )(tile_group, tile_valid, xs, cws, wg, wu, wd)


CB_NPT = N_PROMPT // RT_TM


def _combine_kernel(off_ref, pc_ref, info_ref, x1_ref, mod_ref, g_ref, ys_hbm,
                    yp_ref, ysm_ref, stage, sem):
    i = pl.program_id(0)
    slot = i % 2

    def copy(s, dst, src):
        return pltpu.make_async_copy(ys_hbm.at[pl.ds(src, PIECE)], stage.at[s, pl.ds(dst, PIECE)],
                                     sem.at[s])

    def fetch(tile, s):
        _piece_copies(off_ref, pc_ref, tile, lambda loc, glob: copy(s, loc, glob).start())

    @pl.when(i == 0)
    def _():
        stage[...] = jnp.zeros_like(stage)
        fetch(0, 0)

    @pl.when(i + 1 < RT_NT)
    def _():
        fetch(i + 1, 1 - slot)

    def wait(j, carry):
        copy(slot, 0, 0).wait()
        return carry

    lax.fori_loop(0, _piece_count(pc_ref, i), wait, 0)

    dest = info_ref[...][:, 1:2]
    col = lax.broadcasted_iota(jnp.int32, (RT_TM, R_STAGE), 1).astype(F32)
    moe = jnp.dot((col == dest).astype(BF16), stage[slot], preferred_element_type=F32)
    mod = mod_ref[...]
    out = _rms(x1_ref[...] + mod[:, 5 * D:6 * D] * moe) * g_ref[...]

    @pl.when(i < CB_NPT)
    def _():
        yp_ref[...] = out

    @pl.when(i >= CB_NPT)
    def _():
        ysm_ref[...] = out


def _combine(chunk_off, pc, info, x1, mod3, fg, ys):
    npt = CB_NPT
    grid_spec = pltpu.PrefetchScalarGridSpec(
        num_scalar_prefetch=2,
        grid=(RT_NT,),
        in_specs=[pl.BlockSpec((RT_TM, ROUTE_W), lambda i, o, p: (i, 0)),
                  pl.BlockSpec((RT_TM, D), lambda i, o, p: (i, 0)),
                  pl.BlockSpec((None, 1, 6 * D), lambda i, o, p: (_mod_row(RT_TM)(i), 0, 0)),
                  pl.BlockSpec((1, D), lambda i, o, p: (0, 0)),
                  pl.BlockSpec(memory_space=pl.ANY)],
        out_specs=(pl.BlockSpec((RT_TM, D), lambda i, o, p: (jnp.minimum(i, npt - 1), 0)),
                   pl.BlockSpec((RT_TM, D), lambda i, o, p: (jnp.maximum(i - npt, 0), 0))),
        scratch_shapes=[pltpu.VMEM((2, R_STAGE, D), BF16),
                        pltpu.SemaphoreType.DMA((2,))])
    return pl.pallas_call(
        _combine_kernel,
        out_shape=(jax.ShapeDtypeStruct((N_PROMPT, D), F32),
                   jax.ShapeDtypeStruct((N_SAMPLE, D), F32)),
        grid_spec=grid_spec,
        compiler_params=_cparams(("arbitrary",)),
        name="combine",
    )(chunk_off, pc, info, x1, mod3, fg, ys)


def kernel(x_prompt, x_sample, cache_diff_k, cache_diff_v, state_ret_fwd, state_ret_bwd, c, c_ctx,
           w_ada, b_ada, norm1_g, norm2_g, w_in, ret_decay_fwd, ret_decay_bwd, ret_norm_g,
           diff_lambda_q1, diff_lambda_k1, diff_lambda_q2, diff_lambda_k2, diff_subln_g,
           w_ret_o, w_diff_o, w_o, router_group_w, router_group_b, router_expert_w, router_expert_b,
           moe_w_gate, moe_w_up, moe_w_down, final_norm_g):
    l = 0
    lam_init = 0.8 - 0.6 * math.exp(-0.3 * l)
    lam = (jnp.exp(jnp.sum(diff_lambda_q1[l].astype(F32) * diff_lambda_k1[l].astype(F32)))
           - jnp.exp(jnp.sum(diff_lambda_q2[l].astype(F32) * diff_lambda_k2[l].astype(F32))) + lam_init)
    attn_scal = jnp.stack([lam, jnp.float32(1.0 - lam_init)]).astype(F32)
    lg = jnp.stack([jax.nn.log_sigmoid(ret_decay_fwd[l].astype(F32)),
                    jax.nn.log_sigmoid(ret_decay_bwd[l].astype(F32))])

    xp = x_prompt.reshape(N_PROMPT, D)
    xs = x_sample.reshape(N_SAMPLE, D)
    cvec = jnp.concatenate([c_ctx[None, :], c, jnp.zeros((3, D), F32)], axis=0)
    mod3 = _modulation(cvec, w_ada[l], b_ada[l][None, :]).reshape(8, 1, 6 * D)

    proj, kt32, v32 = _inproj(xp, xs, mod3, norm1_g[l][None, :], w_in[l])

    ret_act, s_f, s_b = _retention(lg, proj, state_ret_fwd[:, l], state_ret_bwd[:, l],
                                   ret_norm_g[l][None, :])
    cos_t, sin_t = _rope_tables()
    cache_kt = jnp.transpose(cache_diff_k[:, l], (0, 2, 3, 4, 1)).reshape(4, D, PAST)
    diff_act = _attention(attn_scal, proj, cache_kt, cache_diff_v,
                          cos_t, sin_t, diff_subln_g[l][None, :])

    wr = jnp.concatenate([router_group_w[l], router_expert_w[l],
                          jnp.zeros((D, ROUTE_W - N_GROUPS - N_EXP), F32)], axis=1).astype(BF16)
    br = jnp.concatenate([router_group_b[l], router_expert_b[l],
                          jnp.zeros((ROUTE_W - N_GROUPS - N_EXP,), F32)])[None, :]
    x1, h2, cw8, info, pc_arr = _outproj(ret_act, diff_act, proj, xp, xs, mod3, norm2_g[l][None, :],
                                         w_ret_o[l].astype(BF16), w_diff_o[l].astype(BF16),
                                         w_o[l].astype(BF16), wr, br)
    chunk_off, pc, tile_group, tile_valid = _dispatch_plan(pc_arr)
    xs_sorted, cw_sorted = _dispatch(chunk_off, pc, h2, info, cw8)
    y_sorted = _experts(tile_group, tile_valid, xs_sorted, cw_sorted,
                        moe_w_gate[l].astype(BF16), moe_w_up[l].astype(BF16), moe_w_down[l].astype(BF16))
    yp, ys = _combine(chunk_off, pc, info, x1, mod3, final_norm_g[None, :], y_sorted)

    return (yp.reshape(16, T_P, D), ys.reshape(4, T_S, D),
            jnp.transpose(kt32.reshape(16, DIFF_H, 2, DIFF_HD, T_P), (0, 4, 1, 2, 3))[:, None],
            v32.reshape(16, 1, T_P, DIFF_H, 2 * DIFF_HD),
            s_f.reshape(16, 1, RET_H, RET_KD, RET_VD), s_b.reshape(16, 1, RET_H, RET_KD, RET_VD))
```

```python
import functools
import math

import jax
import jax.numpy as jnp
from jax import lax
from jax.experimental import pallas as pl
from jax.experimental.pallas import tpu as pltpu

F32 = jnp.float32
BF16 = jnp.bfloat16

D = 1024
N_PROMPT = 16 * 256
N_SAMPLE = 4 * 1024
N_TOK = N_PROMPT + N_SAMPLE
T_P = 256
T_S = 1024
PAST = 512
GRID_W = 64
RET_H = 4
RET_KD = 128
RET_VD = 256
DIFF_H = 8
DIFF_HD = 64
ROPE_BASE = 10000.0
N_GROUPS = 4
EPG = 8
N_EXP = 32
FF = 256
EPS = 1e-6
IN_W = 8192
LANE = 128
N_CHUNK = IN_W // LANE
C_RQ, C_RK, C_RV, C_RG, C_DQ, C_DK, C_DV, C_GR, C_GD = 0, 4, 8, 16, 24, 32, 40, 48, 56
ROUTE_W = 128
SEQ_BLK = 1024
N_PBLK = N_PROMPT // SEQ_BLK
VMEM_LIMIT = 56 * 1024 * 1024


def _cparams(sem):
    return pltpu.CompilerParams(dimension_semantics=sem, vmem_limit_bytes=VMEM_LIMIT)


def _mod_row(tile_rows):
    def f(i):
        start = i * tile_rows
        return jnp.where(start < N_PROMPT, 0, 1 + (start - N_PROMPT) // T_S)
    return f


def _rms(x):
    return x * lax.rsqrt(jnp.mean(x * x, axis=-1, keepdims=True) + EPS)


def _mod_kernel(c_ref, w_ref, b_ref, o_ref):
    s = jax.nn.silu(c_ref[...])
    o_ref[...] = jnp.dot(s.astype(BF16), w_ref[...].astype(BF16),
                         preferred_element_type=F32) + b_ref[...]


def _modulation(cvec, w_ada, b_ada):
    tn = 1536
    return pl.pallas_call(
        _mod_kernel,
        out_shape=jax.ShapeDtypeStruct((8, 6 * D), F32),
        grid=(6 * D // tn,),
        in_specs=[pl.BlockSpec((8, D), lambda j: (0, 0)),
                  pl.BlockSpec((D, tn), lambda j: (0, j)),
                  pl.BlockSpec((1, tn), lambda j: (0, j))],
        out_specs=pl.BlockSpec((8, tn), lambda j: (0, j)),
        compiler_params=_cparams(("arbitrary",)),
        name="mod",
    )(cvec, w_ada, b_ada)


IP_TM = 512
IP_TN = 2048
IP_NPT = N_PROMPT // IP_TM


IP_KV_TILE = C_DK * LANE // IP_TN
IP_SPT = IP_TM // T_P


IP_STEPS = (IN_W // IP_TN) * (N_TOK // IP_TM)
N_SIDE = 6


def _inproj_kernel(xp_ref, xs_ref, mod_ref, n1_ref, w_ref, *rest):
    side_in = rest[:N_SIDE]
    proj_ref, kt_ref, v32_ref = rest[N_SIDE:N_SIDE + 3]
    side_out = rest[N_SIDE + 3:2 * N_SIDE + 3]
    xs0_ref, cws0_ref, w_scr = rest[2 * N_SIDE + 3:]
    j = pl.program_id(0)
    i = pl.program_id(1)


    @pl.when(i == 0)
    def _():
        w_scr[...] = w_ref[...].astype(BF16)

    x = jnp.where(i < IP_NPT, xp_ref[...], xs_ref[...])
    mod = mod_ref[...]
    h = (_rms(x) * n1_ref[...] * (1.0 + mod[:, D:2 * D]) + mod[:, 0:D]).astype(BF16)
    acc = jnp.dot(h, w_scr[...], preferred_element_type=F32)
    for c in range(IP_TN // LANE):
        proj_ref[c] = acc[:, c * LANE:(c + 1) * LANE].astype(BF16)

    for src, dst in zip(side_in, side_out):
        dst[...] = src[...].astype(BF16)
    xs0_ref[...] = jnp.zeros_like(xs0_ref)
    cws0_ref[...] = jnp.zeros_like(cws0_ref)

    @pl.when(jnp.logical_and(j == IP_KV_TILE, i < IP_NPT))
    def _():
        for s in range(IP_SPT):
            kt_ref[s] = acc[s * T_P:(s + 1) * T_P, :D].T
        v32_ref[...] = acc[:, D:]


def _inproj(xp, xs, mod3, n1, w_in, wg, wu, wd, wro, wdo, wo):
    npt = IP_NPT
    cpt = IP_TN // LANE
    n_i = N_TOK // IP_TM

    def kv_tile(j, i):
        return jnp.where(j < IP_KV_TILE, 0, jnp.where(j == IP_KV_TILE, jnp.minimum(i, npt - 1), npt - 1))

    step = lambda j, i: j * n_i + i
    per_exp = IP_STEPS // N_EXP
    exp_slice = lambda j, i: (step(j, i) // per_exp, step(j, i) % per_exp, 0)
    row_slice = lambda j, i: (step(j, i), 0)
    up_spec = pl.BlockSpec((None, D // per_exp, FF), exp_slice)
    down_spec = pl.BlockSpec((None, FF // per_exp, D), exp_slice)
    sq_spec = pl.BlockSpec((D // IP_STEPS, D), row_slice)
    side_specs = [up_spec, up_spec, down_spec, sq_spec, sq_spec, sq_spec]
    side_shapes = [jax.ShapeDtypeStruct(a.shape, BF16) for a in (wg, wu, wd, wro, wdo, wo)]
    zrows = R_MAX // IP_STEPS

    return pl.pallas_call(
        _inproj_kernel,
        out_shape=(jax.ShapeDtypeStruct((N_CHUNK, N_TOK, LANE), BF16),
                   jax.ShapeDtypeStruct((N_PROMPT // T_P, D, T_P), F32),
                   jax.ShapeDtypeStruct((N_PROMPT, D), F32),
                   *side_shapes,
                   jax.ShapeDtypeStruct((R_MAX, D), BF16),
                   jax.ShapeDtypeStruct((R_MAX, ROUTE_W), F32)),
        grid=(IN_W // IP_TN, n_i),
        in_specs=[pl.BlockSpec((IP_TM, D), lambda j, i: (jnp.minimum(i, npt - 1), 0)),
                  pl.BlockSpec((IP_TM, D), lambda j, i: (jnp.maximum(i - npt, 0), 0)),
                  pl.BlockSpec((None, 1, 6 * D), lambda j, i: (_mod_row(IP_TM)(i), 0, 0)),
                  pl.BlockSpec((1, D), lambda j, i: (0, 0)),
                  pl.BlockSpec((D, IP_TN), lambda j, i: (0, j)),
                  *side_specs],
        out_specs=(pl.BlockSpec((cpt, IP_TM, LANE), lambda j, i: (j, i, 0)),
                   pl.BlockSpec((IP_SPT, D, T_P), lambda j, i: (kv_tile(j, i), 0, 0)),
                   pl.BlockSpec((IP_TM, D), lambda j, i: (kv_tile(j, i), 0)),
                   *side_specs,
                   pl.BlockSpec((zrows, D), row_slice),
                   pl.BlockSpec((zrows, ROUTE_W), row_slice)),
        scratch_shapes=[pltpu.VMEM((D, IP_TN), BF16)],
        compiler_params=_cparams(("arbitrary", "arbitrary")),
        name="inproj",
    )(xp, xs, mod3, n1, w_in, wg, wu, wd, wro, wdo, wo)


def _decay_mask(t, lgf, lgb):
    ii = lax.broadcasted_iota(jnp.int32, (t, t), 0)
    jj = lax.broadcasted_iota(jnp.int32, (t, t), 1)
    rel = (ii - jj).astype(F32)
    e = jnp.exp(jnp.where(rel >= 0.0, lgf, -lgb) * rel)
    return jnp.where(rel == 0.0, 2.0, e) * (RET_KD ** -0.5)


def _ret_kernel(lg_ref, q_ref, k_ref, v_ref, rg_ref, s0f_ref, s0b_ref, g_ref,
                o_ref, sf_ref, sb_ref, dmp_scr, dms_scr):
    h = pl.program_id(0)
    i = pl.program_id(1)
    lgf = lg_ref[0, h]
    lgb = lg_ref[1, h]
    gain = g_ref[...]
    nt = (((1,), (1,)), ((), ()))
    tn = (((0,), (0,)), ((), ()))

    def finish(o, rg):
        d = o - jnp.mean(o, axis=-1, keepdims=True)
        y = d * lax.rsqrt(jnp.mean(d * d, axis=-1, keepdims=True) + EPS) * gain
        return (jax.nn.silu(rg.astype(F32)) * y).astype(BF16)

    def vcat(r):
        return jnp.concatenate([v_ref[0, r, :], v_ref[1, r, :]], axis=1)

    def gcat(r):
        return jnp.concatenate([rg_ref[0, r, :], rg_ref[1, r, :]], axis=1)

    @pl.when(i == 0)
    def _():
        dmp_scr[...] = _decay_mask(T_P, lgf, lgb)

    @pl.when(i == N_PBLK)
    def _():
        dms_scr[...] = _decay_mask(T_S, lgf, lgb)

    @pl.when(i < N_PBLK)
    def _():
        dm = dmp_scr[...]
        t = lax.broadcasted_iota(jnp.int32, (T_P, 1), 0).astype(F32)
        kdf = jnp.exp(lgf * (T_P - 1.0 - t)) * (RET_KD ** -0.5)
        kdb = jnp.exp(lgb * t) * (RET_KD ** -0.5)
        for s in range(SEQ_BLK // T_P):
            r = slice(s * T_P, (s + 1) * T_P)
            q = q_ref[r, :]
            k = k_ref[r, :]
            v = vcat(r)
            sc = lax.dot_general(q, k, nt, preferred_element_type=F32)
            o = jnp.dot((sc * dm).astype(BF16), v, preferred_element_type=F32)
            o_ref[r, :] = finish(o, gcat(r))
            kf = k.astype(F32)
            sf_ref[s] = lax.dot_general((kf * kdf).astype(BF16), v, tn, preferred_element_type=F32)
            sb_ref[s] = lax.dot_general((kf * kdb).astype(BF16), v, tn, preferred_element_type=F32)

    @pl.when(i >= N_PBLK)
    def _():
        r = slice(0, T_S)
        t = lax.broadcasted_iota(jnp.int32, (T_S, 1), 0).astype(F32)
        q = q_ref[...]
        k = k_ref[...]
        v = vcat(r)
        sc = lax.dot_general(q, k, nt, preferred_element_type=F32)
        o = jnp.dot((sc * dms_scr[...]).astype(BF16), v, preferred_element_type=F32)
        qf = q.astype(F32)
        o = o + jnp.dot((qf * jnp.exp(lgf * (t + 1.0))).astype(BF16), s0f_ref[...].astype(BF16),
                        preferred_element_type=F32)
        o = o + jnp.dot((qf * jnp.exp(lgb * (T_S - t))).astype(BF16), s0b_ref[...].astype(BF16),
                        preferred_element_type=F32)
        o_ref[...] = finish(o, gcat(r))


def _retention(lg, proj, s0f, s0b, gnorm):
    nb = N_TOK // SEQ_BLK
    spb = SEQ_BLK // T_P
    smp = lambda i: jnp.maximum(i - N_PBLK, 0)
    pmt = lambda i: jnp.minimum(i, N_PBLK - 1)
    return pl.pallas_call(
        _ret_kernel,
        out_shape=(jax.ShapeDtypeStruct((N_TOK, RET_H * RET_VD), BF16),
                   jax.ShapeDtypeStruct((16, RET_H, RET_KD, RET_VD), F32),
                   jax.ShapeDtypeStruct((16, RET_H, RET_KD, RET_VD), F32)),
        grid=(RET_H, nb),
        in_specs=[pl.BlockSpec(memory_space=pltpu.SMEM),
                  pl.BlockSpec((None, SEQ_BLK, LANE), lambda h, i: (C_RQ + h, i, 0)),
                  pl.BlockSpec((None, SEQ_BLK, LANE), lambda h, i: (C_RK + h, i, 0)),
                  pl.BlockSpec((2, SEQ_BLK, LANE), lambda h, i: (C_RV // 2 + h, i, 0)),
                  pl.BlockSpec((2, SEQ_BLK, LANE), lambda h, i: (C_RG // 2 + h, i, 0)),
                  pl.BlockSpec((None, None, RET_KD, RET_VD), lambda h, i: (smp(i), h, 0, 0)),
                  pl.BlockSpec((None, None, RET_KD, RET_VD), lambda h, i: (smp(i), h, 0, 0)),
                  pl.BlockSpec((1, RET_VD), lambda h, i: (0, h))],
        out_specs=(pl.BlockSpec((SEQ_BLK, RET_VD), lambda h, i: (i, h)),
                   pl.BlockSpec((spb, None, RET_KD, RET_VD), lambda h, i: (pmt(i), h, 0, 0)),
                   pl.BlockSpec((spb, None, RET_KD, RET_VD), lambda h, i: (pmt(i), h, 0, 0))),
        scratch_shapes=[pltpu.VMEM((T_P, T_P), F32), pltpu.VMEM((T_S, T_S), F32)],
        compiler_params=_cparams(("arbitrary", "arbitrary")),
        name="ret",
    )(lg, proj, proj, proj, proj, s0f, s0b, gnorm)


ATT_TQ = 256


def _attn_kernel(sc_ref, q_ref, k_ref, v_ref, ckt_ref, cv_ref, cos_ref, sin_ref, g_ref,
                 o_ref, q_scr, k_scr, v_scr):
    i = pl.program_id(0)
    lam = sc_ref[0]
    out_scale = sc_ref[1]
    gain = g_ref[...] * out_scale
    lane = lax.broadcasted_iota(jnp.int32, (1, LANE), 1)
    first = lane < DIFF_HD
    nt = (((1,), (1,)), ((), ()))

    def halves(q):
        zero = jnp.zeros_like(q)
        return jnp.where(first, q, zero), jnp.where(first, zero, q)

    def weights(s):
        return jnp.exp(s - jnp.max(s, axis=-1, keepdims=True)).astype(BF16)

    def finish(of0, of1):
        o = of0[:, :LANE] / of0[:, LANE:] - lam * (of1[:, :LANE] / of1[:, LANE:])
        return (_rms(o) * gain).astype(BF16)

    @pl.when(i < N_PBLK)
    def _():
        n_seq = SEQ_BLK // T_P
        rows = [slice(s * T_P, (s + 1) * T_P) for s in range(n_seq)]
        q0, q1 = halves((q_ref[...].astype(F32) * (DIFF_HD ** -0.5)).astype(BF16))
        s0 = jnp.concatenate([lax.dot_general(q0[r], k_ref[r, :], nt, preferred_element_type=F32)
                              for r in rows], axis=0)
        s1 = jnp.concatenate([lax.dot_general(q1[r], k_ref[r, :], nt, preferred_element_type=F32)
                              for r in rows], axis=0)
        e0 = weights(s0)
        e1 = weights(s1)
        ones = jnp.ones((T_P, LANE), BF16)
        for r in rows:
            v1 = jnp.concatenate([v_ref[r, :], ones], axis=1)
            o_ref[r, :] = finish(jnp.dot(e0[r], v1, preferred_element_type=F32),
                                 jnp.dot(e1[r], v1, preferred_element_type=F32))

    @pl.when(i >= N_PBLK)
    def _():
        cos = cos_ref[...]
        sin = sin_ref[...]
        low = (lax.broadcasted_iota(jnp.int32, (T_S, LANE), 1) & 16) == 0

        def rope(x):
            xs = jnp.where(low, pltpu.roll(x, LANE - 16, 1), pltpu.roll(x, 16, 1))
            return x * cos + xs * sin

        q_scr[...] = (rope(q_ref[...].astype(F32)) * (DIFF_HD ** -0.5)).astype(BF16)
        k_scr[...] = rope(k_ref[...].astype(F32)).astype(BF16)
        ckt = ckt_ref[...].astype(BF16)
        v_scr[0:T_S, 0:LANE] = v_ref[...]
        v_scr[T_S:T_S + PAST, 0:LANE] = cv_ref[:, pl.program_id(1), :].astype(BF16)
        v_scr[:, LANE:2 * LANE] = jnp.ones((T_S + PAST, LANE), BF16)

        def scores(qh):
            return jnp.concatenate([lax.dot_general(qh, k_scr[...], nt, preferred_element_type=F32),
                                    jnp.dot(qh, ckt, preferred_element_type=F32)], axis=1)

        for b in range(T_S // ATT_TQ):
            r = slice(b * ATT_TQ, (b + 1) * ATT_TQ)
            q0, q1 = halves(q_scr[r, :])
            e0 = weights(scores(q0))
            e1 = weights(scores(q1))
            o_ref[r, :] = finish(jnp.dot(e0, v_scr[...], preferred_element_type=F32),
                                 jnp.dot(e1, v_scr[...], preferred_element_type=F32))


def _attention(scal, proj, cache_k, cache_v, cos_t, sin_t, subln_g):
    nb = N_TOK // SEQ_BLK
    smp = lambda i: jnp.maximum(i - N_PBLK, 0)
    return pl.pallas_call(
        _attn_kernel,
        out_shape=jax.ShapeDtypeStruct((N_TOK, DIFF_H * 2 * DIFF_HD), BF16),
        grid=(nb, DIFF_H),
        in_specs=[pl.BlockSpec(memory_space=pltpu.SMEM),
                  pl.BlockSpec((None, SEQ_BLK, LANE), lambda i, h: (C_DQ + h, i, 0)),
                  pl.BlockSpec((None, SEQ_BLK, LANE), lambda i, h: (C_DK + h, i, 0)),
                  pl.BlockSpec((None, SEQ_BLK, LANE), lambda i, h: (C_DV + h, i, 0)),
                  pl.BlockSpec((None, LANE, PAST), lambda i, h: (smp(i), h, 0)),
                  pl.BlockSpec((None, None, PAST, DIFF_H, LANE), lambda i, h: (smp(i), 0, 0, 0, 0)),
                  pl.BlockSpec((T_S, LANE), lambda i, h: (0, 0)),
                  pl.BlockSpec((T_S, LANE), lambda i, h: (0, 0)),
                  pl.BlockSpec((1, LANE), lambda i, h: (0, 0))],
        out_specs=pl.BlockSpec((SEQ_BLK, LANE), lambda i, h: (i, h)),
        scratch_shapes=[pltpu.VMEM((T_S, LANE), BF16),
                        pltpu.VMEM((T_S, LANE), BF16),
                        pltpu.VMEM((T_S + PAST, 2 * LANE), BF16)],
        compiler_params=_cparams(("arbitrary", "arbitrary")),
        name="attn",
    )(scal, proj, proj, proj, cache_k, cache_v, cos_t, sin_t, subln_g)


def _rope_tables():
    n_rows = T_S // GRID_W
    row = jnp.repeat(jnp.arange(n_rows), GRID_W).astype(F32)
    col = jnp.tile(jnp.arange(GRID_W), n_rows).astype(F32)
    n_freq = DIFF_HD // 4
    inv = ROPE_BASE ** (-jnp.arange(n_freq, dtype=F32) / n_freq)

    def axis_tables(pos):
        ang = pos[:, None] * inv[None, :]
        c = jnp.cos(ang)
        s = jnp.sin(ang)
        return jnp.concatenate([c, c], axis=-1), jnp.concatenate([-s, s], axis=-1)

    cr, sr = axis_tables(row)
    cc, sc = axis_tables(col)
    cos_h = jnp.concatenate([cr, cc], axis=-1)
    sin_h = jnp.concatenate([sr, sc], axis=-1)
    return jnp.concatenate([cos_h, cos_h], axis=-1), jnp.concatenate([sin_h, sin_h], axis=-1)


OP_TM = 512
OP_NPT = N_PROMPT // OP_TM


def _outproj_kernel(ra_ref, da_ref, gr_ref, gd_ref, xp_ref, xs_ref, mod_ref, n2_ref,
                    wro_ref, wdo_ref, wo_ref, wr_ref, br_ref,
                    x1_ref, h2_ref, cw8_ref, info_ref, pc_ref, m_scr):
    i = pl.program_id(0)
    ret_out = jnp.dot(ra_ref[...], wro_ref[...], preferred_element_type=F32)
    diff_out = jnp.dot(da_ref[...], wdo_ref[...], preferred_element_type=F32)
    for c in range(D // LANE):
        sl = slice(c * LANE, (c + 1) * LANE)
        m = (jax.nn.sigmoid(gr_ref[c].astype(F32)) * ret_out[:, sl]
             + jax.nn.sigmoid(gd_ref[c].astype(F32)) * diff_out[:, sl])
        m_scr[:, sl] = m.astype(BF16)
    mix = jnp.dot(m_scr[...], wo_ref[...], preferred_element_type=F32)
    mod = mod_ref[...]

    x1 = jnp.where(i < OP_NPT, xp_ref[...], xs_ref[...]) + mod[:, 2 * D:3 * D] * mix
    x1_ref[...] = x1
    h2 = (_rms(x1) * n2_ref[...] * (1.0 + mod[:, 4 * D:5 * D]) + mod[:, 3 * D:4 * D]).astype(BF16)
    h2_ref[...] = h2
    logits = jnp.dot(h2, wr_ref[...], preferred_element_type=F32) + br_ref[...]
    cw8_ref[...], info_ref[...], pc_ref[...] = _route_rows(logits)


def _outproj(ret_act, diff_act, proj, xp, xs, mod3, n2, wro, wdo, wo, wr, br):
    assert OP_TM == RT_TM
    npt = OP_NPT
    full = lambda i: (0, 0)
    return pl.pallas_call(
        _outproj_kernel,
        out_shape=(jax.ShapeDtypeStruct((N_TOK, D), F32),
                   jax.ShapeDtypeStruct((N_TOK, D), BF16),
                   jax.ShapeDtypeStruct((N_TOK, ROUTE_W), F32),
                   jax.ShapeDtypeStruct((N_TOK, ROUTE_W), F32),
                   jax.ShapeDtypeStruct((RT_NT, 1, ROUTE_W), F32)),
        grid=(N_TOK // OP_TM,),
        in_specs=[pl.BlockSpec((OP_TM, D), lambda i: (i, 0)),
                  pl.BlockSpec((OP_TM, D), lambda i: (i, 0)),
                  pl.BlockSpec((8, OP_TM, LANE), lambda i: (C_GR // 8, i, 0)),
                  pl.BlockSpec((8, OP_TM, LANE), lambda i: (C_GD // 8, i, 0)),
                  pl.BlockSpec((OP_TM, D), lambda i: (jnp.minimum(i, npt - 1), 0)),
                  pl.BlockSpec((OP_TM, D), lambda i: (jnp.maximum(i - npt, 0), 0)),
                  pl.BlockSpec((None, 1, 6 * D), lambda i: (_mod_row(OP_TM)(i), 0, 0)),
                  pl.BlockSpec((1, D), full),
                  pl.BlockSpec((D, D), full),
                  pl.BlockSpec((D, D), full),
                  pl.BlockSpec((D, D), full),
                  pl.BlockSpec((D, ROUTE_W), full),
                  pl.BlockSpec((1, ROUTE_W), full)],
        out_specs=(pl.BlockSpec((OP_TM, D), lambda i: (i, 0)),
                   pl.BlockSpec((OP_TM, D), lambda i: (i, 0)),
                   pl.BlockSpec((OP_TM, ROUTE_W), lambda i: (i, 0)),
                   pl.BlockSpec((OP_TM, ROUTE_W), lambda i: (i, 0)),
                   pl.BlockSpec((None, 1, ROUTE_W), lambda i: (i, 0, 0))),
        scratch_shapes=[pltpu.VMEM((OP_TM, D), BF16)],
        compiler_params=_cparams(("arbitrary",)),
        name="outproj",
    )(ret_act, diff_act, proj, proj, xp, xs, mod3, n2, wro, wdo, wo, wr, br)


RT_TM = 512
RT_NT = N_TOK // RT_TM
PIECE = 16
R_LOC = RT_TM + N_GROUPS * PIECE
R_STAGE = 640
EX_TM = 512
R_MAX = 11264
EX_NT = R_MAX // EX_TM


def _route_rows(lg):
    lane = lax.broadcasted_iota(jnp.int32, lg.shape, 1)
    neg = jnp.float32(-jnp.inf)
    big = jnp.int32(ROUTE_W)

    def first_where(cond):
        return jnp.min(jnp.where(cond, lane, big), axis=-1, keepdims=True)

    is_g = lane < N_GROUPS
    gl = jnp.where(is_g, lg, neg)
    gmax = jnp.max(gl, axis=-1, keepdims=True)
    gsum = jnp.sum(jnp.where(is_g, jnp.exp(lg - gmax), 0.0), axis=-1, keepdims=True)
    p_top = 1.0 / gsum
    g_idx = first_where(gl == gmax)
    in_grp = jnp.logical_and(lane >= N_GROUPS, ((lane - N_GROUPS) >> 3) == g_idx)
    el = jnp.where(in_grp, lg, neg)
    emax = jnp.max(el, axis=-1, keepdims=True)
    ee = jnp.where(in_grp, jnp.exp(lg - emax), 0.0)
    ep = ee / jnp.sum(ee, axis=-1, keepdims=True)
    ep = jnp.where(in_grp, ep, -1.0)
    e1 = jnp.max(ep, axis=-1, keepdims=True)
    i1 = first_where(ep == e1)
    ep2 = jnp.where(lane == i1, -1.0, ep)
    e2 = jnp.max(ep2, axis=-1, keepdims=True)
    i2 = first_where(ep2 == e2)
    den = e1 + e2
    cw = (jnp.where(lane == i1, p_top * e1 / den, 0.0)
          + jnp.where(lane == i2, p_top * e2 / den, 0.0))
    cw8 = jnp.zeros_like(cw)
    for g in range(N_GROUPS):
        cw8 = cw8 + jnp.where(g_idx == g, pltpu.roll(cw, LANE - (N_GROUPS + EPG * g), 1), 0.0)

    onehot = (lane == g_idx).astype(F32)
    ii = lax.broadcasted_iota(jnp.int32, (RT_TM, RT_TM), 0)
    jj = lax.broadcasted_iota(jnp.int32, (RT_TM, RT_TM), 1)
    earlier = (jj < ii).astype(BF16)
    prefix = jnp.dot(earlier, onehot.astype(BF16), preferred_element_type=F32)
    cnt = jnp.sum(onehot, axis=0, keepdims=True)
    pc = jnp.floor((cnt + (PIECE - 1.0)) * (1.0 / PIECE)) * PIECE
    lo = pltpu.roll(pc, 1, 1) + pltpu.roll(pc, 2, 1) + pltpu.roll(pc, 3, 1)
    dest = jnp.sum(onehot * (prefix + lo), axis=-1, keepdims=True)
    info = jnp.where(lane == 0, g_idx.astype(F32), jnp.where(lane == 1, dest, 0.0))
    return cw8, info, pc


def _dispatch_plan(pc_arr):
    pc = pc_arr[:, 0, :N_GROUPS].astype(jnp.int32)
    seg_len = jnp.sum(pc, axis=0)
    seg_pad = (seg_len + (EX_TM - 1)) // EX_TM * EX_TM
    seg_end_pad = jnp.cumsum(seg_pad)
    seg_start = seg_end_pad - seg_pad
    chunk_off = seg_start[None, :] + jnp.cumsum(pc, axis=0) - pc
    tile_start = jnp.arange(EX_NT, dtype=jnp.int32) * EX_TM
    tile_group = jnp.minimum(jnp.sum((tile_start[:, None] >= seg_end_pad[None, :]).astype(jnp.int32), axis=1),
                             N_GROUPS - 1)
    tile_valid = (tile_start < (seg_start + seg_len)[tile_group]).astype(jnp.int32)
    return chunk_off.astype(jnp.int32), pc, tile_group.astype(jnp.int32), tile_valid


def _piece_copies(off_ref, pc_ref, tile, make):
    lo = 0
    for g in range(N_GROUPS):
        n = pc_ref[tile, g] // PIECE
        base = off_ref[tile, g]

        def body(j, carry, lo=lo, base=base):
            make(pl.multiple_of(lo + j * PIECE, PIECE), pl.multiple_of(base + j * PIECE, PIECE))
            return carry

        lax.fori_loop(0, n, body, 0)
        lo = lo + pc_ref[tile, g]


def _piece_count(pc_ref, tile):
    n = 0
    for g in range(N_GROUPS):
        n = n + pc_ref[tile, g] // PIECE
    return n


def _dispatch_kernel(off_ref, pc_ref, h_ref, info_ref, cw8_ref, xs_in, cws_in,
                     xs_out, cws_out, x_scr, c_scr, sem):
    del xs_in, cws_in
    i = pl.program_id(0)
    slot = i % 2
    dest = info_ref[...].T[1:2, :]
    row = lax.broadcasted_iota(jnp.int32, (R_LOC, RT_TM), 0).astype(F32)
    sel = row == dest
    sel = sel.astype(BF16)
    x_scr[slot] = jnp.dot(sel, h_ref[...], preferred_element_type=F32).astype(BF16)
    cw = cw8_ref[...]
    hi = cw.astype(BF16).astype(F32)
    mid = (cw - hi).astype(BF16).astype(F32)
    low = (cw - hi - mid).astype(BF16).astype(F32)
    pieces = (hi + pltpu.roll(mid, EPG, 1) + pltpu.roll(low, 2 * EPG, 1)).astype(BF16)
    c_scr[slot] = jnp.dot(sel, pieces, preferred_element_type=F32)

    def x_copy(s, src, dst):
        return pltpu.make_async_copy(x_scr.at[s, pl.ds(src, PIECE)], xs_out.at[pl.ds(dst, PIECE)],
                                     sem.at[0, s])

    def c_copy(s, src, dst):
        return pltpu.make_async_copy(c_scr.at[s, pl.ds(src, PIECE)], cws_out.at[pl.ds(dst, PIECE)],
                                     sem.at[1, s])

    def start(src, dst):
        x_copy(slot, src, dst).start()
        c_copy(slot, src, dst).start()

    _piece_copies(off_ref, pc_ref, i, start)

    def wait_tile(tile, s):
        def wait(j, carry):
            x_copy(s, 0, 0).wait()
            c_copy(s, 0, 0).wait()
            return carry

        lax.fori_loop(0, _piece_count(pc_ref, tile), wait, 0)

    @pl.when(i > 0)
    def _():
        wait_tile(i - 1, 1 - slot)

    @pl.when(i == RT_NT - 1)
    def _():
        wait_tile(i, slot)


def _dispatch(chunk_off, pc, h2, info, cw8, xs0, cws0):
    grid_spec = pltpu.PrefetchScalarGridSpec(
        num_scalar_prefetch=2,
        grid=(RT_NT,),
        in_specs=[pl.BlockSpec((RT_TM, D), lambda i, o, p: (i, 0)),
                  pl.BlockSpec((RT_TM, ROUTE_W), lambda i, o, p: (i, 0)),
                  pl.BlockSpec((RT_TM, ROUTE_W), lambda i, o, p: (i, 0)),
                  pl.BlockSpec(memory_space=pl.ANY),
                  pl.BlockSpec(memory_space=pl.ANY)],
        out_specs=(pl.BlockSpec(memory_space=pl.ANY), pl.BlockSpec(memory_space=pl.ANY)),
        scratch_shapes=[pltpu.VMEM((2, R_LOC, D), BF16),
                        pltpu.VMEM((2, R_LOC, ROUTE_W), F32),
                        pltpu.SemaphoreType.DMA((2, 2))])
    return pl.pallas_call(
        _dispatch_kernel,
        out_shape=(jax.ShapeDtypeStruct((R_MAX, D), BF16),
                   jax.ShapeDtypeStruct((R_MAX, ROUTE_W), F32)),
        grid_spec=grid_spec,
        input_output_aliases={5: 0, 6: 1},
        compiler_params=_cparams(("arbitrary",)),
        name="dispatch",
    )(chunk_off, pc, h2, info, cw8, xs0, cws0)


def _expert_kernel(tg_ref, tv_ref, x_ref, cw_ref, wg_ref, wu_ref, wd_ref, y_ref):
    del tg_ref
    k = pl.program_id(0)

    @pl.when(tv_ref[k] == 0)
    def _():
        y_ref[...] = jnp.zeros_like(y_ref)

    @pl.when(tv_ref[k] != 0)
    def _():
        x = x_ref[...]
        cw = cw_ref[...]
        lane = lax.broadcasted_iota(jnp.int32, cw.shape, 1)
        acc = jnp.zeros(y_ref.shape, F32)
        for j in range(EPG):
            mine = jnp.logical_and((lane & (EPG - 1)) == j, lane < 3 * EPG)
            w = jnp.sum(jnp.where(mine, cw, 0.0), axis=-1, keepdims=True)
            a = (jax.nn.silu(jnp.dot(x, wg_ref[j], preferred_element_type=F32))
                 * jnp.dot(x, wu_ref[j], preferred_element_type=F32))
            acc = acc + jnp.dot((a * w).astype(BF16), wd_ref[j], preferred_element_type=F32)
        y_ref[...] = acc.astype(BF16)


def _experts(tile_group, tile_valid, xs, cws, wg, wu, wd):
    grid_spec = pltpu.PrefetchScalarGridSpec(
        num_scalar_prefetch=2,
        grid=(EX_NT,),
        in_specs=[pl.BlockSpec((EX_TM, D), lambda k, tg, tv: (k, 0)),
                  pl.BlockSpec((EX_TM, ROUTE_W), lambda k, tg, tv: (k, 0)),
                  pl.BlockSpec((EPG, D, FF), lambda k, tg, tv: (tg[k], 0, 0)),
                  pl.BlockSpec((EPG, D, FF), lambda k, tg, tv: (tg[k], 0, 0)),
                  pl.BlockSpec((EPG, FF, D), lambda k, tg, tv: (tg[k], 0, 0))],
        out_specs=pl.BlockSpec((EX_TM, D), lambda k, tg, tv: (k, 0)))
    return pl.pallas_call(
        _expert_kernel,
        out_shape=jax.ShapeDtypeStruct((R_MAX, D), BF16),
        grid_spec=grid_spec,
        compiler_params=_cparams(("arbitrary",)),
        name="experts",
    )(tile_group, tile_valid, xs, cws, wg, wu, wd)


CB_NPT = N_PROMPT // RT_TM


def _combine_kernel(off_ref, pc_ref, info_ref, x1_ref, mod_ref, g_ref, ys_hbm,
                    yp_ref, ysm_ref, stage, sem):
    i = pl.program_id(0)
    slot = i % 2

    def copy(s, dst, src):
        return pltpu.make_async_copy(ys_hbm.at[pl.ds(src, PIECE)], stage.at[s, pl.ds(dst, PIECE)],
                                     sem.at[s])

    def fetch(tile, s):
        _piece_copies(off_ref, pc_ref, tile, lambda loc, glob: copy(s, loc, glob).start())

    @pl.when(i == 0)
    def _():
        stage[...] = jnp.zeros_like(stage)
        fetch(0, 0)

    @pl.when(i + 1 < RT_NT)
    def _():
        fetch(i + 1, 1 - slot)

    def wait(j, carry):
        copy(slot, 0, 0).wait()
        return carry

    lax.fori_loop(0, _piece_count(pc_ref, i), wait, 0)

    dest = info_ref[...][:, 1:2]
    col = lax.broadcasted_iota(jnp.int32, (RT_TM, R_STAGE), 1).astype(F32)
    moe = jnp.dot((col == dest).astype(BF16), stage[slot], preferred_element_type=F32)
    mod = mod_ref[...]
    out = _rms(x1_ref[...] + mod[:, 5 * D:6 * D] * moe) * g_ref[...]

    @pl.when(i < CB_NPT)
    def _():
        yp_ref[...] = out

    @pl.when(i >= CB_NPT)
    def _():
        ysm_ref[...] = out


def _combine(chunk_off, pc, info, x1, mod3, fg, ys):
    npt = CB_NPT
    grid_spec = pltpu.PrefetchScalarGridSpec(
        num_scalar_prefetch=2,
        grid=(RT_NT,),
        in_specs=[pl.BlockSpec((RT_TM, ROUTE_W), lambda i, o, p: (i, 0)),
                  pl.BlockSpec((RT_TM, D), lambda i, o, p: (i, 0)),
                  pl.BlockSpec((None, 1, 6 * D), lambda i, o, p: (_mod_row(RT_TM)(i), 0, 0)),
                  pl.BlockSpec((1, D), lambda i, o, p: (0, 0)),
                  pl.BlockSpec(memory_space=pl.ANY)],
        out_specs=(pl.BlockSpec((RT_TM, D), lambda i, o, p: (jnp.minimum(i, npt - 1), 0)),
                   pl.BlockSpec((RT_TM, D), lambda i, o, p: (jnp.maximum(i - npt, 0), 0))),
        scratch_shapes=[pltpu.VMEM((2, R_STAGE, D), BF16),
                        pltpu.SemaphoreType.DMA((2,))])
    return pl.pallas_call(
        _combine_kernel,
        out_shape=(jax.ShapeDtypeStruct((N_PROMPT, D), F32),
                   jax.ShapeDtypeStruct((N_SAMPLE, D), F32)),
        grid_spec=grid_spec,
        compiler_params=_cparams(("arbitrary",)),
        name="combine",
    )(chunk_off, pc, info, x1, mod3, fg, ys)


def kernel(x_prompt, x_sample, cache_diff_k, cache_diff_v, state_ret_fwd, state_ret_bwd, c, c_ctx,
           w_ada, b_ada, norm1_g, norm2_g, w_in, ret_decay_fwd, ret_decay_bwd, ret_norm_g,
           diff_lambda_q1, diff_lambda_k1, diff_lambda_q2, diff_lambda_k2, diff_subln_g,
           w_ret_o, w_diff_o, w_o, router_group_w, router_group_b, router_expert_w, router_expert_b,
           moe_w_gate, moe_w_up, moe_w_down, final_norm_g):
    l = 0
    lam_init = 0.8 - 0.6 * math.exp(-0.3 * l)
    lam = (jnp.exp(jnp.sum(diff_lambda_q1[l].astype(F32) * diff_lambda_k1[l].astype(F32)))
           - jnp.exp(jnp.sum(diff_lambda_q2[l].astype(F32) * diff_lambda_k2[l].astype(F32))) + lam_init)
    attn_scal = jnp.stack([lam, jnp.float32(1.0 - lam_init)]).astype(F32)
    lg = jnp.stack([jax.nn.log_sigmoid(ret_decay_fwd[l].astype(F32)),
                    jax.nn.log_sigmoid(ret_decay_bwd[l].astype(F32))])

    xp = x_prompt.reshape(N_PROMPT, D)
    xs = x_sample.reshape(N_SAMPLE, D)
    cvec = jnp.concatenate([c_ctx[None, :], c, jnp.zeros((3, D), F32)], axis=0)
    mod3 = _modulation(cvec, w_ada[l], b_ada[l][None, :]).reshape(8, 1, 6 * D)

    (proj, kt32, v32, wg_bf, wu_bf, wd_bf, wro_bf, wdo_bf, wo_bf, xs0, cws0) = _inproj(
        xp, xs, mod3, norm1_g[l][None, :], w_in[l],
        moe_w_gate[l], moe_w_up[l], moe_w_down[l], w_ret_o[l], w_diff_o[l], w_o[l])

    ret_act, s_f, s_b = _retention(lg, proj, state_ret_fwd[:, l], state_ret_bwd[:, l],
                                   ret_norm_g[l][None, :])
    cos_t, sin_t = _rope_tables()
    cache_kt = jnp.transpose(cache_diff_k[:, l], (0, 2, 3, 4, 1)).reshape(4, D, PAST)
    diff_act = _attention(attn_scal, proj, cache_kt, cache_diff_v,
                          cos_t, sin_t, diff_subln_g[l][None, :])

    wr = jnp.concatenate([router_group_w[l], router_expert_w[l],
                          jnp.zeros((D, ROUTE_W - N_GROUPS - N_EXP), F32)], axis=1).astype(BF16)
    br = jnp.concatenate([router_group_b[l], router_expert_b[l],
                          jnp.zeros((ROUTE_W - N_GROUPS - N_EXP,), F32)])[None, :]
    x1, h2, cw8, info, pc_arr = _outproj(ret_act, diff_act, proj, xp, xs, mod3, norm2_g[l][None, :],
                                         wro_bf, wdo_bf, wo_bf, wr, br)
    chunk_off, pc, tile_group, tile_valid = _dispatch_plan(pc_arr)
    xs_sorted, cw_sorted = _dispatch(chunk_off, pc, h2, info, cw8, xs0, cws0)
    y_sorted = _experts(tile_group, tile_valid, xs_sorted, cw_sorted, wg_bf, wu_bf, wd_bf)
    yp, ys = _combine(chunk_off, pc, info, x1, mod3, final_norm_g[None, :], y_sorted)

    return (yp.reshape(16, T_P, D), ys.reshape(4, T_S, D),
            jnp.transpose(kt32.reshape(16, DIFF_H, 2, DIFF_HD, T_P), (0, 4, 1, 2, 3))[:, None],
            v32.reshape(16, 1, T_P, DIFF_H, 2 * DIFF_HD),
            s_f.reshape(16, 1, RET_H, RET_KD, RET_VD), s_b.reshape(16, 1, RET_H, RET_KD, RET_VD))
```

```python
import functools
import math

import jax
import jax.numpy as jnp
from jax import lax
from jax.experimental import pallas as pl
from jax.experimental.pallas import tpu as pltpu

F32 = jnp.float32
BF16 = jnp.bfloat16

D = 1024
N_PROMPT = 16 * 256
N_SAMPLE = 4 * 1024
N_TOK = N_PROMPT + N_SAMPLE
T_P = 256
T_S = 1024
PAST = 512
GRID_W = 64
RET_H = 4
RET_KD = 128
RET_VD = 256
DIFF_H = 8
DIFF_HD = 64
ROPE_BASE = 10000.0
N_GROUPS = 4
EPG = 8
N_EXP = 32
FF = 256
EPS = 1e-6
IN_W = 8192
LANE = 128
N_CHUNK = IN_W // LANE
C_RQ, C_RK, C_RV, C_RG, C_DQ, C_DK, C_DV, C_GR, C_GD = 0, 4, 8, 16, 24, 32, 40, 48, 56
ROUTE_W = 128
SEQ_BLK = 1024
N_PBLK = N_PROMPT // SEQ_BLK
VMEM_LIMIT = 56 * 1024 * 1024


def _cparams(sem):
    return pltpu.CompilerParams(dimension_semantics=sem, vmem_limit_bytes=VMEM_LIMIT)


def _mod_row(tile_rows):
    def f(i):
        start = i * tile_rows
        return jnp.where(start < N_PROMPT, 0, 1 + (start - N_PROMPT) // T_S)
    return f


def _rms(x):
    return x * lax.rsqrt(jnp.mean(x * x, axis=-1, keepdims=True) + EPS)


def _mod_kernel(c_ref, w_ref, b_ref, o_ref):
    s = jax.nn.silu(c_ref[...])
    o_ref[...] = jnp.dot(s.astype(BF16), w_ref[...].astype(BF16),
                         preferred_element_type=F32) + b_ref[...]


def _modulation(cvec, w_ada, b_ada):
    tn = 1536
    return pl.pallas_call(
        _mod_kernel,
        out_shape=jax.ShapeDtypeStruct((8, 6 * D), F32),
        grid=(6 * D // tn,),
        in_specs=[pl.BlockSpec((8, D), lambda j: (0, 0)),
                  pl.BlockSpec((D, tn), lambda j: (0, j)),
                  pl.BlockSpec((1, tn), lambda j: (0, j))],
        out_specs=pl.BlockSpec((8, tn), lambda j: (0, j)),
        compiler_params=_cparams(("arbitrary",)),
        name="mod",
    )(cvec, w_ada, b_ada)


IP_TM = 512
IP_TN = 2048
IP_NPT = N_PROMPT // IP_TM


IP_KV_TILE = C_DK * LANE // IP_TN
IP_SPT = IP_TM // T_P


def _inproj_kernel(xp_ref, xs_ref, mod_ref, n1_ref, w_ref, proj_ref, kt_ref, v32_ref, w_scr):
    j = pl.program_id(0)
    i = pl.program_id(1)

    @pl.when(i == 0)
    def _():
        w_scr[...] = w_ref[...].astype(BF16)

    x = jnp.where(i < IP_NPT, xp_ref[...], xs_ref[...])
    mod = mod_ref[...]
    h = (_rms(x) * n1_ref[...] * (1.0 + mod[:, D:2 * D]) + mod[:, 0:D]).astype(BF16)
    acc = jnp.dot(h, w_scr[...], preferred_element_type=F32)
    for c in range(IP_TN // LANE):
        proj_ref[c] = acc[:, c * LANE:(c + 1) * LANE].astype(BF16)

    @pl.when(jnp.logical_and(j == IP_KV_TILE, i < IP_NPT))
    def _():
        for s in range(IP_SPT):
            kt_ref[s] = acc[s * T_P:(s + 1) * T_P, :D].T
        v32_ref[...] = acc[:, D:]


def _inproj(xp, xs, mod3, n1, w_in):
    npt = IP_NPT
    cpt = IP_TN // LANE

    def kv_tile(j, i):
        return jnp.where(j < IP_KV_TILE, 0, jnp.where(j == IP_KV_TILE, jnp.minimum(i, npt - 1), npt - 1))

    return pl.pallas_call(
        _inproj_kernel,
        out_shape=(jax.ShapeDtypeStruct((N_CHUNK, N_TOK, LANE), BF16),
                   jax.ShapeDtypeStruct((N_PROMPT // T_P, D, T_P), F32),
                   jax.ShapeDtypeStruct((N_PROMPT, D), F32)),
        grid=(IN_W // IP_TN, N_TOK // IP_TM),
        in_specs=[pl.BlockSpec((IP_TM, D), lambda j, i: (jnp.minimum(i, npt - 1), 0)),
                  pl.BlockSpec((IP_TM, D), lambda j, i: (jnp.maximum(i - npt, 0), 0)),
                  pl.BlockSpec((None, 1, 6 * D), lambda j, i: (_mod_row(IP_TM)(i), 0, 0)),
                  pl.BlockSpec((1, D), lambda j, i: (0, 0)),
                  pl.BlockSpec((D, IP_TN), lambda j, i: (0, j))],
        out_specs=(pl.BlockSpec((cpt, IP_TM, LANE), lambda j, i: (j, i, 0)),
                   pl.BlockSpec((IP_SPT, D, T_P), lambda j, i: (kv_tile(j, i), 0, 0)),
                   pl.BlockSpec((IP_TM, D), lambda j, i: (kv_tile(j, i), 0))),
        scratch_shapes=[pltpu.VMEM((D, IP_TN), BF16)],
        compiler_params=_cparams(("arbitrary", "arbitrary")),
        name="inproj",
    )(xp, xs, mod3, n1, w_in)


def _decay_mask(t, lgf, lgb):
    ii = lax.broadcasted_iota(jnp.int32, (t, t), 0)
    jj = lax.broadcasted_iota(jnp.int32, (t, t), 1)
    rel = (ii - jj).astype(F32)
    e = jnp.exp(jnp.where(rel >= 0.0, lgf, -lgb) * rel)
    return jnp.where(rel == 0.0, 2.0, e) * (RET_KD ** -0.5)


def _ret_kernel(lg_ref, q_ref, k_ref, v_ref, rg_ref, s0f_ref, s0b_ref, g_ref,
                o_ref, sf_ref, sb_ref, dmp_scr, dms_scr):
    h = pl.program_id(0)
    i = pl.program_id(1)
    lgf = lg_ref[0, h]
    lgb = lg_ref[1, h]
    gain = g_ref[...]
    nt = (((1,), (1,)), ((), ()))
    tn = (((0,), (0,)), ((), ()))

    def finish(o, rg):
        d = o - jnp.mean(o, axis=-1, keepdims=True)
        y = d * lax.rsqrt(jnp.mean(d * d, axis=-1, keepdims=True) + EPS) * gain
        return (jax.nn.silu(rg.astype(F32)) * y).astype(BF16)

    def vcat(r):
        return jnp.concatenate([v_ref[0, r, :], v_ref[1, r, :]], axis=1)

    def gcat(r):
        return jnp.concatenate([rg_ref[0, r, :], rg_ref[1, r, :]], axis=1)

    @pl.when(i == 0)
    def _():
        dmp_scr[...] = _decay_mask(T_P, lgf, lgb)

    @pl.when(i == N_PBLK)
    def _():
        dms_scr[...] = _decay_mask(T_S, lgf, lgb)

    @pl.when(i < N_PBLK)
    def _():
        dm = dmp_scr[...]
        t = lax.broadcasted_iota(jnp.int32, (T_P, 1), 0).astype(F32)
        kdf = jnp.exp(lgf * (T_P - 1.0 - t)) * (RET_KD ** -0.5)
        kdb = jnp.exp(lgb * t) * (RET_KD ** -0.5)
        for s in range(SEQ_BLK // T_P):
            r = slice(s * T_P, (s + 1) * T_P)
            q = q_ref[r, :]
            k = k_ref[r, :]
            v = vcat(r)
            sc = lax.dot_general(q, k, nt, preferred_element_type=F32)
            o = jnp.dot((sc * dm).astype(BF16), v, preferred_element_type=F32)
            o_ref[r, :] = finish(o, gcat(r))
            kf = k.astype(F32)
            sf_ref[s] = lax.dot_general((kf * kdf).astype(BF16), v, tn, preferred_element_type=F32)
            sb_ref[s] = lax.dot_general((kf * kdb).astype(BF16), v, tn, preferred_element_type=F32)

    @pl.when(i >= N_PBLK)
    def _():
        r = slice(0, T_S)
        t = lax.broadcasted_iota(jnp.int32, (T_S, 1), 0).astype(F32)
        q = q_ref[...]
        k = k_ref[...]
        v = vcat(r)
        sc = lax.dot_general(q, k, nt, preferred_element_type=F32)
        o = jnp.dot((sc * dms_scr[...]).astype(BF16), v, preferred_element_type=F32)
        qf = q.astype(F32)
        o = o + jnp.dot((qf * jnp.exp(lgf * (t + 1.0))).astype(BF16), s0f_ref[...].astype(BF16),
                        preferred_element_type=F32)
        o = o + jnp.dot((qf * jnp.exp(lgb * (T_S - t))).astype(BF16), s0b_ref[...].astype(BF16),
                        preferred_element_type=F32)
        o_ref[...] = finish(o, gcat(r))


def _retention(lg, proj, s0f, s0b, gnorm):
    nb = N_TOK // SEQ_BLK
    spb = SEQ_BLK // T_P
    smp = lambda i: jnp.maximum(i - N_PBLK, 0)
    pmt = lambda i: jnp.minimum(i, N_PBLK - 1)
    return pl.pallas_call(
        _ret_kernel,
        out_shape=(jax.ShapeDtypeStruct((N_TOK, RET_H * RET_VD), BF16),
                   jax.ShapeDtypeStruct((16, RET_H, RET_KD, RET_VD), F32),
                   jax.ShapeDtypeStruct((16, RET_H, RET_KD, RET_VD), F32)),
        grid=(RET_H, nb),
        in_specs=[pl.BlockSpec(memory_space=pltpu.SMEM),
                  pl.BlockSpec((None, SEQ_BLK, LANE), lambda h, i: (C_RQ + h, i, 0)),
                  pl.BlockSpec((None, SEQ_BLK, LANE), lambda h, i: (C_RK + h, i, 0)),
                  pl.BlockSpec((2, SEQ_BLK, LANE), lambda h, i: (C_RV // 2 + h, i, 0)),
                  pl.BlockSpec((2, SEQ_BLK, LANE), lambda h, i: (C_RG // 2 + h, i, 0)),
                  pl.BlockSpec((None, None, RET_KD, RET_VD), lambda h, i: (smp(i), h, 0, 0)),
                  pl.BlockSpec((None, None, RET_KD, RET_VD), lambda h, i: (smp(i), h, 0, 0)),
                  pl.BlockSpec((1, RET_VD), lambda h, i: (0, h))],
        out_specs=(pl.BlockSpec((SEQ_BLK, RET_VD), lambda h, i: (i, h)),
                   pl.BlockSpec((spb, None, RET_KD, RET_VD), lambda h, i: (pmt(i), h, 0, 0)),
                   pl.BlockSpec((spb, None, RET_KD, RET_VD), lambda h, i: (pmt(i), h, 0, 0))),
        scratch_shapes=[pltpu.VMEM((T_P, T_P), F32), pltpu.VMEM((T_S, T_S), F32)],
        compiler_params=_cparams(("arbitrary", "arbitrary")),
        name="ret",
    )(lg, proj, proj, proj, proj, s0f, s0b, gnorm)


ATT_TQ = 256


N_SIDE = 6


def _attn_kernel(sc_ref, q_ref, k_ref, v_ref, ckt_ref, cv_ref, cos_ref, sin_ref, g_ref, *rest):
    side_in = rest[:N_SIDE]
    o_ref = rest[N_SIDE]
    side_out = rest[N_SIDE + 1:2 * N_SIDE + 1]
    xs0_ref, cws0_ref, q_scr, k_scr, v_scr = rest[2 * N_SIDE + 1:]
    i = pl.program_id(0)

    def side_jobs():
        for src, dst in zip(side_in, side_out):
            dst[...] = src[...].astype(BF16)
        xs0_ref[...] = jnp.zeros_like(xs0_ref)
        cws0_ref[...] = jnp.zeros_like(cws0_ref)

    lam = sc_ref[0]
    out_scale = sc_ref[1]
    gain = g_ref[...] * out_scale
    lane = lax.broadcasted_iota(jnp.int32, (1, LANE), 1)
    first = lane < DIFF_HD
    nt = (((1,), (1,)), ((), ()))

    def halves(q):
        zero = jnp.zeros_like(q)
        return jnp.where(first, q, zero), jnp.where(first, zero, q)

    def weights(s):
        return jnp.exp(s - jnp.max(s, axis=-1, keepdims=True)).astype(BF16)

    def finish(of0, of1):
        o = of0[:, :LANE] / of0[:, LANE:] - lam * (of1[:, :LANE] / of1[:, LANE:])
        return (_rms(o) * gain).astype(BF16)

    @pl.when(i < N_PBLK)
    def _():
        side_jobs()
        n_seq = SEQ_BLK // T_P
        rows = [slice(s * T_P, (s + 1) * T_P) for s in range(n_seq)]
        q0, q1 = halves((q_ref[...].astype(F32) * (DIFF_HD ** -0.5)).astype(BF16))
        s0 = jnp.concatenate([lax.dot_general(q0[r], k_ref[r, :], nt, preferred_element_type=F32)
                              for r in rows], axis=0)
        s1 = jnp.concatenate([lax.dot_general(q1[r], k_ref[r, :], nt, preferred_element_type=F32)
                              for r in rows], axis=0)
        e0 = weights(s0)
        e1 = weights(s1)
        ones = jnp.ones((T_P, LANE), BF16)
        for r in rows:
            v1 = jnp.concatenate([v_ref[r, :], ones], axis=1)
            o_ref[r, :] = finish(jnp.dot(e0[r], v1, preferred_element_type=F32),
                                 jnp.dot(e1[r], v1, preferred_element_type=F32))

    @pl.when(i >= N_PBLK)
    def _():
        side_jobs()
        cos = cos_ref[...]
        sin = sin_ref[...]
        low = (lax.broadcasted_iota(jnp.int32, (T_S, LANE), 1) & 16) == 0

        def rope(x):
            xs = jnp.where(low, pltpu.roll(x, LANE - 16, 1), pltpu.roll(x, 16, 1))
            return x * cos + xs * sin

        q_scr[...] = (rope(q_ref[...].astype(F32)) * (DIFF_HD ** -0.5)).astype(BF16)
        k_scr[...] = rope(k_ref[...].astype(F32)).astype(BF16)
        ckt = ckt_ref[...].astype(BF16)
        v_scr[0:T_S, 0:LANE] = v_ref[...]
        v_scr[T_S:T_S + PAST, 0:LANE] = cv_ref[:, pl.program_id(1), :].astype(BF16)
        v_scr[:, LANE:2 * LANE] = jnp.ones((T_S + PAST, LANE), BF16)

        def scores(qh):
            return jnp.concatenate([lax.dot_general(qh, k_scr[...], nt, preferred_element_type=F32),
                                    jnp.dot(qh, ckt, preferred_element_type=F32)], axis=1)

        for b in range(T_S // ATT_TQ):
            r = slice(b * ATT_TQ, (b + 1) * ATT_TQ)
            q0, q1 = halves(q_scr[r, :])
            e0 = weights(scores(q0))
            e1 = weights(scores(q1))
            o_ref[r, :] = finish(jnp.dot(e0, v_scr[...], preferred_element_type=F32),
                                 jnp.dot(e1, v_scr[...], preferred_element_type=F32))


def _attention(scal, proj, cache_k, cache_v, cos_t, sin_t, subln_g, side_weights):
    nb = N_TOK // SEQ_BLK
    smp = lambda i: jnp.maximum(i - N_PBLK, 0)

    n_steps = nb * DIFF_H
    step = lambda i, h: i * DIFF_H + h
    per_exp = n_steps // N_EXP
    exp_slice = lambda i, h: (step(i, h) // per_exp, step(i, h) % per_exp, 0)
    row_slice = lambda i, h: (step(i, h), 0)
    up_spec = pl.BlockSpec((None, D // per_exp, FF), exp_slice)
    down_spec = pl.BlockSpec((None, FF // per_exp, D), exp_slice)
    sq_spec = pl.BlockSpec((D // n_steps, D), row_slice)
    side_specs = [up_spec, up_spec, down_spec, sq_spec, sq_spec, sq_spec]
    side_shapes = [jax.ShapeDtypeStruct(a.shape, BF16) for a in side_weights]
    zrows = R_MAX // n_steps

    return pl.pallas_call(
        _attn_kernel,
        out_shape=(jax.ShapeDtypeStruct((N_TOK, DIFF_H * 2 * DIFF_HD), BF16),
                   *side_shapes,
                   jax.ShapeDtypeStruct((R_MAX, D), BF16),
                   jax.ShapeDtypeStruct((R_MAX, ROUTE_W), F32)),
        grid=(nb, DIFF_H),
        in_specs=[pl.BlockSpec(memory_space=pltpu.SMEM),
                  pl.BlockSpec((None, SEQ_BLK, LANE), lambda i, h: (C_DQ + h, i, 0)),
                  pl.BlockSpec((None, SEQ_BLK, LANE), lambda i, h: (C_DK + h, i, 0)),
                  pl.BlockSpec((None, SEQ_BLK, LANE), lambda i, h: (C_DV + h, i, 0)),
                  pl.BlockSpec((None, LANE, PAST), lambda i, h: (smp(i), h, 0)),
                  pl.BlockSpec((None, None, PAST, DIFF_H, LANE), lambda i, h: (smp(i), 0, 0, 0, 0)),
                  pl.BlockSpec((T_S, LANE), lambda i, h: (0, 0)),
                  pl.BlockSpec((T_S, LANE), lambda i, h: (0, 0)),
                  pl.BlockSpec((1, LANE), lambda i, h: (0, 0)),
                  *side_specs],
        out_specs=(pl.BlockSpec((SEQ_BLK, LANE), lambda i, h: (i, h)),
                   *side_specs,
                   pl.BlockSpec((zrows, D), row_slice),
                   pl.BlockSpec((zrows, ROUTE_W), row_slice)),
        scratch_shapes=[pltpu.VMEM((T_S, LANE), BF16),
                        pltpu.VMEM((T_S, LANE), BF16),
                        pltpu.VMEM((T_S + PAST, 2 * LANE), BF16)],
        compiler_params=_cparams(("arbitrary", "arbitrary")),
        name="attn",
    )(scal, proj, proj, proj, cache_k, cache_v, cos_t, sin_t, subln_g, *side_weights)


def _rope_tables():
    n_rows = T_S // GRID_W
    row = jnp.repeat(jnp.arange(n_rows), GRID_W).astype(F32)
    col = jnp.tile(jnp.arange(GRID_W), n_rows).astype(F32)
    n_freq = DIFF_HD // 4
    inv = ROPE_BASE ** (-jnp.arange(n_freq, dtype=F32) / n_freq)

    def axis_tables(pos):
        ang = pos[:, None] * inv[None, :]
        c = jnp.cos(ang)
        s = jnp.sin(ang)
        return jnp.concatenate([c, c], axis=-1), jnp.concatenate([-s, s], axis=-1)

    cr, sr = axis_tables(row)
    cc, sc = axis_tables(col)
    cos_h = jnp.concatenate([cr, cc], axis=-1)
    sin_h = jnp.concatenate([sr, sc], axis=-1)
    return jnp.concatenate([cos_h, cos_h], axis=-1), jnp.concatenate([sin_h, sin_h], axis=-1)


OP_TM = 512
OP_NPT = N_PROMPT // OP_TM


def _outproj_kernel(ra_ref, da_ref, gr_ref, gd_ref, xp_ref, xs_ref, mod_ref, n2_ref,
                    wro_ref, wdo_ref, wo_ref, wr_ref, br_ref,
                    x1_ref, h2_ref, cw8_ref, info_ref, pc_ref, m_scr):
    i = pl.program_id(0)
    ret_out = jnp.dot(ra_ref[...], wro_ref[...], preferred_element_type=F32)
    diff_out = jnp.dot(da_ref[...], wdo_ref[...], preferred_element_type=F32)
    for c in range(D // LANE):
        sl = slice(c * LANE, (c + 1) * LANE)
        m = (jax.nn.sigmoid(gr_ref[c].astype(F32)) * ret_out[:, sl]
             + jax.nn.sigmoid(gd_ref[c].astype(F32)) * diff_out[:, sl])
        m_scr[:, sl] = m.astype(BF16)
    mix = jnp.dot(m_scr[...], wo_ref[...], preferred_element_type=F32)
    mod = mod_ref[...]

    x1 = jnp.where(i < OP_NPT, xp_ref[...], xs_ref[...]) + mod[:, 2 * D:3 * D] * mix
    x1_ref[...] = x1
    h2 = (_rms(x1) * n2_ref[...] * (1.0 + mod[:, 4 * D:5 * D]) + mod[:, 3 * D:4 * D]).astype(BF16)
    h2_ref[...] = h2
    logits = jnp.dot(h2, wr_ref[...], preferred_element_type=F32) + br_ref[...]
    cw8_ref[...], info_ref[...], pc_ref[...] = _route_rows(logits)


def _outproj(ret_act, diff_act, proj, xp, xs, mod3, n2, wro, wdo, wo, wr, br):
    assert OP_TM == RT_TM
    npt = OP_NPT
    full = lambda i: (0, 0)
    return pl.pallas_call(
        _outproj_kernel,
        out_shape=(jax.ShapeDtypeStruct((N_TOK, D), F32),
                   jax.ShapeDtypeStruct((N_TOK, D), BF16),
                   jax.ShapeDtypeStruct((N_TOK, ROUTE_W), F32),
                   jax.ShapeDtypeStruct((N_TOK, ROUTE_W), F32),
                   jax.ShapeDtypeStruct((RT_NT, 1, ROUTE_W), F32)),
        grid=(N_TOK // OP_TM,),
        in_specs=[pl.BlockSpec((OP_TM, D), lambda i: (i, 0)),
                  pl.BlockSpec((OP_TM, D), lambda i: (i, 0)),
                  pl.BlockSpec((8, OP_TM, LANE), lambda i: (C_GR // 8, i, 0)),
                  pl.BlockSpec((8, OP_TM, LANE), lambda i: (C_GD // 8, i, 0)),
                  pl.BlockSpec((OP_TM, D), lambda i: (jnp.minimum(i, npt - 1), 0)),
                  pl.BlockSpec((OP_TM, D), lambda i: (jnp.maximum(i - npt, 0), 0)),
                  pl.BlockSpec((None, 1, 6 * D), lambda i: (_mod_row(OP_TM)(i), 0, 0)),
                  pl.BlockSpec((1, D), full),
                  pl.BlockSpec((D, D), full),
                  pl.BlockSpec((D, D), full),
                  pl.BlockSpec((D, D), full),
                  pl.BlockSpec((D, ROUTE_W), full),
                  pl.BlockSpec((1, ROUTE_W), full)],
        out_specs=(pl.BlockSpec((OP_TM, D), lambda i: (i, 0)),
                   pl.BlockSpec((OP_TM, D), lambda i: (i, 0)),
                   pl.BlockSpec((OP_TM, ROUTE_W), lambda i: (i, 0)),
                   pl.BlockSpec((OP_TM, ROUTE_W), lambda i: (i, 0)),
                   pl.BlockSpec((None, 1, ROUTE_W), lambda i: (i, 0, 0))),
        scratch_shapes=[pltpu.VMEM((OP_TM, D), BF16)],
        compiler_params=_cparams(("arbitrary",)),
        name="outproj",
    )(ret_act, diff_act, proj, proj, xp, xs, mod3, n2, wro, wdo, wo, wr, br)


RT_TM = 512
RT_NT = N_TOK // RT_TM
PIECE = 16
R_LOC = RT_TM + N_GROUPS * PIECE
R_STAGE = 640
EX_TM = 512
R_MAX = 11264
EX_NT = R_MAX // EX_TM


def _route_rows(lg):
    lane = lax.broadcasted_iota(jnp.int32, lg.shape, 1)
    neg = jnp.float32(-jnp.inf)
    big = jnp.int32(ROUTE_W)

    def first_where(cond):
        return jnp.min(jnp.where(cond, lane, big), axis=-1, keepdims=True)

    is_g = lane < N_GROUPS
    gl = jnp.where(is_g, lg, neg)
    gmax = jnp.max(gl, axis=-1, keepdims=True)
    gsum = jnp.sum(jnp.where(is_g, jnp.exp(lg - gmax), 0.0), axis=-1, keepdims=True)
    p_top = 1.0 / gsum
    g_idx = first_where(gl == gmax)
    in_grp = jnp.logical_and(lane >= N_GROUPS, ((lane - N_GROUPS) >> 3) == g_idx)
    el = jnp.where(in_grp, lg, neg)
    emax = jnp.max(el, axis=-1, keepdims=True)
    ee = jnp.where(in_grp, jnp.exp(lg - emax), 0.0)
    ep = ee / jnp.sum(ee, axis=-1, keepdims=True)
    ep = jnp.where(in_grp, ep, -1.0)
    e1 = jnp.max(ep, axis=-1, keepdims=True)
    i1 = first_where(ep == e1)
    ep2 = jnp.where(lane == i1, -1.0, ep)
    e2 = jnp.max(ep2, axis=-1, keepdims=True)
    i2 = first_where(ep2 == e2)
    den = e1 + e2
    cw = (jnp.where(lane == i1, p_top * e1 / den, 0.0)
          + jnp.where(lane == i2, p_top * e2 / den, 0.0))
    cw8 = jnp.zeros_like(cw)
    for g in range(N_GROUPS):
        cw8 = cw8 + jnp.where(g_idx == g, pltpu.roll(cw, LANE - (N_GROUPS + EPG * g), 1), 0.0)

    onehot = (lane == g_idx).astype(F32)
    ii = lax.broadcasted_iota(jnp.int32, (RT_TM, RT_TM), 0)
    jj = lax.broadcasted_iota(jnp.int32, (RT_TM, RT_TM), 1)
    earlier = (jj < ii).astype(BF16)
    prefix = jnp.dot(earlier, onehot.astype(BF16), preferred_element_type=F32)
    cnt = jnp.sum(onehot, axis=0, keepdims=True)
    pc = jnp.floor((cnt + (PIECE - 1.0)) * (1.0 / PIECE)) * PIECE
    lo = pltpu.roll(pc, 1, 1) + pltpu.roll(pc, 2, 1) + pltpu.roll(pc, 3, 1)
    dest = jnp.sum(onehot * (prefix + lo), axis=-1, keepdims=True)
    info = jnp.where(lane == 0, g_idx.astype(F32), jnp.where(lane == 1, dest, 0.0))
    return cw8, info, pc


def _dispatch_plan(pc_arr):
    pc = pc_arr[:, 0, :N_GROUPS].astype(jnp.int32)
    seg_len = jnp.sum(pc, axis=0)
    seg_pad = (seg_len + (EX_TM - 1)) // EX_TM * EX_TM
    seg_end_pad = jnp.cumsum(seg_pad)
    seg_start = seg_end_pad - seg_pad
    chunk_off = seg_start[None, :] + jnp.cumsum(pc, axis=0) - pc
    tile_start = jnp.arange(EX_NT, dtype=jnp.int32) * EX_TM
    tile_group = jnp.minimum(jnp.sum((tile_start[:, None] >= seg_end_pad[None, :]).astype(jnp.int32), axis=1),
                             N_GROUPS - 1)
    tile_valid = (tile_start < (seg_start + seg_len)[tile_group]).astype(jnp.int32)
    return chunk_off.astype(jnp.int32), pc, tile_group.astype(jnp.int32), tile_valid


def _piece_copies(off_ref, pc_ref, tile, make):
    lo = 0
    for g in range(N_GROUPS):
        n = pc_ref[tile, g] // PIECE
        base = off_ref[tile, g]

        def body(j, carry, lo=lo, base=base):
            make(pl.multiple_of(lo + j * PIECE, PIECE), pl.multiple_of(base + j * PIECE, PIECE))
            return carry

        lax.fori_loop(0, n, body, 0)
        lo = lo + pc_ref[tile, g]


def _piece_count(pc_ref, tile):
    n = 0
    for g in range(N_GROUPS):
        n = n + pc_ref[tile, g] // PIECE
    return n


def _dispatch_kernel(off_ref, pc_ref, h_ref, info_ref, cw8_ref, xs_in, cws_in,
                     xs_out, cws_out, x_scr, c_scr, sem):
    del xs_in, cws_in
    i = pl.program_id(0)
    slot = i % 2
    dest = info_ref[...].T[1:2, :]
    row = lax.broadcasted_iota(jnp.int32, (R_LOC, RT_TM), 0).astype(F32)
    sel = row == dest
    sel = sel.astype(BF16)
    x_scr[slot] = jnp.dot(sel, h_ref[...], preferred_element_type=F32).astype(BF16)
    cw = cw8_ref[...]
    hi = cw.astype(BF16).astype(F32)
    mid = (cw - hi).astype(BF16).astype(F32)
    low = (cw - hi - mid).astype(BF16).astype(F32)
    pieces = (hi + pltpu.roll(mid, EPG, 1) + pltpu.roll(low, 2 * EPG, 1)).astype(BF16)
    c_scr[slot] = jnp.dot(sel, pieces, preferred_element_type=F32)

    def x_copy(s, src, dst):
        return pltpu.make_async_copy(x_scr.at[s, pl.ds(src, PIECE)], xs_out.at[pl.ds(dst, PIECE)],
                                     sem.at[0, s])

    def c_copy(s, src, dst):
        return pltpu.make_async_copy(c_scr.at[s, pl.ds(src, PIECE)], cws_out.at[pl.ds(dst, PIECE)],
                                     sem.at[1, s])

    def start(src, dst):
        x_copy(slot, src, dst).start()
        c_copy(slot, src, dst).start()

    _piece_copies(off_ref, pc_ref, i, start)

    def wait_tile(tile, s):
        def wait(j, carry):
            x_copy(s, 0, 0).wait()
            c_copy(s, 0, 0).wait()
            return carry

        lax.fori_loop(0, _piece_count(pc_ref, tile), wait, 0)

    @pl.when(i > 0)
    def _():
        wait_tile(i - 1, 1 - slot)

    @pl.when(i == RT_NT - 1)
    def _():
        wait_tile(i, slot)


def _dispatch(chunk_off, pc, h2, info, cw8, xs0, cws0):
    grid_spec = pltpu.PrefetchScalarGridSpec(
        num_scalar_prefetch=2,
        grid=(RT_NT,),
        in_specs=[pl.BlockSpec((RT_TM, D), lambda i, o, p: (i, 0)),
                  pl.BlockSpec((RT_TM, ROUTE_W), lambda i, o, p: (i, 0)),
                  pl.BlockSpec((RT_TM, ROUTE_W), lambda i, o, p: (i, 0)),
                  pl.BlockSpec(memory_space=pl.ANY),
                  pl.BlockSpec(memory_space=pl.ANY)],
        out_specs=(pl.BlockSpec(memory_space=pl.ANY), pl.BlockSpec(memory_space=pl.ANY)),
        scratch_shapes=[pltpu.VMEM((2, R_LOC, D), BF16),
                        pltpu.VMEM((2, R_LOC, ROUTE_W), F32),
                        pltpu.SemaphoreType.DMA((2, 2))])
    return pl.pallas_call(
        _dispatch_kernel,
        out_shape=(jax.ShapeDtypeStruct((R_MAX, D), BF16),
                   jax.ShapeDtypeStruct((R_MAX, ROUTE_W), F32)),
        grid_spec=grid_spec,
        input_output_aliases={5: 0, 6: 1},
        compiler_params=_cparams(("arbitrary",)),
        name="dispatch",
    )(chunk_off, pc, h2, info, cw8, xs0, cws0)


def _expert_kernel(tg_ref, tv_ref, x_ref, cw_ref, wg_ref, wu_ref, wd_ref, y_ref):
    del tg_ref
    k = pl.program_id(0)

    @pl.when(tv_ref[k] == 0)
    def _():
        y_ref[...] = jnp.zeros_like(y_ref)

    @pl.when(tv_ref[k] != 0)
    def _():
        x = x_ref[...]
        cw = cw_ref[...]
        lane = lax.broadcasted_iota(jnp.int32, cw.shape, 1)
        acc = jnp.zeros(y_ref.shape, F32)
        for j in range(EPG):
            mine = jnp.logical_and((lane & (EPG - 1)) == j, lane < 3 * EPG)
            w = jnp.sum(jnp.where(mine, cw, 0.0), axis=-1, keepdims=True)
            a = (jax.nn.silu(jnp.dot(x, wg_ref[j], preferred_element_type=F32))
                 * jnp.dot(x, wu_ref[j], preferred_element_type=F32))
            acc = acc + jnp.dot((a * w).astype(BF16), wd_ref[j], preferred_element_type=F32)
        y_ref[...] = acc.astype(BF16)


def _experts(tile_group, tile_valid, xs, cws, wg, wu, wd):
    grid_spec = pltpu.PrefetchScalarGridSpec(
        num_scalar_prefetch=2,
        grid=(EX_NT,),
        in_specs=[pl.BlockSpec((EX_TM, D), lambda k, tg, tv: (k, 0)),
                  pl.BlockSpec((EX_TM, ROUTE_W), lambda k, tg, tv: (k, 0)),
                  pl.BlockSpec((EPG, D, FF), lambda k, tg, tv: (tg[k], 0, 0)),
                  pl.BlockSpec((EPG, D, FF), lambda k, tg, tv: (tg[k], 0, 0)),
                  pl.BlockSpec((EPG, FF, D), lambda k, tg, tv: (tg[k], 0, 0))],
        out_specs=pl.BlockSpec((EX_TM, D), lambda k, tg, tv: (k, 0)))
    return pl.pallas_call(
        _expert_kernel,
        out_shape=jax.ShapeDtypeStruct((R_MAX, D), BF16),
        grid_spec=grid_spec,
        compiler_params=_cparams(("arbitrary",)),
        name="experts",
    )(tile_group, tile_valid, xs, cws, wg, wu, wd)


CB_NPT = N_PROMPT // RT_TM


def _combine_kernel(off_ref, pc_ref, info_ref, x1_ref, mod_ref, g_ref, ys_hbm,
                    yp_ref, ysm_ref, stage, sem):
    i = pl.program_id(0)
    slot = i % 2

    def copy(s, dst, src):
        return pltpu.make_async_copy(ys_hbm.at[pl.ds(src, PIECE)], stage.at[s, pl.ds(dst, PIECE)],
                                     sem.at[s])

    def fetch(tile, s):
        _piece_copies(off_ref, pc_ref, tile, lambda loc, glob: copy(s, loc, glob).start())

    @pl.when(i == 0)
    def _():
        stage[...] = jnp.zeros_like(stage)
        fetch(0, 0)

    @pl.when(i + 1 < RT_NT)
    def _():
        fetch(i + 1, 1 - slot)

    def wait(j, carry):
        copy(slot, 0, 0).wait()
        return carry

    lax.fori_loop(0, _piece_count(pc_ref, i), wait, 0)

    dest = info_ref[...][:, 1:2]
    col = lax.broadcasted_iota(jnp.int32, (RT_TM, R_STAGE), 1).astype(F32)
    moe = jnp.dot((col == dest).astype(BF16), stage[slot], preferred_element_type=F32)
    mod = mod_ref[...]
    out = _rms(x1_ref[...] + mod[:, 5 * D:6 * D] * moe) * g_ref[...]

    @pl.when(i < CB_NPT)
    def _():
        yp_ref[...] = out

    @pl.when(i >= CB_NPT)
    def _():
        ysm_ref[...] = out


def _combine(chunk_off, pc, info, x1, mod3, fg, ys):
    npt = CB_NPT
    grid_spec = pltpu.PrefetchScalarGridSpec(
        num_scalar_prefetch=2,
        grid=(RT_NT,),
        in_specs=[pl.BlockSpec((RT_TM, ROUTE_W), lambda i, o, p: (i, 0)),
                  pl.BlockSpec((RT_TM, D), lambda i, o, p: (i, 0)),
                  pl.BlockSpec((None, 1, 6 * D), lambda i, o, p: (_mod_row(RT_TM)(i), 0, 0)),
                  pl.BlockSpec((1, D), lambda i, o, p: (0, 0)),
                  pl.BlockSpec(memory_space=pl.ANY)],
        out_specs=(pl.BlockSpec((RT_TM, D), lambda i, o, p: (jnp.minimum(i, npt - 1), 0)),
                   pl.BlockSpec((RT_TM, D), lambda i, o, p: (jnp.maximum(i - npt, 0), 0))),
        scratch_shapes=[pltpu.VMEM((2, R_STAGE, D), BF16),
                        pltpu.SemaphoreType.DMA((2,))])
    return pl.pallas_call(
        _combine_kernel,
        out_shape=(jax.ShapeDtypeStruct((N_PROMPT, D), F32),
                   jax.ShapeDtypeStruct((N_SAMPLE, D), F32)),
        grid_spec=grid_spec,
        compiler_params=_cparams(("arbitrary",)),
        name="combine",
    )(chunk_off, pc, info, x1, mod3, fg, ys)


def kernel(x_prompt, x_sample, cache_diff_k, cache_diff_v, state_ret_fwd, state_ret_bwd, c, c_ctx,
           w_ada, b_ada, norm1_g, norm2_g, w_in, ret_decay_fwd, ret_decay_bwd, ret_norm_g,
           diff_lambda_q1, diff_lambda_k1, diff_lambda_q2, diff_lambda_k2, diff_subln_g,
           w_ret_o, w_diff_o, w_o, router_group_w, router_group_b, router_expert_w, router_expert_b,
           moe_w_gate, moe_w_up, moe_w_down, final_norm_g):
    l = 0
    lam_init = 0.8 - 0.6 * math.exp(-0.3 * l)
    lam = (jnp.exp(jnp.sum(diff_lambda_q1[l].astype(F32) * diff_lambda_k1[l].astype(F32)))
           - jnp.exp(jnp.sum(diff_lambda_q2[l].astype(F32) * diff_lambda_k2[l].astype(F32))) + lam_init)
    attn_scal = jnp.stack([lam, jnp.float32(1.0 - lam_init)]).astype(F32)
    lg = jnp.stack([jax.nn.log_sigmoid(ret_decay_fwd[l].astype(F32)),
                    jax.nn.log_sigmoid(ret_decay_bwd[l].astype(F32))])

    xp = x_prompt.reshape(N_PROMPT, D)
    xs = x_sample.reshape(N_SAMPLE, D)
    cvec = jnp.concatenate([c_ctx[None, :], c, jnp.zeros((3, D), F32)], axis=0)
    mod3 = _modulation(cvec, w_ada[l], b_ada[l][None, :]).reshape(8, 1, 6 * D)

    proj, kt32, v32 = _inproj(xp, xs, mod3, norm1_g[l][None, :], w_in[l])

    ret_act, s_f, s_b = _retention(lg, proj, state_ret_fwd[:, l], state_ret_bwd[:, l],
                                   ret_norm_g[l][None, :])
    cos_t, sin_t = _rope_tables()
    cache_kt = jnp.transpose(cache_diff_k[:, l], (0, 2, 3, 4, 1)).reshape(4, D, PAST)
    (diff_act, wg_bf, wu_bf, wd_bf, wro_bf, wdo_bf, wo_bf, xs0, cws0) = _attention(
        attn_scal, proj, cache_kt, cache_diff_v, cos_t, sin_t, diff_subln_g[l][None, :],
        (moe_w_gate[l], moe_w_up[l], moe_w_down[l], w_ret_o[l], w_diff_o[l], w_o[l]))

    wr = jnp.concatenate([router_group_w[l], router_expert_w[l],
                          jnp.zeros((D, ROUTE_W - N_GROUPS - N_EXP), F32)], axis=1).astype(BF16)
    br = jnp.concatenate([router_group_b[l], router_expert_b[l],
                          jnp.zeros((ROUTE_W - N_GROUPS - N_EXP,), F32)])[None, :]
    x1, h2, cw8, info, pc_arr = _outproj(ret_act, diff_act, proj, xp, xs, mod3, norm2_g[l][None, :],
                                         wro_bf, wdo_bf, wo_bf, wr, br)
    chunk_off, pc, tile_group, tile_valid = _dispatch_plan(pc_arr)
    xs_sorted, cw_sorted = _dispatch(chunk_off, pc, h2, info, cw8, xs0, cws0)
    y_sorted = _experts(tile_group, tile_valid, xs_sorted, cw_sorted, wg_bf, wu_bf, wd_bf)
    yp, ys = _combine(chunk_off, pc, info, x1, mod3, final_norm_g[None, :], y_sorted)

    return (yp.reshape(16, T_P, D), ys.reshape(4, T_S, D),
            jnp.transpose(kt32.reshape(16, DIFF_H, 2, DIFF_HD, T_P), (0, 4, 1, 2, 3))[:, None],
            v32.reshape(16, 1, T_P, DIFF_H, 2 * DIFF_HD),
            s_f.reshape(16, 1, RET_H, RET_KD, RET_VD), s_b.reshape(16, 1, RET_H, RET_KD, RET_VD))
```

```python
import functools
import math

import jax
import jax.numpy as jnp
import numpy as np
from jax import lax
from jax.experimental import pallas as pl
from jax.experimental.pallas import tpu as pltpu

F32 = jnp.float32
BF16 = jnp.bfloat16

D = 1024
N_PROMPT = 16 * 256
N_SAMPLE = 4 * 1024
N_TOK = N_PROMPT + N_SAMPLE
T_P = 256
T_S = 1024
PAST = 512
GRID_W = 64
RET_H = 4
RET_KD = 128
RET_VD = 256
DIFF_H = 8
DIFF_HD = 64
ROPE_BASE = 10000.0
N_GROUPS = 4
EPG = 8
N_EXP = 32
FF = 256
EPS = 1e-6
IN_W = 8192
LANE = 128
N_CHUNK = IN_W // LANE
C_RQ, C_RK, C_RV, C_RG, C_DQ, C_DK, C_DV, C_GR, C_GD = 0, 4, 8, 16, 24, 32, 40, 48, 56
ROUTE_W = 128
SEQ_BLK = 1024
N_PBLK = N_PROMPT // SEQ_BLK
VMEM_LIMIT = 56 * 1024 * 1024


def _cparams(sem):
    return pltpu.CompilerParams(dimension_semantics=sem, vmem_limit_bytes=VMEM_LIMIT)


def _mod_row(tile_rows):
    def f(i):
        start = i * tile_rows
        return jnp.where(start < N_PROMPT, 0, 1 + (start - N_PROMPT) // T_S)
    return f


def _rms(x):
    return x * lax.rsqrt(jnp.mean(x * x, axis=-1, keepdims=True) + EPS)


def _mod_kernel(c_ref, w_ref, b_ref, o_ref):
    s = jax.nn.silu(c_ref[...])
    o_ref[...] = jnp.dot(s.astype(BF16), w_ref[...].astype(BF16),
                         preferred_element_type=F32) + b_ref[...]


def _modulation(cvec, w_ada, b_ada):
    tn = 1536
    return pl.pallas_call(
        _mod_kernel,
        out_shape=jax.ShapeDtypeStruct((8, 6 * D), F32),
        grid=(6 * D // tn,),
        in_specs=[pl.BlockSpec((8, D), lambda j: (0, 0)),
                  pl.BlockSpec((D, tn), lambda j: (0, j)),
                  pl.BlockSpec((1, tn), lambda j: (0, j))],
        out_specs=pl.BlockSpec((8, tn), lambda j: (0, j)),
        compiler_params=_cparams(("arbitrary",)),
        name="mod",
    )(cvec, w_ada, b_ada)


IP_TM = 512
IP_TN = 2048
IP_NPT = N_PROMPT // IP_TM


IP_KV_TILE = C_DK * LANE // IP_TN
IP_SPT = IP_TM // T_P


def _inproj_kernel(xp_ref, xs_ref, mod_ref, n1_ref, w_ref, proj_ref, kt_ref, v32_ref, w_scr):
    j = pl.program_id(0)
    i = pl.program_id(1)

    @pl.when(i == 0)
    def _():
        w_scr[...] = w_ref[...].astype(BF16)

    x = jnp.where(i < IP_NPT, xp_ref[...], xs_ref[...])
    mod = mod_ref[...]
    h = (_rms(x) * n1_ref[...] * (1.0 + mod[:, D:2 * D]) + mod[:, 0:D]).astype(BF16)
    acc = jnp.dot(h, w_scr[...], preferred_element_type=F32)
    for c in range(IP_TN // LANE):
        proj_ref[c] = acc[:, c * LANE:(c + 1) * LANE].astype(BF16)

    @pl.when(jnp.logical_and(j == IP_KV_TILE, i < IP_NPT))
    def _():
        for s in range(IP_SPT):
            kt_ref[s] = acc[s * T_P:(s + 1) * T_P, :D].T
        v32_ref[...] = acc[:, D:]


def _inproj(xp, xs, mod3, n1, w_in):
    npt = IP_NPT
    cpt = IP_TN // LANE

    def kv_tile(j, i):
        return jnp.where(j < IP_KV_TILE, 0, jnp.where(j == IP_KV_TILE, jnp.minimum(i, npt - 1), npt - 1))

    return pl.pallas_call(
        _inproj_kernel,
        out_shape=(jax.ShapeDtypeStruct((N_CHUNK, N_TOK, LANE), BF16),
                   jax.ShapeDtypeStruct((N_PROMPT // T_P, D, T_P), F32),
                   jax.ShapeDtypeStruct((N_PROMPT, D), F32)),
        grid=(IN_W // IP_TN, N_TOK // IP_TM),
        in_specs=[pl.BlockSpec((IP_TM, D), lambda j, i: (jnp.minimum(i, npt - 1), 0)),
                  pl.BlockSpec((IP_TM, D), lambda j, i: (jnp.maximum(i - npt, 0), 0)),
                  pl.BlockSpec((None, 1, 6 * D), lambda j, i: (_mod_row(IP_TM)(i), 0, 0)),
                  pl.BlockSpec((1, D), lambda j, i: (0, 0)),
                  pl.BlockSpec((D, IP_TN), lambda j, i: (0, j))],
        out_specs=(pl.BlockSpec((cpt, IP_TM, LANE), lambda j, i: (j, i, 0)),
                   pl.BlockSpec((IP_SPT, D, T_P), lambda j, i: (kv_tile(j, i), 0, 0)),
                   pl.BlockSpec((IP_TM, D), lambda j, i: (kv_tile(j, i), 0))),
        scratch_shapes=[pltpu.VMEM((D, IP_TN), BF16)],
        compiler_params=_cparams(("arbitrary", "arbitrary")),
        name="inproj",
    )(xp, xs, mod3, n1, w_in)


def _decay_mask(t, lgf, lgb):
    ii = lax.broadcasted_iota(jnp.int32, (t, t), 0)
    jj = lax.broadcasted_iota(jnp.int32, (t, t), 1)
    rel = (ii - jj).astype(F32)
    e = jnp.exp(jnp.where(rel >= 0.0, lgf, -lgb) * rel)
    return jnp.where(rel == 0.0, 2.0, e) * (RET_KD ** -0.5)


def _ret_kernel(lg_ref, q_ref, k_ref, v_ref, rg_ref, s0f_ref, s0b_ref, g_ref,
                o_ref, sf_ref, sb_ref, dmp_scr, dms_scr):
    h = pl.program_id(0)
    i = pl.program_id(1)
    lgf = lg_ref[0, h]
    lgb = lg_ref[1, h]
    gain = g_ref[...]
    nt = (((1,), (1,)), ((), ()))
    tn = (((0,), (0,)), ((), ()))

    def finish(o, rg):
        d = o - jnp.mean(o, axis=-1, keepdims=True)
        y = d * lax.rsqrt(jnp.mean(d * d, axis=-1, keepdims=True) + EPS) * gain
        return (jax.nn.silu(rg.astype(F32)) * y).astype(BF16)

    def vcat(r):
        return jnp.concatenate([v_ref[0, r, :], v_ref[1, r, :]], axis=1)

    def gcat(r):
        return jnp.concatenate([rg_ref[0, r, :], rg_ref[1, r, :]], axis=1)

    @pl.when(i == 0)
    def _():
        dmp_scr[...] = _decay_mask(T_P, lgf, lgb)

    @pl.when(i == N_PBLK)
    def _():
        dms_scr[...] = _decay_mask(T_S, lgf, lgb)

    @pl.when(i < N_PBLK)
    def _():
        dm = dmp_scr[...]
        t = lax.broadcasted_iota(jnp.int32, (T_P, 1), 0).astype(F32)
        kdf = jnp.exp(lgf * (T_P - 1.0 - t)) * (RET_KD ** -0.5)
        kdb = jnp.exp(lgb * t) * (RET_KD ** -0.5)
        for s in range(SEQ_BLK // T_P):
            r = slice(s * T_P, (s + 1) * T_P)
            q = q_ref[r, :]
            k = k_ref[r, :]
            v = vcat(r)
            sc = lax.dot_general(q, k, nt, preferred_element_type=F32)
            o = jnp.dot((sc * dm).astype(BF16), v, preferred_element_type=F32)
            o_ref[r, :] = finish(o, gcat(r))
            kf = k.astype(F32)
            sf_ref[s] = lax.dot_general((kf * kdf).astype(BF16), v, tn, preferred_element_type=F32)
            sb_ref[s] = lax.dot_general((kf * kdb).astype(BF16), v, tn, preferred_element_type=F32)

    @pl.when(i >= N_PBLK)
    def _():
        r = slice(0, T_S)
        t = lax.broadcasted_iota(jnp.int32, (T_S, 1), 0).astype(F32)
        q = q_ref[...]
        k = k_ref[...]
        v = vcat(r)
        sc = lax.dot_general(q, k, nt, preferred_element_type=F32)
        o = jnp.dot((sc * dms_scr[...]).astype(BF16), v, preferred_element_type=F32)
        qf = q.astype(F32)
        o = o + jnp.dot((qf * jnp.exp(lgf * (t + 1.0))).astype(BF16), s0f_ref[...].astype(BF16),
                        preferred_element_type=F32)
        o = o + jnp.dot((qf * jnp.exp(lgb * (T_S - t))).astype(BF16), s0b_ref[...].astype(BF16),
                        preferred_element_type=F32)
        o_ref[...] = finish(o, gcat(r))


def _retention(lg, proj, s0f, s0b, gnorm):
    nb = N_TOK // SEQ_BLK
    spb = SEQ_BLK // T_P
    smp = lambda i: jnp.maximum(i - N_PBLK, 0)
    pmt = lambda i: jnp.minimum(i, N_PBLK - 1)
    return pl.pallas_call(
        _ret_kernel,
        out_shape=(jax.ShapeDtypeStruct((N_TOK, RET_H * RET_VD), BF16),
                   jax.ShapeDtypeStruct((16, RET_H, RET_KD, RET_VD), F32),
                   jax.ShapeDtypeStruct((16, RET_H, RET_KD, RET_VD), F32)),
        grid=(RET_H, nb),
        in_specs=[pl.BlockSpec(memory_space=pltpu.SMEM),
                  pl.BlockSpec((None, SEQ_BLK, LANE), lambda h, i: (C_RQ + h, i, 0)),
                  pl.BlockSpec((None, SEQ_BLK, LANE), lambda h, i: (C_RK + h, i, 0)),
                  pl.BlockSpec((2, SEQ_BLK, LANE), lambda h, i: (C_RV // 2 + h, i, 0)),
                  pl.BlockSpec((2, SEQ_BLK, LANE), lambda h, i: (C_RG // 2 + h, i, 0)),
                  pl.BlockSpec((None, None, RET_KD, RET_VD), lambda h, i: (smp(i), h, 0, 0)),
                  pl.BlockSpec((None, None, RET_KD, RET_VD), lambda h, i: (smp(i), h, 0, 0)),
                  pl.BlockSpec((1, RET_VD), lambda h, i: (0, h))],
        out_specs=(pl.BlockSpec((SEQ_BLK, RET_VD), lambda h, i: (i, h)),
                   pl.BlockSpec((spb, None, RET_KD, RET_VD), lambda h, i: (pmt(i), h, 0, 0)),
                   pl.BlockSpec((spb, None, RET_KD, RET_VD), lambda h, i: (pmt(i), h, 0, 0))),
        scratch_shapes=[pltpu.VMEM((T_P, T_P), F32), pltpu.VMEM((T_S, T_S), F32)],
        compiler_params=_cparams(("arbitrary", "arbitrary")),
        name="ret",
    )(lg, proj, proj, proj, proj, s0f, s0b, gnorm)


ATT_TQ = 1024


N_SIDE = 3


def _attn_kernel(sc_ref, q_ref, k_ref, v_ref, ckt_ref, cv_ref, cos_ref, sin_ref, g_ref, *rest):
    side_in = rest[:N_SIDE]
    o_ref = rest[N_SIDE]
    side_out = rest[N_SIDE + 1:2 * N_SIDE + 1]
    xs0_ref, cws0_ref, q_scr, k_scr, v_scr = rest[2 * N_SIDE + 1:]
    i = pl.program_id(0)

    def side_jobs():
        for src, dst in zip(side_in, side_out):
            dst[...] = src[...].astype(BF16)
        xs0_ref[...] = jnp.zeros_like(xs0_ref)
        cws0_ref[...] = jnp.zeros_like(cws0_ref)

    lam = sc_ref[0]
    out_scale = sc_ref[1]
    gain = g_ref[...] * out_scale
    lane = lax.broadcasted_iota(jnp.int32, (1, LANE), 1)
    first = lane < DIFF_HD
    nt = (((1,), (1,)), ((), ()))

    def halves(q):
        zero = jnp.zeros_like(q)
        return jnp.where(first, q, zero), jnp.where(first, zero, q)

    def weights(s):
        return jnp.exp(s - jnp.max(s, axis=-1, keepdims=True)).astype(BF16)

    def finish(of0, of1):
        o = of0[:, :LANE] / of0[:, LANE:] - lam * (of1[:, :LANE] / of1[:, LANE:])
        return (_rms(o) * gain).astype(BF16)

    @pl.when(i < N_PBLK)
    def _():
        side_jobs()
        n_seq = SEQ_BLK // T_P
        rows = [slice(s * T_P, (s + 1) * T_P) for s in range(n_seq)]
        q0, q1 = halves((q_ref[...].astype(F32) * (DIFF_HD ** -0.5)).astype(BF16))
        s0 = jnp.concatenate([lax.dot_general(q0[r], k_ref[r, :], nt, preferred_element_type=F32)
                              for r in rows], axis=0)
        s1 = jnp.concatenate([lax.dot_general(q1[r], k_ref[r, :], nt, preferred_element_type=F32)
                              for r in rows], axis=0)
        e0 = weights(s0)
        e1 = weights(s1)
        ones = jnp.ones((T_P, LANE), BF16)
        for r in rows:
            v1 = jnp.concatenate([v_ref[r, :], ones], axis=1)
            o_ref[r, :] = finish(jnp.dot(e0[r], v1, preferred_element_type=F32),
                                 jnp.dot(e1[r], v1, preferred_element_type=F32))

    @pl.when(i >= N_PBLK)
    def _():
        side_jobs()
        cos = cos_ref[...]
        sin = sin_ref[...]
        low = (lax.broadcasted_iota(jnp.int32, (T_S, LANE), 1) & 16) == 0

        def rope(x):
            xs = jnp.where(low, pltpu.roll(x, LANE - 16, 1), pltpu.roll(x, 16, 1))
            return x * cos + xs * sin

        q_scr[...] = (rope(q_ref[...].astype(F32)) * (DIFF_HD ** -0.5)).astype(BF16)
        k_scr[...] = rope(k_ref[...].astype(F32)).astype(BF16)
        ckt = ckt_ref[...].astype(BF16)
        v_scr[0:T_S, 0:LANE] = v_ref[...]
        v_scr[T_S:T_S + PAST, 0:LANE] = cv_ref[:, pl.program_id(1), :].astype(BF16)
        v_scr[:, LANE:2 * LANE] = jnp.ones((T_S + PAST, LANE), BF16)

        def scores(qh):
            return jnp.concatenate([lax.dot_general(qh, k_scr[...], nt, preferred_element_type=F32),
                                    jnp.dot(qh, ckt, preferred_element_type=F32)], axis=1)

        for b in range(T_S // ATT_TQ):
            r = slice(b * ATT_TQ, (b + 1) * ATT_TQ)
            q0, q1 = halves(q_scr[r, :])
            e0 = weights(scores(q0))
            e1 = weights(scores(q1))
            o_ref[r, :] = finish(jnp.dot(e0, v_scr[...], preferred_element_type=F32),
                                 jnp.dot(e1, v_scr[...], preferred_element_type=F32))


def _attention(scal, proj, cache_k, cache_v, cos_t, sin_t, subln_g, side_weights):
    nb = N_TOK // SEQ_BLK
    smp = lambda i: jnp.maximum(i - N_PBLK, 0)

    n_steps = nb * DIFF_H
    step = lambda i, h: i * DIFF_H + h
    per_exp = n_steps // N_EXP
    exp_slice = lambda i, h: (step(i, h) // per_exp, step(i, h) % per_exp, 0)
    row_slice = lambda i, h: (step(i, h), 0)
    up_spec = pl.BlockSpec((None, D // per_exp, FF), exp_slice)
    down_spec = pl.BlockSpec((None, FF // per_exp, D), exp_slice)
    side_specs = [up_spec, up_spec, down_spec]
    side_shapes = [jax.ShapeDtypeStruct(a.shape, BF16) for a in side_weights]
    zrows = R_MAX // n_steps

    return pl.pallas_call(
        _attn_kernel,
        out_shape=(jax.ShapeDtypeStruct((N_TOK, DIFF_H * 2 * DIFF_HD), BF16),
                   *side_shapes,
                   jax.ShapeDtypeStruct((R_MAX, D), BF16),
                   jax.ShapeDtypeStruct((R_MAX, ROUTE_W), F32)),
        grid=(nb, DIFF_H),
        in_specs=[pl.BlockSpec(memory_space=pltpu.SMEM),
                  pl.BlockSpec((None, SEQ_BLK, LANE), lambda i, h: (C_DQ + h, i, 0)),
                  pl.BlockSpec((None, SEQ_BLK, LANE), lambda i, h: (C_DK + h, i, 0)),
                  pl.BlockSpec((None, SEQ_BLK, LANE), lambda i, h: (C_DV + h, i, 0)),
                  pl.BlockSpec((None, LANE, PAST), lambda i, h: (smp(i), h, 0)),
                  pl.BlockSpec((None, None, PAST, DIFF_H, LANE), lambda i, h: (smp(i), 0, 0, 0, 0)),
                  pl.BlockSpec((T_S, LANE), lambda i, h: (0, 0)),
                  pl.BlockSpec((T_S, LANE), lambda i, h: (0, 0)),
                  pl.BlockSpec((1, LANE), lambda i, h: (0, 0)),
                  *side_specs],
        out_specs=(pl.BlockSpec((SEQ_BLK, LANE), lambda i, h: (i, h)),
                   *side_specs,
                   pl.BlockSpec((zrows, D), row_slice),
                   pl.BlockSpec((zrows, ROUTE_W), row_slice)),
        scratch_shapes=[pltpu.VMEM((T_S, LANE), BF16),
                        pltpu.VMEM((T_S, LANE), BF16),
                        pltpu.VMEM((T_S + PAST, 2 * LANE), BF16)],
        compiler_params=_cparams(("arbitrary", "arbitrary")),
        name="attn",
    )(scal, proj, proj, proj, cache_k, cache_v, cos_t, sin_t, subln_g, *side_weights)


def _rope_tables():
    n_rows = T_S // GRID_W
    row = np.repeat(np.arange(n_rows), GRID_W).astype(np.float64)
    col = np.tile(np.arange(GRID_W), n_rows).astype(np.float64)
    n_freq = DIFF_HD // 4
    inv = ROPE_BASE ** (-np.arange(n_freq, dtype=np.float64) / n_freq)

    def axis_tables(pos):
        ang = pos[:, None] * inv[None, :]
        c = np.cos(ang)
        s = np.sin(ang)
        return np.concatenate([c, c], axis=-1), np.concatenate([-s, s], axis=-1)

    cr, sr = axis_tables(row)
    cc, sc = axis_tables(col)
    cos_h = np.concatenate([cr, cc], axis=-1)
    sin_h = np.concatenate([sr, sc], axis=-1)
    return (jnp.asarray(np.concatenate([cos_h, cos_h], axis=-1), F32),
            jnp.asarray(np.concatenate([sin_h, sin_h], axis=-1), F32))


OP_TM = 512
OP_NPT = N_PROMPT // OP_TM


def _outproj_kernel(ra_ref, da_ref, gr_ref, gd_ref, xp_ref, xs_ref, mod_ref, n2_ref,
                    wro32_ref, wdo32_ref, wo32_ref, rgw_ref, rew_ref, br_ref,
                    x1_ref, h2_ref, cw8_ref, info_ref, pc_ref,
                    m_scr, wro_ref, wdo_ref, wo_ref, wr_ref):
    i = pl.program_id(0)

    @pl.when(i == 0)
    def _():
        wro_ref[...] = wro32_ref[...].astype(BF16)
        wdo_ref[...] = wdo32_ref[...].astype(BF16)
        wo_ref[...] = wo32_ref[...].astype(BF16)
        pad = jnp.zeros((D, ROUTE_W - N_GROUPS - N_EXP), F32)
        wr_ref[...] = jnp.concatenate([rgw_ref[...], rew_ref[...], pad], axis=1).astype(BF16)

    ret_out = jnp.dot(ra_ref[...], wro_ref[...], preferred_element_type=F32)
    diff_out = jnp.dot(da_ref[...], wdo_ref[...], preferred_element_type=F32)
    for c in range(D // LANE):
        sl = slice(c * LANE, (c + 1) * LANE)
        m = (jax.nn.sigmoid(gr_ref[c].astype(F32)) * ret_out[:, sl]
             + jax.nn.sigmoid(gd_ref[c].astype(F32)) * diff_out[:, sl])
        m_scr[:, sl] = m.astype(BF16)
    mix = jnp.dot(m_scr[...], wo_ref[...], preferred_element_type=F32)
    mod = mod_ref[...]

    x1 = jnp.where(i < OP_NPT, xp_ref[...], xs_ref[...]) + mod[:, 2 * D:3 * D] * mix
    x1_ref[...] = x1
    h2 = (_rms(x1) * n2_ref[...] * (1.0 + mod[:, 4 * D:5 * D]) + mod[:, 3 * D:4 * D]).astype(BF16)
    h2_ref[...] = h2
    logits = jnp.dot(h2, wr_ref[...], preferred_element_type=F32) + br_ref[...]
    cw8_ref[...], info_ref[...], pc_ref[...] = _route_rows(logits)


def _outproj(ret_act, diff_act, proj, xp, xs, mod3, n2, wro, wdo, wo, rgw, rew, br):
    assert OP_TM == RT_TM
    npt = OP_NPT
    full = lambda i: (0, 0)
    once = pl.Buffered(1)
    return pl.pallas_call(
        _outproj_kernel,
        out_shape=(jax.ShapeDtypeStruct((N_TOK, D), F32),
                   jax.ShapeDtypeStruct((N_TOK, D), BF16),
                   jax.ShapeDtypeStruct((N_TOK, ROUTE_W), F32),
                   jax.ShapeDtypeStruct((N_TOK, ROUTE_W), F32),
                   jax.ShapeDtypeStruct((RT_NT, 1, ROUTE_W), F32)),
        grid=(N_TOK // OP_TM,),
        in_specs=[pl.BlockSpec((OP_TM, D), lambda i: (i, 0)),
                  pl.BlockSpec((OP_TM, D), lambda i: (i, 0)),
                  pl.BlockSpec((8, OP_TM, LANE), lambda i: (C_GR // 8, i, 0)),
                  pl.BlockSpec((8, OP_TM, LANE), lambda i: (C_GD // 8, i, 0)),
                  pl.BlockSpec((OP_TM, D), lambda i: (jnp.minimum(i, npt - 1), 0)),
                  pl.BlockSpec((OP_TM, D), lambda i: (jnp.maximum(i - npt, 0), 0)),
                  pl.BlockSpec((None, 1, 6 * D), lambda i: (_mod_row(OP_TM)(i), 0, 0)),
                  pl.BlockSpec((1, D), full),
                  pl.BlockSpec((D, D), full, pipeline_mode=once),
                  pl.BlockSpec((D, D), full, pipeline_mode=once),
                  pl.BlockSpec((D, D), full, pipeline_mode=once),
                  pl.BlockSpec((D, N_GROUPS), full, pipeline_mode=once),
                  pl.BlockSpec((D, N_EXP), full, pipeline_mode=once),
                  pl.BlockSpec((1, ROUTE_W), full)],
        out_specs=(pl.BlockSpec((OP_TM, D), lambda i: (i, 0)),
                   pl.BlockSpec((OP_TM, D), lambda i: (i, 0)),
                   pl.BlockSpec((OP_TM, ROUTE_W), lambda i: (i, 0)),
                   pl.BlockSpec((OP_TM, ROUTE_W), lambda i: (i, 0)),
                   pl.BlockSpec((None, 1, ROUTE_W), lambda i: (i, 0, 0))),
        scratch_shapes=[pltpu.VMEM((OP_TM, D), BF16),
                        pltpu.VMEM((D, D), BF16), pltpu.VMEM((D, D), BF16), pltpu.VMEM((D, D), BF16),
                        pltpu.VMEM((D, ROUTE_W), BF16)],
        compiler_params=_cparams(("arbitrary",)),
        name="outproj",
    )(ret_act, diff_act, proj, proj, xp, xs, mod3, n2, wro, wdo, wo, rgw, rew, br)


RT_TM = 512
RT_NT = N_TOK // RT_TM
PIECE = 16
R_LOC = RT_TM + N_GROUPS * PIECE
R_STAGE = 640
EX_TM = 512
R_MAX = 11264
EX_NT = R_MAX // EX_TM


def _route_rows(lg):
    lane = lax.broadcasted_iota(jnp.int32, lg.shape, 1)
    neg = jnp.float32(-jnp.inf)
    big = jnp.int32(ROUTE_W)

    def first_where(cond):
        return jnp.min(jnp.where(cond, lane, big), axis=-1, keepdims=True)

    is_g = lane < N_GROUPS
    gl = jnp.where(is_g, lg, neg)
    gmax = jnp.max(gl, axis=-1, keepdims=True)
    gsum = jnp.sum(jnp.where(is_g, jnp.exp(lg - gmax), 0.0), axis=-1, keepdims=True)
    p_top = 1.0 / gsum
    g_idx = first_where(gl == gmax)
    in_grp = jnp.logical_and(lane >= N_GROUPS, ((lane - N_GROUPS) >> 3) == g_idx)
    el = jnp.where(in_grp, lg, neg)
    emax = jnp.max(el, axis=-1, keepdims=True)
    ee = jnp.where(in_grp, jnp.exp(lg - emax), 0.0)
    ep = ee / jnp.sum(ee, axis=-1, keepdims=True)
    ep = jnp.where(in_grp, ep, -1.0)
    e1 = jnp.max(ep, axis=-1, keepdims=True)
    i1 = first_where(ep == e1)
    ep2 = jnp.where(lane == i1, -1.0, ep)
    e2 = jnp.max(ep2, axis=-1, keepdims=True)
    i2 = first_where(ep2 == e2)
    den = e1 + e2
    cw = (jnp.where(lane == i1, p_top * e1 / den, 0.0)
          + jnp.where(lane == i2, p_top * e2 / den, 0.0))
    cw8 = jnp.zeros_like(cw)
    for g in range(N_GROUPS):
        cw8 = cw8 + jnp.where(g_idx == g, pltpu.roll(cw, LANE - (N_GROUPS + EPG * g), 1), 0.0)

    onehot = (lane == g_idx).astype(F32)
    ii = lax.broadcasted_iota(jnp.int32, (RT_TM, RT_TM), 0)
    jj = lax.broadcasted_iota(jnp.int32, (RT_TM, RT_TM), 1)
    earlier = (jj < ii).astype(BF16)
    prefix = jnp.dot(earlier, onehot.astype(BF16), preferred_element_type=F32)
    cnt = jnp.sum(onehot, axis=0, keepdims=True)
    pc = jnp.floor((cnt + (PIECE - 1.0)) * (1.0 / PIECE)) * PIECE
    lo = pltpu.roll(pc, 1, 1) + pltpu.roll(pc, 2, 1) + pltpu.roll(pc, 3, 1)
    dest = jnp.sum(onehot * (prefix + lo), axis=-1, keepdims=True)
    info = jnp.where(lane == 0, g_idx.astype(F32), jnp.where(lane == 1, dest, 0.0))
    return cw8, info, pc


def _dispatch_plan(pc_arr):
    pc = pc_arr[:, 0, :N_GROUPS].astype(jnp.int32)
    seg_len = jnp.sum(pc, axis=0)
    seg_pad = (seg_len + (EX_TM - 1)) // EX_TM * EX_TM
    seg_end_pad = jnp.cumsum(seg_pad)
    seg_start = seg_end_pad - seg_pad
    chunk_off = seg_start[None, :] + jnp.cumsum(pc, axis=0) - pc
    tile_start = jnp.arange(EX_NT, dtype=jnp.int32) * EX_TM
    tile_group = jnp.minimum(jnp.sum((tile_start[:, None] >= seg_end_pad[None, :]).astype(jnp.int32), axis=1),
                             N_GROUPS - 1)
    tile_valid = (tile_start < (seg_start + seg_len)[tile_group]).astype(jnp.int32)
    return chunk_off.astype(jnp.int32), pc, tile_group.astype(jnp.int32), tile_valid


def _piece_copies(off_ref, pc_ref, tile, make):
    lo = 0
    for g in range(N_GROUPS):
        n = pc_ref[tile, g] // PIECE
        base = off_ref[tile, g]

        def body(j, carry, lo=lo, base=base):
            make(pl.multiple_of(lo + j * PIECE, PIECE), pl.multiple_of(base + j * PIECE, PIECE))
            return carry

        lax.fori_loop(0, n, body, 0)
        lo = lo + pc_ref[tile, g]


def _piece_count(pc_ref, tile):
    n = 0
    for g in range(N_GROUPS):
        n = n + pc_ref[tile, g] // PIECE
    return n


def _dispatch_kernel(off_ref, pc_ref, h_ref, info_ref, cw8_ref, xs_in, cws_in,
                     xs_out, cws_out, x_scr, c_scr, sem):
    del xs_in, cws_in
    i = pl.program_id(0)
    slot = i % 2
    dest = info_ref[...].T[1:2, :]
    row = lax.broadcasted_iota(jnp.int32, (R_LOC, RT_TM), 0).astype(F32)
    sel = row == dest
    sel = sel.astype(BF16)
    x_scr[slot] = jnp.dot(sel, h_ref[...], preferred_element_type=F32).astype(BF16)
    cw = cw8_ref[...]
    hi = cw.astype(BF16).astype(F32)
    mid = (cw - hi).astype(BF16).astype(F32)
    low = (cw - hi - mid).astype(BF16).astype(F32)
    pieces = (hi + pltpu.roll(mid, EPG, 1) + pltpu.roll(low, 2 * EPG, 1)).astype(BF16)
    c_scr[slot] = jnp.dot(sel, pieces, preferred_element_type=F32)

    def x_copy(s, src, dst):
        return pltpu.make_async_copy(x_scr.at[s, pl.ds(src, PIECE)], xs_out.at[pl.ds(dst, PIECE)],
                                     sem.at[0, s])

    def c_copy(s, src, dst):
        return pltpu.make_async_copy(c_scr.at[s, pl.ds(src, PIECE)], cws_out.at[pl.ds(dst, PIECE)],
                                     sem.at[1, s])

    def start(src, dst):
        x_copy(slot, src, dst).start()
        c_copy(slot, src, dst).start()

    _piece_copies(off_ref, pc_ref, i, start)

    def wait_tile(tile, s):
        def wait(j, carry):
            x_copy(s, 0, 0).wait()
            c_copy(s, 0, 0).wait()
            return carry

        lax.fori_loop(0, _piece_count(pc_ref, tile), wait, 0)

    @pl.when(i > 0)
    def _():
        wait_tile(i - 1, 1 - slot)

    @pl.when(i == RT_NT - 1)
    def _():
        wait_tile(i, slot)


def _dispatch(chunk_off, pc, h2, info, cw8, xs0, cws0):
    grid_spec = pltpu.PrefetchScalarGridSpec(
        num_scalar_prefetch=2,
        grid=(RT_NT,),
        in_specs=[pl.BlockSpec((RT_TM, D), lambda i, o, p: (i, 0)),
                  pl.BlockSpec((RT_TM, ROUTE_W), lambda i, o, p: (i, 0)),
                  pl.BlockSpec((RT_TM, ROUTE_W), lambda i, o, p: (i, 0)),
                  pl.BlockSpec(memory_space=pl.ANY),
                  pl.BlockSpec(memory_space=pl.ANY)],
        out_specs=(pl.BlockSpec(memory_space=pl.ANY), pl.BlockSpec(memory_space=pl.ANY)),
        scratch_shapes=[pltpu.VMEM((2, R_LOC, D), BF16),
                        pltpu.VMEM((2, R_LOC, ROUTE_W), F32),
                        pltpu.SemaphoreType.DMA((2, 2))])
    return pl.pallas_call(
        _dispatch_kernel,
        out_shape=(jax.ShapeDtypeStruct((R_MAX, D), BF16),
                   jax.ShapeDtypeStruct((R_MAX, ROUTE_W), F32)),
        grid_spec=grid_spec,
        input_output_aliases={5: 0, 6: 1},
        compiler_params=_cparams(("arbitrary",)),
        name="dispatch",
    )(chunk_off, pc, h2, info, cw8, xs0, cws0)


def _expert_kernel(tg_ref, tv_ref, x_ref, cw_ref, wg_ref, wu_ref, wd_ref, y_ref):
    del tg_ref
    k = pl.program_id(0)

    @pl.when(tv_ref[k] == 0)
    def _():
        y_ref[...] = jnp.zeros_like(y_ref)

    @pl.when(tv_ref[k] != 0)
    def _():
        x = x_ref[...]
        cw = cw_ref[...]
        lane = lax.broadcasted_iota(jnp.int32, cw.shape, 1)
        acc = jnp.zeros(y_ref.shape, F32)
        for j in range(EPG):
            mine = jnp.logical_and((lane & (EPG - 1)) == j, lane < 3 * EPG)
            w = jnp.sum(jnp.where(mine, cw, 0.0), axis=-1, keepdims=True)
            a = (jax.nn.silu(jnp.dot(x, wg_ref[j], preferred_element_type=F32))
                 * jnp.dot(x, wu_ref[j], preferred_element_type=F32))
            acc = acc + jnp.dot((a * w).astype(BF16), wd_ref[j], preferred_element_type=F32)
        y_ref[...] = acc.astype(BF16)


def _experts(tile_group, tile_valid, xs, cws, wg, wu, wd):
    grid_spec = pltpu.PrefetchScalarGridSpec(
        num_scalar_prefetch=2,
        grid=(EX_NT,),
        in_specs=[pl.BlockSpec((EX_TM, D), lambda k, tg, tv: (k, 0)),
                  pl.BlockSpec((EX_TM, ROUTE_W), lambda k, tg, tv: (k, 0)),
                  pl.BlockSpec((EPG, D, FF), lambda k, tg, tv: (tg[k], 0, 0)),
                  pl.BlockSpec((EPG, D, FF), lambda k, tg, tv: (tg[k], 0, 0)),
                  pl.BlockSpec((EPG, FF, D), lambda k, tg, tv: (tg[k], 0, 0))],
        out_specs=pl.BlockSpec((EX_TM, D), lambda k, tg, tv: (k, 0)))
    return pl.pallas_call(
        _expert_kernel,
        out_shape=jax.ShapeDtypeStruct((R_MAX, D), BF16),
        grid_spec=grid_spec,
        compiler_params=_cparams(("arbitrary",)),
        name="experts",
    )(tile_group, tile_valid, xs, cws, wg, wu, wd)


CB_NPT = N_PROMPT // RT_TM


def _combine_kernel(off_ref, pc_ref, info_ref, x1_ref, mod_ref, g_ref, ys_hbm,
                    yp_ref, ysm_ref, stage, sem):
    i = pl.program_id(0)
    slot = i % 2

    def copy(s, dst, src):
        return pltpu.make_async_copy(ys_hbm.at[pl.ds(src, PIECE)], stage.at[s, pl.ds(dst, PIECE)],
                                     sem.at[s])

    def fetch(tile, s):
        _piece_copies(off_ref, pc_ref, tile, lambda loc, glob: copy(s, loc, glob).start())

    @pl.when(i == 0)
    def _():
        stage[...] = jnp.zeros_like(stage)
        fetch(0, 0)

    @pl.when(i + 1 < RT_NT)
    def _():
        fetch(i + 1, 1 - slot)

    def wait(j, carry):
        copy(slot, 0, 0).wait()
        return carry

    lax.fori_loop(0, _piece_count(pc_ref, i), wait, 0)

    dest = info_ref[...][:, 1:2]
    col = lax.broadcasted_iota(jnp.int32, (RT_TM, R_STAGE), 1).astype(F32)
    moe = jnp.dot((col == dest).astype(BF16), stage[slot], preferred_element_type=F32)
    mod = mod_ref[...]
    out = _rms(x1_ref[...] + mod[:, 5 * D:6 * D] * moe) * g_ref[...]

    @pl.when(i < CB_NPT)
    def _():
        yp_ref[...] = out

    @pl.when(i >= CB_NPT)
    def _():
        ysm_ref[...] = out


def _combine(chunk_off, pc, info, x1, mod3, fg, ys):
    npt = CB_NPT
    grid_spec = pltpu.PrefetchScalarGridSpec(
        num_scalar_prefetch=2,
        grid=(RT_NT,),
        in_specs=[pl.BlockSpec((RT_TM, ROUTE_W), lambda i, o, p: (i, 0)),
                  pl.BlockSpec((RT_TM, D), lambda i, o, p: (i, 0)),
                  pl.BlockSpec((None, 1, 6 * D), lambda i, o, p: (_mod_row(RT_TM)(i), 0, 0)),
                  pl.BlockSpec((1, D), lambda i, o, p: (0, 0)),
                  pl.BlockSpec(memory_space=pl.ANY)],
        out_specs=(pl.BlockSpec((RT_TM, D), lambda i, o, p: (jnp.minimum(i, npt - 1), 0)),
                   pl.BlockSpec((RT_TM, D), lambda i, o, p: (jnp.maximum(i - npt, 0), 0))),
        scratch_shapes=[pltpu.VMEM((2, R_STAGE, D), BF16),
                        pltpu.SemaphoreType.DMA((2,))])
    return pl.pallas_call(
        _combine_kernel,
        out_shape=(jax.ShapeDtypeStruct((N_PROMPT, D), F32),
                   jax.ShapeDtypeStruct((N_SAMPLE, D), F32)),
        grid_spec=grid_spec,
        compiler_params=_cparams(("arbitrary",)),
        name="combine",
    )(chunk_off, pc, info, x1, mod3, fg, ys)


def kernel(x_prompt, x_sample, cache_diff_k, cache_diff_v, state_ret_fwd, state_ret_bwd, c, c_ctx,
           w_ada, b_ada, norm1_g, norm2_g, w_in, ret_decay_fwd, ret_decay_bwd, ret_norm_g,
           diff_lambda_q1, diff_lambda_k1, diff_lambda_q2, diff_lambda_k2, diff_subln_g,
           w_ret_o, w_diff_o, w_o, router_group_w, router_group_b, router_expert_w, router_expert_b,
           moe_w_gate, moe_w_up, moe_w_down, final_norm_g):
    l = 0
    lam_init = 0.8 - 0.6 * math.exp(-0.3 * l)
    lam = (jnp.exp(jnp.sum(diff_lambda_q1[l].astype(F32) * diff_lambda_k1[l].astype(F32)))
           - jnp.exp(jnp.sum(diff_lambda_q2[l].astype(F32) * diff_lambda_k2[l].astype(F32))) + lam_init)
    attn_scal = jnp.stack([lam, jnp.float32(1.0 - lam_init)]).astype(F32)
    lg = jnp.stack([jax.nn.log_sigmoid(ret_decay_fwd[l].astype(F32)),
                    jax.nn.log_sigmoid(ret_decay_bwd[l].astype(F32))])

    xp = x_prompt.reshape(N_PROMPT, D)
    xs = x_sample.reshape(N_SAMPLE, D)
    cvec = jnp.concatenate([c_ctx[None, :], c, jnp.zeros((3, D), F32)], axis=0)
    mod3 = _modulation(cvec, w_ada[l], b_ada[l][None, :]).reshape(8, 1, 6 * D)

    proj, kt32, v32 = _inproj(xp, xs, mod3, norm1_g[l][None, :], w_in[l])

    ret_act, s_f, s_b = _retention(lg, proj, state_ret_fwd[:, l], state_ret_bwd[:, l],
                                   ret_norm_g[l][None, :])
    cos_t, sin_t = _rope_tables()
    cache_kt = jnp.transpose(cache_diff_k[:, l], (0, 2, 3, 4, 1)).reshape(4, D, PAST)
    (diff_act, wg_bf, wu_bf, wd_bf, xs0, cws0) = _attention(
        attn_scal, proj, cache_kt, cache_diff_v, cos_t, sin_t, diff_subln_g[l][None, :],
        (moe_w_gate[l], moe_w_up[l], moe_w_down[l]))

    br = jnp.concatenate([router_group_b[l], router_expert_b[l],
                          jnp.zeros((ROUTE_W - N_GROUPS - N_EXP,), F32)])[None, :]
    x1, h2, cw8, info, pc_arr = _outproj(ret_act, diff_act, proj, xp, xs, mod3, norm2_g[l][None, :],
                                         w_ret_o[l], w_diff_o[l], w_o[l],
                                         router_group_w[l], router_expert_w[l], br)
    chunk_off, pc, tile_group, tile_valid = _dispatch_plan(pc_arr)
    xs_sorted, cw_sorted = _dispatch(chunk_off, pc, h2, info, cw8, xs0, cws0)
    y_sorted = _experts(tile_group, tile_valid, xs_sorted, cw_sorted, wg_bf, wu_bf, wd_bf)
    yp, ys = _combine(chunk_off, pc, info, x1, mod3, final_norm_g[None, :], y_sorted)

    return (yp.reshape(16, T_P, D), ys.reshape(4, T_S, D),
            jnp.transpose(kt32.reshape(16, DIFF_H, 2, DIFF_HD, T_P), (0, 4, 1, 2, 3))[:, None],
            v32.reshape(16, 1, T_P, DIFF_H, 2 * DIFF_HD),
            s_f.reshape(16, 1, RET_H, RET_KD, RET_VD), s_b.reshape(16, 1, RET_H, RET_KD, RET_VD))
```

```python
import functools
import math

import jax
import jax.numpy as jnp
import numpy as np
from jax import lax
from jax.experimental import pallas as pl
from jax.experimental.pallas import tpu as pltpu

F32 = jnp.float32
BF16 = jnp.bfloat16

D = 1024
N_PROMPT = 16 * 256
N_SAMPLE = 4 * 1024
N_TOK = N_PROMPT + N_SAMPLE
T_P = 256
T_S = 1024
PAST = 512
GRID_W = 64
RET_H = 4
RET_KD = 128
RET_VD = 256
DIFF_H = 8
DIFF_HD = 64
ROPE_BASE = 10000.0
N_GROUPS = 4
EPG = 8
N_EXP = 32
FF = 256
EPS = 1e-6
IN_W = 8192
LANE = 128
N_CHUNK = IN_W // LANE
C_RQ, C_RK, C_RV, C_RG, C_DQ, C_DK, C_DV, C_GR, C_GD = 0, 4, 8, 16, 24, 32, 40, 48, 56
ROUTE_W = 128
SEQ_BLK = 1024
N_PBLK = N_PROMPT // SEQ_BLK
VMEM_LIMIT = 56 * 1024 * 1024


def _cparams(sem):
    return pltpu.CompilerParams(dimension_semantics=sem, vmem_limit_bytes=VMEM_LIMIT)


def _mod_row(tile_rows):
    def f(i):
        start = i * tile_rows
        return jnp.where(start < N_PROMPT, 0, 1 + (start - N_PROMPT) // T_S)
    return f


def _rms(x):
    return x * lax.rsqrt(jnp.mean(x * x, axis=-1, keepdims=True) + EPS)


def _mod_kernel(c_ref, w_ref, b_ref, o_ref):
    s = jax.nn.silu(c_ref[...])
    o_ref[...] = jnp.dot(s.astype(BF16), w_ref[...].astype(BF16),
                         preferred_element_type=F32) + b_ref[...]


def _modulation(cvec, w_ada, b_ada):
    tn = 1536
    return pl.pallas_call(
        _mod_kernel,
        out_shape=jax.ShapeDtypeStruct((8, 6 * D), F32),
        grid=(6 * D // tn,),
        in_specs=[pl.BlockSpec((8, D), lambda j: (0, 0)),
                  pl.BlockSpec((D, tn), lambda j: (0, j)),
                  pl.BlockSpec((1, tn), lambda j: (0, j))],
        out_specs=pl.BlockSpec((8, tn), lambda j: (0, j)),
        compiler_params=_cparams(("arbitrary",)),
        name="mod",
    )(cvec, w_ada, b_ada)


IP_TM = 512
IP_TN = 2048
IP_NPT = N_PROMPT // IP_TM


IP_KV_TILE = C_DK * LANE // IP_TN
IP_SPT = IP_TM // T_P


def _inproj_kernel(xp_ref, xs_ref, mod_ref, n1_ref, w_ref, proj_ref, kt_ref, v32_ref, w_scr):
    j = pl.program_id(0)
    i = pl.program_id(1)

    @pl.when(i == 0)
    def _():
        w_scr[...] = w_ref[...].astype(BF16)

    x = jnp.where(i < IP_NPT, xp_ref[...], xs_ref[...])
    mod = mod_ref[...]
    h = (_rms(x) * n1_ref[...] * (1.0 + mod[:, D:2 * D]) + mod[:, 0:D]).astype(BF16)
    acc = jnp.dot(h, w_scr[...], preferred_element_type=F32)
    for c in range(IP_TN // LANE):
        proj_ref[c] = acc[:, c * LANE:(c + 1) * LANE].astype(BF16)

    @pl.when(jnp.logical_and(j == IP_KV_TILE, i < IP_NPT))
    def _():
        for s in range(IP_SPT):
            kt_ref[s] = acc[s * T_P:(s + 1) * T_P, :D].T
        v32_ref[...] = acc[:, D:]


def _inproj(xp, xs, mod3, n1, w_in):
    npt = IP_NPT
    cpt = IP_TN // LANE

    def kv_tile(j, i):
        return jnp.where(j < IP_KV_TILE, 0, jnp.where(j == IP_KV_TILE, jnp.minimum(i, npt - 1), npt - 1))

    return pl.pallas_call(
        _inproj_kernel,
        out_shape=(jax.ShapeDtypeStruct((N_CHUNK, N_TOK, LANE), BF16),
                   jax.ShapeDtypeStruct((N_PROMPT // T_P, D, T_P), F32),
                   jax.ShapeDtypeStruct((N_PROMPT, D), F32)),
        grid=(IN_W // IP_TN, N_TOK // IP_TM),
        in_specs=[pl.BlockSpec((IP_TM, D), lambda j, i: (jnp.minimum(i, npt - 1), 0)),
                  pl.BlockSpec((IP_TM, D), lambda j, i: (jnp.maximum(i - npt, 0), 0)),
                  pl.BlockSpec((None, 1, 6 * D), lambda j, i: (_mod_row(IP_TM)(i), 0, 0)),
                  pl.BlockSpec((1, D), lambda j, i: (0, 0)),
                  pl.BlockSpec((D, IP_TN), lambda j, i: (0, j))],
        out_specs=(pl.BlockSpec((cpt, IP_TM, LANE), lambda j, i: (j, i, 0)),
                   pl.BlockSpec((IP_SPT, D, T_P), lambda j, i: (kv_tile(j, i), 0, 0)),
                   pl.BlockSpec((IP_TM, D), lambda j, i: (kv_tile(j, i), 0))),
        scratch_shapes=[pltpu.VMEM((D, IP_TN), BF16)],
        compiler_params=_cparams(("arbitrary", "arbitrary")),
        name="inproj",
    )(xp, xs, mod3, n1, w_in)


def _decay_mask(t, lgf, lgb):
    ii = lax.broadcasted_iota(jnp.int32, (t, t), 0)
    jj = lax.broadcasted_iota(jnp.int32, (t, t), 1)
    rel = (ii - jj).astype(F32)
    e = jnp.exp(jnp.where(rel >= 0.0, lgf, -lgb) * rel)
    return jnp.where(rel == 0.0, 2.0, e) * (RET_KD ** -0.5)


def _ret_kernel(lg_ref, q_ref, k_ref, v_ref, rg_ref, s0f_ref, s0b_ref, g_ref,
                o_ref, sf_ref, sb_ref, dmp_scr, dms_scr):
    h = pl.program_id(0)
    i = pl.program_id(1)
    lgf = lg_ref[0, h]
    lgb = lg_ref[1, h]
    gain = g_ref[...]
    nt = (((1,), (1,)), ((), ()))
    tn = (((0,), (0,)), ((), ()))

    def finish(o, rg):
        d = o - jnp.mean(o, axis=-1, keepdims=True)
        y = d * lax.rsqrt(jnp.mean(d * d, axis=-1, keepdims=True) + EPS) * gain
        return (jax.nn.silu(rg.astype(F32)) * y).astype(BF16)

    def vcat(r):
        return jnp.concatenate([v_ref[0, r, :], v_ref[1, r, :]], axis=1)

    def gcat(r):
        return jnp.concatenate([rg_ref[0, r, :], rg_ref[1, r, :]], axis=1)

    @pl.when(i == 0)
    def _():
        dmp_scr[...] = _decay_mask(T_P, lgf, lgb)

    @pl.when(i == N_PBLK)
    def _():
        dms_scr[...] = _decay_mask(T_S, lgf, lgb)

    @pl.when(i < N_PBLK)
    def _():
        dm = dmp_scr[...]
        t = lax.broadcasted_iota(jnp.int32, (T_P, 1), 0).astype(F32)
        kdf = jnp.exp(lgf * (T_P - 1.0 - t)) * (RET_KD ** -0.5)
        kdb = jnp.exp(lgb * t) * (RET_KD ** -0.5)
        for s in range(SEQ_BLK // T_P):
            r = slice(s * T_P, (s + 1) * T_P)
            q = q_ref[r, :]
            k = k_ref[r, :]
            v = vcat(r)
            sc = lax.dot_general(q, k, nt, preferred_element_type=F32)
            o = jnp.dot((sc * dm).astype(BF16), v, preferred_element_type=F32)
            o_ref[r, :] = finish(o, gcat(r))
            kf = k.astype(F32)
            sf_ref[s] = lax.dot_general((kf * kdf).astype(BF16), v, tn, preferred_element_type=F32)
            sb_ref[s] = lax.dot_general((kf * kdb).astype(BF16), v, tn, preferred_element_type=F32)

    @pl.when(i >= N_PBLK)
    def _():
        r = slice(0, T_S)
        t = lax.broadcasted_iota(jnp.int32, (T_S, 1), 0).astype(F32)
        q = q_ref[...]
        k = k_ref[...]
        v = vcat(r)
        sc = lax.dot_general(q, k, nt, preferred_element_type=F32)
        o = jnp.dot((sc * dms_scr[...]).astype(BF16), v, preferred_element_type=F32)
        qf = q.astype(F32)
        o = o + jnp.dot((qf * jnp.exp(lgf * (t + 1.0))).astype(BF16), s0f_ref[...].astype(BF16),
                        preferred_element_type=F32)
        o = o + jnp.dot((qf * jnp.exp(lgb * (T_S - t))).astype(BF16), s0b_ref[...].astype(BF16),
                        preferred_element_type=F32)
        o_ref[...] = finish(o, gcat(r))


def _retention(lg, proj, s0f, s0b, gnorm):
    nb = N_TOK // SEQ_BLK
    spb = SEQ_BLK // T_P
    smp = lambda i: jnp.maximum(i - N_PBLK, 0)
    pmt = lambda i: jnp.minimum(i, N_PBLK - 1)
    return pl.pallas_call(
        _ret_kernel,
        out_shape=(jax.ShapeDtypeStruct((N_TOK, RET_H * RET_VD), BF16),
                   jax.ShapeDtypeStruct((16, RET_H, RET_KD, RET_VD), F32),
                   jax.ShapeDtypeStruct((16, RET_H, RET_KD, RET_VD), F32)),
        grid=(RET_H, nb),
        in_specs=[pl.BlockSpec(memory_space=pltpu.SMEM),
                  pl.BlockSpec((None, SEQ_BLK, LANE), lambda h, i: (C_RQ + h, i, 0)),
                  pl.BlockSpec((None, SEQ_BLK, LANE), lambda h, i: (C_RK + h, i, 0)),
                  pl.BlockSpec((2, SEQ_BLK, LANE), lambda h, i: (C_RV // 2 + h, i, 0)),
                  pl.BlockSpec((2, SEQ_BLK, LANE), lambda h, i: (C_RG // 2 + h, i, 0)),
                  pl.BlockSpec((None, None, RET_KD, RET_VD), lambda h, i: (smp(i), h, 0, 0)),
                  pl.BlockSpec((None, None, RET_KD, RET_VD), lambda h, i: (smp(i), h, 0, 0)),
                  pl.BlockSpec((1, RET_VD), lambda h, i: (0, h))],
        out_specs=(pl.BlockSpec((SEQ_BLK, RET_VD), lambda h, i: (i, h)),
                   pl.BlockSpec((spb, None, RET_KD, RET_VD), lambda h, i: (pmt(i), h, 0, 0)),
                   pl.BlockSpec((spb, None, RET_KD, RET_VD), lambda h, i: (pmt(i), h, 0, 0))),
        scratch_shapes=[pltpu.VMEM((T_P, T_P), F32), pltpu.VMEM((T_S, T_S), F32)],
        compiler_params=_cparams(("arbitrary", "arbitrary")),
        name="ret",
    )(lg, proj, proj, proj, proj, s0f, s0b, gnorm)


ATT_TQ = 1024
ATT_HPS = 2
ATT_QSCALE = (DIFF_HD ** -0.5) * math.log2(math.e)


N_SIDE = 3


def _attn_kernel(sc_ref, q_ref, k_ref, v_ref, ckt_ref, cv_ref, cos_ref, sin_ref, g_ref, *rest):
    side_in = rest[:N_SIDE]
    o_ref = rest[N_SIDE]
    side_out = rest[N_SIDE + 1:2 * N_SIDE + 1]
    xs0_ref, cws0_ref, q_scr, k_scr, v_scr = rest[2 * N_SIDE + 1:]
    i = pl.program_id(0)

    def side_jobs():
        for src, dst in zip(side_in, side_out):
            dst[...] = src[...].astype(BF16)
        xs0_ref[...] = jnp.zeros_like(xs0_ref)
        cws0_ref[...] = jnp.zeros_like(cws0_ref)

    lam = sc_ref[0]
    out_scale = sc_ref[1]
    gain = g_ref[...] * out_scale
    lane = lax.broadcasted_iota(jnp.int32, (1, LANE), 1)
    first = lane < DIFF_HD
    nt = (((1,), (1,)), ((), ()))

    def halves(q):
        zero = jnp.zeros_like(q)
        return jnp.where(first, q, zero), jnp.where(first, zero, q)

    def weights(s):
        return jnp.exp2(s - jnp.max(s, axis=-1, keepdims=True)).astype(BF16)

    def finish(of0, of1):
        o = of0[:, :LANE] / of0[:, LANE:] - lam * (of1[:, :LANE] / of1[:, LANE:])
        return (_rms(o) * gain).astype(BF16)

    @pl.when(i < N_PBLK)
    def _():
        side_jobs()
        n_seq = SEQ_BLK // T_P
        rows = [slice(s * T_P, (s + 1) * T_P) for s in range(n_seq)]
        ones = jnp.ones((T_P, LANE), BF16)
        for hh in range(ATT_HPS):
            cols = slice(hh * LANE, (hh + 1) * LANE)
            q0, q1 = halves((q_ref[hh].astype(F32) * ATT_QSCALE).astype(BF16))
            s0 = jnp.concatenate([lax.dot_general(q0[r], k_ref[hh, r, :], nt, preferred_element_type=F32)
                                  for r in rows], axis=0)
            s1 = jnp.concatenate([lax.dot_general(q1[r], k_ref[hh, r, :], nt, preferred_element_type=F32)
                                  for r in rows], axis=0)
            e0 = weights(s0)
            e1 = weights(s1)
            for r in rows:
                v1 = jnp.concatenate([v_ref[hh, r, :], ones], axis=1)
                o_ref[r, cols] = finish(jnp.dot(e0[r], v1, preferred_element_type=F32),
                                        jnp.dot(e1[r], v1, preferred_element_type=F32))

    @pl.when(i >= N_PBLK)
    def _():
        side_jobs()
        cos = cos_ref[...]
        sin = sin_ref[...]
        low = (lax.broadcasted_iota(jnp.int32, (T_S, LANE), 1) & 16) == 0

        def rope(x):
            xs = jnp.where(low, pltpu.roll(x, LANE - 16, 1), pltpu.roll(x, 16, 1))
            return x * cos + xs * sin

        for hh in range(ATT_HPS):
            cols = slice(hh * LANE, (hh + 1) * LANE)
            head = pl.program_id(1) * ATT_HPS + hh
            q_scr[hh] = (rope(q_ref[hh].astype(F32)) * ATT_QSCALE).astype(BF16)
            k_scr[hh] = rope(k_ref[hh].astype(F32)).astype(BF16)
            ckt = ckt_ref[cols, :].astype(BF16)
            v_scr[hh, 0:T_S, 0:LANE] = v_ref[hh]
            v_scr[hh, T_S:T_S + PAST, 0:LANE] = cv_ref[:, head, :].astype(BF16)
            v_scr[hh, :, LANE:2 * LANE] = jnp.ones((T_S + PAST, LANE), BF16)

            def scores(qh, hh=hh, ckt=ckt):
                return jnp.concatenate([lax.dot_general(qh, k_scr[hh], nt, preferred_element_type=F32),
                                        jnp.dot(qh, ckt, preferred_element_type=F32)], axis=1)

            for b in range(T_S // ATT_TQ):
                r = slice(b * ATT_TQ, (b + 1) * ATT_TQ)
                q0, q1 = halves(q_scr[hh, r, :])
                e0 = weights(scores(q0))
                e1 = weights(scores(q1))
                o_ref[r, cols] = finish(jnp.dot(e0, v_scr[hh], preferred_element_type=F32),
                                        jnp.dot(e1, v_scr[hh], preferred_element_type=F32))


def _attention(scal, proj, cache_k, cache_v, cos_t, sin_t, subln_g, side_weights):
    nb = N_TOK // SEQ_BLK
    smp = lambda i: jnp.maximum(i - N_PBLK, 0)

    n_hp = DIFF_H // ATT_HPS
    n_steps = nb * n_hp
    step = lambda i, h: i * n_hp + h
    per_exp = n_steps // N_EXP
    exp_slice = lambda i, h: (step(i, h) // per_exp, step(i, h) % per_exp, 0)
    row_slice = lambda i, h: (step(i, h), 0)
    up_spec = pl.BlockSpec((None, D // per_exp, FF), exp_slice)
    down_spec = pl.BlockSpec((None, FF // per_exp, D), exp_slice)
    side_specs = [up_spec, up_spec, down_spec]
    side_shapes = [jax.ShapeDtypeStruct(a.shape, BF16) for a in side_weights]
    zrows = R_MAX // n_steps

    return pl.pallas_call(
        _attn_kernel,
        out_shape=(jax.ShapeDtypeStruct((N_TOK, DIFF_H * 2 * DIFF_HD), BF16),
                   *side_shapes,
                   jax.ShapeDtypeStruct((R_MAX, D), BF16),
                   jax.ShapeDtypeStruct((R_MAX, ROUTE_W), F32)),
        grid=(nb, n_hp),
        in_specs=[pl.BlockSpec(memory_space=pltpu.SMEM),
                  pl.BlockSpec((ATT_HPS, SEQ_BLK, LANE), lambda i, h: (C_DQ // ATT_HPS + h, i, 0)),
                  pl.BlockSpec((ATT_HPS, SEQ_BLK, LANE), lambda i, h: (C_DK // ATT_HPS + h, i, 0)),
                  pl.BlockSpec((ATT_HPS, SEQ_BLK, LANE), lambda i, h: (C_DV // ATT_HPS + h, i, 0)),
                  pl.BlockSpec((None, ATT_HPS * LANE, PAST), lambda i, h: (smp(i), h, 0)),
                  pl.BlockSpec((None, None, PAST, DIFF_H, LANE), lambda i, h: (smp(i), 0, 0, 0, 0)),
                  pl.BlockSpec((T_S, LANE), lambda i, h: (0, 0)),
                  pl.BlockSpec((T_S, LANE), lambda i, h: (0, 0)),
                  pl.BlockSpec((1, LANE), lambda i, h: (0, 0)),
                  *side_specs],
        out_specs=(pl.BlockSpec((SEQ_BLK, ATT_HPS * LANE), lambda i, h: (i, h)),
                   *side_specs,
                   pl.BlockSpec((zrows, D), row_slice),
                   pl.BlockSpec((zrows, ROUTE_W), row_slice)),
        scratch_shapes=[pltpu.VMEM((ATT_HPS, T_S, LANE), BF16),
                        pltpu.VMEM((ATT_HPS, T_S, LANE), BF16),
                        pltpu.VMEM((ATT_HPS, T_S + PAST, 2 * LANE), BF16)],
        compiler_params=_cparams(("arbitrary", "arbitrary")),
        name="attn",
    )(scal, proj, proj, proj, cache_k, cache_v, cos_t, sin_t, subln_g, *side_weights)


def _rope_tables():
    n_rows = T_S // GRID_W
    row = np.repeat(np.arange(n_rows), GRID_W).astype(np.float64)
    col = np.tile(np.arange(GRID_W), n_rows).astype(np.float64)
    n_freq = DIFF_HD // 4
    inv = ROPE_BASE ** (-np.arange(n_freq, dtype=np.float64) / n_freq)

    def axis_tables(pos):
        ang = pos[:, None] * inv[None, :]
        c = np.cos(ang)
        s = np.sin(ang)
        return np.concatenate([c, c], axis=-1), np.concatenate([-s, s], axis=-1)

    cr, sr = axis_tables(row)
    cc, sc = axis_tables(col)
    cos_h = np.concatenate([cr, cc], axis=-1)
    sin_h = np.concatenate([sr, sc], axis=-1)
    return (jnp.asarray(np.concatenate([cos_h, cos_h], axis=-1), F32),
            jnp.asarray(np.concatenate([sin_h, sin_h], axis=-1), F32))


OP_TM = 512
OP_NPT = N_PROMPT // OP_TM


def _outproj_kernel(ra_ref, da_ref, gr_ref, gd_ref, xp_ref, xs_ref, mod_ref, n2_ref,
                    wro32_ref, wdo32_ref, wo32_ref, rgw_ref, rew_ref, br_ref,
                    x1_ref, h2_ref, cw8_ref, info_ref, pc_ref,
                    m_scr, wro_ref, wdo_ref, wo_ref, wr_ref):
    i = pl.program_id(0)

    @pl.when(i == 0)
    def _():
        wro_ref[...] = wro32_ref[...].astype(BF16)
        wdo_ref[...] = wdo32_ref[...].astype(BF16)
        wo_ref[...] = wo32_ref[...].astype(BF16)
        pad = jnp.zeros((D, ROUTE_W - N_GROUPS - N_EXP), F32)
        wr_ref[...] = jnp.concatenate([rgw_ref[...], rew_ref[...], pad], axis=1).astype(BF16)

    ret_out = jnp.dot(ra_ref[...], wro_ref[...], preferred_element_type=F32)
    diff_out = jnp.dot(da_ref[...], wdo_ref[...], preferred_element_type=F32)
    for c in range(D // LANE):
        sl = slice(c * LANE, (c + 1) * LANE)
        m = (jax.nn.sigmoid(gr_ref[c].astype(F32)) * ret_out[:, sl]
             + jax.nn.sigmoid(gd_ref[c].astype(F32)) * diff_out[:, sl])
        m_scr[:, sl] = m.astype(BF16)
    mix = jnp.dot(m_scr[...], wo_ref[...], preferred_element_type=F32)
    mod = mod_ref[...]

    x1 = jnp.where(i < OP_NPT, xp_ref[...], xs_ref[...]) + mod[:, 2 * D:3 * D] * mix
    x1_ref[...] = x1
    h2 = (_rms(x1) * n2_ref[...] * (1.0 + mod[:, 4 * D:5 * D]) + mod[:, 3 * D:4 * D]).astype(BF16)
    h2_ref[...] = h2
    logits = jnp.dot(h2, wr_ref[...], preferred_element_type=F32) + br_ref[...]
    cw8_ref[...], info_ref[...], pc_ref[...] = _route_rows(logits)


def _outproj(ret_act, diff_act, proj, xp, xs, mod3, n2, wro, wdo, wo, rgw, rew, br):
    assert OP_TM == RT_TM
    npt = OP_NPT
    full = lambda i: (0, 0)
    once = pl.Buffered(1)
    return pl.pallas_call(
        _outproj_kernel,
        out_shape=(jax.ShapeDtypeStruct((N_TOK, D), F32),
                   jax.ShapeDtypeStruct((N_TOK, D), BF16),
                   jax.ShapeDtypeStruct((N_TOK, ROUTE_W), F32),
                   jax.ShapeDtypeStruct((N_TOK, ROUTE_W), F32),
                   jax.ShapeDtypeStruct((RT_NT, 1, ROUTE_W), F32)),
        grid=(N_TOK // OP_TM,),
        in_specs=[pl.BlockSpec((OP_TM, D), lambda i: (i, 0)),
                  pl.BlockSpec((OP_TM, D), lambda i: (i, 0)),
                  pl.BlockSpec((8, OP_TM, LANE), lambda i: (C_GR // 8, i, 0)),
                  pl.BlockSpec((8, OP_TM, LANE), lambda i: (C_GD // 8, i, 0)),
                  pl.BlockSpec((OP_TM, D), lambda i: (jnp.minimum(i, npt - 1), 0)),
                  pl.BlockSpec((OP_TM, D), lambda i: (jnp.maximum(i - npt, 0), 0)),
                  pl.BlockSpec((None, 1, 6 * D), lambda i: (_mod_row(OP_TM)(i), 0, 0)),
                  pl.BlockSpec((1, D), full),
                  pl.BlockSpec((D, D), full, pipeline_mode=once),
                  pl.BlockSpec((D, D), full, pipeline_mode=once),
                  pl.BlockSpec((D, D), full, pipeline_mode=once),
                  pl.BlockSpec((D, N_GROUPS), full, pipeline_mode=once),
                  pl.BlockSpec((D, N_EXP), full, pipeline_mode=once),
                  pl.BlockSpec((1, ROUTE_W), full)],
        out_specs=(pl.BlockSpec((OP_TM, D), lambda i: (i, 0)),
                   pl.BlockSpec((OP_TM, D), lambda i: (i, 0)),
                   pl.BlockSpec((OP_TM, ROUTE_W), lambda i: (i, 0)),
                   pl.BlockSpec((OP_TM, ROUTE_W), lambda i: (i, 0)),
                   pl.BlockSpec((None, 1, ROUTE_W), lambda i: (i, 0, 0))),
        scratch_shapes=[pltpu.VMEM((OP_TM, D), BF16),
                        pltpu.VMEM((D, D), BF16), pltpu.VMEM((D, D), BF16), pltpu.VMEM((D, D), BF16),
                        pltpu.VMEM((D, ROUTE_W), BF16)],
        compiler_params=_cparams(("arbitrary",)),
        name="outproj",
    )(ret_act, diff_act, proj, proj, xp, xs, mod3, n2, wro, wdo, wo, rgw, rew, br)


RT_TM = 512
RT_NT = N_TOK // RT_TM
PIECE = 16
R_LOC = RT_TM + N_GROUPS * PIECE
R_STAGE = 640
EX_TM = 512
R_MAX = 11264
EX_NT = R_MAX // EX_TM


def _route_rows(lg):
    lane = lax.broadcasted_iota(jnp.int32, lg.shape, 1)
    neg = jnp.float32(-jnp.inf)
    big = jnp.int32(ROUTE_W)

    def first_where(cond):
        return jnp.min(jnp.where(cond, lane, big), axis=-1, keepdims=True)

    is_g = lane < N_GROUPS
    gl = jnp.where(is_g, lg, neg)
    gmax = jnp.max(gl, axis=-1, keepdims=True)
    gsum = jnp.sum(jnp.where(is_g, jnp.exp(lg - gmax), 0.0), axis=-1, keepdims=True)
    p_top = 1.0 / gsum
    g_idx = first_where(gl == gmax)
    in_grp = jnp.logical_and(lane >= N_GROUPS, ((lane - N_GROUPS) >> 3) == g_idx)
    el = jnp.where(in_grp, lg, neg)
    emax = jnp.max(el, axis=-1, keepdims=True)
    ee = jnp.where(in_grp, jnp.exp(lg - emax), 0.0)
    ep = ee / jnp.sum(ee, axis=-1, keepdims=True)
    ep = jnp.where(in_grp, ep, -1.0)
    e1 = jnp.max(ep, axis=-1, keepdims=True)
    i1 = first_where(ep == e1)
    ep2 = jnp.where(lane == i1, -1.0, ep)
    e2 = jnp.max(ep2, axis=-1, keepdims=True)
    i2 = first_where(ep2 == e2)
    den = e1 + e2
    cw = (jnp.where(lane == i1, p_top * e1 / den, 0.0)
          + jnp.where(lane == i2, p_top * e2 / den, 0.0))
    cw8 = jnp.zeros_like(cw)
    for g in range(N_GROUPS):
        cw8 = cw8 + jnp.where(g_idx == g, pltpu.roll(cw, LANE - (N_GROUPS + EPG * g), 1), 0.0)

    onehot = (lane == g_idx).astype(F32)
    ii = lax.broadcasted_iota(jnp.int32, (RT_TM, RT_TM), 0)
    jj = lax.broadcasted_iota(jnp.int32, (RT_TM, RT_TM), 1)
    earlier = (jj < ii).astype(BF16)
    prefix = jnp.dot(earlier, onehot.astype(BF16), preferred_element_type=F32)
    cnt = jnp.sum(onehot, axis=0, keepdims=True)
    pc = jnp.floor((cnt + (PIECE - 1.0)) * (1.0 / PIECE)) * PIECE
    lo = pltpu.roll(pc, 1, 1) + pltpu.roll(pc, 2, 1) + pltpu.roll(pc, 3, 1)
    dest = jnp.sum(onehot * (prefix + lo), axis=-1, keepdims=True)
    info = jnp.where(lane == 0, g_idx.astype(F32), jnp.where(lane == 1, dest, 0.0))
    return cw8, info, pc


def _dispatch_plan(pc_arr):
    pc = pc_arr[:, 0, :N_GROUPS].astype(jnp.int32)
    seg_len = jnp.sum(pc, axis=0)
    seg_pad = (seg_len + (EX_TM - 1)) // EX_TM * EX_TM
    seg_end_pad = jnp.cumsum(seg_pad)
    seg_start = seg_end_pad - seg_pad
    chunk_off = seg_start[None, :] + jnp.cumsum(pc, axis=0) - pc
    tile_start = jnp.arange(EX_NT, dtype=jnp.int32) * EX_TM
    tile_group = jnp.minimum(jnp.sum((tile_start[:, None] >= seg_end_pad[None, :]).astype(jnp.int32), axis=1),
                             N_GROUPS - 1)
    tile_valid = (tile_start < (seg_start + seg_len)[tile_group]).astype(jnp.int32)
    return chunk_off.astype(jnp.int32), pc, tile_group.astype(jnp.int32), tile_valid


def _piece_copies(off_ref, pc_ref, tile, make):
    lo = 0
    for g in range(N_GROUPS):
        n = pc_ref[tile, g] // PIECE
        base = off_ref[tile, g]

        def body(j, carry, lo=lo, base=base):
            make(pl.multiple_of(lo + j * PIECE, PIECE), pl.multiple_of(base + j * PIECE, PIECE))
            return carry

        lax.fori_loop(0, n, body, 0)
        lo = lo + pc_ref[tile, g]


def _piece_count(pc_ref, tile):
    n = 0
    for g in range(N_GROUPS):
        n = n + pc_ref[tile, g] // PIECE
    return n


def _dispatch_kernel(off_ref, pc_ref, h_ref, info_ref, cw8_ref, xs_in, cws_in,
                     xs_out, cws_out, x_scr, c_scr, sem):
    del xs_in, cws_in
    i = pl.program_id(0)
    slot = i % 2
    dest = info_ref[...].T[1:2, :]
    row = lax.broadcasted_iota(jnp.int32, (R_LOC, RT_TM), 0).astype(F32)
    sel = row == dest
    sel = sel.astype(BF16)
    x_scr[slot] = jnp.dot(sel, h_ref[...], preferred_element_type=F32).astype(BF16)
    cw = cw8_ref[...]
    hi = cw.astype(BF16).astype(F32)
    mid = (cw - hi).astype(BF16).astype(F32)
    low = (cw - hi - mid).astype(BF16).astype(F32)
    pieces = (hi + pltpu.roll(mid, EPG, 1) + pltpu.roll(low, 2 * EPG, 1)).astype(BF16)
    c_scr[slot] = jnp.dot(sel, pieces, preferred_element_type=F32)

    def x_copy(s, src, dst):
        return pltpu.make_async_copy(x_scr.at[s, pl.ds(src, PIECE)], xs_out.at[pl.ds(dst, PIECE)],
                                     sem.at[0, s])

    def c_copy(s, src, dst):
        return pltpu.make_async_copy(c_scr.at[s, pl.ds(src, PIECE)], cws_out.at[pl.ds(dst, PIECE)],
                                     sem.at[1, s])

    def start(src, dst):
        x_copy(slot, src, dst).start()
        c_copy(slot, src, dst).start()

    _piece_copies(off_ref, pc_ref, i, start)

    def wait_tile(tile, s):
        def wait(j, carry):
            x_copy(s, 0, 0).wait()
            c_copy(s, 0, 0).wait()
            return carry

        lax.fori_loop(0, _piece_count(pc_ref, tile), wait, 0)

    @pl.when(i > 0)
    def _():
        wait_tile(i - 1, 1 - slot)

    @pl.when(i == RT_NT - 1)
    def _():
        wait_tile(i, slot)


def _dispatch(chunk_off, pc, h2, info, cw8, xs0, cws0):
    grid_spec = pltpu.PrefetchScalarGridSpec(
        num_scalar_prefetch=2,
        grid=(RT_NT,),
        in_specs=[pl.BlockSpec((RT_TM, D), lambda i, o, p: (i, 0)),
                  pl.BlockSpec((RT_TM, ROUTE_W), lambda i, o, p: (i, 0)),
                  pl.BlockSpec((RT_TM, ROUTE_W), lambda i, o, p: (i, 0)),
                  pl.BlockSpec(memory_space=pl.ANY),
                  pl.BlockSpec(memory_space=pl.ANY)],
        out_specs=(pl.BlockSpec(memory_space=pl.ANY), pl.BlockSpec(memory_space=pl.ANY)),
        scratch_shapes=[pltpu.VMEM((2, R_LOC, D), BF16),
                        pltpu.VMEM((2, R_LOC, ROUTE_W), F32),
                        pltpu.SemaphoreType.DMA((2, 2))])
    return pl.pallas_call(
        _dispatch_kernel,
        out_shape=(jax.ShapeDtypeStruct((R_MAX, D), BF16),
                   jax.ShapeDtypeStruct((R_MAX, ROUTE_W), F32)),
        grid_spec=grid_spec,
        input_output_aliases={5: 0, 6: 1},
        compiler_params=_cparams(("arbitrary",)),
        name="dispatch",
    )(chunk_off, pc, h2, info, cw8, xs0, cws0)


def _expert_kernel(tg_ref, tv_ref, x_ref, cw_ref, wg_ref, wu_ref, wd_ref, y_ref):
    del tg_ref
    k = pl.program_id(0)

    @pl.when(tv_ref[k] == 0)
    def _():
        y_ref[...] = jnp.zeros_like(y_ref)

    @pl.when(tv_ref[k] != 0)
    def _():
        x = x_ref[...]
        cw = cw_ref[...]
        lane = lax.broadcasted_iota(jnp.int32, cw.shape, 1)
        acc = jnp.zeros(y_ref.shape, F32)
        for j in range(EPG):
            mine = jnp.logical_and((lane & (EPG - 1)) == j, lane < 3 * EPG)
            w = jnp.sum(jnp.where(mine, cw, 0.0), axis=-1, keepdims=True)
            a = (jax.nn.silu(jnp.dot(x, wg_ref[j], preferred_element_type=F32))
                 * jnp.dot(x, wu_ref[j], preferred_element_type=F32))
            acc = acc + jnp.dot((a * w).astype(BF16), wd_ref[j], preferred_element_type=F32)
        y_ref[...] = acc.astype(BF16)


def _experts(tile_group, tile_valid, xs, cws, wg, wu, wd):
    grid_spec = pltpu.PrefetchScalarGridSpec(
        num_scalar_prefetch=2,
        grid=(EX_NT,),
        in_specs=[pl.BlockSpec((EX_TM, D), lambda k, tg, tv: (k, 0)),
                  pl.BlockSpec((EX_TM, ROUTE_W), lambda k, tg, tv: (k, 0)),
                  pl.BlockSpec((EPG, D, FF), lambda k, tg, tv: (tg[k], 0, 0)),
                  pl.BlockSpec((EPG, D, FF), lambda k, tg, tv: (tg[k], 0, 0)),
                  pl.BlockSpec((EPG, FF, D), lambda k, tg, tv: (tg[k], 0, 0))],
        out_specs=pl.BlockSpec((EX_TM, D), lambda k, tg, tv: (k, 0)))
    return pl.pallas_call(
        _expert_kernel,
        out_shape=jax.ShapeDtypeStruct((R_MAX, D), BF16),
        grid_spec=grid_spec,
        compiler_params=_cparams(("arbitrary",)),
        name="experts",
    )(tile_group, tile_valid, xs, cws, wg, wu, wd)


CB_NPT = N_PROMPT // RT_TM


def _combine_kernel(off_ref, pc_ref, info_ref, x1_ref, mod_ref, g_ref, ys_hbm,
                    yp_ref, ysm_ref, stage, sem):
    i = pl.program_id(0)
    slot = i % 2

    def copy(s, dst, src):
        return pltpu.make_async_copy(ys_hbm.at[pl.ds(src, PIECE)], stage.at[s, pl.ds(dst, PIECE)],
                                     sem.at[s])

    def fetch(tile, s):
        _piece_copies(off_ref, pc_ref, tile, lambda loc, glob: copy(s, loc, glob).start())

    @pl.when(i == 0)
    def _():
        stage[...] = jnp.zeros_like(stage)
        fetch(0, 0)

    @pl.when(i + 1 < RT_NT)
    def _():
        fetch(i + 1, 1 - slot)

    def wait(j, carry):
        copy(slot, 0, 0).wait()
        return carry

    lax.fori_loop(0, _piece_count(pc_ref, i), wait, 0)

    dest = info_ref[...][:, 1:2]
    col = lax.broadcasted_iota(jnp.int32, (RT_TM, R_STAGE), 1).astype(F32)
    moe = jnp.dot((col == dest).astype(BF16), stage[slot], preferred_element_type=F32)
    mod = mod_ref[...]
    out = _rms(x1_ref[...] + mod[:, 5 * D:6 * D] * moe) * g_ref[...]

    @pl.when(i < CB_NPT)
    def _():
        yp_ref[...] = out

    @pl.when(i >= CB_NPT)
    def _():
        ysm_ref[...] = out


def _combine(chunk_off, pc, info, x1, mod3, fg, ys):
    npt = CB_NPT
    grid_spec = pltpu.PrefetchScalarGridSpec(
        num_scalar_prefetch=2,
        grid=(RT_NT,),
        in_specs=[pl.BlockSpec((RT_TM, ROUTE_W), lambda i, o, p: (i, 0)),
                  pl.BlockSpec((RT_TM, D), lambda i, o, p: (i, 0)),
                  pl.BlockSpec((None, 1, 6 * D), lambda i, o, p: (_mod_row(RT_TM)(i), 0, 0)),
                  pl.BlockSpec((1, D), lambda i, o, p: (0, 0)),
                  pl.BlockSpec(memory_space=pl.ANY)],
        out_specs=(pl.BlockSpec((RT_TM, D), lambda i, o, p: (jnp.minimum(i, npt - 1), 0)),
                   pl.BlockSpec((RT_TM, D), lambda i, o, p: (jnp.maximum(i - npt, 0), 0))),
        scratch_shapes=[pltpu.VMEM((2, R_STAGE, D), BF16),
                        pltpu.SemaphoreType.DMA((2,))])
    return pl.pallas_call(
        _combine_kernel,
        out_shape=(jax.ShapeDtypeStruct((N_PROMPT, D), F32),
                   jax.ShapeDtypeStruct((N_SAMPLE, D), F32)),
        grid_spec=grid_spec,
        compiler_params=_cparams(("arbitrary",)),
        name="combine",
    )(chunk_off, pc, info, x1, mod3, fg, ys)


def kernel(x_prompt, x_sample, cache_diff_k, cache_diff_v, state_ret_fwd, state_ret_bwd, c, c_ctx,
           w_ada, b_ada, norm1_g, norm2_g, w_in, ret_decay_fwd, ret_decay_bwd, ret_norm_g,
           diff_lambda_q1, diff_lambda_k1, diff_lambda_q2, diff_lambda_k2, diff_subln_g,
           w_ret_o, w_diff_o, w_o, router_group_w, router_group_b, router_expert_w, router_expert_b,
           moe_w_gate, moe_w_up, moe_w_down, final_norm_g):
    l = 0
    lam_init = 0.8 - 0.6 * math.exp(-0.3 * l)
    lam = (jnp.exp(jnp.sum(diff_lambda_q1[l].astype(F32) * diff_lambda_k1[l].astype(F32)))
           - jnp.exp(jnp.sum(diff_lambda_q2[l].astype(F32) * diff_lambda_k2[l].astype(F32))) + lam_init)
    attn_scal = jnp.stack([lam, jnp.float32(1.0 - lam_init)]).astype(F32)
    lg = jnp.stack([jax.nn.log_sigmoid(ret_decay_fwd[l].astype(F32)),
                    jax.nn.log_sigmoid(ret_decay_bwd[l].astype(F32))])

    xp = x_prompt.reshape(N_PROMPT, D)
    xs = x_sample.reshape(N_SAMPLE, D)
    cvec = jnp.concatenate([c_ctx[None, :], c, jnp.zeros((3, D), F32)], axis=0)
    mod3 = _modulation(cvec, w_ada[l], b_ada[l][None, :]).reshape(8, 1, 6 * D)

    proj, kt32, v32 = _inproj(xp, xs, mod3, norm1_g[l][None, :], w_in[l])

    ret_act, s_f, s_b = _retention(lg, proj, state_ret_fwd[:, l], state_ret_bwd[:, l],
                                   ret_norm_g[l][None, :])
    cos_t, sin_t = _rope_tables()
    cache_kt = jnp.transpose(cache_diff_k[:, l], (0, 2, 3, 4, 1)).reshape(4, D, PAST)
    (diff_act, wg_bf, wu_bf, wd_bf, xs0, cws0) = _attention(
        attn_scal, proj, cache_kt, cache_diff_v, cos_t, sin_t, diff_subln_g[l][None, :],
        (moe_w_gate[l], moe_w_up[l], moe_w_down[l]))

    br = jnp.concatenate([router_group_b[l], router_expert_b[l],
                          jnp.zeros((ROUTE_W - N_GROUPS - N_EXP,), F32)])[None, :]
    x1, h2, cw8, info, pc_arr = _outproj(ret_act, diff_act, proj, xp, xs, mod3, norm2_g[l][None, :],
                                         w_ret_o[l], w_diff_o[l], w_o[l],
                                         router_group_w[l], router_expert_w[l], br)
    chunk_off, pc, tile_group, tile_valid = _dispatch_plan(pc_arr)
    xs_sorted, cw_sorted = _dispatch(chunk_off, pc, h2, info, cw8, xs0, cws0)
    y_sorted = _experts(tile_group, tile_valid, xs_sorted, cw_sorted, wg_bf, wu_bf, wd_bf)
    yp, ys = _combine(chunk_off, pc, info, x1, mod3, final_norm_g[None, :], y_sorted)

    return (yp.reshape(16, T_P, D), ys.reshape(4, T_S, D),
            jnp.transpose(kt32.reshape(16, DIFF_H, 2, DIFF_HD, T_P), (0, 4, 1, 2, 3))[:, None],
            v32.reshape(16, 1, T_P, DIFF_H, 2 * DIFF_HD),
            s_f.reshape(16, 1, RET_H, RET_KD, RET_VD), s_b.reshape(16, 1, RET_H, RET_KD, RET_VD))
```

```python
import functools
import math

import jax
import jax.numpy as jnp
import numpy as np
from jax import lax
from jax.experimental import pallas as pl
from jax.experimental.pallas import tpu as pltpu

F32 = jnp.float32
BF16 = jnp.bfloat16

D = 1024
N_PROMPT = 16 * 256
N_SAMPLE = 4 * 1024
N_TOK = N_PROMPT + N_SAMPLE
T_P = 256
T_S = 1024
PAST = 512
GRID_W = 64
RET_H = 4
RET_KD = 128
RET_VD = 256
DIFF_H = 8
DIFF_HD = 64
ROPE_BASE = 10000.0
N_GROUPS = 4
EPG = 8
N_EXP = 32
FF = 256
EPS = 1e-6
IN_W = 8192
LANE = 128
N_CHUNK = IN_W // LANE
C_RQ, C_RK, C_RV, C_RG, C_DQ, C_DK, C_DV, C_GR, C_GD = 0, 4, 8, 16, 24, 32, 40, 48, 56
ROUTE_W = 128
SEQ_BLK = 1024
N_PBLK = N_PROMPT // SEQ_BLK
VMEM_LIMIT = 56 * 1024 * 1024


def _cparams(sem):
    return pltpu.CompilerParams(dimension_semantics=sem, vmem_limit_bytes=VMEM_LIMIT)


def _mod_row(tile_rows):
    def f(i):
        start = i * tile_rows
        return jnp.where(start < N_PROMPT, 0, 1 + (start - N_PROMPT) // T_S)
    return f


def _rms(x):
    return x * lax.rsqrt(jnp.mean(x * x, axis=-1, keepdims=True) + EPS)


def _mod_kernel(c_ref, w_ref, b_ref, o_ref):
    s = jax.nn.silu(c_ref[...])
    o_ref[...] = jnp.dot(s.astype(BF16), w_ref[...].astype(BF16),
                         preferred_element_type=F32) + b_ref[...]


def _modulation(cvec, w_ada, b_ada):
    tn = 1536
    return pl.pallas_call(
        _mod_kernel,
        out_shape=jax.ShapeDtypeStruct((8, 6 * D), F32),
        grid=(6 * D // tn,),
        in_specs=[pl.BlockSpec((8, D), lambda j: (0, 0)),
                  pl.BlockSpec((D, tn), lambda j: (0, j)),
                  pl.BlockSpec((1, tn), lambda j: (0, j))],
        out_specs=pl.BlockSpec((8, tn), lambda j: (0, j)),
        compiler_params=_cparams(("arbitrary",)),
        name="mod",
    )(cvec, w_ada, b_ada)


IP_TM = 512
IP_TN = 2048
IP_NPT = N_PROMPT // IP_TM


IP_KV_TILE = C_DK * LANE // IP_TN
IP_SPT = IP_TM // T_P


def _inproj_kernel(xp_ref, xs_ref, mod_ref, n1_ref, w_ref, proj_ref, kt_ref, v32_ref, w_scr):
    j = pl.program_id(0)
    i = pl.program_id(1)

    @pl.when(i == 0)
    def _():
        w_scr[...] = w_ref[...].astype(BF16)

    x = jnp.where(i < IP_NPT, xp_ref[...], xs_ref[...])
    mod = mod_ref[...]
    h = (_rms(x) * n1_ref[...] * (1.0 + mod[:, D:2 * D]) + mod[:, 0:D]).astype(BF16)
    acc = jnp.dot(h, w_scr[...], preferred_element_type=F32)
    for c in range(IP_TN // LANE):
        proj_ref[c] = acc[:, c * LANE:(c + 1) * LANE].astype(BF16)

    @pl.when(jnp.logical_and(j == IP_KV_TILE, i < IP_NPT))
    def _():
        for s in range(IP_SPT):
            kt_ref[s] = acc[s * T_P:(s + 1) * T_P, :D].T
        v32_ref[...] = acc[:, D:]


def _inproj(xp, xs, mod3, n1, w_in):
    npt = IP_NPT
    cpt = IP_TN // LANE

    def kv_tile(j, i):
        return jnp.where(j < IP_KV_TILE, 0, jnp.where(j == IP_KV_TILE, jnp.minimum(i, npt - 1), npt - 1))

    return pl.pallas_call(
        _inproj_kernel,
        out_shape=(jax.ShapeDtypeStruct((N_CHUNK, N_TOK, LANE), BF16),
                   jax.ShapeDtypeStruct((N_PROMPT // T_P, D, T_P), F32),
                   jax.ShapeDtypeStruct((N_PROMPT, D), F32)),
        grid=(IN_W // IP_TN, N_TOK // IP_TM),
        in_specs=[pl.BlockSpec((IP_TM, D), lambda j, i: (jnp.minimum(i, npt - 1), 0)),
                  pl.BlockSpec((IP_TM, D), lambda j, i: (jnp.maximum(i - npt, 0), 0)),
                  pl.BlockSpec((None, 1, 6 * D), lambda j, i: (_mod_row(IP_TM)(i), 0, 0)),
                  pl.BlockSpec((1, D), lambda j, i: (0, 0)),
                  pl.BlockSpec((D, IP_TN), lambda j, i: (0, j))],
        out_specs=(pl.BlockSpec((cpt, IP_TM, LANE), lambda j, i: (j, i, 0)),
                   pl.BlockSpec((IP_SPT, D, T_P), lambda j, i: (kv_tile(j, i), 0, 0)),
                   pl.BlockSpec((IP_TM, D), lambda j, i: (kv_tile(j, i), 0))),
        scratch_shapes=[pltpu.VMEM((D, IP_TN), BF16)],
        compiler_params=_cparams(("arbitrary", "arbitrary")),
        name="inproj",
    )(xp, xs, mod3, n1, w_in)


def _decay_mask(t, lgf, lgb):
    ii = lax.broadcasted_iota(jnp.int32, (t, t), 0)
    jj = lax.broadcasted_iota(jnp.int32, (t, t), 1)
    rel = (ii - jj).astype(F32)
    e = jnp.exp(jnp.where(rel >= 0.0, lgf, -lgb) * rel)
    return jnp.where(rel == 0.0, 2.0, e) * (RET_KD ** -0.5)


def _ret_kernel(lg_ref, q_ref, k_ref, v_ref, rg_ref, s0f_ref, s0b_ref, g_ref,
                o_ref, sf_ref, sb_ref, dm_scr):
    h = pl.program_id(0)
    i = pl.program_id(1)
    lgf = lg_ref[0, h]
    lgb = lg_ref[1, h]
    gain = g_ref[...]
    nt = (((1,), (1,)), ((), ()))
    tn = (((0,), (0,)), ((), ()))

    def finish(o, rg):
        d = o - jnp.mean(o, axis=-1, keepdims=True)
        y = d * lax.rsqrt(jnp.mean(d * d, axis=-1, keepdims=True) + EPS) * gain
        return (jax.nn.silu(rg.astype(F32)) * y).astype(BF16)

    def vcat(r):
        return jnp.concatenate([v_ref[0, r, :], v_ref[1, r, :]], axis=1)

    def gcat(r):
        return jnp.concatenate([rg_ref[0, r, :], rg_ref[1, r, :]], axis=1)

    @pl.when(i == 0)
    def _():
        dm_scr[...] = _decay_mask(T_P, lgf, lgb)

    n_chunk = SEQ_BLK // T_P
    chunks = [slice(c * T_P, (c + 1) * T_P) for c in range(n_chunk)]
    t = lax.broadcasted_iota(jnp.int32, (T_P, 1), 0).astype(F32)
    kdf = jnp.exp(lgf * (T_P - 1.0 - t)) * (RET_KD ** -0.5)
    kdb = jnp.exp(lgb * t) * (RET_KD ** -0.5)

    def intra(r):
        sc = lax.dot_general(q_ref[r, :], k_ref[r, :], nt, preferred_element_type=F32)
        return jnp.dot((sc * dm_scr[...]).astype(BF16), vcat(r), preferred_element_type=F32)

    def key_state(r, kd):
        kw = (k_ref[r, :].astype(F32) * kd).astype(BF16)
        return lax.dot_general(kw, vcat(r), tn, preferred_element_type=F32)

    @pl.when(i < N_PBLK)
    def _():
        for s, r in enumerate(chunks):
            o_ref[r, :] = finish(intra(r), gcat(r))
            sf_ref[s] = key_state(r, kdf)
            sb_ref[s] = key_state(r, kdb)

    @pl.when(i >= N_PBLK)
    def _():
        qdf = jnp.exp(lgf * (t + 1.0))
        qdb = jnp.exp(lgb * (T_P - t))
        span = jnp.full((1, 1), float(T_P), F32)
        cf = jnp.exp(lgf * span)
        cb = jnp.exp(lgb * span)
        before_b = [None] * n_chunk
        state = s0b_ref[...]
        for c in reversed(range(n_chunk)):
            before_b[c] = state
            if c > 0:
                state = cb * state + key_state(chunks[c], kdb)
        state = s0f_ref[...]
        for c, r in enumerate(chunks):
            qf = q_ref[r, :].astype(F32)
            o = (intra(r)
                 + jnp.dot((qf * qdf).astype(BF16), state.astype(BF16), preferred_element_type=F32)
                 + jnp.dot((qf * qdb).astype(BF16), before_b[c].astype(BF16), preferred_element_type=F32))
            o_ref[r, :] = finish(o, gcat(r))
            if c + 1 < n_chunk:
                state = cf * state + key_state(r, kdf)


def _retention(lg, proj, s0f, s0b, gnorm):
    nb = N_TOK // SEQ_BLK
    spb = SEQ_BLK // T_P
    smp = lambda i: jnp.maximum(i - N_PBLK, 0)
    pmt = lambda i: jnp.minimum(i, N_PBLK - 1)
    return pl.pallas_call(
        _ret_kernel,
        out_shape=(jax.ShapeDtypeStruct((N_TOK, RET_H * RET_VD), BF16),
                   jax.ShapeDtypeStruct((16, RET_H, RET_KD, RET_VD), F32),
                   jax.ShapeDtypeStruct((16, RET_H, RET_KD, RET_VD), F32)),
        grid=(RET_H, nb),
        in_specs=[pl.BlockSpec(memory_space=pltpu.SMEM),
                  pl.BlockSpec((None, SEQ_BLK, LANE), lambda h, i: (C_RQ + h, i, 0)),
                  pl.BlockSpec((None, SEQ_BLK, LANE), lambda h, i: (C_RK + h, i, 0)),
                  pl.BlockSpec((2, SEQ_BLK, LANE), lambda h, i: (C_RV // 2 + h, i, 0)),
                  pl.BlockSpec((2, SEQ_BLK, LANE), lambda h, i: (C_RG // 2 + h, i, 0)),
                  pl.BlockSpec((None, None, RET_KD, RET_VD), lambda h, i: (smp(i), h, 0, 0)),
                  pl.BlockSpec((None, None, RET_KD, RET_VD), lambda h, i: (smp(i), h, 0, 0)),
                  pl.BlockSpec((1, RET_VD), lambda h, i: (0, h))],
        out_specs=(pl.BlockSpec((SEQ_BLK, RET_VD), lambda h, i: (i, h)),
                   pl.BlockSpec((spb, None, RET_KD, RET_VD), lambda h, i: (pmt(i), h, 0, 0)),
                   pl.BlockSpec((spb, None, RET_KD, RET_VD), lambda h, i: (pmt(i), h, 0, 0))),
        scratch_shapes=[pltpu.VMEM((T_P, T_P), F32)],
        compiler_params=_cparams(("arbitrary", "arbitrary")),
        name="ret",
    )(lg, proj, proj, proj, proj, s0f, s0b, gnorm)


ATT_TQ = 1024
ATT_HPS = 2
ATT_QSCALE = (DIFF_HD ** -0.5) * math.log2(math.e)


N_SIDE = 3


def _attn_kernel(sc_ref, q_ref, k_ref, v_ref, ckt_ref, cv_ref, cos_ref, sin_ref, g_ref, *rest):
    side_in = rest[:N_SIDE]
    o_ref = rest[N_SIDE]
    side_out = rest[N_SIDE + 1:2 * N_SIDE + 1]
    xs0_ref, cws0_ref, q_scr, k_scr, v_scr = rest[2 * N_SIDE + 1:]
    i = pl.program_id(0)

    def side_jobs():
        for src, dst in zip(side_in, side_out):
            dst[...] = src[...].astype(BF16)
        xs0_ref[...] = jnp.zeros_like(xs0_ref)
        cws0_ref[...] = jnp.zeros_like(cws0_ref)

    lam = sc_ref[0]
    out_scale = sc_ref[1]
    gain = g_ref[...] * out_scale
    lane = lax.broadcasted_iota(jnp.int32, (1, LANE), 1)
    first = lane < DIFF_HD
    nt = (((1,), (1,)), ((), ()))

    def halves(q):
        zero = jnp.zeros_like(q)
        return jnp.where(first, q, zero), jnp.where(first, zero, q)

    def weights(s):
        return jnp.exp2(s - jnp.max(s, axis=-1, keepdims=True)).astype(BF16)

    def finish(of0, of1):
        o = of0[:, :LANE] / of0[:, LANE:] - lam * (of1[:, :LANE] / of1[:, LANE:])
        return (_rms(o) * gain).astype(BF16)

    @pl.when(i < N_PBLK)
    def _():
        side_jobs()
        n_seq = SEQ_BLK // T_P
        rows = [slice(s * T_P, (s + 1) * T_P) for s in range(n_seq)]
        ones = jnp.ones((T_P, LANE), BF16)
        for hh in range(ATT_HPS):
            cols = slice(hh * LANE, (hh + 1) * LANE)
            q0, q1 = halves((q_ref[hh].astype(F32) * ATT_QSCALE).astype(BF16))
            s0 = jnp.concatenate([lax.dot_general(q0[r], k_ref[hh, r, :], nt, preferred_element_type=F32)
                                  for r in rows], axis=0)
            s1 = jnp.concatenate([lax.dot_general(q1[r], k_ref[hh, r, :], nt, preferred_element_type=F32)
                                  for r in rows], axis=0)
            e0 = weights(s0)
            e1 = weights(s1)
            for r in rows:
                v1 = jnp.concatenate([v_ref[hh, r, :], ones], axis=1)
                o_ref[r, cols] = finish(jnp.dot(e0[r], v1, preferred_element_type=F32),
                                        jnp.dot(e1[r], v1, preferred_element_type=F32))

    @pl.when(i >= N_PBLK)
    def _():
        side_jobs()
        cos = cos_ref[...]
        sin = sin_ref[...]
        low = (lax.broadcasted_iota(jnp.int32, (T_S, LANE), 1) & 16) == 0

        def rope(x):
            xs = jnp.where(low, pltpu.roll(x, LANE - 16, 1), pltpu.roll(x, 16, 1))
            return x * cos + xs * sin

        for hh in range(ATT_HPS):
            cols = slice(hh * LANE, (hh + 1) * LANE)
            head = pl.program_id(1) * ATT_HPS + hh
            q_scr[hh] = (rope(q_ref[hh].astype(F32)) * ATT_QSCALE).astype(BF16)
            k_scr[hh] = rope(k_ref[hh].astype(F32)).astype(BF16)
            ckt = ckt_ref[cols, :].astype(BF16)
            v_scr[hh, 0:T_S, 0:LANE] = v_ref[hh]
            v_scr[hh, T_S:T_S + PAST, 0:LANE] = cv_ref[:, head, :].astype(BF16)
            v_scr[hh, :, LANE:2 * LANE] = jnp.ones((T_S + PAST, LANE), BF16)

            def scores(qh, hh=hh, ckt=ckt):
                return jnp.concatenate([lax.dot_general(qh, k_scr[hh], nt, preferred_element_type=F32),
                                        jnp.dot(qh, ckt, preferred_element_type=F32)], axis=1)

            for b in range(T_S // ATT_TQ):
                r = slice(b * ATT_TQ, (b + 1) * ATT_TQ)
                q0, q1 = halves(q_scr[hh, r, :])
                e0 = weights(scores(q0))
                e1 = weights(scores(q1))
                o_ref[r, cols] = finish(jnp.dot(e0, v_scr[hh], preferred_element_type=F32),
                                        jnp.dot(e1, v_scr[hh], preferred_element_type=F32))


def _attention(scal, proj, cache_k, cache_v, cos_t, sin_t, subln_g, side_weights):
    nb = N_TOK // SEQ_BLK
    smp = lambda i: jnp.maximum(i - N_PBLK, 0)

    n_hp = DIFF_H // ATT_HPS
    n_steps = nb * n_hp
    step = lambda i, h: i * n_hp + h
    per_exp = n_steps // N_EXP
    exp_slice = lambda i, h: (step(i, h) // per_exp, step(i, h) % per_exp, 0)
    row_slice = lambda i, h: (step(i, h), 0)
    up_spec = pl.BlockSpec((None, D // per_exp, FF), exp_slice)
    down_spec = pl.BlockSpec((None, FF // per_exp, D), exp_slice)
    side_specs = [up_spec, up_spec, down_spec]
    side_shapes = [jax.ShapeDtypeStruct(a.shape, BF16) for a in side_weights]
    zrows = R_MAX // n_steps

    return pl.pallas_call(
        _attn_kernel,
        out_shape=(jax.ShapeDtypeStruct((N_TOK, DIFF_H * 2 * DIFF_HD), BF16),
                   *side_shapes,
                   jax.ShapeDtypeStruct((R_MAX, D), BF16),
                   jax.ShapeDtypeStruct((R_MAX, ROUTE_W), F32)),
        grid=(nb, n_hp),
        in_specs=[pl.BlockSpec(memory_space=pltpu.SMEM),
                  pl.BlockSpec((ATT_HPS, SEQ_BLK, LANE), lambda i, h: (C_DQ // ATT_HPS + h, i, 0)),
                  pl.BlockSpec((ATT_HPS, SEQ_BLK, LANE), lambda i, h: (C_DK // ATT_HPS + h, i, 0)),
                  pl.BlockSpec((ATT_HPS, SEQ_BLK, LANE), lambda i, h: (C_DV // ATT_HPS + h, i, 0)),
                  pl.BlockSpec((None, ATT_HPS * LANE, PAST), lambda i, h: (smp(i), h, 0)),
                  pl.BlockSpec((None, None, PAST, DIFF_H, LANE), lambda i, h: (smp(i), 0, 0, 0, 0)),
                  pl.BlockSpec((T_S, LANE), lambda i, h: (0, 0)),
                  pl.BlockSpec((T_S, LANE), lambda i, h: (0, 0)),
                  pl.BlockSpec((1, LANE), lambda i, h: (0, 0)),
                  *side_specs],
        out_specs=(pl.BlockSpec((SEQ_BLK, ATT_HPS * LANE), lambda i, h: (i, h)),
                   *side_specs,
                   pl.BlockSpec((zrows, D), row_slice),
                   pl.BlockSpec((zrows, ROUTE_W), row_slice)),
        scratch_shapes=[pltpu.VMEM((ATT_HPS, T_S, LANE), BF16),
                        pltpu.VMEM((ATT_HPS, T_S, LANE), BF16),
                        pltpu.VMEM((ATT_HPS, T_S + PAST, 2 * LANE), BF16)],
        compiler_params=_cparams(("arbitrary", "arbitrary")),
        name="attn",
    )(scal, proj, proj, proj, cache_k, cache_v, cos_t, sin_t, subln_g, *side_weights)


def _rope_tables():
    n_rows = T_S // GRID_W
    row = np.repeat(np.arange(n_rows), GRID_W).astype(np.float64)
    col = np.tile(np.arange(GRID_W), n_rows).astype(np.float64)
    n_freq = DIFF_HD // 4
    inv = ROPE_BASE ** (-np.arange(n_freq, dtype=np.float64) / n_freq)

    def axis_tables(pos):
        ang = pos[:, None] * inv[None, :]
        c = np.cos(ang)
        s = np.sin(ang)
        return np.concatenate([c, c], axis=-1), np.concatenate([-s, s], axis=-1)

    cr, sr = axis_tables(row)
    cc, sc = axis_tables(col)
    cos_h = np.concatenate([cr, cc], axis=-1)
    sin_h = np.concatenate([sr, sc], axis=-1)
    return (jnp.asarray(np.concatenate([cos_h, cos_h], axis=-1), F32),
            jnp.asarray(np.concatenate([sin_h, sin_h], axis=-1), F32))


OP_TM = 512
OP_NPT = N_PROMPT // OP_TM


def _outproj_kernel(ra_ref, da_ref, gr_ref, gd_ref, xp_ref, xs_ref, mod_ref, n2_ref,
                    wro32_ref, wdo32_ref, wo32_ref, rgw_ref, rew_ref, br_ref,
                    x1_ref, h2_ref, cw8_ref, info_ref, pc_ref,
                    m_scr, wro_ref, wdo_ref, wo_ref, wr_ref):
    i = pl.program_id(0)

    @pl.when(i == 0)
    def _():
        wro_ref[...] = wro32_ref[...].astype(BF16)
        wdo_ref[...] = wdo32_ref[...].astype(BF16)
        wo_ref[...] = wo32_ref[...].astype(BF16)
        pad = jnp.zeros((D, ROUTE_W - N_GROUPS - N_EXP), F32)
        wr_ref[...] = jnp.concatenate([rgw_ref[...], rew_ref[...], pad], axis=1).astype(BF16)

    ret_out = jnp.dot(ra_ref[...], wro_ref[...], preferred_element_type=F32)
    diff_out = jnp.dot(da_ref[...], wdo_ref[...], preferred_element_type=F32)
    for c in range(D // LANE):
        sl = slice(c * LANE, (c + 1) * LANE)
        m = (jax.nn.sigmoid(gr_ref[c].astype(F32)) * ret_out[:, sl]
             + jax.nn.sigmoid(gd_ref[c].astype(F32)) * diff_out[:, sl])
        m_scr[:, sl] = m.astype(BF16)
    mix = jnp.dot(m_scr[...], wo_ref[...], preferred_element_type=F32)
    mod = mod_ref[...]

    x1 = jnp.where(i < OP_NPT, xp_ref[...], xs_ref[...]) + mod[:, 2 * D:3 * D] * mix
    x1_ref[...] = x1
    h2 = (_rms(x1) * n2_ref[...] * (1.0 + mod[:, 4 * D:5 * D]) + mod[:, 3 * D:4 * D]).astype(BF16)
    h2_ref[...] = h2
    logits = jnp.dot(h2, wr_ref[...], preferred_element_type=F32) + br_ref[...]
    cw8_ref[...], info_ref[...], pc_ref[...] = _route_rows(logits)


def _outproj(ret_act, diff_act, proj, xp, xs, mod3, n2, wro, wdo, wo, rgw, rew, br):
    assert OP_TM == RT_TM
    npt = OP_NPT
    full = lambda i: (0, 0)
    once = pl.Buffered(1)
    return pl.pallas_call(
        _outproj_kernel,
        out_shape=(jax.ShapeDtypeStruct((N_TOK, D), F32),
                   jax.ShapeDtypeStruct((N_TOK, D), BF16),
                   jax.ShapeDtypeStruct((N_TOK, ROUTE_W), F32),
                   jax.ShapeDtypeStruct((N_TOK, ROUTE_W), F32),
                   jax.ShapeDtypeStruct((RT_NT, 1, ROUTE_W), F32)),
        grid=(N_TOK // OP_TM,),
        in_specs=[pl.BlockSpec((OP_TM, D), lambda i: (i, 0)),
                  pl.BlockSpec((OP_TM, D), lambda i: (i, 0)),
                  pl.BlockSpec((8, OP_TM, LANE), lambda i: (C_GR // 8, i, 0)),
                  pl.BlockSpec((8, OP_TM, LANE), lambda i: (C_GD // 8, i, 0)),
                  pl.BlockSpec((OP_TM, D), lambda i: (jnp.minimum(i, npt - 1), 0)),
                  pl.BlockSpec((OP_TM, D), lambda i: (jnp.maximum(i - npt, 0), 0)),
                  pl.BlockSpec((None, 1, 6 * D), lambda i: (_mod_row(OP_TM)(i), 0, 0)),
                  pl.BlockSpec((1, D), full),
                  pl.BlockSpec((D, D), full, pipeline_mode=once),
                  pl.BlockSpec((D, D), full, pipeline_mode=once),
                  pl.BlockSpec((D, D), full, pipeline_mode=once),
                  pl.BlockSpec((D, N_GROUPS), full, pipeline_mode=once),
                  pl.BlockSpec((D, N_EXP), full, pipeline_mode=once),
                  pl.BlockSpec((1, ROUTE_W), full)],
        out_specs=(pl.BlockSpec((OP_TM, D), lambda i: (i, 0)),
                   pl.BlockSpec((OP_TM, D), lambda i: (i, 0)),
                   pl.BlockSpec((OP_TM, ROUTE_W), lambda i: (i, 0)),
                   pl.BlockSpec((OP_TM, ROUTE_W), lambda i: (i, 0)),
                   pl.BlockSpec((None, 1, ROUTE_W), lambda i: (i, 0, 0))),
        scratch_shapes=[pltpu.VMEM((OP_TM, D), BF16),
                        pltpu.VMEM((D, D), BF16), pltpu.VMEM((D, D), BF16), pltpu.VMEM((D, D), BF16),
                        pltpu.VMEM((D, ROUTE_W), BF16)],
        compiler_params=_cparams(("arbitrary",)),
        name="outproj",
    )(ret_act, diff_act, proj, proj, xp, xs, mod3, n2, wro, wdo, wo, rgw, rew, br)


RT_TM = 512
RT_NT = N_TOK // RT_TM
PIECE = 16
R_LOC = RT_TM + N_GROUPS * PIECE
R_STAGE = 640
EX_TM = 512
R_MAX = 11264
EX_NT = R_MAX // EX_TM


def _route_rows(lg):
    lane = lax.broadcasted_iota(jnp.int32, lg.shape, 1)
    neg = jnp.float32(-jnp.inf)
    big = jnp.int32(ROUTE_W)

    def first_where(cond):
        return jnp.min(jnp.where(cond, lane, big), axis=-1, keepdims=True)

    is_g = lane < N_GROUPS
    gl = jnp.where(is_g, lg, neg)
    gmax = jnp.max(gl, axis=-1, keepdims=True)
    gsum = jnp.sum(jnp.where(is_g, jnp.exp(lg - gmax), 0.0), axis=-1, keepdims=True)
    p_top = 1.0 / gsum
    g_idx = first_where(gl == gmax)
    in_grp = jnp.logical_and(lane >= N_GROUPS, ((lane - N_GROUPS) >> 3) == g_idx)
    el = jnp.where(in_grp, lg, neg)
    emax = jnp.max(el, axis=-1, keepdims=True)
    ee = jnp.where(in_grp, jnp.exp(lg - emax), 0.0)
    ep = ee / jnp.sum(ee, axis=-1, keepdims=True)
    ep = jnp.where(in_grp, ep, -1.0)
    e1 = jnp.max(ep, axis=-1, keepdims=True)
    i1 = first_where(ep == e1)
    ep2 = jnp.where(lane == i1, -1.0, ep)
    e2 = jnp.max(ep2, axis=-1, keepdims=True)
    i2 = first_where(ep2 == e2)
    den = e1 + e2
    cw = (jnp.where(lane == i1, p_top * e1 / den, 0.0)
          + jnp.where(lane == i2, p_top * e2 / den, 0.0))
    cw8 = jnp.zeros_like(cw)
    for g in range(N_GROUPS):
        cw8 = cw8 + jnp.where(g_idx == g, pltpu.roll(cw, LANE - (N_GROUPS + EPG * g), 1), 0.0)

    onehot = (lane == g_idx).astype(F32)
    ii = lax.broadcasted_iota(jnp.int32, (RT_TM, RT_TM), 0)
    jj = lax.broadcasted_iota(jnp.int32, (RT_TM, RT_TM), 1)
    earlier = (jj < ii).astype(BF16)
    prefix = jnp.dot(earlier, onehot.astype(BF16), preferred_element_type=F32)
    cnt = jnp.sum(onehot, axis=0, keepdims=True)
    pc = jnp.floor((cnt + (PIECE - 1.0)) * (1.0 / PIECE)) * PIECE
    lo = pltpu.roll(pc, 1, 1) + pltpu.roll(pc, 2, 1) + pltpu.roll(pc, 3, 1)
    dest = jnp.sum(onehot * (prefix + lo), axis=-1, keepdims=True)
    info = jnp.where(lane == 0, g_idx.astype(F32), jnp.where(lane == 1, dest, 0.0))
    return cw8, info, pc


def _dispatch_plan(pc_arr):
    pc = pc_arr[:, 0, :N_GROUPS].astype(jnp.int32)
    seg_len = jnp.sum(pc, axis=0)
    seg_pad = (seg_len + (EX_TM - 1)) // EX_TM * EX_TM
    seg_end_pad = jnp.cumsum(seg_pad)
    seg_start = seg_end_pad - seg_pad
    chunk_off = seg_start[None, :] + jnp.cumsum(pc, axis=0) - pc
    tile_start = jnp.arange(EX_NT, dtype=jnp.int32) * EX_TM
    tile_group = jnp.minimum(jnp.sum((tile_start[:, None] >= seg_end_pad[None, :]).astype(jnp.int32), axis=1),
                             N_GROUPS - 1)
    tile_valid = (tile_start < (seg_start + seg_len)[tile_group]).astype(jnp.int32)
    return chunk_off.astype(jnp.int32), pc, tile_group.astype(jnp.int32), tile_valid


def _piece_copies(off_ref, pc_ref, tile, make):
    lo = 0
    for g in range(N_GROUPS):
        n = pc_ref[tile, g] // PIECE
        base = off_ref[tile, g]

        def body(j, carry, lo=lo, base=base):
            make(pl.multiple_of(lo + j * PIECE, PIECE), pl.multiple_of(base + j * PIECE, PIECE))
            return carry

        lax.fori_loop(0, n, body, 0)
        lo = lo + pc_ref[tile, g]


def _piece_count(pc_ref, tile):
    n = 0
    for g in range(N_GROUPS):
        n = n + pc_ref[tile, g] // PIECE
    return n


def _dispatch_kernel(off_ref, pc_ref, h_ref, info_ref, cw8_ref, xs_in, cws_in,
                     xs_out, cws_out, x_scr, c_scr, sem):
    del xs_in, cws_in
    i = pl.program_id(0)
    slot = i % 2
    dest = info_ref[...].T[1:2, :]
    row = lax.broadcasted_iota(jnp.int32, (R_LOC, RT_TM), 0).astype(F32)
    sel = row == dest
    sel = sel.astype(BF16)
    x_scr[slot] = jnp.dot(sel, h_ref[...], preferred_element_type=F32).astype(BF16)
    cw = cw8_ref[...]
    hi = cw.astype(BF16).astype(F32)
    mid = (cw - hi).astype(BF16).astype(F32)
    low = (cw - hi - mid).astype(BF16).astype(F32)
    pieces = (hi + pltpu.roll(mid, EPG, 1) + pltpu.roll(low, 2 * EPG, 1)).astype(BF16)
    c_scr[slot] = jnp.dot(sel, pieces, preferred_element_type=F32)

    def x_copy(s, src, dst):
        return pltpu.make_async_copy(x_scr.at[s, pl.ds(src, PIECE)], xs_out.at[pl.ds(dst, PIECE)],
                                     sem.at[0, s])

    def c_copy(s, src, dst):
        return pltpu.make_async_copy(c_scr.at[s, pl.ds(src, PIECE)], cws_out.at[pl.ds(dst, PIECE)],
                                     sem.at[1, s])

    def start(src, dst):
        x_copy(slot, src, dst).start()
        c_copy(slot, src, dst).start()

    _piece_copies(off_ref, pc_ref, i, start)

    def wait_tile(tile, s):
        def wait(j, carry):
            x_copy(s, 0, 0).wait()
            c_copy(s, 0, 0).wait()
            return carry

        lax.fori_loop(0, _piece_count(pc_ref, tile), wait, 0)

    @pl.when(i > 0)
    def _():
        wait_tile(i - 1, 1 - slot)

    @pl.when(i == RT_NT - 1)
    def _():
        wait_tile(i, slot)


def _dispatch(chunk_off, pc, h2, info, cw8, xs0, cws0):
    grid_spec = pltpu.PrefetchScalarGridSpec(
        num_scalar_prefetch=2,
        grid=(RT_NT,),
        in_specs=[pl.BlockSpec((RT_TM, D), lambda i, o, p: (i, 0)),
                  pl.BlockSpec((RT_TM, ROUTE_W), lambda i, o, p: (i, 0)),
                  pl.BlockSpec((RT_TM, ROUTE_W), lambda i, o, p: (i, 0)),
                  pl.BlockSpec(memory_space=pl.ANY),
                  pl.BlockSpec(memory_space=pl.ANY)],
        out_specs=(pl.BlockSpec(memory_space=pl.ANY), pl.BlockSpec(memory_space=pl.ANY)),
        scratch_shapes=[pltpu.VMEM((2, R_LOC, D), BF16),
                        pltpu.VMEM((2, R_LOC, ROUTE_W), F32),
                        pltpu.SemaphoreType.DMA((2, 2))])
    return pl.pallas_call(
        _dispatch_kernel,
        out_shape=(jax.ShapeDtypeStruct((R_MAX, D), BF16),
                   jax.ShapeDtypeStruct((R_MAX, ROUTE_W), F32)),
        grid_spec=grid_spec,
        input_output_aliases={5: 0, 6: 1},
        compiler_params=_cparams(("arbitrary",)),
        name="dispatch",
    )(chunk_off, pc, h2, info, cw8, xs0, cws0)


def _expert_kernel(tg_ref, tv_ref, x_ref, cw_ref, wg_ref, wu_ref, wd_ref, y_ref):
    del tg_ref
    k = pl.program_id(0)

    @pl.when(tv_ref[k] == 0)
    def _():
        y_ref[...] = jnp.zeros_like(y_ref)

    @pl.when(tv_ref[k] != 0)
    def _():
        x = x_ref[...]
        cw = cw_ref[...]
        lane = lax.broadcasted_iota(jnp.int32, cw.shape, 1)
        acc = jnp.zeros(y_ref.shape, F32)
        for j in range(EPG):
            mine = jnp.logical_and((lane & (EPG - 1)) == j, lane < 3 * EPG)
            w = jnp.sum(jnp.where(mine, cw, 0.0), axis=-1, keepdims=True)
            a = (jax.nn.silu(jnp.dot(x, wg_ref[j], preferred_element_type=F32))
                 * jnp.dot(x, wu_ref[j], preferred_element_type=F32))
            acc = acc + jnp.dot((a * w).astype(BF16), wd_ref[j], preferred_element_type=F32)
        y_ref[...] = acc.astype(BF16)


def _experts(tile_group, tile_valid, xs, cws, wg, wu, wd):
    grid_spec = pltpu.PrefetchScalarGridSpec(
        num_scalar_prefetch=2,
        grid=(EX_NT,),
        in_specs=[pl.BlockSpec((EX_TM, D), lambda k, tg, tv: (k, 0)),
                  pl.BlockSpec((EX_TM, ROUTE_W), lambda k, tg, tv: (k, 0)),
                  pl.BlockSpec((EPG, D, FF), lambda k, tg, tv: (tg[k], 0, 0)),
                  pl.BlockSpec((EPG, D, FF), lambda k, tg, tv: (tg[k], 0, 0)),
                  pl.BlockSpec((EPG, FF, D), lambda k, tg, tv: (tg[k], 0, 0))],
        out_specs=pl.BlockSpec((EX_TM, D), lambda k, tg, tv: (k, 0)))
    return pl.pallas_call(
        _expert_kernel,
        out_shape=jax.ShapeDtypeStruct((R_MAX, D), BF16),
        grid_spec=grid_spec,
        compiler_params=_cparams(("arbitrary",)),
        name="experts",
    )(tile_group, tile_valid, xs, cws, wg, wu, wd)


CB_NPT = N_PROMPT // RT_TM


def _combine_kernel(off_ref, pc_ref, info_ref, x1_ref, mod_ref, g_ref, ys_hbm,
                    yp_ref, ysm_ref, stage, sem):
    i = pl.program_id(0)
    slot = i % 2

    def copy(s, dst, src):
        return pltpu.make_async_copy(ys_hbm.at[pl.ds(src, PIECE)], stage.at[s, pl.ds(dst, PIECE)],
                                     sem.at[s])

    def fetch(tile, s):
        _piece_copies(off_ref, pc_ref, tile, lambda loc, glob: copy(s, loc, glob).start())

    @pl.when(i == 0)
    def _():
        stage[...] = jnp.zeros_like(stage)
        fetch(0, 0)

    @pl.when(i + 1 < RT_NT)
    def _():
        fetch(i + 1, 1 - slot)

    def wait(j, carry):
        copy(slot, 0, 0).wait()
        return carry

    lax.fori_loop(0, _piece_count(pc_ref, i), wait, 0)

    dest = info_ref[...][:, 1:2]
    col = lax.broadcasted_iota(jnp.int32, (RT_TM, R_STAGE), 1).astype(F32)
    moe = jnp.dot((col == dest).astype(BF16), stage[slot], preferred_element_type=F32)
    mod = mod_ref[...]
    out = _rms(x1_ref[...] + mod[:, 5 * D:6 * D] * moe) * g_ref[...]

    @pl.when(i < CB_NPT)
    def _():
        yp_ref[...] = out

    @pl.when(i >= CB_NPT)
    def _():
        ysm_ref[...] = out


def _combine(chunk_off, pc, info, x1, mod3, fg, ys):
    npt = CB_NPT
    grid_spec = pltpu.PrefetchScalarGridSpec(
        num_scalar_prefetch=2,
        grid=(RT_NT,),
        in_specs=[pl.BlockSpec((RT_TM, ROUTE_W), lambda i, o, p: (i, 0)),
                  pl.BlockSpec((RT_TM, D), lambda i, o, p: (i, 0)),
                  pl.BlockSpec((None, 1, 6 * D), lambda i, o, p: (_mod_row(RT_TM)(i), 0, 0)),
                  pl.BlockSpec((1, D), lambda i, o, p: (0, 0)),
                  pl.BlockSpec(memory_space=pl.ANY)],
        out_specs=(pl.BlockSpec((RT_TM, D), lambda i, o, p: (jnp.minimum(i, npt - 1), 0)),
                   pl.BlockSpec((RT_TM, D), lambda i, o, p: (jnp.maximum(i - npt, 0), 0))),
        scratch_shapes=[pltpu.VMEM((2, R_STAGE, D), BF16),
                        pltpu.SemaphoreType.DMA((2,))])
    return pl.pallas_call(
        _combine_kernel,
        out_shape=(jax.ShapeDtypeStruct((N_PROMPT, D), F32),
                   jax.ShapeDtypeStruct((N_SAMPLE, D), F32)),
        grid_spec=grid_spec,
        compiler_params=_cparams(("arbitrary",)),
        name="combine",
    )(chunk_off, pc, info, x1, mod3, fg, ys)


def kernel(x_prompt, x_sample, cache_diff_k, cache_diff_v, state_ret_fwd, state_ret_bwd, c, c_ctx,
           w_ada, b_ada, norm1_g, norm2_g, w_in, ret_decay_fwd, ret_decay_bwd, ret_norm_g,
           diff_lambda_q1, diff_lambda_k1, diff_lambda_q2, diff_lambda_k2, diff_subln_g,
           w_ret_o, w_diff_o, w_o, router_group_w, router_group_b, router_expert_w, router_expert_b,
           moe_w_gate, moe_w_up, moe_w_down, final_norm_g):
    l = 0
    lam_init = 0.8 - 0.6 * math.exp(-0.3 * l)
    lam = (jnp.exp(jnp.sum(diff_lambda_q1[l].astype(F32) * diff_lambda_k1[l].astype(F32)))
           - jnp.exp(jnp.sum(diff_lambda_q2[l].astype(F32) * diff_lambda_k2[l].astype(F32))) + lam_init)
    attn_scal = jnp.stack([lam, jnp.float32(1.0 - lam_init)]).astype(F32)
    lg = jnp.stack([jax.nn.log_sigmoid(ret_decay_fwd[l].astype(F32)),
                    jax.nn.log_sigmoid(ret_decay_bwd[l].astype(F32))])

    xp = x_prompt.reshape(N_PROMPT, D)
    xs = x_sample.reshape(N_SAMPLE, D)
    cvec = jnp.concatenate([c_ctx[None, :], c, jnp.zeros((3, D), F32)], axis=0)
    mod3 = _modulation(cvec, w_ada[l], b_ada[l][None, :]).reshape(8, 1, 6 * D)

    proj, kt32, v32 = _inproj(xp, xs, mod3, norm1_g[l][None, :], w_in[l])

    ret_act, s_f, s_b = _retention(lg, proj, state_ret_fwd[:, l], state_ret_bwd[:, l],
                                   ret_norm_g[l][None, :])
    cos_t, sin_t = _rope_tables()
    cache_kt = jnp.transpose(cache_diff_k[:, l], (0, 2, 3, 4, 1)).reshape(4, D, PAST)
    (diff_act, wg_bf, wu_bf, wd_bf, xs0, cws0) = _attention(
        attn_scal, proj, cache_kt, cache_diff_v, cos_t, sin_t, diff_subln_g[l][None, :],
        (moe_w_gate[l], moe_w_up[l], moe_w_down[l]))

    br = jnp.concatenate([router_group_b[l], router_expert_b[l],
                          jnp.zeros((ROUTE_W - N_GROUPS - N_EXP,), F32)])[None, :]
    x1, h2, cw8, info, pc_arr = _outproj(ret_act, diff_act, proj, xp, xs, mod3, norm2_g[l][None, :],
                                         w_ret_o[l], w_diff_o[l], w_o[l],
                                         router_group_w[l], router_expert_w[l], br)
    chunk_off, pc, tile_group, tile_valid = _dispatch_plan(pc_arr)
    xs_sorted, cw_sorted = _dispatch(chunk_off, pc, h2, info, cw8, xs0, cws0)
    y_sorted = _experts(tile_group, tile_valid, xs_sorted, cw_sorted, wg_bf, wu_bf, wd_bf)
    yp, ys = _combine(chunk_off, pc, info, x1, mod3, final_norm_g[None, :], y_sorted)

    return (yp.reshape(16, T_P, D), ys.reshape(4, T_S, D),
            jnp.transpose(kt32.reshape(16, DIFF_H, 2, DIFF_HD, T_P), (0, 4, 1, 2, 3))[:, None],
            v32.reshape(16, 1, T_P, DIFF_H, 2 * DIFF_HD),
            s_f.reshape(16, 1, RET_H, RET_KD, RET_VD), s_b.reshape(16, 1, RET_H, RET_KD, RET_VD))
```

```python
import functools
import math

import jax
import jax.numpy as jnp
import numpy as np
from jax import lax
from jax.experimental import pallas as pl
from jax.experimental.pallas import tpu as pltpu

F32 = jnp.float32
BF16 = jnp.bfloat16

D = 1024
N_PROMPT = 16 * 256
N_SAMPLE = 4 * 1024
N_TOK = N_PROMPT + N_SAMPLE
T_P = 256
T_S = 1024
PAST = 512
GRID_W = 64
RET_H = 4
RET_KD = 128
RET_VD = 256
DIFF_H = 8
DIFF_HD = 64
ROPE_BASE = 10000.0
N_GROUPS = 4
EPG = 8
N_EXP = 32
FF = 256
EPS = 1e-6
IN_W = 8192
LANE = 128
N_CHUNK = IN_W // LANE
C_RQ, C_RK, C_RV, C_RG, C_DQ, C_DK, C_DV, C_GR, C_GD = 0, 4, 8, 16, 24, 32, 40, 48, 56
ROUTE_W = 128
SEQ_BLK = 1024
N_PBLK = N_PROMPT // SEQ_BLK
VMEM_LIMIT = 56 * 1024 * 1024


def _cparams(sem):
    return pltpu.CompilerParams(dimension_semantics=sem, vmem_limit_bytes=VMEM_LIMIT)


def _mod_row(tile_rows):
    def f(i):
        start = i * tile_rows
        return jnp.where(start < N_PROMPT, 0, 1 + (start - N_PROMPT) // T_S)
    return f


def _rms(x):
    return x * lax.rsqrt(jnp.mean(x * x, axis=-1, keepdims=True) + EPS)


def _mod_kernel(c_ref, w_ref, b_ref, o_ref):
    s = jax.nn.silu(c_ref[...])
    o_ref[...] = jnp.dot(s.astype(BF16), w_ref[...].astype(BF16),
                         preferred_element_type=F32) + b_ref[...]


def _modulation(cvec, w_ada, b_ada):
    tn = 1536
    return pl.pallas_call(
        _mod_kernel,
        out_shape=jax.ShapeDtypeStruct((8, 6 * D), F32),
        grid=(6 * D // tn,),
        in_specs=[pl.BlockSpec((8, D), lambda j: (0, 0)),
                  pl.BlockSpec((D, tn), lambda j: (0, j)),
                  pl.BlockSpec((1, tn), lambda j: (0, j))],
        out_specs=pl.BlockSpec((8, tn), lambda j: (0, j)),
        compiler_params=_cparams(("arbitrary",)),
        name="mod",
    )(cvec, w_ada, b_ada)


IP_TM = 512
IP_TN = 2048
IP_NPT = N_PROMPT // IP_TM


IP_KV_TILE = C_DK * LANE // IP_TN
IP_SPT = IP_TM // T_P


def _inproj_kernel(xp_ref, xs_ref, mod_ref, n1_ref, w_ref, proj_ref, kt_ref, v32_ref, w_scr):
    j = pl.program_id(0)
    i = pl.program_id(1)

    @pl.when(i == 0)
    def _():
        w_scr[...] = w_ref[...].astype(BF16)

    x = jnp.where(i < IP_NPT, xp_ref[...], xs_ref[...])
    mod = mod_ref[...]
    h = (_rms(x) * n1_ref[...] * (1.0 + mod[:, D:2 * D]) + mod[:, 0:D]).astype(BF16)
    acc = jnp.dot(h, w_scr[...], preferred_element_type=F32)
    for c in range(IP_TN // LANE):
        proj_ref[c] = acc[:, c * LANE:(c + 1) * LANE].astype(BF16)

    @pl.when(jnp.logical_and(j == IP_KV_TILE, i < IP_NPT))
    def _():
        for s in range(IP_SPT):
            kt_ref[s] = acc[s * T_P:(s + 1) * T_P, :D].T
        v32_ref[...] = acc[:, D:]


def _inproj(xp, xs, mod3, n1, w_in):
    npt = IP_NPT
    cpt = IP_TN // LANE

    def kv_tile(j, i):
        return jnp.where(j < IP_KV_TILE, 0, jnp.where(j == IP_KV_TILE, jnp.minimum(i, npt - 1), npt - 1))

    return pl.pallas_call(
        _inproj_kernel,
        out_shape=(jax.ShapeDtypeStruct((N_CHUNK, N_TOK, LANE), BF16),
                   jax.ShapeDtypeStruct((N_PROMPT // T_P, D, T_P), F32),
                   jax.ShapeDtypeStruct((N_PROMPT, D), F32)),
        grid=(IN_W // IP_TN, N_TOK // IP_TM),
        in_specs=[pl.BlockSpec((IP_TM, D), lambda j, i: (jnp.minimum(i, npt - 1), 0)),
                  pl.BlockSpec((IP_TM, D), lambda j, i: (jnp.maximum(i - npt, 0), 0)),
                  pl.BlockSpec((None, 1, 6 * D), lambda j, i: (_mod_row(IP_TM)(i), 0, 0)),
                  pl.BlockSpec((1, D), lambda j, i: (0, 0)),
                  pl.BlockSpec((D, IP_TN), lambda j, i: (0, j))],
        out_specs=(pl.BlockSpec((cpt, IP_TM, LANE), lambda j, i: (j, i, 0)),
                   pl.BlockSpec((IP_SPT, D, T_P), lambda j, i: (kv_tile(j, i), 0, 0)),
                   pl.BlockSpec((IP_TM, D), lambda j, i: (kv_tile(j, i), 0))),
        scratch_shapes=[pltpu.VMEM((D, IP_TN), BF16)],
        compiler_params=_cparams(("arbitrary", "arbitrary")),
        name="inproj",
    )(xp, xs, mod3, n1, w_in)


def _decay_mask(t, lgf, lgb):
    ii = lax.broadcasted_iota(jnp.int32, (t, t), 0)
    jj = lax.broadcasted_iota(jnp.int32, (t, t), 1)
    rel = (ii - jj).astype(F32)
    e = jnp.exp(jnp.where(rel >= 0.0, lgf, -lgb) * rel)
    return jnp.where(rel == 0.0, 2.0, e) * (RET_KD ** -0.5)


def _ret_parts(lgf, lgb, q_ref, k_ref, v_ref, rg_ref, s0f_ref, s0b_ref, g_ref,
               o_ref, sf_ref, sb_ref, dm_ref):
    gain = g_ref[...]
    nt = (((1,), (1,)), ((), ()))
    tn = (((0,), (0,)), ((), ()))

    def finish(o, rg):
        d = o - jnp.mean(o, axis=-1, keepdims=True)
        y = d * lax.rsqrt(jnp.mean(d * d, axis=-1, keepdims=True) + EPS) * gain
        return (jax.nn.silu(rg.astype(F32)) * y).astype(BF16)

    def vcat(r):
        return jnp.concatenate([v_ref[0, r, :], v_ref[1, r, :]], axis=1)

    def gcat(r):
        return jnp.concatenate([rg_ref[0, r, :], rg_ref[1, r, :]], axis=1)

    def build_mask():
        dm_ref[...] = _decay_mask(T_P, lgf, lgb)

    n_chunk = SEQ_BLK // T_P
    chunks = [slice(c * T_P, (c + 1) * T_P) for c in range(n_chunk)]
    t = lax.broadcasted_iota(jnp.int32, (T_P, 1), 0).astype(F32)
    kdf = jnp.exp(lgf * (T_P - 1.0 - t)) * (RET_KD ** -0.5)
    kdb = jnp.exp(lgb * t) * (RET_KD ** -0.5)

    def intra(r):
        sc = lax.dot_general(q_ref[r, :], k_ref[r, :], nt, preferred_element_type=F32)
        return jnp.dot((sc * dm_ref[...]).astype(BF16), vcat(r), preferred_element_type=F32)

    def key_state(r, kd):
        kw = (k_ref[r, :].astype(F32) * kd).astype(BF16)
        return lax.dot_general(kw, vcat(r), tn, preferred_element_type=F32)

    def context_part():
        for s, r in enumerate(chunks):
            o_ref[r, :] = finish(intra(r), gcat(r))
            sf_ref[s] = key_state(r, kdf)
            sb_ref[s] = key_state(r, kdb)

    def latent_part():
        qdf = jnp.exp(lgf * (t + 1.0))
        qdb = jnp.exp(lgb * (T_P - t))
        span = jnp.full((1, 1), float(T_P), F32)
        cf = jnp.exp(lgf * span)
        cb = jnp.exp(lgb * span)
        before_b = [None] * n_chunk
        state = s0b_ref[...]
        for c in reversed(range(n_chunk)):
            before_b[c] = state
            if c > 0:
                state = cb * state + key_state(chunks[c], kdb)
        state = s0f_ref[...]
        for c, r in enumerate(chunks):
            qf = q_ref[r, :].astype(F32)
            o = (intra(r)
                 + jnp.dot((qf * qdf).astype(BF16), state.astype(BF16), preferred_element_type=F32)
                 + jnp.dot((qf * qdb).astype(BF16), before_b[c].astype(BF16), preferred_element_type=F32))
            o_ref[r, :] = finish(o, gcat(r))
            if c + 1 < n_chunk:
                state = cf * state + key_state(r, kdf)

    return build_mask, context_part, latent_part


ATT_TQ = 1024
ATT_HPS = 2
ATT_QSCALE = (DIFF_HD ** -0.5) * math.log2(math.e)


N_SIDE = 3


def _mixer_kernel(sc_ref, lg_ref, q_ref, k_ref, v_ref, ckt_ref, cv_ref, cos_ref, sin_ref, g_ref,
                  rq_ref, rk_ref, rv_ref, rg_ref, s0f_ref, s0b_ref, gn_ref, *rest):
    side_in = rest[:N_SIDE]
    o_ref, ro_ref, sf_ref, sb_ref = rest[N_SIDE:N_SIDE + 4]
    side_out = rest[N_SIDE + 4:2 * N_SIDE + 4]
    xs0_ref, cws0_ref, q_scr, k_scr, v_scr, dm_scr = rest[2 * N_SIDE + 4:]
    i = pl.program_id(0)
    rh = pl.program_id(1)
    build_mask, ret_context, ret_latent = _ret_parts(
        lg_ref[0, rh], lg_ref[1, rh], rq_ref, rk_ref, rv_ref, rg_ref, s0f_ref, s0b_ref, gn_ref,
        ro_ref, sf_ref, sb_ref, dm_scr.at[rh])

    @pl.when(i == 0)
    def _():
        build_mask()

    def side_jobs():
        for src, dst in zip(side_in, side_out):
            dst[...] = src[...].astype(BF16)
        xs0_ref[...] = jnp.zeros_like(xs0_ref)
        cws0_ref[...] = jnp.zeros_like(cws0_ref)

    lam = sc_ref[0]
    out_scale = sc_ref[1]
    gain = g_ref[...] * out_scale
    lane = lax.broadcasted_iota(jnp.int32, (1, LANE), 1)
    first = lane < DIFF_HD
    nt = (((1,), (1,)), ((), ()))

    def halves(q):
        zero = jnp.zeros_like(q)
        return jnp.where(first, q, zero), jnp.where(first, zero, q)

    def weights(s):
        return jnp.exp2(s - jnp.max(s, axis=-1, keepdims=True)).astype(BF16)

    def finish(of0, of1):
        o = of0[:, :LANE] / of0[:, LANE:] - lam * (of1[:, :LANE] / of1[:, LANE:])
        return (_rms(o) * gain).astype(BF16)

    @pl.when(i < N_PBLK)
    def _():
        side_jobs()
        ret_context()
        n_seq = SEQ_BLK // T_P
        rows = [slice(s * T_P, (s + 1) * T_P) for s in range(n_seq)]
        ones = jnp.ones((T_P, LANE), BF16)
        for hh in range(ATT_HPS):
            cols = slice(hh * LANE, (hh + 1) * LANE)
            q0, q1 = halves((q_ref[hh].astype(F32) * ATT_QSCALE).astype(BF16))
            s0 = jnp.concatenate([lax.dot_general(q0[r], k_ref[hh, r, :], nt, preferred_element_type=F32)
                                  for r in rows], axis=0)
            s1 = jnp.concatenate([lax.dot_general(q1[r], k_ref[hh, r, :], nt, preferred_element_type=F32)
                                  for r in rows], axis=0)
            e0 = weights(s0)
            e1 = weights(s1)
            for r in rows:
                v1 = jnp.concatenate([v_ref[hh, r, :], ones], axis=1)
                o_ref[r, cols] = finish(jnp.dot(e0[r], v1, preferred_element_type=F32),
                                        jnp.dot(e1[r], v1, preferred_element_type=F32))

    @pl.when(i >= N_PBLK)
    def _():
        side_jobs()
        ret_latent()
        cos = cos_ref[...]
        sin = sin_ref[...]
        low = (lax.broadcasted_iota(jnp.int32, (T_S, LANE), 1) & 16) == 0

        def rope(x):
            xs = jnp.where(low, pltpu.roll(x, LANE - 16, 1), pltpu.roll(x, 16, 1))
            return x * cos + xs * sin

        for hh in range(ATT_HPS):
            cols = slice(hh * LANE, (hh + 1) * LANE)
            head = pl.program_id(1) * ATT_HPS + hh
            q_scr[hh] = (rope(q_ref[hh].astype(F32)) * ATT_QSCALE).astype(BF16)
            k_scr[hh] = rope(k_ref[hh].astype(F32)).astype(BF16)
            ckt = ckt_ref[cols, :].astype(BF16)
            v_scr[hh, 0:T_S, 0:LANE] = v_ref[hh]
            v_scr[hh, T_S:T_S + PAST, 0:LANE] = cv_ref[:, head, :].astype(BF16)
            v_scr[hh, :, LANE:2 * LANE] = jnp.ones((T_S + PAST, LANE), BF16)

            def scores(qh, hh=hh, ckt=ckt):
                return jnp.concatenate([lax.dot_general(qh, k_scr[hh], nt, preferred_element_type=F32),
                                        jnp.dot(qh, ckt, preferred_element_type=F32)], axis=1)

            for b in range(T_S // ATT_TQ):
                r = slice(b * ATT_TQ, (b + 1) * ATT_TQ)
                q0, q1 = halves(q_scr[hh, r, :])
                e0 = weights(scores(q0))
                e1 = weights(scores(q1))
                o_ref[r, cols] = finish(jnp.dot(e0, v_scr[hh], preferred_element_type=F32),
                                        jnp.dot(e1, v_scr[hh], preferred_element_type=F32))


def _mixers(scal, lg, proj, cache_k, cache_v, cos_t, sin_t, subln_g, s0f, s0b, gnorm, side_weights):
    nb = N_TOK // SEQ_BLK
    spb = SEQ_BLK // T_P
    smp = lambda i: jnp.maximum(i - N_PBLK, 0)
    pmt = lambda i: jnp.minimum(i, N_PBLK - 1)
    pmh = lambda i, h: jnp.where(i < N_PBLK, h, RET_H - 1)
    state_in = pl.BlockSpec((None, None, RET_KD, RET_VD), lambda i, h: (smp(i), h, 0, 0))
    state_out = pl.BlockSpec((spb, None, RET_KD, RET_VD), lambda i, h: (pmt(i), pmh(i, h), 0, 0))

    n_hp = DIFF_H // ATT_HPS
    assert n_hp == RET_H
    n_steps = nb * n_hp
    step = lambda i, h: i * n_hp + h
    per_exp = n_steps // N_EXP
    exp_slice = lambda i, h: (step(i, h) // per_exp, step(i, h) % per_exp, 0)
    row_slice = lambda i, h: (step(i, h), 0)
    up_spec = pl.BlockSpec((None, D // per_exp, FF), exp_slice)
    down_spec = pl.BlockSpec((None, FF // per_exp, D), exp_slice)
    side_specs = [up_spec, up_spec, down_spec]
    side_shapes = [jax.ShapeDtypeStruct(a.shape, BF16) for a in side_weights]
    zrows = R_MAX // n_steps

    return pl.pallas_call(
        _mixer_kernel,
        out_shape=(jax.ShapeDtypeStruct((N_TOK, DIFF_H * 2 * DIFF_HD), BF16),
                   jax.ShapeDtypeStruct((N_TOK, RET_H * RET_VD), BF16),
                   jax.ShapeDtypeStruct((16, RET_H, RET_KD, RET_VD), F32),
                   jax.ShapeDtypeStruct((16, RET_H, RET_KD, RET_VD), F32),
                   *side_shapes,
                   jax.ShapeDtypeStruct((R_MAX, D), BF16),
                   jax.ShapeDtypeStruct((R_MAX, ROUTE_W), F32)),
        grid=(nb, n_hp),
        in_specs=[pl.BlockSpec(memory_space=pltpu.SMEM),
                  pl.BlockSpec(memory_space=pltpu.SMEM),
                  pl.BlockSpec((ATT_HPS, SEQ_BLK, LANE), lambda i, h: (C_DQ // ATT_HPS + h, i, 0)),
                  pl.BlockSpec((ATT_HPS, SEQ_BLK, LANE), lambda i, h: (C_DK // ATT_HPS + h, i, 0)),
                  pl.BlockSpec((ATT_HPS, SEQ_BLK, LANE), lambda i, h: (C_DV // ATT_HPS + h, i, 0)),
                  pl.BlockSpec((None, ATT_HPS * LANE, PAST), lambda i, h: (smp(i), h, 0)),
                  pl.BlockSpec((None, None, PAST, DIFF_H, LANE), lambda i, h: (smp(i), 0, 0, 0, 0)),
                  pl.BlockSpec((T_S, LANE), lambda i, h: (0, 0)),
                  pl.BlockSpec((T_S, LANE), lambda i, h: (0, 0)),
                  pl.BlockSpec((1, LANE), lambda i, h: (0, 0)),
                  pl.BlockSpec((None, SEQ_BLK, LANE), lambda i, h: (C_RQ + h, i, 0)),
                  pl.BlockSpec((None, SEQ_BLK, LANE), lambda i, h: (C_RK + h, i, 0)),
                  pl.BlockSpec((2, SEQ_BLK, LANE), lambda i, h: (C_RV // 2 + h, i, 0)),
                  pl.BlockSpec((2, SEQ_BLK, LANE), lambda i, h: (C_RG // 2 + h, i, 0)),
                  state_in, state_in,
                  pl.BlockSpec((1, RET_VD), lambda i, h: (0, h)),
                  *side_specs],
        out_specs=(pl.BlockSpec((SEQ_BLK, ATT_HPS * LANE), lambda i, h: (i, h)),
                   pl.BlockSpec((SEQ_BLK, RET_VD), lambda i, h: (i, h)),
                   state_out, state_out,
                   *side_specs,
                   pl.BlockSpec((zrows, D), row_slice),
                   pl.BlockSpec((zrows, ROUTE_W), row_slice)),
        scratch_shapes=[pltpu.VMEM((ATT_HPS, T_S, LANE), BF16),
                        pltpu.VMEM((ATT_HPS, T_S, LANE), BF16),
                        pltpu.VMEM((ATT_HPS, T_S + PAST, 2 * LANE), BF16),
                        pltpu.VMEM((RET_H, T_P, T_P), F32)],
        compiler_params=_cparams(("arbitrary", "arbitrary")),
        name="mixers",
    )(scal, lg, proj, proj, proj, cache_k, cache_v, cos_t, sin_t, subln_g,
      proj, proj, proj, proj, s0f, s0b, gnorm, *side_weights)


def _rope_tables():
    n_rows = T_S // GRID_W
    row = np.repeat(np.arange(n_rows), GRID_W).astype(np.float64)
    col = np.tile(np.arange(GRID_W), n_rows).astype(np.float64)
    n_freq = DIFF_HD // 4
    inv = ROPE_BASE ** (-np.arange(n_freq, dtype=np.float64) / n_freq)

    def axis_tables(pos):
        ang = pos[:, None] * inv[None, :]
        c = np.cos(ang)
        s = np.sin(ang)
        return np.concatenate([c, c], axis=-1), np.concatenate([-s, s], axis=-1)

    cr, sr = axis_tables(row)
    cc, sc = axis_tables(col)
    cos_h = np.concatenate([cr, cc], axis=-1)
    sin_h = np.concatenate([sr, sc], axis=-1)
    return (jnp.asarray(np.concatenate([cos_h, cos_h], axis=-1), F32),
            jnp.asarray(np.concatenate([sin_h, sin_h], axis=-1), F32))


OP_TM = 512
OP_NPT = N_PROMPT // OP_TM


def _outproj_kernel(ra_ref, da_ref, gr_ref, gd_ref, xp_ref, xs_ref, mod_ref, n2_ref,
                    wro32_ref, wdo32_ref, wo32_ref, rgw_ref, rew_ref, br_ref,
                    x1_ref, h2_ref, cw8_ref, info_ref, pc_ref,
                    m_scr, wro_ref, wdo_ref, wo_ref, wr_ref):
    i = pl.program_id(0)

    @pl.when(i == 0)
    def _():
        wro_ref[...] = wro32_ref[...].astype(BF16)
        wdo_ref[...] = wdo32_ref[...].astype(BF16)
        wo_ref[...] = wo32_ref[...].astype(BF16)
        pad = jnp.zeros((D, ROUTE_W - N_GROUPS - N_EXP), F32)
        wr_ref[...] = jnp.concatenate([rgw_ref[...], rew_ref[...], pad], axis=1).astype(BF16)

    ret_out = jnp.dot(ra_ref[...], wro_ref[...], preferred_element_type=F32)
    diff_out = jnp.dot(da_ref[...], wdo_ref[...], preferred_element_type=F32)
    for c in range(D // LANE):
        sl = slice(c * LANE, (c + 1) * LANE)
        m = (jax.nn.sigmoid(gr_ref[c].astype(F32)) * ret_out[:, sl]
             + jax.nn.sigmoid(gd_ref[c].astype(F32)) * diff_out[:, sl])
        m_scr[:, sl] = m.astype(BF16)
    mix = jnp.dot(m_scr[...], wo_ref[...], preferred_element_type=F32)
    mod = mod_ref[...]

    x1 = jnp.where(i < OP_NPT, xp_ref[...], xs_ref[...]) + mod[:, 2 * D:3 * D] * mix
    x1_ref[...] = x1
    h2 = (_rms(x1) * n2_ref[...] * (1.0 + mod[:, 4 * D:5 * D]) + mod[:, 3 * D:4 * D]).astype(BF16)
    h2_ref[...] = h2
    logits = jnp.dot(h2, wr_ref[...], preferred_element_type=F32) + br_ref[...]
    cw8_ref[...], info_ref[...], pc_ref[...] = _route_rows(logits)


def _outproj(ret_act, diff_act, proj, xp, xs, mod3, n2, wro, wdo, wo, rgw, rew, br):
    assert OP_TM == RT_TM
    npt = OP_NPT
    full = lambda i: (0, 0)
    once = pl.Buffered(1)
    return pl.pallas_call(
        _outproj_kernel,
        out_shape=(jax.ShapeDtypeStruct((N_TOK, D), F32),
                   jax.ShapeDtypeStruct((N_TOK, D), BF16),
                   jax.ShapeDtypeStruct((N_TOK, ROUTE_W), F32),
                   jax.ShapeDtypeStruct((N_TOK, ROUTE_W), F32),
                   jax.ShapeDtypeStruct((RT_NT, 1, ROUTE_W), F32)),
        grid=(N_TOK // OP_TM,),
        in_specs=[pl.BlockSpec((OP_TM, D), lambda i: (i, 0)),
                  pl.BlockSpec((OP_TM, D), lambda i: (i, 0)),
                  pl.BlockSpec((8, OP_TM, LANE), lambda i: (C_GR // 8, i, 0)),
                  pl.BlockSpec((8, OP_TM, LANE), lambda i: (C_GD // 8, i, 0)),
                  pl.BlockSpec((OP_TM, D), lambda i: (jnp.minimum(i, npt - 1), 0)),
                  pl.BlockSpec((OP_TM, D), lambda i: (jnp.maximum(i - npt, 0), 0)),
                  pl.BlockSpec((None, 1, 6 * D), lambda i: (_mod_row(OP_TM)(i), 0, 0)),
                  pl.BlockSpec((1, D), full),
                  pl.BlockSpec((D, D), full, pipeline_mode=once),
                  pl.BlockSpec((D, D), full, pipeline_mode=once),
                  pl.BlockSpec((D, D), full, pipeline_mode=once),
                  pl.BlockSpec((D, N_GROUPS), full, pipeline_mode=once),
                  pl.BlockSpec((D, N_EXP), full, pipeline_mode=once),
                  pl.BlockSpec((1, ROUTE_W), full)],
        out_specs=(pl.BlockSpec((OP_TM, D), lambda i: (i, 0)),
                   pl.BlockSpec((OP_TM, D), lambda i: (i, 0)),
                   pl.BlockSpec((OP_TM, ROUTE_W), lambda i: (i, 0)),
                   pl.BlockSpec((OP_TM, ROUTE_W), lambda i: (i, 0)),
                   pl.BlockSpec((None, 1, ROUTE_W), lambda i: (i, 0, 0))),
        scratch_shapes=[pltpu.VMEM((OP_TM, D), BF16),
                        pltpu.VMEM((D, D), BF16), pltpu.VMEM((D, D), BF16), pltpu.VMEM((D, D), BF16),
                        pltpu.VMEM((D, ROUTE_W), BF16)],
        compiler_params=_cparams(("arbitrary",)),
        name="outproj",
    )(ret_act, diff_act, proj, proj, xp, xs, mod3, n2, wro, wdo, wo, rgw, rew, br)


RT_TM = 512
RT_NT = N_TOK // RT_TM
PIECE = 16
R_LOC = RT_TM + N_GROUPS * PIECE
R_STAGE = 640
EX_TM = 512
R_MAX = 11264
EX_NT = R_MAX // EX_TM


def _route_rows(lg):
    lane = lax.broadcasted_iota(jnp.int32, lg.shape, 1)
    neg = jnp.float32(-jnp.inf)
    big = jnp.int32(ROUTE_W)

    def first_where(cond):
        return jnp.min(jnp.where(cond, lane, big), axis=-1, keepdims=True)

    is_g = lane < N_GROUPS
    gl = jnp.where(is_g, lg, neg)
    gmax = jnp.max(gl, axis=-1, keepdims=True)
    gsum = jnp.sum(jnp.where(is_g, jnp.exp(lg - gmax), 0.0), axis=-1, keepdims=True)
    p_top = 1.0 / gsum
    g_idx = first_where(gl == gmax)
    in_grp = jnp.logical_and(lane >= N_GROUPS, ((lane - N_GROUPS) >> 3) == g_idx)
    el = jnp.where(in_grp, lg, neg)
    emax = jnp.max(el, axis=-1, keepdims=True)
    ee = jnp.where(in_grp, jnp.exp(lg - emax), 0.0)
    ep = ee / jnp.sum(ee, axis=-1, keepdims=True)
    ep = jnp.where(in_grp, ep, -1.0)
    e1 = jnp.max(ep, axis=-1, keepdims=True)
    i1 = first_where(ep == e1)
    ep2 = jnp.where(lane == i1, -1.0, ep)
    e2 = jnp.max(ep2, axis=-1, keepdims=True)
    i2 = first_where(ep2 == e2)
    den = e1 + e2
    cw = (jnp.where(lane == i1, p_top * e1 / den, 0.0)
          + jnp.where(lane == i2, p_top * e2 / den, 0.0))
    cw8 = jnp.zeros_like(cw)
    for g in range(N_GROUPS):
        cw8 = cw8 + jnp.where(g_idx == g, pltpu.roll(cw, LANE - (N_GROUPS + EPG * g), 1), 0.0)

    onehot = (lane == g_idx).astype(F32)
    ii = lax.broadcasted_iota(jnp.int32, (RT_TM, RT_TM), 0)
    jj = lax.broadcasted_iota(jnp.int32, (RT_TM, RT_TM), 1)
    earlier = (jj < ii).astype(BF16)
    prefix = jnp.dot(earlier, onehot.astype(BF16), preferred_element_type=F32)
    cnt = jnp.sum(onehot, axis=0, keepdims=True)
    pc = jnp.floor((cnt + (PIECE - 1.0)) * (1.0 / PIECE)) * PIECE
    lo = pltpu.roll(pc, 1, 1) + pltpu.roll(pc, 2, 1) + pltpu.roll(pc, 3, 1)
    dest = jnp.sum(onehot * (prefix + lo), axis=-1, keepdims=True)
    info = jnp.where(lane == 0, g_idx.astype(F32), jnp.where(lane == 1, dest, 0.0))
    return cw8, info, pc


def _dispatch_plan(pc_arr):
    pc = pc_arr[:, 0, :N_GROUPS].astype(jnp.int32)
    seg_len = jnp.sum(pc, axis=0)
    seg_pad = (seg_len + (EX_TM - 1)) // EX_TM * EX_TM
    seg_end_pad = jnp.cumsum(seg_pad)
    seg_start = seg_end_pad - seg_pad
    chunk_off = seg_start[None, :] + jnp.cumsum(pc, axis=0) - pc
    tile_start = jnp.arange(EX_NT, dtype=jnp.int32) * EX_TM
    tile_group = jnp.minimum(jnp.sum((tile_start[:, None] >= seg_end_pad[None, :]).astype(jnp.int32), axis=1),
                             N_GROUPS - 1)
    tile_valid = (tile_start < (seg_start + seg_len)[tile_group]).astype(jnp.int32)
    return chunk_off.astype(jnp.int32), pc, tile_group.astype(jnp.int32), tile_valid


def _piece_copies(off_ref, pc_ref, tile, make):
    lo = 0
    for g in range(N_GROUPS):
        n = pc_ref[tile, g] // PIECE
        base = off_ref[tile, g]

        def body(j, carry, lo=lo, base=base):
            make(pl.multiple_of(lo + j * PIECE, PIECE), pl.multiple_of(base + j * PIECE, PIECE))
            return carry

        lax.fori_loop(0, n, body, 0)
        lo = lo + pc_ref[tile, g]


def _piece_count(pc_ref, tile):
    n = 0
    for g in range(N_GROUPS):
        n = n + pc_ref[tile, g] // PIECE
    return n


def _dispatch_kernel(off_ref, pc_ref, h_ref, info_ref, cw8_ref, xs_in, cws_in,
                     xs_out, cws_out, x_scr, c_scr, sem):
    del xs_in, cws_in
    i = pl.program_id(0)
    slot = i % 2
    dest = info_ref[...].T[1:2, :]
    row = lax.broadcasted_iota(jnp.int32, (R_LOC, RT_TM), 0).astype(F32)
    sel = row == dest
    sel = sel.astype(BF16)
    x_scr[slot] = jnp.dot(sel, h_ref[...], preferred_element_type=F32).astype(BF16)
    cw = cw8_ref[...]
    hi = cw.astype(BF16).astype(F32)
    mid = (cw - hi).astype(BF16).astype(F32)
    low = (cw - hi - mid).astype(BF16).astype(F32)
    pieces = (hi + pltpu.roll(mid, EPG, 1) + pltpu.roll(low, 2 * EPG, 1)).astype(BF16)
    c_scr[slot] = jnp.dot(sel, pieces, preferred_element_type=F32)

    def x_copy(s, src, dst):
        return pltpu.make_async_copy(x_scr.at[s, pl.ds(src, PIECE)], xs_out.at[pl.ds(dst, PIECE)],
                                     sem.at[0, s])

    def c_copy(s, src, dst):
        return pltpu.make_async_copy(c_scr.at[s, pl.ds(src, PIECE)], cws_out.at[pl.ds(dst, PIECE)],
                                     sem.at[1, s])

    def start(src, dst):
        x_copy(slot, src, dst).start()
        c_copy(slot, src, dst).start()

    _piece_copies(off_ref, pc_ref, i, start)

    def wait_tile(tile, s):
        def wait(j, carry):
            x_copy(s, 0, 0).wait()
            c_copy(s, 0, 0).wait()
            return carry

        lax.fori_loop(0, _piece_count(pc_ref, tile), wait, 0)

    @pl.when(i > 0)
    def _():
        wait_tile(i - 1, 1 - slot)

    @pl.when(i == RT_NT - 1)
    def _():
        wait_tile(i, slot)


def _dispatch(chunk_off, pc, h2, info, cw8, xs0, cws0):
    grid_spec = pltpu.PrefetchScalarGridSpec(
        num_scalar_prefetch=2,
        grid=(RT_NT,),
        in_specs=[pl.BlockSpec((RT_TM, D), lambda i, o, p: (i, 0)),
                  pl.BlockSpec((RT_TM, ROUTE_W), lambda i, o, p: (i, 0)),
                  pl.BlockSpec((RT_TM, ROUTE_W), lambda i, o, p: (i, 0)),
                  pl.BlockSpec(memory_space=pl.ANY),
                  pl.BlockSpec(memory_space=pl.ANY)],
        out_specs=(pl.BlockSpec(memory_space=pl.ANY), pl.BlockSpec(memory_space=pl.ANY)),
        scratch_shapes=[pltpu.VMEM((2, R_LOC, D), BF16),
                        pltpu.VMEM((2, R_LOC, ROUTE_W), F32),
                        pltpu.SemaphoreType.DMA((2, 2))])
    return pl.pallas_call(
        _dispatch_kernel,
        out_shape=(jax.ShapeDtypeStruct((R_MAX, D), BF16),
                   jax.ShapeDtypeStruct((R_MAX, ROUTE_W), F32)),
        grid_spec=grid_spec,
        input_output_aliases={5: 0, 6: 1},
        compiler_params=_cparams(("arbitrary",)),
        name="dispatch",
    )(chunk_off, pc, h2, info, cw8, xs0, cws0)


def _expert_kernel(tg_ref, tv_ref, x_ref, cw_ref, wg_ref, wu_ref, wd_ref, y_ref):
    del tg_ref
    k = pl.program_id(0)

    @pl.when(tv_ref[k] == 0)
    def _():
        y_ref[...] = jnp.zeros_like(y_ref)

    @pl.when(tv_ref[k] != 0)
    def _():
        x = x_ref[...]
        cw = cw_ref[...]
        lane = lax.broadcasted_iota(jnp.int32, cw.shape, 1)
        acc = jnp.zeros(y_ref.shape, F32)
        for j in range(EPG):
            mine = jnp.logical_and((lane & (EPG - 1)) == j, lane < 3 * EPG)
            w = jnp.sum(jnp.where(mine, cw, 0.0), axis=-1, keepdims=True)
            a = (jax.nn.silu(jnp.dot(x, wg_ref[j], preferred_element_type=F32))
                 * jnp.dot(x, wu_ref[j], preferred_element_type=F32))
            acc = acc + jnp.dot((a * w).astype(BF16), wd_ref[j], preferred_element_type=F32)
        y_ref[...] = acc.astype(BF16)


def _experts(tile_group, tile_valid, xs, cws, wg, wu, wd):
    grid_spec = pltpu.PrefetchScalarGridSpec(
        num_scalar_prefetch=2,
        grid=(EX_NT,),
        in_specs=[pl.BlockSpec((EX_TM, D), lambda k, tg, tv: (k, 0)),
                  pl.BlockSpec((EX_TM, ROUTE_W), lambda k, tg, tv: (k, 0)),
                  pl.BlockSpec((EPG, D, FF), lambda k, tg, tv: (tg[k], 0, 0)),
                  pl.BlockSpec((EPG, D, FF), lambda k, tg, tv: (tg[k], 0, 0)),
                  pl.BlockSpec((EPG, FF, D), lambda k, tg, tv: (tg[k], 0, 0))],
        out_specs=pl.BlockSpec((EX_TM, D), lambda k, tg, tv: (k, 0)))
    return pl.pallas_call(
        _expert_kernel,
        out_shape=jax.ShapeDtypeStruct((R_MAX, D), BF16),
        grid_spec=grid_spec,
        compiler_params=_cparams(("arbitrary",)),
        name="experts",
    )(tile_group, tile_valid, xs, cws, wg, wu, wd)


CB_NPT = N_PROMPT // RT_TM


def _combine_kernel(off_ref, pc_ref, info_ref, x1_ref, mod_ref, g_ref, ys_hbm,
                    yp_ref, ysm_ref, stage, sem):
    i = pl.program_id(0)
    slot = i % 2

    def copy(s, dst, src):
        return pltpu.make_async_copy(ys_hbm.at[pl.ds(src, PIECE)], stage.at[s, pl.ds(dst, PIECE)],
                                     sem.at[s])

    def fetch(tile, s):
        _piece_copies(off_ref, pc_ref, tile, lambda loc, glob: copy(s, loc, glob).start())

    @pl.when(i == 0)
    def _():
        stage[...] = jnp.zeros_like(stage)
        fetch(0, 0)

    @pl.when(i + 1 < RT_NT)
    def _():
        fetch(i + 1, 1 - slot)

    def wait(j, carry):
        copy(slot, 0, 0).wait()
        return carry

    lax.fori_loop(0, _piece_count(pc_ref, i), wait, 0)

    dest = info_ref[...][:, 1:2]
    col = lax.broadcasted_iota(jnp.int32, (RT_TM, R_STAGE), 1).astype(F32)
    moe = jnp.dot((col == dest).astype(BF16), stage[slot], preferred_element_type=F32)
    mod = mod_ref[...]
    out = _rms(x1_ref[...] + mod[:, 5 * D:6 * D] * moe) * g_ref[...]

    @pl.when(i < CB_NPT)
    def _():
        yp_ref[...] = out

    @pl.when(i >= CB_NPT)
    def _():
        ysm_ref[...] = out


def _combine(chunk_off, pc, info, x1, mod3, fg, ys):
    npt = CB_NPT
    grid_spec = pltpu.PrefetchScalarGridSpec(
        num_scalar_prefetch=2,
        grid=(RT_NT,),
        in_specs=[pl.BlockSpec((RT_TM, ROUTE_W), lambda i, o, p: (i, 0)),
                  pl.BlockSpec((RT_TM, D), lambda i, o, p: (i, 0)),
                  pl.BlockSpec((None, 1, 6 * D), lambda i, o, p: (_mod_row(RT_TM)(i), 0, 0)),
                  pl.BlockSpec((1, D), lambda i, o, p: (0, 0)),
                  pl.BlockSpec(memory_space=pl.ANY)],
        out_specs=(pl.BlockSpec((RT_TM, D), lambda i, o, p: (jnp.minimum(i, npt - 1), 0)),
                   pl.BlockSpec((RT_TM, D), lambda i, o, p: (jnp.maximum(i - npt, 0), 0))),
        scratch_shapes=[pltpu.VMEM((2, R_STAGE, D), BF16),
                        pltpu.SemaphoreType.DMA((2,))])
    return pl.pallas_call(
        _combine_kernel,
        out_shape=(jax.ShapeDtypeStruct((N_PROMPT, D), F32),
                   jax.ShapeDtypeStruct((N_SAMPLE, D), F32)),
        grid_spec=grid_spec,
        compiler_params=_cparams(("arbitrary",)),
        name="combine",
    )(chunk_off, pc, info, x1, mod3, fg, ys)


def kernel(x_prompt, x_sample, cache_diff_k, cache_diff_v, state_ret_fwd, state_ret_bwd, c, c_ctx,
           w_ada, b_ada, norm1_g, norm2_g, w_in, ret_decay_fwd, ret_decay_bwd, ret_norm_g,
           diff_lambda_q1, diff_lambda_k1, diff_lambda_q2, diff_lambda_k2, diff_subln_g,
           w_ret_o, w_diff_o, w_o, router_group_w, router_group_b, router_expert_w, router_expert_b,
           moe_w_gate, moe_w_up, moe_w_down, final_norm_g):
    l = 0
    lam_init = 0.8 - 0.6 * math.exp(-0.3 * l)
    lam = (jnp.exp(jnp.sum(diff_lambda_q1[l].astype(F32) * diff_lambda_k1[l].astype(F32)))
           - jnp.exp(jnp.sum(diff_lambda_q2[l].astype(F32) * diff_lambda_k2[l].astype(F32))) + lam_init)
    attn_scal = jnp.stack([lam, jnp.float32(1.0 - lam_init)]).astype(F32)
    lg = jnp.stack([jax.nn.log_sigmoid(ret_decay_fwd[l].astype(F32)),
                    jax.nn.log_sigmoid(ret_decay_bwd[l].astype(F32))])

    xp = x_prompt.reshape(N_PROMPT, D)
    xs = x_sample.reshape(N_SAMPLE, D)
    cvec = jnp.concatenate([c_ctx[None, :], c, jnp.zeros((3, D), F32)], axis=0)
    mod3 = _modulation(cvec, w_ada[l], b_ada[l][None, :]).reshape(8, 1, 6 * D)

    proj, kt32, v32 = _inproj(xp, xs, mod3, norm1_g[l][None, :], w_in[l])

    cos_t, sin_t = _rope_tables()
    cache_kt = jnp.transpose(cache_diff_k[:, l], (0, 2, 3, 4, 1)).reshape(4, D, PAST)
    (diff_act, ret_act, s_f, s_b, wg_bf, wu_bf, wd_bf, xs0, cws0) = _mixers(
        attn_scal, lg, proj, cache_kt, cache_diff_v, cos_t, sin_t, diff_subln_g[l][None, :],
        state_ret_fwd[:, l], state_ret_bwd[:, l], ret_norm_g[l][None, :],
        (moe_w_gate[l], moe_w_up[l], moe_w_down[l]))

    br = jnp.concatenate([router_group_b[l], router_expert_b[l],
                          jnp.zeros((ROUTE_W - N_GROUPS - N_EXP,), F32)])[None, :]
    x1, h2, cw8, info, pc_arr = _outproj(ret_act, diff_act, proj, xp, xs, mod3, norm2_g[l][None, :],
                                         w_ret_o[l], w_diff_o[l], w_o[l],
                                         router_group_w[l], router_expert_w[l], br)
    chunk_off, pc, tile_group, tile_valid = _dispatch_plan(pc_arr)
    xs_sorted, cw_sorted = _dispatch(chunk_off, pc, h2, info, cw8, xs0, cws0)
    y_sorted = _experts(tile_group, tile_valid, xs_sorted, cw_sorted, wg_bf, wu_bf, wd_bf)
    yp, ys = _combine(chunk_off, pc, info, x1, mod3, final_norm_g[None, :], y_sorted)

    return (yp.reshape(16, T_P, D), ys.reshape(4, T_S, D),
            jnp.transpose(kt32.reshape(16, DIFF_H, 2, DIFF_HD, T_P), (0, 4, 1, 2, 3))[:, None],
            v32.reshape(16, 1, T_P, DIFF_H, 2 * DIFF_HD),
            s_f.reshape(16, 1, RET_H, RET_KD, RET_VD), s_b.reshape(16, 1, RET_H, RET_KD, RET_VD))
```

```python
import functools
import math

import jax
import jax.numpy as jnp
import numpy as np
from jax import lax
from jax.experimental import pallas as pl
from jax.experimental.pallas import tpu as pltpu

F32 = jnp.float32
BF16 = jnp.bfloat16

D = 1024
N_PROMPT = 16 * 256
N_SAMPLE = 4 * 1024
N_TOK = N_PROMPT + N_SAMPLE
T_P = 256
T_S = 1024
PAST = 512
GRID_W = 64
RET_H = 4
RET_KD = 128
RET_VD = 256
DIFF_H = 8
DIFF_HD = 64
ROPE_BASE = 10000.0
N_GROUPS = 4
EPG = 8
N_EXP = 32
FF = 256
EPS = 1e-6
IN_W = 8192
LANE = 128
N_CHUNK = IN_W // LANE
C_RQ, C_RK, C_RV, C_RG, C_DQ, C_DK, C_DV, C_GR, C_GD = 0, 4, 8, 16, 24, 32, 40, 48, 56
ROUTE_W = 128
SEQ_BLK = 1024
N_PBLK = N_PROMPT // SEQ_BLK
VMEM_LIMIT = 56 * 1024 * 1024


def _cparams(sem):
    return pltpu.CompilerParams(dimension_semantics=sem, vmem_limit_bytes=VMEM_LIMIT)


def _mod_row(tile_rows):
    def f(i):
        start = i * tile_rows
        return jnp.where(start < N_PROMPT, 0, 1 + (start - N_PROMPT) // T_S)
    return f


def _rms(x):
    return x * lax.rsqrt(jnp.mean(x * x, axis=-1, keepdims=True) + EPS)


def _mod_kernel(c_ref, w_ref, b_ref, o_ref):
    s = jax.nn.silu(c_ref[...])
    o_ref[...] = jnp.dot(s.astype(BF16), w_ref[...].astype(BF16),
                         preferred_element_type=F32) + b_ref[...]


def _modulation(cvec, w_ada, b_ada):
    tn = 1536
    return pl.pallas_call(
        _mod_kernel,
        out_shape=jax.ShapeDtypeStruct((8, 6 * D), F32),
        grid=(6 * D // tn,),
        in_specs=[pl.BlockSpec((8, D), lambda j: (0, 0)),
                  pl.BlockSpec((D, tn), lambda j: (0, j)),
                  pl.BlockSpec((1, tn), lambda j: (0, j))],
        out_specs=pl.BlockSpec((8, tn), lambda j: (0, j)),
        compiler_params=_cparams(("arbitrary",)),
        name="mod",
    )(cvec, w_ada, b_ada)


IP_TM = 512
IP_TN = 2048
IP_NPT = N_PROMPT // IP_TM


IP_KV_TILE = C_DK * LANE // IP_TN
IP_SPT = IP_TM // T_P


def _inproj_kernel(xp_ref, xs_ref, mod_ref, n1_ref, w_ref, proj_ref, kt_ref, v32_ref, w_scr):
    j = pl.program_id(0)
    i = pl.program_id(1)

    @pl.when(i == 0)
    def _():
        w_scr[...] = w_ref[...].astype(BF16)

    x = jnp.where(i < IP_NPT, xp_ref[...], xs_ref[...])
    mod = mod_ref[...]
    h = (_rms(x) * n1_ref[...] * (1.0 + mod[:, D:2 * D]) + mod[:, 0:D]).astype(BF16)
    acc = jnp.dot(h, w_scr[...], preferred_element_type=F32)
    for c in range(IP_TN // LANE):
        proj_ref[c] = acc[:, c * LANE:(c + 1) * LANE].astype(BF16)

    @pl.when(jnp.logical_and(j == IP_KV_TILE, i < IP_NPT))
    def _():
        for s in range(IP_SPT):
            kt_ref[s] = acc[s * T_P:(s + 1) * T_P, :D].T
        v32_ref[...] = acc[:, D:]


def _inproj(xp, xs, mod3, n1, w_in):
    npt = IP_NPT
    cpt = IP_TN // LANE

    def kv_tile(j, i):
        return jnp.where(j < IP_KV_TILE, 0, jnp.where(j == IP_KV_TILE, jnp.minimum(i, npt - 1), npt - 1))

    return pl.pallas_call(
        _inproj_kernel,
        out_shape=(jax.ShapeDtypeStruct((N_CHUNK, N_TOK, LANE), BF16),
                   jax.ShapeDtypeStruct((N_PROMPT // T_P, D, T_P), F32),
                   jax.ShapeDtypeStruct((N_PROMPT, D), F32)),
        grid=(IN_W // IP_TN, N_TOK // IP_TM),
        in_specs=[pl.BlockSpec((IP_TM, D), lambda j, i: (jnp.minimum(i, npt - 1), 0)),
                  pl.BlockSpec((IP_TM, D), lambda j, i: (jnp.maximum(i - npt, 0), 0)),
                  pl.BlockSpec((None, 1, 6 * D), lambda j, i: (_mod_row(IP_TM)(i), 0, 0)),
                  pl.BlockSpec((1, D), lambda j, i: (0, 0)),
                  pl.BlockSpec((D, IP_TN), lambda j, i: (0, j))],
        out_specs=(pl.BlockSpec((cpt, IP_TM, LANE), lambda j, i: (j, i, 0)),
                   pl.BlockSpec((IP_SPT, D, T_P), lambda j, i: (kv_tile(j, i), 0, 0)),
                   pl.BlockSpec((IP_TM, D), lambda j, i: (kv_tile(j, i), 0))),
        scratch_shapes=[pltpu.VMEM((D, IP_TN), BF16)],
        compiler_params=_cparams(("arbitrary", "arbitrary")),
        name="inproj",
    )(xp, xs, mod3, n1, w_in)


def _decay_mask(t, lgf, lgb):
    ii = lax.broadcasted_iota(jnp.int32, (t, t), 0)
    jj = lax.broadcasted_iota(jnp.int32, (t, t), 1)
    rel = (ii - jj).astype(F32)
    e = jnp.exp(jnp.where(rel >= 0.0, lgf, -lgb) * rel)
    return jnp.where(rel == 0.0, 2.0, e) * (RET_KD ** -0.5)


def _ret_parts(lgf, lgb, q_ref, k_ref, v_ref, rg_ref, s0f_ref, s0b_ref, g_ref,
               o_ref, sf_ref, sb_ref, dm_ref):
    gain = g_ref[...]
    nt = (((1,), (1,)), ((), ()))
    tn = (((0,), (0,)), ((), ()))

    def finish(o, rg):
        d = o - jnp.mean(o, axis=-1, keepdims=True)
        y = d * lax.rsqrt(jnp.mean(d * d, axis=-1, keepdims=True) + EPS) * gain
        return (jax.nn.silu(rg.astype(F32)) * y).astype(BF16)

    def vcat(r):
        return jnp.concatenate([v_ref[0, r, :], v_ref[1, r, :]], axis=1)

    def gcat(r):
        return jnp.concatenate([rg_ref[0, r, :], rg_ref[1, r, :]], axis=1)

    def build_mask():
        dm_ref[...] = _decay_mask(T_P, lgf, lgb)

    n_chunk = SEQ_BLK // T_P
    chunks = [slice(c * T_P, (c + 1) * T_P) for c in range(n_chunk)]
    t = lax.broadcasted_iota(jnp.int32, (T_P, 1), 0).astype(F32)
    kdf = jnp.exp(lgf * (T_P - 1.0 - t)) * (RET_KD ** -0.5)
    kdb = jnp.exp(lgb * t) * (RET_KD ** -0.5)

    def intra(r):
        sc = lax.dot_general(q_ref[r, :], k_ref[r, :], nt, preferred_element_type=F32)
        return jnp.dot((sc * dm_ref[...]).astype(BF16), vcat(r), preferred_element_type=F32)

    def key_state(r, kd):
        kw = (k_ref[r, :].astype(F32) * kd).astype(BF16)
        return lax.dot_general(kw, vcat(r), tn, preferred_element_type=F32)

    def context_part():
        for s, r in enumerate(chunks):
            o_ref[r, :] = finish(intra(r), gcat(r))
            sf_ref[s] = key_state(r, kdf)
            sb_ref[s] = key_state(r, kdb)

    def latent_part():
        qdf = jnp.exp(lgf * (t + 1.0))
        qdb = jnp.exp(lgb * (T_P - t))
        span = jnp.full((1, 1), float(T_P), F32)
        cf = jnp.exp(lgf * span)
        cb = jnp.exp(lgb * span)
        before_b = [None] * n_chunk
        state = s0b_ref[...]
        for c in reversed(range(n_chunk)):
            before_b[c] = state
            if c > 0:
                state = cb * state + key_state(chunks[c], kdb)
        state = s0f_ref[...]
        for c, r in enumerate(chunks):
            qf = q_ref[r, :].astype(F32)
            o = (intra(r)
                 + jnp.dot((qf * qdf).astype(BF16), state.astype(BF16), preferred_element_type=F32)
                 + jnp.dot((qf * qdb).astype(BF16), before_b[c].astype(BF16), preferred_element_type=F32))
            o_ref[r, :] = finish(o, gcat(r))
            if c + 1 < n_chunk:
                state = cf * state + key_state(r, kdf)

    return build_mask, context_part, latent_part


ATT_TQ = 1024
ATT_HPS = 2
ATT_QSCALE = (DIFF_HD ** -0.5) * math.log2(math.e)


N_SIDE = 3


def _mixer_kernel(sc_ref, lg_ref, q_ref, k_ref, v_ref, ckt_ref, cv_ref, cos_ref, sin_ref, g_ref,
                  rq_ref, rk_ref, rv_ref, rg_ref, s0f_ref, s0b_ref, gn_ref, *rest):
    side_in = rest[:N_SIDE]
    o_ref, ro_ref, sf_ref, sb_ref = rest[N_SIDE:N_SIDE + 4]
    side_out = rest[N_SIDE + 4:2 * N_SIDE + 4]
    xs0_ref, cws0_ref, q_scr, k_scr, v_scr, dm_scr = rest[2 * N_SIDE + 4:]
    i = pl.program_id(0)
    rh = pl.program_id(1)
    build_mask, ret_context, ret_latent = _ret_parts(
        lg_ref[0, rh], lg_ref[1, rh], rq_ref, rk_ref, rv_ref, rg_ref, s0f_ref, s0b_ref, gn_ref,
        ro_ref, sf_ref, sb_ref, dm_scr.at[rh])

    @pl.when(i == 0)
    def _():
        build_mask()

    def side_jobs():
        for src, dst in zip(side_in, side_out):
            dst[...] = src[...].astype(BF16)
        xs0_ref[...] = jnp.zeros_like(xs0_ref)
        cws0_ref[...] = jnp.zeros_like(cws0_ref)

    lam = sc_ref[0]
    out_scale = sc_ref[1]
    gain = g_ref[...] * out_scale
    lane = lax.broadcasted_iota(jnp.int32, (1, LANE), 1)
    first = lane < DIFF_HD
    nt = (((1,), (1,)), ((), ()))

    def halves(q):
        zero = jnp.zeros_like(q)
        return jnp.where(first, q, zero), jnp.where(first, zero, q)

    def weights(s):
        return jnp.exp2(s - jnp.max(s, axis=-1, keepdims=True)).astype(BF16)

    def finish(of0, of1):
        o = of0[:, :LANE] / of0[:, LANE:] - lam * (of1[:, :LANE] / of1[:, LANE:])
        return (_rms(o) * gain).astype(BF16)

    @pl.when(i < N_PBLK)
    def _():
        ret_context()
        n_seq = SEQ_BLK // T_P
        rows = [slice(s * T_P, (s + 1) * T_P) for s in range(n_seq)]
        ones = jnp.ones((T_P, LANE), BF16)
        for hh in range(ATT_HPS):
            cols = slice(hh * LANE, (hh + 1) * LANE)
            q0, q1 = halves((q_ref[hh].astype(F32) * ATT_QSCALE).astype(BF16))
            s0 = jnp.concatenate([lax.dot_general(q0[r], k_ref[hh, r, :], nt, preferred_element_type=F32)
                                  for r in rows], axis=0)
            s1 = jnp.concatenate([lax.dot_general(q1[r], k_ref[hh, r, :], nt, preferred_element_type=F32)
                                  for r in rows], axis=0)
            e0 = weights(s0)
            e1 = weights(s1)
            for r in rows:
                v1 = jnp.concatenate([v_ref[hh, r, :], ones], axis=1)
                o_ref[r, cols] = finish(jnp.dot(e0[r], v1, preferred_element_type=F32),
                                        jnp.dot(e1[r], v1, preferred_element_type=F32))

    @pl.when(i >= N_PBLK)
    def _():
        side_jobs()
        ret_latent()
        cos = cos_ref[...]
        sin = sin_ref[...]
        low = (lax.broadcasted_iota(jnp.int32, (T_S, LANE), 1) & 16) == 0

        def rope(x):
            xs = jnp.where(low, pltpu.roll(x, LANE - 16, 1), pltpu.roll(x, 16, 1))
            return x * cos + xs * sin

        for hh in range(ATT_HPS):
            cols = slice(hh * LANE, (hh + 1) * LANE)
            head = pl.program_id(1) * ATT_HPS + hh
            q_scr[hh] = (rope(q_ref[hh].astype(F32)) * ATT_QSCALE).astype(BF16)
            k_scr[hh] = rope(k_ref[hh].astype(F32)).astype(BF16)
            ckt = ckt_ref[cols, :].astype(BF16)
            v_scr[hh, 0:T_S, 0:LANE] = v_ref[hh]
            v_scr[hh, T_S:T_S + PAST, 0:LANE] = cv_ref[:, head, :].astype(BF16)
            v_scr[hh, :, LANE:2 * LANE] = jnp.ones((T_S + PAST, LANE), BF16)

            def scores(qh, hh=hh, ckt=ckt):
                return jnp.concatenate([lax.dot_general(qh, k_scr[hh], nt, preferred_element_type=F32),
                                        jnp.dot(qh, ckt, preferred_element_type=F32)], axis=1)

            for b in range(T_S // ATT_TQ):
                r = slice(b * ATT_TQ, (b + 1) * ATT_TQ)
                q0, q1 = halves(q_scr[hh, r, :])
                e0 = weights(scores(q0))
                e1 = weights(scores(q1))
                o_ref[r, cols] = finish(jnp.dot(e0, v_scr[hh], preferred_element_type=F32),
                                        jnp.dot(e1, v_scr[hh], preferred_element_type=F32))


def _mixers(scal, lg, proj, cache_k, cache_v, cos_t, sin_t, subln_g, s0f, s0b, gnorm, side_weights):
    nb = N_TOK // SEQ_BLK
    spb = SEQ_BLK // T_P
    smp = lambda i: jnp.maximum(i - N_PBLK, 0)
    pmt = lambda i: jnp.minimum(i, N_PBLK - 1)
    pmh = lambda i, h: jnp.where(i < N_PBLK, h, RET_H - 1)
    state_in = pl.BlockSpec((None, None, RET_KD, RET_VD), lambda i, h: (smp(i), h, 0, 0))
    state_out = pl.BlockSpec((spb, None, RET_KD, RET_VD), lambda i, h: (pmt(i), pmh(i, h), 0, 0))

    n_hp = DIFF_H // ATT_HPS
    assert n_hp == RET_H
    n_steps = (nb - N_PBLK) * n_hp
    step = lambda i, h: jnp.maximum((i - N_PBLK) * n_hp + h, 0)
    exp_per_step = N_EXP // n_steps
    exp_slice = lambda i, h: (step(i, h), 0, 0)
    row_slice = lambda i, h: (step(i, h), 0)
    up_spec = pl.BlockSpec((exp_per_step, D, FF), exp_slice)
    down_spec = pl.BlockSpec((exp_per_step, FF, D), exp_slice)
    side_specs = [up_spec, up_spec, down_spec]
    side_shapes = [jax.ShapeDtypeStruct(a.shape, BF16) for a in side_weights]
    zrows = R_MAX // n_steps

    return pl.pallas_call(
        _mixer_kernel,
        out_shape=(jax.ShapeDtypeStruct((N_TOK, DIFF_H * 2 * DIFF_HD), BF16),
                   jax.ShapeDtypeStruct((N_TOK, RET_H * RET_VD), BF16),
                   jax.ShapeDtypeStruct((16, RET_H, RET_KD, RET_VD), F32),
                   jax.ShapeDtypeStruct((16, RET_H, RET_KD, RET_VD), F32),
                   *side_shapes,
                   jax.ShapeDtypeStruct((R_MAX, D), BF16),
                   jax.ShapeDtypeStruct((R_MAX, ROUTE_W), F32)),
        grid=(nb, n_hp),
        in_specs=[pl.BlockSpec(memory_space=pltpu.SMEM),
                  pl.BlockSpec(memory_space=pltpu.SMEM),
                  pl.BlockSpec((ATT_HPS, SEQ_BLK, LANE), lambda i, h: (C_DQ // ATT_HPS + h, i, 0)),
                  pl.BlockSpec((ATT_HPS, SEQ_BLK, LANE), lambda i, h: (C_DK // ATT_HPS + h, i, 0)),
                  pl.BlockSpec((ATT_HPS, SEQ_BLK, LANE), lambda i, h: (C_DV // ATT_HPS + h, i, 0)),
                  pl.BlockSpec((None, ATT_HPS * LANE, PAST), lambda i, h: (smp(i), h, 0)),
                  pl.BlockSpec((None, None, PAST, DIFF_H, LANE), lambda i, h: (smp(i), 0, 0, 0, 0)),
                  pl.BlockSpec((T_S, LANE), lambda i, h: (0, 0)),
                  pl.BlockSpec((T_S, LANE), lambda i, h: (0, 0)),
                  pl.BlockSpec((1, LANE), lambda i, h: (0, 0)),
                  pl.BlockSpec((None, SEQ_BLK, LANE), lambda i, h: (C_RQ + h, i, 0)),
                  pl.BlockSpec((None, SEQ_BLK, LANE), lambda i, h: (C_RK + h, i, 0)),
                  pl.BlockSpec((2, SEQ_BLK, LANE), lambda i, h: (C_RV // 2 + h, i, 0)),
                  pl.BlockSpec((2, SEQ_BLK, LANE), lambda i, h: (C_RG // 2 + h, i, 0)),
                  state_in, state_in,
                  pl.BlockSpec((1, RET_VD), lambda i, h: (0, h)),
                  *side_specs],
        out_specs=(pl.BlockSpec((SEQ_BLK, ATT_HPS * LANE), lambda i, h: (i, h)),
                   pl.BlockSpec((SEQ_BLK, RET_VD), lambda i, h: (i, h)),
                   state_out, state_out,
                   *side_specs,
                   pl.BlockSpec((zrows, D), row_slice),
                   pl.BlockSpec((zrows, ROUTE_W), row_slice)),
        scratch_shapes=[pltpu.VMEM((ATT_HPS, T_S, LANE), BF16),
                        pltpu.VMEM((ATT_HPS, T_S, LANE), BF16),
                        pltpu.VMEM((ATT_HPS, T_S + PAST, 2 * LANE), BF16),
                        pltpu.VMEM((RET_H, T_P, T_P), F32)],
        compiler_params=_cparams(("arbitrary", "arbitrary")),
        name="mixers",
    )(scal, lg, proj, proj, proj, cache_k, cache_v, cos_t, sin_t, subln_g,
      proj, proj, proj, proj, s0f, s0b, gnorm, *side_weights)


def _rope_tables():
    n_rows = T_S // GRID_W
    row = np.repeat(np.arange(n_rows), GRID_W).astype(np.float64)
    col = np.tile(np.arange(GRID_W), n_rows).astype(np.float64)
    n_freq = DIFF_HD // 4
    inv = ROPE_BASE ** (-np.arange(n_freq, dtype=np.float64) / n_freq)

    def axis_tables(pos):
        ang = pos[:, None] * inv[None, :]
        c = np.cos(ang)
        s = np.sin(ang)
        return np.concatenate([c, c], axis=-1), np.concatenate([-s, s], axis=-1)

    cr, sr = axis_tables(row)
    cc, sc = axis_tables(col)
    cos_h = np.concatenate([cr, cc], axis=-1)
    sin_h = np.concatenate([sr, sc], axis=-1)
    return (jnp.asarray(np.concatenate([cos_h, cos_h], axis=-1), F32),
            jnp.asarray(np.concatenate([sin_h, sin_h], axis=-1), F32))


OP_TM = 512
OP_NPT = N_PROMPT // OP_TM


def _outproj_kernel(ra_ref, da_ref, gr_ref, gd_ref, xp_ref, xs_ref, mod_ref, n2_ref,
                    wro32_ref, wdo32_ref, wo32_ref, rgw_ref, rew_ref, br_ref,
                    x1_ref, h2_ref, cw8_ref, info_ref, pc_ref,
                    m_scr, wro_ref, wdo_ref, wo_ref, wr_ref):
    i = pl.program_id(0)

    @pl.when(i == 0)
    def _():
        wro_ref[...] = wro32_ref[...].astype(BF16)
        wdo_ref[...] = wdo32_ref[...].astype(BF16)
        wo_ref[...] = wo32_ref[...].astype(BF16)
        pad = jnp.zeros((D, ROUTE_W - N_GROUPS - N_EXP), F32)
        wr_ref[...] = jnp.concatenate([rgw_ref[...], rew_ref[...], pad], axis=1).astype(BF16)

    ret_out = jnp.dot(ra_ref[...], wro_ref[...], preferred_element_type=F32)
    diff_out = jnp.dot(da_ref[...], wdo_ref[...], preferred_element_type=F32)
    for c in range(D // LANE):
        sl = slice(c * LANE, (c + 1) * LANE)
        m = (jax.nn.sigmoid(gr_ref[c].astype(F32)) * ret_out[:, sl]
             + jax.nn.sigmoid(gd_ref[c].astype(F32)) * diff_out[:, sl])
        m_scr[:, sl] = m.astype(BF16)
    mix = jnp.dot(m_scr[...], wo_ref[...], preferred_element_type=F32)
    mod = mod_ref[...]

    x1 = jnp.where(i < OP_NPT, xp_ref[...], xs_ref[...]) + mod[:, 2 * D:3 * D] * mix
    x1_ref[...] = x1
    h2 = (_rms(x1) * n2_ref[...] * (1.0 + mod[:, 4 * D:5 * D]) + mod[:, 3 * D:4 * D]).astype(BF16)
    h2_ref[...] = h2
    logits = jnp.dot(h2, wr_ref[...], preferred_element_type=F32) + br_ref[...]
    cw8_ref[...], info_ref[...], pc_ref[...] = _route_rows(logits)


def _outproj(ret_act, diff_act, proj, xp, xs, mod3, n2, wro, wdo, wo, rgw, rew, br):
    assert OP_TM == RT_TM
    npt = OP_NPT
    full = lambda i: (0, 0)
    once = pl.Buffered(1)
    return pl.pallas_call(
        _outproj_kernel,
        out_shape=(jax.ShapeDtypeStruct((N_TOK, D), F32),
                   jax.ShapeDtypeStruct((N_TOK, D), BF16),
                   jax.ShapeDtypeStruct((N_TOK, ROUTE_W), F32),
                   jax.ShapeDtypeStruct((N_TOK, ROUTE_W), F32),
                   jax.ShapeDtypeStruct((RT_NT, 1, ROUTE_W), F32)),
        grid=(N_TOK // OP_TM,),
        in_specs=[pl.BlockSpec((OP_TM, D), lambda i: (i, 0)),
                  pl.BlockSpec((OP_TM, D), lambda i: (i, 0)),
                  pl.BlockSpec((8, OP_TM, LANE), lambda i: (C_GR // 8, i, 0)),
                  pl.BlockSpec((8, OP_TM, LANE), lambda i: (C_GD // 8, i, 0)),
                  pl.BlockSpec((OP_TM, D), lambda i: (jnp.minimum(i, npt - 1), 0)),
                  pl.BlockSpec((OP_TM, D), lambda i: (jnp.maximum(i - npt, 0), 0)),
                  pl.BlockSpec((None, 1, 6 * D), lambda i: (_mod_row(OP_TM)(i), 0, 0)),
                  pl.BlockSpec((1, D), full),
                  pl.BlockSpec((D, D), full, pipeline_mode=once),
                  pl.BlockSpec((D, D), full, pipeline_mode=once),
                  pl.BlockSpec((D, D), full, pipeline_mode=once),
                  pl.BlockSpec((D, N_GROUPS), full, pipeline_mode=once),
                  pl.BlockSpec((D, N_EXP), full, pipeline_mode=once),
                  pl.BlockSpec((1, ROUTE_W), full)],
        out_specs=(pl.BlockSpec((OP_TM, D), lambda i: (i, 0)),
                   pl.BlockSpec((OP_TM, D), lambda i: (i, 0)),
                   pl.BlockSpec((OP_TM, ROUTE_W), lambda i: (i, 0)),
                   pl.BlockSpec((OP_TM, ROUTE_W), lambda i: (i, 0)),
                   pl.BlockSpec((None, 1, ROUTE_W), lambda i: (i, 0, 0))),
        scratch_shapes=[pltpu.VMEM((OP_TM, D), BF16),
                        pltpu.VMEM((D, D), BF16), pltpu.VMEM((D, D), BF16), pltpu.VMEM((D, D), BF16),
                        pltpu.VMEM((D, ROUTE_W), BF16)],
        compiler_params=_cparams(("arbitrary",)),
        name="outproj",
    )(ret_act, diff_act, proj, proj, xp, xs, mod3, n2, wro, wdo, wo, rgw, rew, br)


RT_TM = 512
RT_NT = N_TOK // RT_TM
PIECE = 16
R_LOC = RT_TM + N_GROUPS * PIECE
R_STAGE = 640
EX_TM = 512
R_MAX = 11264
EX_NT = R_MAX // EX_TM


def _route_rows(lg):
    lane = lax.broadcasted_iota(jnp.int32, lg.shape, 1)
    neg = jnp.float32(-jnp.inf)
    big = jnp.int32(ROUTE_W)

    def first_where(cond):
        return jnp.min(jnp.where(cond, lane, big), axis=-1, keepdims=True)

    is_g = lane < N_GROUPS
    gl = jnp.where(is_g, lg, neg)
    gmax = jnp.max(gl, axis=-1, keepdims=True)
    gsum = jnp.sum(jnp.where(is_g, jnp.exp(lg - gmax), 0.0), axis=-1, keepdims=True)
    p_top = 1.0 / gsum
    g_idx = first_where(gl == gmax)
    in_grp = jnp.logical_and(lane >= N_GROUPS, ((lane - N_GROUPS) >> 3) == g_idx)
    el = jnp.where(in_grp, lg, neg)
    emax = jnp.max(el, axis=-1, keepdims=True)
    ee = jnp.where(in_grp, jnp.exp(lg - emax), 0.0)
    ep = ee / jnp.sum(ee, axis=-1, keepdims=True)
    ep = jnp.where(in_grp, ep, -1.0)
    e1 = jnp.max(ep, axis=-1, keepdims=True)
    i1 = first_where(ep == e1)
    ep2 = jnp.where(lane == i1, -1.0, ep)
    e2 = jnp.max(ep2, axis=-1, keepdims=True)
    i2 = first_where(ep2 == e2)
    den = e1 + e2
    cw = (jnp.where(lane == i1, p_top * e1 / den, 0.0)
          + jnp.where(lane == i2, p_top * e2 / den, 0.0))
    cw8 = jnp.zeros_like(cw)
    for g in range(N_GROUPS):
        cw8 = cw8 + jnp.where(g_idx == g, pltpu.roll(cw, LANE - (N_GROUPS + EPG * g), 1), 0.0)

    onehot = (lane == g_idx).astype(F32)
    ii = lax.broadcasted_iota(jnp.int32, (RT_TM, RT_TM), 0)
    jj = lax.broadcasted_iota(jnp.int32, (RT_TM, RT_TM), 1)
    earlier = (jj < ii).astype(BF16)
    prefix = jnp.dot(earlier, onehot.astype(BF16), preferred_element_type=F32)
    cnt = jnp.sum(onehot, axis=0, keepdims=True)
    pc = jnp.floor((cnt + (PIECE - 1.0)) * (1.0 / PIECE)) * PIECE
    lo = pltpu.roll(pc, 1, 1) + pltpu.roll(pc, 2, 1) + pltpu.roll(pc, 3, 1)
    dest = jnp.sum(onehot * (prefix + lo), axis=-1, keepdims=True)
    info = jnp.where(lane == 0, g_idx.astype(F32), jnp.where(lane == 1, dest, 0.0))
    return cw8, info, pc


def _dispatch_plan(pc_arr):
    pc = pc_arr[:, 0, :N_GROUPS].astype(jnp.int32)
    seg_len = jnp.sum(pc, axis=0)
    seg_pad = (seg_len + (EX_TM - 1)) // EX_TM * EX_TM
    seg_end_pad = jnp.cumsum(seg_pad)
    seg_start = seg_end_pad - seg_pad
    chunk_off = seg_start[None, :] + jnp.cumsum(pc, axis=0) - pc
    tile_start = jnp.arange(EX_NT, dtype=jnp.int32) * EX_TM
    tile_group = jnp.minimum(jnp.sum((tile_start[:, None] >= seg_end_pad[None, :]).astype(jnp.int32), axis=1),
                             N_GROUPS - 1)
    tile_valid = (tile_start < (seg_start + seg_len)[tile_group]).astype(jnp.int32)
    return chunk_off.astype(jnp.int32), pc, tile_group.astype(jnp.int32), tile_valid


def _piece_copies(off_ref, pc_ref, tile, make):
    lo = 0
    for g in range(N_GROUPS):
        n = pc_ref[tile, g] // PIECE
        base = off_ref[tile, g]

        def body(j, carry, lo=lo, base=base):
            make(pl.multiple_of(lo + j * PIECE, PIECE), pl.multiple_of(base + j * PIECE, PIECE))
            return carry

        lax.fori_loop(0, n, body, 0)
        lo = lo + pc_ref[tile, g]


def _piece_count(pc_ref, tile):
    n = 0
    for g in range(N_GROUPS):
        n = n + pc_ref[tile, g] // PIECE
    return n


def _dispatch_kernel(off_ref, pc_ref, h_ref, info_ref, cw8_ref, xs_in, cws_in,
                     xs_out, cws_out, x_scr, c_scr, sem):
    del xs_in, cws_in
    i = pl.program_id(0)
    slot = i % 2
    dest = info_ref[...].T[1:2, :]
    row = lax.broadcasted_iota(jnp.int32, (R_LOC, RT_TM), 0).astype(F32)
    sel = row == dest
    sel = sel.astype(BF16)
    x_scr[slot] = jnp.dot(sel, h_ref[...], preferred_element_type=F32).astype(BF16)
    cw = cw8_ref[...]
    hi = cw.astype(BF16).astype(F32)
    mid = (cw - hi).astype(BF16).astype(F32)
    low = (cw - hi - mid).astype(BF16).astype(F32)
    pieces = (hi + pltpu.roll(mid, EPG, 1) + pltpu.roll(low, 2 * EPG, 1)).astype(BF16)
    c_scr[slot] = jnp.dot(sel, pieces, preferred_element_type=F32)

    def x_copy(s, src, dst):
        return pltpu.make_async_copy(x_scr.at[s, pl.ds(src, PIECE)], xs_out.at[pl.ds(dst, PIECE)],
                                     sem.at[0, s])

    def c_copy(s, src, dst):
        return pltpu.make_async_copy(c_scr.at[s, pl.ds(src, PIECE)], cws_out.at[pl.ds(dst, PIECE)],
                                     sem.at[1, s])

    def start(src, dst):
        x_copy(slot, src, dst).start()
        c_copy(slot, src, dst).start()

    _piece_copies(off_ref, pc_ref, i, start)

    def wait_tile(tile, s):
        def wait(j, carry):
            x_copy(s, 0, 0).wait()
            c_copy(s, 0, 0).wait()
            return carry

        lax.fori_loop(0, _piece_count(pc_ref, tile), wait, 0)

    @pl.when(i > 0)
    def _():
        wait_tile(i - 1, 1 - slot)

    @pl.when(i == RT_NT - 1)
    def _():
        wait_tile(i, slot)


def _dispatch(chunk_off, pc, h2, info, cw8, xs0, cws0):
    grid_spec = pltpu.PrefetchScalarGridSpec(
        num_scalar_prefetch=2,
        grid=(RT_NT,),
        in_specs=[pl.BlockSpec((RT_TM, D), lambda i, o, p: (i, 0)),
                  pl.BlockSpec((RT_TM, ROUTE_W), lambda i, o, p: (i, 0)),
                  pl.BlockSpec((RT_TM, ROUTE_W), lambda i, o, p: (i, 0)),
                  pl.BlockSpec(memory_space=pl.ANY),
                  pl.BlockSpec(memory_space=pl.ANY)],
        out_specs=(pl.BlockSpec(memory_space=pl.ANY), pl.BlockSpec(memory_space=pl.ANY)),
        scratch_shapes=[pltpu.VMEM((2, R_LOC, D), BF16),
                        pltpu.VMEM((2, R_LOC, ROUTE_W), F32),
                        pltpu.SemaphoreType.DMA((2, 2))])
    return pl.pallas_call(
        _dispatch_kernel,
        out_shape=(jax.ShapeDtypeStruct((R_MAX, D), BF16),
                   jax.ShapeDtypeStruct((R_MAX, ROUTE_W), F32)),
        grid_spec=grid_spec,
        input_output_aliases={5: 0, 6: 1},
        compiler_params=_cparams(("arbitrary",)),
        name="dispatch",
    )(chunk_off, pc, h2, info, cw8, xs0, cws0)


def _expert_kernel(tg_ref, tv_ref, x_ref, cw_ref, wg_ref, wu_ref, wd_ref, y_ref):
    del tg_ref
    k = pl.program_id(0)

    @pl.when(tv_ref[k] == 0)
    def _():
        y_ref[...] = jnp.zeros_like(y_ref)

    @pl.when(tv_ref[k] != 0)
    def _():
        x = x_ref[...]
        cw = cw_ref[...]
        lane = lax.broadcasted_iota(jnp.int32, cw.shape, 1)
        acc = jnp.zeros(y_ref.shape, F32)
        for j in range(EPG):
            mine = jnp.logical_and((lane & (EPG - 1)) == j, lane < 3 * EPG)
            w = jnp.sum(jnp.where(mine, cw, 0.0), axis=-1, keepdims=True)
            a = (jax.nn.silu(jnp.dot(x, wg_ref[j], preferred_element_type=F32))
                 * jnp.dot(x, wu_ref[j], preferred_element_type=F32))
            acc = acc + jnp.dot((a * w).astype(BF16), wd_ref[j], preferred_element_type=F32)
        y_ref[...] = acc.astype(BF16)


def _experts(tile_group, tile_valid, xs, cws, wg, wu, wd):
    grid_spec = pltpu.PrefetchScalarGridSpec(
        num_scalar_prefetch=2,
        grid=(EX_NT,),
        in_specs=[pl.BlockSpec((EX_TM, D), lambda k, tg, tv: (k, 0)),
                  pl.BlockSpec((EX_TM, ROUTE_W), lambda k, tg, tv: (k, 0)),
                  pl.BlockSpec((EPG, D, FF), lambda k, tg, tv: (tg[k], 0, 0)),
                  pl.BlockSpec((EPG, D, FF), lambda k, tg, tv: (tg[k], 0, 0)),
                  pl.BlockSpec((EPG, FF, D), lambda k, tg, tv: (tg[k], 0, 0))],
        out_specs=pl.BlockSpec((EX_TM, D), lambda k, tg, tv: (k, 0)))
    return pl.pallas_call(
        _expert_kernel,
        out_shape=jax.ShapeDtypeStruct((R_MAX, D), BF16),
        grid_spec=grid_spec,
        compiler_params=_cparams(("arbitrary",)),
        name="experts",
    )(tile_group, tile_valid, xs, cws, wg, wu, wd)


CB_NPT = N_PROMPT // RT_TM


def _combine_kernel(off_ref, pc_ref, info_ref, x1_ref, mod_ref, g_ref, ys_hbm,
                    yp_ref, ysm_ref, stage, sem):
    i = pl.program_id(0)
    slot = i % 2

    def copy(s, dst, src):
        return pltpu.make_async_copy(ys_hbm.at[pl.ds(src, PIECE)], stage.at[s, pl.ds(dst, PIECE)],
                                     sem.at[s])

    def fetch(tile, s):
        _piece_copies(off_ref, pc_ref, tile, lambda loc, glob: copy(s, loc, glob).start())

    @pl.when(i == 0)
    def _():
        stage[...] = jnp.zeros_like(stage)
        fetch(0, 0)

    @pl.when(i + 1 < RT_NT)
    def _():
        fetch(i + 1, 1 - slot)

    def wait(j, carry):
        copy(slot, 0, 0).wait()
        return carry

    lax.fori_loop(0, _piece_count(pc_ref, i), wait, 0)

    dest = info_ref[...][:, 1:2]
    col = lax.broadcasted_iota(jnp.int32, (RT_TM, R_STAGE), 1).astype(F32)
    moe = jnp.dot((col == dest).astype(BF16), stage[slot], preferred_element_type=F32)
    mod = mod_ref[...]
    out = _rms(x1_ref[...] + mod[:, 5 * D:6 * D] * moe) * g_ref[...]

    @pl.when(i < CB_NPT)
    def _():
        yp_ref[...] = out

    @pl.when(i >= CB_NPT)
    def _():
        ysm_ref[...] = out


def _combine(chunk_off, pc, info, x1, mod3, fg, ys):
    npt = CB_NPT
    grid_spec = pltpu.PrefetchScalarGridSpec(
        num_scalar_prefetch=2,
        grid=(RT_NT,),
        in_specs=[pl.BlockSpec((RT_TM, ROUTE_W), lambda i, o, p: (i, 0)),
                  pl.BlockSpec((RT_TM, D), lambda i, o, p: (i, 0)),
                  pl.BlockSpec((None, 1, 6 * D), lambda i, o, p: (_mod_row(RT_TM)(i), 0, 0)),
                  pl.BlockSpec((1, D), lambda i, o, p: (0, 0)),
                  pl.BlockSpec(memory_space=pl.ANY)],
        out_specs=(pl.BlockSpec((RT_TM, D), lambda i, o, p: (jnp.minimum(i, npt - 1), 0)),
                   pl.BlockSpec((RT_TM, D), lambda i, o, p: (jnp.maximum(i - npt, 0), 0))),
        scratch_shapes=[pltpu.VMEM((2, R_STAGE, D), BF16),
                        pltpu.SemaphoreType.DMA((2,))])
    return pl.pallas_call(
        _combine_kernel,
        out_shape=(jax.ShapeDtypeStruct((N_PROMPT, D), F32),
                   jax.ShapeDtypeStruct((N_SAMPLE, D), F32)),
        grid_spec=grid_spec,
        compiler_params=_cparams(("arbitrary",)),
        name="combine",
    )(chunk_off, pc, info, x1, mod3, fg, ys)


def kernel(x_prompt, x_sample, cache_diff_k, cache_diff_v, state_ret_fwd, state_ret_bwd, c, c_ctx,
           w_ada, b_ada, norm1_g, norm2_g, w_in, ret_decay_fwd, ret_decay_bwd, ret_norm_g,
           diff_lambda_q1, diff_lambda_k1, diff_lambda_q2, diff_lambda_k2, diff_subln_g,
           w_ret_o, w_diff_o, w_o, router_group_w, router_group_b, router_expert_w, router_expert_b,
           moe_w_gate, moe_w_up, moe_w_down, final_norm_g):
    l = 0
    lam_init = 0.8 - 0.6 * math.exp(-0.3 * l)
    lam = (jnp.exp(jnp.sum(diff_lambda_q1[l].astype(F32) * diff_lambda_k1[l].astype(F32)))
           - jnp.exp(jnp.sum(diff_lambda_q2[l].astype(F32) * diff_lambda_k2[l].astype(F32))) + lam_init)
    attn_scal = jnp.stack([lam, jnp.float32(1.0 - lam_init)]).astype(F32)
    lg = jnp.stack([jax.nn.log_sigmoid(ret_decay_fwd[l].astype(F32)),
                    jax.nn.log_sigmoid(ret_decay_bwd[l].astype(F32))])

    xp = x_prompt.reshape(N_PROMPT, D)
    xs = x_sample.reshape(N_SAMPLE, D)
    cvec = jnp.concatenate([c_ctx[None, :], c, jnp.zeros((3, D), F32)], axis=0)
    mod3 = _modulation(cvec, w_ada[l], b_ada[l][None, :]).reshape(8, 1, 6 * D)

    proj, kt32, v32 = _inproj(xp, xs, mod3, norm1_g[l][None, :], w_in[l])

    cos_t, sin_t = _rope_tables()
    cache_kt = jnp.transpose(cache_diff_k[:, l], (0, 2, 3, 4, 1)).reshape(4, D, PAST)
    (diff_act, ret_act, s_f, s_b, wg_bf, wu_bf, wd_bf, xs0, cws0) = _mixers(
        attn_scal, lg, proj, cache_kt, cache_diff_v, cos_t, sin_t, diff_subln_g[l][None, :],
        state_ret_fwd[:, l], state_ret_bwd[:, l], ret_norm_g[l][None, :],
        (moe_w_gate[l], moe_w_up[l], moe_w_down[l]))

    br = jnp.concatenate([router_group_b[l], router_expert_b[l],
                          jnp.zeros((ROUTE_W - N_GROUPS - N_EXP,), F32)])[None, :]
    x1, h2, cw8, info, pc_arr = _outproj(ret_act, diff_act, proj, xp, xs, mod3, norm2_g[l][None, :],
                                         w_ret_o[l], w_diff_o[l], w_o[l],
                                         router_group_w[l], router_expert_w[l], br)
    chunk_off, pc, tile_group, tile_valid = _dispatch_plan(pc_arr)
    xs_sorted, cw_sorted = _dispatch(chunk_off, pc, h2, info, cw8, xs0, cws0)
    y_sorted = _experts(tile_group, tile_valid, xs_sorted, cw_sorted, wg_bf, wu_bf, wd_bf)
    yp, ys = _combine(chunk_off, pc, info, x1, mod3, final_norm_g[None, :], y_sorted)

    return (yp.reshape(16, T_P, D), ys.reshape(4, T_S, D),
            jnp.transpose(kt32.reshape(16, DIFF_H, 2, DIFF_HD, T_P), (0, 4, 1, 2, 3))[:, None],
            v32.reshape(16, 1, T_P, DIFF_H, 2 * DIFF_HD),
            s_f.reshape(16, 1, RET_H, RET_KD, RET_VD), s_b.reshape(16, 1, RET_H, RET_KD, RET_VD))
```

```python
import functools
import math

import jax
import jax.numpy as jnp
import numpy as np
from jax import lax
from jax.experimental import pallas as pl
from jax.experimental.pallas import tpu as pltpu

F32 = jnp.float32
BF16 = jnp.bfloat16

D = 1024
N_PROMPT = 16 * 256
N_SAMPLE = 4 * 1024
N_TOK = N_PROMPT + N_SAMPLE
T_P = 256
T_S = 1024
PAST = 512
GRID_W = 64
RET_H = 4
RET_KD = 128
RET_VD = 256
DIFF_H = 8
DIFF_HD = 64
ROPE_BASE = 10000.0
N_GROUPS = 4
EPG = 8
N_EXP = 32
FF = 256
EPS = 1e-6
IN_W = 8192
LANE = 128
N_CHUNK = IN_W // LANE
C_RQ, C_RK, C_RV, C_RG, C_DQ, C_DK, C_DV, C_GR, C_GD = 0, 4, 8, 16, 24, 32, 40, 48, 56
ROUTE_W = 128
SEQ_BLK = 1024
N_PBLK = N_PROMPT // SEQ_BLK
VMEM_LIMIT = 56 * 1024 * 1024


def _cparams(sem):
    return pltpu.CompilerParams(dimension_semantics=sem, vmem_limit_bytes=VMEM_LIMIT)


def _mod_row(tile_rows):
    def f(i):
        start = i * tile_rows
        return jnp.where(start < N_PROMPT, 0, 1 + (start - N_PROMPT) // T_S)
    return f


def _rms(x):
    return x * lax.rsqrt(jnp.mean(x * x, axis=-1, keepdims=True) + EPS)


def _mod_kernel(c_ref, w_ref, b_ref, o_ref):
    s = jax.nn.silu(c_ref[...])
    o_ref[...] = jnp.dot(s.astype(BF16), w_ref[...].astype(BF16),
                         preferred_element_type=F32) + b_ref[...]


def _modulation(cvec, w_ada, b_ada):
    tn = 1536
    return pl.pallas_call(
        _mod_kernel,
        out_shape=jax.ShapeDtypeStruct((8, 6 * D), F32),
        grid=(6 * D // tn,),
        in_specs=[pl.BlockSpec((8, D), lambda j: (0, 0)),
                  pl.BlockSpec((D, tn), lambda j: (0, j)),
                  pl.BlockSpec((1, tn), lambda j: (0, j))],
        out_specs=pl.BlockSpec((8, tn), lambda j: (0, j)),
        compiler_params=_cparams(("arbitrary",)),
        name="mod",
    )(cvec, w_ada, b_ada)


IP_TM = 512
IP_TN = 2048
IP_NPT = N_PROMPT // IP_TM


IP_KV_TILE = C_DK * LANE // IP_TN
IP_SPT = IP_TM // T_P


def _inproj_kernel(xp_ref, xs_ref, mod_ref, n1_ref, w_ref, proj_ref, kt_ref, v32_ref, w_scr):
    j = pl.program_id(0)
    i = pl.program_id(1)

    @pl.when(i == 0)
    def _():
        w_scr[...] = w_ref[...].astype(BF16)

    x = jnp.where(i < IP_NPT, xp_ref[...], xs_ref[...])
    mod = mod_ref[...]
    h = (_rms(x) * n1_ref[...] * (1.0 + mod[:, D:2 * D]) + mod[:, 0:D]).astype(BF16)
    acc = jnp.dot(h, w_scr[...], preferred_element_type=F32)
    for c in range(IP_TN // LANE):
        proj_ref[c] = acc[:, c * LANE:(c + 1) * LANE].astype(BF16)

    @pl.when(jnp.logical_and(j == IP_KV_TILE, i < IP_NPT))
    def _():
        for s in range(IP_SPT):
            kt_ref[s] = acc[s * T_P:(s + 1) * T_P, :D].T
        v32_ref[...] = acc[:, D:]


def _inproj(xp, xs, mod3, n1, w_in):
    npt = IP_NPT
    cpt = IP_TN // LANE

    def kv_tile(j, i):
        return jnp.where(j < IP_KV_TILE, 0, jnp.where(j == IP_KV_TILE, jnp.minimum(i, npt - 1), npt - 1))

    return pl.pallas_call(
        _inproj_kernel,
        out_shape=(jax.ShapeDtypeStruct((N_CHUNK, N_TOK, LANE), BF16),
                   jax.ShapeDtypeStruct((N_PROMPT // T_P, D, T_P), F32),
                   jax.ShapeDtypeStruct((N_PROMPT, D), F32)),
        grid=(IN_W // IP_TN, N_TOK // IP_TM),
        in_specs=[pl.BlockSpec((IP_TM, D), lambda j, i: (jnp.minimum(i, npt - 1), 0)),
                  pl.BlockSpec((IP_TM, D), lambda j, i: (jnp.maximum(i - npt, 0), 0)),
                  pl.BlockSpec((None, 1, 6 * D), lambda j, i: (_mod_row(IP_TM)(i), 0, 0)),
                  pl.BlockSpec((1, D), lambda j, i: (0, 0)),
                  pl.BlockSpec((D, IP_TN), lambda j, i: (0, j))],
        out_specs=(pl.BlockSpec((cpt, IP_TM, LANE), lambda j, i: (j, i, 0)),
                   pl.BlockSpec((IP_SPT, D, T_P), lambda j, i: (kv_tile(j, i), 0, 0)),
                   pl.BlockSpec((IP_TM, D), lambda j, i: (kv_tile(j, i), 0))),
        scratch_shapes=[pltpu.VMEM((D, IP_TN), BF16)],
        compiler_params=_cparams(("arbitrary", "arbitrary")),
        name="inproj",
    )(xp, xs, mod3, n1, w_in)


def _decay_mask(t, lgf, lgb):
    ii = lax.broadcasted_iota(jnp.int32, (t, t), 0)
    jj = lax.broadcasted_iota(jnp.int32, (t, t), 1)
    rel = (ii - jj).astype(F32)
    e = jnp.exp(jnp.where(rel >= 0.0, lgf, -lgb) * rel)
    return jnp.where(rel == 0.0, 2.0, e) * (RET_KD ** -0.5)


def _ret_parts(lgf, lgb, q_ref, k_ref, v_ref, rg_ref, s0f_ref, s0b_ref, g_ref,
               o_ref, sf_ref, sb_ref, dm_ref):
    gain = g_ref[...]
    nt = (((1,), (1,)), ((), ()))
    tn = (((0,), (0,)), ((), ()))

    def finish(o, rg):
        d = o - jnp.mean(o, axis=-1, keepdims=True)
        y = d * lax.rsqrt(jnp.mean(d * d, axis=-1, keepdims=True) + EPS) * gain
        return (jax.nn.silu(rg.astype(F32)) * y).astype(BF16)

    def vcat(r):
        return jnp.concatenate([v_ref[0, r, :], v_ref[1, r, :]], axis=1)

    def gcat(r):
        return jnp.concatenate([rg_ref[0, r, :], rg_ref[1, r, :]], axis=1)

    def build_mask():
        dm_ref[...] = _decay_mask(T_P, lgf, lgb)

    n_chunk = SEQ_BLK // T_P
    chunks = [slice(c * T_P, (c + 1) * T_P) for c in range(n_chunk)]
    t = lax.broadcasted_iota(jnp.int32, (T_P, 1), 0).astype(F32)
    kdf = jnp.exp(lgf * (T_P - 1.0 - t)) * (RET_KD ** -0.5)
    kdb = jnp.exp(lgb * t) * (RET_KD ** -0.5)

    def intra(r):
        sc = lax.dot_general(q_ref[r, :], k_ref[r, :], nt, preferred_element_type=F32)
        return jnp.dot((sc * dm_ref[...]).astype(BF16), vcat(r), preferred_element_type=F32)

    def key_state(r, kd):
        kw = (k_ref[r, :].astype(F32) * kd).astype(BF16)
        return lax.dot_general(kw, vcat(r), tn, preferred_element_type=F32)

    def context_part():
        for s, r in enumerate(chunks):
            o_ref[r, :] = finish(intra(r), gcat(r))
            sf_ref[s] = key_state(r, kdf)
            sb_ref[s] = key_state(r, kdb)

    def latent_part():
        qdf = jnp.exp(lgf * (t + 1.0))
        qdb = jnp.exp(lgb * (T_P - t))
        span = jnp.full((1, 1), float(T_P), F32)
        cf = jnp.exp(lgf * span)
        cb = jnp.exp(lgb * span)
        before_b = [None] * n_chunk
        state = s0b_ref[...]
        for c in reversed(range(n_chunk)):
            before_b[c] = state
            if c > 0:
                state = cb * state + key_state(chunks[c], kdb)
        state = s0f_ref[...]
        for c, r in enumerate(chunks):
            qf = q_ref[r, :].astype(F32)
            o = (intra(r)
                 + jnp.dot((qf * qdf).astype(BF16), state.astype(BF16), preferred_element_type=F32)
                 + jnp.dot((qf * qdb).astype(BF16), before_b[c].astype(BF16), preferred_element_type=F32))
            o_ref[r, :] = finish(o, gcat(r))
            if c + 1 < n_chunk:
                state = cf * state + key_state(r, kdf)

    return build_mask, context_part, latent_part


ATT_TQ = 1024
ATT_HPS = 2
ATT_QSCALE = (DIFF_HD ** -0.5) * math.log2(math.e)


N_SIDE = 3


def _mixer_kernel(sc_ref, lg_ref, q_ref, k_ref, v_ref, ckt_ref, cv_ref, cos_ref, sin_ref, g_ref,
                  rq_ref, rk_ref, rv_ref, rg_ref, s0f_ref, s0b_ref, gn_ref, *rest):
    side_in = rest[:N_SIDE]
    o_ref, ro_ref, sf_ref, sb_ref = rest[N_SIDE:N_SIDE + 4]
    side_out = rest[N_SIDE + 4:2 * N_SIDE + 4]
    xs0_ref, cws0_ref, q_scr, k_scr, v_scr, dm_scr = rest[2 * N_SIDE + 4:]
    i = pl.program_id(0)
    rh = pl.program_id(1)
    build_mask, ret_context, ret_latent = _ret_parts(
        lg_ref[0, rh], lg_ref[1, rh], rq_ref, rk_ref, rv_ref, rg_ref, s0f_ref, s0b_ref, gn_ref,
        ro_ref, sf_ref, sb_ref, dm_scr.at[rh])

    @pl.when(i == 0)
    def _():
        build_mask()

    def side_jobs():
        for src, dst in zip(side_in, side_out):
            dst[...] = src[...].astype(BF16)
        xs0_ref[...] = jnp.zeros_like(xs0_ref)
        cws0_ref[...] = jnp.zeros_like(cws0_ref)

    lam = sc_ref[0]
    out_scale = sc_ref[1]
    gain = g_ref[...] * out_scale
    lane = lax.broadcasted_iota(jnp.int32, (1, LANE), 1)
    first = lane < DIFF_HD
    nt = (((1,), (1,)), ((), ()))

    def halves(q):
        zero = jnp.zeros_like(q)
        return jnp.where(first, q, zero), jnp.where(first, zero, q)

    def weights(s):
        return jnp.exp2(s - jnp.max(s, axis=-1, keepdims=True)).astype(BF16)

    def finish(of0, of1):
        o = of0[:, :LANE] / of0[:, LANE:] - lam * (of1[:, :LANE] / of1[:, LANE:])
        return (_rms(o) * gain).astype(BF16)

    @pl.when(i < N_PBLK)
    def _():
        ret_context()
        n_seq = SEQ_BLK // T_P
        rows = [slice(s * T_P, (s + 1) * T_P) for s in range(n_seq)]
        ones = jnp.ones((T_P, LANE), BF16)
        for hh in range(ATT_HPS):
            cols = slice(hh * LANE, (hh + 1) * LANE)
            q0, q1 = halves((q_ref[hh].astype(F32) * ATT_QSCALE).astype(BF16))
            s0 = jnp.concatenate([lax.dot_general(q0[r], k_ref[hh, r, :], nt, preferred_element_type=F32)
                                  for r in rows], axis=0)
            s1 = jnp.concatenate([lax.dot_general(q1[r], k_ref[hh, r, :], nt, preferred_element_type=F32)
                                  for r in rows], axis=0)
            e0 = weights(s0)
            e1 = weights(s1)
            for r in rows:
                v1 = jnp.concatenate([v_ref[hh, r, :], ones], axis=1)
                o_ref[r, cols] = finish(jnp.dot(e0[r], v1, preferred_element_type=F32),
                                        jnp.dot(e1[r], v1, preferred_element_type=F32))

    @pl.when(i >= N_PBLK)
    def _():
        side_jobs()
        ret_latent()
        cos = cos_ref[...]
        sin = sin_ref[...]
        low = (lax.broadcasted_iota(jnp.int32, (T_S, LANE), 1) & 16) == 0

        def rope(x):
            xs = jnp.where(low, pltpu.roll(x, LANE - 16, 1), pltpu.roll(x, 16, 1))
            return x * cos + xs * sin

        for hh in range(ATT_HPS):
            cols = slice(hh * LANE, (hh + 1) * LANE)
            head = pl.program_id(1) * ATT_HPS + hh
            q_scr[hh] = (rope(q_ref[hh].astype(F32)) * ATT_QSCALE).astype(BF16)
            k_scr[hh] = rope(k_ref[hh].astype(F32)).astype(BF16)
            ckt = ckt_ref[cols, :].astype(BF16)
            v_scr[hh, 0:T_S, 0:LANE] = v_ref[hh]
            v_scr[hh, T_S:T_S + PAST, 0:LANE] = cv_ref[:, head, :].astype(BF16)
            v_scr[hh, :, LANE:2 * LANE] = jnp.ones((T_S + PAST, LANE), BF16)

            def scores(qh, hh=hh, ckt=ckt):
                return jnp.concatenate([lax.dot_general(qh, k_scr[hh], nt, preferred_element_type=F32),
                                        jnp.dot(qh, ckt, preferred_element_type=F32)], axis=1)

            for b in range(T_S // ATT_TQ):
                r = slice(b * ATT_TQ, (b + 1) * ATT_TQ)
                q0, q1 = halves(q_scr[hh, r, :])
                e0 = weights(scores(q0))
                e1 = weights(scores(q1))
                o_ref[r, cols] = finish(jnp.dot(e0, v_scr[hh], preferred_element_type=F32),
                                        jnp.dot(e1, v_scr[hh], preferred_element_type=F32))


def _mixers(scal, lg, proj, cache_k, cache_v, cos_t, sin_t, subln_g, s0f, s0b, gnorm, side_weights):
    nb = N_TOK // SEQ_BLK
    spb = SEQ_BLK // T_P
    smp = lambda i: jnp.maximum(i - N_PBLK, 0)
    pmt = lambda i: jnp.minimum(i, N_PBLK - 1)
    pmh = lambda i, h: jnp.where(i < N_PBLK, h, RET_H - 1)
    state_in = pl.BlockSpec((None, None, RET_KD, RET_VD), lambda i, h: (smp(i), h, 0, 0))
    state_out = pl.BlockSpec((spb, None, RET_KD, RET_VD), lambda i, h: (pmt(i), pmh(i, h), 0, 0))

    n_hp = DIFF_H // ATT_HPS
    assert n_hp == RET_H
    n_steps = (nb - N_PBLK) * n_hp
    step = lambda i, h: jnp.maximum((i - N_PBLK) * n_hp + h, 0)
    exp_per_step = N_EXP // n_steps
    exp_slice = lambda i, h: (step(i, h), 0, 0)
    row_slice = lambda i, h: (step(i, h), 0)
    up_spec = pl.BlockSpec((exp_per_step, D, FF), exp_slice)
    down_spec = pl.BlockSpec((exp_per_step, FF, D), exp_slice)
    side_specs = [up_spec, up_spec, down_spec]
    side_shapes = [jax.ShapeDtypeStruct(a.shape, BF16) for a in side_weights]
    zrows = R_MAX // n_steps

    return pl.pallas_call(
        _mixer_kernel,
        out_shape=(jax.ShapeDtypeStruct((N_TOK, DIFF_H * 2 * DIFF_HD), BF16),
                   jax.ShapeDtypeStruct((N_TOK, RET_H * RET_VD), BF16),
                   jax.ShapeDtypeStruct((16, RET_H, RET_KD, RET_VD), F32),
                   jax.ShapeDtypeStruct((16, RET_H, RET_KD, RET_VD), F32),
                   *side_shapes,
                   jax.ShapeDtypeStruct((R_MAX, D), BF16),
                   jax.ShapeDtypeStruct((R_MAX, ROUTE_W), F32)),
        grid=(nb, n_hp),
        in_specs=[pl.BlockSpec(memory_space=pltpu.SMEM),
                  pl.BlockSpec(memory_space=pltpu.SMEM),
                  pl.BlockSpec((ATT_HPS, SEQ_BLK, LANE), lambda i, h: (C_DQ // ATT_HPS + h, i, 0)),
                  pl.BlockSpec((ATT_HPS, SEQ_BLK, LANE), lambda i, h: (C_DK // ATT_HPS + h, i, 0)),
                  pl.BlockSpec((ATT_HPS, SEQ_BLK, LANE), lambda i, h: (C_DV // ATT_HPS + h, i, 0)),
                  pl.BlockSpec((None, ATT_HPS * LANE, PAST), lambda i, h: (smp(i), h, 0)),
                  pl.BlockSpec((None, None, PAST, DIFF_H, LANE), lambda i, h: (smp(i), 0, 0, 0, 0)),
                  pl.BlockSpec((T_S, LANE), lambda i, h: (0, 0)),
                  pl.BlockSpec((T_S, LANE), lambda i, h: (0, 0)),
                  pl.BlockSpec((1, LANE), lambda i, h: (0, 0)),
                  pl.BlockSpec((None, SEQ_BLK, LANE), lambda i, h: (C_RQ + h, i, 0)),
                  pl.BlockSpec((None, SEQ_BLK, LANE), lambda i, h: (C_RK + h, i, 0)),
                  pl.BlockSpec((2, SEQ_BLK, LANE), lambda i, h: (C_RV // 2 + h, i, 0)),
                  pl.BlockSpec((2, SEQ_BLK, LANE), lambda i, h: (C_RG // 2 + h, i, 0)),
                  state_in, state_in,
                  pl.BlockSpec((1, RET_VD), lambda i, h: (0, h)),
                  *side_specs],
        out_specs=(pl.BlockSpec((SEQ_BLK, ATT_HPS * LANE), lambda i, h: (i, h)),
                   pl.BlockSpec((SEQ_BLK, RET_VD), lambda i, h: (i, h)),
                   state_out, state_out,
                   *side_specs,
                   pl.BlockSpec((zrows, D), row_slice),
                   pl.BlockSpec((zrows, ROUTE_W), row_slice)),
        scratch_shapes=[pltpu.VMEM((ATT_HPS, T_S, LANE), BF16),
                        pltpu.VMEM((ATT_HPS, T_S, LANE), BF16),
                        pltpu.VMEM((ATT_HPS, T_S + PAST, 2 * LANE), BF16),
                        pltpu.VMEM((RET_H, T_P, T_P), F32)],
        compiler_params=_cparams(("arbitrary", "arbitrary")),
        name="mixers",
    )(scal, lg, proj, proj, proj, cache_k, cache_v, cos_t, sin_t, subln_g,
      proj, proj, proj, proj, s0f, s0b, gnorm, *side_weights)


def _rope_tables():
    n_rows = T_S // GRID_W
    row = np.repeat(np.arange(n_rows), GRID_W).astype(np.float64)
    col = np.tile(np.arange(GRID_W), n_rows).astype(np.float64)
    n_freq = DIFF_HD // 4
    inv = ROPE_BASE ** (-np.arange(n_freq, dtype=np.float64) / n_freq)

    def axis_tables(pos):
        ang = pos[:, None] * inv[None, :]
        c = np.cos(ang)
        s = np.sin(ang)
        return np.concatenate([c, c], axis=-1), np.concatenate([-s, s], axis=-1)

    cr, sr = axis_tables(row)
    cc, sc = axis_tables(col)
    cos_h = np.concatenate([cr, cc], axis=-1)
    sin_h = np.concatenate([sr, sc], axis=-1)
    return (jnp.asarray(np.concatenate([cos_h, cos_h], axis=-1), F32),
            jnp.asarray(np.concatenate([sin_h, sin_h], axis=-1), F32))


OP_TM = 512
OP_NPT = N_PROMPT // OP_TM


def _outproj_kernel(ra_ref, da_ref, gr_ref, gd_ref, xp_ref, xs_ref, mod_ref, n2_ref,
                    wro32_ref, wdo32_ref, wo32_ref, rgw_ref, rew_ref, brt_ref,
                    x1_ref, h2_ref, cw8_ref, info_ref, pc_ref, infot_ref,
                    m_scr, wro_ref, wdo_ref, wo_ref, wrt_ref):
    i = pl.program_id(0)

    @pl.when(i == 0)
    def _():
        wro_ref[...] = wro32_ref[...].astype(BF16)
        wdo_ref[...] = wdo32_ref[...].astype(BF16)
        wo_ref[...] = wo32_ref[...].astype(BF16)
        gap = jnp.zeros((D, ROUTE_E0 - N_GROUPS), F32)
        pad = jnp.zeros((D, ROUTE_W - ROUTE_E0 - N_EXP), F32)
        wr = jnp.concatenate([rgw_ref[...], gap, rew_ref[...], pad], axis=1)
        wrt_ref[...] = wr.T.astype(BF16)

    ret_out = jnp.dot(ra_ref[...], wro_ref[...], preferred_element_type=F32)
    diff_out = jnp.dot(da_ref[...], wdo_ref[...], preferred_element_type=F32)
    for c in range(D // LANE):
        sl = slice(c * LANE, (c + 1) * LANE)
        m = (jax.nn.sigmoid(gr_ref[c].astype(F32)) * ret_out[:, sl]
             + jax.nn.sigmoid(gd_ref[c].astype(F32)) * diff_out[:, sl])
        m_scr[:, sl] = m.astype(BF16)
    mix = jnp.dot(m_scr[...], wo_ref[...], preferred_element_type=F32)
    mod = mod_ref[...]

    x1 = jnp.where(i < OP_NPT, xp_ref[...], xs_ref[...]) + mod[:, 2 * D:3 * D] * mix
    x1_ref[...] = x1
    h2 = (_rms(x1) * n2_ref[...] * (1.0 + mod[:, 4 * D:5 * D]) + mod[:, 3 * D:4 * D]).astype(BF16)
    h2_ref[...] = h2
    logits_t = lax.dot_general(wrt_ref[...], h2, (((1,), (1,)), ((), ())),
                               preferred_element_type=F32) + brt_ref[...]
    cw8_ref[...], info_ref[...], pc_ref[...], infot_ref[...] = _route_cols(logits_t)


def _outproj(ret_act, diff_act, proj, xp, xs, mod3, n2, wro, wdo, wo, rgw, rew, brt):
    assert OP_TM == RT_TM
    npt = OP_NPT
    full = lambda i: (0, 0)
    once = pl.Buffered(1)
    return pl.pallas_call(
        _outproj_kernel,
        out_shape=(jax.ShapeDtypeStruct((N_TOK, D), F32),
                   jax.ShapeDtypeStruct((N_TOK, D), BF16),
                   jax.ShapeDtypeStruct((N_TOK, ROUTE_W), F32),
                   jax.ShapeDtypeStruct((N_TOK, ROUTE_W), F32),
                   jax.ShapeDtypeStruct((RT_NT, 1, ROUTE_W), F32),
                   jax.ShapeDtypeStruct((RT_NT, EPG, RT_TM), F32)),
        grid=(N_TOK // OP_TM,),
        in_specs=[pl.BlockSpec((OP_TM, D), lambda i: (i, 0)),
                  pl.BlockSpec((OP_TM, D), lambda i: (i, 0)),
                  pl.BlockSpec((8, OP_TM, LANE), lambda i: (C_GR // 8, i, 0)),
                  pl.BlockSpec((8, OP_TM, LANE), lambda i: (C_GD // 8, i, 0)),
                  pl.BlockSpec((OP_TM, D), lambda i: (jnp.minimum(i, npt - 1), 0)),
                  pl.BlockSpec((OP_TM, D), lambda i: (jnp.maximum(i - npt, 0), 0)),
                  pl.BlockSpec((None, 1, 6 * D), lambda i: (_mod_row(OP_TM)(i), 0, 0)),
                  pl.BlockSpec((1, D), full),
                  pl.BlockSpec((D, D), full, pipeline_mode=once),
                  pl.BlockSpec((D, D), full, pipeline_mode=once),
                  pl.BlockSpec((D, D), full, pipeline_mode=once),
                  pl.BlockSpec((D, N_GROUPS), full, pipeline_mode=once),
                  pl.BlockSpec((D, N_EXP), full, pipeline_mode=once),
                  pl.BlockSpec((ROUTE_W, 1), full)],
        out_specs=(pl.BlockSpec((OP_TM, D), lambda i: (i, 0)),
                   pl.BlockSpec((OP_TM, D), lambda i: (i, 0)),
                   pl.BlockSpec((OP_TM, ROUTE_W), lambda i: (i, 0)),
                   pl.BlockSpec((OP_TM, ROUTE_W), lambda i: (i, 0)),
                   pl.BlockSpec((None, 1, ROUTE_W), lambda i: (i, 0, 0)),
                   pl.BlockSpec((None, EPG, RT_TM), lambda i: (i, 0, 0))),
        scratch_shapes=[pltpu.VMEM((OP_TM, D), BF16),
                        pltpu.VMEM((D, D), BF16), pltpu.VMEM((D, D), BF16), pltpu.VMEM((D, D), BF16),
                        pltpu.VMEM((ROUTE_W, D), BF16)],
        compiler_params=_cparams(("arbitrary",)),
        name="outproj",
    )(ret_act, diff_act, proj, proj, xp, xs, mod3, n2, wro, wdo, wo, rgw, rew, brt)


RT_TM = 512
RT_NT = N_TOK // RT_TM
PIECE = 16
R_LOC = RT_TM + N_GROUPS * PIECE
R_STAGE = 640
EX_TM = 512
R_MAX = 11264
EX_NT = R_MAX // EX_TM


ROUTE_E0 = EPG


def _route_cols(lgt):
    row = lax.broadcasted_iota(jnp.int32, (EPG, RT_TM), 0)
    neg = jnp.float32(-jnp.inf)

    def first_row(cond):
        return jnp.min(jnp.where(cond, row, EPG), axis=0, keepdims=True)

    head = lgt[0:EPG]
    is_g = row < N_GROUPS
    gl = jnp.where(is_g, head, neg)
    gmax = jnp.max(gl, axis=0, keepdims=True)
    gsum = jnp.sum(jnp.where(is_g, jnp.exp(head - gmax), 0.0), axis=0, keepdims=True)
    p_top = 1.0 / gsum
    g_idx = first_row(gl == gmax)
    el = lgt[ROUTE_E0 + (N_GROUPS - 1) * EPG:ROUTE_E0 + N_GROUPS * EPG]
    for g in reversed(range(N_GROUPS - 1)):
        el = jnp.where(g_idx == g, lgt[ROUTE_E0 + g * EPG:ROUTE_E0 + (g + 1) * EPG], el)
    ee = jnp.exp(el - jnp.max(el, axis=0, keepdims=True))
    ep = ee / jnp.sum(ee, axis=0, keepdims=True)
    e1 = jnp.max(ep, axis=0, keepdims=True)
    i1 = first_row(ep == e1)
    ep2 = jnp.where(row == i1, -1.0, ep)
    e2 = jnp.max(ep2, axis=0, keepdims=True)
    i2 = first_row(ep2 == e2)
    den = e1 + e2
    cw8_t = (jnp.where(row == i1, p_top * e1 / den, 0.0)
             + jnp.where(row == i2, p_top * e2 / den, 0.0))

    onehot = (row == g_idx).astype(F32)
    ii = lax.broadcasted_iota(jnp.int32, (RT_TM, RT_TM), 0)
    jj = lax.broadcasted_iota(jnp.int32, (RT_TM, RT_TM), 1)
    earlier = (ii < jj).astype(BF16)
    prefix = jnp.dot(onehot.astype(BF16), earlier, preferred_element_type=F32)
    cnt = jnp.sum(onehot, axis=1, keepdims=True)
    pc_col = jnp.floor((cnt + (PIECE - 1.0)) * (1.0 / PIECE)) * PIECE
    row1 = lax.broadcasted_iota(jnp.int32, (EPG, 1), 0)
    lane1 = lax.broadcasted_iota(jnp.int32, (1, ROUTE_W), 1)
    lo = jnp.zeros((EPG, 1), F32)
    run = jnp.zeros((1, 1), F32)
    pc = jnp.zeros((1, ROUTE_W), F32)
    for g in range(N_GROUPS):
        pc_g = jnp.sum(jnp.where(row1 == g, pc_col, 0.0), axis=0, keepdims=True)
        lo = jnp.where(row1 == g, run, lo)
        pc = jnp.where(lane1 == g, pc_g, pc)
        run = run + pc_g
    dest = jnp.sum(onehot * (prefix + lo), axis=0, keepdims=True)
    info_t = jnp.where(row == 0, g_idx.astype(F32), jnp.where(row == 1, dest, 0.0))

    slab = jnp.concatenate([cw8_t, info_t, jnp.zeros((LANE - 2 * EPG, RT_TM), F32)], axis=0)
    cols = slab.T
    lane = lax.broadcasted_iota(jnp.int32, cols.shape, 1)
    cw8 = jnp.where(lane < EPG, cols, 0.0)
    info = jnp.where(lane < 2, pltpu.roll(cols, LANE - EPG, 1), 0.0)
    return cw8, info, pc, info_t


def _dispatch_plan(pc_arr):
    pc = pc_arr[:, 0, :N_GROUPS].astype(jnp.int32)
    seg_len = jnp.sum(pc, axis=0)
    seg_pad = (seg_len + (EX_TM - 1)) // EX_TM * EX_TM
    seg_end_pad = jnp.cumsum(seg_pad)
    seg_start = seg_end_pad - seg_pad
    chunk_off = seg_start[None, :] + jnp.cumsum(pc, axis=0) - pc
    tile_start = jnp.arange(EX_NT, dtype=jnp.int32) * EX_TM
    tile_group = jnp.minimum(jnp.sum((tile_start[:, None] >= seg_end_pad[None, :]).astype(jnp.int32), axis=1),
                             N_GROUPS - 1)
    tile_valid = (tile_start < (seg_start + seg_len)[tile_group]).astype(jnp.int32)
    return chunk_off.astype(jnp.int32), pc, tile_group.astype(jnp.int32), tile_valid


def _piece_copies(off_ref, pc_ref, tile, make):
    lo = 0
    for g in range(N_GROUPS):
        n = pc_ref[tile, g] // PIECE
        base = off_ref[tile, g]

        def body(j, carry, lo=lo, base=base):
            make(pl.multiple_of(lo + j * PIECE, PIECE), pl.multiple_of(base + j * PIECE, PIECE))
            return carry

        lax.fori_loop(0, n, body, 0)
        lo = lo + pc_ref[tile, g]


def _piece_count(pc_ref, tile):
    n = 0
    for g in range(N_GROUPS):
        n = n + pc_ref[tile, g] // PIECE
    return n


def _dispatch_kernel(off_ref, pc_ref, h_ref, infot_ref, cw8_ref, xs_in, cws_in,
                     xs_out, cws_out, x_scr, c_scr, sem):
    del xs_in, cws_in
    i = pl.program_id(0)
    slot = i % 2
    dest = infot_ref[1:2, :]
    row = lax.broadcasted_iota(jnp.int32, (R_LOC, RT_TM), 0).astype(F32)
    sel = row == dest
    sel = sel.astype(BF16)
    x_scr[slot] = jnp.dot(sel, h_ref[...], preferred_element_type=F32).astype(BF16)
    cw = cw8_ref[...]
    hi = cw.astype(BF16).astype(F32)
    mid = (cw - hi).astype(BF16).astype(F32)
    low = (cw - hi - mid).astype(BF16).astype(F32)
    pieces = (hi + pltpu.roll(mid, EPG, 1) + pltpu.roll(low, 2 * EPG, 1)).astype(BF16)
    c_scr[slot] = jnp.dot(sel, pieces, preferred_element_type=F32)

    def x_copy(s, src, dst):
        return pltpu.make_async_copy(x_scr.at[s, pl.ds(src, PIECE)], xs_out.at[pl.ds(dst, PIECE)],
                                     sem.at[0, s])

    def c_copy(s, src, dst):
        return pltpu.make_async_copy(c_scr.at[s, pl.ds(src, PIECE)], cws_out.at[pl.ds(dst, PIECE)],
                                     sem.at[1, s])

    def start(src, dst):
        x_copy(slot, src, dst).start()
        c_copy(slot, src, dst).start()

    _piece_copies(off_ref, pc_ref, i, start)

    def wait_tile(tile, s):
        def wait(j, carry):
            x_copy(s, 0, 0).wait()
            c_copy(s, 0, 0).wait()
            return carry

        lax.fori_loop(0, _piece_count(pc_ref, tile), wait, 0)

    @pl.when(i > 0)
    def _():
        wait_tile(i - 1, 1 - slot)

    @pl.when(i == RT_NT - 1)
    def _():
        wait_tile(i, slot)


def _dispatch(chunk_off, pc, h2, info_t, cw8, xs0, cws0):
    grid_spec = pltpu.PrefetchScalarGridSpec(
        num_scalar_prefetch=2,
        grid=(RT_NT,),
        in_specs=[pl.BlockSpec((RT_TM, D), lambda i, o, p: (i, 0)),
                  pl.BlockSpec((None, EPG, RT_TM), lambda i, o, p: (i, 0, 0)),
                  pl.BlockSpec((RT_TM, ROUTE_W), lambda i, o, p: (i, 0)),
                  pl.BlockSpec(memory_space=pl.ANY),
                  pl.BlockSpec(memory_space=pl.ANY)],
        out_specs=(pl.BlockSpec(memory_space=pl.ANY), pl.BlockSpec(memory_space=pl.ANY)),
        scratch_shapes=[pltpu.VMEM((2, R_LOC, D), BF16),
                        pltpu.VMEM((2, R_LOC, ROUTE_W), F32),
                        pltpu.SemaphoreType.DMA((2, 2))])
    return pl.pallas_call(
        _dispatch_kernel,
        out_shape=(jax.ShapeDtypeStruct((R_MAX, D), BF16),
                   jax.ShapeDtypeStruct((R_MAX, ROUTE_W), F32)),
        grid_spec=grid_spec,
        input_output_aliases={5: 0, 6: 1},
        compiler_params=_cparams(("arbitrary",)),
        name="dispatch",
    )(chunk_off, pc, h2, info_t, cw8, xs0, cws0)


def _expert_kernel(tg_ref, tv_ref, x_ref, cw_ref, wg_ref, wu_ref, wd_ref, y_ref):
    del tg_ref
    k = pl.program_id(0)

    @pl.when(tv_ref[k] == 0)
    def _():
        y_ref[...] = jnp.zeros_like(y_ref)

    @pl.when(tv_ref[k] != 0)
    def _():
        x = x_ref[...]
        cw = cw_ref[...]
        lane = lax.broadcasted_iota(jnp.int32, cw.shape, 1)
        acc = jnp.zeros(y_ref.shape, F32)
        for j in range(EPG):
            mine = jnp.logical_and((lane & (EPG - 1)) == j, lane < 3 * EPG)
            w = jnp.sum(jnp.where(mine, cw, 0.0), axis=-1, keepdims=True)
            a = (jax.nn.silu(jnp.dot(x, wg_ref[j], preferred_element_type=F32))
                 * jnp.dot(x, wu_ref[j], preferred_element_type=F32))
            acc = acc + jnp.dot((a * w).astype(BF16), wd_ref[j], preferred_element_type=F32)
        y_ref[...] = acc.astype(BF16)


def _experts(tile_group, tile_valid, xs, cws, wg, wu, wd):
    grid_spec = pltpu.PrefetchScalarGridSpec(
        num_scalar_prefetch=2,
        grid=(EX_NT,),
        in_specs=[pl.BlockSpec((EX_TM, D), lambda k, tg, tv: (k, 0)),
                  pl.BlockSpec((EX_TM, ROUTE_W), lambda k, tg, tv: (k, 0)),
                  pl.BlockSpec((EPG, D, FF), lambda k, tg, tv: (tg[k], 0, 0)),
                  pl.BlockSpec((EPG, D, FF), lambda k, tg, tv: (tg[k], 0, 0)),
                  pl.BlockSpec((EPG, FF, D), lambda k, tg, tv: (tg[k], 0, 0))],
        out_specs=pl.BlockSpec((EX_TM, D), lambda k, tg, tv: (k, 0)))
    return pl.pallas_call(
        _expert_kernel,
        out_shape=jax.ShapeDtypeStruct((R_MAX, D), BF16),
        grid_spec=grid_spec,
        compiler_params=_cparams(("arbitrary",)),
        name="experts",
    )(tile_group, tile_valid, xs, cws, wg, wu, wd)


CB_NPT = N_PROMPT // RT_TM


def _combine_kernel(off_ref, pc_ref, info_ref, x1_ref, mod_ref, g_ref, ys_hbm,
                    yp_ref, ysm_ref, stage, sem):
    i = pl.program_id(0)
    slot = i % 2

    def copy(s, dst, src):
        return pltpu.make_async_copy(ys_hbm.at[pl.ds(src, PIECE)], stage.at[s, pl.ds(dst, PIECE)],
                                     sem.at[s])

    def fetch(tile, s):
        _piece_copies(off_ref, pc_ref, tile, lambda loc, glob: copy(s, loc, glob).start())

    @pl.when(i == 0)
    def _():
        stage[...] = jnp.zeros_like(stage)
        fetch(0, 0)

    @pl.when(i + 1 < RT_NT)
    def _():
        fetch(i + 1, 1 - slot)

    def wait(j, carry):
        copy(slot, 0, 0).wait()
        return carry

    lax.fori_loop(0, _piece_count(pc_ref, i), wait, 0)

    dest = info_ref[...][:, 1:2]
    col = lax.broadcasted_iota(jnp.int32, (RT_TM, R_STAGE), 1).astype(F32)
    moe = jnp.dot((col == dest).astype(BF16), stage[slot], preferred_element_type=F32)
    mod = mod_ref[...]
    out = _rms(x1_ref[...] + mod[:, 5 * D:6 * D] * moe) * g_ref[...]

    @pl.when(i < CB_NPT)
    def _():
        yp_ref[...] = out

    @pl.when(i >= CB_NPT)
    def _():
        ysm_ref[...] = out


def _combine(chunk_off, pc, info, x1, mod3, fg, ys):
    npt = CB_NPT
    grid_spec = pltpu.PrefetchScalarGridSpec(
        num_scalar_prefetch=2,
        grid=(RT_NT,),
        in_specs=[pl.BlockSpec((RT_TM, ROUTE_W), lambda i, o, p: (i, 0)),
                  pl.BlockSpec((RT_TM, D), lambda i, o, p: (i, 0)),
                  pl.BlockSpec((None, 1, 6 * D), lambda i, o, p: (_mod_row(RT_TM)(i), 0, 0)),
                  pl.BlockSpec((1, D), lambda i, o, p: (0, 0)),
                  pl.BlockSpec(memory_space=pl.ANY)],
        out_specs=(pl.BlockSpec((RT_TM, D), lambda i, o, p: (jnp.minimum(i, npt - 1), 0)),
                   pl.BlockSpec((RT_TM, D), lambda i, o, p: (jnp.maximum(i - npt, 0), 0))),
        scratch_shapes=[pltpu.VMEM((2, R_STAGE, D), BF16),
                        pltpu.SemaphoreType.DMA((2,))])
    return pl.pallas_call(
        _combine_kernel,
        out_shape=(jax.ShapeDtypeStruct((N_PROMPT, D), F32),
                   jax.ShapeDtypeStruct((N_SAMPLE, D), F32)),
        grid_spec=grid_spec,
        compiler_params=_cparams(("arbitrary",)),
        name="combine",
    )(chunk_off, pc, info, x1, mod3, fg, ys)


def kernel(x_prompt, x_sample, cache_diff_k, cache_diff_v, state_ret_fwd, state_ret_bwd, c, c_ctx,
           w_ada, b_ada, norm1_g, norm2_g, w_in, ret_decay_fwd, ret_decay_bwd, ret_norm_g,
           diff_lambda_q1, diff_lambda_k1, diff_lambda_q2, diff_lambda_k2, diff_subln_g,
           w_ret_o, w_diff_o, w_o, router_group_w, router_group_b, router_expert_w, router_expert_b,
           moe_w_gate, moe_w_up, moe_w_down, final_norm_g):
    l = 0
    lam_init = 0.8 - 0.6 * math.exp(-0.3 * l)
    lam = (jnp.exp(jnp.sum(diff_lambda_q1[l].astype(F32) * diff_lambda_k1[l].astype(F32)))
           - jnp.exp(jnp.sum(diff_lambda_q2[l].astype(F32) * diff_lambda_k2[l].astype(F32))) + lam_init)
    attn_scal = jnp.stack([lam, jnp.float32(1.0 - lam_init)]).astype(F32)
    lg = jnp.stack([jax.nn.log_sigmoid(ret_decay_fwd[l].astype(F32)),
                    jax.nn.log_sigmoid(ret_decay_bwd[l].astype(F32))])

    xp = x_prompt.reshape(N_PROMPT, D)
    xs = x_sample.reshape(N_SAMPLE, D)
    cvec = jnp.concatenate([c_ctx[None, :], c, jnp.zeros((3, D), F32)], axis=0)
    mod3 = _modulation(cvec, w_ada[l], b_ada[l][None, :]).reshape(8, 1, 6 * D)

    proj, kt32, v32 = _inproj(xp, xs, mod3, norm1_g[l][None, :], w_in[l])

    cos_t, sin_t = _rope_tables()
    cache_kt = jnp.transpose(cache_diff_k[:, l], (0, 2, 3, 4, 1)).reshape(4, D, PAST)
    (diff_act, ret_act, s_f, s_b, wg_bf, wu_bf, wd_bf, xs0, cws0) = _mixers(
        attn_scal, lg, proj, cache_kt, cache_diff_v, cos_t, sin_t, diff_subln_g[l][None, :],
        state_ret_fwd[:, l], state_ret_bwd[:, l], ret_norm_g[l][None, :],
        (moe_w_gate[l], moe_w_up[l], moe_w_down[l]))

    brt = jnp.concatenate([router_group_b[l], jnp.zeros((ROUTE_E0 - N_GROUPS,), F32), router_expert_b[l],
                           jnp.zeros((ROUTE_W - ROUTE_E0 - N_EXP,), F32)])[:, None]
    x1, h2, cw8, info, pc_arr, info_t = _outproj(
        ret_act, diff_act, proj, xp, xs, mod3, norm2_g[l][None, :],
        w_ret_o[l], w_diff_o[l], w_o[l], router_group_w[l], router_expert_w[l], brt)
    chunk_off, pc, tile_group, tile_valid = _dispatch_plan(pc_arr)
    xs_sorted, cw_sorted = _dispatch(chunk_off, pc, h2, info_t, cw8, xs0, cws0)
    y_sorted = _experts(tile_group, tile_valid, xs_sorted, cw_sorted, wg_bf, wu_bf, wd_bf)
    yp, ys = _combine(chunk_off, pc, info, x1, mod3, final_norm_g[None, :], y_sorted)

    return (yp.reshape(16, T_P, D), ys.reshape(4, T_S, D),
            jnp.transpose(kt32.reshape(16, DIFF_H, 2, DIFF_HD, T_P), (0, 4, 1, 2, 3))[:, None],
            v32.reshape(16, 1, T_P, DIFF_H, 2 * DIFF_HD),
            s_f.reshape(16, 1, RET_H, RET_KD, RET_VD), s_b.reshape(16, 1, RET_H, RET_KD, RET_VD))
```

```python
import functools
import math

import jax
import jax.numpy as jnp
import numpy as np
from jax import lax
from jax.experimental import pallas as pl
from jax.experimental.pallas import tpu as pltpu

F32 = jnp.float32
BF16 = jnp.bfloat16

D = 1024
N_PROMPT = 16 * 256
N_SAMPLE = 4 * 1024
N_TOK = N_PROMPT + N_SAMPLE
T_P = 256
T_S = 1024
PAST = 512
GRID_W = 64
RET_H = 4
RET_KD = 128
RET_VD = 256
DIFF_H = 8
DIFF_HD = 64
ROPE_BASE = 10000.0
N_GROUPS = 4
EPG = 8
N_EXP = 32
FF = 256
EPS = 1e-6
IN_W = 8192
LANE = 128
N_CHUNK = IN_W // LANE
C_RQ, C_RK, C_RV, C_RG, C_DQ, C_DK, C_DV, C_GR, C_GD = 0, 4, 8, 16, 24, 32, 40, 48, 56
ROUTE_W = 128
SEQ_BLK = 1024
N_PBLK = N_PROMPT // SEQ_BLK
VMEM_LIMIT = 56 * 1024 * 1024


def _cparams(sem):
    return pltpu.CompilerParams(dimension_semantics=sem, vmem_limit_bytes=VMEM_LIMIT)


def _mod_row(tile_rows):
    def f(i):
        start = i * tile_rows
        return jnp.where(start < N_PROMPT, 0, 1 + (start - N_PROMPT) // T_S)
    return f


def _rms(x):
    return x * lax.rsqrt(jnp.mean(x * x, axis=-1, keepdims=True) + EPS)


def _mod_kernel(c_ref, w_ref, b_ref, o_ref):
    s = jax.nn.silu(c_ref[...])
    o_ref[...] = jnp.dot(s.astype(BF16), w_ref[...].astype(BF16),
                         preferred_element_type=F32) + b_ref[...]


def _modulation(cvec, w_ada, b_ada):
    tn = 1536
    return pl.pallas_call(
        _mod_kernel,
        out_shape=jax.ShapeDtypeStruct((8, 6 * D), F32),
        grid=(6 * D // tn,),
        in_specs=[pl.BlockSpec((8, D), lambda j: (0, 0)),
                  pl.BlockSpec((D, tn), lambda j: (0, j)),
                  pl.BlockSpec((1, tn), lambda j: (0, j))],
        out_specs=pl.BlockSpec((8, tn), lambda j: (0, j)),
        compiler_params=_cparams(("arbitrary",)),
        name="mod",
    )(cvec, w_ada, b_ada)


IP_TM = 512
IP_TN = 2048
IP_NPT = N_PROMPT // IP_TM


IP_KV_TILE = C_DK * LANE // IP_TN
IP_SPT = IP_TM // T_P


def _inproj_kernel(xp_ref, xs_ref, mod_ref, n1_ref, w_ref, proj_ref, kt_ref, v32_ref, w_scr):
    j = pl.program_id(0)
    i = pl.program_id(1)

    @pl.when(i == 0)
    def _():
        w_scr[...] = w_ref[...].astype(BF16)

    x = jnp.where(i < IP_NPT, xp_ref[...], xs_ref[...])
    mod = mod_ref[...]
    h = (_rms(x) * n1_ref[...] * (1.0 + mod[:, D:2 * D]) + mod[:, 0:D]).astype(BF16)
    acc = jnp.dot(h, w_scr[...], preferred_element_type=F32)
    for c in range(IP_TN // LANE):
        proj_ref[c] = acc[:, c * LANE:(c + 1) * LANE].astype(BF16)

    @pl.when(jnp.logical_and(j == IP_KV_TILE, i < IP_NPT))
    def _():
        for s in range(IP_SPT):
            kt_ref[s] = acc[s * T_P:(s + 1) * T_P, :D].T
        v32_ref[...] = acc[:, D:]


def _inproj(xp, xs, mod3, n1, w_in):
    npt = IP_NPT
    cpt = IP_TN // LANE

    def kv_tile(j, i):
        return jnp.where(j < IP_KV_TILE, 0, jnp.where(j == IP_KV_TILE, jnp.minimum(i, npt - 1), npt - 1))

    return pl.pallas_call(
        _inproj_kernel,
        out_shape=(jax.ShapeDtypeStruct((N_CHUNK, N_TOK, LANE), BF16),
                   jax.ShapeDtypeStruct((N_PROMPT // T_P, D, T_P), F32),
                   jax.ShapeDtypeStruct((N_PROMPT, D), F32)),
        grid=(IN_W // IP_TN, N_TOK // IP_TM),
        in_specs=[pl.BlockSpec((IP_TM, D), lambda j, i: (jnp.minimum(i, npt - 1), 0)),
                  pl.BlockSpec((IP_TM, D), lambda j, i: (jnp.maximum(i - npt, 0), 0)),
                  pl.BlockSpec((None, 1, 6 * D), lambda j, i: (_mod_row(IP_TM)(i), 0, 0)),
                  pl.BlockSpec((1, D), lambda j, i: (0, 0)),
                  pl.BlockSpec((D, IP_TN), lambda j, i: (0, j))],
        out_specs=(pl.BlockSpec((cpt, IP_TM, LANE), lambda j, i: (j, i, 0)),
                   pl.BlockSpec((IP_SPT, D, T_P), lambda j, i: (kv_tile(j, i), 0, 0)),
                   pl.BlockSpec((IP_TM, D), lambda j, i: (kv_tile(j, i), 0))),
        scratch_shapes=[pltpu.VMEM((D, IP_TN), BF16)],
        compiler_params=_cparams(("arbitrary", "arbitrary")),
        name="inproj",
    )(xp, xs, mod3, n1, w_in)


def _decay_mask(t, lgf, lgb):
    ii = lax.broadcasted_iota(jnp.int32, (t, t), 0)
    jj = lax.broadcasted_iota(jnp.int32, (t, t), 1)
    rel = (ii - jj).astype(F32)
    e = jnp.exp(jnp.where(rel >= 0.0, lgf, -lgb) * rel)
    return jnp.where(rel == 0.0, 2.0, e) * (RET_KD ** -0.5)


def _ret_parts(lgf, lgb, q_ref, k_ref, v_ref, rg_ref, s0f_ref, s0b_ref, g_ref,
               o_ref, sf_ref, sb_ref, dm_ref):
    gain = g_ref[...]
    nt = (((1,), (1,)), ((), ()))
    tn = (((0,), (0,)), ((), ()))

    def finish(o, rg):
        d = o - jnp.mean(o, axis=-1, keepdims=True)
        y = d * lax.rsqrt(jnp.mean(d * d, axis=-1, keepdims=True) + EPS) * gain
        return (jax.nn.silu(rg.astype(F32)) * y).astype(BF16)

    def vcat(r):
        return jnp.concatenate([v_ref[0, r, :], v_ref[1, r, :]], axis=1)

    def gcat(r):
        return jnp.concatenate([rg_ref[0, r, :], rg_ref[1, r, :]], axis=1)

    def build_mask():
        dm_ref[...] = _decay_mask(T_P, lgf, lgb)

    n_chunk = SEQ_BLK // T_P
    chunks = [slice(c * T_P, (c + 1) * T_P) for c in range(n_chunk)]
    t = lax.broadcasted_iota(jnp.int32, (T_P, 1), 0).astype(F32)
    kdf = jnp.exp(lgf * (T_P - 1.0 - t)) * (RET_KD ** -0.5)
    kdb = jnp.exp(lgb * t) * (RET_KD ** -0.5)

    def intra(r):
        sc = lax.dot_general(q_ref[r, :], k_ref[r, :], nt, preferred_element_type=F32)
        return jnp.dot((sc * dm_ref[...]).astype(BF16), vcat(r), preferred_element_type=F32)

    def key_state(r, kd):
        kw = (k_ref[r, :].astype(F32) * kd).astype(BF16)
        return lax.dot_general(kw, vcat(r), tn, preferred_element_type=F32)

    def context_part():
        for s, r in enumerate(chunks):
            o_ref[r, :] = finish(intra(r), gcat(r))
            sf_ref[s] = key_state(r, kdf)
            sb_ref[s] = key_state(r, kdb)

    def latent_part():
        qdf = jnp.exp(lgf * (t + 1.0))
        qdb = jnp.exp(lgb * (T_P - t))
        span = jnp.full((1, 1), float(T_P), F32)
        cf = jnp.exp(lgf * span)
        cb = jnp.exp(lgb * span)
        before_b = [None] * n_chunk
        state = s0b_ref[...]
        for c in reversed(range(n_chunk)):
            before_b[c] = state
            if c > 0:
                state = cb * state + key_state(chunks[c], kdb)
        state = s0f_ref[...]
        for c, r in enumerate(chunks):
            qf = q_ref[r, :].astype(F32)
            o = (intra(r)
                 + jnp.dot((qf * qdf).astype(BF16), state.astype(BF16), preferred_element_type=F32)
                 + jnp.dot((qf * qdb).astype(BF16), before_b[c].astype(BF16), preferred_element_type=F32))
            o_ref[r, :] = finish(o, gcat(r))
            if c + 1 < n_chunk:
                state = cf * state + key_state(r, kdf)

    return build_mask, context_part, latent_part


ATT_TQ = 1024
ATT_HPS = 2
ATT_QSCALE = (DIFF_HD ** -0.5) * math.log2(math.e)


N_SIDE = 3


def _mixer_kernel(sc_ref, lg_ref, q_ref, k_ref, v_ref, ckt_ref, cv_ref, cos_ref, sin_ref, g_ref,
                  rq_ref, rk_ref, rv_ref, rg_ref, s0f_ref, s0b_ref, gn_ref, *rest):
    side_in = rest[:N_SIDE]
    o_ref, ro_ref, sf_ref, sb_ref = rest[N_SIDE:N_SIDE + 4]
    side_out = rest[N_SIDE + 4:2 * N_SIDE + 4]
    xs0_ref, cws0_ref, q_scr, k_scr, v_scr, dm_scr = rest[2 * N_SIDE + 4:]
    i = pl.program_id(0)
    rh = pl.program_id(1)
    build_mask, ret_context, ret_latent = _ret_parts(
        lg_ref[0, rh], lg_ref[1, rh], rq_ref, rk_ref, rv_ref, rg_ref, s0f_ref, s0b_ref, gn_ref,
        ro_ref, sf_ref, sb_ref, dm_scr.at[rh])

    @pl.when(i == 0)
    def _():
        build_mask()

    def side_jobs():
        for src, dst in zip(side_in, side_out):
            dst[...] = src[...].astype(BF16)
        xs0_ref[...] = jnp.zeros_like(xs0_ref)
        cws0_ref[...] = jnp.zeros_like(cws0_ref)

    lam = sc_ref[0]
    out_scale = sc_ref[1]
    gain = g_ref[...] * out_scale
    lane = lax.broadcasted_iota(jnp.int32, (1, LANE), 1)
    first = lane < DIFF_HD
    nt = (((1,), (1,)), ((), ()))

    def halves(q):
        zero = jnp.zeros_like(q)
        return jnp.where(first, q, zero), jnp.where(first, zero, q)

    def weights(s):
        return jnp.exp2(s - jnp.max(s, axis=-1, keepdims=True)).astype(BF16)

    def finish(of0, of1):
        o = of0[:, :LANE] / of0[:, LANE:] - lam * (of1[:, :LANE] / of1[:, LANE:])
        return (_rms(o) * gain).astype(BF16)

    @pl.when(i < N_PBLK)
    def _():
        ret_context()
        n_seq = SEQ_BLK // T_P
        rows = [slice(s * T_P, (s + 1) * T_P) for s in range(n_seq)]
        ones = jnp.ones((T_P, LANE), BF16)
        for hh in range(ATT_HPS):
            cols = slice(hh * LANE, (hh + 1) * LANE)
            q0, q1 = halves((q_ref[hh].astype(F32) * ATT_QSCALE).astype(BF16))
            s0 = jnp.concatenate([lax.dot_general(q0[r], k_ref[hh, r, :], nt, preferred_element_type=F32)
                                  for r in rows], axis=0)
            s1 = jnp.concatenate([lax.dot_general(q1[r], k_ref[hh, r, :], nt, preferred_element_type=F32)
                                  for r in rows], axis=0)
            e0 = weights(s0)
            e1 = weights(s1)
            for r in rows:
                v1 = jnp.concatenate([v_ref[hh, r, :], ones], axis=1)
                o_ref[r, cols] = finish(jnp.dot(e0[r], v1, preferred_element_type=F32),
                                        jnp.dot(e1[r], v1, preferred_element_type=F32))

    @pl.when(i >= N_PBLK)
    def _():
        side_jobs()
        ret_latent()
        cos = cos_ref[...]
        sin = sin_ref[...]
        low = (lax.broadcasted_iota(jnp.int32, (T_S, LANE), 1) & 16) == 0

        def rope(x):
            xs = jnp.where(low, pltpu.roll(x, LANE - 16, 1), pltpu.roll(x, 16, 1))
            return x * cos + xs * sin

        for hh in range(ATT_HPS):
            cols = slice(hh * LANE, (hh + 1) * LANE)
            head = pl.program_id(1) * ATT_HPS + hh
            q_scr[hh] = (rope(q_ref[hh].astype(F32)) * ATT_QSCALE).astype(BF16)
            k_scr[hh] = rope(k_ref[hh].astype(F32)).astype(BF16)
            ckt = ckt_ref[cols, :].astype(BF16)
            v_scr[hh, 0:T_S, 0:LANE] = v_ref[hh]
            v_scr[hh, T_S:T_S + PAST, 0:LANE] = cv_ref[:, head, :].astype(BF16)
            v_scr[hh, :, LANE:2 * LANE] = jnp.ones((T_S + PAST, LANE), BF16)

            def scores(qh, hh=hh, ckt=ckt):
                return jnp.concatenate([lax.dot_general(qh, k_scr[hh], nt, preferred_element_type=F32),
                                        jnp.dot(qh, ckt, preferred_element_type=F32)], axis=1)

            for b in range(T_S // ATT_TQ):
                r = slice(b * ATT_TQ, (b + 1) * ATT_TQ)
                q0, q1 = halves(q_scr[hh, r, :])
                e0 = weights(scores(q0))
                e1 = weights(scores(q1))
                o_ref[r, cols] = finish(jnp.dot(e0, v_scr[hh], preferred_element_type=F32),
                                        jnp.dot(e1, v_scr[hh], preferred_element_type=F32))


def _mixers(scal, lg, proj, cache_k, cache_v, cos_t, sin_t, subln_g, s0f, s0b, gnorm, side_weights):
    nb = N_TOK // SEQ_BLK
    spb = SEQ_BLK // T_P
    smp = lambda i: jnp.maximum(i - N_PBLK, 0)
    pmt = lambda i: jnp.minimum(i, N_PBLK - 1)
    pmh = lambda i, h: jnp.where(i < N_PBLK, h, RET_H - 1)
    state_in = pl.BlockSpec((None, None, RET_KD, RET_VD), lambda i, h: (smp(i), h, 0, 0))
    state_out = pl.BlockSpec((spb, None, RET_KD, RET_VD), lambda i, h: (pmt(i), pmh(i, h), 0, 0))

    n_hp = DIFF_H // ATT_HPS
    assert n_hp == RET_H
    n_steps = (nb - N_PBLK) * n_hp
    step = lambda i, h: jnp.maximum((i - N_PBLK) * n_hp + h, 0)
    exp_per_step = N_EXP // n_steps
    exp_slice = lambda i, h: (step(i, h), 0, 0)
    row_slice = lambda i, h: (step(i, h), 0)
    up_spec = pl.BlockSpec((exp_per_step, D, FF), exp_slice)
    down_spec = pl.BlockSpec((exp_per_step, FF, D), exp_slice)
    side_specs = [up_spec, up_spec, down_spec]
    side_shapes = [jax.ShapeDtypeStruct(a.shape, BF16) for a in side_weights]
    zrows = R_MAX // n_steps

    return pl.pallas_call(
        _mixer_kernel,
        out_shape=(jax.ShapeDtypeStruct((N_TOK, DIFF_H * 2 * DIFF_HD), BF16),
                   jax.ShapeDtypeStruct((N_TOK, RET_H * RET_VD), BF16),
                   jax.ShapeDtypeStruct((16, RET_H, RET_KD, RET_VD), F32),
                   jax.ShapeDtypeStruct((16, RET_H, RET_KD, RET_VD), F32),
                   *side_shapes,
                   jax.ShapeDtypeStruct((R_MAX, D), BF16),
                   jax.ShapeDtypeStruct((R_MAX, ROUTE_W), F32)),
        grid=(nb, n_hp),
        in_specs=[pl.BlockSpec(memory_space=pltpu.SMEM),
                  pl.BlockSpec(memory_space=pltpu.SMEM),
                  pl.BlockSpec((ATT_HPS, SEQ_BLK, LANE), lambda i, h: (C_DQ // ATT_HPS + h, i, 0)),
                  pl.BlockSpec((ATT_HPS, SEQ_BLK, LANE), lambda i, h: (C_DK // ATT_HPS + h, i, 0)),
                  pl.BlockSpec((ATT_HPS, SEQ_BLK, LANE), lambda i, h: (C_DV // ATT_HPS + h, i, 0)),
                  pl.BlockSpec((None, ATT_HPS * LANE, PAST), lambda i, h: (smp(i), h, 0)),
                  pl.BlockSpec((None, None, PAST, DIFF_H, LANE), lambda i, h: (smp(i), 0, 0, 0, 0)),
                  pl.BlockSpec((T_S, LANE), lambda i, h: (0, 0)),
                  pl.BlockSpec((T_S, LANE), lambda i, h: (0, 0)),
                  pl.BlockSpec((1, LANE), lambda i, h: (0, 0)),
                  pl.BlockSpec((None, SEQ_BLK, LANE), lambda i, h: (C_RQ + h, i, 0)),
                  pl.BlockSpec((None, SEQ_BLK, LANE), lambda i, h: (C_RK + h, i, 0)),
                  pl.BlockSpec((2, SEQ_BLK, LANE), lambda i, h: (C_RV // 2 + h, i, 0)),
                  pl.BlockSpec((2, SEQ_BLK, LANE), lambda i, h: (C_RG // 2 + h, i, 0)),
                  state_in, state_in,
                  pl.BlockSpec((1, RET_VD), lambda i, h: (0, h)),
                  *side_specs],
        out_specs=(pl.BlockSpec((SEQ_BLK, ATT_HPS * LANE), lambda i, h: (i, h)),
                   pl.BlockSpec((SEQ_BLK, RET_VD), lambda i, h: (i, h)),
                   state_out, state_out,
                   *side_specs,
                   pl.BlockSpec((zrows, D), row_slice),
                   pl.BlockSpec((zrows, ROUTE_W), row_slice)),
        scratch_shapes=[pltpu.VMEM((ATT_HPS, T_S, LANE), BF16),
                        pltpu.VMEM((ATT_HPS, T_S, LANE), BF16),
                        pltpu.VMEM((ATT_HPS, T_S + PAST, 2 * LANE), BF16),
                        pltpu.VMEM((RET_H, T_P, T_P), F32)],
        compiler_params=_cparams(("arbitrary", "arbitrary")),
        name="mixers",
    )(scal, lg, proj, proj, proj, cache_k, cache_v, cos_t, sin_t, subln_g,
      proj, proj, proj, proj, s0f, s0b, gnorm, *side_weights)


def _rope_tables():
    n_rows = T_S // GRID_W
    row = np.repeat(np.arange(n_rows), GRID_W).astype(np.float64)
    col = np.tile(np.arange(GRID_W), n_rows).astype(np.float64)
    n_freq = DIFF_HD // 4
    inv = ROPE_BASE ** (-np.arange(n_freq, dtype=np.float64) / n_freq)

    def axis_tables(pos):
        ang = pos[:, None] * inv[None, :]
        c = np.cos(ang)
        s = np.sin(ang)
        return np.concatenate([c, c], axis=-1), np.concatenate([-s, s], axis=-1)

    cr, sr = axis_tables(row)
    cc, sc = axis_tables(col)
    cos_h = np.concatenate([cr, cc], axis=-1)
    sin_h = np.concatenate([sr, sc], axis=-1)
    return (jnp.asarray(np.concatenate([cos_h, cos_h], axis=-1), F32),
            jnp.asarray(np.concatenate([sin_h, sin_h], axis=-1), F32))


OP_TM = 512
OP_NPT = N_PROMPT // OP_TM


def _outproj_kernel(ra_ref, da_ref, gr_ref, gd_ref, xp_ref, xs_ref, mod_ref, n2_ref,
                    wro32_ref, wdo32_ref, wo32_ref, rgw_ref, rew_ref, brt_ref,
                    x1_ref, h2_ref, cw8_ref, info_ref, infot_ref, off_ref, pctab_ref, steps_ref,
                    m_scr, wro_ref, wdo_ref, wo_ref, wrt_ref, pc_scr):
    i = pl.program_id(0)

    @pl.when(i == 0)
    def _():
        wro_ref[...] = wro32_ref[...].astype(BF16)
        wdo_ref[...] = wdo32_ref[...].astype(BF16)
        wo_ref[...] = wo32_ref[...].astype(BF16)
        gap = jnp.zeros((D, ROUTE_E0 - N_GROUPS), F32)
        pad = jnp.zeros((D, ROUTE_W - ROUTE_E0 - N_EXP), F32)
        wr = jnp.concatenate([rgw_ref[...], gap, rew_ref[...], pad], axis=1)
        wrt_ref[...] = wr.T.astype(BF16)

    ret_out = jnp.dot(ra_ref[...], wro_ref[...], preferred_element_type=F32)
    diff_out = jnp.dot(da_ref[...], wdo_ref[...], preferred_element_type=F32)
    for c in range(D // LANE):
        sl = slice(c * LANE, (c + 1) * LANE)
        m = (jax.nn.sigmoid(gr_ref[c].astype(F32)) * ret_out[:, sl]
             + jax.nn.sigmoid(gd_ref[c].astype(F32)) * diff_out[:, sl])
        m_scr[:, sl] = m.astype(BF16)
    mix = jnp.dot(m_scr[...], wo_ref[...], preferred_element_type=F32)
    mod = mod_ref[...]

    x1 = jnp.where(i < OP_NPT, xp_ref[...], xs_ref[...]) + mod[:, 2 * D:3 * D] * mix
    x1_ref[...] = x1
    h2 = (_rms(x1) * n2_ref[...] * (1.0 + mod[:, 4 * D:5 * D]) + mod[:, 3 * D:4 * D]).astype(BF16)
    h2_ref[...] = h2
    logits_t = lax.dot_general(wrt_ref[...], h2, (((1,), (1,)), ((), ())),
                               preferred_element_type=F32) + brt_ref[...]
    cw8_ref[...], info_ref[...], pc, infot_ref[...] = _route_cols(logits_t)
    pc_scr[pl.ds(i, 1), :] = pc

    @pl.when(i == RT_NT - 1)
    def _():
        off_ref[...], pctab_ref[...], steps_ref[...] = _dispatch_plan(pc_scr[...])


def _outproj(ret_act, diff_act, proj, xp, xs, mod3, n2, wro, wdo, wo, rgw, rew, brt):
    assert OP_TM == RT_TM
    npt = OP_NPT
    full = lambda i: (0, 0)
    once = pl.Buffered(1)
    return pl.pallas_call(
        _outproj_kernel,
        out_shape=(jax.ShapeDtypeStruct((N_TOK, D), F32),
                   jax.ShapeDtypeStruct((N_TOK, D), BF16),
                   jax.ShapeDtypeStruct((N_TOK, ROUTE_W), F32),
                   jax.ShapeDtypeStruct((N_TOK, ROUTE_W), F32),
                   jax.ShapeDtypeStruct((RT_NT, EPG, RT_TM), F32),
                   jax.ShapeDtypeStruct((RT_NT, ROUTE_W), jnp.int32),
                   jax.ShapeDtypeStruct((RT_NT, ROUTE_W), jnp.int32),
                   jax.ShapeDtypeStruct((PLAN_ROWS, ROUTE_W), jnp.int32)),
        grid=(N_TOK // OP_TM,),
        in_specs=[pl.BlockSpec((OP_TM, D), lambda i: (i, 0)),
                  pl.BlockSpec((OP_TM, D), lambda i: (i, 0)),
                  pl.BlockSpec((8, OP_TM, LANE), lambda i: (C_GR // 8, i, 0)),
                  pl.BlockSpec((8, OP_TM, LANE), lambda i: (C_GD // 8, i, 0)),
                  pl.BlockSpec((OP_TM, D), lambda i: (jnp.minimum(i, npt - 1), 0)),
                  pl.BlockSpec((OP_TM, D), lambda i: (jnp.maximum(i - npt, 0), 0)),
                  pl.BlockSpec((None, 1, 6 * D), lambda i: (_mod_row(OP_TM)(i), 0, 0)),
                  pl.BlockSpec((1, D), full),
                  pl.BlockSpec((D, D), full, pipeline_mode=once),
                  pl.BlockSpec((D, D), full, pipeline_mode=once),
                  pl.BlockSpec((D, D), full, pipeline_mode=once),
                  pl.BlockSpec((D, N_GROUPS), full, pipeline_mode=once),
                  pl.BlockSpec((D, N_EXP), full, pipeline_mode=once),
                  pl.BlockSpec((ROUTE_W, 1), full)],
        out_specs=(pl.BlockSpec((OP_TM, D), lambda i: (i, 0)),
                   pl.BlockSpec((OP_TM, D), lambda i: (i, 0)),
                   pl.BlockSpec((OP_TM, ROUTE_W), lambda i: (i, 0)),
                   pl.BlockSpec((OP_TM, ROUTE_W), lambda i: (i, 0)),
                   pl.BlockSpec((None, EPG, RT_TM), lambda i: (i, 0, 0)),
                   pl.BlockSpec((RT_NT, ROUTE_W), full),
                   pl.BlockSpec((RT_NT, ROUTE_W), full),
                   pl.BlockSpec((PLAN_ROWS, ROUTE_W), full)),
        scratch_shapes=[pltpu.VMEM((OP_TM, D), BF16),
                        pltpu.VMEM((D, D), BF16), pltpu.VMEM((D, D), BF16), pltpu.VMEM((D, D), BF16),
                        pltpu.VMEM((ROUTE_W, D), BF16),
                        pltpu.VMEM((RT_NT, ROUTE_W), F32)],
        compiler_params=_cparams(("arbitrary",)),
        name="outproj",
    )(ret_act, diff_act, proj, proj, xp, xs, mod3, n2, wro, wdo, wo, rgw, rew, brt)


RT_TM = 512
RT_NT = N_TOK // RT_TM
PIECE = 16
R_LOC = RT_TM + N_GROUPS * PIECE
R_STAGE = 640
EX_TM = 512
R_MAX = 11264
EX_NT = R_MAX // EX_TM


ROUTE_E0 = EPG


def _route_cols(lgt):
    row = lax.broadcasted_iota(jnp.int32, (EPG, RT_TM), 0)
    neg = jnp.float32(-jnp.inf)

    def first_row(cond):
        return jnp.min(jnp.where(cond, row, EPG), axis=0, keepdims=True)

    head = lgt[0:EPG]
    is_g = row < N_GROUPS
    gl = jnp.where(is_g, head, neg)
    gmax = jnp.max(gl, axis=0, keepdims=True)
    gsum = jnp.sum(jnp.where(is_g, jnp.exp(head - gmax), 0.0), axis=0, keepdims=True)
    p_top = 1.0 / gsum
    g_idx = first_row(gl == gmax)
    el = lgt[ROUTE_E0 + (N_GROUPS - 1) * EPG:ROUTE_E0 + N_GROUPS * EPG]
    for g in reversed(range(N_GROUPS - 1)):
        el = jnp.where(g_idx == g, lgt[ROUTE_E0 + g * EPG:ROUTE_E0 + (g + 1) * EPG], el)
    ee = jnp.exp(el - jnp.max(el, axis=0, keepdims=True))
    ep = ee / jnp.sum(ee, axis=0, keepdims=True)
    e1 = jnp.max(ep, axis=0, keepdims=True)
    i1 = first_row(ep == e1)
    ep2 = jnp.where(row == i1, -1.0, ep)
    e2 = jnp.max(ep2, axis=0, keepdims=True)
    i2 = first_row(ep2 == e2)
    den = e1 + e2
    cw8_t = (jnp.where(row == i1, p_top * e1 / den, 0.0)
             + jnp.where(row == i2, p_top * e2 / den, 0.0))

    onehot = (row == g_idx).astype(F32)
    ii = lax.broadcasted_iota(jnp.int32, (RT_TM, RT_TM), 0)
    jj = lax.broadcasted_iota(jnp.int32, (RT_TM, RT_TM), 1)
    earlier = (ii < jj).astype(BF16)
    prefix = jnp.dot(onehot.astype(BF16), earlier, preferred_element_type=F32)
    cnt = jnp.sum(onehot, axis=1, keepdims=True)
    pc_col = jnp.floor((cnt + (PIECE - 1.0)) * (1.0 / PIECE)) * PIECE
    row1 = lax.broadcasted_iota(jnp.int32, (EPG, 1), 0)
    lane1 = lax.broadcasted_iota(jnp.int32, (1, ROUTE_W), 1)
    lo = jnp.zeros((EPG, 1), F32)
    run = jnp.zeros((1, 1), F32)
    pc = jnp.zeros((1, ROUTE_W), F32)
    for g in range(N_GROUPS):
        pc_g = jnp.sum(jnp.where(row1 == g, pc_col, 0.0), axis=0, keepdims=True)
        lo = jnp.where(row1 == g, run, lo)
        pc = jnp.where(lane1 == g, pc_g, pc)
        run = run + pc_g
    dest = jnp.sum(onehot * (prefix + lo), axis=0, keepdims=True)
    info_t = jnp.where(row == 0, g_idx.astype(F32), jnp.where(row == 1, dest, 0.0))

    slab = jnp.concatenate([cw8_t, info_t, jnp.zeros((LANE - 2 * EPG, RT_TM), F32)], axis=0)
    cols = slab.T
    lane = lax.broadcasted_iota(jnp.int32, cols.shape, 1)
    cw8 = jnp.where(lane < EPG, cols, 0.0)
    info = jnp.where(lane < 2, pltpu.roll(cols, LANE - EPG, 1), 0.0)
    return cw8, info, pc, info_t


PLAN_ROWS = 32


def _dispatch_plan(pc_all):
    lane = lax.broadcasted_iota(jnp.int32, (1, ROUTE_W), 1)
    is_g = lane < N_GROUPS
    seg_len = jnp.sum(pc_all, axis=0, keepdims=True)
    seg_pad = jnp.floor((seg_len + (EX_TM - 1.0)) * (1.0 / EX_TM)) * EX_TM
    run = seg_pad + pltpu.roll(seg_pad, 1, 1)
    seg_end_pad = jnp.where(is_g, run + pltpu.roll(run, 2, 1), 0.0)
    seg_start = seg_end_pad - seg_pad
    ti = lax.broadcasted_iota(jnp.int32, (RT_NT, RT_NT), 0)
    tj = lax.broadcasted_iota(jnp.int32, (RT_NT, RT_NT), 1)
    earlier = (tj < ti).astype(BF16)
    within = jnp.dot(earlier, pc_all.astype(BF16), preferred_element_type=F32)
    chunk_off = jnp.where(is_g, seg_start + within, 0.0)

    start = (lax.broadcasted_iota(jnp.int32, (PLAN_ROWS, 1), 0) * EX_TM).astype(F32)
    passed = jnp.where(jnp.logical_and(is_g, start >= seg_end_pad), 1.0, 0.0)
    group = jnp.minimum(jnp.sum(passed, axis=1, keepdims=True), N_GROUPS - 1.0)
    used_end = jnp.sum(jnp.where(lane == group.astype(jnp.int32), seg_start + seg_len, 0.0),
                       axis=1, keepdims=True)
    valid = jnp.where(start < used_end, 1.0, 0.0)
    steps = jnp.where(lane == 0, group, jnp.where(lane == 1, valid, 0.0))
    return chunk_off.astype(jnp.int32), pc_all.astype(jnp.int32), steps.astype(jnp.int32)


def _piece_copies(off_ref, pc_ref, tile, make):
    lo = 0
    for g in range(N_GROUPS):
        n = pc_ref[tile, g] // PIECE
        base = off_ref[tile, g]

        def body(j, carry, lo=lo, base=base):
            make(pl.multiple_of(lo + j * PIECE, PIECE), pl.multiple_of(base + j * PIECE, PIECE))
            return carry

        lax.fori_loop(0, n, body, 0)
        lo = lo + pc_ref[tile, g]


def _piece_count(pc_ref, tile):
    n = 0
    for g in range(N_GROUPS):
        n = n + pc_ref[tile, g] // PIECE
    return n


def _dispatch_kernel(off_ref, pc_ref, h_ref, infot_ref, cw8_ref, xs_in, cws_in,
                     xs_out, cws_out, x_scr, c_scr, sem):
    del xs_in, cws_in
    i = pl.program_id(0)
    slot = i % 2
    dest = infot_ref[1:2, :]
    row = lax.broadcasted_iota(jnp.int32, (R_LOC, RT_TM), 0).astype(F32)
    sel = row == dest
    sel = sel.astype(BF16)
    x_scr[slot] = jnp.dot(sel, h_ref[...], preferred_element_type=F32).astype(BF16)
    cw = cw8_ref[...]
    hi = cw.astype(BF16).astype(F32)
    mid = (cw - hi).astype(BF16).astype(F32)
    low = (cw - hi - mid).astype(BF16).astype(F32)
    pieces = (hi + pltpu.roll(mid, EPG, 1) + pltpu.roll(low, 2 * EPG, 1)).astype(BF16)
    c_scr[slot] = jnp.dot(sel, pieces, preferred_element_type=F32)

    def x_copy(s, src, dst):
        return pltpu.make_async_copy(x_scr.at[s, pl.ds(src, PIECE)], xs_out.at[pl.ds(dst, PIECE)],
                                     sem.at[0, s])

    def c_copy(s, src, dst):
        return pltpu.make_async_copy(c_scr.at[s, pl.ds(src, PIECE)], cws_out.at[pl.ds(dst, PIECE)],
                                     sem.at[1, s])

    def start(src, dst):
        x_copy(slot, src, dst).start()
        c_copy(slot, src, dst).start()

    _piece_copies(off_ref, pc_ref, i, start)

    def wait_tile(tile, s):
        def wait(j, carry):
            x_copy(s, 0, 0).wait()
            c_copy(s, 0, 0).wait()
            return carry

        lax.fori_loop(0, _piece_count(pc_ref, tile), wait, 0)

    @pl.when(i > 0)
    def _():
        wait_tile(i - 1, 1 - slot)

    @pl.when(i == RT_NT - 1)
    def _():
        wait_tile(i, slot)


def _dispatch(chunk_off, pc, h2, info_t, cw8, xs0, cws0):
    grid_spec = pltpu.PrefetchScalarGridSpec(
        num_scalar_prefetch=2,
        grid=(RT_NT,),
        in_specs=[pl.BlockSpec((RT_TM, D), lambda i, o, p: (i, 0)),
                  pl.BlockSpec((None, EPG, RT_TM), lambda i, o, p: (i, 0, 0)),
                  pl.BlockSpec((RT_TM, ROUTE_W), lambda i, o, p: (i, 0)),
                  pl.BlockSpec(memory_space=pl.ANY),
                  pl.BlockSpec(memory_space=pl.ANY)],
        out_specs=(pl.BlockSpec(memory_space=pl.ANY), pl.BlockSpec(memory_space=pl.ANY)),
        scratch_shapes=[pltpu.VMEM((2, R_LOC, D), BF16),
                        pltpu.VMEM((2, R_LOC, ROUTE_W), F32),
                        pltpu.SemaphoreType.DMA((2, 2))])
    return pl.pallas_call(
        _dispatch_kernel,
        out_shape=(jax.ShapeDtypeStruct((R_MAX, D), BF16),
                   jax.ShapeDtypeStruct((R_MAX, ROUTE_W), F32)),
        grid_spec=grid_spec,
        input_output_aliases={5: 0, 6: 1},
        compiler_params=_cparams(("arbitrary",)),
        name="dispatch",
    )(chunk_off, pc, h2, info_t, cw8, xs0, cws0)


def _expert_kernel(steps_ref, x_ref, cw_ref, wg_ref, wu_ref, wd_ref, y_ref):
    k = pl.program_id(0)
    valid = steps_ref[k, 1]

    @pl.when(valid == 0)
    def _():
        y_ref[...] = jnp.zeros_like(y_ref)

    @pl.when(valid != 0)
    def _():
        x = x_ref[...]
        cw = cw_ref[...]
        lane = lax.broadcasted_iota(jnp.int32, cw.shape, 1)
        acc = jnp.zeros(y_ref.shape, F32)
        for j in range(EPG):
            mine = jnp.logical_and((lane & (EPG - 1)) == j, lane < 3 * EPG)
            w = jnp.sum(jnp.where(mine, cw, 0.0), axis=-1, keepdims=True)
            a = (jax.nn.silu(jnp.dot(x, wg_ref[j], preferred_element_type=F32))
                 * jnp.dot(x, wu_ref[j], preferred_element_type=F32))
            acc = acc + jnp.dot((a * w).astype(BF16), wd_ref[j], preferred_element_type=F32)
        y_ref[...] = acc.astype(BF16)


def _experts(steps, xs, cws, wg, wu, wd):
    grid_spec = pltpu.PrefetchScalarGridSpec(
        num_scalar_prefetch=1,
        grid=(EX_NT,),
        in_specs=[pl.BlockSpec((EX_TM, D), lambda k, st: (k, 0)),
                  pl.BlockSpec((EX_TM, ROUTE_W), lambda k, st: (k, 0)),
                  pl.BlockSpec((EPG, D, FF), lambda k, st: (st[k, 0], 0, 0)),
                  pl.BlockSpec((EPG, D, FF), lambda k, st: (st[k, 0], 0, 0)),
                  pl.BlockSpec((EPG, FF, D), lambda k, st: (st[k, 0], 0, 0))],
        out_specs=pl.BlockSpec((EX_TM, D), lambda k, st: (k, 0)))
    return pl.pallas_call(
        _expert_kernel,
        out_shape=jax.ShapeDtypeStruct((R_MAX, D), BF16),
        grid_spec=grid_spec,
        compiler_params=_cparams(("arbitrary",)),
        name="experts",
    )(steps, xs, cws, wg, wu, wd)


CB_NPT = N_PROMPT // RT_TM


def _combine_kernel(off_ref, pc_ref, info_ref, x1_ref, mod_ref, g_ref, ys_hbm,
                    yp_ref, ysm_ref, stage, sem):
    i = pl.program_id(0)
    slot = i % 2

    def copy(s, dst, src):
        return pltpu.make_async_copy(ys_hbm.at[pl.ds(src, PIECE)], stage.at[s, pl.ds(dst, PIECE)],
                                     sem.at[s])

    def fetch(tile, s):
        _piece_copies(off_ref, pc_ref, tile, lambda loc, glob: copy(s, loc, glob).start())

    @pl.when(i == 0)
    def _():
        stage[...] = jnp.zeros_like(stage)
        fetch(0, 0)

    @pl.when(i + 1 < RT_NT)
    def _():
        fetch(i + 1, 1 - slot)

    def wait(j, carry):
        copy(slot, 0, 0).wait()
        return carry

    lax.fori_loop(0, _piece_count(pc_ref, i), wait, 0)

    dest = info_ref[...][:, 1:2]
    col = lax.broadcasted_iota(jnp.int32, (RT_TM, R_STAGE), 1).astype(F32)
    moe = jnp.dot((col == dest).astype(BF16), stage[slot], preferred_element_type=F32)
    mod = mod_ref[...]
    out = _rms(x1_ref[...] + mod[:, 5 * D:6 * D] * moe) * g_ref[...]

    @pl.when(i < CB_NPT)
    def _():
        yp_ref[...] = out

    @pl.when(i >= CB_NPT)
    def _():
        ysm_ref[...] = out


def _combine(chunk_off, pc, info, x1, mod3, fg, ys):
    npt = CB_NPT
    grid_spec = pltpu.PrefetchScalarGridSpec(
        num_scalar_prefetch=2,
        grid=(RT_NT,),
        in_specs=[pl.BlockSpec((RT_TM, ROUTE_W), lambda i, o, p: (i, 0)),
                  pl.BlockSpec((RT_TM, D), lambda i, o, p: (i, 0)),
                  pl.BlockSpec((None, 1, 6 * D), lambda i, o, p: (_mod_row(RT_TM)(i), 0, 0)),
                  pl.BlockSpec((1, D), lambda i, o, p: (0, 0)),
                  pl.BlockSpec(memory_space=pl.ANY)],
        out_specs=(pl.BlockSpec((RT_TM, D), lambda i, o, p: (jnp.minimum(i, npt - 1), 0)),
                   pl.BlockSpec((RT_TM, D), lambda i, o, p: (jnp.maximum(i - npt, 0), 0))),
        scratch_shapes=[pltpu.VMEM((2, R_STAGE, D), BF16),
                        pltpu.SemaphoreType.DMA((2,))])
    return pl.pallas_call(
        _combine_kernel,
        out_shape=(jax.ShapeDtypeStruct((N_PROMPT, D), F32),
                   jax.ShapeDtypeStruct((N_SAMPLE, D), F32)),
        grid_spec=grid_spec,
        compiler_params=_cparams(("arbitrary",)),
        name="combine",
    )(chunk_off, pc, info, x1, mod3, fg, ys)


def kernel(x_prompt, x_sample, cache_diff_k, cache_diff_v, state_ret_fwd, state_ret_bwd, c, c_ctx,
           w_ada, b_ada, norm1_g, norm2_g, w_in, ret_decay_fwd, ret_decay_bwd, ret_norm_g,
           diff_lambda_q1, diff_lambda_k1, diff_lambda_q2, diff_lambda_k2, diff_subln_g,
           w_ret_o, w_diff_o, w_o, router_group_w, router_group_b, router_expert_w, router_expert_b,
           moe_w_gate, moe_w_up, moe_w_down, final_norm_g):
    l = 0
    lam_init = 0.8 - 0.6 * math.exp(-0.3 * l)
    lam = (jnp.exp(jnp.sum(diff_lambda_q1[l].astype(F32) * diff_lambda_k1[l].astype(F32)))
           - jnp.exp(jnp.sum(diff_lambda_q2[l].astype(F32) * diff_lambda_k2[l].astype(F32))) + lam_init)
    attn_scal = jnp.stack([lam, jnp.float32(1.0 - lam_init)]).astype(F32)
    lg = jnp.stack([jax.nn.log_sigmoid(ret_decay_fwd[l].astype(F32)),
                    jax.nn.log_sigmoid(ret_decay_bwd[l].astype(F32))])

    xp = x_prompt.reshape(N_PROMPT, D)
    xs = x_sample.reshape(N_SAMPLE, D)
    cvec = jnp.concatenate([c_ctx[None, :], c, jnp.zeros((3, D), F32)], axis=0)
    mod3 = _modulation(cvec, w_ada[l], b_ada[l][None, :]).reshape(8, 1, 6 * D)

    proj, kt32, v32 = _inproj(xp, xs, mod3, norm1_g[l][None, :], w_in[l])

    cos_t, sin_t = _rope_tables()
    cache_kt = jnp.transpose(cache_diff_k[:, l], (0, 2, 3, 4, 1)).reshape(4, D, PAST)
    (diff_act, ret_act, s_f, s_b, wg_bf, wu_bf, wd_bf, xs0, cws0) = _mixers(
        attn_scal, lg, proj, cache_kt, cache_diff_v, cos_t, sin_t, diff_subln_g[l][None, :],
        state_ret_fwd[:, l], state_ret_bwd[:, l], ret_norm_g[l][None, :],
        (moe_w_gate[l], moe_w_up[l], moe_w_down[l]))

    brt = jnp.concatenate([router_group_b[l], jnp.zeros((ROUTE_E0 - N_GROUPS,), F32), router_expert_b[l],
                           jnp.zeros((ROUTE_W - ROUTE_E0 - N_EXP,), F32)])[:, None]
    x1, h2, cw8, info, info_t, chunk_off, pc, steps = _outproj(
        ret_act, diff_act, proj, xp, xs, mod3, norm2_g[l][None, :],
        w_ret_o[l], w_diff_o[l], w_o[l], router_group_w[l], router_expert_w[l], brt)
    xs_sorted, cw_sorted = _dispatch(chunk_off, pc, h2, info_t, cw8, xs0, cws0)
    y_sorted = _experts(steps, xs_sorted, cw_sorted, wg_bf, wu_bf, wd_bf)
    yp, ys = _combine(chunk_off, pc, info, x1, mod3, final_norm_g[None, :], y_sorted)

    return (yp.reshape(16, T_P, D), ys.reshape(4, T_S, D),
            jnp.transpose(kt32.reshape(16, DIFF_H, 2, DIFF_HD, T_P), (0, 4, 1, 2, 3))[:, None],
            v32.reshape(16, 1, T_P, DIFF_H, 2 * DIFF_HD),
            s_f.reshape(16, 1, RET_H, RET_KD, RET_VD), s_b.reshape(16, 1, RET_H, RET_KD, RET_VD))
```

```python
import functools
import math

import jax
import jax.numpy as jnp
import numpy as np
from jax import lax
from jax.experimental import pallas as pl
from jax.experimental.pallas import tpu as pltpu

F32 = jnp.float32
BF16 = jnp.bfloat16

D = 1024
N_PROMPT = 16 * 256
N_SAMPLE = 4 * 1024
N_TOK = N_PROMPT + N_SAMPLE
T_P = 256
T_S = 1024
PAST = 512
GRID_W = 64
RET_H = 4
RET_KD = 128
RET_VD = 256
DIFF_H = 8
DIFF_HD = 64
ROPE_BASE = 10000.0
N_GROUPS = 4
EPG = 8
N_EXP = 32
FF = 256
EPS = 1e-6
IN_W = 8192
LANE = 128
N_CHUNK = IN_W // LANE
C_RQ, C_RK, C_RV, C_RG, C_DQ, C_DK, C_DV, C_GR, C_GD = 0, 4, 8, 16, 24, 32, 40, 48, 56
ROUTE_W = 128
SEQ_BLK = 1024
N_PBLK = N_PROMPT // SEQ_BLK
VMEM_LIMIT = 56 * 1024 * 1024


def _cparams(sem):
    return pltpu.CompilerParams(dimension_semantics=sem, vmem_limit_bytes=VMEM_LIMIT)


def _mod_row(tile_rows):
    def f(i):
        start = i * tile_rows
        return jnp.where(start < N_PROMPT, 0, 1 + (start - N_PROMPT) // T_S)
    return f


def _rms(x):
    return x * lax.rsqrt(jnp.mean(x * x, axis=-1, keepdims=True) + EPS)


MOD_ROWS = 8


def _mod_kernel(ctx_ref, c_ref, w_ref, b_ref, o_ref):
    cond = jnp.concatenate([ctx_ref[...], c_ref[...],
                            jnp.zeros((MOD_ROWS - 1 - c_ref.shape[0], D), F32)], axis=0)
    s = jax.nn.silu(cond)
    out = jnp.dot(s.astype(BF16), w_ref[...].astype(BF16), preferred_element_type=F32) + b_ref[...]
    for r in range(MOD_ROWS):
        o_ref[r] = out[r:r + 1, :]


def _modulation(c_ctx, c, w_ada, b_ada):
    tn = 1536
    return pl.pallas_call(
        _mod_kernel,
        out_shape=jax.ShapeDtypeStruct((MOD_ROWS, 1, 6 * D), F32),
        grid=(6 * D // tn,),
        in_specs=[pl.BlockSpec((1, D), lambda j: (0, 0)),
                  pl.BlockSpec(c.shape, lambda j: (0, 0)),
                  pl.BlockSpec((D, tn), lambda j: (0, j)),
                  pl.BlockSpec((1, tn), lambda j: (0, j))],
        out_specs=pl.BlockSpec((MOD_ROWS, 1, tn), lambda j: (0, 0, j)),
        compiler_params=_cparams(("arbitrary",)),
        name="mod",
    )(c_ctx, c, w_ada, b_ada)


IP_TM = 512
IP_TN = 2048
IP_NPT = N_PROMPT // IP_TM


IP_KV_TILE = C_DK * LANE // IP_TN
IP_SPT = IP_TM // T_P


def _inproj_kernel(xp_ref, xs_ref, mod_ref, n1_ref, w_ref, proj_ref, kt_ref, v32_ref, w_scr):
    j = pl.program_id(0)
    i = pl.program_id(1)

    @pl.when(i == 0)
    def _():
        w_scr[...] = w_ref[...].astype(BF16)

    x = jnp.where(i < IP_NPT, xp_ref[...], xs_ref[...])
    mod = mod_ref[...]
    h = (_rms(x) * n1_ref[...] * (1.0 + mod[:, D:2 * D]) + mod[:, 0:D]).astype(BF16)
    acc = jnp.dot(h, w_scr[...], preferred_element_type=F32)
    for c in range(IP_TN // LANE):
        proj_ref[c] = acc[:, c * LANE:(c + 1) * LANE].astype(BF16)

    @pl.when(jnp.logical_and(j == IP_KV_TILE, i < IP_NPT))
    def _():
        for s in range(IP_SPT):
            kt_ref[s] = acc[s * T_P:(s + 1) * T_P, :D].T
        v32_ref[...] = acc[:, D:]


def _inproj(xp, xs, mod3, n1, w_in):
    npt = IP_NPT
    cpt = IP_TN // LANE

    def kv_tile(j, i):
        return jnp.where(j < IP_KV_TILE, 0, jnp.where(j == IP_KV_TILE, jnp.minimum(i, npt - 1), npt - 1))

    return pl.pallas_call(
        _inproj_kernel,
        out_shape=(jax.ShapeDtypeStruct((N_CHUNK, N_TOK, LANE), BF16),
                   jax.ShapeDtypeStruct((N_PROMPT // T_P, D, T_P), F32),
                   jax.ShapeDtypeStruct((N_PROMPT, D), F32)),
        grid=(IN_W // IP_TN, N_TOK // IP_TM),
        in_specs=[pl.BlockSpec((IP_TM, D), lambda j, i: (jnp.minimum(i, npt - 1), 0)),
                  pl.BlockSpec((IP_TM, D), lambda j, i: (jnp.maximum(i - npt, 0), 0)),
                  pl.BlockSpec((None, 1, 6 * D), lambda j, i: (_mod_row(IP_TM)(i), 0, 0)),
                  pl.BlockSpec((1, D), lambda j, i: (0, 0)),
                  pl.BlockSpec((D, IP_TN), lambda j, i: (0, j))],
        out_specs=(pl.BlockSpec((cpt, IP_TM, LANE), lambda j, i: (j, i, 0)),
                   pl.BlockSpec((IP_SPT, D, T_P), lambda j, i: (kv_tile(j, i), 0, 0)),
                   pl.BlockSpec((IP_TM, D), lambda j, i: (kv_tile(j, i), 0))),
        scratch_shapes=[pltpu.VMEM((D, IP_TN), BF16)],
        compiler_params=_cparams(("arbitrary", "arbitrary")),
        name="inproj",
    )(xp, xs, mod3, n1, w_in)


def _decay_mask(t, lgf, lgb):
    ii = lax.broadcasted_iota(jnp.int32, (t, t), 0)
    jj = lax.broadcasted_iota(jnp.int32, (t, t), 1)
    rel = (ii - jj).astype(F32)
    e = jnp.exp(jnp.where(rel >= 0.0, lgf, -lgb) * rel)
    return jnp.where(rel == 0.0, 2.0, e) * (RET_KD ** -0.5)


def _ret_parts(lgf, lgb, q_ref, k_ref, v_ref, rg_ref, s0f_ref, s0b_ref, g_ref,
               o_ref, sf_ref, sb_ref, dm_ref):
    gain = g_ref[...]
    nt = (((1,), (1,)), ((), ()))
    tn = (((0,), (0,)), ((), ()))

    def finish(o, rg):
        d = o - jnp.mean(o, axis=-1, keepdims=True)
        y = d * lax.rsqrt(jnp.mean(d * d, axis=-1, keepdims=True) + EPS) * gain
        return (jax.nn.silu(rg.astype(F32)) * y).astype(BF16)

    def vcat(r):
        return jnp.concatenate([v_ref[0, r, :], v_ref[1, r, :]], axis=1)

    def gcat(r):
        return jnp.concatenate([rg_ref[0, r, :], rg_ref[1, r, :]], axis=1)

    def build_mask():
        dm_ref[...] = _decay_mask(T_P, lgf, lgb)

    n_chunk = SEQ_BLK // T_P
    chunks = [slice(c * T_P, (c + 1) * T_P) for c in range(n_chunk)]
    t = lax.broadcasted_iota(jnp.int32, (T_P, 1), 0).astype(F32)
    kdf = jnp.exp(lgf * (T_P - 1.0 - t)) * (RET_KD ** -0.5)
    kdb = jnp.exp(lgb * t) * (RET_KD ** -0.5)

    def intra(r):
        sc = lax.dot_general(q_ref[r, :], k_ref[r, :], nt, preferred_element_type=F32)
        return jnp.dot((sc * dm_ref[...]).astype(BF16), vcat(r), preferred_element_type=F32)

    def key_state(r, kd):
        kw = (k_ref[r, :].astype(F32) * kd).astype(BF16)
        return lax.dot_general(kw, vcat(r), tn, preferred_element_type=F32)

    def context_part():
        for s, r in enumerate(chunks):
            o_ref[r, :] = finish(intra(r), gcat(r))
            sf_ref[s] = key_state(r, kdf)
            sb_ref[s] = key_state(r, kdb)

    def latent_part():
        qdf = jnp.exp(lgf * (t + 1.0))
        qdb = jnp.exp(lgb * (T_P - t))
        span = jnp.full((1, 1), float(T_P), F32)
        cf = jnp.exp(lgf * span)
        cb = jnp.exp(lgb * span)
        before_b = [None] * n_chunk
        state = s0b_ref[...]
        for c in reversed(range(n_chunk)):
            before_b[c] = state
            if c > 0:
                state = cb * state + key_state(chunks[c], kdb)
        state = s0f_ref[...]
        for c, r in enumerate(chunks):
            qf = q_ref[r, :].astype(F32)
            o = (intra(r)
                 + jnp.dot((qf * qdf).astype(BF16), state.astype(BF16), preferred_element_type=F32)
                 + jnp.dot((qf * qdb).astype(BF16), before_b[c].astype(BF16), preferred_element_type=F32))
            o_ref[r, :] = finish(o, gcat(r))
            if c + 1 < n_chunk:
                state = cf * state + key_state(r, kdf)

    return build_mask, context_part, latent_part


ATT_TQ = 1024
ATT_HPS = 2
ATT_QSCALE = (DIFF_HD ** -0.5) * math.log2(math.e)


N_SIDE = 3


def _mixer_kernel(sc_ref, lg_ref, q_ref, k_ref, v_ref, ckt_ref, cv_ref, cos_ref, sin_ref, g_ref,
                  rq_ref, rk_ref, rv_ref, rg_ref, s0f_ref, s0b_ref, gn_ref, *rest):
    side_in = rest[:N_SIDE]
    o_ref, ro_ref, sf_ref, sb_ref = rest[N_SIDE:N_SIDE + 4]
    side_out = rest[N_SIDE + 4:2 * N_SIDE + 4]
    xs0_ref, cws0_ref, q_scr, k_scr, v_scr, dm_scr = rest[2 * N_SIDE + 4:]
    i = pl.program_id(0)
    rh = pl.program_id(1)
    build_mask, ret_context, ret_latent = _ret_parts(
        lg_ref[0, rh], lg_ref[1, rh], rq_ref, rk_ref, rv_ref, rg_ref, s0f_ref, s0b_ref, gn_ref,
        ro_ref, sf_ref, sb_ref, dm_scr.at[rh])

    @pl.when(i == 0)
    def _():
        build_mask()

    def side_jobs():
        for src, dst in zip(side_in, side_out):
            dst[...] = src[...].astype(BF16)
        xs0_ref[...] = jnp.zeros_like(xs0_ref)
        cws0_ref[...] = jnp.zeros_like(cws0_ref)

    lam = sc_ref[0]
    out_scale = sc_ref[1]
    gain = g_ref[...] * out_scale
    lane = lax.broadcasted_iota(jnp.int32, (1, LANE), 1)
    first = lane < DIFF_HD
    nt = (((1,), (1,)), ((), ()))

    def halves(q):
        zero = jnp.zeros_like(q)
        return jnp.where(first, q, zero), jnp.where(first, zero, q)

    def weights(s):
        return jnp.exp2(s - jnp.max(s, axis=-1, keepdims=True)).astype(BF16)

    def finish(of0, of1):
        o = of0[:, :LANE] / of0[:, LANE:] - lam * (of1[:, :LANE] / of1[:, LANE:])
        return (_rms(o) * gain).astype(BF16)

    @pl.when(i < N_PBLK)
    def _():
        ret_context()
        n_seq = SEQ_BLK // T_P
        rows = [slice(s * T_P, (s + 1) * T_P) for s in range(n_seq)]
        ones = jnp.ones((T_P, LANE), BF16)
        for hh in range(ATT_HPS):
            cols = slice(hh * LANE, (hh + 1) * LANE)
            q0, q1 = halves((q_ref[hh].astype(F32) * ATT_QSCALE).astype(BF16))
            s0 = jnp.concatenate([lax.dot_general(q0[r], k_ref[hh, r, :], nt, preferred_element_type=F32)
                                  for r in rows], axis=0)
            s1 = jnp.concatenate([lax.dot_general(q1[r], k_ref[hh, r, :], nt, preferred_element_type=F32)
                                  for r in rows], axis=0)
            e0 = weights(s0)
            e1 = weights(s1)
            for r in rows:
                v1 = jnp.concatenate([v_ref[hh, r, :], ones], axis=1)
                o_ref[r, cols] = finish(jnp.dot(e0[r], v1, preferred_element_type=F32),
                                        jnp.dot(e1[r], v1, preferred_element_type=F32))

    @pl.when(i >= N_PBLK)
    def _():
        side_jobs()
        ret_latent()
        cos = cos_ref[...]
        sin = sin_ref[...]
        low = (lax.broadcasted_iota(jnp.int32, (T_S, LANE), 1) & 16) == 0

        def rope(x):
            xs = jnp.where(low, pltpu.roll(x, LANE - 16, 1), pltpu.roll(x, 16, 1))
            return x * cos + xs * sin

        for hh in range(ATT_HPS):
            cols = slice(hh * LANE, (hh + 1) * LANE)
            head = pl.program_id(1) * ATT_HPS + hh
            q_scr[hh] = (rope(q_ref[hh].astype(F32)) * ATT_QSCALE).astype(BF16)
            k_scr[hh] = rope(k_ref[hh].astype(F32)).astype(BF16)
            ckt = ckt_ref[cols, :].astype(BF16)
            v_scr[hh, 0:T_S, 0:LANE] = v_ref[hh]
            v_scr[hh, T_S:T_S + PAST, 0:LANE] = cv_ref[:, head, :].astype(BF16)
            v_scr[hh, :, LANE:2 * LANE] = jnp.ones((T_S + PAST, LANE), BF16)

            def scores(qh, hh=hh, ckt=ckt):
                return jnp.concatenate([lax.dot_general(qh, k_scr[hh], nt, preferred_element_type=F32),
                                        jnp.dot(qh, ckt, preferred_element_type=F32)], axis=1)

            for b in range(T_S // ATT_TQ):
                r = slice(b * ATT_TQ, (b + 1) * ATT_TQ)
                q0, q1 = halves(q_scr[hh, r, :])
                e0 = weights(scores(q0))
                e1 = weights(scores(q1))
                o_ref[r, cols] = finish(jnp.dot(e0, v_scr[hh], preferred_element_type=F32),
                                        jnp.dot(e1, v_scr[hh], preferred_element_type=F32))


def _mixers(scal, lg, proj, cache_k, cache_v, cos_t, sin_t, subln_g, s0f, s0b, gnorm, side_weights):
    nb = N_TOK // SEQ_BLK
    spb = SEQ_BLK // T_P
    smp = lambda i: jnp.maximum(i - N_PBLK, 0)
    pmt = lambda i: jnp.minimum(i, N_PBLK - 1)
    pmh = lambda i, h: jnp.where(i < N_PBLK, h, RET_H - 1)
    state_in = pl.BlockSpec((None, None, RET_KD, RET_VD), lambda i, h: (smp(i), h, 0, 0))
    state_out = pl.BlockSpec((spb, None, RET_KD, RET_VD), lambda i, h: (pmt(i), pmh(i, h), 0, 0))

    n_hp = DIFF_H // ATT_HPS
    assert n_hp == RET_H
    n_steps = (nb - N_PBLK) * n_hp
    step = lambda i, h: jnp.maximum((i - N_PBLK) * n_hp + h, 0)
    exp_per_step = N_EXP // n_steps
    exp_slice = lambda i, h: (step(i, h), 0, 0)
    row_slice = lambda i, h: (step(i, h), 0)
    up_spec = pl.BlockSpec((exp_per_step, D, FF), exp_slice)
    down_spec = pl.BlockSpec((exp_per_step, FF, D), exp_slice)
    side_specs = [up_spec, up_spec, down_spec]
    side_shapes = [jax.ShapeDtypeStruct(a.shape, BF16) for a in side_weights]
    zrows = R_MAX // n_steps

    return pl.pallas_call(
        _mixer_kernel,
        out_shape=(jax.ShapeDtypeStruct((N_TOK, DIFF_H * 2 * DIFF_HD), BF16),
                   jax.ShapeDtypeStruct((N_TOK, RET_H * RET_VD), BF16),
                   jax.ShapeDtypeStruct((16, RET_H, RET_KD, RET_VD), F32),
                   jax.ShapeDtypeStruct((16, RET_H, RET_KD, RET_VD), F32),
                   *side_shapes,
                   jax.ShapeDtypeStruct((R_MAX, D), BF16),
                   jax.ShapeDtypeStruct((R_MAX, ROUTE_W), F32)),
        grid=(nb, n_hp),
        in_specs=[pl.BlockSpec(memory_space=pltpu.SMEM),
                  pl.BlockSpec(memory_space=pltpu.SMEM),
                  pl.BlockSpec((ATT_HPS, SEQ_BLK, LANE), lambda i, h: (C_DQ // ATT_HPS + h, i, 0)),
                  pl.BlockSpec((ATT_HPS, SEQ_BLK, LANE), lambda i, h: (C_DK // ATT_HPS + h, i, 0)),
                  pl.BlockSpec((ATT_HPS, SEQ_BLK, LANE), lambda i, h: (C_DV // ATT_HPS + h, i, 0)),
                  pl.BlockSpec((None, ATT_HPS * LANE, PAST), lambda i, h: (smp(i), h, 0)),
                  pl.BlockSpec((None, None, PAST, DIFF_H, LANE), lambda i, h: (smp(i), 0, 0, 0, 0)),
                  pl.BlockSpec((T_S, LANE), lambda i, h: (0, 0)),
                  pl.BlockSpec((T_S, LANE), lambda i, h: (0, 0)),
                  pl.BlockSpec((1, LANE), lambda i, h: (0, 0)),
                  pl.BlockSpec((None, SEQ_BLK, LANE), lambda i, h: (C_RQ + h, i, 0)),
                  pl.BlockSpec((None, SEQ_BLK, LANE), lambda i, h: (C_RK + h, i, 0)),
                  pl.BlockSpec((2, SEQ_BLK, LANE), lambda i, h: (C_RV // 2 + h, i, 0)),
                  pl.BlockSpec((2, SEQ_BLK, LANE), lambda i, h: (C_RG // 2 + h, i, 0)),
                  state_in, state_in,
                  pl.BlockSpec((1, RET_VD), lambda i, h: (0, h)),
                  *side_specs],
        out_specs=(pl.BlockSpec((SEQ_BLK, ATT_HPS * LANE), lambda i, h: (i, h)),
                   pl.BlockSpec((SEQ_BLK, RET_VD), lambda i, h: (i, h)),
                   state_out, state_out,
                   *side_specs,
                   pl.BlockSpec((zrows, D), row_slice),
                   pl.BlockSpec((zrows, ROUTE_W), row_slice)),
        scratch_shapes=[pltpu.VMEM((ATT_HPS, T_S, LANE), BF16),
                        pltpu.VMEM((ATT_HPS, T_S, LANE), BF16),
                        pltpu.VMEM((ATT_HPS, T_S + PAST, 2 * LANE), BF16),
                        pltpu.VMEM((RET_H, T_P, T_P), F32)],
        compiler_params=_cparams(("arbitrary", "arbitrary")),
        name="mixers",
    )(scal, lg, proj, proj, proj, cache_k, cache_v, cos_t, sin_t, subln_g,
      proj, proj, proj, proj, s0f, s0b, gnorm, *side_weights)


def _rope_tables():
    n_rows = T_S // GRID_W
    row = np.repeat(np.arange(n_rows), GRID_W).astype(np.float64)
    col = np.tile(np.arange(GRID_W), n_rows).astype(np.float64)
    n_freq = DIFF_HD // 4
    inv = ROPE_BASE ** (-np.arange(n_freq, dtype=np.float64) / n_freq)

    def axis_tables(pos):
        ang = pos[:, None] * inv[None, :]
        c = np.cos(ang)
        s = np.sin(ang)
        return np.concatenate([c, c], axis=-1), np.concatenate([-s, s], axis=-1)

    cr, sr = axis_tables(row)
    cc, sc = axis_tables(col)
    cos_h = np.concatenate([cr, cc], axis=-1)
    sin_h = np.concatenate([sr, sc], axis=-1)
    return (jnp.asarray(np.concatenate([cos_h, cos_h], axis=-1), F32),
            jnp.asarray(np.concatenate([sin_h, sin_h], axis=-1), F32))


OP_TM = 512
OP_NPT = N_PROMPT // OP_TM


def _outproj_kernel(ra_ref, da_ref, gr_ref, gd_ref, xp_ref, xs_ref, mod_ref, n2_ref,
                    wro32_ref, wdo32_ref, wo32_ref, rgw_ref, rew_ref, brt_ref,
                    x1_ref, h2_ref, cw8_ref, info_ref, infot_ref, off_ref, pctab_ref, steps_ref,
                    m_scr, wro_ref, wdo_ref, wo_ref, wrt_ref, pc_scr):
    i = pl.program_id(0)

    @pl.when(i == 0)
    def _():
        wro_ref[...] = wro32_ref[...].astype(BF16)
        wdo_ref[...] = wdo32_ref[...].astype(BF16)
        wo_ref[...] = wo32_ref[...].astype(BF16)
        gap = jnp.zeros((D, ROUTE_E0 - N_GROUPS), F32)
        pad = jnp.zeros((D, ROUTE_W - ROUTE_E0 - N_EXP), F32)
        wr = jnp.concatenate([rgw_ref[...], gap, rew_ref[...], pad], axis=1)
        wrt_ref[...] = wr.T.astype(BF16)

    ret_out = jnp.dot(ra_ref[...], wro_ref[...], preferred_element_type=F32)
    diff_out = jnp.dot(da_ref[...], wdo_ref[...], preferred_element_type=F32)
    for c in range(D // LANE):
        sl = slice(c * LANE, (c + 1) * LANE)
        m = (jax.nn.sigmoid(gr_ref[c].astype(F32)) * ret_out[:, sl]
             + jax.nn.sigmoid(gd_ref[c].astype(F32)) * diff_out[:, sl])
        m_scr[:, sl] = m.astype(BF16)
    mix = jnp.dot(m_scr[...], wo_ref[...], preferred_element_type=F32)
    mod = mod_ref[...]

    x1 = jnp.where(i < OP_NPT, xp_ref[...], xs_ref[...]) + mod[:, 2 * D:3 * D] * mix
    x1_ref[...] = x1
    h2 = (_rms(x1) * n2_ref[...] * (1.0 + mod[:, 4 * D:5 * D]) + mod[:, 3 * D:4 * D]).astype(BF16)
    h2_ref[...] = h2
    logits_t = lax.dot_general(wrt_ref[...], h2, (((1,), (1,)), ((), ())),
                               preferred_element_type=F32) + brt_ref[...]
    cw8_ref[...], info_ref[...], pc, infot_ref[...] = _route_cols(logits_t)
    pc_scr[pl.ds(i, 1), :] = pc

    @pl.when(i == RT_NT - 1)
    def _():
        off_ref[...], pctab_ref[...], steps_ref[...] = _dispatch_plan(pc_scr[...])


def _outproj(ret_act, diff_act, proj, xp, xs, mod3, n2, wro, wdo, wo, rgw, rew, brt):
    assert OP_TM == RT_TM
    npt = OP_NPT
    full = lambda i: (0, 0)
    once = pl.Buffered(1)
    return pl.pallas_call(
        _outproj_kernel,
        out_shape=(jax.ShapeDtypeStruct((N_TOK, D), F32),
                   jax.ShapeDtypeStruct((N_TOK, D), BF16),
                   jax.ShapeDtypeStruct((N_TOK, ROUTE_W), F32),
                   jax.ShapeDtypeStruct((N_TOK, ROUTE_W), F32),
                   jax.ShapeDtypeStruct((RT_NT, EPG, RT_TM), F32),
                   jax.ShapeDtypeStruct((RT_NT, ROUTE_W), jnp.int32),
                   jax.ShapeDtypeStruct((RT_NT, ROUTE_W), jnp.int32),
                   jax.ShapeDtypeStruct((PLAN_ROWS, ROUTE_W), jnp.int32)),
        grid=(N_TOK // OP_TM,),
        in_specs=[pl.BlockSpec((OP_TM, D), lambda i: (i, 0)),
                  pl.BlockSpec((OP_TM, D), lambda i: (i, 0)),
                  pl.BlockSpec((8, OP_TM, LANE), lambda i: (C_GR // 8, i, 0)),
                  pl.BlockSpec((8, OP_TM, LANE), lambda i: (C_GD // 8, i, 0)),
                  pl.BlockSpec((OP_TM, D), lambda i: (jnp.minimum(i, npt - 1), 0)),
                  pl.BlockSpec((OP_TM, D), lambda i: (jnp.maximum(i - npt, 0), 0)),
                  pl.BlockSpec((None, 1, 6 * D), lambda i: (_mod_row(OP_TM)(i), 0, 0)),
                  pl.BlockSpec((1, D), full),
                  pl.BlockSpec((D, D), full, pipeline_mode=once),
                  pl.BlockSpec((D, D), full, pipeline_mode=once),
                  pl.BlockSpec((D, D), full, pipeline_mode=once),
                  pl.BlockSpec((D, N_GROUPS), full, pipeline_mode=once),
                  pl.BlockSpec((D, N_EXP), full, pipeline_mode=once),
                  pl.BlockSpec((ROUTE_W, 1), full)],
        out_specs=(pl.BlockSpec((OP_TM, D), lambda i: (i, 0)),
                   pl.BlockSpec((OP_TM, D), lambda i: (i, 0)),
                   pl.BlockSpec((OP_TM, ROUTE_W), lambda i: (i, 0)),
                   pl.BlockSpec((OP_TM, ROUTE_W), lambda i: (i, 0)),
                   pl.BlockSpec((None, EPG, RT_TM), lambda i: (i, 0, 0)),
                   pl.BlockSpec((RT_NT, ROUTE_W), full),
                   pl.BlockSpec((RT_NT, ROUTE_W), full),
                   pl.BlockSpec((PLAN_ROWS, ROUTE_W), full)),
        scratch_shapes=[pltpu.VMEM((OP_TM, D), BF16),
                        pltpu.VMEM((D, D), BF16), pltpu.VMEM((D, D), BF16), pltpu.VMEM((D, D), BF16),
                        pltpu.VMEM((ROUTE_W, D), BF16),
                        pltpu.VMEM((RT_NT, ROUTE_W), F32)],
        compiler_params=_cparams(("arbitrary",)),
        name="outproj",
    )(ret_act, diff_act, proj, proj, xp, xs, mod3, n2, wro, wdo, wo, rgw, rew, brt)


RT_TM = 512
RT_NT = N_TOK // RT_TM
PIECE = 16
R_LOC = RT_TM + N_GROUPS * PIECE
R_STAGE = 640
EX_TM = 512
R_MAX = 11264
EX_NT = R_MAX // EX_TM


ROUTE_E0 = EPG


def _route_cols(lgt):
    row = lax.broadcasted_iota(jnp.int32, (EPG, RT_TM), 0)
    neg = jnp.float32(-jnp.inf)

    def first_row(cond):
        return jnp.min(jnp.where(cond, row, EPG), axis=0, keepdims=True)

    head = lgt[0:EPG]
    is_g = row < N_GROUPS
    gl = jnp.where(is_g, head, neg)
    gmax = jnp.max(gl, axis=0, keepdims=True)
    gsum = jnp.sum(jnp.where(is_g, jnp.exp(head - gmax), 0.0), axis=0, keepdims=True)
    p_top = 1.0 / gsum
    g_idx = first_row(gl == gmax)
    el = lgt[ROUTE_E0 + (N_GROUPS - 1) * EPG:ROUTE_E0 + N_GROUPS * EPG]
    for g in reversed(range(N_GROUPS - 1)):
        el = jnp.where(g_idx == g, lgt[ROUTE_E0 + g * EPG:ROUTE_E0 + (g + 1) * EPG], el)
    ee = jnp.exp(el - jnp.max(el, axis=0, keepdims=True))
    ep = ee / jnp.sum(ee, axis=0, keepdims=True)
    e1 = jnp.max(ep, axis=0, keepdims=True)
    i1 = first_row(ep == e1)
    ep2 = jnp.where(row == i1, -1.0, ep)
    e2 = jnp.max(ep2, axis=0, keepdims=True)
    i2 = first_row(ep2 == e2)
    den = e1 + e2
    cw8_t = (jnp.where(row == i1, p_top * e1 / den, 0.0)
             + jnp.where(row == i2, p_top * e2 / den, 0.0))

    onehot = (row == g_idx).astype(F32)
    ii = lax.broadcasted_iota(jnp.int32, (RT_TM, RT_TM), 0)
    jj = lax.broadcasted_iota(jnp.int32, (RT_TM, RT_TM), 1)
    earlier = (ii < jj).astype(BF16)
    prefix = jnp.dot(onehot.astype(BF16), earlier, preferred_element_type=F32)
    cnt = jnp.sum(onehot, axis=1, keepdims=True)
    pc_col = jnp.floor((cnt + (PIECE - 1.0)) * (1.0 / PIECE)) * PIECE
    row1 = lax.broadcasted_iota(jnp.int32, (EPG, 1), 0)
    lane1 = lax.broadcasted_iota(jnp.int32, (1, ROUTE_W), 1)
    lo = jnp.zeros((EPG, 1), F32)
    run = jnp.zeros((1, 1), F32)
    pc = jnp.zeros((1, ROUTE_W), F32)
    for g in range(N_GROUPS):
        pc_g = jnp.sum(jnp.where(row1 == g, pc_col, 0.0), axis=0, keepdims=True)
        lo = jnp.where(row1 == g, run, lo)
        pc = jnp.where(lane1 == g, pc_g, pc)
        run = run + pc_g
    dest = jnp.sum(onehot * (prefix + lo), axis=0, keepdims=True)
    info_t = jnp.where(row == 0, g_idx.astype(F32), jnp.where(row == 1, dest, 0.0))

    slab = jnp.concatenate([cw8_t, info_t, jnp.zeros((LANE - 2 * EPG, RT_TM), F32)], axis=0)
    cols = slab.T
    lane = lax.broadcasted_iota(jnp.int32, cols.shape, 1)
    cw8 = jnp.where(lane < EPG, cols, 0.0)
    info = jnp.where(lane < 2, pltpu.roll(cols, LANE - EPG, 1), 0.0)
    return cw8, info, pc, info_t


PLAN_ROWS = 32


def _dispatch_plan(pc_all):
    lane = lax.broadcasted_iota(jnp.int32, (1, ROUTE_W), 1)
    is_g = lane < N_GROUPS
    seg_len = jnp.sum(pc_all, axis=0, keepdims=True)
    seg_pad = jnp.floor((seg_len + (EX_TM - 1.0)) * (1.0 / EX_TM)) * EX_TM
    run = seg_pad + pltpu.roll(seg_pad, 1, 1)
    seg_end_pad = jnp.where(is_g, run + pltpu.roll(run, 2, 1), 0.0)
    seg_start = seg_end_pad - seg_pad
    ti = lax.broadcasted_iota(jnp.int32, (RT_NT, RT_NT), 0)
    tj = lax.broadcasted_iota(jnp.int32, (RT_NT, RT_NT), 1)
    earlier = (tj < ti).astype(BF16)
    within = jnp.dot(earlier, pc_all.astype(BF16), preferred_element_type=F32)
    chunk_off = jnp.where(is_g, seg_start + within, 0.0)

    start = (lax.broadcasted_iota(jnp.int32, (PLAN_ROWS, 1), 0) * EX_TM).astype(F32)
    passed = jnp.where(jnp.logical_and(is_g, start >= seg_end_pad), 1.0, 0.0)
    group = jnp.minimum(jnp.sum(passed, axis=1, keepdims=True), N_GROUPS - 1.0)
    used_end = jnp.sum(jnp.where(lane == group.astype(jnp.int32), seg_start + seg_len, 0.0),
                       axis=1, keepdims=True)
    used = jnp.clip(used_end - start, 0.0, float(EX_TM))
    steps = jnp.where(lane == 0, group, jnp.where(lane == 1, used, 0.0))
    return chunk_off.astype(jnp.int32), pc_all.astype(jnp.int32), steps.astype(jnp.int32)


def _piece_copies(off_ref, pc_ref, tile, make):
    lo = 0
    for g in range(N_GROUPS):
        n = pc_ref[tile, g] // PIECE
        base = off_ref[tile, g]

        def body(j, carry, lo=lo, base=base):
            make(pl.multiple_of(lo + j * PIECE, PIECE), pl.multiple_of(base + j * PIECE, PIECE))
            return carry

        lax.fori_loop(0, n, body, 0)
        lo = lo + pc_ref[tile, g]


def _piece_count(pc_ref, tile):
    n = 0
    for g in range(N_GROUPS):
        n = n + pc_ref[tile, g] // PIECE
    return n


def _dispatch_kernel(off_ref, pc_ref, h_ref, infot_ref, cw8_ref, xs_in, cws_in,
                     xs_out, cws_out, x_scr, c_scr, sem):
    del xs_in, cws_in
    i = pl.program_id(0)
    slot = i % 2
    dest = infot_ref[1:2, :]
    row = lax.broadcasted_iota(jnp.int32, (R_LOC, RT_TM), 0).astype(F32)
    sel = row == dest
    sel = sel.astype(BF16)
    x_scr[slot] = jnp.dot(sel, h_ref[...], preferred_element_type=F32).astype(BF16)
    cw = cw8_ref[...]
    hi = cw.astype(BF16).astype(F32)
    mid = (cw - hi).astype(BF16).astype(F32)
    low = (cw - hi - mid).astype(BF16).astype(F32)
    pieces = (hi + pltpu.roll(mid, EPG, 1) + pltpu.roll(low, 2 * EPG, 1)).astype(BF16)
    c_scr[slot] = jnp.dot(sel, pieces, preferred_element_type=F32)

    def x_copy(s, src, dst):
        return pltpu.make_async_copy(x_scr.at[s, pl.ds(src, PIECE)], xs_out.at[pl.ds(dst, PIECE)],
                                     sem.at[0, s])

    def c_copy(s, src, dst):
        return pltpu.make_async_copy(c_scr.at[s, pl.ds(src, PIECE)], cws_out.at[pl.ds(dst, PIECE)],
                                     sem.at[1, s])

    def start(src, dst):
        x_copy(slot, src, dst).start()
        c_copy(slot, src, dst).start()

    _piece_copies(off_ref, pc_ref, i, start)

    def wait_tile(tile, s):
        def wait(j, carry):
            x_copy(s, 0, 0).wait()
            c_copy(s, 0, 0).wait()
            return carry

        lax.fori_loop(0, _piece_count(pc_ref, tile), wait, 0)

    @pl.when(i > 0)
    def _():
        wait_tile(i - 1, 1 - slot)

    @pl.when(i == RT_NT - 1)
    def _():
        wait_tile(i, slot)


def _dispatch(chunk_off, pc, h2, info_t, cw8, xs0, cws0):
    grid_spec = pltpu.PrefetchScalarGridSpec(
        num_scalar_prefetch=2,
        grid=(RT_NT,),
        in_specs=[pl.BlockSpec((RT_TM, D), lambda i, o, p: (i, 0)),
                  pl.BlockSpec((None, EPG, RT_TM), lambda i, o, p: (i, 0, 0)),
                  pl.BlockSpec((RT_TM, ROUTE_W), lambda i, o, p: (i, 0)),
                  pl.BlockSpec(memory_space=pl.ANY),
                  pl.BlockSpec(memory_space=pl.ANY)],
        out_specs=(pl.BlockSpec(memory_space=pl.ANY), pl.BlockSpec(memory_space=pl.ANY)),
        scratch_shapes=[pltpu.VMEM((2, R_LOC, D), BF16),
                        pltpu.VMEM((2, R_LOC, ROUTE_W), F32),
                        pltpu.SemaphoreType.DMA((2, 2))])
    return pl.pallas_call(
        _dispatch_kernel,
        out_shape=(jax.ShapeDtypeStruct((R_MAX, D), BF16),
                   jax.ShapeDtypeStruct((R_MAX, ROUTE_W), F32)),
        grid_spec=grid_spec,
        input_output_aliases={5: 0, 6: 1},
        compiler_params=_cparams(("arbitrary",)),
        name="dispatch",
    )(chunk_off, pc, h2, info_t, cw8, xs0, cws0)


def _expert_kernel(steps_ref, x_ref, cw_ref, wg_ref, wu_ref, wd_ref, y_ref):
    k = pl.program_id(0)
    used = steps_ref[k, 1]
    half = EX_TM // 2

    def run(rows):
        x = x_ref[rows, :]
        cw = cw_ref[rows, :]
        lane = lax.broadcasted_iota(jnp.int32, cw.shape, 1)
        acc = jnp.zeros(x.shape, F32)
        for j in range(EPG):
            mine = jnp.logical_and((lane & (EPG - 1)) == j, lane < 3 * EPG)
            w = jnp.sum(jnp.where(mine, cw, 0.0), axis=-1, keepdims=True)
            a = (jax.nn.silu(jnp.dot(x, wg_ref[j], preferred_element_type=F32))
                 * jnp.dot(x, wu_ref[j], preferred_element_type=F32))
            acc = acc + jnp.dot((a * w).astype(BF16), wd_ref[j], preferred_element_type=F32)
        y_ref[rows, :] = acc.astype(BF16)

    @pl.when(used > half)
    def _():
        run(slice(0, EX_TM))

    @pl.when(jnp.logical_and(used > 0, used <= half))
    def _():
        run(slice(0, half))
        y_ref[half:EX_TM, :] = jnp.zeros((EX_TM - half, D), BF16)

    @pl.when(used == 0)
    def _():
        y_ref[...] = jnp.zeros_like(y_ref)


def _experts(steps, xs, cws, wg, wu, wd):
    grid_spec = pltpu.PrefetchScalarGridSpec(
        num_scalar_prefetch=1,
        grid=(EX_NT,),
        in_specs=[pl.BlockSpec((EX_TM, D), lambda k, st: (k, 0)),
                  pl.BlockSpec((EX_TM, ROUTE_W), lambda k, st: (k, 0)),
                  pl.BlockSpec((EPG, D, FF), lambda k, st: (st[k, 0], 0, 0)),
                  pl.BlockSpec((EPG, D, FF), lambda k, st: (st[k, 0], 0, 0)),
                  pl.BlockSpec((EPG, FF, D), lambda k, st: (st[k, 0], 0, 0))],
        out_specs=pl.BlockSpec((EX_TM, D), lambda k, st: (k, 0)))
    return pl.pallas_call(
        _expert_kernel,
        out_shape=jax.ShapeDtypeStruct((R_MAX, D), BF16),
        grid_spec=grid_spec,
        compiler_params=_cparams(("arbitrary",)),
        name="experts",
    )(steps, xs, cws, wg, wu, wd)


CB_NPT = N_PROMPT // RT_TM


def _combine_kernel(off_ref, pc_ref, info_ref, x1_ref, mod_ref, g_ref, ys_hbm,
                    yp_ref, ysm_ref, stage, sem):
    i = pl.program_id(0)
    slot = i % 2

    def copy(s, dst, src):
        return pltpu.make_async_copy(ys_hbm.at[pl.ds(src, PIECE)], stage.at[s, pl.ds(dst, PIECE)],
                                     sem.at[s])

    def fetch(tile, s):
        _piece_copies(off_ref, pc_ref, tile, lambda loc, glob: copy(s, loc, glob).start())

    @pl.when(i == 0)
    def _():
        stage[...] = jnp.zeros_like(stage)
        fetch(0, 0)

    @pl.when(i + 1 < RT_NT)
    def _():
        fetch(i + 1, 1 - slot)

    def wait(j, carry):
        copy(slot, 0, 0).wait()
        return carry

    lax.fori_loop(0, _piece_count(pc_ref, i), wait, 0)

    dest = info_ref[...][:, 1:2]
    col = lax.broadcasted_iota(jnp.int32, (RT_TM, R_STAGE), 1).astype(F32)
    moe = jnp.dot((col == dest).astype(BF16), stage[slot], preferred_element_type=F32)
    mod = mod_ref[...]
    out = _rms(x1_ref[...] + mod[:, 5 * D:6 * D] * moe) * g_ref[...]

    @pl.when(i < CB_NPT)
    def _():
        yp_ref[...] = out

    @pl.when(i >= CB_NPT)
    def _():
        ysm_ref[...] = out


def _combine(chunk_off, pc, info, x1, mod3, fg, ys):
    npt = CB_NPT
    grid_spec = pltpu.PrefetchScalarGridSpec(
        num_scalar_prefetch=2,
        grid=(RT_NT,),
        in_specs=[pl.BlockSpec((RT_TM, ROUTE_W), lambda i, o, p: (i, 0)),
                  pl.BlockSpec((RT_TM, D), lambda i, o, p: (i, 0)),
                  pl.BlockSpec((None, 1, 6 * D), lambda i, o, p: (_mod_row(RT_TM)(i), 0, 0)),
                  pl.BlockSpec((1, D), lambda i, o, p: (0, 0)),
                  pl.BlockSpec(memory_space=pl.ANY)],
        out_specs=(pl.BlockSpec((RT_TM, D), lambda i, o, p: (jnp.minimum(i, npt - 1), 0)),
                   pl.BlockSpec((RT_TM, D), lambda i, o, p: (jnp.maximum(i - npt, 0), 0))),
        scratch_shapes=[pltpu.VMEM((2, R_STAGE, D), BF16),
                        pltpu.SemaphoreType.DMA((2,))])
    return pl.pallas_call(
        _combine_kernel,
        out_shape=(jax.ShapeDtypeStruct((N_PROMPT, D), F32),
                   jax.ShapeDtypeStruct((N_SAMPLE, D), F32)),
        grid_spec=grid_spec,
        compiler_params=_cparams(("arbitrary",)),
        name="combine",
    )(chunk_off, pc, info, x1, mod3, fg, ys)


def kernel(x_prompt, x_sample, cache_diff_k, cache_diff_v, state_ret_fwd, state_ret_bwd, c, c_ctx,
           w_ada, b_ada, norm1_g, norm2_g, w_in, ret_decay_fwd, ret_decay_bwd, ret_norm_g,
           diff_lambda_q1, diff_lambda_k1, diff_lambda_q2, diff_lambda_k2, diff_subln_g,
           w_ret_o, w_diff_o, w_o, router_group_w, router_group_b, router_expert_w, router_expert_b,
           moe_w_gate, moe_w_up, moe_w_down, final_norm_g):
    l = 0
    lam_init = 0.8 - 0.6 * math.exp(-0.3 * l)
    lam = (jnp.exp(jnp.sum(diff_lambda_q1[l].astype(F32) * diff_lambda_k1[l].astype(F32)))
           - jnp.exp(jnp.sum(diff_lambda_q2[l].astype(F32) * diff_lambda_k2[l].astype(F32))) + lam_init)
    attn_scal = jnp.stack([lam, jnp.float32(1.0 - lam_init)]).astype(F32)
    lg = jnp.stack([jax.nn.log_sigmoid(ret_decay_fwd[l].astype(F32)),
                    jax.nn.log_sigmoid(ret_decay_bwd[l].astype(F32))])

    xp = x_prompt.reshape(N_PROMPT, D)
    xs = x_sample.reshape(N_SAMPLE, D)
    mod3 = _modulation(c_ctx[None, :], c, w_ada[l], b_ada[l][None, :])

    proj, kt32, v32 = _inproj(xp, xs, mod3, norm1_g[l][None, :], w_in[l])

    cos_t, sin_t = _rope_tables()
    cache_kt = jnp.transpose(cache_diff_k[:, l], (0, 2, 3, 4, 1)).reshape(4, D, PAST)
    (diff_act, ret_act, s_f, s_b, wg_bf, wu_bf, wd_bf, xs0, cws0) = _mixers(
        attn_scal, lg, proj, cache_kt, cache_diff_v, cos_t, sin_t, diff_subln_g[l][None, :],
        state_ret_fwd[:, l], state_ret_bwd[:, l], ret_norm_g[l][None, :],
        (moe_w_gate[l], moe_w_up[l], moe_w_down[l]))

    brt = jnp.concatenate([router_group_b[l], jnp.zeros((ROUTE_E0 - N_GROUPS,), F32), router_expert_b[l],
                           jnp.zeros((ROUTE_W - ROUTE_E0 - N_EXP,), F32)])[:, None]
    x1, h2, cw8, info, info_t, chunk_off, pc, steps = _outproj(
        ret_act, diff_act, proj, xp, xs, mod3, norm2_g[l][None, :],
        w_ret_o[l], w_diff_o[l], w_o[l], router_group_w[l], router_expert_w[l], brt)
    xs_sorted, cw_sorted = _dispatch(chunk_off, pc, h2, info_t, cw8, xs0, cws0)
    y_sorted = _experts(steps, xs_sorted, cw_sorted, wg_bf, wu_bf, wd_bf)
    yp, ys = _combine(chunk_off, pc, info, x1, mod3, final_norm_g[None, :], y_sorted)

    return (yp.reshape(16, T_P, D), ys.reshape(4, T_S, D),
            jnp.transpose(kt32.reshape(16, DIFF_H, 2, DIFF_HD, T_P), (0, 4, 1, 2, 3))[:, None],
            v32.reshape(16, 1, T_P, DIFF_H, 2 * DIFF_HD),
            s_f.reshape(16, 1, RET_H, RET_KD, RET_VD), s_b.reshape(16, 1, RET_H, RET_KD, RET_VD))
```

```python
import functools
import math

import jax
import jax.numpy as jnp
import numpy as np
from jax import lax
from jax.experimental import pallas as pl
from jax.experimental.pallas import tpu as pltpu

F32 = jnp.float32
BF16 = jnp.bfloat16

D = 1024
N_PROMPT = 16 * 256
N_SAMPLE = 4 * 1024
N_TOK = N_PROMPT + N_SAMPLE
T_P = 256
T_S = 1024
PAST = 512
GRID_W = 64
RET_H = 4
RET_KD = 128
RET_VD = 256
DIFF_H = 8
DIFF_HD = 64
ROPE_BASE = 10000.0
N_GROUPS = 4
EPG = 8
N_EXP = 32
FF = 256
EPS = 1e-6
IN_W = 8192
LANE = 128
N_CHUNK = IN_W // LANE
C_RQ, C_RK, C_RV, C_RG, C_DQ, C_DK, C_DV, C_GR, C_GD = 0, 4, 8, 16, 24, 32, 40, 48, 56
ROUTE_W = 128
SEQ_BLK = 1024
N_PBLK = N_PROMPT // SEQ_BLK
VMEM_LIMIT = 56 * 1024 * 1024


def _cparams(sem):
    return pltpu.CompilerParams(dimension_semantics=sem, vmem_limit_bytes=VMEM_LIMIT)


def _mod_row(tile_rows):
    def f(i):
        start = i * tile_rows
        return jnp.where(start < N_PROMPT, 0, 1 + (start - N_PROMPT) // T_S)
    return f


def _rms(x):
    return x * lax.rsqrt(jnp.mean(x * x, axis=-1, keepdims=True) + EPS)


MOD_ROWS = 8


def _mod_kernel(ctx_ref, c_ref, w_ref, b_ref, o_ref):
    cond = jnp.concatenate([ctx_ref[...], c_ref[...],
                            jnp.zeros((MOD_ROWS - 1 - c_ref.shape[0], D), F32)], axis=0)
    s = jax.nn.silu(cond)
    out = jnp.dot(s.astype(BF16), w_ref[...].astype(BF16), preferred_element_type=F32) + b_ref[...]
    for r in range(MOD_ROWS):
        o_ref[r] = out[r:r + 1, :]


def _modulation(c_ctx, c, w_ada, b_ada):
    tn = 1536
    return pl.pallas_call(
        _mod_kernel,
        out_shape=jax.ShapeDtypeStruct((MOD_ROWS, 1, 6 * D), F32),
        grid=(6 * D // tn,),
        in_specs=[pl.BlockSpec((1, D), lambda j: (0, 0)),
                  pl.BlockSpec(c.shape, lambda j: (0, 0)),
                  pl.BlockSpec((D, tn), lambda j: (0, j)),
                  pl.BlockSpec((1, tn), lambda j: (0, j))],
        out_specs=pl.BlockSpec((MOD_ROWS, 1, tn), lambda j: (0, 0, j)),
        compiler_params=_cparams(("arbitrary",)),
        name="mod",
    )(c_ctx, c, w_ada, b_ada)


IP_TM = 512
IP_TN = 2048
IP_NPT = N_PROMPT // IP_TM


IP_KV_TILE = C_DK * LANE // IP_TN
IP_SPT = IP_TM // T_P


def _inproj_kernel(xp_ref, xs_ref, mod_ref, n1_ref, w_ref, proj_ref, kt_ref, v32_ref, w_scr):
    j = pl.program_id(0)
    i = pl.program_id(1)

    @pl.when(i == 0)
    def _():
        w_scr[...] = w_ref[...].astype(BF16)

    x = jnp.where(i < IP_NPT, xp_ref[...], xs_ref[...])
    mod = mod_ref[...]
    h = (_rms(x) * n1_ref[...] * (1.0 + mod[:, D:2 * D]) + mod[:, 0:D]).astype(BF16)
    acc = jnp.dot(h, w_scr[...], preferred_element_type=F32)
    for c in range(IP_TN // LANE):
        proj_ref[c] = acc[:, c * LANE:(c + 1) * LANE].astype(BF16)

    @pl.when(jnp.logical_and(j == IP_KV_TILE, i < IP_NPT))
    def _():
        for s in range(IP_SPT):
            kt_ref[s] = acc[s * T_P:(s + 1) * T_P, :D].T
        v32_ref[...] = acc[:, D:]


def _inproj(xp, xs, mod3, n1, w_in):
    npt = IP_NPT
    cpt = IP_TN // LANE

    def kv_tile(j, i):
        return jnp.where(j < IP_KV_TILE, 0, jnp.where(j == IP_KV_TILE, jnp.minimum(i, npt - 1), npt - 1))

    return pl.pallas_call(
        _inproj_kernel,
        out_shape=(jax.ShapeDtypeStruct((N_CHUNK, N_TOK, LANE), BF16),
                   jax.ShapeDtypeStruct((N_PROMPT // T_P, D, T_P), F32),
                   jax.ShapeDtypeStruct((N_PROMPT, D), F32)),
        grid=(IN_W // IP_TN, N_TOK // IP_TM),
        in_specs=[pl.BlockSpec((IP_TM, D), lambda j, i: (jnp.minimum(i, npt - 1), 0)),
                  pl.BlockSpec((IP_TM, D), lambda j, i: (jnp.maximum(i - npt, 0), 0)),
                  pl.BlockSpec((None, 1, 6 * D), lambda j, i: (_mod_row(IP_TM)(i), 0, 0)),
                  pl.BlockSpec((1, D), lambda j, i: (0, 0)),
                  pl.BlockSpec((D, IP_TN), lambda j, i: (0, j))],
        out_specs=(pl.BlockSpec((cpt, IP_TM, LANE), lambda j, i: (j, i, 0)),
                   pl.BlockSpec((IP_SPT, D, T_P), lambda j, i: (kv_tile(j, i), 0, 0)),
                   pl.BlockSpec((IP_TM, D), lambda j, i: (kv_tile(j, i), 0))),
        scratch_shapes=[pltpu.VMEM((D, IP_TN), BF16)],
        compiler_params=_cparams(("arbitrary", "arbitrary")),
        name="inproj",
    )(xp, xs, mod3, n1, w_in)


def _decay_mask(t, lgf, lgb):
    ii = lax.broadcasted_iota(jnp.int32, (t, t), 0)
    jj = lax.broadcasted_iota(jnp.int32, (t, t), 1)
    rel = (ii - jj).astype(F32)
    e = jnp.exp(jnp.where(rel >= 0.0, lgf, -lgb) * rel)
    return jnp.where(rel == 0.0, 2.0, e) * (RET_KD ** -0.5)


def _ret_parts(lgf, lgb, q_ref, k_ref, v_ref, rg_ref, s0f_ref, s0b_ref, g_ref,
               o_ref, sf_ref, sb_ref, dm_ref):
    gain = g_ref[...]
    nt = (((1,), (1,)), ((), ()))
    tn = (((0,), (0,)), ((), ()))

    def finish(o, rg):
        d = o - jnp.mean(o, axis=-1, keepdims=True)
        y = d * lax.rsqrt(jnp.mean(d * d, axis=-1, keepdims=True) + EPS) * gain
        return (jax.nn.silu(rg.astype(F32)) * y).astype(BF16)

    def vcat(r):
        return jnp.concatenate([v_ref[0, r, :], v_ref[1, r, :]], axis=1)

    def gcat(r):
        return jnp.concatenate([rg_ref[0, r, :], rg_ref[1, r, :]], axis=1)

    def build_mask():
        dm_ref[...] = _decay_mask(T_P, lgf, lgb)

    n_chunk = SEQ_BLK // T_P
    chunks = [slice(c * T_P, (c + 1) * T_P) for c in range(n_chunk)]
    t = lax.broadcasted_iota(jnp.int32, (T_P, 1), 0).astype(F32)
    kdf = jnp.exp(lgf * (T_P - 1.0 - t)) * (RET_KD ** -0.5)
    kdb = jnp.exp(lgb * t) * (RET_KD ** -0.5)

    def intra(r):
        sc = lax.dot_general(q_ref[r, :], k_ref[r, :], nt, preferred_element_type=F32)
        return jnp.dot((sc * dm_ref[...]).astype(BF16), vcat(r), preferred_element_type=F32)

    def key_state(r, kd):
        kw = (k_ref[r, :].astype(F32) * kd).astype(BF16)
        return lax.dot_general(kw, vcat(r), tn, preferred_element_type=F32)

    def context_part():
        for s, r in enumerate(chunks):
            o_ref[r, :] = finish(intra(r), gcat(r))
            sf_ref[s] = key_state(r, kdf)
            sb_ref[s] = key_state(r, kdb)

    def latent_part():
        qdf = jnp.exp(lgf * (t + 1.0))
        qdb = jnp.exp(lgb * (T_P - t))
        span = jnp.full((1, 1), float(T_P), F32)
        cf = jnp.exp(lgf * span)
        cb = jnp.exp(lgb * span)
        before_b = [None] * n_chunk
        state = s0b_ref[...]
        for c in reversed(range(n_chunk)):
            before_b[c] = state
            if c > 0:
                state = cb * state + key_state(chunks[c], kdb)
        state = s0f_ref[...]
        for c, r in enumerate(chunks):
            qf = q_ref[r, :].astype(F32)
            o = (intra(r)
                 + jnp.dot((qf * qdf).astype(BF16), state.astype(BF16), preferred_element_type=F32)
                 + jnp.dot((qf * qdb).astype(BF16), before_b[c].astype(BF16), preferred_element_type=F32))
            o_ref[r, :] = finish(o, gcat(r))
            if c + 1 < n_chunk:
                state = cf * state + key_state(r, kdf)

    return build_mask, context_part, latent_part


ATT_TQ = 1024
ATT_HPS = 2
ATT_QSCALE = (DIFF_HD ** -0.5) * math.log2(math.e)


N_SIDE = 3


def _mixer_kernel(sc_ref, lg_ref, q_ref, k_ref, v_ref, ckt_ref, cv_ref, cos_ref, sin_ref, g_ref,
                  rq_ref, rk_ref, rv_ref, rg_ref, s0f_ref, s0b_ref, gn_ref, *rest):
    side_in = rest[:N_SIDE]
    o_ref, ro_ref, sf_ref, sb_ref = rest[N_SIDE:N_SIDE + 4]
    side_out = rest[N_SIDE + 4:2 * N_SIDE + 4]
    xs0_ref, q_scr, k_scr, v_scr, dm_scr = rest[2 * N_SIDE + 4:]
    i = pl.program_id(0)
    rh = pl.program_id(1)
    build_mask, ret_context, ret_latent = _ret_parts(
        lg_ref[0, rh], lg_ref[1, rh], rq_ref, rk_ref, rv_ref, rg_ref, s0f_ref, s0b_ref, gn_ref,
        ro_ref, sf_ref, sb_ref, dm_scr.at[rh])

    @pl.when(i == 0)
    def _():
        build_mask()

    def side_jobs():
        for src, dst in zip(side_in, side_out):
            dst[...] = src[...].astype(BF16)
        xs0_ref[...] = jnp.zeros_like(xs0_ref)

    lam = sc_ref[0]
    out_scale = sc_ref[1]
    gain = g_ref[...] * out_scale
    lane = lax.broadcasted_iota(jnp.int32, (1, LANE), 1)
    first = lane < DIFF_HD
    nt = (((1,), (1,)), ((), ()))

    def halves(q):
        zero = jnp.zeros_like(q)
        return jnp.where(first, q, zero), jnp.where(first, zero, q)

    def weights(s):
        return jnp.exp2(s - jnp.max(s, axis=-1, keepdims=True)).astype(BF16)

    def finish(of0, of1):
        o = of0[:, :LANE] / of0[:, LANE:] - lam * (of1[:, :LANE] / of1[:, LANE:])
        return (_rms(o) * gain).astype(BF16)

    @pl.when(i < N_PBLK)
    def _():
        ret_context()
        n_seq = SEQ_BLK // T_P
        rows = [slice(s * T_P, (s + 1) * T_P) for s in range(n_seq)]
        ones = jnp.ones((T_P, LANE), BF16)
        for hh in range(ATT_HPS):
            cols = slice(hh * LANE, (hh + 1) * LANE)
            q0, q1 = halves((q_ref[hh].astype(F32) * ATT_QSCALE).astype(BF16))
            s0 = jnp.concatenate([lax.dot_general(q0[r], k_ref[hh, r, :], nt, preferred_element_type=F32)
                                  for r in rows], axis=0)
            s1 = jnp.concatenate([lax.dot_general(q1[r], k_ref[hh, r, :], nt, preferred_element_type=F32)
                                  for r in rows], axis=0)
            e0 = weights(s0)
            e1 = weights(s1)
            for r in rows:
                v1 = jnp.concatenate([v_ref[hh, r, :], ones], axis=1)
                o_ref[r, cols] = finish(jnp.dot(e0[r], v1, preferred_element_type=F32),
                                        jnp.dot(e1[r], v1, preferred_element_type=F32))

    @pl.when(i >= N_PBLK)
    def _():
        side_jobs()
        ret_latent()
        cos = cos_ref[...]
        sin = sin_ref[...]
        low = (lax.broadcasted_iota(jnp.int32, (T_S, LANE), 1) & 16) == 0

        def rope(x):
            xs = jnp.where(low, pltpu.roll(x, LANE - 16, 1), pltpu.roll(x, 16, 1))
            return x * cos + xs * sin

        for hh in range(ATT_HPS):
            cols = slice(hh * LANE, (hh + 1) * LANE)
            head = pl.program_id(1) * ATT_HPS + hh
            q_scr[hh] = (rope(q_ref[hh].astype(F32)) * ATT_QSCALE).astype(BF16)
            k_scr[hh] = rope(k_ref[hh].astype(F32)).astype(BF16)
            ckt = ckt_ref[cols, :].astype(BF16)
            v_scr[hh, 0:T_S, 0:LANE] = v_ref[hh]
            v_scr[hh, T_S:T_S + PAST, 0:LANE] = cv_ref[:, head, :].astype(BF16)
            v_scr[hh, :, LANE:2 * LANE] = jnp.ones((T_S + PAST, LANE), BF16)

            def scores(qh, hh=hh, ckt=ckt):
                return jnp.concatenate([lax.dot_general(qh, k_scr[hh], nt, preferred_element_type=F32),
                                        jnp.dot(qh, ckt, preferred_element_type=F32)], axis=1)

            for b in range(T_S // ATT_TQ):
                r = slice(b * ATT_TQ, (b + 1) * ATT_TQ)
                q0, q1 = halves(q_scr[hh, r, :])
                e0 = weights(scores(q0))
                e1 = weights(scores(q1))
                o_ref[r, cols] = finish(jnp.dot(e0, v_scr[hh], preferred_element_type=F32),
                                        jnp.dot(e1, v_scr[hh], preferred_element_type=F32))


def _mixers(scal, lg, proj, cache_k, cache_v, cos_t, sin_t, subln_g, s0f, s0b, gnorm, side_weights):
    nb = N_TOK // SEQ_BLK
    spb = SEQ_BLK // T_P
    smp = lambda i: jnp.maximum(i - N_PBLK, 0)
    pmt = lambda i: jnp.minimum(i, N_PBLK - 1)
    pmh = lambda i, h: jnp.where(i < N_PBLK, h, RET_H - 1)
    state_in = pl.BlockSpec((None, None, RET_KD, RET_VD), lambda i, h: (smp(i), h, 0, 0))
    state_out = pl.BlockSpec((spb, None, RET_KD, RET_VD), lambda i, h: (pmt(i), pmh(i, h), 0, 0))

    n_hp = DIFF_H // ATT_HPS
    assert n_hp == RET_H
    n_steps = (nb - N_PBLK) * n_hp
    step = lambda i, h: jnp.maximum((i - N_PBLK) * n_hp + h, 0)
    exp_per_step = N_EXP // n_steps
    exp_slice = lambda i, h: (step(i, h), 0, 0)
    row_slice = lambda i, h: (step(i, h), 0)
    up_spec = pl.BlockSpec((exp_per_step, D, FF), exp_slice)
    down_spec = pl.BlockSpec((exp_per_step, FF, D), exp_slice)
    side_specs = [up_spec, up_spec, down_spec]
    side_shapes = [jax.ShapeDtypeStruct(a.shape, BF16) for a in side_weights]
    zrows = R_MAX // n_steps

    return pl.pallas_call(
        _mixer_kernel,
        out_shape=(jax.ShapeDtypeStruct((N_TOK, DIFF_H * 2 * DIFF_HD), BF16),
                   jax.ShapeDtypeStruct((N_TOK, RET_H * RET_VD), BF16),
                   jax.ShapeDtypeStruct((16, RET_H, RET_KD, RET_VD), F32),
                   jax.ShapeDtypeStruct((16, RET_H, RET_KD, RET_VD), F32),
                   *side_shapes,
                   jax.ShapeDtypeStruct((R_MAX, XW), BF16)),
        grid=(nb, n_hp),
        in_specs=[pl.BlockSpec(memory_space=pltpu.SMEM),
                  pl.BlockSpec(memory_space=pltpu.SMEM),
                  pl.BlockSpec((ATT_HPS, SEQ_BLK, LANE), lambda i, h: (C_DQ // ATT_HPS + h, i, 0)),
                  pl.BlockSpec((ATT_HPS, SEQ_BLK, LANE), lambda i, h: (C_DK // ATT_HPS + h, i, 0)),
                  pl.BlockSpec((ATT_HPS, SEQ_BLK, LANE), lambda i, h: (C_DV // ATT_HPS + h, i, 0)),
                  pl.BlockSpec((None, ATT_HPS * LANE, PAST), lambda i, h: (smp(i), h, 0)),
                  pl.BlockSpec((None, None, PAST, DIFF_H, LANE), lambda i, h: (smp(i), 0, 0, 0, 0)),
                  pl.BlockSpec((T_S, LANE), lambda i, h: (0, 0)),
                  pl.BlockSpec((T_S, LANE), lambda i, h: (0, 0)),
                  pl.BlockSpec((1, LANE), lambda i, h: (0, 0)),
                  pl.BlockSpec((None, SEQ_BLK, LANE), lambda i, h: (C_RQ + h, i, 0)),
                  pl.BlockSpec((None, SEQ_BLK, LANE), lambda i, h: (C_RK + h, i, 0)),
                  pl.BlockSpec((2, SEQ_BLK, LANE), lambda i, h: (C_RV // 2 + h, i, 0)),
                  pl.BlockSpec((2, SEQ_BLK, LANE), lambda i, h: (C_RG // 2 + h, i, 0)),
                  state_in, state_in,
                  pl.BlockSpec((1, RET_VD), lambda i, h: (0, h)),
                  *side_specs],
        out_specs=(pl.BlockSpec((SEQ_BLK, ATT_HPS * LANE), lambda i, h: (i, h)),
                   pl.BlockSpec((SEQ_BLK, RET_VD), lambda i, h: (i, h)),
                   state_out, state_out,
                   *side_specs,
                   pl.BlockSpec((zrows, XW), row_slice)),
        scratch_shapes=[pltpu.VMEM((ATT_HPS, T_S, LANE), BF16),
                        pltpu.VMEM((ATT_HPS, T_S, LANE), BF16),
                        pltpu.VMEM((ATT_HPS, T_S + PAST, 2 * LANE), BF16),
                        pltpu.VMEM((RET_H, T_P, T_P), F32)],
        compiler_params=_cparams(("arbitrary", "arbitrary")),
        name="mixers",
    )(scal, lg, proj, proj, proj, cache_k, cache_v, cos_t, sin_t, subln_g,
      proj, proj, proj, proj, s0f, s0b, gnorm, *side_weights)


def _rope_tables():
    n_rows = T_S // GRID_W
    row = np.repeat(np.arange(n_rows), GRID_W).astype(np.float64)
    col = np.tile(np.arange(GRID_W), n_rows).astype(np.float64)
    n_freq = DIFF_HD // 4
    inv = ROPE_BASE ** (-np.arange(n_freq, dtype=np.float64) / n_freq)

    def axis_tables(pos):
        ang = pos[:, None] * inv[None, :]
        c = np.cos(ang)
        s = np.sin(ang)
        return np.concatenate([c, c], axis=-1), np.concatenate([-s, s], axis=-1)

    cr, sr = axis_tables(row)
    cc, sc = axis_tables(col)
    cos_h = np.concatenate([cr, cc], axis=-1)
    sin_h = np.concatenate([sr, sc], axis=-1)
    return (jnp.asarray(np.concatenate([cos_h, cos_h], axis=-1), F32),
            jnp.asarray(np.concatenate([sin_h, sin_h], axis=-1), F32))


OP_TM = 512
OP_NPT = N_PROMPT // OP_TM


def _outproj_kernel(ra_ref, da_ref, gr_ref, gd_ref, xp_ref, xs_ref, mod_ref, n2_ref,
                    wro32_ref, wdo32_ref, wo32_ref, rgw_ref, rew_ref, brt_ref,
                    x1_ref, h2_ref, cw8_ref, info_ref, infot_ref, off_ref, pctab_ref, steps_ref,
                    m_scr, wro_ref, wdo_ref, wo_ref, wrt_ref, pc_scr):
    i = pl.program_id(0)

    @pl.when(i == 0)
    def _():
        wro_ref[...] = wro32_ref[...].astype(BF16)
        wdo_ref[...] = wdo32_ref[...].astype(BF16)
        wo_ref[...] = wo32_ref[...].astype(BF16)
        gap = jnp.zeros((D, ROUTE_E0 - N_GROUPS), F32)
        pad = jnp.zeros((D, ROUTE_W - ROUTE_E0 - N_EXP), F32)
        wr = jnp.concatenate([rgw_ref[...], gap, rew_ref[...], pad], axis=1)
        wrt_ref[...] = wr.T.astype(BF16)

    ret_out = jnp.dot(ra_ref[...], wro_ref[...], preferred_element_type=F32)
    diff_out = jnp.dot(da_ref[...], wdo_ref[...], preferred_element_type=F32)
    for c in range(D // LANE):
        sl = slice(c * LANE, (c + 1) * LANE)
        m = (jax.nn.sigmoid(gr_ref[c].astype(F32)) * ret_out[:, sl]
             + jax.nn.sigmoid(gd_ref[c].astype(F32)) * diff_out[:, sl])
        m_scr[:, sl] = m.astype(BF16)
    mix = jnp.dot(m_scr[...], wo_ref[...], preferred_element_type=F32)
    mod = mod_ref[...]

    x1 = jnp.where(i < OP_NPT, xp_ref[...], xs_ref[...]) + mod[:, 2 * D:3 * D] * mix
    x1_ref[...] = x1
    h2 = (_rms(x1) * n2_ref[...] * (1.0 + mod[:, 4 * D:5 * D]) + mod[:, 3 * D:4 * D]).astype(BF16)
    h2_ref[...] = h2
    logits_t = lax.dot_general(wrt_ref[...], h2, (((1,), (1,)), ((), ())),
                               preferred_element_type=F32) + brt_ref[...]
    cw8_ref[...], info_ref[...], pc, infot_ref[...] = _route_cols(logits_t)
    pc_scr[pl.ds(i, 1), :] = pc

    @pl.when(i == RT_NT - 1)
    def _():
        off_ref[...], pctab_ref[...], steps_ref[...] = _dispatch_plan(pc_scr[...])


def _outproj(ret_act, diff_act, proj, xp, xs, mod3, n2, wro, wdo, wo, rgw, rew, brt):
    assert OP_TM == RT_TM
    npt = OP_NPT
    full = lambda i: (0, 0)
    once = pl.Buffered(1)
    return pl.pallas_call(
        _outproj_kernel,
        out_shape=(jax.ShapeDtypeStruct((N_TOK, D), F32),
                   jax.ShapeDtypeStruct((N_TOK, D), BF16),
                   jax.ShapeDtypeStruct((N_TOK, ROUTE_W), F32),
                   jax.ShapeDtypeStruct((N_TOK, ROUTE_W), F32),
                   jax.ShapeDtypeStruct((RT_NT, EPG, RT_TM), F32),
                   jax.ShapeDtypeStruct((RT_NT, ROUTE_W), jnp.int32),
                   jax.ShapeDtypeStruct((RT_NT, ROUTE_W), jnp.int32),
                   jax.ShapeDtypeStruct((PLAN_ROWS, ROUTE_W), jnp.int32)),
        grid=(N_TOK // OP_TM,),
        in_specs=[pl.BlockSpec((OP_TM, D), lambda i: (i, 0)),
                  pl.BlockSpec((OP_TM, D), lambda i: (i, 0)),
                  pl.BlockSpec((8, OP_TM, LANE), lambda i: (C_GR // 8, i, 0)),
                  pl.BlockSpec((8, OP_TM, LANE), lambda i: (C_GD // 8, i, 0)),
                  pl.BlockSpec((OP_TM, D), lambda i: (jnp.minimum(i, npt - 1), 0)),
                  pl.BlockSpec((OP_TM, D), lambda i: (jnp.maximum(i - npt, 0), 0)),
                  pl.BlockSpec((None, 1, 6 * D), lambda i: (_mod_row(OP_TM)(i), 0, 0)),
                  pl.BlockSpec((1, D), full),
                  pl.BlockSpec((D, D), full, pipeline_mode=once),
                  pl.BlockSpec((D, D), full, pipeline_mode=once),
                  pl.BlockSpec((D, D), full, pipeline_mode=once),
                  pl.BlockSpec((D, N_GROUPS), full, pipeline_mode=once),
                  pl.BlockSpec((D, N_EXP), full, pipeline_mode=once),
                  pl.BlockSpec((ROUTE_W, 1), full)],
        out_specs=(pl.BlockSpec((OP_TM, D), lambda i: (i, 0)),
                   pl.BlockSpec((OP_TM, D), lambda i: (i, 0)),
                   pl.BlockSpec((OP_TM, ROUTE_W), lambda i: (i, 0)),
                   pl.BlockSpec((OP_TM, ROUTE_W), lambda i: (i, 0)),
                   pl.BlockSpec((None, EPG, RT_TM), lambda i: (i, 0, 0)),
                   pl.BlockSpec((RT_NT, ROUTE_W), full),
                   pl.BlockSpec((RT_NT, ROUTE_W), full),
                   pl.BlockSpec((PLAN_ROWS, ROUTE_W), full)),
        scratch_shapes=[pltpu.VMEM((OP_TM, D), BF16),
                        pltpu.VMEM((D, D), BF16), pltpu.VMEM((D, D), BF16), pltpu.VMEM((D, D), BF16),
                        pltpu.VMEM((ROUTE_W, D), BF16),
                        pltpu.VMEM((RT_NT, ROUTE_W), F32)],
        compiler_params=_cparams(("arbitrary",)),
        name="outproj",
    )(ret_act, diff_act, proj, proj, xp, xs, mod3, n2, wro, wdo, wo, rgw, rew, brt)


RT_TM = 512
RT_NT = N_TOK // RT_TM
PIECE = 16
R_LOC = RT_TM + N_GROUPS * PIECE
R_STAGE = 640
EX_TM = 512
R_MAX = 11264
EX_NT = R_MAX // EX_TM


ROUTE_E0 = EPG


def _route_cols(lgt):
    row = lax.broadcasted_iota(jnp.int32, (EPG, RT_TM), 0)
    neg = jnp.float32(-jnp.inf)

    def first_row(cond):
        return jnp.min(jnp.where(cond, row, EPG), axis=0, keepdims=True)

    head = lgt[0:EPG]
    is_g = row < N_GROUPS
    gl = jnp.where(is_g, head, neg)
    gmax = jnp.max(gl, axis=0, keepdims=True)
    gsum = jnp.sum(jnp.where(is_g, jnp.exp(head - gmax), 0.0), axis=0, keepdims=True)
    p_top = 1.0 / gsum
    g_idx = first_row(gl == gmax)
    el = lgt[ROUTE_E0 + (N_GROUPS - 1) * EPG:ROUTE_E0 + N_GROUPS * EPG]
    for g in reversed(range(N_GROUPS - 1)):
        el = jnp.where(g_idx == g, lgt[ROUTE_E0 + g * EPG:ROUTE_E0 + (g + 1) * EPG], el)
    ee = jnp.exp(el - jnp.max(el, axis=0, keepdims=True))
    ep = ee / jnp.sum(ee, axis=0, keepdims=True)
    e1 = jnp.max(ep, axis=0, keepdims=True)
    i1 = first_row(ep == e1)
    ep2 = jnp.where(row == i1, -1.0, ep)
    e2 = jnp.max(ep2, axis=0, keepdims=True)
    i2 = first_row(ep2 == e2)
    den = e1 + e2
    cw8_t = (jnp.where(row == i1, p_top * e1 / den, 0.0)
             + jnp.where(row == i2, p_top * e2 / den, 0.0))

    onehot = (row == g_idx).astype(F32)
    ii = lax.broadcasted_iota(jnp.int32, (RT_TM, RT_TM), 0)
    jj = lax.broadcasted_iota(jnp.int32, (RT_TM, RT_TM), 1)
    earlier = (ii < jj).astype(BF16)
    prefix = jnp.dot(onehot.astype(BF16), earlier, preferred_element_type=F32)
    cnt = jnp.sum(onehot, axis=1, keepdims=True)
    pc_col = jnp.floor((cnt + (PIECE - 1.0)) * (1.0 / PIECE)) * PIECE
    row1 = lax.broadcasted_iota(jnp.int32, (EPG, 1), 0)
    lane1 = lax.broadcasted_iota(jnp.int32, (1, ROUTE_W), 1)
    lo = jnp.zeros((EPG, 1), F32)
    run = jnp.zeros((1, 1), F32)
    pc = jnp.zeros((1, ROUTE_W), F32)
    for g in range(N_GROUPS):
        pc_g = jnp.sum(jnp.where(row1 == g, pc_col, 0.0), axis=0, keepdims=True)
        lo = jnp.where(row1 == g, run, lo)
        pc = jnp.where(lane1 == g, pc_g, pc)
        run = run + pc_g
    dest = jnp.sum(onehot * (prefix + lo), axis=0, keepdims=True)
    info_t = jnp.where(row == 0, g_idx.astype(F32), jnp.where(row == 1, dest, 0.0))

    slab = jnp.concatenate([cw8_t, info_t, jnp.zeros((LANE - 2 * EPG, RT_TM), F32)], axis=0)
    cols = slab.T
    lane = lax.broadcasted_iota(jnp.int32, cols.shape, 1)
    cw8 = jnp.where(lane < EPG, cols, 0.0)
    info = jnp.where(lane < 2, pltpu.roll(cols, LANE - EPG, 1), 0.0)
    return cw8, info, pc, info_t


PLAN_ROWS = 32


def _dispatch_plan(pc_all):
    lane = lax.broadcasted_iota(jnp.int32, (1, ROUTE_W), 1)
    is_g = lane < N_GROUPS
    seg_len = jnp.sum(pc_all, axis=0, keepdims=True)
    seg_pad = jnp.floor((seg_len + (EX_TM - 1.0)) * (1.0 / EX_TM)) * EX_TM
    run = seg_pad + pltpu.roll(seg_pad, 1, 1)
    seg_end_pad = jnp.where(is_g, run + pltpu.roll(run, 2, 1), 0.0)
    seg_start = seg_end_pad - seg_pad
    ti = lax.broadcasted_iota(jnp.int32, (RT_NT, RT_NT), 0)
    tj = lax.broadcasted_iota(jnp.int32, (RT_NT, RT_NT), 1)
    earlier = (tj < ti).astype(BF16)
    within = jnp.dot(earlier, pc_all.astype(BF16), preferred_element_type=F32)
    chunk_off = jnp.where(is_g, seg_start + within, 0.0)

    start = (lax.broadcasted_iota(jnp.int32, (PLAN_ROWS, 1), 0) * EX_TM).astype(F32)
    passed = jnp.where(jnp.logical_and(is_g, start >= seg_end_pad), 1.0, 0.0)
    group = jnp.minimum(jnp.sum(passed, axis=1, keepdims=True), N_GROUPS - 1.0)
    used_end = jnp.sum(jnp.where(lane == group.astype(jnp.int32), seg_start + seg_len, 0.0),
                       axis=1, keepdims=True)
    used = jnp.clip(used_end - start, 0.0, float(EX_TM))
    steps = jnp.where(lane == 0, group, jnp.where(lane == 1, used, 0.0))
    return chunk_off.astype(jnp.int32), pc_all.astype(jnp.int32), steps.astype(jnp.int32)


def _piece_copies(off_ref, pc_ref, tile, make):
    lo = 0
    for g in range(N_GROUPS):
        n = pc_ref[tile, g] // PIECE
        base = off_ref[tile, g]

        def body(j, carry, lo=lo, base=base):
            make(pl.multiple_of(lo + j * PIECE, PIECE), pl.multiple_of(base + j * PIECE, PIECE))
            return carry

        lax.fori_loop(0, n, body, 0)
        lo = lo + pc_ref[tile, g]


def _piece_count(pc_ref, tile):
    n = 0
    for g in range(N_GROUPS):
        n = n + pc_ref[tile, g] // PIECE
    return n


XW = D + ROUTE_W


def _dispatch_kernel(off_ref, pc_ref, h_ref, infot_ref, cw8_ref, xs_in, xs_out, x_scr, sem):
    del xs_in
    i = pl.program_id(0)
    slot = i % 2
    dest = infot_ref[1:2, :]
    row = lax.broadcasted_iota(jnp.int32, (R_LOC, RT_TM), 0).astype(F32)
    sel = (row == dest).astype(BF16)
    cw = cw8_ref[...]
    hi = cw.astype(BF16).astype(F32)
    mid = (cw - hi).astype(BF16).astype(F32)
    low = (cw - hi - mid).astype(BF16).astype(F32)
    pieces = (hi + pltpu.roll(mid, EPG, 1) + pltpu.roll(low, 2 * EPG, 1)).astype(BF16)
    rows = jnp.concatenate([h_ref[...], pieces], axis=1)
    x_scr[slot] = jnp.dot(sel, rows, preferred_element_type=F32).astype(BF16)

    def x_copy(s, src, dst):
        return pltpu.make_async_copy(x_scr.at[s, pl.ds(src, PIECE)], xs_out.at[pl.ds(dst, PIECE)],
                                     sem.at[s])

    _piece_copies(off_ref, pc_ref, i, lambda src, dst: x_copy(slot, src, dst).start())

    def wait_tile(tile, s):
        def wait(j, carry):
            x_copy(s, 0, 0).wait()
            return carry

        lax.fori_loop(0, _piece_count(pc_ref, tile), wait, 0)

    @pl.when(i > 0)
    def _():
        wait_tile(i - 1, 1 - slot)

    @pl.when(i == RT_NT - 1)
    def _():
        wait_tile(i, slot)


def _dispatch(chunk_off, pc, h2, info_t, cw8, xs0):
    grid_spec = pltpu.PrefetchScalarGridSpec(
        num_scalar_prefetch=2,
        grid=(RT_NT,),
        in_specs=[pl.BlockSpec((RT_TM, D), lambda i, o, p: (i, 0)),
                  pl.BlockSpec((None, EPG, RT_TM), lambda i, o, p: (i, 0, 0)),
                  pl.BlockSpec((RT_TM, ROUTE_W), lambda i, o, p: (i, 0)),
                  pl.BlockSpec(memory_space=pl.ANY)],
        out_specs=pl.BlockSpec(memory_space=pl.ANY),
        scratch_shapes=[pltpu.VMEM((2, R_LOC, XW), BF16),
                        pltpu.SemaphoreType.DMA((2,))])
    return pl.pallas_call(
        _dispatch_kernel,
        out_shape=jax.ShapeDtypeStruct((R_MAX, XW), BF16),
        grid_spec=grid_spec,
        input_output_aliases={5: 0},
        compiler_params=_cparams(("arbitrary",)),
        name="dispatch",
    )(chunk_off, pc, h2, info_t, cw8, xs0)


def _expert_kernel(steps_ref, x_ref, wg_ref, wu_ref, wd_ref, y_ref):
    k = pl.program_id(0)
    used = steps_ref[k, 1]
    half = EX_TM // 2

    def run(rows):
        x = x_ref[rows, 0:D]
        cw = x_ref[rows, D:XW].astype(F32)
        lane = lax.broadcasted_iota(jnp.int32, cw.shape, 1)
        acc = jnp.zeros(x.shape, F32)
        for j in range(EPG):
            mine = jnp.logical_and((lane & (EPG - 1)) == j, lane < 3 * EPG)
            w = jnp.sum(jnp.where(mine, cw, 0.0), axis=-1, keepdims=True)
            a = (jax.nn.silu(jnp.dot(x, wg_ref[j], preferred_element_type=F32))
                 * jnp.dot(x, wu_ref[j], preferred_element_type=F32))
            acc = acc + jnp.dot((a * w).astype(BF16), wd_ref[j], preferred_element_type=F32)
        y_ref[rows, :] = acc.astype(BF16)

    @pl.when(used > half)
    def _():
        run(slice(0, EX_TM))

    @pl.when(jnp.logical_and(used > 0, used <= half))
    def _():
        run(slice(0, half))
        y_ref[half:EX_TM, :] = jnp.zeros((EX_TM - half, D), BF16)

    @pl.when(used == 0)
    def _():
        y_ref[...] = jnp.zeros_like(y_ref)


def _experts(steps, xs, wg, wu, wd):
    grid_spec = pltpu.PrefetchScalarGridSpec(
        num_scalar_prefetch=1,
        grid=(EX_NT,),
        in_specs=[pl.BlockSpec((EX_TM, XW), lambda k, st: (k, 0)),
                  pl.BlockSpec((EPG, D, FF), lambda k, st: (st[k, 0], 0, 0)),
                  pl.BlockSpec((EPG, D, FF), lambda k, st: (st[k, 0], 0, 0)),
                  pl.BlockSpec((EPG, FF, D), lambda k, st: (st[k, 0], 0, 0))],
        out_specs=pl.BlockSpec((EX_TM, D), lambda k, st: (k, 0)))
    return pl.pallas_call(
        _expert_kernel,
        out_shape=jax.ShapeDtypeStruct((R_MAX, D), BF16),
        grid_spec=grid_spec,
        compiler_params=_cparams(("arbitrary",)),
        name="experts",
    )(steps, xs, wg, wu, wd)


CB_NPT = N_PROMPT // RT_TM


def _combine_kernel(off_ref, pc_ref, info_ref, x1_ref, mod_ref, g_ref, ys_hbm,
                    yp_ref, ysm_ref, stage, sem):
    i = pl.program_id(0)
    slot = i % 2

    def copy(s, dst, src):
        return pltpu.make_async_copy(ys_hbm.at[pl.ds(src, PIECE)], stage.at[s, pl.ds(dst, PIECE)],
                                     sem.at[s])

    def fetch(tile, s):
        _piece_copies(off_ref, pc_ref, tile, lambda loc, glob: copy(s, loc, glob).start())

    @pl.when(i == 0)
    def _():
        stage[...] = jnp.zeros_like(stage)
        fetch(0, 0)

    @pl.when(i + 1 < RT_NT)
    def _():
        fetch(i + 1, 1 - slot)

    def wait(j, carry):
        copy(slot, 0, 0).wait()
        return carry

    lax.fori_loop(0, _piece_count(pc_ref, i), wait, 0)

    dest = info_ref[...][:, 1:2]
    col = lax.broadcasted_iota(jnp.int32, (RT_TM, R_STAGE), 1).astype(F32)
    moe = jnp.dot((col == dest).astype(BF16), stage[slot], preferred_element_type=F32)
    mod = mod_ref[...]
    out = _rms(x1_ref[...] + mod[:, 5 * D:6 * D] * moe) * g_ref[...]

    @pl.when(i < CB_NPT)
    def _():
        yp_ref[...] = out

    @pl.when(i >= CB_NPT)
    def _():
        ysm_ref[...] = out


def _combine(chunk_off, pc, info, x1, mod3, fg, ys):
    npt = CB_NPT
    grid_spec = pltpu.PrefetchScalarGridSpec(
        num_scalar_prefetch=2,
        grid=(RT_NT,),
        in_specs=[pl.BlockSpec((RT_TM, ROUTE_W), lambda i, o, p: (i, 0)),
                  pl.BlockSpec((RT_TM, D), lambda i, o, p: (i, 0)),
                  pl.BlockSpec((None, 1, 6 * D), lambda i, o, p: (_mod_row(RT_TM)(i), 0, 0)),
                  pl.BlockSpec((1, D), lambda i, o, p: (0, 0)),
                  pl.BlockSpec(memory_space=pl.ANY)],
        out_specs=(pl.BlockSpec((RT_TM, D), lambda i, o, p: (jnp.minimum(i, npt - 1), 0)),
                   pl.BlockSpec((RT_TM, D), lambda i, o, p: (jnp.maximum(i - npt, 0), 0))),
        scratch_shapes=[pltpu.VMEM((2, R_STAGE, D), BF16),
                        pltpu.SemaphoreType.DMA((2,))])
    return pl.pallas_call(
        _combine_kernel,
        out_shape=(jax.ShapeDtypeStruct((N_PROMPT, D), F32),
                   jax.ShapeDtypeStruct((N_SAMPLE, D), F32)),
        grid_spec=grid_spec,
        compiler_params=_cparams(("arbitrary",)),
        name="combine",
    )(chunk_off, pc, info, x1, mod3, fg, ys)


def kernel(x_prompt, x_sample, cache_diff_k, cache_diff_v, state_ret_fwd, state_ret_bwd, c, c_ctx,
           w_ada, b_ada, norm1_g, norm2_g, w_in, ret_decay_fwd, ret_decay_bwd, ret_norm_g,
           diff_lambda_q1, diff_lambda_k1, diff_lambda_q2, diff_lambda_k2, diff_subln_g,
           w_ret_o, w_diff_o, w_o, router_group_w, router_group_b, router_expert_w, router_expert_b,
           moe_w_gate, moe_w_up, moe_w_down, final_norm_g):
    l = 0
    lam_init = 0.8 - 0.6 * math.exp(-0.3 * l)
    lam = (jnp.exp(jnp.sum(diff_lambda_q1[l].astype(F32) * diff_lambda_k1[l].astype(F32)))
           - jnp.exp(jnp.sum(diff_lambda_q2[l].astype(F32) * diff_lambda_k2[l].astype(F32))) + lam_init)
    attn_scal = jnp.stack([lam, jnp.float32(1.0 - lam_init)]).astype(F32)
    lg = jnp.stack([jax.nn.log_sigmoid(ret_decay_fwd[l].astype(F32)),
                    jax.nn.log_sigmoid(ret_decay_bwd[l].astype(F32))])

    xp = x_prompt.reshape(N_PROMPT, D)
    xs = x_sample.reshape(N_SAMPLE, D)
    mod3 = _modulation(c_ctx[None, :], c, w_ada[l], b_ada[l][None, :])

    proj, kt32, v32 = _inproj(xp, xs, mod3, norm1_g[l][None, :], w_in[l])

    cos_t, sin_t = _rope_tables()
    cache_kt = jnp.transpose(cache_diff_k[:, l], (0, 2, 3, 4, 1)).reshape(4, D, PAST)
    (diff_act, ret_act, s_f, s_b, wg_bf, wu_bf, wd_bf, xs0) = _mixers(
        attn_scal, lg, proj, cache_kt, cache_diff_v, cos_t, sin_t, diff_subln_g[l][None, :],
        state_ret_fwd[:, l], state_ret_bwd[:, l], ret_norm_g[l][None, :],
        (moe_w_gate[l], moe_w_up[l], moe_w_down[l]))

    brt = jnp.concatenate([router_group_b[l], jnp.zeros((ROUTE_E0 - N_GROUPS,), F32), router_expert_b[l],
                           jnp.zeros((ROUTE_W - ROUTE_E0 - N_EXP,), F32)])[:, None]
    x1, h2, cw8, info, info_t, chunk_off, pc, steps = _outproj(
        ret_act, diff_act, proj, xp, xs, mod3, norm2_g[l][None, :],
        w_ret_o[l], w_diff_o[l], w_o[l], router_group_w[l], router_expert_w[l], brt)
    xs_sorted = _dispatch(chunk_off, pc, h2, info_t, cw8, xs0)
    y_sorted = _experts(steps, xs_sorted, wg_bf, wu_bf, wd_bf)
    yp, ys = _combine(chunk_off, pc, info, x1, mod3, final_norm_g[None, :], y_sorted)

    return (yp.reshape(16, T_P, D), ys.reshape(4, T_S, D),
            jnp.transpose(kt32.reshape(16, DIFF_H, 2, DIFF_HD, T_P), (0, 4, 1, 2, 3))[:, None],
            v32.reshape(16, 1, T_P, DIFF_H, 2 * DIFF_HD),
            s_f.reshape(16, 1, RET_H, RET_KD, RET_VD), s_b.reshape(16, 1, RET_H, RET_KD, RET_VD))
```

```python
import math

import jax
import jax.numpy as jnp
import numpy as np
from jax import lax
from jax.experimental import pallas as pl
from jax.experimental.pallas import tpu as pltpu

F32 = jnp.float32
BF16 = jnp.bfloat16

D = 1024
N_PROMPT = 16 * 256
N_SAMPLE = 4 * 1024
N_TOK = N_PROMPT + N_SAMPLE
T_P = 256
T_S = 1024
PAST = 512
GRID_W = 64
RET_H = 4
RET_KD = 128
RET_VD = 256
DIFF_H = 8
DIFF_HD = 64
ROPE_BASE = 10000.0
N_GROUPS = 4
EPG = 8
N_EXP = 32
FF = 256
EPS = 1e-6
IN_W = 8192
LANE = 128
N_CHUNK = IN_W // LANE
C_RQ, C_RK, C_RV, C_RG, C_DQ, C_DK, C_DV, C_GR, C_GD = 0, 4, 8, 16, 24, 32, 40, 48, 56
ROUTE_W = 128
SEQ_BLK = 1024
N_PBLK = N_PROMPT // SEQ_BLK
V7X_VMEM_BYTES = 64 * 1024 * 1024
VMEM_LIMIT = V7X_VMEM_BYTES - 8 * 1024 * 1024


def _cparams(sem):
    return pltpu.CompilerParams(dimension_semantics=sem, vmem_limit_bytes=VMEM_LIMIT)


def _mod_row(tile_rows):
    def f(i):
        start = i * tile_rows
        return jnp.where(start < N_PROMPT, 0, 1 + (start - N_PROMPT) // T_S)
    return f


def _rms(x):
    return x * lax.rsqrt(jnp.mean(x * x, axis=-1, keepdims=True) + EPS)


MOD_ROWS = 8


def _mod_kernel(ctx_ref, c_ref, w_ref, b_ref, o_ref):
    cond = jnp.concatenate([ctx_ref[...], c_ref[...],
                            jnp.zeros((MOD_ROWS - 1 - c_ref.shape[0], D), F32)], axis=0)
    s = jax.nn.silu(cond)
    out = jnp.dot(s.astype(BF16), w_ref[...].astype(BF16), preferred_element_type=F32) + b_ref[...]
    for r in range(MOD_ROWS):
        o_ref[r] = out[r:r + 1, :]


def _modulation(c_ctx, c, w_ada, b_ada):
    tn = 1536
    return pl.pallas_call(
        _mod_kernel,
        out_shape=jax.ShapeDtypeStruct((MOD_ROWS, 1, 6 * D), F32),
        grid=(6 * D // tn,),
        in_specs=[pl.BlockSpec((1, D), lambda j: (0, 0)),
                  pl.BlockSpec(c.shape, lambda j: (0, 0)),
                  pl.BlockSpec((D, tn), lambda j: (0, j)),
                  pl.BlockSpec((1, tn), lambda j: (0, j))],
        out_specs=pl.BlockSpec((MOD_ROWS, 1, tn), lambda j: (0, 0, j)),
        compiler_params=_cparams(("arbitrary",)),
        name="mod",
    )(c_ctx, c, w_ada, b_ada)


IP_TM = 512
IP_TN = 2048
IP_NPT = N_PROMPT // IP_TM


IP_KV_TILE = C_DK * LANE // IP_TN
IP_SPT = IP_TM // T_P


def _inproj_kernel(xp_ref, xs_ref, mod_ref, n1_ref, w_ref, proj_ref, kt_ref, v32_ref, w_scr):
    j = pl.program_id(0)
    i = pl.program_id(1)

    @pl.when(i == 0)
    def _():
        w_scr[...] = w_ref[...].astype(BF16)

    x = jnp.where(i < IP_NPT, xp_ref[...], xs_ref[...])
    mod = mod_ref[...]
    h = (_rms(x) * n1_ref[...] * (1.0 + mod[:, D:2 * D]) + mod[:, 0:D]).astype(BF16)
    acc = jnp.dot(h, w_scr[...], preferred_element_type=F32)
    for c in range(IP_TN // LANE):
        proj_ref[c] = acc[:, c * LANE:(c + 1) * LANE].astype(BF16)

    @pl.when(jnp.logical_and(j == IP_KV_TILE, i < IP_NPT))
    def _():
        for s in range(IP_SPT):
            kt_ref[s] = acc[s * T_P:(s + 1) * T_P, :D].T
        v32_ref[...] = acc[:, D:]


def _inproj(xp, xs, mod3, n1, w_in):
    npt = IP_NPT
    cpt = IP_TN // LANE

    def kv_tile(j, i):
        return jnp.where(j < IP_KV_TILE, 0, jnp.where(j == IP_KV_TILE, jnp.minimum(i, npt - 1), npt - 1))

    return pl.pallas_call(
        _inproj_kernel,
        out_shape=(jax.ShapeDtypeStruct((N_CHUNK, N_TOK, LANE), BF16),
                   jax.ShapeDtypeStruct((N_PROMPT // T_P, D, T_P), F32),
                   jax.ShapeDtypeStruct((N_PROMPT, D), F32)),
        grid=(IN_W // IP_TN, N_TOK // IP_TM),
        in_specs=[pl.BlockSpec((IP_TM, D), lambda j, i: (jnp.minimum(i, npt - 1), 0)),
                  pl.BlockSpec((IP_TM, D), lambda j, i: (jnp.maximum(i - npt, 0), 0)),
                  pl.BlockSpec((None, 1, 6 * D), lambda j, i: (_mod_row(IP_TM)(i), 0, 0)),
                  pl.BlockSpec((1, D), lambda j, i: (0, 0)),
                  pl.BlockSpec((D, IP_TN), lambda j, i: (0, j))],
        out_specs=(pl.BlockSpec((cpt, IP_TM, LANE), lambda j, i: (j, i, 0)),
                   pl.BlockSpec((IP_SPT, D, T_P), lambda j, i: (kv_tile(j, i), 0, 0)),
                   pl.BlockSpec((IP_TM, D), lambda j, i: (kv_tile(j, i), 0))),
        scratch_shapes=[pltpu.VMEM((D, IP_TN), BF16)],
        compiler_params=_cparams(("arbitrary", "arbitrary")),
        name="inproj",
    )(xp, xs, mod3, n1, w_in)


def _decay_mask(t, lgf, lgb):
    ii = lax.broadcasted_iota(jnp.int32, (t, t), 0)
    jj = lax.broadcasted_iota(jnp.int32, (t, t), 1)
    rel = (ii - jj).astype(F32)
    e = jnp.exp(jnp.where(rel >= 0.0, lgf, -lgb) * rel)
    return jnp.where(rel == 0.0, 2.0, e) * (RET_KD ** -0.5)


def _ret_parts(lgf, lgb, q_ref, k_ref, v_ref, rg_ref, s0f_ref, s0b_ref, g_ref,
               o_ref, sf_ref, sb_ref, dm_ref):
    gain = g_ref[...]
    nt = (((1,), (1,)), ((), ()))
    tn = (((0,), (0,)), ((), ()))

    def finish(o, rg):
        d = o - jnp.mean(o, axis=-1, keepdims=True)
        y = d * lax.rsqrt(jnp.mean(d * d, axis=-1, keepdims=True) + EPS) * gain
        return (jax.nn.silu(rg.astype(F32)) * y).astype(BF16)

    def vcat(r):
        return jnp.concatenate([v_ref[0, r, :], v_ref[1, r, :]], axis=1)

    def gcat(r):
        return jnp.concatenate([rg_ref[0, r, :], rg_ref[1, r, :]], axis=1)

    def build_mask():
        dm_ref[...] = _decay_mask(T_P, lgf, lgb)

    n_chunk = SEQ_BLK // T_P
    chunks = [slice(c * T_P, (c + 1) * T_P) for c in range(n_chunk)]
    t = lax.broadcasted_iota(jnp.int32, (T_P, 1), 0).astype(F32)
    kdf = jnp.exp(lgf * (T_P - 1.0 - t)) * (RET_KD ** -0.5)
    kdb = jnp.exp(lgb * t) * (RET_KD ** -0.5)

    def intra(r):
        sc = lax.dot_general(q_ref[r, :], k_ref[r, :], nt, preferred_element_type=F32)
        return jnp.dot((sc * dm_ref[...]).astype(BF16), vcat(r), preferred_element_type=F32)

    def key_state(r, kd):
        kw = (k_ref[r, :].astype(F32) * kd).astype(BF16)
        return lax.dot_general(kw, vcat(r), tn, preferred_element_type=F32)

    def context_part():
        for s, r in enumerate(chunks):
            o_ref[r, :] = finish(intra(r), gcat(r))
            sf_ref[s] = key_state(r, kdf)
            sb_ref[s] = key_state(r, kdb)

    def latent_part():
        qdf = jnp.exp(lgf * (t + 1.0))
        qdb = jnp.exp(lgb * (T_P - t))
        span = jnp.full((1, 1), float(T_P), F32)
        cf = jnp.exp(lgf * span)
        cb = jnp.exp(lgb * span)
        before_b = [None] * n_chunk
        state = s0b_ref[...]
        for c in reversed(range(n_chunk)):
            before_b[c] = state
            if c > 0:
                state = cb * state + key_state(chunks[c], kdb)
        state = s0f_ref[...]
        for c, r in enumerate(chunks):
            qf = q_ref[r, :].astype(F32)
            o = (intra(r)
                 + jnp.dot((qf * qdf).astype(BF16), state.astype(BF16), preferred_element_type=F32)
                 + jnp.dot((qf * qdb).astype(BF16), before_b[c].astype(BF16), preferred_element_type=F32))
            o_ref[r, :] = finish(o, gcat(r))
            if c + 1 < n_chunk:
                state = cf * state + key_state(r, kdf)

    return build_mask, context_part, latent_part


ATT_TQ = 1024
ATT_HPS = 2
ATT_QSCALE = (DIFF_HD ** -0.5) * math.log2(math.e)


N_SIDE = 3


def _mixer_kernel(sc_ref, lg_ref, q_ref, k_ref, v_ref, ckt_ref, cv_ref, cos_ref, sin_ref, g_ref,
                  rq_ref, rk_ref, rv_ref, rg_ref, s0f_ref, s0b_ref, gn_ref, *rest):
    side_in = rest[:N_SIDE]
    o_ref, ro_ref, sf_ref, sb_ref = rest[N_SIDE:N_SIDE + 4]
    side_out = rest[N_SIDE + 4:2 * N_SIDE + 4]
    xs0_ref, q_scr, k_scr, v_scr, dm_scr = rest[2 * N_SIDE + 4:]
    i = pl.program_id(0)
    rh = pl.program_id(1)
    build_mask, ret_context, ret_latent = _ret_parts(
        lg_ref[0, rh], lg_ref[1, rh], rq_ref, rk_ref, rv_ref, rg_ref, s0f_ref, s0b_ref, gn_ref,
        ro_ref, sf_ref, sb_ref, dm_scr.at[rh])

    @pl.when(i == 0)
    def _():
        build_mask()

    def side_jobs():
        for src, dst in zip(side_in, side_out):
            dst[...] = src[...].astype(BF16)
        xs0_ref[...] = jnp.zeros_like(xs0_ref)

    lam = sc_ref[0]
    out_scale = sc_ref[1]
    gain = g_ref[...] * out_scale
    lane = lax.broadcasted_iota(jnp.int32, (1, LANE), 1)
    first = lane < DIFF_HD
    nt = (((1,), (1,)), ((), ()))

    def halves(q):
        zero = jnp.zeros_like(q)
        return jnp.where(first, q, zero), jnp.where(first, zero, q)

    def weights(s):
        return jnp.exp2(s - jnp.max(s, axis=-1, keepdims=True)).astype(BF16)

    def finish(of0, of1):
        o = of0[:, :LANE] / of0[:, LANE:] - lam * (of1[:, :LANE] / of1[:, LANE:])
        return (_rms(o) * gain).astype(BF16)

    @pl.when(i < N_PBLK)
    def _():
        ret_context()
        n_seq = SEQ_BLK // T_P
        rows = [slice(s * T_P, (s + 1) * T_P) for s in range(n_seq)]
        ones = jnp.ones((T_P, LANE), BF16)
        for hh in range(ATT_HPS):
            cols = slice(hh * LANE, (hh + 1) * LANE)
            q0, q1 = halves((q_ref[hh].astype(F32) * ATT_QSCALE).astype(BF16))
            s0 = jnp.concatenate([lax.dot_general(q0[r], k_ref[hh, r, :], nt, preferred_element_type=F32)
                                  for r in rows], axis=0)
            s1 = jnp.concatenate([lax.dot_general(q1[r], k_ref[hh, r, :], nt, preferred_element_type=F32)
                                  for r in rows], axis=0)
            e0 = weights(s0)
            e1 = weights(s1)
            for r in rows:
                v1 = jnp.concatenate([v_ref[hh, r, :], ones], axis=1)
                o_ref[r, cols] = finish(jnp.dot(e0[r], v1, preferred_element_type=F32),
                                        jnp.dot(e1[r], v1, preferred_element_type=F32))

    @pl.when(i >= N_PBLK)
    def _():
        side_jobs()
        ret_latent()
        cos = cos_ref[...]
        sin = sin_ref[...]
        low = (lax.broadcasted_iota(jnp.int32, (T_S, LANE), 1) & 16) == 0

        def rope(x):
            xs = jnp.where(low, pltpu.roll(x, LANE - 16, 1), pltpu.roll(x, 16, 1))
            return x * cos + xs * sin

        for hh in range(ATT_HPS):
            cols = slice(hh * LANE, (hh + 1) * LANE)
            head = pl.program_id(1) * ATT_HPS + hh
            q_scr[hh] = (rope(q_ref[hh].astype(F32)) * ATT_QSCALE).astype(BF16)
            k_scr[hh] = rope(k_ref[hh].astype(F32)).astype(BF16)
            ckt = ckt_ref[cols, :].astype(BF16)
            v_scr[hh, 0:T_S, 0:LANE] = v_ref[hh]
            v_scr[hh, T_S:T_S + PAST, 0:LANE] = cv_ref[:, head, :].astype(BF16)
            v_scr[hh, :, LANE:2 * LANE] = jnp.ones((T_S + PAST, LANE), BF16)

            def scores(qh, hh=hh, ckt=ckt):
                return jnp.concatenate([lax.dot_general(qh, k_scr[hh], nt, preferred_element_type=F32),
                                        jnp.dot(qh, ckt, preferred_element_type=F32)], axis=1)

            for b in range(T_S // ATT_TQ):
                r = slice(b * ATT_TQ, (b + 1) * ATT_TQ)
                q0, q1 = halves(q_scr[hh, r, :])
                e0 = weights(scores(q0))
                e1 = weights(scores(q1))
                o_ref[r, cols] = finish(jnp.dot(e0, v_scr[hh], preferred_element_type=F32),
                                        jnp.dot(e1, v_scr[hh], preferred_element_type=F32))


def _mixers(scal, lg, proj, cache_k, cache_v, cos_t, sin_t, subln_g, s0f, s0b, gnorm, side_weights):
    nb = N_TOK // SEQ_BLK
    spb = SEQ_BLK // T_P
    smp = lambda i: jnp.maximum(i - N_PBLK, 0)
    pmt = lambda i: jnp.minimum(i, N_PBLK - 1)
    pmh = lambda i, h: jnp.where(i < N_PBLK, h, RET_H - 1)
    state_in = pl.BlockSpec((None, None, RET_KD, RET_VD), lambda i, h: (smp(i), h, 0, 0))
    state_out = pl.BlockSpec((spb, None, RET_KD, RET_VD), lambda i, h: (pmt(i), pmh(i, h), 0, 0))

    n_hp = DIFF_H // ATT_HPS
    assert n_hp == RET_H
    n_steps = (nb - N_PBLK) * n_hp
    step = lambda i, h: jnp.maximum((i - N_PBLK) * n_hp + h, 0)
    exp_per_step = N_EXP // n_steps
    exp_slice = lambda i, h: (step(i, h), 0, 0)
    row_slice = lambda i, h: (step(i, h), 0)
    up_spec = pl.BlockSpec((exp_per_step, D, FF), exp_slice)
    down_spec = pl.BlockSpec((exp_per_step, FF, D), exp_slice)
    side_specs = [up_spec, up_spec, down_spec]
    side_shapes = [jax.ShapeDtypeStruct(a.shape, BF16) for a in side_weights]
    zrows = R_MAX // n_steps

    return pl.pallas_call(
        _mixer_kernel,
        out_shape=(jax.ShapeDtypeStruct((N_TOK, DIFF_H * 2 * DIFF_HD), BF16),
                   jax.ShapeDtypeStruct((N_TOK, RET_H * RET_VD), BF16),
                   jax.ShapeDtypeStruct((16, RET_H, RET_KD, RET_VD), F32),
                   jax.ShapeDtypeStruct((16, RET_H, RET_KD, RET_VD), F32),
                   *side_shapes,
                   jax.ShapeDtypeStruct((R_MAX, XW), BF16)),
        grid=(nb, n_hp),
        in_specs=[pl.BlockSpec(memory_space=pltpu.SMEM),
                  pl.BlockSpec(memory_space=pltpu.SMEM),
                  pl.BlockSpec((ATT_HPS, SEQ_BLK, LANE), lambda i, h: (C_DQ // ATT_HPS + h, i, 0)),
                  pl.BlockSpec((ATT_HPS, SEQ_BLK, LANE), lambda i, h: (C_DK // ATT_HPS + h, i, 0)),
                  pl.BlockSpec((ATT_HPS, SEQ_BLK, LANE), lambda i, h: (C_DV // ATT_HPS + h, i, 0)),
                  pl.BlockSpec((None, ATT_HPS * LANE, PAST), lambda i, h: (smp(i), h, 0)),
                  pl.BlockSpec((None, None, PAST, DIFF_H, LANE), lambda i, h: (smp(i), 0, 0, 0, 0)),
                  pl.BlockSpec((T_S, LANE), lambda i, h: (0, 0)),
                  pl.BlockSpec((T_S, LANE), lambda i, h: (0, 0)),
                  pl.BlockSpec((1, LANE), lambda i, h: (0, 0)),
                  pl.BlockSpec((None, SEQ_BLK, LANE), lambda i, h: (C_RQ + h, i, 0)),
                  pl.BlockSpec((None, SEQ_BLK, LANE), lambda i, h: (C_RK + h, i, 0)),
                  pl.BlockSpec((2, SEQ_BLK, LANE), lambda i, h: (C_RV // 2 + h, i, 0)),
                  pl.BlockSpec((2, SEQ_BLK, LANE), lambda i, h: (C_RG // 2 + h, i, 0)),
                  state_in, state_in,
                  pl.BlockSpec((1, RET_VD), lambda i, h: (0, h)),
                  *side_specs],
        out_specs=(pl.BlockSpec((SEQ_BLK, ATT_HPS * LANE), lambda i, h: (i, h)),
                   pl.BlockSpec((SEQ_BLK, RET_VD), lambda i, h: (i, h)),
                   state_out, state_out,
                   *side_specs,
                   pl.BlockSpec((zrows, XW), row_slice)),
        scratch_shapes=[pltpu.VMEM((ATT_HPS, T_S, LANE), BF16),
                        pltpu.VMEM((ATT_HPS, T_S, LANE), BF16),
                        pltpu.VMEM((ATT_HPS, T_S + PAST, 2 * LANE), BF16),
                        pltpu.VMEM((RET_H, T_P, T_P), F32)],
        compiler_params=_cparams(("arbitrary", "arbitrary")),
        name="mixers",
    )(scal, lg, proj, proj, proj, cache_k, cache_v, cos_t, sin_t, subln_g,
      proj, proj, proj, proj, s0f, s0b, gnorm, *side_weights)


def _rope_tables():
    n_rows = T_S // GRID_W
    row = np.repeat(np.arange(n_rows), GRID_W).astype(np.float64)
    col = np.tile(np.arange(GRID_W), n_rows).astype(np.float64)
    n_freq = DIFF_HD // 4
    inv = ROPE_BASE ** (-np.arange(n_freq, dtype=np.float64) / n_freq)

    def axis_tables(pos):
        ang = pos[:, None] * inv[None, :]
        c = np.cos(ang)
        s = np.sin(ang)
        return np.concatenate([c, c], axis=-1), np.concatenate([-s, s], axis=-1)

    cr, sr = axis_tables(row)
    cc, sc = axis_tables(col)
    cos_h = np.concatenate([cr, cc], axis=-1)
    sin_h = np.concatenate([sr, sc], axis=-1)
    return (jnp.asarray(np.concatenate([cos_h, cos_h], axis=-1), F32),
            jnp.asarray(np.concatenate([sin_h, sin_h], axis=-1), F32))


OP_TM = 512
OP_NPT = N_PROMPT // OP_TM


def _outproj_kernel(ra_ref, da_ref, gr_ref, gd_ref, xp_ref, xs_ref, mod_ref, n2_ref,
                    wro32_ref, wdo32_ref, wo32_ref, rgw_ref, rew_ref, brt_ref,
                    x1_ref, h2_ref, cw8_ref, info_ref, infot_ref, off_ref, pctab_ref, steps_ref,
                    m_scr, wro_ref, wdo_ref, wo_ref, wrt_ref, pc_scr):
    i = pl.program_id(0)

    @pl.when(i == 0)
    def _():
        wro_ref[...] = wro32_ref[...].astype(BF16)
        wdo_ref[...] = wdo32_ref[...].astype(BF16)
        wo_ref[...] = wo32_ref[...].astype(BF16)
        gap = jnp.zeros((D, ROUTE_E0 - N_GROUPS), F32)
        pad = jnp.zeros((D, ROUTE_W - ROUTE_E0 - N_EXP), F32)
        wr = jnp.concatenate([rgw_ref[...], gap, rew_ref[...], pad], axis=1)
        wrt_ref[...] = wr.T.astype(BF16)

    ret_out = jnp.dot(ra_ref[...], wro_ref[...], preferred_element_type=F32)
    diff_out = jnp.dot(da_ref[...], wdo_ref[...], preferred_element_type=F32)
    for c in range(D // LANE):
        sl = slice(c * LANE, (c + 1) * LANE)
        m = (jax.nn.sigmoid(gr_ref[c].astype(F32)) * ret_out[:, sl]
             + jax.nn.sigmoid(gd_ref[c].astype(F32)) * diff_out[:, sl])
        m_scr[:, sl] = m.astype(BF16)
    mix = jnp.dot(m_scr[...], wo_ref[...], preferred_element_type=F32)
    mod = mod_ref[...]

    x1 = jnp.where(i < OP_NPT, xp_ref[...], xs_ref[...]) + mod[:, 2 * D:3 * D] * mix
    x1_ref[...] = x1
    h2 = (_rms(x1) * n2_ref[...] * (1.0 + mod[:, 4 * D:5 * D]) + mod[:, 3 * D:4 * D]).astype(BF16)
    h2_ref[...] = h2
    logits_t = lax.dot_general(wrt_ref[...], h2, (((1,), (1,)), ((), ())),
                               preferred_element_type=F32) + brt_ref[...]
    cw8_ref[...], info_ref[...], pc, infot_ref[...] = _route_cols(logits_t)
    pc_scr[pl.ds(i, 1), :] = pc

    @pl.when(i == RT_NT - 1)
    def _():
        off_ref[...], pctab_ref[...], steps_ref[...] = _dispatch_plan(pc_scr[...])


def _outproj(ret_act, diff_act, proj, xp, xs, mod3, n2, wro, wdo, wo, rgw, rew, brt):
    assert OP_TM == RT_TM
    npt = OP_NPT
    full = lambda i: (0, 0)
    once = pl.Buffered(1)
    return pl.pallas_call(
        _outproj_kernel,
        out_shape=(jax.ShapeDtypeStruct((N_TOK, D), F32),
                   jax.ShapeDtypeStruct((N_TOK, D), BF16),
                   jax.ShapeDtypeStruct((N_TOK, ROUTE_W), F32),
                   jax.ShapeDtypeStruct((N_TOK, ROUTE_W), F32),
                   jax.ShapeDtypeStruct((RT_NT, EPG, RT_TM), F32),
                   jax.ShapeDtypeStruct((RT_NT, ROUTE_W), jnp.int32),
                   jax.ShapeDtypeStruct((RT_NT, ROUTE_W), jnp.int32),
                   jax.ShapeDtypeStruct((PLAN_ROWS, ROUTE_W), jnp.int32)),
        grid=(N_TOK // OP_TM,),
        in_specs=[pl.BlockSpec((OP_TM, D), lambda i: (i, 0)),
                  pl.BlockSpec((OP_TM, D), lambda i: (i, 0)),
                  pl.BlockSpec((8, OP_TM, LANE), lambda i: (C_GR // 8, i, 0)),
                  pl.BlockSpec((8, OP_TM, LANE), lambda i: (C_GD // 8, i, 0)),
                  pl.BlockSpec((OP_TM, D), lambda i: (jnp.minimum(i, npt - 1), 0)),
                  pl.BlockSpec((OP_TM, D), lambda i: (jnp.maximum(i - npt, 0), 0)),
                  pl.BlockSpec((None, 1, 6 * D), lambda i: (_mod_row(OP_TM)(i), 0, 0)),
                  pl.BlockSpec((1, D), full),
                  pl.BlockSpec((D, D), full, pipeline_mode=once),
                  pl.BlockSpec((D, D), full, pipeline_mode=once),
                  pl.BlockSpec((D, D), full, pipeline_mode=once),
                  pl.BlockSpec((D, N_GROUPS), full, pipeline_mode=once),
                  pl.BlockSpec((D, N_EXP), full, pipeline_mode=once),
                  pl.BlockSpec((ROUTE_W, 1), full)],
        out_specs=(pl.BlockSpec((OP_TM, D), lambda i: (i, 0)),
                   pl.BlockSpec((OP_TM, D), lambda i: (i, 0)),
                   pl.BlockSpec((OP_TM, ROUTE_W), lambda i: (i, 0)),
                   pl.BlockSpec((OP_TM, ROUTE_W), lambda i: (i, 0)),
                   pl.BlockSpec((None, EPG, RT_TM), lambda i: (i, 0, 0)),
                   pl.BlockSpec((RT_NT, ROUTE_W), full),
                   pl.BlockSpec((RT_NT, ROUTE_W), full),
                   pl.BlockSpec((PLAN_ROWS, ROUTE_W), full)),
        scratch_shapes=[pltpu.VMEM((OP_TM, D), BF16),
                        pltpu.VMEM((D, D), BF16), pltpu.VMEM((D, D), BF16), pltpu.VMEM((D, D), BF16),
                        pltpu.VMEM((ROUTE_W, D), BF16),
                        pltpu.VMEM((RT_NT, ROUTE_W), F32)],
        compiler_params=_cparams(("arbitrary",)),
        name="outproj",
    )(ret_act, diff_act, proj, proj, xp, xs, mod3, n2, wro, wdo, wo, rgw, rew, brt)


RT_TM = 512
RT_NT = N_TOK // RT_TM
PIECE = 16
R_LOC = RT_TM + N_GROUPS * PIECE
R_STAGE = 640
EX_TM = 512
R_MAX = 11264
EX_NT = R_MAX // EX_TM


ROUTE_E0 = EPG


def _route_cols(lgt):
    row = lax.broadcasted_iota(jnp.int32, (EPG, RT_TM), 0)
    neg = jnp.float32(-jnp.inf)

    def first_row(cond):
        return jnp.min(jnp.where(cond, row, EPG), axis=0, keepdims=True)

    head = lgt[0:EPG]
    is_g = row < N_GROUPS
    gl = jnp.where(is_g, head, neg)
    gmax = jnp.max(gl, axis=0, keepdims=True)
    gsum = jnp.sum(jnp.where(is_g, jnp.exp(head - gmax), 0.0), axis=0, keepdims=True)
    p_top = 1.0 / gsum
    g_idx = first_row(gl == gmax)
    el = lgt[ROUTE_E0 + (N_GROUPS - 1) * EPG:ROUTE_E0 + N_GROUPS * EPG]
    for g in reversed(range(N_GROUPS - 1)):
        el = jnp.where(g_idx == g, lgt[ROUTE_E0 + g * EPG:ROUTE_E0 + (g + 1) * EPG], el)
    ee = jnp.exp(el - jnp.max(el, axis=0, keepdims=True))
    ep = ee / jnp.sum(ee, axis=0, keepdims=True)
    e1 = jnp.max(ep, axis=0, keepdims=True)
    i1 = first_row(ep == e1)
    ep2 = jnp.where(row == i1, -1.0, ep)
    e2 = jnp.max(ep2, axis=0, keepdims=True)
    i2 = first_row(ep2 == e2)
    den = e1 + e2
    cw8_t = (jnp.where(row == i1, p_top * e1 / den, 0.0)
             + jnp.where(row == i2, p_top * e2 / den, 0.0))

    onehot = (row == g_idx).astype(F32)
    ii = lax.broadcasted_iota(jnp.int32, (RT_TM, RT_TM), 0)
    jj = lax.broadcasted_iota(jnp.int32, (RT_TM, RT_TM), 1)
    earlier = (ii < jj).astype(BF16)
    prefix = jnp.dot(onehot.astype(BF16), earlier, preferred_element_type=F32)
    cnt = jnp.sum(onehot, axis=1, keepdims=True)
    pc_col = jnp.floor((cnt + (PIECE - 1.0)) * (1.0 / PIECE)) * PIECE
    row1 = lax.broadcasted_iota(jnp.int32, (EPG, 1), 0)
    lane1 = lax.broadcasted_iota(jnp.int32, (1, ROUTE_W), 1)
    lo = jnp.zeros((EPG, 1), F32)
    run = jnp.zeros((1, 1), F32)
    pc = jnp.zeros((1, ROUTE_W), F32)
    for g in range(N_GROUPS):
        pc_g = jnp.sum(jnp.where(row1 == g, pc_col, 0.0), axis=0, keepdims=True)
        lo = jnp.where(row1 == g, run, lo)
        pc = jnp.where(lane1 == g, pc_g, pc)
        run = run + pc_g
    dest = jnp.sum(onehot * (prefix + lo), axis=0, keepdims=True)
    info_t = jnp.where(row == 0, g_idx.astype(F32), jnp.where(row == 1, dest, 0.0))

    slab = jnp.concatenate([cw8_t, info_t, jnp.zeros((LANE - 2 * EPG, RT_TM), F32)], axis=0)
    cols = slab.T
    lane = lax.broadcasted_iota(jnp.int32, cols.shape, 1)
    cw8 = jnp.where(lane < EPG, cols, 0.0)
    info = jnp.where(lane < 2, pltpu.roll(cols, LANE - EPG, 1), 0.0)
    return cw8, info, pc, info_t


PLAN_ROWS = 32


def _dispatch_plan(pc_all):
    lane = lax.broadcasted_iota(jnp.int32, (1, ROUTE_W), 1)
    is_g = lane < N_GROUPS
    seg_len = jnp.sum(pc_all, axis=0, keepdims=True)
    seg_pad = jnp.floor((seg_len + (EX_TM - 1.0)) * (1.0 / EX_TM)) * EX_TM
    run = seg_pad + pltpu.roll(seg_pad, 1, 1)
    seg_end_pad = jnp.where(is_g, run + pltpu.roll(run, 2, 1), 0.0)
    seg_start = seg_end_pad - seg_pad
    ti = lax.broadcasted_iota(jnp.int32, (RT_NT, RT_NT), 0)
    tj = lax.broadcasted_iota(jnp.int32, (RT_NT, RT_NT), 1)
    earlier = (tj < ti).astype(BF16)
    within = jnp.dot(earlier, pc_all.astype(BF16), preferred_element_type=F32)
    chunk_off = jnp.where(is_g, seg_start + within, 0.0)

    start = (lax.broadcasted_iota(jnp.int32, (PLAN_ROWS, 1), 0) * EX_TM).astype(F32)
    passed = jnp.where(jnp.logical_and(is_g, start >= seg_end_pad), 1.0, 0.0)
    group = jnp.minimum(jnp.sum(passed, axis=1, keepdims=True), N_GROUPS - 1.0)
    used_end = jnp.sum(jnp.where(lane == group.astype(jnp.int32), seg_start + seg_len, 0.0),
                       axis=1, keepdims=True)
    used = jnp.clip(used_end - start, 0.0, float(EX_TM))
    steps = jnp.where(lane == 0, group, jnp.where(lane == 1, used, 0.0))
    return chunk_off.astype(jnp.int32), pc_all.astype(jnp.int32), steps.astype(jnp.int32)


def _piece_copies(off_ref, pc_ref, tile, make):
    lo = 0
    for g in range(N_GROUPS):
        n = pc_ref[tile, g] // PIECE
        base = off_ref[tile, g]

        def body(j, carry, lo=lo, base=base):
            make(pl.multiple_of(lo + j * PIECE, PIECE), pl.multiple_of(base + j * PIECE, PIECE))
            return carry

        lax.fori_loop(0, n, body, 0)
        lo = lo + pc_ref[tile, g]


def _piece_count(pc_ref, tile):
    n = 0
    for g in range(N_GROUPS):
        n = n + pc_ref[tile, g] // PIECE
    return n


XW = D + ROUTE_W


def _dispatch_kernel(off_ref, pc_ref, h_ref, infot_ref, cw8_ref, xs_in, xs_out, x_scr, sem):
    del xs_in
    i = pl.program_id(0)
    slot = i % 2
    dest = infot_ref[1:2, :]
    row = lax.broadcasted_iota(jnp.int32, (R_LOC, RT_TM), 0).astype(F32)
    sel = (row == dest).astype(BF16)
    cw = cw8_ref[...]
    hi = cw.astype(BF16).astype(F32)
    mid = (cw - hi).astype(BF16).astype(F32)
    low = (cw - hi - mid).astype(BF16).astype(F32)
    pieces = (hi + pltpu.roll(mid, EPG, 1) + pltpu.roll(low, 2 * EPG, 1)).astype(BF16)
    rows = jnp.concatenate([h_ref[...], pieces], axis=1)
    x_scr[slot] = jnp.dot(sel, rows, preferred_element_type=F32).astype(BF16)

    def x_copy(s, src, dst):
        return pltpu.make_async_copy(x_scr.at[s, pl.ds(src, PIECE)], xs_out.at[pl.ds(dst, PIECE)],
                                     sem.at[s])

    _piece_copies(off_ref, pc_ref, i, lambda src, dst: x_copy(slot, src, dst).start())

    def wait_tile(tile, s):
        def wait(j, carry):
            x_copy(s, 0, 0).wait()
            return carry

        lax.fori_loop(0, _piece_count(pc_ref, tile), wait, 0)

    @pl.when(i > 0)
    def _():
        wait_tile(i - 1, 1 - slot)

    @pl.when(i == RT_NT - 1)
    def _():
        wait_tile(i, slot)


def _dispatch(chunk_off, pc, h2, info_t, cw8, xs0):
    grid_spec = pltpu.PrefetchScalarGridSpec(
        num_scalar_prefetch=2,
        grid=(RT_NT,),
        in_specs=[pl.BlockSpec((RT_TM, D), lambda i, o, p: (i, 0)),
                  pl.BlockSpec((None, EPG, RT_TM), lambda i, o, p: (i, 0, 0)),
                  pl.BlockSpec((RT_TM, ROUTE_W), lambda i, o, p: (i, 0)),
                  pl.BlockSpec(memory_space=pl.ANY)],
        out_specs=pl.BlockSpec(memory_space=pl.ANY),
        scratch_shapes=[pltpu.VMEM((2, R_LOC, XW), BF16),
                        pltpu.SemaphoreType.DMA((2,))])
    return pl.pallas_call(
        _dispatch_kernel,
        out_shape=jax.ShapeDtypeStruct((R_MAX, XW), BF16),
        grid_spec=grid_spec,
        input_output_aliases={5: 0},
        compiler_params=_cparams(("arbitrary",)),
        name="dispatch",
    )(chunk_off, pc, h2, info_t, cw8, xs0)


def _expert_kernel(steps_ref, x_ref, wg_ref, wu_ref, wd_ref, y_ref):
    k = pl.program_id(0)
    used = steps_ref[k, 1]
    half = EX_TM // 2

    def run(rows):
        x = x_ref[rows, 0:D]
        cw = x_ref[rows, D:XW].astype(F32)
        lane = lax.broadcasted_iota(jnp.int32, cw.shape, 1)
        acc = jnp.zeros(x.shape, F32)
        for j in range(EPG):
            mine = jnp.logical_and((lane & (EPG - 1)) == j, lane < 3 * EPG)
            w = jnp.sum(jnp.where(mine, cw, 0.0), axis=-1, keepdims=True)
            a = (jax.nn.silu(jnp.dot(x, wg_ref[j], preferred_element_type=F32))
                 * jnp.dot(x, wu_ref[j], preferred_element_type=F32))
            acc = acc + jnp.dot((a * w).astype(BF16), wd_ref[j], preferred_element_type=F32)
        y_ref[rows, :] = acc.astype(BF16)

    @pl.when(used > half)
    def _():
        run(slice(0, EX_TM))

    @pl.when(jnp.logical_and(used > 0, used <= half))
    def _():
        run(slice(0, half))
        y_ref[half:EX_TM, :] = jnp.zeros((EX_TM - half, D), BF16)

    @pl.when(used == 0)
    def _():
        y_ref[...] = jnp.zeros_like(y_ref)


def _experts(steps, xs, wg, wu, wd):
    grid_spec = pltpu.PrefetchScalarGridSpec(
        num_scalar_prefetch=1,
        grid=(EX_NT,),
        in_specs=[pl.BlockSpec((EX_TM, XW), lambda k, st: (k, 0)),
                  pl.BlockSpec((EPG, D, FF), lambda k, st: (st[k, 0], 0, 0)),
                  pl.BlockSpec((EPG, D, FF), lambda k, st: (st[k, 0], 0, 0)),
                  pl.BlockSpec((EPG, FF, D), lambda k, st: (st[k, 0], 0, 0))],
        out_specs=pl.BlockSpec((EX_TM, D), lambda k, st: (k, 0)))
    return pl.pallas_call(
        _expert_kernel,
        out_shape=jax.ShapeDtypeStruct((R_MAX, D), BF16),
        grid_spec=grid_spec,
        compiler_params=_cparams(("arbitrary",)),
        name="experts",
    )(steps, xs, wg, wu, wd)


CB_NPT = N_PROMPT // RT_TM


def _combine_kernel(off_ref, pc_ref, info_ref, x1_ref, mod_ref, g_ref, ys_hbm,
                    yp_ref, ysm_ref, stage, sem):
    i = pl.program_id(0)
    slot = i % 2

    def copy(s, dst, src):
        return pltpu.make_async_copy(ys_hbm.at[pl.ds(src, PIECE)], stage.at[s, pl.ds(dst, PIECE)],
                                     sem.at[s])

    def fetch(tile, s):
        _piece_copies(off_ref, pc_ref, tile, lambda loc, glob: copy(s, loc, glob).start())

    @pl.when(i == 0)
    def _():
        stage[...] = jnp.zeros_like(stage)
        fetch(0, 0)

    @pl.when(i + 1 < RT_NT)
    def _():
        fetch(i + 1, 1 - slot)

    def wait(j, carry):
        copy(slot, 0, 0).wait()
        return carry

    lax.fori_loop(0, _piece_count(pc_ref, i), wait, 0)

    dest = info_ref[...][:, 1:2]
    col = lax.broadcasted_iota(jnp.int32, (RT_TM, R_STAGE), 1).astype(F32)
    moe = jnp.dot((col == dest).astype(BF16), stage[slot], preferred_element_type=F32)
    mod = mod_ref[...]
    out = _rms(x1_ref[...] + mod[:, 5 * D:6 * D] * moe) * g_ref[...]

    @pl.when(i < CB_NPT)
    def _():
        yp_ref[...] = out

    @pl.when(i >= CB_NPT)
    def _():
        ysm_ref[...] = out


def _combine(chunk_off, pc, info, x1, mod3, fg, ys):
    npt = CB_NPT
    grid_spec = pltpu.PrefetchScalarGridSpec(
        num_scalar_prefetch=2,
        grid=(RT_NT,),
        in_specs=[pl.BlockSpec((RT_TM, ROUTE_W), lambda i, o, p: (i, 0)),
                  pl.BlockSpec((RT_TM, D), lambda i, o, p: (i, 0)),
                  pl.BlockSpec((None, 1, 6 * D), lambda i, o, p: (_mod_row(RT_TM)(i), 0, 0)),
                  pl.BlockSpec((1, D), lambda i, o, p: (0, 0)),
                  pl.BlockSpec(memory_space=pl.ANY)],
        out_specs=(pl.BlockSpec((RT_TM, D), lambda i, o, p: (jnp.minimum(i, npt - 1), 0)),
                   pl.BlockSpec((RT_TM, D), lambda i, o, p: (jnp.maximum(i - npt, 0), 0))),
        scratch_shapes=[pltpu.VMEM((2, R_STAGE, D), BF16),
                        pltpu.SemaphoreType.DMA((2,))])
    return pl.pallas_call(
        _combine_kernel,
        out_shape=(jax.ShapeDtypeStruct((N_PROMPT, D), F32),
                   jax.ShapeDtypeStruct((N_SAMPLE, D), F32)),
        grid_spec=grid_spec,
        compiler_params=_cparams(("arbitrary",)),
        name="combine",
    )(chunk_off, pc, info, x1, mod3, fg, ys)


def kernel(x_prompt, x_sample, cache_diff_k, cache_diff_v, state_ret_fwd, state_ret_bwd, c, c_ctx,
           w_ada, b_ada, norm1_g, norm2_g, w_in, ret_decay_fwd, ret_decay_bwd, ret_norm_g,
           diff_lambda_q1, diff_lambda_k1, diff_lambda_q2, diff_lambda_k2, diff_subln_g,
           w_ret_o, w_diff_o, w_o, router_group_w, router_group_b, router_expert_w, router_expert_b,
           moe_w_gate, moe_w_up, moe_w_down, final_norm_g):
    l = 0
    lam_init = 0.8 - 0.6 * math.exp(-0.3 * l)
    lam = (jnp.exp(jnp.sum(diff_lambda_q1[l].astype(F32) * diff_lambda_k1[l].astype(F32)))
           - jnp.exp(jnp.sum(diff_lambda_q2[l].astype(F32) * diff_lambda_k2[l].astype(F32))) + lam_init)
    attn_scal = jnp.stack([lam, jnp.float32(1.0 - lam_init)]).astype(F32)
    lg = jnp.stack([jax.nn.log_sigmoid(ret_decay_fwd[l].astype(F32)),
                    jax.nn.log_sigmoid(ret_decay_bwd[l].astype(F32))])

    xp = x_prompt.reshape(N_PROMPT, D)
    xs = x_sample.reshape(N_SAMPLE, D)
    mod3 = _modulation(c_ctx[None, :], c, w_ada[l], b_ada[l][None, :])

    proj, kt32, v32 = _inproj(xp, xs, mod3, norm1_g[l][None, :], w_in[l])

    cos_t, sin_t = _rope_tables()
    cache_kt = jnp.transpose(cache_diff_k[:, l], (0, 2, 3, 4, 1)).reshape(4, D, PAST)
    (diff_act, ret_act, s_f, s_b, wg_bf, wu_bf, wd_bf, xs0) = _mixers(
        attn_scal, lg, proj, cache_kt, cache_diff_v, cos_t, sin_t, diff_subln_g[l][None, :],
        state_ret_fwd[:, l], state_ret_bwd[:, l], ret_norm_g[l][None, :],
        (moe_w_gate[l], moe_w_up[l], moe_w_down[l]))

    brt = jnp.concatenate([router_group_b[l], jnp.zeros((ROUTE_E0 - N_GROUPS,), F32), router_expert_b[l],
                           jnp.zeros((ROUTE_W - ROUTE_E0 - N_EXP,), F32)])[:, None]
    x1, h2, cw8, info, info_t, chunk_off, pc, steps = _outproj(
        ret_act, diff_act, proj, xp, xs, mod3, norm2_g[l][None, :],
        w_ret_o[l], w_diff_o[l], w_o[l], router_group_w[l], router_expert_w[l], brt)
    xs_sorted = _dispatch(chunk_off, pc, h2, info_t, cw8, xs0)
    y_sorted = _experts(steps, xs_sorted, wg_bf, wu_bf, wd_bf)
    yp, ys = _combine(chunk_off, pc, info, x1, mod3, final_norm_g[None, :], y_sorted)

    return (yp.reshape(16, T_P, D), ys.reshape(4, T_S, D),
            jnp.transpose(kt32.reshape(16, DIFF_H, 2, DIFF_HD, T_P), (0, 4, 1, 2, 3))[:, None],
            v32.reshape(16, 1, T_P, DIFF_H, 2 * DIFF_HD),
            s_f.reshape(16, 1, RET_H, RET_KD, RET_VD), s_b.reshape(16, 1, RET_H, RET_KD, RET_VD))
```

```python
import math

import jax
import jax.numpy as jnp
import numpy as np
from jax import lax
from jax.experimental import pallas as pl
from jax.experimental.pallas import tpu as pltpu

F32 = jnp.float32
BF16 = jnp.bfloat16

D = 1024
N_PROMPT = 16 * 256
N_SAMPLE = 4 * 1024
N_TOK = N_PROMPT + N_SAMPLE
T_P = 256
T_S = 1024
PAST = 512
GRID_W = 64
RET_H = 4
RET_KD = 128
RET_VD = 256
DIFF_H = 8
DIFF_HD = 64
ROPE_BASE = 10000.0
N_GROUPS = 4
EPG = 8
N_EXP = 32
FF = 256
EPS = 1e-6
IN_W = 8192
LANE = 128
N_CHUNK = IN_W // LANE
C_RQ, C_RK, C_RV, C_RG, C_DQ, C_DK, C_DV, C_GR, C_GD = 0, 4, 8, 16, 24, 32, 40, 48, 56
ROUTE_W = 128
SEQ_BLK = 1024
N_PBLK = N_PROMPT // SEQ_BLK
V7X_VMEM_BYTES = 64 * 1024 * 1024
VMEM_LIMIT = V7X_VMEM_BYTES - 8 * 1024 * 1024


def _cparams(sem):
    return pltpu.CompilerParams(dimension_semantics=sem, vmem_limit_bytes=VMEM_LIMIT)


def _mod_row(tile_rows):
    def f(i):
        start = i * tile_rows
        return jnp.where(start < N_PROMPT, 0, 1 + (start - N_PROMPT) // T_S)
    return f


def _rms(x):
    return x * lax.rsqrt(jnp.mean(x * x, axis=-1, keepdims=True) + EPS)


MOD_ROWS = 8


MOD_A = 2 * D
MOD_B = 4 * D


def _mod_columns(ctx_ref, c_ref, w_ref, b_ref, o_ref):
    cond = jnp.concatenate([ctx_ref[...], c_ref[...],
                            jnp.zeros((MOD_ROWS - 1 - c_ref.shape[0], D), F32)], axis=0)
    s = jax.nn.silu(cond)
    out = jnp.dot(s.astype(BF16), w_ref[...].astype(BF16), preferred_element_type=F32) + b_ref[...]
    for r in range(MOD_ROWS):
        o_ref[r] = out[r:r + 1, :]


def _modulation_a(c_ctx, c, w_ada, b_ada):
    tn = 1024
    return pl.pallas_call(
        _mod_columns,
        out_shape=jax.ShapeDtypeStruct((MOD_ROWS, 1, MOD_A), F32),
        grid=(MOD_A // tn,),
        in_specs=[pl.BlockSpec((1, D), lambda j: (0, 0)),
                  pl.BlockSpec(c.shape, lambda j: (0, 0)),
                  pl.BlockSpec((D, tn), lambda j: (0, j)),
                  pl.BlockSpec((1, tn), lambda j: (0, j))],
        out_specs=pl.BlockSpec((MOD_ROWS, 1, tn), lambda j: (0, 0, j)),
        compiler_params=_cparams(("arbitrary",)),
        name="mod",
    )(c_ctx, c, w_ada, b_ada)


IP_TM = 512
IP_TN = 2048
IP_NPT = N_PROMPT // IP_TM


IP_KV_TILE = C_DK * LANE // IP_TN
IP_SPT = IP_TM // T_P


MODB_TN = LANE
MODB_STEPS = MOD_B // MODB_TN


def _inproj_kernel(xp_ref, xs_ref, mod_ref, n1_ref, w_ref, ctx_ref, c_ref, wada_ref, bada_ref,
                   proj_ref, kt_ref, v32_ref, modb_ref, w_scr):
    j = pl.program_id(0)
    i = pl.program_id(1)

    @pl.when(i == 0)
    def _():
        w_scr[...] = w_ref[...].astype(BF16)

    x = jnp.where(i < IP_NPT, xp_ref[...], xs_ref[...])
    mod = mod_ref[...]
    h = (_rms(x) * n1_ref[...] * (1.0 + mod[:, D:2 * D]) + mod[:, 0:D]).astype(BF16)
    acc = jnp.dot(h, w_scr[...], preferred_element_type=F32)
    for c in range(IP_TN // LANE):
        proj_ref[c] = acc[:, c * LANE:(c + 1) * LANE].astype(BF16)

    _mod_columns(ctx_ref, c_ref, wada_ref, bada_ref, modb_ref)

    @pl.when(jnp.logical_and(j == IP_KV_TILE, i < IP_NPT))
    def _():
        for s in range(IP_SPT):
            kt_ref[s] = acc[s * T_P:(s + 1) * T_P, :D].T
        v32_ref[...] = acc[:, D:]


def _inproj(xp, xs, mod_a, n1, w_in, c_ctx, c, w_ada, b_ada):
    npt = IP_NPT
    cpt = IP_TN // LANE
    n_i = N_TOK // IP_TM
    assert MODB_STEPS <= (IN_W // IP_TN) * n_i

    def kv_tile(j, i):
        return jnp.where(j < IP_KV_TILE, 0, jnp.where(j == IP_KV_TILE, jnp.minimum(i, npt - 1), npt - 1))

    modb_blk = lambda j, i: jnp.minimum(j * n_i + i, MODB_STEPS - 1)
    ada_blk = lambda j, i: (0, MOD_A // MODB_TN + modb_blk(j, i))

    return pl.pallas_call(
        _inproj_kernel,
        out_shape=(jax.ShapeDtypeStruct((N_CHUNK, N_TOK, LANE), BF16),
                   jax.ShapeDtypeStruct((N_PROMPT // T_P, D, T_P), F32),
                   jax.ShapeDtypeStruct((N_PROMPT, D), F32),
                   jax.ShapeDtypeStruct((MOD_ROWS, 1, MOD_B), F32)),
        grid=(IN_W // IP_TN, n_i),
        in_specs=[pl.BlockSpec((IP_TM, D), lambda j, i: (jnp.minimum(i, npt - 1), 0)),
                  pl.BlockSpec((IP_TM, D), lambda j, i: (jnp.maximum(i - npt, 0), 0)),
                  pl.BlockSpec((None, 1, MOD_A), lambda j, i: (_mod_row(IP_TM)(i), 0, 0)),
                  pl.BlockSpec((1, D), lambda j, i: (0, 0)),
                  pl.BlockSpec((D, IP_TN), lambda j, i: (0, j)),
                  pl.BlockSpec((1, D), lambda j, i: (0, 0)),
                  pl.BlockSpec(c.shape, lambda j, i: (0, 0)),
                  pl.BlockSpec((D, MODB_TN), ada_blk),
                  pl.BlockSpec((1, MODB_TN), ada_blk)],
        out_specs=(pl.BlockSpec((cpt, IP_TM, LANE), lambda j, i: (j, i, 0)),
                   pl.BlockSpec((IP_SPT, D, T_P), lambda j, i: (kv_tile(j, i), 0, 0)),
                   pl.BlockSpec((IP_TM, D), lambda j, i: (kv_tile(j, i), 0)),
                   pl.BlockSpec((MOD_ROWS, 1, MODB_TN), lambda j, i: (0, 0, modb_blk(j, i)))),
        scratch_shapes=[pltpu.VMEM((D, IP_TN), BF16)],
        compiler_params=_cparams(("arbitrary", "arbitrary")),
        name="inproj",
    )(xp, xs, mod_a, n1, w_in, c_ctx, c, w_ada, b_ada)


def _decay_mask(t, lgf, lgb):
    ii = lax.broadcasted_iota(jnp.int32, (t, t), 0)
    jj = lax.broadcasted_iota(jnp.int32, (t, t), 1)
    rel = (ii - jj).astype(F32)
    e = jnp.exp(jnp.where(rel >= 0.0, lgf, -lgb) * rel)
    return jnp.where(rel == 0.0, 2.0, e) * (RET_KD ** -0.5)


def _ret_parts(lgf, lgb, q_ref, k_ref, v_ref, rg_ref, s0f_ref, s0b_ref, g_ref,
               o_ref, sf_ref, sb_ref, dm_ref):
    gain = g_ref[...]
    nt = (((1,), (1,)), ((), ()))
    tn = (((0,), (0,)), ((), ()))

    def finish(o, rg):
        d = o - jnp.mean(o, axis=-1, keepdims=True)
        y = d * lax.rsqrt(jnp.mean(d * d, axis=-1, keepdims=True) + EPS) * gain
        return (jax.nn.silu(rg.astype(F32)) * y).astype(BF16)

    def vcat(r):
        return jnp.concatenate([v_ref[0, r, :], v_ref[1, r, :]], axis=1)

    def gcat(r):
        return jnp.concatenate([rg_ref[0, r, :], rg_ref[1, r, :]], axis=1)

    def build_mask():
        dm_ref[...] = _decay_mask(T_P, lgf, lgb)

    n_chunk = SEQ_BLK // T_P
    chunks = [slice(c * T_P, (c + 1) * T_P) for c in range(n_chunk)]
    t = lax.broadcasted_iota(jnp.int32, (T_P, 1), 0).astype(F32)
    kdf = jnp.exp(lgf * (T_P - 1.0 - t)) * (RET_KD ** -0.5)
    kdb = jnp.exp(lgb * t) * (RET_KD ** -0.5)

    def intra(r):
        sc = lax.dot_general(q_ref[r, :], k_ref[r, :], nt, preferred_element_type=F32)
        return jnp.dot((sc * dm_ref[...]).astype(BF16), vcat(r), preferred_element_type=F32)

    def key_state(r, kd):
        kw = (k_ref[r, :].astype(F32) * kd).astype(BF16)
        return lax.dot_general(kw, vcat(r), tn, preferred_element_type=F32)

    def context_part():
        for s, r in enumerate(chunks):
            o_ref[r, :] = finish(intra(r), gcat(r))
            sf_ref[s] = key_state(r, kdf)
            sb_ref[s] = key_state(r, kdb)

    def latent_part():
        qdf = jnp.exp(lgf * (t + 1.0))
        qdb = jnp.exp(lgb * (T_P - t))
        span = jnp.full((1, 1), float(T_P), F32)
        cf = jnp.exp(lgf * span)
        cb = jnp.exp(lgb * span)
        before_b = [None] * n_chunk
        state = s0b_ref[...]
        for c in reversed(range(n_chunk)):
            before_b[c] = state
            if c > 0:
                state = cb * state + key_state(chunks[c], kdb)
        state = s0f_ref[...]
        for c, r in enumerate(chunks):
            qf = q_ref[r, :].astype(F32)
            o = (intra(r)
                 + jnp.dot((qf * qdf).astype(BF16), state.astype(BF16), preferred_element_type=F32)
                 + jnp.dot((qf * qdb).astype(BF16), before_b[c].astype(BF16), preferred_element_type=F32))
            o_ref[r, :] = finish(o, gcat(r))
            if c + 1 < n_chunk:
                state = cf * state + key_state(r, kdf)

    return build_mask, context_part, latent_part


ATT_TQ = 1024
ATT_HPS = 2
ATT_QSCALE = (DIFF_HD ** -0.5) * math.log2(math.e)


N_SIDE = 3


def _mixer_kernel(sc_ref, lg_ref, q_ref, k_ref, v_ref, ckt_ref, cv_ref, cos_ref, sin_ref, g_ref,
                  rq_ref, rk_ref, rv_ref, rg_ref, s0f_ref, s0b_ref, gn_ref, *rest):
    side_in = rest[:N_SIDE]
    o_ref, ro_ref, sf_ref, sb_ref = rest[N_SIDE:N_SIDE + 4]
    side_out = rest[N_SIDE + 4:2 * N_SIDE + 4]
    xs0_ref, q_scr, k_scr, v_scr, dm_scr = rest[2 * N_SIDE + 4:]
    i = pl.program_id(0)
    rh = pl.program_id(1)
    build_mask, ret_context, ret_latent = _ret_parts(
        lg_ref[0, rh], lg_ref[1, rh], rq_ref, rk_ref, rv_ref, rg_ref, s0f_ref, s0b_ref, gn_ref,
        ro_ref, sf_ref, sb_ref, dm_scr.at[rh])

    @pl.when(i == 0)
    def _():
        build_mask()

    def side_jobs():
        for src, dst in zip(side_in, side_out):
            dst[...] = src[...].astype(BF16)
        xs0_ref[...] = jnp.zeros_like(xs0_ref)

    lam = sc_ref[0]
    out_scale = sc_ref[1]
    gain = g_ref[...] * out_scale
    lane = lax.broadcasted_iota(jnp.int32, (1, LANE), 1)
    first = lane < DIFF_HD
    nt = (((1,), (1,)), ((), ()))

    def halves(q):
        zero = jnp.zeros_like(q)
        return jnp.where(first, q, zero), jnp.where(first, zero, q)

    def weights(s):
        return jnp.exp2(s - jnp.max(s, axis=-1, keepdims=True)).astype(BF16)

    def finish(of0, of1):
        o = of0[:, :LANE] / of0[:, LANE:] - lam * (of1[:, :LANE] / of1[:, LANE:])
        return (_rms(o) * gain).astype(BF16)

    @pl.when(i < N_PBLK)
    def _():
        ret_context()
        n_seq = SEQ_BLK // T_P
        rows = [slice(s * T_P, (s + 1) * T_P) for s in range(n_seq)]
        ones = jnp.ones((T_P, LANE), BF16)
        for hh in range(ATT_HPS):
            cols = slice(hh * LANE, (hh + 1) * LANE)
            q0, q1 = halves((q_ref[hh].astype(F32) * ATT_QSCALE).astype(BF16))
            s0 = jnp.concatenate([lax.dot_general(q0[r], k_ref[hh, r, :], nt, preferred_element_type=F32)
                                  for r in rows], axis=0)
            s1 = jnp.concatenate([lax.dot_general(q1[r], k_ref[hh, r, :], nt, preferred_element_type=F32)
                                  for r in rows], axis=0)
            e0 = weights(s0)
            e1 = weights(s1)
            for r in rows:
                v1 = jnp.concatenate([v_ref[hh, r, :], ones], axis=1)
                o_ref[r, cols] = finish(jnp.dot(e0[r], v1, preferred_element_type=F32),
                                        jnp.dot(e1[r], v1, preferred_element_type=F32))

    @pl.when(i >= N_PBLK)
    def _():
        side_jobs()
        ret_latent()
        cos = cos_ref[...]
        sin = sin_ref[...]
        low = (lax.broadcasted_iota(jnp.int32, (T_S, LANE), 1) & 16) == 0

        def rope(x):
            xs = jnp.where(low, pltpu.roll(x, LANE - 16, 1), pltpu.roll(x, 16, 1))
            return x * cos + xs * sin

        for hh in range(ATT_HPS):
            cols = slice(hh * LANE, (hh + 1) * LANE)
            head = pl.program_id(1) * ATT_HPS + hh
            q_scr[hh] = (rope(q_ref[hh].astype(F32)) * ATT_QSCALE).astype(BF16)
            k_scr[hh] = rope(k_ref[hh].astype(F32)).astype(BF16)
            ckt = ckt_ref[cols, :].astype(BF16)
            v_scr[hh, 0:T_S, 0:LANE] = v_ref[hh]
            v_scr[hh, T_S:T_S + PAST, 0:LANE] = cv_ref[:, head, :].astype(BF16)
            v_scr[hh, :, LANE:2 * LANE] = jnp.ones((T_S + PAST, LANE), BF16)

            def scores(qh, hh=hh, ckt=ckt):
                return jnp.concatenate([lax.dot_general(qh, k_scr[hh], nt, preferred_element_type=F32),
                                        jnp.dot(qh, ckt, preferred_element_type=F32)], axis=1)

            for b in range(T_S // ATT_TQ):
                r = slice(b * ATT_TQ, (b + 1) * ATT_TQ)
                q0, q1 = halves(q_scr[hh, r, :])
                e0 = weights(scores(q0))
                e1 = weights(scores(q1))
                o_ref[r, cols] = finish(jnp.dot(e0, v_scr[hh], preferred_element_type=F32),
                                        jnp.dot(e1, v_scr[hh], preferred_element_type=F32))


def _mixers(scal, lg, proj, cache_k, cache_v, cos_t, sin_t, subln_g, s0f, s0b, gnorm, side_weights):
    nb = N_TOK // SEQ_BLK
    spb = SEQ_BLK // T_P
    smp = lambda i: jnp.maximum(i - N_PBLK, 0)
    pmt = lambda i: jnp.minimum(i, N_PBLK - 1)
    pmh = lambda i, h: jnp.where(i < N_PBLK, h, RET_H - 1)
    state_in = pl.BlockSpec((None, None, RET_KD, RET_VD), lambda i, h: (smp(i), h, 0, 0))
    state_out = pl.BlockSpec((spb, None, RET_KD, RET_VD), lambda i, h: (pmt(i), pmh(i, h), 0, 0))

    n_hp = DIFF_H // ATT_HPS
    assert n_hp == RET_H
    n_steps = (nb - N_PBLK) * n_hp
    step = lambda i, h: jnp.maximum((i - N_PBLK) * n_hp + h, 0)
    exp_per_step = N_EXP // n_steps
    exp_slice = lambda i, h: (step(i, h), 0, 0)
    row_slice = lambda i, h: (step(i, h), 0)
    up_spec = pl.BlockSpec((exp_per_step, D, FF), exp_slice)
    down_spec = pl.BlockSpec((exp_per_step, FF, D), exp_slice)
    side_specs = [up_spec, up_spec, down_spec]
    side_shapes = [jax.ShapeDtypeStruct(a.shape, BF16) for a in side_weights]
    zrows = R_MAX // n_steps

    return pl.pallas_call(
        _mixer_kernel,
        out_shape=(jax.ShapeDtypeStruct((N_TOK, DIFF_H * 2 * DIFF_HD), BF16),
                   jax.ShapeDtypeStruct((N_TOK, RET_H * RET_VD), BF16),
                   jax.ShapeDtypeStruct((16, RET_H, RET_KD, RET_VD), F32),
                   jax.ShapeDtypeStruct((16, RET_H, RET_KD, RET_VD), F32),
                   *side_shapes,
                   jax.ShapeDtypeStruct((R_MAX, XW), BF16)),
        grid=(nb, n_hp),
        in_specs=[pl.BlockSpec(memory_space=pltpu.SMEM),
                  pl.BlockSpec(memory_space=pltpu.SMEM),
                  pl.BlockSpec((ATT_HPS, SEQ_BLK, LANE), lambda i, h: (C_DQ // ATT_HPS + h, i, 0)),
                  pl.BlockSpec((ATT_HPS, SEQ_BLK, LANE), lambda i, h: (C_DK // ATT_HPS + h, i, 0)),
                  pl.BlockSpec((ATT_HPS, SEQ_BLK, LANE), lambda i, h: (C_DV // ATT_HPS + h, i, 0)),
                  pl.BlockSpec((None, ATT_HPS * LANE, PAST), lambda i, h: (smp(i), h, 0)),
                  pl.BlockSpec((None, None, PAST, DIFF_H, LANE), lambda i, h: (smp(i), 0, 0, 0, 0)),
                  pl.BlockSpec((T_S, LANE), lambda i, h: (0, 0)),
                  pl.BlockSpec((T_S, LANE), lambda i, h: (0, 0)),
                  pl.BlockSpec((1, LANE), lambda i, h: (0, 0)),
                  pl.BlockSpec((None, SEQ_BLK, LANE), lambda i, h: (C_RQ + h, i, 0)),
                  pl.BlockSpec((None, SEQ_BLK, LANE), lambda i, h: (C_RK + h, i, 0)),
                  pl.BlockSpec((2, SEQ_BLK, LANE), lambda i, h: (C_RV // 2 + h, i, 0)),
                  pl.BlockSpec((2, SEQ_BLK, LANE), lambda i, h: (C_RG // 2 + h, i, 0)),
                  state_in, state_in,
                  pl.BlockSpec((1, RET_VD), lambda i, h: (0, h)),
                  *side_specs],
        out_specs=(pl.BlockSpec((SEQ_BLK, ATT_HPS * LANE), lambda i, h: (i, h)),
                   pl.BlockSpec((SEQ_BLK, RET_VD), lambda i, h: (i, h)),
                   state_out, state_out,
                   *side_specs,
                   pl.BlockSpec((zrows, XW), row_slice)),
        scratch_shapes=[pltpu.VMEM((ATT_HPS, T_S, LANE), BF16),
                        pltpu.VMEM((ATT_HPS, T_S, LANE), BF16),
                        pltpu.VMEM((ATT_HPS, T_S + PAST, 2 * LANE), BF16),
                        pltpu.VMEM((RET_H, T_P, T_P), F32)],
        compiler_params=_cparams(("arbitrary", "arbitrary")),
        name="mixers",
    )(scal, lg, proj, proj, proj, cache_k, cache_v, cos_t, sin_t, subln_g,
      proj, proj, proj, proj, s0f, s0b, gnorm, *side_weights)


def _rope_tables():
    n_rows = T_S // GRID_W
    row = np.repeat(np.arange(n_rows), GRID_W).astype(np.float64)
    col = np.tile(np.arange(GRID_W), n_rows).astype(np.float64)
    n_freq = DIFF_HD // 4
    inv = ROPE_BASE ** (-np.arange(n_freq, dtype=np.float64) / n_freq)

    def axis_tables(pos):
        ang = pos[:, None] * inv[None, :]
        c = np.cos(ang)
        s = np.sin(ang)
        return np.concatenate([c, c], axis=-1), np.concatenate([-s, s], axis=-1)

    cr, sr = axis_tables(row)
    cc, sc = axis_tables(col)
    cos_h = np.concatenate([cr, cc], axis=-1)
    sin_h = np.concatenate([sr, sc], axis=-1)
    return (jnp.asarray(np.concatenate([cos_h, cos_h], axis=-1), F32),
            jnp.asarray(np.concatenate([sin_h, sin_h], axis=-1), F32))


OP_TM = 512
OP_NPT = N_PROMPT // OP_TM


def _outproj_kernel(ra_ref, da_ref, gr_ref, gd_ref, xp_ref, xs_ref, mod_ref, n2_ref,
                    wro32_ref, wdo32_ref, wo32_ref, rgw_ref, rew_ref, brt_ref,
                    x1_ref, h2_ref, cw8_ref, info_ref, infot_ref, off_ref, pctab_ref, steps_ref,
                    m_scr, wro_ref, wdo_ref, wo_ref, wrt_ref, pc_scr):
    i = pl.program_id(0)

    @pl.when(i == 0)
    def _():
        wro_ref[...] = wro32_ref[...].astype(BF16)
        wdo_ref[...] = wdo32_ref[...].astype(BF16)
        wo_ref[...] = wo32_ref[...].astype(BF16)
        gap = jnp.zeros((D, ROUTE_E0 - N_GROUPS), F32)
        pad = jnp.zeros((D, ROUTE_W - ROUTE_E0 - N_EXP), F32)
        wr = jnp.concatenate([rgw_ref[...], gap, rew_ref[...], pad], axis=1)
        wrt_ref[...] = wr.T.astype(BF16)

    ret_out = jnp.dot(ra_ref[...], wro_ref[...], preferred_element_type=F32)
    diff_out = jnp.dot(da_ref[...], wdo_ref[...], preferred_element_type=F32)
    for c in range(D // LANE):
        sl = slice(c * LANE, (c + 1) * LANE)
        m = (jax.nn.sigmoid(gr_ref[c].astype(F32)) * ret_out[:, sl]
             + jax.nn.sigmoid(gd_ref[c].astype(F32)) * diff_out[:, sl])
        m_scr[:, sl] = m.astype(BF16)
    mix = jnp.dot(m_scr[...], wo_ref[...], preferred_element_type=F32)
    mod = mod_ref[...]

    x1 = jnp.where(i < OP_NPT, xp_ref[...], xs_ref[...]) + mod[:, 0:D] * mix
    x1_ref[...] = x1
    h2 = (_rms(x1) * n2_ref[...] * (1.0 + mod[:, 2 * D:3 * D]) + mod[:, D:2 * D]).astype(BF16)
    h2_ref[...] = h2
    logits_t = lax.dot_general(wrt_ref[...], h2, (((1,), (1,)), ((), ())),
                               preferred_element_type=F32) + brt_ref[...]
    cw8_ref[...], info_ref[...], pc, infot_ref[...] = _route_cols(logits_t)
    pc_scr[pl.ds(i, 1), :] = pc

    @pl.when(i == RT_NT - 1)
    def _():
        off_ref[...], pctab_ref[...], steps_ref[...] = _dispatch_plan(pc_scr[...])


def _outproj(ret_act, diff_act, proj, xp, xs, mod_rows, n2, wro, wdo, wo, rgw, rew, brt):
    assert OP_TM == RT_TM
    npt = OP_NPT
    full = lambda i: (0, 0)
    once = pl.Buffered(1)
    return pl.pallas_call(
        _outproj_kernel,
        out_shape=(jax.ShapeDtypeStruct((N_TOK, D), F32),
                   jax.ShapeDtypeStruct((N_TOK, D), BF16),
                   jax.ShapeDtypeStruct((N_TOK, ROUTE_W), F32),
                   jax.ShapeDtypeStruct((N_TOK, ROUTE_W), F32),
                   jax.ShapeDtypeStruct((RT_NT, EPG, RT_TM), F32),
                   jax.ShapeDtypeStruct((RT_NT, ROUTE_W), jnp.int32),
                   jax.ShapeDtypeStruct((RT_NT, ROUTE_W), jnp.int32),
                   jax.ShapeDtypeStruct((PLAN_ROWS, ROUTE_W), jnp.int32)),
        grid=(N_TOK // OP_TM,),
        in_specs=[pl.BlockSpec((OP_TM, D), lambda i: (i, 0)),
                  pl.BlockSpec((OP_TM, D), lambda i: (i, 0)),
                  pl.BlockSpec((8, OP_TM, LANE), lambda i: (C_GR // 8, i, 0)),
                  pl.BlockSpec((8, OP_TM, LANE), lambda i: (C_GD // 8, i, 0)),
                  pl.BlockSpec((OP_TM, D), lambda i: (jnp.minimum(i, npt - 1), 0)),
                  pl.BlockSpec((OP_TM, D), lambda i: (jnp.maximum(i - npt, 0), 0)),
                  pl.BlockSpec((None, 1, MOD_B), lambda i: (_mod_row(OP_TM)(i), 0, 0)),
                  pl.BlockSpec((1, D), full),
                  pl.BlockSpec((D, D), full, pipeline_mode=once),
                  pl.BlockSpec((D, D), full, pipeline_mode=once),
                  pl.BlockSpec((D, D), full, pipeline_mode=once),
                  pl.BlockSpec((D, N_GROUPS), full, pipeline_mode=once),
                  pl.BlockSpec((D, N_EXP), full, pipeline_mode=once),
                  pl.BlockSpec((ROUTE_W, 1), full)],
        out_specs=(pl.BlockSpec((OP_TM, D), lambda i: (i, 0)),
                   pl.BlockSpec((OP_TM, D), lambda i: (i, 0)),
                   pl.BlockSpec((OP_TM, ROUTE_W), lambda i: (i, 0)),
                   pl.BlockSpec((OP_TM, ROUTE_W), lambda i: (i, 0)),
                   pl.BlockSpec((None, EPG, RT_TM), lambda i: (i, 0, 0)),
                   pl.BlockSpec((RT_NT, ROUTE_W), full),
                   pl.BlockSpec((RT_NT, ROUTE_W), full),
                   pl.BlockSpec((PLAN_ROWS, ROUTE_W), full)),
        scratch_shapes=[pltpu.VMEM((OP_TM, D), BF16),
                        pltpu.VMEM((D, D), BF16), pltpu.VMEM((D, D), BF16), pltpu.VMEM((D, D), BF16),
                        pltpu.VMEM((ROUTE_W, D), BF16),
                        pltpu.VMEM((RT_NT, ROUTE_W), F32)],
        compiler_params=_cparams(("arbitrary",)),
        name="outproj",
    )(ret_act, diff_act, proj, proj, xp, xs, mod_rows, n2, wro, wdo, wo, rgw, rew, brt)


RT_TM = 512
RT_NT = N_TOK // RT_TM
PIECE = 16
R_LOC = RT_TM + N_GROUPS * PIECE
R_STAGE = 640
EX_TM = 512
R_MAX = 11264
EX_NT = R_MAX // EX_TM


ROUTE_E0 = EPG


def _route_cols(lgt):
    row = lax.broadcasted_iota(jnp.int32, (EPG, RT_TM), 0)
    neg = jnp.float32(-jnp.inf)

    def first_row(cond):
        return jnp.min(jnp.where(cond, row, EPG), axis=0, keepdims=True)

    head = lgt[0:EPG]
    is_g = row < N_GROUPS
    gl = jnp.where(is_g, head, neg)
    gmax = jnp.max(gl, axis=0, keepdims=True)
    gsum = jnp.sum(jnp.where(is_g, jnp.exp(head - gmax), 0.0), axis=0, keepdims=True)
    p_top = 1.0 / gsum
    g_idx = first_row(gl == gmax)
    el = lgt[ROUTE_E0 + (N_GROUPS - 1) * EPG:ROUTE_E0 + N_GROUPS * EPG]
    for g in reversed(range(N_GROUPS - 1)):
        el = jnp.where(g_idx == g, lgt[ROUTE_E0 + g * EPG:ROUTE_E0 + (g + 1) * EPG], el)
    ee = jnp.exp(el - jnp.max(el, axis=0, keepdims=True))
    ep = ee / jnp.sum(ee, axis=0, keepdims=True)
    e1 = jnp.max(ep, axis=0, keepdims=True)
    i1 = first_row(ep == e1)
    ep2 = jnp.where(row == i1, -1.0, ep)
    e2 = jnp.max(ep2, axis=0, keepdims=True)
    i2 = first_row(ep2 == e2)
    den = e1 + e2
    cw8_t = (jnp.where(row == i1, p_top * e1 / den, 0.0)
             + jnp.where(row == i2, p_top * e2 / den, 0.0))

    onehot = (row == g_idx).astype(F32)
    ii = lax.broadcasted_iota(jnp.int32, (RT_TM, RT_TM), 0)
    jj = lax.broadcasted_iota(jnp.int32, (RT_TM, RT_TM), 1)
    earlier = (ii < jj).astype(BF16)
    prefix = jnp.dot(onehot.astype(BF16), earlier, preferred_element_type=F32)
    cnt = jnp.sum(onehot, axis=1, keepdims=True)
    pc_col = jnp.floor((cnt + (PIECE - 1.0)) * (1.0 / PIECE)) * PIECE
    row1 = lax.broadcasted_iota(jnp.int32, (EPG, 1), 0)
    lane1 = lax.broadcasted_iota(jnp.int32, (1, ROUTE_W), 1)
    lo = jnp.zeros((EPG, 1), F32)
    run = jnp.zeros((1, 1), F32)
    pc = jnp.zeros((1, ROUTE_W), F32)
    for g in range(N_GROUPS):
        pc_g = jnp.sum(jnp.where(row1 == g, pc_col, 0.0), axis=0, keepdims=True)
        lo = jnp.where(row1 == g, run, lo)
        pc = jnp.where(lane1 == g, pc_g, pc)
        run = run + pc_g
    dest = jnp.sum(onehot * (prefix + lo), axis=0, keepdims=True)
    info_t = jnp.where(row == 0, g_idx.astype(F32), jnp.where(row == 1, dest, 0.0))

    slab = jnp.concatenate([cw8_t, info_t, jnp.zeros((LANE - 2 * EPG, RT_TM), F32)], axis=0)
    cols = slab.T
    lane = lax.broadcasted_iota(jnp.int32, cols.shape, 1)
    cw8 = jnp.where(lane < EPG, cols, 0.0)
    info = jnp.where(lane < 2, pltpu.roll(cols, LANE - EPG, 1), 0.0)
    return cw8, info, pc, info_t


PLAN_ROWS = 32


def _dispatch_plan(pc_all):
    lane = lax.broadcasted_iota(jnp.int32, (1, ROUTE_W), 1)
    is_g = lane < N_GROUPS
    seg_len = jnp.sum(pc_all, axis=0, keepdims=True)
    seg_pad = jnp.floor((seg_len + (EX_TM - 1.0)) * (1.0 / EX_TM)) * EX_TM
    run = seg_pad + pltpu.roll(seg_pad, 1, 1)
    seg_end_pad = jnp.where(is_g, run + pltpu.roll(run, 2, 1), 0.0)
    seg_start = seg_end_pad - seg_pad
    ti = lax.broadcasted_iota(jnp.int32, (RT_NT, RT_NT), 0)
    tj = lax.broadcasted_iota(jnp.int32, (RT_NT, RT_NT), 1)
    earlier = (tj < ti).astype(BF16)
    within = jnp.dot(earlier, pc_all.astype(BF16), preferred_element_type=F32)
    chunk_off = jnp.where(is_g, seg_start + within, 0.0)

    start = (lax.broadcasted_iota(jnp.int32, (PLAN_ROWS, 1), 0) * EX_TM).astype(F32)
    passed = jnp.where(jnp.logical_and(is_g, start >= seg_end_pad), 1.0, 0.0)
    group = jnp.minimum(jnp.sum(passed, axis=1, keepdims=True), N_GROUPS - 1.0)
    used_end = jnp.sum(jnp.where(lane == group.astype(jnp.int32), seg_start + seg_len, 0.0),
                       axis=1, keepdims=True)
    used = jnp.clip(used_end - start, 0.0, float(EX_TM))
    steps = jnp.where(lane == 0, group, jnp.where(lane == 1, used, 0.0))
    return chunk_off.astype(jnp.int32), pc_all.astype(jnp.int32), steps.astype(jnp.int32)


def _piece_copies(off_ref, pc_ref, tile, make):
    lo = 0
    for g in range(N_GROUPS):
        n = pc_ref[tile, g] // PIECE
        base = off_ref[tile, g]

        def body(j, carry, lo=lo, base=base):
            make(pl.multiple_of(lo + j * PIECE, PIECE), pl.multiple_of(base + j * PIECE, PIECE))
            return carry

        lax.fori_loop(0, n, body, 0)
        lo = lo + pc_ref[tile, g]


def _piece_count(pc_ref, tile):
    n = 0
    for g in range(N_GROUPS):
        n = n + pc_ref[tile, g] // PIECE
    return n


XW = D + ROUTE_W


def _dispatch_kernel(off_ref, pc_ref, h_ref, infot_ref, cw8_ref, xs_in, xs_out, x_scr, sem):
    del xs_in
    i = pl.program_id(0)
    slot = i % 2
    dest = infot_ref[1:2, :]
    row = lax.broadcasted_iota(jnp.int32, (R_LOC, RT_TM), 0).astype(F32)
    sel = (row == dest).astype(BF16)
    cw = cw8_ref[...]
    hi = cw.astype(BF16).astype(F32)
    mid = (cw - hi).astype(BF16).astype(F32)
    low = (cw - hi - mid).astype(BF16).astype(F32)
    pieces = (hi + pltpu.roll(mid, EPG, 1) + pltpu.roll(low, 2 * EPG, 1)).astype(BF16)
    rows = jnp.concatenate([h_ref[...], pieces], axis=1)
    x_scr[slot] = jnp.dot(sel, rows, preferred_element_type=F32).astype(BF16)

    def x_copy(s, src, dst):
        return pltpu.make_async_copy(x_scr.at[s, pl.ds(src, PIECE)], xs_out.at[pl.ds(dst, PIECE)],
                                     sem.at[s])

    _piece_copies(off_ref, pc_ref, i, lambda src, dst: x_copy(slot, src, dst).start())

    def wait_tile(tile, s):
        def wait(j, carry):
            x_copy(s, 0, 0).wait()
            return carry

        lax.fori_loop(0, _piece_count(pc_ref, tile), wait, 0)

    @pl.when(i > 0)
    def _():
        wait_tile(i - 1, 1 - slot)

    @pl.when(i == RT_NT - 1)
    def _():
        wait_tile(i, slot)


def _dispatch(chunk_off, pc, h2, info_t, cw8, xs0):
    grid_spec = pltpu.PrefetchScalarGridSpec(
        num_scalar_prefetch=2,
        grid=(RT_NT,),
        in_specs=[pl.BlockSpec((RT_TM, D), lambda i, o, p: (i, 0)),
                  pl.BlockSpec((None, EPG, RT_TM), lambda i, o, p: (i, 0, 0)),
                  pl.BlockSpec((RT_TM, ROUTE_W), lambda i, o, p: (i, 0)),
                  pl.BlockSpec(memory_space=pl.ANY)],
        out_specs=pl.BlockSpec(memory_space=pl.ANY),
        scratch_shapes=[pltpu.VMEM((2, R_LOC, XW), BF16),
                        pltpu.SemaphoreType.DMA((2,))])
    return pl.pallas_call(
        _dispatch_kernel,
        out_shape=jax.ShapeDtypeStruct((R_MAX, XW), BF16),
        grid_spec=grid_spec,
        input_output_aliases={5: 0},
        compiler_params=_cparams(("arbitrary",)),
        name="dispatch",
    )(chunk_off, pc, h2, info_t, cw8, xs0)


def _expert_kernel(steps_ref, x_ref, wg_ref, wu_ref, wd_ref, y_ref):
    k = pl.program_id(0)
    used = steps_ref[k, 1]
    half = EX_TM // 2

    def run(rows):
        x = x_ref[rows, 0:D]
        cw = x_ref[rows, D:XW].astype(F32)
        lane = lax.broadcasted_iota(jnp.int32, cw.shape, 1)
        acc = jnp.zeros(x.shape, F32)
        for j in range(EPG):
            mine = jnp.logical_and((lane & (EPG - 1)) == j, lane < 3 * EPG)
            w = jnp.sum(jnp.where(mine, cw, 0.0), axis=-1, keepdims=True)
            a = (jax.nn.silu(jnp.dot(x, wg_ref[j], preferred_element_type=F32))
                 * jnp.dot(x, wu_ref[j], preferred_element_type=F32))
            acc = acc + jnp.dot((a * w).astype(BF16), wd_ref[j], preferred_element_type=F32)
        y_ref[rows, :] = acc.astype(BF16)

    @pl.when(used > half)
    def _():
        run(slice(0, EX_TM))

    @pl.when(jnp.logical_and(used > 0, used <= half))
    def _():
        run(slice(0, half))
        y_ref[half:EX_TM, :] = jnp.zeros((EX_TM - half, D), BF16)

    @pl.when(used == 0)
    def _():
        y_ref[...] = jnp.zeros_like(y_ref)


def _experts(steps, xs, wg, wu, wd):
    grid_spec = pltpu.PrefetchScalarGridSpec(
        num_scalar_prefetch=1,
        grid=(EX_NT,),
        in_specs=[pl.BlockSpec((EX_TM, XW), lambda k, st: (k, 0)),
                  pl.BlockSpec((EPG, D, FF), lambda k, st: (st[k, 0], 0, 0)),
                  pl.BlockSpec((EPG, D, FF), lambda k, st: (st[k, 0], 0, 0)),
                  pl.BlockSpec((EPG, FF, D), lambda k, st: (st[k, 0], 0, 0))],
        out_specs=pl.BlockSpec((EX_TM, D), lambda k, st: (k, 0)))
    return pl.pallas_call(
        _expert_kernel,
        out_shape=jax.ShapeDtypeStruct((R_MAX, D), BF16),
        grid_spec=grid_spec,
        compiler_params=_cparams(("arbitrary",)),
        name="experts",
    )(steps, xs, wg, wu, wd)


CB_NPT = N_PROMPT // RT_TM


def _combine_kernel(off_ref, pc_ref, info_ref, x1_ref, mod_ref, g_ref, ys_hbm,
                    yp_ref, ysm_ref, stage, sem):
    i = pl.program_id(0)
    slot = i % 2

    def copy(s, dst, src):
        return pltpu.make_async_copy(ys_hbm.at[pl.ds(src, PIECE)], stage.at[s, pl.ds(dst, PIECE)],
                                     sem.at[s])

    def fetch(tile, s):
        _piece_copies(off_ref, pc_ref, tile, lambda loc, glob: copy(s, loc, glob).start())

    @pl.when(i == 0)
    def _():
        stage[...] = jnp.zeros_like(stage)
        fetch(0, 0)

    @pl.when(i + 1 < RT_NT)
    def _():
        fetch(i + 1, 1 - slot)

    def wait(j, carry):
        copy(slot, 0, 0).wait()
        return carry

    lax.fori_loop(0, _piece_count(pc_ref, i), wait, 0)

    dest = info_ref[...][:, 1:2]
    col = lax.broadcasted_iota(jnp.int32, (RT_TM, R_STAGE), 1).astype(F32)
    moe = jnp.dot((col == dest).astype(BF16), stage[slot], preferred_element_type=F32)
    mod = mod_ref[...]
    out = _rms(x1_ref[...] + mod[:, 3 * D:4 * D] * moe) * g_ref[...]

    @pl.when(i < CB_NPT)
    def _():
        yp_ref[...] = out

    @pl.when(i >= CB_NPT)
    def _():
        ysm_ref[...] = out


def _combine(chunk_off, pc, info, x1, mod_rows, fg, ys):
    npt = CB_NPT
    grid_spec = pltpu.PrefetchScalarGridSpec(
        num_scalar_prefetch=2,
        grid=(RT_NT,),
        in_specs=[pl.BlockSpec((RT_TM, ROUTE_W), lambda i, o, p: (i, 0)),
                  pl.BlockSpec((RT_TM, D), lambda i, o, p: (i, 0)),
                  pl.BlockSpec((None, 1, MOD_B), lambda i, o, p: (_mod_row(RT_TM)(i), 0, 0)),
                  pl.BlockSpec((1, D), lambda i, o, p: (0, 0)),
                  pl.BlockSpec(memory_space=pl.ANY)],
        out_specs=(pl.BlockSpec((RT_TM, D), lambda i, o, p: (jnp.minimum(i, npt - 1), 0)),
                   pl.BlockSpec((RT_TM, D), lambda i, o, p: (jnp.maximum(i - npt, 0), 0))),
        scratch_shapes=[pltpu.VMEM((2, R_STAGE, D), BF16),
                        pltpu.SemaphoreType.DMA((2,))])
    return pl.pallas_call(
        _combine_kernel,
        out_shape=(jax.ShapeDtypeStruct((N_PROMPT, D), F32),
                   jax.ShapeDtypeStruct((N_SAMPLE, D), F32)),
        grid_spec=grid_spec,
        compiler_params=_cparams(("arbitrary",)),
        name="combine",
    )(chunk_off, pc, info, x1, mod_rows, fg, ys)


def kernel(x_prompt, x_sample, cache_diff_k, cache_diff_v, state_ret_fwd, state_ret_bwd, c, c_ctx,
           w_ada, b_ada, norm1_g, norm2_g, w_in, ret_decay_fwd, ret_decay_bwd, ret_norm_g,
           diff_lambda_q1, diff_lambda_k1, diff_lambda_q2, diff_lambda_k2, diff_subln_g,
           w_ret_o, w_diff_o, w_o, router_group_w, router_group_b, router_expert_w, router_expert_b,
           moe_w_gate, moe_w_up, moe_w_down, final_norm_g):
    l = 0
    lam_init = 0.8 - 0.6 * math.exp(-0.3 * l)
    lam = (jnp.exp(jnp.sum(diff_lambda_q1[l].astype(F32) * diff_lambda_k1[l].astype(F32)))
           - jnp.exp(jnp.sum(diff_lambda_q2[l].astype(F32) * diff_lambda_k2[l].astype(F32))) + lam_init)
    attn_scal = jnp.stack([lam, jnp.float32(1.0 - lam_init)]).astype(F32)
    lg = jnp.stack([jax.nn.log_sigmoid(ret_decay_fwd[l].astype(F32)),
                    jax.nn.log_sigmoid(ret_decay_bwd[l].astype(F32))])

    xp = x_prompt.reshape(N_PROMPT, D)
    xs = x_sample.reshape(N_SAMPLE, D)
    mod_a = _modulation_a(c_ctx[None, :], c, w_ada[l], b_ada[l][None, :])

    proj, kt32, v32, mod_b = _inproj(xp, xs, mod_a, norm1_g[l][None, :], w_in[l],
                                     c_ctx[None, :], c, w_ada[l], b_ada[l][None, :])

    cos_t, sin_t = _rope_tables()
    cache_kt = jnp.transpose(cache_diff_k[:, l], (0, 2, 3, 4, 1)).reshape(4, D, PAST)
    (diff_act, ret_act, s_f, s_b, wg_bf, wu_bf, wd_bf, xs0) = _mixers(
        attn_scal, lg, proj, cache_kt, cache_diff_v, cos_t, sin_t, diff_subln_g[l][None, :],
        state_ret_fwd[:, l], state_ret_bwd[:, l], ret_norm_g[l][None, :],
        (moe_w_gate[l], moe_w_up[l], moe_w_down[l]))

    brt = jnp.concatenate([router_group_b[l], jnp.zeros((ROUTE_E0 - N_GROUPS,), F32), router_expert_b[l],
                           jnp.zeros((ROUTE_W - ROUTE_E0 - N_EXP,), F32)])[:, None]
    x1, h2, cw8, info, info_t, chunk_off, pc, steps = _outproj(
        ret_act, diff_act, proj, xp, xs, mod_b, norm2_g[l][None, :],
        w_ret_o[l], w_diff_o[l], w_o[l], router_group_w[l], router_expert_w[l], brt)
    xs_sorted = _dispatch(chunk_off, pc, h2, info_t, cw8, xs0)
    y_sorted = _experts(steps, xs_sorted, wg_bf, wu_bf, wd_bf)
    yp, ys = _combine(chunk_off, pc, info, x1, mod_b, final_norm_g[None, :], y_sorted)

    return (yp.reshape(16, T_P, D), ys.reshape(4, T_S, D),
            jnp.transpose(kt32.reshape(16, DIFF_H, 2, DIFF_HD, T_P), (0, 4, 1, 2, 3))[:, None],
            v32.reshape(16, 1, T_P, DIFF_H, 2 * DIFF_HD),
            s_f.reshape(16, 1, RET_H, RET_KD, RET_VD), s_b.reshape(16, 1, RET_H, RET_KD, RET_VD))
```

```python
import math

import jax
import jax.numpy as jnp
import numpy as np
from jax import lax
from jax.experimental import pallas as pl
from jax.experimental.pallas import tpu as pltpu

F32 = jnp.float32
BF16 = jnp.bfloat16

D = 1024
N_PROMPT = 16 * 256
N_SAMPLE = 4 * 1024
N_TOK = N_PROMPT + N_SAMPLE
T_P = 256
T_S = 1024
PAST = 512
GRID_W = 64
RET_H = 4
RET_KD = 128
RET_VD = 256
DIFF_H = 8
DIFF_HD = 64
ROPE_BASE = 10000.0
N_GROUPS = 4
EPG = 8
N_EXP = 32
FF = 256
EPS = 1e-6
IN_W = 8192
LANE = 128
N_CHUNK = IN_W // LANE
C_RQ, C_RK, C_RV, C_RG, C_DQ, C_DK, C_DV, C_GR, C_GD = 0, 4, 8, 16, 24, 32, 40, 48, 56
ROUTE_W = 128
SEQ_BLK = 1024
N_PBLK = N_PROMPT // SEQ_BLK
V7X_VMEM_BYTES = 64 * 1024 * 1024
VMEM_LIMIT = V7X_VMEM_BYTES - 8 * 1024 * 1024


def _cparams(sem):
    return pltpu.CompilerParams(dimension_semantics=sem, vmem_limit_bytes=VMEM_LIMIT)


def _mod_row(tile_rows):
    def f(i):
        start = i * tile_rows
        return jnp.where(start < N_PROMPT, 0, 1 + (start - N_PROMPT) // T_S)
    return f


def _rms(x):
    return x * lax.rsqrt(jnp.mean(x * x, axis=-1, keepdims=True) + EPS)


MOD_ROWS = 8


def _mod_kernel(ctx_ref, c_ref, w_ref, b_ref, o_ref):
    cond = jnp.concatenate([ctx_ref[...], c_ref[...],
                            jnp.zeros((MOD_ROWS - 1 - c_ref.shape[0], D), F32)], axis=0)
    s = jax.nn.silu(cond)
    out = jnp.dot(s.astype(BF16), w_ref[...].astype(BF16), preferred_element_type=F32) + b_ref[...]
    for r in range(MOD_ROWS):
        o_ref[r] = out[r:r + 1, :]


def _modulation(c_ctx, c, w_ada, b_ada):
    tn = 1536
    return pl.pallas_call(
        _mod_kernel,
        out_shape=jax.ShapeDtypeStruct((MOD_ROWS, 1, 6 * D), F32),
        grid=(6 * D // tn,),
        in_specs=[pl.BlockSpec((1, D), lambda j: (0, 0)),
                  pl.BlockSpec(c.shape, lambda j: (0, 0)),
                  pl.BlockSpec((D, tn), lambda j: (0, j)),
                  pl.BlockSpec((1, tn), lambda j: (0, j))],
        out_specs=pl.BlockSpec((MOD_ROWS, 1, tn), lambda j: (0, 0, j)),
        compiler_params=_cparams(("arbitrary",)),
        name="mod",
    )(c_ctx, c, w_ada, b_ada)


IP_TM = 512
IP_TN = 2048
IP_NPT = N_PROMPT // IP_TM


IP_KV_TILE = C_DK * LANE // IP_TN
IP_SPT = IP_TM // T_P


IP_NT = N_TOK // IP_TM
IP_EDGES = 16


def _ordered_after(dst, src):
    sub = 16
    m = None
    for a in range(src.shape[0] // sub):
        for b in range(src.shape[1] // LANE):
            part = src[a * sub:(a + 1) * sub, b * LANE:(b + 1) * LANE]
            m = part if m is None else jnp.maximum(m, part)
    never = jnp.logical_and(m != m, m == m)
    return jnp.where(jnp.tile(never, (dst.shape[0] // sub, 1)), jnp.zeros_like(dst), dst)


def _inproj_kernel(x0_ref, mod0_ref, xp_ref, xs_ref, mod_ref, n1_ref, w_ref, proj_ref, kt_ref, v32_ref,
                   w_scr, h_even, h_odd):
    j = pl.program_id(0)
    i = pl.program_id(1)

    def normed(x, mod):
        return _rms(x) * n1_ref[...] * (1.0 + mod[:, D:2 * D]) + mod[:, 0:D]

    @pl.when(i == 0)
    def _():
        w_scr[...] = w_ref[...].astype(BF16)

    @pl.when(jnp.logical_and(j == 0, i == 0))
    def _():
        h_even[...] = normed(x0_ref[...], mod0_ref[...]).astype(BF16)

    def step(h_cur, h_nxt):
        nxt = (i + 1) % IP_NT
        hn = normed(jnp.where(nxt < IP_NPT, xp_ref[...], xs_ref[...]), mod_ref[...]).astype(BF16)
        h_nxt[...] = hn
        acc = jnp.dot(h_cur[...], w_scr[...], preferred_element_type=F32)
        n_c = IP_TN // LANE
        blk = IP_TM // IP_EDGES
        for c in range(n_c):
            chunk = acc[:, c * LANE:(c + 1) * LANE].astype(BF16)
            r = c - (n_c - IP_EDGES)
            if r >= 0:
                chunk = _ordered_after(chunk, hn[r * blk:(r + 1) * blk, :])
            proj_ref[c] = chunk

        @pl.when(jnp.logical_and(j == IP_KV_TILE, i < IP_NPT))
        def _():
            for s in range(IP_SPT):
                kt_ref[s] = acc[s * T_P:(s + 1) * T_P, :D].T
            v32_ref[...] = acc[:, D:]

    @pl.when(i % 2 == 0)
    def _():
        step(h_even, h_odd)

    @pl.when(i % 2 == 1)
    def _():
        step(h_odd, h_even)


def _inproj(xp, xs, mod3, n1, w_in):
    npt = IP_NPT
    cpt = IP_TN // LANE
    assert IP_NT % 2 == 0

    def kv_tile(j, i):
        return jnp.where(j < IP_KV_TILE, 0, jnp.where(j == IP_KV_TILE, jnp.minimum(i, npt - 1), npt - 1))

    nxt = lambda i: (i + 1) % IP_NT
    once = pl.Buffered(1)
    return pl.pallas_call(
        _inproj_kernel,
        out_shape=(jax.ShapeDtypeStruct((N_CHUNK, N_TOK, LANE), BF16),
                   jax.ShapeDtypeStruct((N_PROMPT // T_P, D, T_P), F32),
                   jax.ShapeDtypeStruct((N_PROMPT, D), F32)),
        grid=(IN_W // IP_TN, IP_NT),
        in_specs=[pl.BlockSpec((IP_TM, D), lambda j, i: (0, 0), pipeline_mode=once),
                  pl.BlockSpec((None, 1, 6 * D), lambda j, i: (0, 0, 0), pipeline_mode=once),
                  pl.BlockSpec((IP_TM, D), lambda j, i: (jnp.minimum(nxt(i), npt - 1), 0)),
                  pl.BlockSpec((IP_TM, D), lambda j, i: (jnp.maximum(nxt(i) - npt, 0), 0)),
                  pl.BlockSpec((None, 1, 6 * D), lambda j, i: (_mod_row(IP_TM)(nxt(i)), 0, 0)),
                  pl.BlockSpec((1, D), lambda j, i: (0, 0)),
                  pl.BlockSpec((D, IP_TN), lambda j, i: (0, j))],
        out_specs=(pl.BlockSpec((cpt, IP_TM, LANE), lambda j, i: (j, i, 0)),
                   pl.BlockSpec((IP_SPT, D, T_P), lambda j, i: (kv_tile(j, i), 0, 0)),
                   pl.BlockSpec((IP_TM, D), lambda j, i: (kv_tile(j, i), 0))),
        scratch_shapes=[pltpu.VMEM((D, IP_TN), BF16),
                        pltpu.VMEM((IP_TM, D), BF16), pltpu.VMEM((IP_TM, D), BF16)],
        compiler_params=_cparams(("arbitrary", "arbitrary")),
        name="inproj",
    )(xp, mod3, xp, xs, mod3, n1, w_in)


def _decay_mask(t, lgf, lgb):
    ii = lax.broadcasted_iota(jnp.int32, (t, t), 0)
    jj = lax.broadcasted_iota(jnp.int32, (t, t), 1)
    rel = (ii - jj).astype(F32)
    e = jnp.exp(jnp.where(rel >= 0.0, lgf, -lgb) * rel)
    return jnp.where(rel == 0.0, 2.0, e) * (RET_KD ** -0.5)


def _ret_parts(lgf, lgb, q_ref, k_ref, v_ref, rg_ref, s0f_ref, s0b_ref, g_ref,
               o_ref, sf_ref, sb_ref, dm_ref):
    gain = g_ref[...]
    nt = (((1,), (1,)), ((), ()))
    tn = (((0,), (0,)), ((), ()))

    def finish(o, rg):
        d = o - jnp.mean(o, axis=-1, keepdims=True)
        y = d * lax.rsqrt(jnp.mean(d * d, axis=-1, keepdims=True) + EPS) * gain
        return (jax.nn.silu(rg.astype(F32)) * y).astype(BF16)

    def vcat(r):
        return jnp.concatenate([v_ref[0, r, :], v_ref[1, r, :]], axis=1)

    def gcat(r):
        return jnp.concatenate([rg_ref[0, r, :], rg_ref[1, r, :]], axis=1)

    def build_mask():
        dm_ref[...] = _decay_mask(T_P, lgf, lgb)

    n_chunk = SEQ_BLK // T_P
    chunks = [slice(c * T_P, (c + 1) * T_P) for c in range(n_chunk)]
    t = lax.broadcasted_iota(jnp.int32, (T_P, 1), 0).astype(F32)
    kdf = jnp.exp(lgf * (T_P - 1.0 - t)) * (RET_KD ** -0.5)
    kdb = jnp.exp(lgb * t) * (RET_KD ** -0.5)

    def intra(r):
        sc = lax.dot_general(q_ref[r, :], k_ref[r, :], nt, preferred_element_type=F32)
        return jnp.dot((sc * dm_ref[...]).astype(BF16), vcat(r), preferred_element_type=F32)

    def key_state(r, kd):
        kw = (k_ref[r, :].astype(F32) * kd).astype(BF16)
        return lax.dot_general(kw, vcat(r), tn, preferred_element_type=F32)

    def context_part():
        for s, r in enumerate(chunks):
            o_ref[r, :] = finish(intra(r), gcat(r))
            sf_ref[s] = key_state(r, kdf)
            sb_ref[s] = key_state(r, kdb)

    def latent_part():
        qdf = jnp.exp(lgf * (t + 1.0))
        qdb = jnp.exp(lgb * (T_P - t))
        span = jnp.full((1, 1), float(T_P), F32)
        cf = jnp.exp(lgf * span)
        cb = jnp.exp(lgb * span)
        before_b = [None] * n_chunk
        state = s0b_ref[...]
        for c in reversed(range(n_chunk)):
            before_b[c] = state
            if c > 0:
                state = cb * state + key_state(chunks[c], kdb)
        state = s0f_ref[...]
        for c, r in enumerate(chunks):
            qf = q_ref[r, :].astype(F32)
            o = (intra(r)
                 + jnp.dot((qf * qdf).astype(BF16), state.astype(BF16), preferred_element_type=F32)
                 + jnp.dot((qf * qdb).astype(BF16), before_b[c].astype(BF16), preferred_element_type=F32))
            o_ref[r, :] = finish(o, gcat(r))
            if c + 1 < n_chunk:
                state = cf * state + key_state(r, kdf)

    return build_mask, context_part, latent_part


ATT_TQ = 1024
ATT_HPS = 2
ATT_QSCALE = (DIFF_HD ** -0.5) * math.log2(math.e)


N_SIDE = 3


def _mixer_kernel(sc_ref, lg_ref, q_ref, k_ref, v_ref, ckt_ref, cv_ref, cos_ref, sin_ref, g_ref,
                  rq_ref, rk_ref, rv_ref, rg_ref, s0f_ref, s0b_ref, gn_ref, *rest):
    side_in = rest[:N_SIDE]
    o_ref, ro_ref, sf_ref, sb_ref = rest[N_SIDE:N_SIDE + 4]
    side_out = rest[N_SIDE + 4:2 * N_SIDE + 4]
    xs0_ref, q_scr, k_scr, v_scr, dm_scr = rest[2 * N_SIDE + 4:]
    i = pl.program_id(0)
    rh = pl.program_id(1)
    build_mask, ret_context, ret_latent = _ret_parts(
        lg_ref[0, rh], lg_ref[1, rh], rq_ref, rk_ref, rv_ref, rg_ref, s0f_ref, s0b_ref, gn_ref,
        ro_ref, sf_ref, sb_ref, dm_scr.at[rh])

    @pl.when(i == 0)
    def _():
        build_mask()

    def side_jobs():
        for src, dst in zip(side_in, side_out):
            dst[...] = src[...].astype(BF16)
        xs0_ref[...] = jnp.zeros_like(xs0_ref)

    lam = sc_ref[0]
    out_scale = sc_ref[1]
    gain = g_ref[...] * out_scale
    lane = lax.broadcasted_iota(jnp.int32, (1, LANE), 1)
    first = lane < DIFF_HD
    nt = (((1,), (1,)), ((), ()))

    def halves(q):
        zero = jnp.zeros_like(q)
        return jnp.where(first, q, zero), jnp.where(first, zero, q)

    def weights(s):
        return jnp.exp2(s - jnp.max(s, axis=-1, keepdims=True)).astype(BF16)

    def finish(of0, of1):
        o = of0[:, :LANE] / of0[:, LANE:] - lam * (of1[:, :LANE] / of1[:, LANE:])
        return (_rms(o) * gain).astype(BF16)

    @pl.when(i < N_PBLK)
    def _():
        ret_context()
        n_seq = SEQ_BLK // T_P
        rows = [slice(s * T_P, (s + 1) * T_P) for s in range(n_seq)]
        ones = jnp.ones((T_P, LANE), BF16)
        for hh in range(ATT_HPS):
            cols = slice(hh * LANE, (hh + 1) * LANE)
            q0, q1 = halves((q_ref[hh].astype(F32) * ATT_QSCALE).astype(BF16))
            s0 = jnp.concatenate([lax.dot_general(q0[r], k_ref[hh, r, :], nt, preferred_element_type=F32)
                                  for r in rows], axis=0)
            s1 = jnp.concatenate([lax.dot_general(q1[r], k_ref[hh, r, :], nt, preferred_element_type=F32)
                                  for r in rows], axis=0)
            e0 = weights(s0)
            e1 = weights(s1)
            for r in rows:
                v1 = jnp.concatenate([v_ref[hh, r, :], ones], axis=1)
                o_ref[r, cols] = finish(jnp.dot(e0[r], v1, preferred_element_type=F32),
                                        jnp.dot(e1[r], v1, preferred_element_type=F32))

    @pl.when(i >= N_PBLK)
    def _():
        side_jobs()
        ret_latent()
        cos = cos_ref[...]
        sin = sin_ref[...]
        low = (lax.broadcasted_iota(jnp.int32, (T_S, LANE), 1) & 16) == 0

        def rope(x):
            xs = jnp.where(low, pltpu.roll(x, LANE - 16, 1), pltpu.roll(x, 16, 1))
            return x * cos + xs * sin

        for hh in range(ATT_HPS):
            cols = slice(hh * LANE, (hh + 1) * LANE)
            head = pl.program_id(1) * ATT_HPS + hh
            q_scr[hh] = (rope(q_ref[hh].astype(F32)) * ATT_QSCALE).astype(BF16)
            k_scr[hh] = rope(k_ref[hh].astype(F32)).astype(BF16)
            ckt = ckt_ref[cols, :].astype(BF16)
            v_scr[hh, 0:T_S, 0:LANE] = v_ref[hh]
            v_scr[hh, T_S:T_S + PAST, 0:LANE] = cv_ref[:, head, :].astype(BF16)
            v_scr[hh, :, LANE:2 * LANE] = jnp.ones((T_S + PAST, LANE), BF16)

            def scores(qh, hh=hh, ckt=ckt):
                return jnp.concatenate([lax.dot_general(qh, k_scr[hh], nt, preferred_element_type=F32),
                                        jnp.dot(qh, ckt, preferred_element_type=F32)], axis=1)

            for b in range(T_S // ATT_TQ):
                r = slice(b * ATT_TQ, (b + 1) * ATT_TQ)
                q0, q1 = halves(q_scr[hh, r, :])
                e0 = weights(scores(q0))
                e1 = weights(scores(q1))
                o_ref[r, cols] = finish(jnp.dot(e0, v_scr[hh], preferred_element_type=F32),
                                        jnp.dot(e1, v_scr[hh], preferred_element_type=F32))


def _mixers(scal, lg, proj, cache_k, cache_v, cos_t, sin_t, subln_g, s0f, s0b, gnorm, side_weights):
    nb = N_TOK // SEQ_BLK
    spb = SEQ_BLK // T_P
    smp = lambda i: jnp.maximum(i - N_PBLK, 0)
    pmt = lambda i: jnp.minimum(i, N_PBLK - 1)
    pmh = lambda i, h: jnp.where(i < N_PBLK, h, RET_H - 1)
    state_in = pl.BlockSpec((None, None, RET_KD, RET_VD), lambda i, h: (smp(i), h, 0, 0))
    state_out = pl.BlockSpec((spb, None, RET_KD, RET_VD), lambda i, h: (pmt(i), pmh(i, h), 0, 0))

    n_hp = DIFF_H // ATT_HPS
    assert n_hp == RET_H
    n_steps = (nb - N_PBLK) * n_hp
    step = lambda i, h: jnp.maximum((i - N_PBLK) * n_hp + h, 0)
    exp_per_step = N_EXP // n_steps
    exp_slice = lambda i, h: (step(i, h), 0, 0)
    row_slice = lambda i, h: (step(i, h), 0)
    up_spec = pl.BlockSpec((exp_per_step, D, FF), exp_slice)
    down_spec = pl.BlockSpec((exp_per_step, FF, D), exp_slice)
    side_specs = [up_spec, up_spec, down_spec]
    side_shapes = [jax.ShapeDtypeStruct(a.shape, BF16) for a in side_weights]
    zrows = R_MAX // n_steps

    return pl.pallas_call(
        _mixer_kernel,
        out_shape=(jax.ShapeDtypeStruct((N_TOK, DIFF_H * 2 * DIFF_HD), BF16),
                   jax.ShapeDtypeStruct((N_TOK, RET_H * RET_VD), BF16),
                   jax.ShapeDtypeStruct((16, RET_H, RET_KD, RET_VD), F32),
                   jax.ShapeDtypeStruct((16, RET_H, RET_KD, RET_VD), F32),
                   *side_shapes,
                   jax.ShapeDtypeStruct((R_MAX, XW), BF16)),
        grid=(nb, n_hp),
        in_specs=[pl.BlockSpec(memory_space=pltpu.SMEM),
                  pl.BlockSpec(memory_space=pltpu.SMEM),
                  pl.BlockSpec((ATT_HPS, SEQ_BLK, LANE), lambda i, h: (C_DQ // ATT_HPS + h, i, 0)),
                  pl.BlockSpec((ATT_HPS, SEQ_BLK, LANE), lambda i, h: (C_DK // ATT_HPS + h, i, 0)),
                  pl.BlockSpec((ATT_HPS, SEQ_BLK, LANE), lambda i, h: (C_DV // ATT_HPS + h, i, 0)),
                  pl.BlockSpec((None, ATT_HPS * LANE, PAST), lambda i, h: (smp(i), h, 0)),
                  pl.BlockSpec((None, None, PAST, DIFF_H, LANE), lambda i, h: (smp(i), 0, 0, 0, 0)),
                  pl.BlockSpec((T_S, LANE), lambda i, h: (0, 0)),
                  pl.BlockSpec((T_S, LANE), lambda i, h: (0, 0)),
                  pl.BlockSpec((1, LANE), lambda i, h: (0, 0)),
                  pl.BlockSpec((None, SEQ_BLK, LANE), lambda i, h: (C_RQ + h, i, 0)),
                  pl.BlockSpec((None, SEQ_BLK, LANE), lambda i, h: (C_RK + h, i, 0)),
                  pl.BlockSpec((2, SEQ_BLK, LANE), lambda i, h: (C_RV // 2 + h, i, 0)),
                  pl.BlockSpec((2, SEQ_BLK, LANE), lambda i, h: (C_RG // 2 + h, i, 0)),
                  state_in, state_in,
                  pl.BlockSpec((1, RET_VD), lambda i, h: (0, h)),
                  *side_specs],
        out_specs=(pl.BlockSpec((SEQ_BLK, ATT_HPS * LANE), lambda i, h: (i, h)),
                   pl.BlockSpec((SEQ_BLK, RET_VD), lambda i, h: (i, h)),
                   state_out, state_out,
                   *side_specs,
                   pl.BlockSpec((zrows, XW), row_slice)),
        scratch_shapes=[pltpu.VMEM((ATT_HPS, T_S, LANE), BF16),
                        pltpu.VMEM((ATT_HPS, T_S, LANE), BF16),
                        pltpu.VMEM((ATT_HPS, T_S + PAST, 2 * LANE), BF16),
                        pltpu.VMEM((RET_H, T_P, T_P), F32)],
        compiler_params=_cparams(("arbitrary", "arbitrary")),
        name="mixers",
    )(scal, lg, proj, proj, proj, cache_k, cache_v, cos_t, sin_t, subln_g,
      proj, proj, proj, proj, s0f, s0b, gnorm, *side_weights)


def _rope_tables():
    n_rows = T_S // GRID_W
    row = np.repeat(np.arange(n_rows), GRID_W).astype(np.float64)
    col = np.tile(np.arange(GRID_W), n_rows).astype(np.float64)
    n_freq = DIFF_HD // 4
    inv = ROPE_BASE ** (-np.arange(n_freq, dtype=np.float64) / n_freq)

    def axis_tables(pos):
        ang = pos[:, None] * inv[None, :]
        c = np.cos(ang)
        s = np.sin(ang)
        return np.concatenate([c, c], axis=-1), np.concatenate([-s, s], axis=-1)

    cr, sr = axis_tables(row)
    cc, sc = axis_tables(col)
    cos_h = np.concatenate([cr, cc], axis=-1)
    sin_h = np.concatenate([sr, sc], axis=-1)
    return (jnp.asarray(np.concatenate([cos_h, cos_h], axis=-1), F32),
            jnp.asarray(np.concatenate([sin_h, sin_h], axis=-1), F32))


OP_TM = 512
OP_NPT = N_PROMPT // OP_TM


def _outproj_kernel(ra_ref, da_ref, gr_ref, gd_ref, xp_ref, xs_ref, mod_ref, n2_ref,
                    wro32_ref, wdo32_ref, wo32_ref, rgw_ref, rew_ref, brt_ref,
                    x1_ref, h2_ref, cw8_ref, info_ref, infot_ref, off_ref, pctab_ref, steps_ref,
                    m_scr, wro_ref, wdo_ref, wo_ref, wrt_ref, pc_scr):
    i = pl.program_id(0)

    @pl.when(i == 0)
    def _():
        wro_ref[...] = wro32_ref[...].astype(BF16)
        wdo_ref[...] = wdo32_ref[...].astype(BF16)
        wo_ref[...] = wo32_ref[...].astype(BF16)
        gap = jnp.zeros((D, ROUTE_E0 - N_GROUPS), F32)
        pad = jnp.zeros((D, ROUTE_W - ROUTE_E0 - N_EXP), F32)
        wr = jnp.concatenate([rgw_ref[...], gap, rew_ref[...], pad], axis=1)
        wrt_ref[...] = wr.T.astype(BF16)

    ret_out = jnp.dot(ra_ref[...], wro_ref[...], preferred_element_type=F32)
    diff_out = jnp.dot(da_ref[...], wdo_ref[...], preferred_element_type=F32)
    for c in range(D // LANE):
        sl = slice(c * LANE, (c + 1) * LANE)
        m = (jax.nn.sigmoid(gr_ref[c].astype(F32)) * ret_out[:, sl]
             + jax.nn.sigmoid(gd_ref[c].astype(F32)) * diff_out[:, sl])
        m_scr[:, sl] = m.astype(BF16)
    mix = jnp.dot(m_scr[...], wo_ref[...], preferred_element_type=F32)
    mod = mod_ref[...]

    x1 = jnp.where(i < OP_NPT, xp_ref[...], xs_ref[...]) + mod[:, 2 * D:3 * D] * mix
    x1_ref[...] = x1
    h2 = (_rms(x1) * n2_ref[...] * (1.0 + mod[:, 4 * D:5 * D]) + mod[:, 3 * D:4 * D]).astype(BF16)
    h2_ref[...] = h2
    logits_t = lax.dot_general(wrt_ref[...], h2, (((1,), (1,)), ((), ())),
                               preferred_element_type=F32) + brt_ref[...]
    cw8_ref[...], info_ref[...], pc, infot_ref[...] = _route_cols(logits_t)
    pc_scr[pl.ds(i, 1), :] = pc

    @pl.when(i == RT_NT - 1)
    def _():
        off_ref[...], pctab_ref[...], steps_ref[...] = _dispatch_plan(pc_scr[...])


def _outproj(ret_act, diff_act, proj, xp, xs, mod3, n2, wro, wdo, wo, rgw, rew, brt):
    assert OP_TM == RT_TM
    npt = OP_NPT
    full = lambda i: (0, 0)
    once = pl.Buffered(1)
    return pl.pallas_call(
        _outproj_kernel,
        out_shape=(jax.ShapeDtypeStruct((N_TOK, D), F32),
                   jax.ShapeDtypeStruct((N_TOK, D), BF16),
                   jax.ShapeDtypeStruct((N_TOK, ROUTE_W), F32),
                   jax.ShapeDtypeStruct((N_TOK, ROUTE_W), F32),
                   jax.ShapeDtypeStruct((RT_NT, EPG, RT_TM), F32),
                   jax.ShapeDtypeStruct((RT_NT, ROUTE_W), jnp.int32),
                   jax.ShapeDtypeStruct((RT_NT, ROUTE_W), jnp.int32),
                   jax.ShapeDtypeStruct((PLAN_ROWS, ROUTE_W), jnp.int32)),
        grid=(N_TOK // OP_TM,),
        in_specs=[pl.BlockSpec((OP_TM, D), lambda i: (i, 0)),
                  pl.BlockSpec((OP_TM, D), lambda i: (i, 0)),
                  pl.BlockSpec((8, OP_TM, LANE), lambda i: (C_GR // 8, i, 0)),
                  pl.BlockSpec((8, OP_TM, LANE), lambda i: (C_GD // 8, i, 0)),
                  pl.BlockSpec((OP_TM, D), lambda i: (jnp.minimum(i, npt - 1), 0)),
                  pl.BlockSpec((OP_TM, D), lambda i: (jnp.maximum(i - npt, 0), 0)),
                  pl.BlockSpec((None, 1, 6 * D), lambda i: (_mod_row(OP_TM)(i), 0, 0)),
                  pl.BlockSpec((1, D), full),
                  pl.BlockSpec((D, D), full, pipeline_mode=once),
                  pl.BlockSpec((D, D), full, pipeline_mode=once),
                  pl.BlockSpec((D, D), full, pipeline_mode=once),
                  pl.BlockSpec((D, N_GROUPS), full, pipeline_mode=once),
                  pl.BlockSpec((D, N_EXP), full, pipeline_mode=once),
                  pl.BlockSpec((ROUTE_W, 1), full)],
        out_specs=(pl.BlockSpec((OP_TM, D), lambda i: (i, 0)),
                   pl.BlockSpec((OP_TM, D), lambda i: (i, 0)),
                   pl.BlockSpec((OP_TM, ROUTE_W), lambda i: (i, 0)),
                   pl.BlockSpec((OP_TM, ROUTE_W), lambda i: (i, 0)),
                   pl.BlockSpec((None, EPG, RT_TM), lambda i: (i, 0, 0)),
                   pl.BlockSpec((RT_NT, ROUTE_W), full),
                   pl.BlockSpec((RT_NT, ROUTE_W), full),
                   pl.BlockSpec((PLAN_ROWS, ROUTE_W), full)),
        scratch_shapes=[pltpu.VMEM((OP_TM, D), BF16),
                        pltpu.VMEM((D, D), BF16), pltpu.VMEM((D, D), BF16), pltpu.VMEM((D, D), BF16),
                        pltpu.VMEM((ROUTE_W, D), BF16),
                        pltpu.VMEM((RT_NT, ROUTE_W), F32)],
        compiler_params=_cparams(("arbitrary",)),
        name="outproj",
    )(ret_act, diff_act, proj, proj, xp, xs, mod3, n2, wro, wdo, wo, rgw, rew, brt)


RT_TM = 512
RT_NT = N_TOK // RT_TM
PIECE = 16
R_LOC = RT_TM + N_GROUPS * PIECE
R_STAGE = 640
EX_TM = 512
R_MAX = 11264
EX_NT = R_MAX // EX_TM


ROUTE_E0 = EPG


def _route_cols(lgt):
    row = lax.broadcasted_iota(jnp.int32, (EPG, RT_TM), 0)
    neg = jnp.float32(-jnp.inf)

    def first_row(cond):
        return jnp.min(jnp.where(cond, row, EPG), axis=0, keepdims=True)

    head = lgt[0:EPG]
    is_g = row < N_GROUPS
    gl = jnp.where(is_g, head, neg)
    gmax = jnp.max(gl, axis=0, keepdims=True)
    gsum = jnp.sum(jnp.where(is_g, jnp.exp(head - gmax), 0.0), axis=0, keepdims=True)
    p_top = 1.0 / gsum
    g_idx = first_row(gl == gmax)
    el = lgt[ROUTE_E0 + (N_GROUPS - 1) * EPG:ROUTE_E0 + N_GROUPS * EPG]
    for g in reversed(range(N_GROUPS - 1)):
        el = jnp.where(g_idx == g, lgt[ROUTE_E0 + g * EPG:ROUTE_E0 + (g + 1) * EPG], el)
    ee = jnp.exp(el - jnp.max(el, axis=0, keepdims=True))
    ep = ee / jnp.sum(ee, axis=0, keepdims=True)
    e1 = jnp.max(ep, axis=0, keepdims=True)
    i1 = first_row(ep == e1)
    ep2 = jnp.where(row == i1, -1.0, ep)
    e2 = jnp.max(ep2, axis=0, keepdims=True)
    i2 = first_row(ep2 == e2)
    den = e1 + e2
    cw8_t = (jnp.where(row == i1, p_top * e1 / den, 0.0)
             + jnp.where(row == i2, p_top * e2 / den, 0.0))

    onehot = (row == g_idx).astype(F32)
    ii = lax.broadcasted_iota(jnp.int32, (RT_TM, RT_TM), 0)
    jj = lax.broadcasted_iota(jnp.int32, (RT_TM, RT_TM), 1)
    earlier = (ii < jj).astype(BF16)
    prefix = jnp.dot(onehot.astype(BF16), earlier, preferred_element_type=F32)
    cnt = jnp.sum(onehot, axis=1, keepdims=True)
    pc_col = jnp.floor((cnt + (PIECE - 1.0)) * (1.0 / PIECE)) * PIECE
    row1 = lax.broadcasted_iota(jnp.int32, (EPG, 1), 0)
    lane1 = lax.broadcasted_iota(jnp.int32, (1, ROUTE_W), 1)
    lo = jnp.zeros((EPG, 1), F32)
    run = jnp.zeros((1, 1), F32)
    pc = jnp.zeros((1, ROUTE_W), F32)
    for g in range(N_GROUPS):
        pc_g = jnp.sum(jnp.where(row1 == g, pc_col, 0.0), axis=0, keepdims=True)
        lo = jnp.where(row1 == g, run, lo)
        pc = jnp.where(lane1 == g, pc_g, pc)
        run = run + pc_g
    dest = jnp.sum(onehot * (prefix + lo), axis=0, keepdims=True)
    info_t = jnp.where(row == 0, g_idx.astype(F32), jnp.where(row == 1, dest, 0.0))

    slab = jnp.concatenate([cw8_t, info_t, jnp.zeros((LANE - 2 * EPG, RT_TM), F32)], axis=0)
    cols = slab.T
    lane = lax.broadcasted_iota(jnp.int32, cols.shape, 1)
    cw8 = jnp.where(lane < EPG, cols, 0.0)
    info = jnp.where(lane < 2, pltpu.roll(cols, LANE - EPG, 1), 0.0)
    return cw8, info, pc, info_t


PLAN_ROWS = 32


def _dispatch_plan(pc_all):
    lane = lax.broadcasted_iota(jnp.int32, (1, ROUTE_W), 1)
    is_g = lane < N_GROUPS
    seg_len = jnp.sum(pc_all, axis=0, keepdims=True)
    seg_pad = jnp.floor((seg_len + (EX_TM - 1.0)) * (1.0 / EX_TM)) * EX_TM
    run = seg_pad + pltpu.roll(seg_pad, 1, 1)
    seg_end_pad = jnp.where(is_g, run + pltpu.roll(run, 2, 1), 0.0)
    seg_start = seg_end_pad - seg_pad
    ti = lax.broadcasted_iota(jnp.int32, (RT_NT, RT_NT), 0)
    tj = lax.broadcasted_iota(jnp.int32, (RT_NT, RT_NT), 1)
    earlier = (tj < ti).astype(BF16)
    within = jnp.dot(earlier, pc_all.astype(BF16), preferred_element_type=F32)
    chunk_off = jnp.where(is_g, seg_start + within, 0.0)

    start = (lax.broadcasted_iota(jnp.int32, (PLAN_ROWS, 1), 0) * EX_TM).astype(F32)
    passed = jnp.where(jnp.logical_and(is_g, start >= seg_end_pad), 1.0, 0.0)
    group = jnp.minimum(jnp.sum(passed, axis=1, keepdims=True), N_GROUPS - 1.0)
    used_end = jnp.sum(jnp.where(lane == group.astype(jnp.int32), seg_start + seg_len, 0.0),
                       axis=1, keepdims=True)
    used = jnp.clip(used_end - start, 0.0, float(EX_TM))
    steps = jnp.where(lane == 0, group, jnp.where(lane == 1, used, 0.0))
    return chunk_off.astype(jnp.int32), pc_all.astype(jnp.int32), steps.astype(jnp.int32)


def _piece_copies(off_ref, pc_ref, tile, make):
    lo = 0
    for g in range(N_GROUPS):
        n = pc_ref[tile, g] // PIECE
        base = off_ref[tile, g]

        def body(j, carry, lo=lo, base=base):
            make(pl.multiple_of(lo + j * PIECE, PIECE), pl.multiple_of(base + j * PIECE, PIECE))
            return carry

        lax.fori_loop(0, n, body, 0)
        lo = lo + pc_ref[tile, g]


def _piece_count(pc_ref, tile):
    n = 0
    for g in range(N_GROUPS):
        n = n + pc_ref[tile, g] // PIECE
    return n


XW = D + ROUTE_W


def _dispatch_kernel(off_ref, pc_ref, h_ref, infot_ref, cw8_ref, xs_in, xs_out, x_scr, sem):
    del xs_in
    i = pl.program_id(0)
    slot = i % 2
    dest = infot_ref[1:2, :]
    row = lax.broadcasted_iota(jnp.int32, (R_LOC, RT_TM), 0).astype(F32)
    sel = (row == dest).astype(BF16)
    cw = cw8_ref[...]
    hi = cw.astype(BF16).astype(F32)
    mid = (cw - hi).astype(BF16).astype(F32)
    low = (cw - hi - mid).astype(BF16).astype(F32)
    pieces = (hi + pltpu.roll(mid, EPG, 1) + pltpu.roll(low, 2 * EPG, 1)).astype(BF16)
    rows = jnp.concatenate([h_ref[...], pieces], axis=1)
    x_scr[slot] = jnp.dot(sel, rows, preferred_element_type=F32).astype(BF16)

    def x_copy(s, src, dst):
        return pltpu.make_async_copy(x_scr.at[s, pl.ds(src, PIECE)], xs_out.at[pl.ds(dst, PIECE)],
                                     sem.at[s])

    _piece_copies(off_ref, pc_ref, i, lambda src, dst: x_copy(slot, src, dst).start())

    def wait_tile(tile, s):
        def wait(j, carry):
            x_copy(s, 0, 0).wait()
            return carry

        lax.fori_loop(0, _piece_count(pc_ref, tile), wait, 0)

    @pl.when(i > 0)
    def _():
        wait_tile(i - 1, 1 - slot)

    @pl.when(i == RT_NT - 1)
    def _():
        wait_tile(i, slot)


def _dispatch(chunk_off, pc, h2, info_t, cw8, xs0):
    grid_spec = pltpu.PrefetchScalarGridSpec(
        num_scalar_prefetch=2,
        grid=(RT_NT,),
        in_specs=[pl.BlockSpec((RT_TM, D), lambda i, o, p: (i, 0)),
                  pl.BlockSpec((None, EPG, RT_TM), lambda i, o, p: (i, 0, 0)),
                  pl.BlockSpec((RT_TM, ROUTE_W), lambda i, o, p: (i, 0)),
                  pl.BlockSpec(memory_space=pl.ANY)],
        out_specs=pl.BlockSpec(memory_space=pl.ANY),
        scratch_shapes=[pltpu.VMEM((2, R_LOC, XW), BF16),
                        pltpu.SemaphoreType.DMA((2,))])
    return pl.pallas_call(
        _dispatch_kernel,
        out_shape=jax.ShapeDtypeStruct((R_MAX, XW), BF16),
        grid_spec=grid_spec,
        input_output_aliases={5: 0},
        compiler_params=_cparams(("arbitrary",)),
        name="dispatch",
    )(chunk_off, pc, h2, info_t, cw8, xs0)


def _expert_kernel(steps_ref, x_ref, wg_ref, wu_ref, wd_ref, y_ref):
    k = pl.program_id(0)
    used = steps_ref[k, 1]
    half = EX_TM // 2

    def run(rows):
        x = x_ref[rows, 0:D]
        cw = x_ref[rows, D:XW].astype(F32)
        lane = lax.broadcasted_iota(jnp.int32, cw.shape, 1)
        acc = jnp.zeros(x.shape, F32)
        for j in range(EPG):
            mine = jnp.logical_and((lane & (EPG - 1)) == j, lane < 3 * EPG)
            w = jnp.sum(jnp.where(mine, cw, 0.0), axis=-1, keepdims=True)
            a = (jax.nn.silu(jnp.dot(x, wg_ref[j], preferred_element_type=F32))
                 * jnp.dot(x, wu_ref[j], preferred_element_type=F32))
            acc = acc + jnp.dot((a * w).astype(BF16), wd_ref[j], preferred_element_type=F32)
        y_ref[rows, :] = acc.astype(BF16)

    @pl.when(used > half)
    def _():
        run(slice(0, EX_TM))

    @pl.when(jnp.logical_and(used > 0, used <= half))
    def _():
        run(slice(0, half))
        y_ref[half:EX_TM, :] = jnp.zeros((EX_TM - half, D), BF16)

    @pl.when(used == 0)
    def _():
        y_ref[...] = jnp.zeros_like(y_ref)


def _experts(steps, xs, wg, wu, wd):
    grid_spec = pltpu.PrefetchScalarGridSpec(
        num_scalar_prefetch=1,
        grid=(EX_NT,),
        in_specs=[pl.BlockSpec((EX_TM, XW), lambda k, st: (k, 0)),
                  pl.BlockSpec((EPG, D, FF), lambda k, st: (st[k, 0], 0, 0)),
                  pl.BlockSpec((EPG, D, FF), lambda k, st: (st[k, 0], 0, 0)),
                  pl.BlockSpec((EPG, FF, D), lambda k, st: (st[k, 0], 0, 0))],
        out_specs=pl.BlockSpec((EX_TM, D), lambda k, st: (k, 0)))
    return pl.pallas_call(
        _expert_kernel,
        out_shape=jax.ShapeDtypeStruct((R_MAX, D), BF16),
        grid_spec=grid_spec,
        compiler_params=_cparams(("arbitrary",)),
        name="experts",
    )(steps, xs, wg, wu, wd)


CB_NPT = N_PROMPT // RT_TM


def _combine_kernel(off_ref, pc_ref, info_ref, x1_ref, mod_ref, g_ref, ys_hbm,
                    yp_ref, ysm_ref, stage, sem):
    i = pl.program_id(0)
    slot = i % 2

    def copy(s, dst, src):
        return pltpu.make_async_copy(ys_hbm.at[pl.ds(src, PIECE)], stage.at[s, pl.ds(dst, PIECE)],
                                     sem.at[s])

    def fetch(tile, s):
        _piece_copies(off_ref, pc_ref, tile, lambda loc, glob: copy(s, loc, glob).start())

    @pl.when(i == 0)
    def _():
        stage[...] = jnp.zeros_like(stage)
        fetch(0, 0)

    @pl.when(i + 1 < RT_NT)
    def _():
        fetch(i + 1, 1 - slot)

    def wait(j, carry):
        copy(slot, 0, 0).wait()
        return carry

    lax.fori_loop(0, _piece_count(pc_ref, i), wait, 0)

    dest = info_ref[...][:, 1:2]
    col = lax.broadcasted_iota(jnp.int32, (RT_TM, R_STAGE), 1).astype(F32)
    moe = jnp.dot((col == dest).astype(BF16), stage[slot], preferred_element_type=F32)
    mod = mod_ref[...]
    out = _rms(x1_ref[...] + mod[:, 5 * D:6 * D] * moe) * g_ref[...]

    @pl.when(i < CB_NPT)
    def _():
        yp_ref[...] = out

    @pl.when(i >= CB_NPT)
    def _():
        ysm_ref[...] = out


def _combine(chunk_off, pc, info, x1, mod3, fg, ys):
    npt = CB_NPT
    grid_spec = pltpu.PrefetchScalarGridSpec(
        num_scalar_prefetch=2,
        grid=(RT_NT,),
        in_specs=[pl.BlockSpec((RT_TM, ROUTE_W), lambda i, o, p: (i, 0)),
                  pl.BlockSpec((RT_TM, D), lambda i, o, p: (i, 0)),
                  pl.BlockSpec((None, 1, 6 * D), lambda i, o, p: (_mod_row(RT_TM)(i), 0, 0)),
                  pl.BlockSpec((1, D), lambda i, o, p: (0, 0)),
                  pl.BlockSpec(memory_space=pl.ANY)],
        out_specs=(pl.BlockSpec((RT_TM, D), lambda i, o, p: (jnp.minimum(i, npt - 1), 0)),
                   pl.BlockSpec((RT_TM, D), lambda i, o, p: (jnp.maximum(i - npt, 0), 0))),
        scratch_shapes=[pltpu.VMEM((2, R_STAGE, D), BF16),
                        pltpu.SemaphoreType.DMA((2,))])
    return pl.pallas_call(
        _combine_kernel,
        out_shape=(jax.ShapeDtypeStruct((N_PROMPT, D), F32),
                   jax.ShapeDtypeStruct((N_SAMPLE, D), F32)),
        grid_spec=grid_spec,
        compiler_params=_cparams(("arbitrary",)),
        name="combine",
    )(chunk_off, pc, info, x1, mod3, fg, ys)


def kernel(x_prompt, x_sample, cache_diff_k, cache_diff_v, state_ret_fwd, state_ret_bwd, c, c_ctx,
           w_ada, b_ada, norm1_g, norm2_g, w_in, ret_decay_fwd, ret_decay_bwd, ret_norm_g,
           diff_lambda_q1, diff_lambda_k1, diff_lambda_q2, diff_lambda_k2, diff_subln_g,
           w_ret_o, w_diff_o, w_o, router_group_w, router_group_b, router_expert_w, router_expert_b,
           moe_w_gate, moe_w_up, moe_w_down, final_norm_g):
    l = 0
    lam_init = 0.8 - 0.6 * math.exp(-0.3 * l)
    lam = (jnp.exp(jnp.sum(diff_lambda_q1[l].astype(F32) * diff_lambda_k1[l].astype(F32)))
           - jnp.exp(jnp.sum(diff_lambda_q2[l].astype(F32) * diff_lambda_k2[l].astype(F32))) + lam_init)
    attn_scal = jnp.stack([lam, jnp.float32(1.0 - lam_init)]).astype(F32)
    lg = jnp.stack([jax.nn.log_sigmoid(ret_decay_fwd[l].astype(F32)),
                    jax.nn.log_sigmoid(ret_decay_bwd[l].astype(F32))])

    xp = x_prompt.reshape(N_PROMPT, D)
    xs = x_sample.reshape(N_SAMPLE, D)
    mod3 = _modulation(c_ctx[None, :], c, w_ada[l], b_ada[l][None, :])

    proj, kt32, v32 = _inproj(xp, xs, mod3, norm1_g[l][None, :], w_in[l])

    cos_t, sin_t = _rope_tables()
    cache_kt = jnp.transpose(cache_diff_k[:, l], (0, 2, 3, 4, 1)).reshape(4, D, PAST)
    (diff_act, ret_act, s_f, s_b, wg_bf, wu_bf, wd_bf, xs0) = _mixers(
        attn_scal, lg, proj, cache_kt, cache_diff_v, cos_t, sin_t, diff_subln_g[l][None, :],
        state_ret_fwd[:, l], state_ret_bwd[:, l], ret_norm_g[l][None, :],
        (moe_w_gate[l], moe_w_up[l], moe_w_down[l]))

    brt = jnp.concatenate([router_group_b[l], jnp.zeros((ROUTE_E0 - N_GROUPS,), F32), router_expert_b[l],
                           jnp.zeros((ROUTE_W - ROUTE_E0 - N_EXP,), F32)])[:, None]
    x1, h2, cw8, info, info_t, chunk_off, pc, steps = _outproj(
        ret_act, diff_act, proj, xp, xs, mod3, norm2_g[l][None, :],
        w_ret_o[l], w_diff_o[l], w_o[l], router_group_w[l], router_expert_w[l], brt)
    xs_sorted = _dispatch(chunk_off, pc, h2, info_t, cw8, xs0)
    y_sorted = _experts(steps, xs_sorted, wg_bf, wu_bf, wd_bf)
    yp, ys = _combine(chunk_off, pc, info, x1, mod3, final_norm_g[None, :], y_sorted)

    return (yp.reshape(16, T_P, D), ys.reshape(4, T_S, D),
            jnp.transpose(kt32.reshape(16, DIFF_H, 2, DIFF_HD, T_P), (0, 4, 1, 2, 3))[:, None],
            v32.reshape(16, 1, T_P, DIFF_H, 2 * DIFF_HD),
            s_f.reshape(16, 1, RET_H, RET_KD, RET_VD), s_b.reshape(16, 1, RET_H, RET_KD, RET_VD))
```

```python
import math

import jax
import jax.numpy as jnp
import numpy as np
from jax import lax
from jax.experimental import pallas as pl
from jax.experimental.pallas import tpu as pltpu

F32 = jnp.float32
BF16 = jnp.bfloat16

D = 1024
N_PROMPT = 16 * 256
N_SAMPLE = 4 * 1024
N_TOK = N_PROMPT + N_SAMPLE
T_P = 256
T_S = 1024
PAST = 512
GRID_W = 64
RET_H = 4
RET_KD = 128
RET_VD = 256
DIFF_H = 8
DIFF_HD = 64
ROPE_BASE = 10000.0
N_GROUPS = 4
EPG = 8
N_EXP = 32
FF = 256
EPS = 1e-6
IN_W = 8192
LANE = 128
N_CHUNK = IN_W // LANE
C_RQ, C_RK, C_RV, C_RG, C_DQ, C_DK, C_DV, C_GR, C_GD = 0, 4, 8, 16, 24, 32, 40, 48, 56
ROUTE_W = 128
SEQ_BLK = 1024
N_PBLK = N_PROMPT // SEQ_BLK
V7X_VMEM_BYTES = 64 * 1024 * 1024
VMEM_LIMIT = V7X_VMEM_BYTES - 8 * 1024 * 1024


def _cparams(sem):
    return pltpu.CompilerParams(dimension_semantics=sem, vmem_limit_bytes=VMEM_LIMIT)


def _mod_row(tile_rows):
    def f(i):
        start = i * tile_rows
        return jnp.where(start < N_PROMPT, 0, 1 + (start - N_PROMPT) // T_S)
    return f


def _rms(x):
    return x * lax.rsqrt(jnp.mean(x * x, axis=-1, keepdims=True) + EPS)


MOD_ROWS = 8


def _mod_kernel(ctx_ref, c_ref, w_ref, b_ref, o_ref):
    cond = jnp.concatenate([ctx_ref[...], c_ref[...],
                            jnp.zeros((MOD_ROWS - 1 - c_ref.shape[0], D), F32)], axis=0)
    s = jax.nn.silu(cond)
    out = jnp.dot(s.astype(BF16), w_ref[...].astype(BF16), preferred_element_type=F32) + b_ref[...]
    for r in range(MOD_ROWS):
        o_ref[r] = out[r:r + 1, :]


def _modulation(c_ctx, c, w_ada, b_ada):
    tn = 1536
    return pl.pallas_call(
        _mod_kernel,
        out_shape=jax.ShapeDtypeStruct((MOD_ROWS, 1, 6 * D), F32),
        grid=(6 * D // tn,),
        in_specs=[pl.BlockSpec((1, D), lambda j: (0, 0)),
                  pl.BlockSpec(c.shape, lambda j: (0, 0)),
                  pl.BlockSpec((D, tn), lambda j: (0, j)),
                  pl.BlockSpec((1, tn), lambda j: (0, j))],
        out_specs=pl.BlockSpec((MOD_ROWS, 1, tn), lambda j: (0, 0, j)),
        compiler_params=_cparams(("arbitrary",)),
        name="mod",
    )(c_ctx, c, w_ada, b_ada)


IP_TM = 512
IP_TN = 2048
IP_NPT = N_PROMPT // IP_TM


IP_KV_TILE = C_DK * LANE // IP_TN
IP_SPT = IP_TM // T_P


IP_NT = N_TOK // IP_TM
IP_EDGES = 16


def _ordered_after(dst, src):
    sub = 16
    m = None
    for a in range(src.shape[0] // sub):
        for b in range(src.shape[1] // LANE):
            part = src[a * sub:(a + 1) * sub, b * LANE:(b + 1) * LANE]
            m = part if m is None else jnp.maximum(m, part)
    never = jnp.logical_and(m != m, m == m)
    return jnp.where(jnp.tile(never, (dst.shape[0] // sub, 1)), jnp.zeros_like(dst), dst)


def _inproj_kernel(xp_hbm, mod_hbm, xp_ref, xs_ref, mod_ref, n1_ref, w_ref, proj_ref, kt_ref, v32_ref,
                   w_scr, h_even, h_odd, x0_scr, mod0_scr, sem):
    j = pl.program_id(0)
    i = pl.program_id(1)

    def normed(x, mod):
        return _rms(x) * n1_ref[...] * (1.0 + mod[:, D:2 * D]) + mod[:, 0:D]

    @pl.when(i == 0)
    def _():
        w_scr[...] = w_ref[...].astype(BF16)

    @pl.when(jnp.logical_and(j == 0, i == 0))
    def _():
        cx = pltpu.make_async_copy(xp_hbm.at[pl.ds(0, IP_TM)], x0_scr, sem.at[0])
        cm = pltpu.make_async_copy(mod_hbm.at[0], mod0_scr, sem.at[1])
        cx.start()
        cm.start()
        cx.wait()
        cm.wait()
        h_even[...] = normed(x0_scr[...], mod0_scr[...]).astype(BF16)

    def step(h_cur, h_nxt):
        nxt = (i + 1) % IP_NT
        hn = normed(jnp.where(nxt < IP_NPT, xp_ref[...], xs_ref[...]), mod_ref[...]).astype(BF16)
        h_nxt[...] = hn
        acc = jnp.dot(h_cur[...], w_scr[...], preferred_element_type=F32)
        n_c = IP_TN // LANE
        blk = IP_TM // IP_EDGES
        for c in range(n_c):
            chunk = acc[:, c * LANE:(c + 1) * LANE].astype(BF16)
            r = c - (n_c - IP_EDGES)
            if r >= 0:
                chunk = _ordered_after(chunk, hn[r * blk:(r + 1) * blk, :])
            proj_ref[c] = chunk

        @pl.when(jnp.logical_and(j == IP_KV_TILE, i < IP_NPT))
        def _():
            for s in range(IP_SPT):
                kt_ref[s] = acc[s * T_P:(s + 1) * T_P, :D].T
            v32_ref[...] = acc[:, D:]

    @pl.when(i % 2 == 0)
    def _():
        step(h_even, h_odd)

    @pl.when(i % 2 == 1)
    def _():
        step(h_odd, h_even)


def _inproj(xp, xs, mod3, n1, w_in):
    npt = IP_NPT
    cpt = IP_TN // LANE
    assert IP_NT % 2 == 0

    def kv_tile(j, i):
        return jnp.where(j < IP_KV_TILE, 0, jnp.where(j == IP_KV_TILE, jnp.minimum(i, npt - 1), npt - 1))

    nxt = lambda i: (i + 1) % IP_NT
    return pl.pallas_call(
        _inproj_kernel,
        out_shape=(jax.ShapeDtypeStruct((N_CHUNK, N_TOK, LANE), BF16),
                   jax.ShapeDtypeStruct((N_PROMPT // T_P, D, T_P), F32),
                   jax.ShapeDtypeStruct((N_PROMPT, D), F32)),
        grid=(IN_W // IP_TN, IP_NT),
        in_specs=[pl.BlockSpec(memory_space=pl.ANY),
                  pl.BlockSpec(memory_space=pl.ANY),
                  pl.BlockSpec((IP_TM, D), lambda j, i: (jnp.minimum(nxt(i), npt - 1), 0)),
                  pl.BlockSpec((IP_TM, D), lambda j, i: (jnp.maximum(nxt(i) - npt, 0), 0)),
                  pl.BlockSpec((None, 1, 6 * D), lambda j, i: (_mod_row(IP_TM)(nxt(i)), 0, 0)),
                  pl.BlockSpec((1, D), lambda j, i: (0, 0)),
                  pl.BlockSpec((D, IP_TN), lambda j, i: (0, j))],
        out_specs=(pl.BlockSpec((cpt, IP_TM, LANE), lambda j, i: (j, i, 0)),
                   pl.BlockSpec((IP_SPT, D, T_P), lambda j, i: (kv_tile(j, i), 0, 0)),
                   pl.BlockSpec((IP_TM, D), lambda j, i: (kv_tile(j, i), 0))),
        scratch_shapes=[pltpu.VMEM((D, IP_TN), BF16),
                        pltpu.VMEM((IP_TM, D), BF16), pltpu.VMEM((IP_TM, D), BF16),
                        pltpu.VMEM((IP_TM, D), F32), pltpu.VMEM((1, 6 * D), F32),
                        pltpu.SemaphoreType.DMA((2,))],
        compiler_params=_cparams(("arbitrary", "arbitrary")),
        name="inproj",
    )(xp, mod3, xp, xs, mod3, n1, w_in)


def _decay_mask(t, lgf, lgb):
    ii = lax.broadcasted_iota(jnp.int32, (t, t), 0)
    jj = lax.broadcasted_iota(jnp.int32, (t, t), 1)
    rel = (ii - jj).astype(F32)
    e = jnp.exp(jnp.where(rel >= 0.0, lgf, -lgb) * rel)
    return jnp.where(rel == 0.0, 2.0, e) * (RET_KD ** -0.5)


def _ret_parts(lgf, lgb, q_ref, k_ref, v_ref, rg_ref, s0f_ref, s0b_ref, g_ref,
               o_ref, sf_ref, sb_ref, dm_ref):
    gain = g_ref[...]
    nt = (((1,), (1,)), ((), ()))
    tn = (((0,), (0,)), ((), ()))

    def finish(o, rg):
        d = o - jnp.mean(o, axis=-1, keepdims=True)
        y = d * lax.rsqrt(jnp.mean(d * d, axis=-1, keepdims=True) + EPS) * gain
        return (jax.nn.silu(rg.astype(F32)) * y).astype(BF16)

    def vcat(r):
        return jnp.concatenate([v_ref[0, r, :], v_ref[1, r, :]], axis=1)

    def gcat(r):
        return jnp.concatenate([rg_ref[0, r, :], rg_ref[1, r, :]], axis=1)

    def build_mask():
        dm_ref[...] = _decay_mask(T_P, lgf, lgb)

    n_chunk = SEQ_BLK // T_P
    chunks = [slice(c * T_P, (c + 1) * T_P) for c in range(n_chunk)]
    t = lax.broadcasted_iota(jnp.int32, (T_P, 1), 0).astype(F32)
    kdf = jnp.exp(lgf * (T_P - 1.0 - t)) * (RET_KD ** -0.5)
    kdb = jnp.exp(lgb * t) * (RET_KD ** -0.5)

    def intra(r):
        sc = lax.dot_general(q_ref[r, :], k_ref[r, :], nt, preferred_element_type=F32)
        return jnp.dot((sc * dm_ref[...]).astype(BF16), vcat(r), preferred_element_type=F32)

    def key_state(r, kd):
        kw = (k_ref[r, :].astype(F32) * kd).astype(BF16)
        return lax.dot_general(kw, vcat(r), tn, preferred_element_type=F32)

    def context_part():
        for s, r in enumerate(chunks):
            o_ref[r, :] = finish(intra(r), gcat(r))
            sf_ref[s] = key_state(r, kdf)
            sb_ref[s] = key_state(r, kdb)

    def latent_part():
        qdf = jnp.exp(lgf * (t + 1.0))
        qdb = jnp.exp(lgb * (T_P - t))
        span = jnp.full((1, 1), float(T_P), F32)
        cf = jnp.exp(lgf * span)
        cb = jnp.exp(lgb * span)
        before_b = [None] * n_chunk
        state = s0b_ref[...]
        for c in reversed(range(n_chunk)):
            before_b[c] = state
            if c > 0:
                state = cb * state + key_state(chunks[c], kdb)
        state = s0f_ref[...]
        for c, r in enumerate(chunks):
            qf = q_ref[r, :].astype(F32)
            o = (intra(r)
                 + jnp.dot((qf * qdf).astype(BF16), state.astype(BF16), preferred_element_type=F32)
                 + jnp.dot((qf * qdb).astype(BF16), before_b[c].astype(BF16), preferred_element_type=F32))
            o_ref[r, :] = finish(o, gcat(r))
            if c + 1 < n_chunk:
                state = cf * state + key_state(r, kdf)

    return build_mask, context_part, latent_part


ATT_TQ = 1024
ATT_HPS = 2
ATT_QSCALE = (DIFF_HD ** -0.5) * math.log2(math.e)


N_SIDE = 3


def _mixer_kernel(sc_ref, lg_ref, q_ref, k_ref, v_ref, ckt_ref, cv_ref, cos_ref, sin_ref, g_ref,
                  rq_ref, rk_ref, rv_ref, rg_ref, s0f_ref, s0b_ref, gn_ref, *rest):
    side_in = rest[:N_SIDE]
    o_ref, ro_ref, sf_ref, sb_ref = rest[N_SIDE:N_SIDE + 4]
    side_out = rest[N_SIDE + 4:2 * N_SIDE + 4]
    xs0_ref, q_scr, k_scr, v_scr, dm_scr = rest[2 * N_SIDE + 4:]
    i = pl.program_id(0)
    rh = pl.program_id(1)
    build_mask, ret_context, ret_latent = _ret_parts(
        lg_ref[0, rh], lg_ref[1, rh], rq_ref, rk_ref, rv_ref, rg_ref, s0f_ref, s0b_ref, gn_ref,
        ro_ref, sf_ref, sb_ref, dm_scr.at[rh])

    @pl.when(i == 0)
    def _():
        build_mask()

    def side_jobs():
        for src, dst in zip(side_in, side_out):
            dst[...] = src[...].astype(BF16)
        xs0_ref[...] = jnp.zeros_like(xs0_ref)

    lam = sc_ref[0]
    out_scale = sc_ref[1]
    gain = g_ref[...] * out_scale
    lane = lax.broadcasted_iota(jnp.int32, (1, LANE), 1)
    first = lane < DIFF_HD
    nt = (((1,), (1,)), ((), ()))

    def halves(q):
        zero = jnp.zeros_like(q)
        return jnp.where(first, q, zero), jnp.where(first, zero, q)

    def weights(s):
        return jnp.exp2(s - jnp.max(s, axis=-1, keepdims=True)).astype(BF16)

    def finish(of0, of1):
        o = of0[:, :LANE] / of0[:, LANE:] - lam * (of1[:, :LANE] / of1[:, LANE:])
        return (_rms(o) * gain).astype(BF16)

    @pl.when(i < N_PBLK)
    def _():
        ret_context()
        n_seq = SEQ_BLK // T_P
        rows = [slice(s * T_P, (s + 1) * T_P) for s in range(n_seq)]
        ones = jnp.ones((T_P, LANE), BF16)
        for hh in range(ATT_HPS):
            cols = slice(hh * LANE, (hh + 1) * LANE)
            q0, q1 = halves((q_ref[hh].astype(F32) * ATT_QSCALE).astype(BF16))
            s0 = jnp.concatenate([lax.dot_general(q0[r], k_ref[hh, r, :], nt, preferred_element_type=F32)
                                  for r in rows], axis=0)
            s1 = jnp.concatenate([lax.dot_general(q1[r], k_ref[hh, r, :], nt, preferred_element_type=F32)
                                  for r in rows], axis=0)
            e0 = weights(s0)
            e1 = weights(s1)
            for r in rows:
                v1 = jnp.concatenate([v_ref[hh, r, :], ones], axis=1)
                o_ref[r, cols] = finish(jnp.dot(e0[r], v1, preferred_element_type=F32),
                                        jnp.dot(e1[r], v1, preferred_element_type=F32))

    @pl.when(i >= N_PBLK)
    def _():
        side_jobs()
        ret_latent()
        cos = cos_ref[...]
        sin = sin_ref[...]
        low = (lax.broadcasted_iota(jnp.int32, (T_S, LANE), 1) & 16) == 0

        def rope(x):
            xs = jnp.where(low, pltpu.roll(x, LANE - 16, 1), pltpu.roll(x, 16, 1))
            return x * cos + xs * sin

        for hh in range(ATT_HPS):
            cols = slice(hh * LANE, (hh + 1) * LANE)
            head = pl.program_id(1) * ATT_HPS + hh
            q_scr[hh] = (rope(q_ref[hh].astype(F32)) * ATT_QSCALE).astype(BF16)
            k_scr[hh] = rope(k_ref[hh].astype(F32)).astype(BF16)
            ckt = ckt_ref[cols, :].astype(BF16)
            v_scr[hh, 0:T_S, 0:LANE] = v_ref[hh]
            v_scr[hh, T_S:T_S + PAST, 0:LANE] = cv_ref[:, head, :].astype(BF16)
            v_scr[hh, :, LANE:2 * LANE] = jnp.ones((T_S + PAST, LANE), BF16)

            def scores(qh, hh=hh, ckt=ckt):
                return jnp.concatenate([lax.dot_general(qh, k_scr[hh], nt, preferred_element_type=F32),
                                        jnp.dot(qh, ckt, preferred_element_type=F32)], axis=1)

            for b in range(T_S // ATT_TQ):
                r = slice(b * ATT_TQ, (b + 1) * ATT_TQ)
                q0, q1 = halves(q_scr[hh, r, :])
                e0 = weights(scores(q0))
                e1 = weights(scores(q1))
                o_ref[r, cols] = finish(jnp.dot(e0, v_scr[hh], preferred_element_type=F32),
                                        jnp.dot(e1, v_scr[hh], preferred_element_type=F32))


def _mixers(scal, lg, proj, cache_k, cache_v, cos_t, sin_t, subln_g, s0f, s0b, gnorm, side_weights):
    nb = N_TOK // SEQ_BLK
    spb = SEQ_BLK // T_P
    smp = lambda i: jnp.maximum(i - N_PBLK, 0)
    pmt = lambda i: jnp.minimum(i, N_PBLK - 1)
    pmh = lambda i, h: jnp.where(i < N_PBLK, h, RET_H - 1)
    state_in = pl.BlockSpec((None, None, RET_KD, RET_VD), lambda i, h: (smp(i), h, 0, 0))
    state_out = pl.BlockSpec((spb, None, RET_KD, RET_VD), lambda i, h: (pmt(i), pmh(i, h), 0, 0))

    n_hp = DIFF_H // ATT_HPS
    assert n_hp == RET_H
    n_steps = (nb - N_PBLK) * n_hp
    step = lambda i, h: jnp.maximum((i - N_PBLK) * n_hp + h, 0)
    exp_per_step = N_EXP // n_steps
    exp_slice = lambda i, h: (step(i, h), 0, 0)
    row_slice = lambda i, h: (step(i, h), 0)
    up_spec = pl.BlockSpec((exp_per_step, D, FF), exp_slice)
    down_spec = pl.BlockSpec((exp_per_step, FF, D), exp_slice)
    side_specs = [up_spec, up_spec, down_spec]
    side_shapes = [jax.ShapeDtypeStruct(a.shape, BF16) for a in side_weights]
    zrows = R_MAX // n_steps

    return pl.pallas_call(
        _mixer_kernel,
        out_shape=(jax.ShapeDtypeStruct((N_TOK, DIFF_H * 2 * DIFF_HD), BF16),
                   jax.ShapeDtypeStruct((N_TOK, RET_H * RET_VD), BF16),
                   jax.ShapeDtypeStruct((16, RET_H, RET_KD, RET_VD), F32),
                   jax.ShapeDtypeStruct((16, RET_H, RET_KD, RET_VD), F32),
                   *side_shapes,
                   jax.ShapeDtypeStruct((R_MAX, XW), BF16)),
        grid=(nb, n_hp),
        in_specs=[pl.BlockSpec(memory_space=pltpu.SMEM),
                  pl.BlockSpec(memory_space=pltpu.SMEM),
                  pl.BlockSpec((ATT_HPS, SEQ_BLK, LANE), lambda i, h: (C_DQ // ATT_HPS + h, i, 0)),
                  pl.BlockSpec((ATT_HPS, SEQ_BLK, LANE), lambda i, h: (C_DK // ATT_HPS + h, i, 0)),
                  pl.BlockSpec((ATT_HPS, SEQ_BLK, LANE), lambda i, h: (C_DV // ATT_HPS + h, i, 0)),
                  pl.BlockSpec((None, ATT_HPS * LANE, PAST), lambda i, h: (smp(i), h, 0)),
                  pl.BlockSpec((None, None, PAST, DIFF_H, LANE), lambda i, h: (smp(i), 0, 0, 0, 0)),
                  pl.BlockSpec((T_S, LANE), lambda i, h: (0, 0)),
                  pl.BlockSpec((T_S, LANE), lambda i, h: (0, 0)),
                  pl.BlockSpec((1, LANE), lambda i, h: (0, 0)),
                  pl.BlockSpec((None, SEQ_BLK, LANE), lambda i, h: (C_RQ + h, i, 0)),
                  pl.BlockSpec((None, SEQ_BLK, LANE), lambda i, h: (C_RK + h, i, 0)),
                  pl.BlockSpec((2, SEQ_BLK, LANE), lambda i, h: (C_RV // 2 + h, i, 0)),
                  pl.BlockSpec((2, SEQ_BLK, LANE), lambda i, h: (C_RG // 2 + h, i, 0)),
                  state_in, state_in,
                  pl.BlockSpec((1, RET_VD), lambda i, h: (0, h)),
                  *side_specs],
        out_specs=(pl.BlockSpec((SEQ_BLK, ATT_HPS * LANE), lambda i, h: (i, h)),
                   pl.BlockSpec((SEQ_BLK, RET_VD), lambda i, h: (i, h)),
                   state_out, state_out,
                   *side_specs,
                   pl.BlockSpec((zrows, XW), row_slice)),
        scratch_shapes=[pltpu.VMEM((ATT_HPS, T_S, LANE), BF16),
                        pltpu.VMEM((ATT_HPS, T_S, LANE), BF16),
                        pltpu.VMEM((ATT_HPS, T_S + PAST, 2 * LANE), BF16),
                        pltpu.VMEM((RET_H, T_P, T_P), F32)],
        compiler_params=_cparams(("arbitrary", "arbitrary")),
        name="mixers",
    )(scal, lg, proj, proj, proj, cache_k, cache_v, cos_t, sin_t, subln_g,
      proj, proj, proj, proj, s0f, s0b, gnorm, *side_weights)


def _rope_tables():
    n_rows = T_S // GRID_W
    row = np.repeat(np.arange(n_rows), GRID_W).astype(np.float64)
    col = np.tile(np.arange(GRID_W), n_rows).astype(np.float64)
    n_freq = DIFF_HD // 4
    inv = ROPE_BASE ** (-np.arange(n_freq, dtype=np.float64) / n_freq)

    def axis_tables(pos):
        ang = pos[:, None] * inv[None, :]
        c = np.cos(ang)
        s = np.sin(ang)
        return np.concatenate([c, c], axis=-1), np.concatenate([-s, s], axis=-1)

    cr, sr = axis_tables(row)
    cc, sc = axis_tables(col)
    cos_h = np.concatenate([cr, cc], axis=-1)
    sin_h = np.concatenate([sr, sc], axis=-1)
    return (jnp.asarray(np.concatenate([cos_h, cos_h], axis=-1), F32),
            jnp.asarray(np.concatenate([sin_h, sin_h], axis=-1), F32))


OP_TM = 512
OP_NPT = N_PROMPT // OP_TM


def _outproj_kernel(ra_ref, da_ref, gr_ref, gd_ref, xp_ref, xs_ref, mod_ref, n2_ref,
                    wro32_ref, wdo32_ref, wo32_ref, rgw_ref, rew_ref, brt_ref,
                    x1_ref, h2_ref, cw8_ref, info_ref, infot_ref, off_ref, pctab_ref, steps_ref,
                    m_scr, wro_ref, wdo_ref, wo_ref, wrt_ref, pc_scr):
    i = pl.program_id(0)

    @pl.when(i == 0)
    def _():
        wro_ref[...] = wro32_ref[...].astype(BF16)
        wdo_ref[...] = wdo32_ref[...].astype(BF16)
        wo_ref[...] = wo32_ref[...].astype(BF16)
        gap = jnp.zeros((D, ROUTE_E0 - N_GROUPS), F32)
        pad = jnp.zeros((D, ROUTE_W - ROUTE_E0 - N_EXP), F32)
        wr = jnp.concatenate([rgw_ref[...], gap, rew_ref[...], pad], axis=1)
        wrt_ref[...] = wr.T.astype(BF16)

    ret_out = jnp.dot(ra_ref[...], wro_ref[...], preferred_element_type=F32)
    diff_out = jnp.dot(da_ref[...], wdo_ref[...], preferred_element_type=F32)
    for c in range(D // LANE):
        sl = slice(c * LANE, (c + 1) * LANE)
        m = (jax.nn.sigmoid(gr_ref[c].astype(F32)) * ret_out[:, sl]
             + jax.nn.sigmoid(gd_ref[c].astype(F32)) * diff_out[:, sl])
        m_scr[:, sl] = m.astype(BF16)
    mix = jnp.dot(m_scr[...], wo_ref[...], preferred_element_type=F32)
    mod = mod_ref[...]

    x1 = jnp.where(i < OP_NPT, xp_ref[...], xs_ref[...]) + mod[:, 2 * D:3 * D] * mix
    x1_ref[...] = x1
    h2 = (_rms(x1) * n2_ref[...] * (1.0 + mod[:, 4 * D:5 * D]) + mod[:, 3 * D:4 * D]).astype(BF16)
    h2_ref[...] = h2
    logits_t = lax.dot_general(wrt_ref[...], h2, (((1,), (1,)), ((), ())),
                               preferred_element_type=F32) + brt_ref[...]
    cw8_ref[...], info_ref[...], pc, infot_ref[...] = _route_cols(logits_t)
    pc_scr[pl.ds(i, 1), :] = pc

    @pl.when(i == RT_NT - 1)
    def _():
        off_ref[...], pctab_ref[...], steps_ref[...] = _dispatch_plan(pc_scr[...])


def _outproj(ret_act, diff_act, proj, xp, xs, mod3, n2, wro, wdo, wo, rgw, rew, brt):
    assert OP_TM == RT_TM
    npt = OP_NPT
    full = lambda i: (0, 0)
    once = pl.Buffered(1)
    return pl.pallas_call(
        _outproj_kernel,
        out_shape=(jax.ShapeDtypeStruct((N_TOK, D), F32),
                   jax.ShapeDtypeStruct((N_TOK, D), BF16),
                   jax.ShapeDtypeStruct((N_TOK, ROUTE_W), F32),
                   jax.ShapeDtypeStruct((N_TOK, ROUTE_W), F32),
                   jax.ShapeDtypeStruct((RT_NT, EPG, RT_TM), F32),
                   jax.ShapeDtypeStruct((RT_NT, ROUTE_W), jnp.int32),
                   jax.ShapeDtypeStruct((RT_NT, ROUTE_W), jnp.int32),
                   jax.ShapeDtypeStruct((PLAN_ROWS, ROUTE_W), jnp.int32)),
        grid=(N_TOK // OP_TM,),
        in_specs=[pl.BlockSpec((OP_TM, D), lambda i: (i, 0)),
                  pl.BlockSpec((OP_TM, D), lambda i: (i, 0)),
                  pl.BlockSpec((8, OP_TM, LANE), lambda i: (C_GR // 8, i, 0)),
                  pl.BlockSpec((8, OP_TM, LANE), lambda i: (C_GD // 8, i, 0)),
                  pl.BlockSpec((OP_TM, D), lambda i: (jnp.minimum(i, npt - 1), 0)),
                  pl.BlockSpec((OP_TM, D), lambda i: (jnp.maximum(i - npt, 0), 0)),
                  pl.BlockSpec((None, 1, 6 * D), lambda i: (_mod_row(OP_TM)(i), 0, 0)),
                  pl.BlockSpec((1, D), full),
                  pl.BlockSpec((D, D), full, pipeline_mode=once),
                  pl.BlockSpec((D, D), full, pipeline_mode=once),
                  pl.BlockSpec((D, D), full, pipeline_mode=once),
                  pl.BlockSpec((D, N_GROUPS), full, pipeline_mode=once),
                  pl.BlockSpec((D, N_EXP), full, pipeline_mode=once),
                  pl.BlockSpec((ROUTE_W, 1), full)],
        out_specs=(pl.BlockSpec((OP_TM, D), lambda i: (i, 0)),
                   pl.BlockSpec((OP_TM, D), lambda i: (i, 0)),
                   pl.BlockSpec((OP_TM, ROUTE_W), lambda i: (i, 0)),
                   pl.BlockSpec((OP_TM, ROUTE_W), lambda i: (i, 0)),
                   pl.BlockSpec((None, EPG, RT_TM), lambda i: (i, 0, 0)),
                   pl.BlockSpec((RT_NT, ROUTE_W), full),
                   pl.BlockSpec((RT_NT, ROUTE_W), full),
                   pl.BlockSpec((PLAN_ROWS, ROUTE_W), full)),
        scratch_shapes=[pltpu.VMEM((OP_TM, D), BF16),
                        pltpu.VMEM((D, D), BF16), pltpu.VMEM((D, D), BF16), pltpu.VMEM((D, D), BF16),
                        pltpu.VMEM((ROUTE_W, D), BF16),
                        pltpu.VMEM((RT_NT, ROUTE_W), F32)],
        compiler_params=_cparams(("arbitrary",)),
        name="outproj",
    )(ret_act, diff_act, proj, proj, xp, xs, mod3, n2, wro, wdo, wo, rgw, rew, brt)


RT_TM = 512
RT_NT = N_TOK // RT_TM
PIECE = 16
R_LOC = RT_TM + N_GROUPS * PIECE
R_STAGE = 640
EX_TM = 512
R_MAX = 11264
EX_NT = R_MAX // EX_TM


ROUTE_E0 = EPG


def _route_cols(lgt):
    row = lax.broadcasted_iota(jnp.int32, (EPG, RT_TM), 0)
    neg = jnp.float32(-jnp.inf)

    def first_row(cond):
        return jnp.min(jnp.where(cond, row, EPG), axis=0, keepdims=True)

    head = lgt[0:EPG]
    is_g = row < N_GROUPS
    gl = jnp.where(is_g, head, neg)
    gmax = jnp.max(gl, axis=0, keepdims=True)
    gsum = jnp.sum(jnp.where(is_g, jnp.exp(head - gmax), 0.0), axis=0, keepdims=True)
    p_top = 1.0 / gsum
    g_idx = first_row(gl == gmax)
    el = lgt[ROUTE_E0 + (N_GROUPS - 1) * EPG:ROUTE_E0 + N_GROUPS * EPG]
    for g in reversed(range(N_GROUPS - 1)):
        el = jnp.where(g_idx == g, lgt[ROUTE_E0 + g * EPG:ROUTE_E0 + (g + 1) * EPG], el)
    ee = jnp.exp(el - jnp.max(el, axis=0, keepdims=True))
    ep = ee / jnp.sum(ee, axis=0, keepdims=True)
    e1 = jnp.max(ep, axis=0, keepdims=True)
    i1 = first_row(ep == e1)
    ep2 = jnp.where(row == i1, -1.0, ep)
    e2 = jnp.max(ep2, axis=0, keepdims=True)
    i2 = first_row(ep2 == e2)
    den = e1 + e2
    cw8_t = (jnp.where(row == i1, p_top * e1 / den, 0.0)
             + jnp.where(row == i2, p_top * e2 / den, 0.0))

    onehot = (row == g_idx).astype(F32)
    ii = lax.broadcasted_iota(jnp.int32, (RT_TM, RT_TM), 0)
    jj = lax.broadcasted_iota(jnp.int32, (RT_TM, RT_TM), 1)
    earlier = (ii < jj).astype(BF16)
    prefix = jnp.dot(onehot.astype(BF16), earlier, preferred_element_type=F32)
    cnt = jnp.sum(onehot, axis=1, keepdims=True)
    pc_col = jnp.floor((cnt + (PIECE - 1.0)) * (1.0 / PIECE)) * PIECE
    row1 = lax.broadcasted_iota(jnp.int32, (EPG, 1), 0)
    lane1 = lax.broadcasted_iota(jnp.int32, (1, ROUTE_W), 1)
    lo = jnp.zeros((EPG, 1), F32)
    run = jnp.zeros((1, 1), F32)
    pc = jnp.zeros((1, ROUTE_W), F32)
    for g in range(N_GROUPS):
        pc_g = jnp.sum(jnp.where(row1 == g, pc_col, 0.0), axis=0, keepdims=True)
        lo = jnp.where(row1 == g, run, lo)
        pc = jnp.where(lane1 == g, pc_g, pc)
        run = run + pc_g
    dest = jnp.sum(onehot * (prefix + lo), axis=0, keepdims=True)
    info_t = jnp.where(row == 0, g_idx.astype(F32), jnp.where(row == 1, dest, 0.0))

    slab = jnp.concatenate([cw8_t, info_t, jnp.zeros((LANE - 2 * EPG, RT_TM), F32)], axis=0)
    cols = slab.T
    lane = lax.broadcasted_iota(jnp.int32, cols.shape, 1)
    cw8 = jnp.where(lane < EPG, cols, 0.0)
    info = jnp.where(lane < 2, pltpu.roll(cols, LANE - EPG, 1), 0.0)
    return cw8, info, pc, info_t


PLAN_ROWS = 32


def _dispatch_plan(pc_all):
    lane = lax.broadcasted_iota(jnp.int32, (1, ROUTE_W), 1)
    is_g = lane < N_GROUPS
    seg_len = jnp.sum(pc_all, axis=0, keepdims=True)
    seg_pad = jnp.floor((seg_len + (EX_TM - 1.0)) * (1.0 / EX_TM)) * EX_TM
    run = seg_pad + pltpu.roll(seg_pad, 1, 1)
    seg_end_pad = jnp.where(is_g, run + pltpu.roll(run, 2, 1), 0.0)
    seg_start = seg_end_pad - seg_pad
    ti = lax.broadcasted_iota(jnp.int32, (RT_NT, RT_NT), 0)
    tj = lax.broadcasted_iota(jnp.int32, (RT_NT, RT_NT), 1)
    earlier = (tj < ti).astype(BF16)
    within = jnp.dot(earlier, pc_all.astype(BF16), preferred_element_type=F32)
    chunk_off = jnp.where(is_g, seg_start + within, 0.0)

    start = (lax.broadcasted_iota(jnp.int32, (PLAN_ROWS, 1), 0) * EX_TM).astype(F32)
    passed = jnp.where(jnp.logical_and(is_g, start >= seg_end_pad), 1.0, 0.0)
    group = jnp.minimum(jnp.sum(passed, axis=1, keepdims=True), N_GROUPS - 1.0)
    used_end = jnp.sum(jnp.where(lane == group.astype(jnp.int32), seg_start + seg_len, 0.0),
                       axis=1, keepdims=True)
    used = jnp.clip(used_end - start, 0.0, float(EX_TM))
    steps = jnp.where(lane == 0, group, jnp.where(lane == 1, used, 0.0))
    return chunk_off.astype(jnp.int32), pc_all.astype(jnp.int32), steps.astype(jnp.int32)


def _piece_copies(off_ref, pc_ref, tile, make):
    lo = 0
    for g in range(N_GROUPS):
        n = pc_ref[tile, g] // PIECE
        base = off_ref[tile, g]

        def body(j, carry, lo=lo, base=base):
            make(pl.multiple_of(lo + j * PIECE, PIECE), pl.multiple_of(base + j * PIECE, PIECE))
            return carry

        lax.fori_loop(0, n, body, 0)
        lo = lo + pc_ref[tile, g]


def _piece_count(pc_ref, tile):
    n = 0
    for g in range(N_GROUPS):
        n = n + pc_ref[tile, g] // PIECE
    return n


XW = D + ROUTE_W


def _dispatch_kernel(off_ref, pc_ref, h_ref, infot_ref, cw8_ref, xs_in, xs_out, x_scr, sem):
    del xs_in
    i = pl.program_id(0)
    slot = i % 2
    dest = infot_ref[1:2, :]
    row = lax.broadcasted_iota(jnp.int32, (R_LOC, RT_TM), 0).astype(F32)
    sel = (row == dest).astype(BF16)
    cw = cw8_ref[...]
    hi = cw.astype(BF16).astype(F32)
    mid = (cw - hi).astype(BF16).astype(F32)
    low = (cw - hi - mid).astype(BF16).astype(F32)
    pieces = (hi + pltpu.roll(mid, EPG, 1) + pltpu.roll(low, 2 * EPG, 1)).astype(BF16)
    rows = jnp.concatenate([h_ref[...], pieces], axis=1)
    x_scr[slot] = jnp.dot(sel, rows, preferred_element_type=F32).astype(BF16)

    def x_copy(s, src, dst):
        return pltpu.make_async_copy(x_scr.at[s, pl.ds(src, PIECE)], xs_out.at[pl.ds(dst, PIECE)],
                                     sem.at[s])

    _piece_copies(off_ref, pc_ref, i, lambda src, dst: x_copy(slot, src, dst).start())

    def wait_tile(tile, s):
        def wait(j, carry):
            x_copy(s, 0, 0).wait()
            return carry

        lax.fori_loop(0, _piece_count(pc_ref, tile), wait, 0)

    @pl.when(i > 0)
    def _():
        wait_tile(i - 1, 1 - slot)

    @pl.when(i == RT_NT - 1)
    def _():
        wait_tile(i, slot)


def _dispatch(chunk_off, pc, h2, info_t, cw8, xs0):
    grid_spec = pltpu.PrefetchScalarGridSpec(
        num_scalar_prefetch=2,
        grid=(RT_NT,),
        in_specs=[pl.BlockSpec((RT_TM, D), lambda i, o, p: (i, 0)),
                  pl.BlockSpec((None, EPG, RT_TM), lambda i, o, p: (i, 0, 0)),
                  pl.BlockSpec((RT_TM, ROUTE_W), lambda i, o, p: (i, 0)),
                  pl.BlockSpec(memory_space=pl.ANY)],
        out_specs=pl.BlockSpec(memory_space=pl.ANY),
        scratch_shapes=[pltpu.VMEM((2, R_LOC, XW), BF16),
                        pltpu.SemaphoreType.DMA((2,))])
    return pl.pallas_call(
        _dispatch_kernel,
        out_shape=jax.ShapeDtypeStruct((R_MAX, XW), BF16),
        grid_spec=grid_spec,
        input_output_aliases={5: 0},
        compiler_params=_cparams(("arbitrary",)),
        name="dispatch",
    )(chunk_off, pc, h2, info_t, cw8, xs0)


def _expert_kernel(steps_ref, x_ref, wg_ref, wu_ref, wd_ref, y_ref):
    k = pl.program_id(0)
    used = steps_ref[k, 1]
    half = EX_TM // 2

    def run(rows):
        x = x_ref[rows, 0:D]
        cw = x_ref[rows, D:XW].astype(F32)
        lane = lax.broadcasted_iota(jnp.int32, cw.shape, 1)
        acc = jnp.zeros(x.shape, F32)
        for j in range(EPG):
            mine = jnp.logical_and((lane & (EPG - 1)) == j, lane < 3 * EPG)
            w = jnp.sum(jnp.where(mine, cw, 0.0), axis=-1, keepdims=True)
            a = (jax.nn.silu(jnp.dot(x, wg_ref[j], preferred_element_type=F32))
                 * jnp.dot(x, wu_ref[j], preferred_element_type=F32))
            acc = acc + jnp.dot((a * w).astype(BF16), wd_ref[j], preferred_element_type=F32)
        y_ref[rows, :] = acc.astype(BF16)

    @pl.when(used > half)
    def _():
        run(slice(0, EX_TM))

    @pl.when(jnp.logical_and(used > 0, used <= half))
    def _():
        run(slice(0, half))
        y_ref[half:EX_TM, :] = jnp.zeros((EX_TM - half, D), BF16)

    @pl.when(used == 0)
    def _():
        y_ref[...] = jnp.zeros_like(y_ref)


def _experts(steps, xs, wg, wu, wd):
    grid_spec = pltpu.PrefetchScalarGridSpec(
        num_scalar_prefetch=1,
        grid=(EX_NT,),
        in_specs=[pl.BlockSpec((EX_TM, XW), lambda k, st: (k, 0)),
                  pl.BlockSpec((EPG, D, FF), lambda k, st: (st[k, 0], 0, 0)),
                  pl.BlockSpec((EPG, D, FF), lambda k, st: (st[k, 0], 0, 0)),
                  pl.BlockSpec((EPG, FF, D), lambda k, st: (st[k, 0], 0, 0))],
        out_specs=pl.BlockSpec((EX_TM, D), lambda k, st: (k, 0)))
    return pl.pallas_call(
        _expert_kernel,
        out_shape=jax.ShapeDtypeStruct((R_MAX, D), BF16),
        grid_spec=grid_spec,
        compiler_params=_cparams(("arbitrary",)),
        name="experts",
    )(steps, xs, wg, wu, wd)


CB_NPT = N_PROMPT // RT_TM


def _combine_kernel(off_ref, pc_ref, info_ref, x1_ref, mod_ref, g_ref, ys_hbm,
                    yp_ref, ysm_ref, stage, sem):
    i = pl.program_id(0)
    slot = i % 2

    def copy(s, dst, src):
        return pltpu.make_async_copy(ys_hbm.at[pl.ds(src, PIECE)], stage.at[s, pl.ds(dst, PIECE)],
                                     sem.at[s])

    def fetch(tile, s):
        _piece_copies(off_ref, pc_ref, tile, lambda loc, glob: copy(s, loc, glob).start())

    @pl.when(i == 0)
    def _():
        stage[...] = jnp.zeros_like(stage)
        fetch(0, 0)

    @pl.when(i + 1 < RT_NT)
    def _():
        fetch(i + 1, 1 - slot)

    def wait(j, carry):
        copy(slot, 0, 0).wait()
        return carry

    lax.fori_loop(0, _piece_count(pc_ref, i), wait, 0)

    dest = info_ref[...][:, 1:2]
    col = lax.broadcasted_iota(jnp.int32, (RT_TM, R_STAGE), 1).astype(F32)
    moe = jnp.dot((col == dest).astype(BF16), stage[slot], preferred_element_type=F32)
    mod = mod_ref[...]
    out = _rms(x1_ref[...] + mod[:, 5 * D:6 * D] * moe) * g_ref[...]

    @pl.when(i < CB_NPT)
    def _():
        yp_ref[...] = out

    @pl.when(i >= CB_NPT)
    def _():
        ysm_ref[...] = out


def _combine(chunk_off, pc, info, x1, mod3, fg, ys):
    npt = CB_NPT
    grid_spec = pltpu.PrefetchScalarGridSpec(
        num_scalar_prefetch=2,
        grid=(RT_NT,),
        in_specs=[pl.BlockSpec((RT_TM, ROUTE_W), lambda i, o, p: (i, 0)),
                  pl.BlockSpec((RT_TM, D), lambda i, o, p: (i, 0)),
                  pl.BlockSpec((None, 1, 6 * D), lambda i, o, p: (_mod_row(RT_TM)(i), 0, 0)),
                  pl.BlockSpec((1, D), lambda i, o, p: (0, 0)),
                  pl.BlockSpec(memory_space=pl.ANY)],
        out_specs=(pl.BlockSpec((RT_TM, D), lambda i, o, p: (jnp.minimum(i, npt - 1), 0)),
                   pl.BlockSpec((RT_TM, D), lambda i, o, p: (jnp.maximum(i - npt, 0), 0))),
        scratch_shapes=[pltpu.VMEM((2, R_STAGE, D), BF16),
                        pltpu.SemaphoreType.DMA((2,))])
    return pl.pallas_call(
        _combine_kernel,
        out_shape=(jax.ShapeDtypeStruct((N_PROMPT, D), F32),
                   jax.ShapeDtypeStruct((N_SAMPLE, D), F32)),
        grid_spec=grid_spec,
        compiler_params=_cparams(("arbitrary",)),
        name="combine",
    )(chunk_off, pc, info, x1, mod3, fg, ys)


def kernel(x_prompt, x_sample, cache_diff_k, cache_diff_v, state_ret_fwd, state_ret_bwd, c, c_ctx,
           w_ada, b_ada, norm1_g, norm2_g, w_in, ret_decay_fwd, ret_decay_bwd, ret_norm_g,
           diff_lambda_q1, diff_lambda_k1, diff_lambda_q2, diff_lambda_k2, diff_subln_g,
           w_ret_o, w_diff_o, w_o, router_group_w, router_group_b, router_expert_w, router_expert_b,
           moe_w_gate, moe_w_up, moe_w_down, final_norm_g):
    l = 0
    lam_init = 0.8 - 0.6 * math.exp(-0.3 * l)
    lam = (jnp.exp(jnp.sum(diff_lambda_q1[l].astype(F32) * diff_lambda_k1[l].astype(F32)))
           - jnp.exp(jnp.sum(diff_lambda_q2[l].astype(F32) * diff_lambda_k2[l].astype(F32))) + lam_init)
    attn_scal = jnp.stack([lam, jnp.float32(1.0 - lam_init)]).astype(F32)
    lg = jnp.stack([jax.nn.log_sigmoid(ret_decay_fwd[l].astype(F32)),
                    jax.nn.log_sigmoid(ret_decay_bwd[l].astype(F32))])

    xp = x_prompt.reshape(N_PROMPT, D)
    xs = x_sample.reshape(N_SAMPLE, D)
    mod3 = _modulation(c_ctx[None, :], c, w_ada[l], b_ada[l][None, :])

    proj, kt32, v32 = _inproj(xp, xs, mod3, norm1_g[l][None, :], w_in[l])

    cos_t, sin_t = _rope_tables()
    cache_kt = jnp.transpose(cache_diff_k[:, l], (0, 2, 3, 4, 1)).reshape(4, D, PAST)
    (diff_act, ret_act, s_f, s_b, wg_bf, wu_bf, wd_bf, xs0) = _mixers(
        attn_scal, lg, proj, cache_kt, cache_diff_v, cos_t, sin_t, diff_subln_g[l][None, :],
        state_ret_fwd[:, l], state_ret_bwd[:, l], ret_norm_g[l][None, :],
        (moe_w_gate[l], moe_w_up[l], moe_w_down[l]))

    brt = jnp.concatenate([router_group_b[l], jnp.zeros((ROUTE_E0 - N_GROUPS,), F32), router_expert_b[l],
                           jnp.zeros((ROUTE_W - ROUTE_E0 - N_EXP,), F32)])[:, None]
    x1, h2, cw8, info, info_t, chunk_off, pc, steps = _outproj(
        ret_act, diff_act, proj, xp, xs, mod3, norm2_g[l][None, :],
        w_ret_o[l], w_diff_o[l], w_o[l], router_group_w[l], router_expert_w[l], brt)
    xs_sorted = _dispatch(chunk_off, pc, h2, info_t, cw8, xs0)
    y_sorted = _experts(steps, xs_sorted, wg_bf, wu_bf, wd_bf)
    yp, ys = _combine(chunk_off, pc, info, x1, mod3, final_norm_g[None, :], y_sorted)

    return (yp.reshape(16, T_P, D), ys.reshape(4, T_S, D),
            jnp.transpose(kt32.reshape(16, DIFF_H, 2, DIFF_HD, T_P), (0, 4, 1, 2, 3))[:, None],
            v32.reshape(16, 1, T_P, DIFF_H, 2 * DIFF_HD),
            s_f.reshape(16, 1, RET_H, RET_KD, RET_VD), s_b.reshape(16, 1, RET_H, RET_KD, RET_VD))
```

```python
import functools
import math

import jax
import jax.numpy as jnp
import numpy as np
from jax import lax
from jax.experimental import pallas as pl
from jax.experimental.pallas import tpu as pltpu

F32 = jnp.float32
BF16 = jnp.bfloat16

D = 1024
N_PROMPT = 16 * 256
N_SAMPLE = 4 * 1024
N_TOK = N_PROMPT + N_SAMPLE
T_P = 256
T_S = 1024
PAST = 512
GRID_W = 64
RET_H = 4
RET_KD = 128
RET_VD = 256
DIFF_H = 8
DIFF_HD = 64
ROPE_BASE = 10000.0
N_GROUPS = 4
EPG = 8
N_EXP = 32
FF = 256
EPS = 1e-6
IN_W = 8192
LANE = 128
N_CHUNK = IN_W // LANE
C_RQ, C_RK, C_RV, C_RG, C_DQ, C_DK, C_DV, C_GR, C_GD = 0, 4, 8, 16, 24, 32, 40, 48, 56
ROUTE_W = 128
SEQ_BLK = 1024
N_PBLK = N_PROMPT // SEQ_BLK
V7X_VMEM_BYTES = 64 * 1024 * 1024
VMEM_LIMIT = V7X_VMEM_BYTES - 8 * 1024 * 1024


def _cparams(sem):
    return pltpu.CompilerParams(dimension_semantics=sem, vmem_limit_bytes=VMEM_LIMIT)


def _mod_row(tile_rows):
    def f(i):
        start = i * tile_rows
        return jnp.where(start < N_PROMPT, 0, 1 + (start - N_PROMPT) // T_S)
    return f


def _rms(x):
    return x * lax.rsqrt(jnp.mean(x * x, axis=-1, keepdims=True) + EPS)


MOD_ROWS = 8


AUX_LAM = 0
AUX_LG = 1


def _mod_kernel(lam_init, ctx_ref, c_ref, w_ref, b_ref, lq1_ref, lk1_ref, lq2_ref, lk2_ref, df_ref, db_ref,
                gb_ref, eb_ref, o_ref, aux_ref, brt_ref):
    cond = jnp.concatenate([ctx_ref[...], c_ref[...],
                            jnp.zeros((MOD_ROWS - 1 - c_ref.shape[0], D), F32)], axis=0)
    s = jax.nn.silu(cond)
    out = jnp.dot(s.astype(BF16), w_ref[...].astype(BF16), preferred_element_type=F32) + b_ref[...]
    for r in range(MOD_ROWS):
        o_ref[r] = out[r:r + 1, :]

    def lane_row(parts):
        used = sum(p.shape[1] for p in parts)
        return jnp.concatenate(parts + [jnp.zeros((1, LANE - used), F32)], axis=1)

    lam = (jnp.exp(jnp.sum(lq1_ref[...] * lk1_ref[...], axis=1, keepdims=True))
           - jnp.exp(jnp.sum(lq2_ref[...] * lk2_ref[...], axis=1, keepdims=True)) + lam_init)
    rows = [lane_row([lam, jnp.full((1, 1), 1.0 - lam_init, F32)]),
            lane_row([jax.nn.log_sigmoid(df_ref[...])]),
            lane_row([jax.nn.log_sigmoid(db_ref[...])])]
    aux_ref[...] = jnp.concatenate(rows + [jnp.zeros((8 - len(rows), LANE), F32)], axis=0)
    bias = lane_row([gb_ref[...], jnp.zeros((1, ROUTE_E0 - N_GROUPS), F32), eb_ref[...]])
    brt_ref[...] = jnp.tile(bias, (8, 1)).T[:, 0:1]


def _modulation(lam_init, c_ctx, c, w_ada, b_ada, small):
    tn = 1536
    whole = lambda a: pl.BlockSpec(a.shape, lambda j: (0, 0))
    return pl.pallas_call(
        functools.partial(_mod_kernel, lam_init),
        out_shape=(jax.ShapeDtypeStruct((MOD_ROWS, 1, 6 * D), F32),
                   jax.ShapeDtypeStruct((8, LANE), F32),
                   jax.ShapeDtypeStruct((ROUTE_W, 1), F32)),
        grid=(6 * D // tn,),
        in_specs=[pl.BlockSpec((1, D), lambda j: (0, 0)),
                  pl.BlockSpec(c.shape, lambda j: (0, 0)),
                  pl.BlockSpec((D, tn), lambda j: (0, j)),
                  pl.BlockSpec((1, tn), lambda j: (0, j))] + [whole(a) for a in small],
        out_specs=(pl.BlockSpec((MOD_ROWS, 1, tn), lambda j: (0, 0, j)),
                   pl.BlockSpec((8, LANE), lambda j: (0, 0)),
                   pl.BlockSpec((ROUTE_W, 1), lambda j: (0, 0))),
        compiler_params=_cparams(("arbitrary",)),
        name="mod",
    )(c_ctx, c, w_ada, b_ada, *small)


IP_TM = 512
IP_TN = 2048
IP_NPT = N_PROMPT // IP_TM


IP_KV_TILE = C_DK * LANE // IP_TN
IP_SPT = IP_TM // T_P


def _inproj_kernel(xp_ref, xs_ref, mod_ref, n1_ref, w_ref, proj_ref, kt_ref, v32_ref, w_scr):
    j = pl.program_id(0)
    i = pl.program_id(1)

    @pl.when(i == 0)
    def _():
        w_scr[...] = w_ref[...].astype(BF16)

    x = jnp.where(i < IP_NPT, xp_ref[...], xs_ref[...])
    mod = mod_ref[...]
    h = (_rms(x) * n1_ref[...] * (1.0 + mod[:, D:2 * D]) + mod[:, 0:D]).astype(BF16)
    acc = jnp.dot(h, w_scr[...], preferred_element_type=F32)
    for c in range(IP_TN // LANE):
        proj_ref[c] = acc[:, c * LANE:(c + 1) * LANE].astype(BF16)

    @pl.when(jnp.logical_and(j == IP_KV_TILE, i < IP_NPT))
    def _():
        for s in range(IP_SPT):
            kt_ref[s] = acc[s * T_P:(s + 1) * T_P, :D].T
        v32_ref[...] = acc[:, D:]


def _inproj(xp, xs, mod3, n1, w_in):
    npt = IP_NPT
    cpt = IP_TN // LANE

    def kv_tile(j, i):
        return jnp.where(j < IP_KV_TILE, 0, jnp.where(j == IP_KV_TILE, jnp.minimum(i, npt - 1), npt - 1))

    return pl.pallas_call(
        _inproj_kernel,
        out_shape=(jax.ShapeDtypeStruct((N_CHUNK, N_TOK, LANE), BF16),
                   jax.ShapeDtypeStruct((N_PROMPT // T_P, D, T_P), F32),
                   jax.ShapeDtypeStruct((N_PROMPT, D), F32)),
        grid=(IN_W // IP_TN, N_TOK // IP_TM),
        in_specs=[pl.BlockSpec((IP_TM, D), lambda j, i: (jnp.minimum(i, npt - 1), 0)),
                  pl.BlockSpec((IP_TM, D), lambda j, i: (jnp.maximum(i - npt, 0), 0)),
                  pl.BlockSpec((None, 1, 6 * D), lambda j, i: (_mod_row(IP_TM)(i), 0, 0)),
                  pl.BlockSpec((1, D), lambda j, i: (0, 0)),
                  pl.BlockSpec((D, IP_TN), lambda j, i: (0, j))],
        out_specs=(pl.BlockSpec((cpt, IP_TM, LANE), lambda j, i: (j, i, 0)),
                   pl.BlockSpec((IP_SPT, D, T_P), lambda j, i: (kv_tile(j, i), 0, 0)),
                   pl.BlockSpec((IP_TM, D), lambda j, i: (kv_tile(j, i), 0))),
        scratch_shapes=[pltpu.VMEM((D, IP_TN), BF16)],
        compiler_params=_cparams(("arbitrary", "arbitrary")),
        name="inproj",
    )(xp, xs, mod3, n1, w_in)


def _decay_mask(t, lgf, lgb):
    ii = lax.broadcasted_iota(jnp.int32, (t, t), 0)
    jj = lax.broadcasted_iota(jnp.int32, (t, t), 1)
    rel = (ii - jj).astype(F32)
    e = jnp.exp(jnp.where(rel >= 0.0, lgf, -lgb) * rel)
    return jnp.where(rel == 0.0, 2.0, e) * (RET_KD ** -0.5)


def _ret_parts(lgf, lgb, q_ref, k_ref, v_ref, rg_ref, s0f_ref, s0b_ref, g_ref,
               o_ref, sf_ref, sb_ref, dm_ref):
    gain = g_ref[...]
    nt = (((1,), (1,)), ((), ()))
    tn = (((0,), (0,)), ((), ()))

    def finish(o, rg):
        d = o - jnp.mean(o, axis=-1, keepdims=True)
        y = d * lax.rsqrt(jnp.mean(d * d, axis=-1, keepdims=True) + EPS) * gain
        return (jax.nn.silu(rg.astype(F32)) * y).astype(BF16)

    def vcat(r):
        return jnp.concatenate([v_ref[0, r, :], v_ref[1, r, :]], axis=1)

    def gcat(r):
        return jnp.concatenate([rg_ref[0, r, :], rg_ref[1, r, :]], axis=1)

    def build_mask():
        dm_ref[...] = _decay_mask(T_P, lgf, lgb)

    n_chunk = SEQ_BLK // T_P
    chunks = [slice(c * T_P, (c + 1) * T_P) for c in range(n_chunk)]
    t = lax.broadcasted_iota(jnp.int32, (T_P, 1), 0).astype(F32)
    kdf = jnp.exp(lgf * (T_P - 1.0 - t)) * (RET_KD ** -0.5)
    kdb = jnp.exp(lgb * t) * (RET_KD ** -0.5)

    def intra(r):
        sc = lax.dot_general(q_ref[r, :], k_ref[r, :], nt, preferred_element_type=F32)
        return jnp.dot((sc * dm_ref[...]).astype(BF16), vcat(r), preferred_element_type=F32)

    def key_state(r, kd):
        kw = (k_ref[r, :].astype(F32) * kd).astype(BF16)
        return lax.dot_general(kw, vcat(r), tn, preferred_element_type=F32)

    def context_part():
        for s, r in enumerate(chunks):
            o_ref[r, :] = finish(intra(r), gcat(r))
            sf_ref[s] = key_state(r, kdf)
            sb_ref[s] = key_state(r, kdb)

    def latent_part():
        qdf = jnp.exp(lgf * (t + 1.0))
        qdb = jnp.exp(lgb * (T_P - t))
        span = jnp.full((1, 1), float(T_P), F32)
        cf = jnp.exp(lgf * span)
        cb = jnp.exp(lgb * span)
        before_b = [None] * n_chunk
        state = s0b_ref[...]
        for c in reversed(range(n_chunk)):
            before_b[c] = state
            if c > 0:
                state = cb * state + key_state(chunks[c], kdb)
        state = s0f_ref[...]
        for c, r in enumerate(chunks):
            qf = q_ref[r, :].astype(F32)
            o = (intra(r)
                 + jnp.dot((qf * qdf).astype(BF16), state.astype(BF16), preferred_element_type=F32)
                 + jnp.dot((qf * qdb).astype(BF16), before_b[c].astype(BF16), preferred_element_type=F32))
            o_ref[r, :] = finish(o, gcat(r))
            if c + 1 < n_chunk:
                state = cf * state + key_state(r, kdf)

    return build_mask, context_part, latent_part


ATT_TQ = 1024
ATT_HPS = 2
ATT_QSCALE = (DIFF_HD ** -0.5) * math.log2(math.e)


N_SIDE = 3


def _mixer_kernel(aux_ref, q_ref, k_ref, v_ref, ckt_ref, cv_ref, cos_ref, sin_ref, g_ref,
                  rq_ref, rk_ref, rv_ref, rg_ref, s0f_ref, s0b_ref, gn_ref, *rest):
    side_in = rest[:N_SIDE]
    o_ref, ro_ref, sf_ref, sb_ref = rest[N_SIDE:N_SIDE + 4]
    side_out = rest[N_SIDE + 4:2 * N_SIDE + 4]
    xs0_ref, q_scr, k_scr, v_scr, dm_scr = rest[2 * N_SIDE + 4:]
    i = pl.program_id(0)
    rh = pl.program_id(1)
    build_mask, ret_context, ret_latent = _ret_parts(
        aux_ref[AUX_LG, rh], aux_ref[AUX_LG + 1, rh], rq_ref, rk_ref, rv_ref, rg_ref, s0f_ref, s0b_ref, gn_ref,
        ro_ref, sf_ref, sb_ref, dm_scr.at[rh])

    @pl.when(i == 0)
    def _():
        build_mask()

    def side_jobs():
        for src, dst in zip(side_in, side_out):
            dst[...] = src[...].astype(BF16)
        xs0_ref[...] = jnp.zeros_like(xs0_ref)

    lam = aux_ref[AUX_LAM, 0]
    out_scale = aux_ref[AUX_LAM, 1]
    gain = g_ref[...] * out_scale
    lane = lax.broadcasted_iota(jnp.int32, (1, LANE), 1)
    first = lane < DIFF_HD
    nt = (((1,), (1,)), ((), ()))

    def halves(q):
        zero = jnp.zeros_like(q)
        return jnp.where(first, q, zero), jnp.where(first, zero, q)

    def weights(s):
        return jnp.exp2(s - jnp.max(s, axis=-1, keepdims=True)).astype(BF16)

    def finish(of0, of1):
        o = of0[:, :LANE] / of0[:, LANE:] - lam * (of1[:, :LANE] / of1[:, LANE:])
        return (_rms(o) * gain).astype(BF16)

    @pl.when(i < N_PBLK)
    def _():
        ret_context()
        n_seq = SEQ_BLK // T_P
        rows = [slice(s * T_P, (s + 1) * T_P) for s in range(n_seq)]
        ones = jnp.ones((T_P, LANE), BF16)
        for hh in range(ATT_HPS):
            cols = slice(hh * LANE, (hh + 1) * LANE)
            q0, q1 = halves((q_ref[hh].astype(F32) * ATT_QSCALE).astype(BF16))
            s0 = jnp.concatenate([lax.dot_general(q0[r], k_ref[hh, r, :], nt, preferred_element_type=F32)
                                  for r in rows], axis=0)
            s1 = jnp.concatenate([lax.dot_general(q1[r], k_ref[hh, r, :], nt, preferred_element_type=F32)
                                  for r in rows], axis=0)
            e0 = weights(s0)
            e1 = weights(s1)
            for r in rows:
                v1 = jnp.concatenate([v_ref[hh, r, :], ones], axis=1)
                o_ref[r, cols] = finish(jnp.dot(e0[r], v1, preferred_element_type=F32),
                                        jnp.dot(e1[r], v1, preferred_element_type=F32))

    @pl.when(i >= N_PBLK)
    def _():
        side_jobs()
        ret_latent()
        cos = cos_ref[...]
        sin = sin_ref[...]
        low = (lax.broadcasted_iota(jnp.int32, (T_S, LANE), 1) & 16) == 0

        def rope(x):
            xs = jnp.where(low, pltpu.roll(x, LANE - 16, 1), pltpu.roll(x, 16, 1))
            return x * cos + xs * sin

        for hh in range(ATT_HPS):
            cols = slice(hh * LANE, (hh + 1) * LANE)
            head = pl.program_id(1) * ATT_HPS + hh
            q_scr[hh] = (rope(q_ref[hh].astype(F32)) * ATT_QSCALE).astype(BF16)
            k_scr[hh] = rope(k_ref[hh].astype(F32)).astype(BF16)
            ckt = ckt_ref[cols, :].astype(BF16)
            v_scr[hh, 0:T_S, 0:LANE] = v_ref[hh]
            v_scr[hh, T_S:T_S + PAST, 0:LANE] = cv_ref[:, head, :].astype(BF16)
            v_scr[hh, :, LANE:2 * LANE] = jnp.ones((T_S + PAST, LANE), BF16)

            def scores(qh, hh=hh, ckt=ckt):
                return jnp.concatenate([lax.dot_general(qh, k_scr[hh], nt, preferred_element_type=F32),
                                        jnp.dot(qh, ckt, preferred_element_type=F32)], axis=1)

            for b in range(T_S // ATT_TQ):
                r = slice(b * ATT_TQ, (b + 1) * ATT_TQ)
                q0, q1 = halves(q_scr[hh, r, :])
                e0 = weights(scores(q0))
                e1 = weights(scores(q1))
                o_ref[r, cols] = finish(jnp.dot(e0, v_scr[hh], preferred_element_type=F32),
                                        jnp.dot(e1, v_scr[hh], preferred_element_type=F32))


def _mixers(aux, proj, cache_k, cache_v, cos_t, sin_t, subln_g, s0f, s0b, gnorm, side_weights):
    nb = N_TOK // SEQ_BLK
    spb = SEQ_BLK // T_P
    smp = lambda i: jnp.maximum(i - N_PBLK, 0)
    pmt = lambda i: jnp.minimum(i, N_PBLK - 1)
    pmh = lambda i, h: jnp.where(i < N_PBLK, h, RET_H - 1)
    state_in = pl.BlockSpec((None, None, RET_KD, RET_VD), lambda i, h: (smp(i), h, 0, 0))
    state_out = pl.BlockSpec((spb, None, RET_KD, RET_VD), lambda i, h: (pmt(i), pmh(i, h), 0, 0))

    n_hp = DIFF_H // ATT_HPS
    assert n_hp == RET_H
    n_steps = (nb - N_PBLK) * n_hp
    step = lambda i, h: jnp.maximum((i - N_PBLK) * n_hp + h, 0)
    exp_per_step = N_EXP // n_steps
    exp_slice = lambda i, h: (step(i, h), 0, 0)
    row_slice = lambda i, h: (step(i, h), 0)
    up_spec = pl.BlockSpec((exp_per_step, D, FF), exp_slice)
    down_spec = pl.BlockSpec((exp_per_step, FF, D), exp_slice)
    side_specs = [up_spec, up_spec, down_spec]
    side_shapes = [jax.ShapeDtypeStruct(a.shape, BF16) for a in side_weights]
    zrows = R_MAX // n_steps

    return pl.pallas_call(
        _mixer_kernel,
        out_shape=(jax.ShapeDtypeStruct((N_TOK, DIFF_H * 2 * DIFF_HD), BF16),
                   jax.ShapeDtypeStruct((N_TOK, RET_H * RET_VD), BF16),
                   jax.ShapeDtypeStruct((16, RET_H, RET_KD, RET_VD), F32),
                   jax.ShapeDtypeStruct((16, RET_H, RET_KD, RET_VD), F32),
                   *side_shapes,
                   jax.ShapeDtypeStruct((R_MAX, XW), BF16)),
        grid=(nb, n_hp),
        in_specs=[pl.BlockSpec(memory_space=pltpu.SMEM),
                  pl.BlockSpec((ATT_HPS, SEQ_BLK, LANE), lambda i, h: (C_DQ // ATT_HPS + h, i, 0)),
                  pl.BlockSpec((ATT_HPS, SEQ_BLK, LANE), lambda i, h: (C_DK // ATT_HPS + h, i, 0)),
                  pl.BlockSpec((ATT_HPS, SEQ_BLK, LANE), lambda i, h: (C_DV // ATT_HPS + h, i, 0)),
                  pl.BlockSpec((None, ATT_HPS * LANE, PAST), lambda i, h: (smp(i), h, 0)),
                  pl.BlockSpec((None, None, PAST, DIFF_H, LANE), lambda i, h: (smp(i), 0, 0, 0, 0)),
                  pl.BlockSpec((T_S, LANE), lambda i, h: (0, 0)),
                  pl.BlockSpec((T_S, LANE), lambda i, h: (0, 0)),
                  pl.BlockSpec((1, LANE), lambda i, h: (0, 0)),
                  pl.BlockSpec((None, SEQ_BLK, LANE), lambda i, h: (C_RQ + h, i, 0)),
                  pl.BlockSpec((None, SEQ_BLK, LANE), lambda i, h: (C_RK + h, i, 0)),
                  pl.BlockSpec((2, SEQ_BLK, LANE), lambda i, h: (C_RV // 2 + h, i, 0)),
                  pl.BlockSpec((2, SEQ_BLK, LANE), lambda i, h: (C_RG // 2 + h, i, 0)),
                  state_in, state_in,
                  pl.BlockSpec((1, RET_VD), lambda i, h: (0, h)),
                  *side_specs],
        out_specs=(pl.BlockSpec((SEQ_BLK, ATT_HPS * LANE), lambda i, h: (i, h)),
                   pl.BlockSpec((SEQ_BLK, RET_VD), lambda i, h: (i, h)),
                   state_out, state_out,
                   *side_specs,
                   pl.BlockSpec((zrows, XW), row_slice)),
        scratch_shapes=[pltpu.VMEM((ATT_HPS, T_S, LANE), BF16),
                        pltpu.VMEM((ATT_HPS, T_S, LANE), BF16),
                        pltpu.VMEM((ATT_HPS, T_S + PAST, 2 * LANE), BF16),
                        pltpu.VMEM((RET_H, T_P, T_P), F32)],
        compiler_params=_cparams(("arbitrary", "arbitrary")),
        name="mixers",
    )(aux, proj, proj, proj, cache_k, cache_v, cos_t, sin_t, subln_g,
      proj, proj, proj, proj, s0f, s0b, gnorm, *side_weights)


def _rope_tables():
    n_rows = T_S // GRID_W
    row = np.repeat(np.arange(n_rows), GRID_W).astype(np.float64)
    col = np.tile(np.arange(GRID_W), n_rows).astype(np.float64)
    n_freq = DIFF_HD // 4
    inv = ROPE_BASE ** (-np.arange(n_freq, dtype=np.float64) / n_freq)

    def axis_tables(pos):
        ang = pos[:, None] * inv[None, :]
        c = np.cos(ang)
        s = np.sin(ang)
        return np.concatenate([c, c], axis=-1), np.concatenate([-s, s], axis=-1)

    cr, sr = axis_tables(row)
    cc, sc = axis_tables(col)
    cos_h = np.concatenate([cr, cc], axis=-1)
    sin_h = np.concatenate([sr, sc], axis=-1)
    return (jnp.asarray(np.concatenate([cos_h, cos_h], axis=-1), F32),
            jnp.asarray(np.concatenate([sin_h, sin_h], axis=-1), F32))


OP_TM = 512
OP_NPT = N_PROMPT // OP_TM


def _outproj_kernel(ra_ref, da_ref, gr_ref, gd_ref, xp_ref, xs_ref, mod_ref, n2_ref,
                    wro32_ref, wdo32_ref, wo32_ref, rgw_ref, rew_ref, brt_ref,
                    x1_ref, h2_ref, cw8_ref, info_ref, infot_ref, off_ref, pctab_ref, steps_ref,
                    m_scr, wro_ref, wdo_ref, wo_ref, wrt_ref, pc_scr):
    i = pl.program_id(0)

    @pl.when(i == 0)
    def _():
        wro_ref[...] = wro32_ref[...].astype(BF16)
        wdo_ref[...] = wdo32_ref[...].astype(BF16)
        wo_ref[...] = wo32_ref[...].astype(BF16)
        gap = jnp.zeros((ROUTE_E0 - N_GROUPS, D), F32)
        pad = jnp.zeros((ROUTE_W - ROUTE_E0 - N_EXP, D), F32)
        wr = jnp.concatenate([rgw_ref[...], gap, rew_ref[...], pad], axis=0)
        wrt_ref[...] = wr.astype(BF16)

    ret_out = jnp.dot(ra_ref[...], wro_ref[...], preferred_element_type=F32)
    diff_out = jnp.dot(da_ref[...], wdo_ref[...], preferred_element_type=F32)
    for c in range(D // LANE):
        sl = slice(c * LANE, (c + 1) * LANE)
        m = (jax.nn.sigmoid(gr_ref[c].astype(F32)) * ret_out[:, sl]
             + jax.nn.sigmoid(gd_ref[c].astype(F32)) * diff_out[:, sl])
        m_scr[:, sl] = m.astype(BF16)
    mix = jnp.dot(m_scr[...], wo_ref[...], preferred_element_type=F32)
    mod = mod_ref[...]

    x1 = jnp.where(i < OP_NPT, xp_ref[...], xs_ref[...]) + mod[:, 2 * D:3 * D] * mix
    x1_ref[...] = x1
    h2 = (_rms(x1) * n2_ref[...] * (1.0 + mod[:, 4 * D:5 * D]) + mod[:, 3 * D:4 * D]).astype(BF16)
    h2_ref[...] = h2
    logits_t = lax.dot_general(wrt_ref[...], h2, (((1,), (1,)), ((), ())),
                               preferred_element_type=F32) + brt_ref[...]
    cw8_ref[...], info_ref[...], pc, infot_ref[...] = _route_cols(logits_t)
    pc_scr[pl.ds(i, 1), :] = pc

    @pl.when(i == RT_NT - 1)
    def _():
        off_ref[...], pctab_ref[...], steps_ref[...] = _dispatch_plan(pc_scr[...])


def _outproj(ret_act, diff_act, proj, xp, xs, mod3, n2, wro, wdo, wo, rgw, rew, brt):
    assert OP_TM == RT_TM
    npt = OP_NPT
    full = lambda i: (0, 0)
    once = pl.Buffered(1)
    return pl.pallas_call(
        _outproj_kernel,
        out_shape=(jax.ShapeDtypeStruct((N_TOK, D), F32),
                   jax.ShapeDtypeStruct((N_TOK, D), BF16),
                   jax.ShapeDtypeStruct((N_TOK, ROUTE_W), F32),
                   jax.ShapeDtypeStruct((N_TOK, ROUTE_W), F32),
                   jax.ShapeDtypeStruct((RT_NT, EPG, RT_TM), F32),
                   jax.ShapeDtypeStruct((RT_NT, ROUTE_W), jnp.int32),
                   jax.ShapeDtypeStruct((RT_NT, ROUTE_W), jnp.int32),
                   jax.ShapeDtypeStruct((PLAN_ROWS, ROUTE_W), jnp.int32)),
        grid=(N_TOK // OP_TM,),
        in_specs=[pl.BlockSpec((OP_TM, D), lambda i: (i, 0)),
                  pl.BlockSpec((OP_TM, D), lambda i: (i, 0)),
                  pl.BlockSpec((8, OP_TM, LANE), lambda i: (C_GR // 8, i, 0)),
                  pl.BlockSpec((8, OP_TM, LANE), lambda i: (C_GD // 8, i, 0)),
                  pl.BlockSpec((OP_TM, D), lambda i: (jnp.minimum(i, npt - 1), 0)),
                  pl.BlockSpec((OP_TM, D), lambda i: (jnp.maximum(i - npt, 0), 0)),
                  pl.BlockSpec((None, 1, 6 * D), lambda i: (_mod_row(OP_TM)(i), 0, 0)),
                  pl.BlockSpec((1, D), full),
                  pl.BlockSpec((D, D), full, pipeline_mode=once),
                  pl.BlockSpec((D, D), full, pipeline_mode=once),
                  pl.BlockSpec((D, D), full, pipeline_mode=once),
                  pl.BlockSpec((N_GROUPS, D), full, pipeline_mode=once),
                  pl.BlockSpec((N_EXP, D), full, pipeline_mode=once),
                  pl.BlockSpec((ROUTE_W, 1), full)],
        out_specs=(pl.BlockSpec((OP_TM, D), lambda i: (i, 0)),
                   pl.BlockSpec((OP_TM, D), lambda i: (i, 0)),
                   pl.BlockSpec((OP_TM, ROUTE_W), lambda i: (i, 0)),
                   pl.BlockSpec((OP_TM, ROUTE_W), lambda i: (i, 0)),
                   pl.BlockSpec((None, EPG, RT_TM), lambda i: (i, 0, 0)),
                   pl.BlockSpec((RT_NT, ROUTE_W), full),
                   pl.BlockSpec((RT_NT, ROUTE_W), full),
                   pl.BlockSpec((PLAN_ROWS, ROUTE_W), full)),
        scratch_shapes=[pltpu.VMEM((OP_TM, D), BF16),
                        pltpu.VMEM((D, D), BF16), pltpu.VMEM((D, D), BF16), pltpu.VMEM((D, D), BF16),
                        pltpu.VMEM((ROUTE_W, D), BF16),
                        pltpu.VMEM((RT_NT, ROUTE_W), F32)],
        compiler_params=_cparams(("arbitrary",)),
        name="outproj",
    )(ret_act, diff_act, proj, proj, xp, xs, mod3, n2, wro, wdo, wo, rgw, rew, brt)


RT_TM = 512
RT_NT = N_TOK // RT_TM
PIECE = 16
R_LOC = RT_TM + N_GROUPS * PIECE
R_STAGE = 640
EX_TM = 512
R_MAX = 11264
EX_NT = R_MAX // EX_TM


ROUTE_E0 = EPG


def _route_cols(lgt):
    row = lax.broadcasted_iota(jnp.int32, (EPG, RT_TM), 0)
    neg = jnp.float32(-jnp.inf)

    def first_row(cond):
        return jnp.min(jnp.where(cond, row, EPG), axis=0, keepdims=True)

    head = lgt[0:EPG]
    is_g = row < N_GROUPS
    gl = jnp.where(is_g, head, neg)
    gmax = jnp.max(gl, axis=0, keepdims=True)
    gsum = jnp.sum(jnp.where(is_g, jnp.exp(head - gmax), 0.0), axis=0, keepdims=True)
    p_top = 1.0 / gsum
    g_idx = first_row(gl == gmax)
    el = lgt[ROUTE_E0 + (N_GROUPS - 1) * EPG:ROUTE_E0 + N_GROUPS * EPG]
    for g in reversed(range(N_GROUPS - 1)):
        el = jnp.where(g_idx == g, lgt[ROUTE_E0 + g * EPG:ROUTE_E0 + (g + 1) * EPG], el)
    ee = jnp.exp(el - jnp.max(el, axis=0, keepdims=True))
    ep = ee / jnp.sum(ee, axis=0, keepdims=True)
    e1 = jnp.max(ep, axis=0, keepdims=True)
    i1 = first_row(ep == e1)
    ep2 = jnp.where(row == i1, -1.0, ep)
    e2 = jnp.max(ep2, axis=0, keepdims=True)
    i2 = first_row(ep2 == e2)
    den = e1 + e2
    cw8_t = (jnp.where(row == i1, p_top * e1 / den, 0.0)
             + jnp.where(row == i2, p_top * e2 / den, 0.0))

    onehot = (row == g_idx).astype(F32)
    ii = lax.broadcasted_iota(jnp.int32, (RT_TM, RT_TM), 0)
    jj = lax.broadcasted_iota(jnp.int32, (RT_TM, RT_TM), 1)
    earlier = (ii < jj).astype(BF16)
    prefix = jnp.dot(onehot.astype(BF16), earlier, preferred_element_type=F32)
    cnt = jnp.sum(onehot, axis=1, keepdims=True)
    pc_col = jnp.floor((cnt + (PIECE - 1.0)) * (1.0 / PIECE)) * PIECE
    row1 = lax.broadcasted_iota(jnp.int32, (EPG, 1), 0)
    lane1 = lax.broadcasted_iota(jnp.int32, (1, ROUTE_W), 1)
    lo = jnp.zeros((EPG, 1), F32)
    run = jnp.zeros((1, 1), F32)
    pc = jnp.zeros((1, ROUTE_W), F32)
    for g in range(N_GROUPS):
        pc_g = jnp.sum(jnp.where(row1 == g, pc_col, 0.0), axis=0, keepdims=True)
        lo = jnp.where(row1 == g, run, lo)
        pc = jnp.where(lane1 == g, pc_g, pc)
        run = run + pc_g
    dest = jnp.sum(onehot * (prefix + lo), axis=0, keepdims=True)
    info_t = jnp.where(row == 0, g_idx.astype(F32), jnp.where(row == 1, dest, 0.0))

    slab = jnp.concatenate([cw8_t, info_t, jnp.zeros((LANE - 2 * EPG, RT_TM), F32)], axis=0)
    cols = slab.T
    lane = lax.broadcasted_iota(jnp.int32, cols.shape, 1)
    cw8 = jnp.where(lane < EPG, cols, 0.0)
    info = jnp.where(lane < 2, pltpu.roll(cols, LANE - EPG, 1), 0.0)
    return cw8, info, pc, info_t


PLAN_ROWS = 32


def _dispatch_plan(pc_all):
    lane = lax.broadcasted_iota(jnp.int32, (1, ROUTE_W), 1)
    is_g = lane < N_GROUPS
    seg_len = jnp.sum(pc_all, axis=0, keepdims=True)
    seg_pad = jnp.floor((seg_len + (EX_TM - 1.0)) * (1.0 / EX_TM)) * EX_TM
    run = seg_pad + pltpu.roll(seg_pad, 1, 1)
    seg_end_pad = jnp.where(is_g, run + pltpu.roll(run, 2, 1), 0.0)
    seg_start = seg_end_pad - seg_pad
    ti = lax.broadcasted_iota(jnp.int32, (RT_NT, RT_NT), 0)
    tj = lax.broadcasted_iota(jnp.int32, (RT_NT, RT_NT), 1)
    earlier = (tj < ti).astype(BF16)
    within = jnp.dot(earlier, pc_all.astype(BF16), preferred_element_type=F32)
    chunk_off = jnp.where(is_g, seg_start + within, 0.0)

    start = (lax.broadcasted_iota(jnp.int32, (PLAN_ROWS, 1), 0) * EX_TM).astype(F32)
    passed = jnp.where(jnp.logical_and(is_g, start >= seg_end_pad), 1.0, 0.0)
    group = jnp.minimum(jnp.sum(passed, axis=1, keepdims=True), N_GROUPS - 1.0)
    used_end = jnp.sum(jnp.where(lane == group.astype(jnp.int32), seg_start + seg_len, 0.0),
                       axis=1, keepdims=True)
    used = jnp.clip(used_end - start, 0.0, float(EX_TM))
    steps = jnp.where(lane == 0, group, jnp.where(lane == 1, used, 0.0))
    return chunk_off.astype(jnp.int32), pc_all.astype(jnp.int32), steps.astype(jnp.int32)


def _piece_copies(off_ref, pc_ref, tile, make):
    lo = 0
    for g in range(N_GROUPS):
        n = pc_ref[tile, g] // PIECE
        base = off_ref[tile, g]

        def body(j, carry, lo=lo, base=base):
            make(pl.multiple_of(lo + j * PIECE, PIECE), pl.multiple_of(base + j * PIECE, PIECE))
            return carry

        lax.fori_loop(0, n, body, 0)
        lo = lo + pc_ref[tile, g]


def _piece_count(pc_ref, tile):
    n = 0
    for g in range(N_GROUPS):
        n = n + pc_ref[tile, g] // PIECE
    return n


XW = D + ROUTE_W


def _dispatch_kernel(off_ref, pc_ref, h_ref, infot_ref, cw8_ref, xs_in, xs_out, x_scr, sem):
    del xs_in
    i = pl.program_id(0)
    slot = i % 2
    dest = infot_ref[1:2, :]
    row = lax.broadcasted_iota(jnp.int32, (R_LOC, RT_TM), 0).astype(F32)
    sel = (row == dest).astype(BF16)
    cw = cw8_ref[...]
    hi = cw.astype(BF16).astype(F32)
    mid = (cw - hi).astype(BF16).astype(F32)
    low = (cw - hi - mid).astype(BF16).astype(F32)
    pieces = (hi + pltpu.roll(mid, EPG, 1) + pltpu.roll(low, 2 * EPG, 1)).astype(BF16)
    rows = jnp.concatenate([h_ref[...], pieces], axis=1)
    x_scr[slot] = jnp.dot(sel, rows, preferred_element_type=F32).astype(BF16)

    def x_copy(s, src, dst):
        return pltpu.make_async_copy(x_scr.at[s, pl.ds(src, PIECE)], xs_out.at[pl.ds(dst, PIECE)],
                                     sem.at[s])

    _piece_copies(off_ref, pc_ref, i, lambda src, dst: x_copy(slot, src, dst).start())

    def wait_tile(tile, s):
        def wait(j, carry):
            x_copy(s, 0, 0).wait()
            return carry

        lax.fori_loop(0, _piece_count(pc_ref, tile), wait, 0)

    @pl.when(i > 0)
    def _():
        wait_tile(i - 1, 1 - slot)

    @pl.when(i == RT_NT - 1)
    def _():
        wait_tile(i, slot)


def _dispatch(chunk_off, pc, h2, info_t, cw8, xs0):
    grid_spec = pltpu.PrefetchScalarGridSpec(
        num_scalar_prefetch=2,
        grid=(RT_NT,),
        in_specs=[pl.BlockSpec((RT_TM, D), lambda i, o, p: (i, 0)),
                  pl.BlockSpec((None, EPG, RT_TM), lambda i, o, p: (i, 0, 0)),
                  pl.BlockSpec((RT_TM, ROUTE_W), lambda i, o, p: (i, 0)),
                  pl.BlockSpec(memory_space=pl.ANY)],
        out_specs=pl.BlockSpec(memory_space=pl.ANY),
        scratch_shapes=[pltpu.VMEM((2, R_LOC, XW), BF16),
                        pltpu.SemaphoreType.DMA((2,))])
    return pl.pallas_call(
        _dispatch_kernel,
        out_shape=jax.ShapeDtypeStruct((R_MAX, XW), BF16),
        grid_spec=grid_spec,
        input_output_aliases={5: 0},
        compiler_params=_cparams(("arbitrary",)),
        name="dispatch",
    )(chunk_off, pc, h2, info_t, cw8, xs0)


def _expert_kernel(steps_ref, x_ref, wg_ref, wu_ref, wd_ref, y_ref):
    k = pl.program_id(0)
    used = steps_ref[k, 1]
    half = EX_TM // 2

    def run(rows):
        x = x_ref[rows, 0:D]
        cw = x_ref[rows, D:XW].astype(F32)
        lane = lax.broadcasted_iota(jnp.int32, cw.shape, 1)
        acc = jnp.zeros(x.shape, F32)
        for j in range(EPG):
            mine = jnp.logical_and((lane & (EPG - 1)) == j, lane < 3 * EPG)
            w = jnp.sum(jnp.where(mine, cw, 0.0), axis=-1, keepdims=True)
            a = (jax.nn.silu(jnp.dot(x, wg_ref[j], preferred_element_type=F32))
                 * jnp.dot(x, wu_ref[j], preferred_element_type=F32))
            acc = acc + jnp.dot((a * w).astype(BF16), wd_ref[j], preferred_element_type=F32)
        y_ref[rows, :] = acc.astype(BF16)

    @pl.when(used > half)
    def _():
        run(slice(0, EX_TM))

    @pl.when(jnp.logical_and(used > 0, used <= half))
    def _():
        run(slice(0, half))
        y_ref[half:EX_TM, :] = jnp.zeros((EX_TM - half, D), BF16)

    @pl.when(used == 0)
    def _():
        y_ref[...] = jnp.zeros_like(y_ref)


def _experts(steps, xs, wg, wu, wd):
    grid_spec = pltpu.PrefetchScalarGridSpec(
        num_scalar_prefetch=1,
        grid=(EX_NT,),
        in_specs=[pl.BlockSpec((EX_TM, XW), lambda k, st: (k, 0)),
                  pl.BlockSpec((EPG, D, FF), lambda k, st: (st[k, 0], 0, 0)),
                  pl.BlockSpec((EPG, D, FF), lambda k, st: (st[k, 0], 0, 0)),
                  pl.BlockSpec((EPG, FF, D), lambda k, st: (st[k, 0], 0, 0))],
        out_specs=pl.BlockSpec((EX_TM, D), lambda k, st: (k, 0)))
    return pl.pallas_call(
        _expert_kernel,
        out_shape=jax.ShapeDtypeStruct((R_MAX, D), BF16),
        grid_spec=grid_spec,
        compiler_params=_cparams(("arbitrary",)),
        name="experts",
    )(steps, xs, wg, wu, wd)


CB_NPT = N_PROMPT // RT_TM


def _combine_kernel(off_ref, pc_ref, info_ref, x1_ref, mod_ref, g_ref, ys_hbm,
                    yp_ref, ysm_ref, stage, sem):
    i = pl.program_id(0)
    slot = i % 2

    def copy(s, dst, src):
        return pltpu.make_async_copy(ys_hbm.at[pl.ds(src, PIECE)], stage.at[s, pl.ds(dst, PIECE)],
                                     sem.at[s])

    def fetch(tile, s):
        _piece_copies(off_ref, pc_ref, tile, lambda loc, glob: copy(s, loc, glob).start())

    @pl.when(i == 0)
    def _():
        stage[...] = jnp.zeros_like(stage)
        fetch(0, 0)

    @pl.when(i + 1 < RT_NT)
    def _():
        fetch(i + 1, 1 - slot)

    def wait(j, carry):
        copy(slot, 0, 0).wait()
        return carry

    lax.fori_loop(0, _piece_count(pc_ref, i), wait, 0)

    dest = info_ref[...][:, 1:2]
    col = lax.broadcasted_iota(jnp.int32, (RT_TM, R_STAGE), 1).astype(F32)
    moe = jnp.dot((col == dest).astype(BF16), stage[slot], preferred_element_type=F32)
    mod = mod_ref[...]
    out = _rms(x1_ref[...] + mod[:, 5 * D:6 * D] * moe) * g_ref[...]

    @pl.when(i < CB_NPT)
    def _():
        yp_ref[...] = out

    @pl.when(i >= CB_NPT)
    def _():
        ysm_ref[...] = out


def _combine(chunk_off, pc, info, x1, mod3, fg, ys):
    npt = CB_NPT
    grid_spec = pltpu.PrefetchScalarGridSpec(
        num_scalar_prefetch=2,
        grid=(RT_NT,),
        in_specs=[pl.BlockSpec((RT_TM, ROUTE_W), lambda i, o, p: (i, 0)),
                  pl.BlockSpec((RT_TM, D), lambda i, o, p: (i, 0)),
                  pl.BlockSpec((None, 1, 6 * D), lambda i, o, p: (_mod_row(RT_TM)(i), 0, 0)),
                  pl.BlockSpec((1, D), lambda i, o, p: (0, 0)),
                  pl.BlockSpec(memory_space=pl.ANY)],
        out_specs=(pl.BlockSpec((RT_TM, D), lambda i, o, p: (jnp.minimum(i, npt - 1), 0)),
                   pl.BlockSpec((RT_TM, D), lambda i, o, p: (jnp.maximum(i - npt, 0), 0))),
        scratch_shapes=[pltpu.VMEM((2, R_STAGE, D), BF16),
                        pltpu.SemaphoreType.DMA((2,))])
    return pl.pallas_call(
        _combine_kernel,
        out_shape=(jax.ShapeDtypeStruct((N_PROMPT, D), F32),
                   jax.ShapeDtypeStruct((N_SAMPLE, D), F32)),
        grid_spec=grid_spec,
        compiler_params=_cparams(("arbitrary",)),
        name="combine",
    )(chunk_off, pc, info, x1, mod3, fg, ys)


def kernel(x_prompt, x_sample, cache_diff_k, cache_diff_v, state_ret_fwd, state_ret_bwd, c, c_ctx,
           w_ada, b_ada, norm1_g, norm2_g, w_in, ret_decay_fwd, ret_decay_bwd, ret_norm_g,
           diff_lambda_q1, diff_lambda_k1, diff_lambda_q2, diff_lambda_k2, diff_subln_g,
           w_ret_o, w_diff_o, w_o, router_group_w, router_group_b, router_expert_w, router_expert_b,
           moe_w_gate, moe_w_up, moe_w_down, final_norm_g):
    l = 0
    lam_init = 0.8 - 0.6 * math.exp(-0.3 * l)
    row = lambda a: a[l].astype(F32)[None, :]

    xp = x_prompt.reshape(N_PROMPT, D)
    xs = x_sample.reshape(N_SAMPLE, D)
    mod3, aux, brt = _modulation(
        lam_init, c_ctx[None, :], c, w_ada[l], b_ada[l][None, :],
        [row(diff_lambda_q1), row(diff_lambda_k1), row(diff_lambda_q2), row(diff_lambda_k2),
         row(ret_decay_fwd), row(ret_decay_bwd), row(router_group_b), row(router_expert_b)])

    proj, kt32, v32 = _inproj(xp, xs, mod3, norm1_g[l][None, :], w_in[l])

    cos_t, sin_t = _rope_tables()
    cache_kt = jnp.transpose(cache_diff_k[:, l], (0, 2, 3, 4, 1)).reshape(4, D, PAST)
    (diff_act, ret_act, s_f, s_b, wg_bf, wu_bf, wd_bf, xs0) = _mixers(
        aux, proj, cache_kt, cache_diff_v, cos_t, sin_t, diff_subln_g[l][None, :],
        state_ret_fwd[:, l], state_ret_bwd[:, l], ret_norm_g[l][None, :],
        (moe_w_gate[l], moe_w_up[l], moe_w_down[l]))

    x1, h2, cw8, info, info_t, chunk_off, pc, steps = _outproj(
        ret_act, diff_act, proj, xp, xs, mod3, norm2_g[l][None, :],
        w_ret_o[l], w_diff_o[l], w_o[l], router_group_w[l].T, router_expert_w[l].T, brt)
    xs_sorted = _dispatch(chunk_off, pc, h2, info_t, cw8, xs0)
    y_sorted = _experts(steps, xs_sorted, wg_bf, wu_bf, wd_bf)
    yp, ys = _combine(chunk_off, pc, info, x1, mod3, final_norm_g[None, :], y_sorted)

    return (yp.reshape(16, T_P, D), ys.reshape(4, T_S, D),
            jnp.transpose(kt32.reshape(16, DIFF_H, 2, DIFF_HD, T_P), (0, 4, 1, 2, 3))[:, None],
            v32.reshape(16, 1, T_P, DIFF_H, 2 * DIFF_HD),
            s_f.reshape(16, 1, RET_H, RET_KD, RET_VD), s_b.reshape(16, 1, RET_H, RET_KD, RET_VD))
```

```python
import functools
import math

import jax
import jax.numpy as jnp
import numpy as np
from jax import lax
from jax.experimental import pallas as pl
from jax.experimental.pallas import tpu as pltpu

F32 = jnp.float32
BF16 = jnp.bfloat16

D = 1024
N_PROMPT = 16 * 256
N_SAMPLE = 4 * 1024
N_TOK = N_PROMPT + N_SAMPLE
T_P = 256
T_S = 1024
PAST = 512
GRID_W = 64
RET_H = 4
RET_KD = 128
RET_VD = 256
DIFF_H = 8
DIFF_HD = 64
ROPE_BASE = 10000.0
N_GROUPS = 4
EPG = 8
N_EXP = 32
FF = 256
EPS = 1e-6
IN_W = 8192
LANE = 128
N_CHUNK = IN_W // LANE
C_RQ, C_RK, C_RV, C_RG, C_DQ, C_DK, C_DV, C_GR, C_GD = 0, 4, 8, 16, 24, 32, 40, 48, 56
ROUTE_W = 128
SEQ_BLK = 1024
N_PBLK = N_PROMPT // SEQ_BLK
V7X_VMEM_BYTES = 64 * 1024 * 1024
VMEM_LIMIT = V7X_VMEM_BYTES - 8 * 1024 * 1024


def _cparams(sem):
    return pltpu.CompilerParams(dimension_semantics=sem, vmem_limit_bytes=VMEM_LIMIT)


def _mod_row(tile_rows):
    def f(i):
        start = i * tile_rows
        return jnp.where(start < N_PROMPT, 0, 1 + (start - N_PROMPT) // T_S)
    return f


def _rms(x):
    return x * lax.rsqrt(jnp.mean(x * x, axis=-1, keepdims=True) + EPS)


def _ordered_after(dst, src):
    m = None
    for a in range(src.shape[0] // 8):
        for b in range(src.shape[1] // LANE):
            part = src[a * 8:(a + 1) * 8, b * LANE:(b + 1) * LANE]
            m = part if m is None else jnp.maximum(m, part)
    never = jnp.logical_and(m != m, m == m)
    return jnp.where(jnp.tile(never, (dst.shape[0] // 8, 1)), 0.0, dst)


MOD_ROWS = 8


AUX_LAM = 0
AUX_LG = 1


def _mod_kernel(lam_init, ctx_ref, c_ref, w_ref, b_ref, lq1_ref, lk1_ref, lq2_ref, lk2_ref, df_ref, db_ref,
                gb_ref, eb_ref, o_ref, aux_ref, brt_ref):
    cond = jnp.concatenate([ctx_ref[...], c_ref[...],
                            jnp.zeros((MOD_ROWS - 1 - c_ref.shape[0], D), F32)], axis=0)
    s = jax.nn.silu(cond)
    out = jnp.dot(s.astype(BF16), w_ref[...].astype(BF16), preferred_element_type=F32) + b_ref[...]
    for r in range(MOD_ROWS):
        o_ref[r] = out[r:r + 1, :]

    def lane_row(parts):
        used = sum(p.shape[1] for p in parts)
        return jnp.concatenate(parts + [jnp.zeros((1, LANE - used), F32)], axis=1)

    lam = (jnp.exp(jnp.sum(lq1_ref[...] * lk1_ref[...], axis=1, keepdims=True))
           - jnp.exp(jnp.sum(lq2_ref[...] * lk2_ref[...], axis=1, keepdims=True)) + lam_init)
    rows = [lane_row([lam, jnp.full((1, 1), 1.0 - lam_init, F32)]),
            lane_row([jax.nn.log_sigmoid(df_ref[...])]),
            lane_row([jax.nn.log_sigmoid(db_ref[...])])]
    aux_ref[...] = jnp.concatenate(rows + [jnp.zeros((8 - len(rows), LANE), F32)], axis=0)
    bias = lane_row([gb_ref[...], jnp.zeros((1, ROUTE_E0 - N_GROUPS), F32), eb_ref[...]])
    brt_ref[...] = jnp.tile(bias, (8, 1)).T[:, 0:1]


def _modulation(lam_init, c_ctx, c, w_ada, b_ada, small):
    tn = 1536
    whole = lambda a: pl.BlockSpec(a.shape, lambda j: (0, 0))
    return pl.pallas_call(
        functools.partial(_mod_kernel, lam_init),
        out_shape=(jax.ShapeDtypeStruct((MOD_ROWS, 1, 6 * D), F32),
                   jax.ShapeDtypeStruct((8, LANE), F32),
                   jax.ShapeDtypeStruct((ROUTE_W, 1), F32)),
        grid=(6 * D // tn,),
        in_specs=[pl.BlockSpec((1, D), lambda j: (0, 0)),
                  pl.BlockSpec(c.shape, lambda j: (0, 0)),
                  pl.BlockSpec((D, tn), lambda j: (0, j)),
                  pl.BlockSpec((1, tn), lambda j: (0, j))] + [whole(a) for a in small],
        out_specs=(pl.BlockSpec((MOD_ROWS, 1, tn), lambda j: (0, 0, j)),
                   pl.BlockSpec((8, LANE), lambda j: (0, 0)),
                   pl.BlockSpec((ROUTE_W, 1), lambda j: (0, 0))),
        compiler_params=_cparams(("arbitrary",)),
        name="mod",
    )(c_ctx, c, w_ada, b_ada, *small)


IP_TM = 512
IP_TN = 2048
IP_NPT = N_PROMPT // IP_TM


IP_KV_TILE = C_DK * LANE // IP_TN
IP_SPT = IP_TM // T_P


def _inproj_kernel(xp_ref, xs_ref, mod_ref, n1_ref, w_ref, proj_ref, kt_ref, v32_ref, w_scr):
    j = pl.program_id(0)
    i = pl.program_id(1)

    @pl.when(i == 0)
    def _():
        w_scr[...] = w_ref[...].astype(BF16)

    x = jnp.where(i < IP_NPT, xp_ref[...], xs_ref[...])
    mod = mod_ref[...]
    h = (_rms(x) * n1_ref[...] * (1.0 + mod[:, D:2 * D]) + mod[:, 0:D]).astype(BF16)
    acc = jnp.dot(h, w_scr[...], preferred_element_type=F32)
    for c in range(IP_TN // LANE):
        proj_ref[c] = acc[:, c * LANE:(c + 1) * LANE].astype(BF16)

    @pl.when(jnp.logical_and(j == IP_KV_TILE, i < IP_NPT))
    def _():
        for s in range(IP_SPT):
            kt_ref[s] = acc[s * T_P:(s + 1) * T_P, :D].T
        v32_ref[...] = acc[:, D:]


def _inproj(xp, xs, mod3, n1, w_in):
    npt = IP_NPT
    cpt = IP_TN // LANE

    def kv_tile(j, i):
        return jnp.where(j < IP_KV_TILE, 0, jnp.where(j == IP_KV_TILE, jnp.minimum(i, npt - 1), npt - 1))

    return pl.pallas_call(
        _inproj_kernel,
        out_shape=(jax.ShapeDtypeStruct((N_CHUNK, N_TOK, LANE), BF16),
                   jax.ShapeDtypeStruct((N_PROMPT // T_P, D, T_P), F32),
                   jax.ShapeDtypeStruct((N_PROMPT, D), F32)),
        grid=(IN_W // IP_TN, N_TOK // IP_TM),
        in_specs=[pl.BlockSpec((IP_TM, D), lambda j, i: (jnp.minimum(i, npt - 1), 0)),
                  pl.BlockSpec((IP_TM, D), lambda j, i: (jnp.maximum(i - npt, 0), 0)),
                  pl.BlockSpec((None, 1, 6 * D), lambda j, i: (_mod_row(IP_TM)(i), 0, 0)),
                  pl.BlockSpec((1, D), lambda j, i: (0, 0)),
                  pl.BlockSpec((D, IP_TN), lambda j, i: (0, j))],
        out_specs=(pl.BlockSpec((cpt, IP_TM, LANE), lambda j, i: (j, i, 0)),
                   pl.BlockSpec((IP_SPT, D, T_P), lambda j, i: (kv_tile(j, i), 0, 0)),
                   pl.BlockSpec((IP_TM, D), lambda j, i: (kv_tile(j, i), 0))),
        scratch_shapes=[pltpu.VMEM((D, IP_TN), BF16)],
        compiler_params=_cparams(("arbitrary", "arbitrary")),
        name="inproj",
    )(xp, xs, mod3, n1, w_in)


def _decay_mask(t, lgf, lgb):
    ii = lax.broadcasted_iota(jnp.int32, (t, t), 0)
    jj = lax.broadcasted_iota(jnp.int32, (t, t), 1)
    rel = (ii - jj).astype(F32)
    e = jnp.exp(jnp.where(rel >= 0.0, lgf, -lgb) * rel)
    return jnp.where(rel == 0.0, 2.0, e) * (RET_KD ** -0.5)


def _ret_parts(lgf, lgb, q_ref, k_ref, v_ref, rg_ref, s0f_ref, s0b_ref, g_ref,
               o_ref, sf_ref, sb_ref, dm_ref):
    gain = g_ref[...]
    nt = (((1,), (1,)), ((), ()))
    tn = (((0,), (0,)), ((), ()))

    def finish(o, rg):
        d = o - jnp.mean(o, axis=-1, keepdims=True)
        y = d * lax.rsqrt(jnp.mean(d * d, axis=-1, keepdims=True) + EPS) * gain
        return (jax.nn.silu(rg.astype(F32)) * y).astype(BF16)

    def vcat(r):
        return jnp.concatenate([v_ref[0, r, :], v_ref[1, r, :]], axis=1)

    def gcat(r):
        return jnp.concatenate([rg_ref[0, r, :], rg_ref[1, r, :]], axis=1)

    def build_mask():
        dm_ref[...] = _decay_mask(T_P, lgf, lgb)

    n_chunk = SEQ_BLK // T_P
    chunks = [slice(c * T_P, (c + 1) * T_P) for c in range(n_chunk)]
    t = lax.broadcasted_iota(jnp.int32, (T_P, 1), 0).astype(F32)
    kdf = jnp.exp(lgf * (T_P - 1.0 - t)) * (RET_KD ** -0.5)
    kdb = jnp.exp(lgb * t) * (RET_KD ** -0.5)

    def intra(r):
        sc = lax.dot_general(q_ref[r, :], k_ref[r, :], nt, preferred_element_type=F32)
        return jnp.dot((sc * dm_ref[...]).astype(BF16), vcat(r), preferred_element_type=F32)

    def key_state(r, kd):
        kw = (k_ref[r, :].astype(F32) * kd).astype(BF16)
        return lax.dot_general(kw, vcat(r), tn, preferred_element_type=F32)

    def context_part():
        for s, r in enumerate(chunks):
            o_ref[r, :] = finish(intra(r), gcat(r))
            sf_ref[s] = key_state(r, kdf)
            sb_ref[s] = key_state(r, kdb)

    def latent_part():
        qdf = jnp.exp(lgf * (t + 1.0))
        qdb = jnp.exp(lgb * (T_P - t))
        span = jnp.full((1, 1), float(T_P), F32)
        cf = jnp.exp(lgf * span)
        cb = jnp.exp(lgb * span)
        before_b = [None] * n_chunk
        state = s0b_ref[...]
        for c in reversed(range(n_chunk)):
            before_b[c] = state
            if c > 0:
                state = cb * state + key_state(chunks[c], kdb)
        state = s0f_ref[...]
        for c, r in enumerate(chunks):
            qf = q_ref[r, :].astype(F32)
            o = (intra(r)
                 + jnp.dot((qf * qdf).astype(BF16), state.astype(BF16), preferred_element_type=F32)
                 + jnp.dot((qf * qdb).astype(BF16), before_b[c].astype(BF16), preferred_element_type=F32))
            o_ref[r, :] = finish(o, gcat(r))
            if c + 1 < n_chunk:
                state = cf * state + key_state(r, kdf)

    return build_mask, context_part, latent_part


ATT_TQ = 1024
ATT_HPS = 2
ATT_QSCALE = (DIFF_HD ** -0.5) * math.log2(math.e)


N_SIDE = 3


def _mixer_kernel(aux_ref, q_ref, k_ref, v_ref, ckt_ref, cv_ref, cos_ref, sin_ref, g_ref,
                  rq_ref, rk_ref, rv_ref, rg_ref, s0f_ref, s0b_ref, gn_ref, *rest):
    side_in = rest[:N_SIDE]
    o_ref, ro_ref, sf_ref, sb_ref = rest[N_SIDE:N_SIDE + 4]
    side_out = rest[N_SIDE + 4:2 * N_SIDE + 4]
    xs0_ref, q_scr, k_scr, v_scr, dm_scr = rest[2 * N_SIDE + 4:]
    i = pl.program_id(0)
    rh = pl.program_id(1)
    build_mask, ret_context, ret_latent = _ret_parts(
        aux_ref[AUX_LG, rh], aux_ref[AUX_LG + 1, rh], rq_ref, rk_ref, rv_ref, rg_ref, s0f_ref, s0b_ref, gn_ref,
        ro_ref, sf_ref, sb_ref, dm_scr.at[rh])

    @pl.when(i == 0)
    def _():
        build_mask()

    def side_jobs():
        for src, dst in zip(side_in, side_out):
            dst[...] = src[...].astype(BF16)
        xs0_ref[...] = jnp.zeros_like(xs0_ref)

    lam = aux_ref[AUX_LAM, 0]
    out_scale = aux_ref[AUX_LAM, 1]
    gain = g_ref[...] * out_scale
    lane = lax.broadcasted_iota(jnp.int32, (1, LANE), 1)
    first = lane < DIFF_HD
    nt = (((1,), (1,)), ((), ()))

    def halves(q):
        zero = jnp.zeros_like(q)
        return jnp.where(first, q, zero), jnp.where(first, zero, q)

    def weights(s):
        return jnp.exp2(s - jnp.max(s, axis=-1, keepdims=True)).astype(BF16)

    def finish(of0, of1):
        o = of0[:, :LANE] / of0[:, LANE:] - lam * (of1[:, :LANE] / of1[:, LANE:])
        return (_rms(o) * gain).astype(BF16)

    @pl.when(i < N_PBLK)
    def _():
        ret_context()
        n_seq = SEQ_BLK // T_P
        rows = [slice(s * T_P, (s + 1) * T_P) for s in range(n_seq)]
        ones = jnp.ones((T_P, LANE), BF16)
        for hh in range(ATT_HPS):
            cols = slice(hh * LANE, (hh + 1) * LANE)
            q0, q1 = halves((q_ref[hh].astype(F32) * ATT_QSCALE).astype(BF16))
            s0 = jnp.concatenate([lax.dot_general(q0[r], k_ref[hh, r, :], nt, preferred_element_type=F32)
                                  for r in rows], axis=0)
            s1 = jnp.concatenate([lax.dot_general(q1[r], k_ref[hh, r, :], nt, preferred_element_type=F32)
                                  for r in rows], axis=0)
            e0 = weights(s0)
            e1 = weights(s1)
            for r in rows:
                v1 = jnp.concatenate([v_ref[hh, r, :], ones], axis=1)
                o_ref[r, cols] = finish(jnp.dot(e0[r], v1, preferred_element_type=F32),
                                        jnp.dot(e1[r], v1, preferred_element_type=F32))

    @pl.when(i >= N_PBLK)
    def _():
        side_jobs()
        ret_latent()
        cos = cos_ref[...]
        sin = sin_ref[...]
        low = (lax.broadcasted_iota(jnp.int32, (T_S, LANE), 1) & 16) == 0

        def rope(x):
            xs = jnp.where(low, pltpu.roll(x, LANE - 16, 1), pltpu.roll(x, 16, 1))
            return x * cos + xs * sin

        for hh in range(ATT_HPS):
            cols = slice(hh * LANE, (hh + 1) * LANE)
            head = pl.program_id(1) * ATT_HPS + hh
            q_scr[hh] = (rope(q_ref[hh].astype(F32)) * ATT_QSCALE).astype(BF16)
            k_scr[hh] = rope(k_ref[hh].astype(F32)).astype(BF16)
            ckt = ckt_ref[cols, :].astype(BF16)
            v_scr[hh, 0:T_S, 0:LANE] = v_ref[hh]
            v_scr[hh, T_S:T_S + PAST, 0:LANE] = cv_ref[:, head, :].astype(BF16)
            v_scr[hh, :, LANE:2 * LANE] = jnp.ones((T_S + PAST, LANE), BF16)

            def scores(qh, hh=hh, ckt=ckt):
                return jnp.concatenate([lax.dot_general(qh, k_scr[hh], nt, preferred_element_type=F32),
                                        jnp.dot(qh, ckt, preferred_element_type=F32)], axis=1)

            for b in range(T_S // ATT_TQ):
                r = slice(b * ATT_TQ, (b + 1) * ATT_TQ)
                q0, q1 = halves(q_scr[hh, r, :])
                e0 = weights(scores(q0))
                e1 = weights(scores(q1))
                o_ref[r, cols] = finish(jnp.dot(e0, v_scr[hh], preferred_element_type=F32),
                                        jnp.dot(e1, v_scr[hh], preferred_element_type=F32))


def _mixers(aux, proj, cache_k, cache_v, cos_t, sin_t, subln_g, s0f, s0b, gnorm, side_weights):
    nb = N_TOK // SEQ_BLK
    spb = SEQ_BLK // T_P
    smp = lambda i: jnp.maximum(i - N_PBLK, 0)
    pmt = lambda i: jnp.minimum(i, N_PBLK - 1)
    pmh = lambda i, h: jnp.where(i < N_PBLK, h, RET_H - 1)
    state_in = pl.BlockSpec((None, None, RET_KD, RET_VD), lambda i, h: (smp(i), h, 0, 0))
    state_out = pl.BlockSpec((spb, None, RET_KD, RET_VD), lambda i, h: (pmt(i), pmh(i, h), 0, 0))

    n_hp = DIFF_H // ATT_HPS
    assert n_hp == RET_H
    n_steps = (nb - N_PBLK) * n_hp
    step = lambda i, h: jnp.maximum((i - N_PBLK) * n_hp + h, 0)
    exp_per_step = N_EXP // n_steps
    exp_slice = lambda i, h: (step(i, h), 0, 0)
    row_slice = lambda i, h: (step(i, h), 0)
    up_spec = pl.BlockSpec((exp_per_step, D, FF), exp_slice)
    down_spec = pl.BlockSpec((exp_per_step, FF, D), exp_slice)
    side_specs = [up_spec, up_spec, down_spec]
    side_shapes = [jax.ShapeDtypeStruct(a.shape, BF16) for a in side_weights]
    zrows = R_MAX // n_steps

    return pl.pallas_call(
        _mixer_kernel,
        out_shape=(jax.ShapeDtypeStruct((N_TOK, DIFF_H * 2 * DIFF_HD), BF16),
                   jax.ShapeDtypeStruct((N_TOK, RET_H * RET_VD), BF16),
                   jax.ShapeDtypeStruct((16, RET_H, RET_KD, RET_VD), F32),
                   jax.ShapeDtypeStruct((16, RET_H, RET_KD, RET_VD), F32),
                   *side_shapes,
                   jax.ShapeDtypeStruct((R_MAX, XW), BF16)),
        grid=(nb, n_hp),
        in_specs=[pl.BlockSpec(memory_space=pltpu.SMEM),
                  pl.BlockSpec((ATT_HPS, SEQ_BLK, LANE), lambda i, h: (C_DQ // ATT_HPS + h, i, 0)),
                  pl.BlockSpec((ATT_HPS, SEQ_BLK, LANE), lambda i, h: (C_DK // ATT_HPS + h, i, 0)),
                  pl.BlockSpec((ATT_HPS, SEQ_BLK, LANE), lambda i, h: (C_DV // ATT_HPS + h, i, 0)),
                  pl.BlockSpec((None, ATT_HPS * LANE, PAST), lambda i, h: (smp(i), h, 0)),
                  pl.BlockSpec((None, None, PAST, DIFF_H, LANE), lambda i, h: (smp(i), 0, 0, 0, 0)),
                  pl.BlockSpec((T_S, LANE), lambda i, h: (0, 0)),
                  pl.BlockSpec((T_S, LANE), lambda i, h: (0, 0)),
                  pl.BlockSpec((1, LANE), lambda i, h: (0, 0)),
                  pl.BlockSpec((None, SEQ_BLK, LANE), lambda i, h: (C_RQ + h, i, 0)),
                  pl.BlockSpec((None, SEQ_BLK, LANE), lambda i, h: (C_RK + h, i, 0)),
                  pl.BlockSpec((2, SEQ_BLK, LANE), lambda i, h: (C_RV // 2 + h, i, 0)),
                  pl.BlockSpec((2, SEQ_BLK, LANE), lambda i, h: (C_RG // 2 + h, i, 0)),
                  state_in, state_in,
                  pl.BlockSpec((1, RET_VD), lambda i, h: (0, h)),
                  *side_specs],
        out_specs=(pl.BlockSpec((SEQ_BLK, ATT_HPS * LANE), lambda i, h: (i, h)),
                   pl.BlockSpec((SEQ_BLK, RET_VD), lambda i, h: (i, h)),
                   state_out, state_out,
                   *side_specs,
                   pl.BlockSpec((zrows, XW), row_slice)),
        scratch_shapes=[pltpu.VMEM((ATT_HPS, T_S, LANE), BF16),
                        pltpu.VMEM((ATT_HPS, T_S, LANE), BF16),
                        pltpu.VMEM((ATT_HPS, T_S + PAST, 2 * LANE), BF16),
                        pltpu.VMEM((RET_H, T_P, T_P), F32)],
        compiler_params=_cparams(("arbitrary", "arbitrary")),
        name="mixers",
    )(aux, proj, proj, proj, cache_k, cache_v, cos_t, sin_t, subln_g,
      proj, proj, proj, proj, s0f, s0b, gnorm, *side_weights)


def _rope_tables():
    n_rows = T_S // GRID_W
    row = np.repeat(np.arange(n_rows), GRID_W).astype(np.float64)
    col = np.tile(np.arange(GRID_W), n_rows).astype(np.float64)
    n_freq = DIFF_HD // 4
    inv = ROPE_BASE ** (-np.arange(n_freq, dtype=np.float64) / n_freq)

    def axis_tables(pos):
        ang = pos[:, None] * inv[None, :]
        c = np.cos(ang)
        s = np.sin(ang)
        return np.concatenate([c, c], axis=-1), np.concatenate([-s, s], axis=-1)

    cr, sr = axis_tables(row)
    cc, sc = axis_tables(col)
    cos_h = np.concatenate([cr, cc], axis=-1)
    sin_h = np.concatenate([sr, sc], axis=-1)
    return (jnp.asarray(np.concatenate([cos_h, cos_h], axis=-1), F32),
            jnp.asarray(np.concatenate([sin_h, sin_h], axis=-1), F32))


OP_TM = 512
OP_NPT = N_PROMPT // OP_TM


def _outproj_kernel(ra_ref, da_ref, gr_ref, gd_ref, xp_ref, xs_ref, mod_ref, n2_ref,
                    wro32_ref, wdo32_ref, wo32_ref, rgw_ref, rew_ref, brt_ref,
                    x1_ref, h2_ref, cw8_ref, info_ref, infot_ref, off_ref, pctab_ref, steps_ref,
                    m_scr, wro_ref, wdo_ref, wo_ref, wrt_ref, pc_scr, lg_scr):
    i = pl.program_id(0)

    @pl.when(i == 0)
    def _():
        wro_ref[...] = wro32_ref[...].astype(BF16)
        wdo_ref[...] = wdo32_ref[...].astype(BF16)
        wo_ref[...] = wo32_ref[...].astype(BF16)
        gap = jnp.zeros((ROUTE_E0 - N_GROUPS, D), F32)
        pad = jnp.zeros((ROUTE_W - ROUTE_E0 - N_EXP, D), F32)
        wr = jnp.concatenate([rgw_ref[...], gap, rew_ref[...], pad], axis=0)
        wrt_ref[...] = wr.astype(BF16)

        lg_scr[...] = jnp.zeros_like(lg_scr)

    def route_previous():
        cw8, info, pc, info_t = _route_cols(lg_scr[...])
        cw8_ref[...] = cw8
        info_ref[...] = info
        infot_ref[...] = info_t
        pc_scr[pl.ds(jnp.maximum(i - 1, 0), 1), :] = pc
        return cw8, info

    @pl.when(i < RT_NT)
    def _():
        routed = route_previous()
        ret_out = jnp.dot(ra_ref[...], wro_ref[...], preferred_element_type=F32)
        diff_out = jnp.dot(da_ref[...], wdo_ref[...], preferred_element_type=F32)
        for c in range(D // LANE):
            sl = slice(c * LANE, (c + 1) * LANE)
            m = (jax.nn.sigmoid(gr_ref[c].astype(F32)) * ret_out[:, sl]
                 + jax.nn.sigmoid(gd_ref[c].astype(F32)) * diff_out[:, sl])
            if c == 0:
                m = _ordered_after(_ordered_after(m, routed[0]), routed[1])
            m_scr[:, sl] = m.astype(BF16)
        mix = jnp.dot(m_scr[...], wo_ref[...], preferred_element_type=F32)
        mod = mod_ref[...]

        x1 = jnp.where(i < OP_NPT, xp_ref[...], xs_ref[...]) + mod[:, 2 * D:3 * D] * mix
        x1_ref[...] = x1
        h2 = (_rms(x1) * n2_ref[...] * (1.0 + mod[:, 4 * D:5 * D]) + mod[:, 3 * D:4 * D]).astype(BF16)
        h2_ref[...] = h2
        lg_scr[...] = lax.dot_general(wrt_ref[...], h2, (((1,), (1,)), ((), ())),
                                      preferred_element_type=F32) + brt_ref[...]

    @pl.when(i == RT_NT)
    def _():
        route_previous()
        off_ref[...], pctab_ref[...], steps_ref[...] = _dispatch_plan(pc_scr[...])


def _outproj(ret_act, diff_act, proj, xp, xs, mod3, n2, wro, wdo, wo, rgw, rew, brt):
    assert OP_TM == RT_TM
    npt = OP_NPT
    nt = N_TOK // OP_TM
    cur = lambda i: jnp.minimum(i, nt - 1)
    prev = lambda i: jnp.maximum(i - 1, 0)
    full = lambda i: (0, 0)
    once = pl.Buffered(1)
    return pl.pallas_call(
        _outproj_kernel,
        out_shape=(jax.ShapeDtypeStruct((N_TOK, D), F32),
                   jax.ShapeDtypeStruct((N_TOK, D), BF16),
                   jax.ShapeDtypeStruct((N_TOK, ROUTE_W), F32),
                   jax.ShapeDtypeStruct((N_TOK, ROUTE_W), F32),
                   jax.ShapeDtypeStruct((RT_NT, EPG, RT_TM), F32),
                   jax.ShapeDtypeStruct((RT_NT, ROUTE_W), jnp.int32),
                   jax.ShapeDtypeStruct((RT_NT, ROUTE_W), jnp.int32),
                   jax.ShapeDtypeStruct((PLAN_ROWS, ROUTE_W), jnp.int32)),
        grid=(nt + 1,),
        in_specs=[pl.BlockSpec((OP_TM, D), lambda i: (cur(i), 0)),
                  pl.BlockSpec((OP_TM, D), lambda i: (cur(i), 0)),
                  pl.BlockSpec((8, OP_TM, LANE), lambda i: (C_GR // 8, cur(i), 0)),
                  pl.BlockSpec((8, OP_TM, LANE), lambda i: (C_GD // 8, cur(i), 0)),
                  pl.BlockSpec((OP_TM, D), lambda i: (jnp.minimum(i, npt - 1), 0)),
                  pl.BlockSpec((OP_TM, D), lambda i: (jnp.maximum(cur(i) - npt, 0), 0)),
                  pl.BlockSpec((None, 1, 6 * D), lambda i: (_mod_row(OP_TM)(cur(i)), 0, 0)),
                  pl.BlockSpec((1, D), full),
                  pl.BlockSpec((D, D), full, pipeline_mode=once),
                  pl.BlockSpec((D, D), full, pipeline_mode=once),
                  pl.BlockSpec((D, D), full, pipeline_mode=once),
                  pl.BlockSpec((N_GROUPS, D), full, pipeline_mode=once),
                  pl.BlockSpec((N_EXP, D), full, pipeline_mode=once),
                  pl.BlockSpec((ROUTE_W, 1), full)],
        out_specs=(pl.BlockSpec((OP_TM, D), lambda i: (cur(i), 0)),
                   pl.BlockSpec((OP_TM, D), lambda i: (cur(i), 0)),
                   pl.BlockSpec((OP_TM, ROUTE_W), lambda i: (prev(i), 0)),
                   pl.BlockSpec((OP_TM, ROUTE_W), lambda i: (prev(i), 0)),
                   pl.BlockSpec((None, EPG, RT_TM), lambda i: (prev(i), 0, 0)),
                   pl.BlockSpec((RT_NT, ROUTE_W), full),
                   pl.BlockSpec((RT_NT, ROUTE_W), full),
                   pl.BlockSpec((PLAN_ROWS, ROUTE_W), full)),
        scratch_shapes=[pltpu.VMEM((OP_TM, D), BF16),
                        pltpu.VMEM((D, D), BF16), pltpu.VMEM((D, D), BF16), pltpu.VMEM((D, D), BF16),
                        pltpu.VMEM((ROUTE_W, D), BF16),
                        pltpu.VMEM((RT_NT, ROUTE_W), F32),
                        pltpu.VMEM((ROUTE_W, RT_TM), F32)],
        compiler_params=_cparams(("arbitrary",)),
        name="outproj",
    )(ret_act, diff_act, proj, proj, xp, xs, mod3, n2, wro, wdo, wo, rgw, rew, brt)


RT_TM = 512
RT_NT = N_TOK // RT_TM
PIECE = 16
R_LOC = RT_TM + N_GROUPS * PIECE
R_STAGE = 640
EX_TM = 512
R_MAX = 11264
EX_NT = R_MAX // EX_TM


ROUTE_E0 = EPG


def _route_cols(lgt):
    row = lax.broadcasted_iota(jnp.int32, (EPG, RT_TM), 0)
    neg = jnp.float32(-jnp.inf)

    def first_row(cond):
        return jnp.min(jnp.where(cond, row, EPG), axis=0, keepdims=True)

    head = lgt[0:EPG]
    is_g = row < N_GROUPS
    gl = jnp.where(is_g, head, neg)
    gmax = jnp.max(gl, axis=0, keepdims=True)
    gsum = jnp.sum(jnp.where(is_g, jnp.exp(head - gmax), 0.0), axis=0, keepdims=True)
    p_top = 1.0 / gsum
    g_idx = first_row(gl == gmax)
    el = lgt[ROUTE_E0 + (N_GROUPS - 1) * EPG:ROUTE_E0 + N_GROUPS * EPG]
    for g in reversed(range(N_GROUPS - 1)):
        el = jnp.where(g_idx == g, lgt[ROUTE_E0 + g * EPG:ROUTE_E0 + (g + 1) * EPG], el)
    ee = jnp.exp(el - jnp.max(el, axis=0, keepdims=True))
    ep = ee / jnp.sum(ee, axis=0, keepdims=True)
    e1 = jnp.max(ep, axis=0, keepdims=True)
    i1 = first_row(ep == e1)
    ep2 = jnp.where(row == i1, -1.0, ep)
    e2 = jnp.max(ep2, axis=0, keepdims=True)
    i2 = first_row(ep2 == e2)
    den = e1 + e2
    cw8_t = (jnp.where(row == i1, p_top * e1 / den, 0.0)
             + jnp.where(row == i2, p_top * e2 / den, 0.0))

    onehot = (row == g_idx).astype(F32)
    ii = lax.broadcasted_iota(jnp.int32, (RT_TM, RT_TM), 0)
    jj = lax.broadcasted_iota(jnp.int32, (RT_TM, RT_TM), 1)
    earlier = (ii < jj).astype(BF16)
    prefix = jnp.dot(onehot.astype(BF16), earlier, preferred_element_type=F32)
    cnt = jnp.sum(onehot, axis=1, keepdims=True)
    pc_col = jnp.floor((cnt + (PIECE - 1.0)) * (1.0 / PIECE)) * PIECE
    row1 = lax.broadcasted_iota(jnp.int32, (EPG, 1), 0)
    lane1 = lax.broadcasted_iota(jnp.int32, (1, ROUTE_W), 1)
    lo = jnp.zeros((EPG, 1), F32)
    run = jnp.zeros((1, 1), F32)
    pc = jnp.zeros((1, ROUTE_W), F32)
    for g in range(N_GROUPS):
        pc_g = jnp.sum(jnp.where(row1 == g, pc_col, 0.0), axis=0, keepdims=True)
        lo = jnp.where(row1 == g, run, lo)
        pc = jnp.where(lane1 == g, pc_g, pc)
        run = run + pc_g
    dest = jnp.sum(onehot * (prefix + lo), axis=0, keepdims=True)
    info_t = jnp.where(row == 0, g_idx.astype(F32), jnp.where(row == 1, dest, 0.0))

    slab = jnp.concatenate([cw8_t, info_t, jnp.zeros((LANE - 2 * EPG, RT_TM), F32)], axis=0)
    cols = slab.T
    lane = lax.broadcasted_iota(jnp.int32, cols.shape, 1)
    cw8 = jnp.where(lane < EPG, cols, 0.0)
    info = jnp.where(lane < 2, pltpu.roll(cols, LANE - EPG, 1), 0.0)
    return cw8, info, pc, info_t


PLAN_ROWS = 32


def _dispatch_plan(pc_all):
    lane = lax.broadcasted_iota(jnp.int32, (1, ROUTE_W), 1)
    is_g = lane < N_GROUPS
    seg_len = jnp.sum(pc_all, axis=0, keepdims=True)
    seg_pad = jnp.floor((seg_len + (EX_TM - 1.0)) * (1.0 / EX_TM)) * EX_TM
    run = seg_pad + pltpu.roll(seg_pad, 1, 1)
    seg_end_pad = jnp.where(is_g, run + pltpu.roll(run, 2, 1), 0.0)
    seg_start = seg_end_pad - seg_pad
    ti = lax.broadcasted_iota(jnp.int32, (RT_NT, RT_NT), 0)
    tj = lax.broadcasted_iota(jnp.int32, (RT_NT, RT_NT), 1)
    earlier = (tj < ti).astype(BF16)
    within = jnp.dot(earlier, pc_all.astype(BF16), preferred_element_type=F32)
    chunk_off = jnp.where(is_g, seg_start + within, 0.0)

    start = (lax.broadcasted_iota(jnp.int32, (PLAN_ROWS, 1), 0) * EX_TM).astype(F32)
    passed = jnp.where(jnp.logical_and(is_g, start >= seg_end_pad), 1.0, 0.0)
    group = jnp.minimum(jnp.sum(passed, axis=1, keepdims=True), N_GROUPS - 1.0)
    used_end = jnp.sum(jnp.where(lane == group.astype(jnp.int32), seg_start + seg_len, 0.0),
                       axis=1, keepdims=True)
    used = jnp.clip(used_end - start, 0.0, float(EX_TM))
    steps = jnp.where(lane == 0, group, jnp.where(lane == 1, used, 0.0))
    return chunk_off.astype(jnp.int32), pc_all.astype(jnp.int32), steps.astype(jnp.int32)


def _piece_copies(off_ref, pc_ref, tile, make):
    lo = 0
    for g in range(N_GROUPS):
        n = pc_ref[tile, g] // PIECE
        base = off_ref[tile, g]

        def body(j, carry, lo=lo, base=base):
            make(pl.multiple_of(lo + j * PIECE, PIECE), pl.multiple_of(base + j * PIECE, PIECE))
            return carry

        lax.fori_loop(0, n, body, 0)
        lo = lo + pc_ref[tile, g]


def _piece_count(pc_ref, tile):
    n = 0
    for g in range(N_GROUPS):
        n = n + pc_ref[tile, g] // PIECE
    return n


XW = D + ROUTE_W


def _dispatch_kernel(off_ref, pc_ref, h_ref, infot_ref, cw8_ref, xs_in, xs_out, x_scr, sem):
    del xs_in
    i = pl.program_id(0)
    slot = i % 2
    dest = infot_ref[1:2, :]
    row = lax.broadcasted_iota(jnp.int32, (R_LOC, RT_TM), 0).astype(F32)
    sel = (row == dest).astype(BF16)
    cw = cw8_ref[...]
    hi = cw.astype(BF16).astype(F32)
    mid = (cw - hi).astype(BF16).astype(F32)
    low = (cw - hi - mid).astype(BF16).astype(F32)
    pieces = (hi + pltpu.roll(mid, EPG, 1) + pltpu.roll(low, 2 * EPG, 1)).astype(BF16)
    rows = jnp.concatenate([h_ref[...], pieces], axis=1)
    x_scr[slot] = jnp.dot(sel, rows, preferred_element_type=F32).astype(BF16)

    def x_copy(s, src, dst):
        return pltpu.make_async_copy(x_scr.at[s, pl.ds(src, PIECE)], xs_out.at[pl.ds(dst, PIECE)],
                                     sem.at[s])

    _piece_copies(off_ref, pc_ref, i, lambda src, dst: x_copy(slot, src, dst).start())

    def wait_tile(tile, s):
        def wait(j, carry):
            x_copy(s, 0, 0).wait()
            return carry

        lax.fori_loop(0, _piece_count(pc_ref, tile), wait, 0)

    @pl.when(i > 0)
    def _():
        wait_tile(i - 1, 1 - slot)

    @pl.when(i == RT_NT - 1)
    def _():
        wait_tile(i, slot)


def _dispatch(chunk_off, pc, h2, info_t, cw8, xs0):
    grid_spec = pltpu.PrefetchScalarGridSpec(
        num_scalar_prefetch=2,
        grid=(RT_NT,),
        in_specs=[pl.BlockSpec((RT_TM, D), lambda i, o, p: (i, 0)),
                  pl.BlockSpec((None, EPG, RT_TM), lambda i, o, p: (i, 0, 0)),
                  pl.BlockSpec((RT_TM, ROUTE_W), lambda i, o, p: (i, 0)),
                  pl.BlockSpec(memory_space=pl.ANY)],
        out_specs=pl.BlockSpec(memory_space=pl.ANY),
        scratch_shapes=[pltpu.VMEM((2, R_LOC, XW), BF16),
                        pltpu.SemaphoreType.DMA((2,))])
    return pl.pallas_call(
        _dispatch_kernel,
        out_shape=jax.ShapeDtypeStruct((R_MAX, XW), BF16),
        grid_spec=grid_spec,
        input_output_aliases={5: 0},
        compiler_params=_cparams(("arbitrary",)),
        name="dispatch",
    )(chunk_off, pc, h2, info_t, cw8, xs0)


def _expert_kernel(steps_ref, x_ref, wg_ref, wu_ref, wd_ref, y_ref):
    k = pl.program_id(0)
    used = steps_ref[k, 1]
    half = EX_TM // 2

    def run(rows):
        x = x_ref[rows, 0:D]
        cw = x_ref[rows, D:XW].astype(F32)
        lane = lax.broadcasted_iota(jnp.int32, cw.shape, 1)
        acc = jnp.zeros(x.shape, F32)
        for j in range(EPG):
            mine = jnp.logical_and((lane & (EPG - 1)) == j, lane < 3 * EPG)
            w = jnp.sum(jnp.where(mine, cw, 0.0), axis=-1, keepdims=True)
            a = (jax.nn.silu(jnp.dot(x, wg_ref[j], preferred_element_type=F32))
                 * jnp.dot(x, wu_ref[j], preferred_element_type=F32))
            acc = acc + jnp.dot((a * w).astype(BF16), wd_ref[j], preferred_element_type=F32)
        y_ref[rows, :] = acc.astype(BF16)

    @pl.when(used > half)
    def _():
        run(slice(0, EX_TM))

    @pl.when(jnp.logical_and(used > 0, used <= half))
    def _():
        run(slice(0, half))
        y_ref[half:EX_TM, :] = jnp.zeros((EX_TM - half, D), BF16)

    @pl.when(used == 0)
    def _():
        y_ref[...] = jnp.zeros_like(y_ref)


def _experts(steps, xs, wg, wu, wd):
    grid_spec = pltpu.PrefetchScalarGridSpec(
        num_scalar_prefetch=1,
        grid=(EX_NT,),
        in_specs=[pl.BlockSpec((EX_TM, XW), lambda k, st: (k, 0)),
                  pl.BlockSpec((EPG, D, FF), lambda k, st: (st[k, 0], 0, 0)),
                  pl.BlockSpec((EPG, D, FF), lambda k, st: (st[k, 0], 0, 0)),
                  pl.BlockSpec((EPG, FF, D), lambda k, st: (st[k, 0], 0, 0))],
        out_specs=pl.BlockSpec((EX_TM, D), lambda k, st: (k, 0)))
    return pl.pallas_call(
        _expert_kernel,
        out_shape=jax.ShapeDtypeStruct((R_MAX, D), BF16),
        grid_spec=grid_spec,
        compiler_params=_cparams(("arbitrary",)),
        name="experts",
    )(steps, xs, wg, wu, wd)


CB_NPT = N_PROMPT // RT_TM


def _combine_kernel(off_ref, pc_ref, info_ref, x1_ref, mod_ref, g_ref, ys_hbm,
                    yp_ref, ysm_ref, stage, sem):
    i = pl.program_id(0)
    slot = i % 2

    def copy(s, dst, src):
        return pltpu.make_async_copy(ys_hbm.at[pl.ds(src, PIECE)], stage.at[s, pl.ds(dst, PIECE)],
                                     sem.at[s])

    def fetch(tile, s):
        _piece_copies(off_ref, pc_ref, tile, lambda loc, glob: copy(s, loc, glob).start())

    @pl.when(i == 0)
    def _():
        stage[...] = jnp.zeros_like(stage)
        fetch(0, 0)

    @pl.when(i + 1 < RT_NT)
    def _():
        fetch(i + 1, 1 - slot)

    def wait(j, carry):
        copy(slot, 0, 0).wait()
        return carry

    lax.fori_loop(0, _piece_count(pc_ref, i), wait, 0)

    dest = info_ref[...][:, 1:2]
    col = lax.broadcasted_iota(jnp.int32, (RT_TM, R_STAGE), 1).astype(F32)
    moe = jnp.dot((col == dest).astype(BF16), stage[slot], preferred_element_type=F32)
    mod = mod_ref[...]
    out = _rms(x1_ref[...] + mod[:, 5 * D:6 * D] * moe) * g_ref[...]

    @pl.when(i < CB_NPT)
    def _():
        yp_ref[...] = out

    @pl.when(i >= CB_NPT)
    def _():
        ysm_ref[...] = out


def _combine(chunk_off, pc, info, x1, mod3, fg, ys):
    npt = CB_NPT
    grid_spec = pltpu.PrefetchScalarGridSpec(
        num_scalar_prefetch=2,
        grid=(RT_NT,),
        in_specs=[pl.BlockSpec((RT_TM, ROUTE_W), lambda i, o, p: (i, 0)),
                  pl.BlockSpec((RT_TM, D), lambda i, o, p: (i, 0)),
                  pl.BlockSpec((None, 1, 6 * D), lambda i, o, p: (_mod_row(RT_TM)(i), 0, 0)),
                  pl.BlockSpec((1, D), lambda i, o, p: (0, 0)),
                  pl.BlockSpec(memory_space=pl.ANY)],
        out_specs=(pl.BlockSpec((RT_TM, D), lambda i, o, p: (jnp.minimum(i, npt - 1), 0)),
                   pl.BlockSpec((RT_TM, D), lambda i, o, p: (jnp.maximum(i - npt, 0), 0))),
        scratch_shapes=[pltpu.VMEM((2, R_STAGE, D), BF16),
                        pltpu.SemaphoreType.DMA((2,))])
    return pl.pallas_call(
        _combine_kernel,
        out_shape=(jax.ShapeDtypeStruct((N_PROMPT, D), F32),
                   jax.ShapeDtypeStruct((N_SAMPLE, D), F32)),
        grid_spec=grid_spec,
        compiler_params=_cparams(("arbitrary",)),
        name="combine",
    )(chunk_off, pc, info, x1, mod3, fg, ys)


def kernel(x_prompt, x_sample, cache_diff_k, cache_diff_v, state_ret_fwd, state_ret_bwd, c, c_ctx,
           w_ada, b_ada, norm1_g, norm2_g, w_in, ret_decay_fwd, ret_decay_bwd, ret_norm_g,
           diff_lambda_q1, diff_lambda_k1, diff_lambda_q2, diff_lambda_k2, diff_subln_g,
           w_ret_o, w_diff_o, w_o, router_group_w, router_group_b, router_expert_w, router_expert_b,
           moe_w_gate, moe_w_up, moe_w_down, final_norm_g):
    l = 0
    lam_init = 0.8 - 0.6 * math.exp(-0.3 * l)
    row = lambda a: a[l].astype(F32)[None, :]

    xp = x_prompt.reshape(N_PROMPT, D)
    xs = x_sample.reshape(N_SAMPLE, D)
    mod3, aux, brt = _modulation(
        lam_init, c_ctx[None, :], c, w_ada[l], b_ada[l][None, :],
        [row(diff_lambda_q1), row(diff_lambda_k1), row(diff_lambda_q2), row(diff_lambda_k2),
         row(ret_decay_fwd), row(ret_decay_bwd), row(router_group_b), row(router_expert_b)])

    proj, kt32, v32 = _inproj(xp, xs, mod3, norm1_g[l][None, :], w_in[l])

    cos_t, sin_t = _rope_tables()
    cache_kt = jnp.transpose(cache_diff_k[:, l], (0, 2, 3, 4, 1)).reshape(4, D, PAST)
    (diff_act, ret_act, s_f, s_b, wg_bf, wu_bf, wd_bf, xs0) = _mixers(
        aux, proj, cache_kt, cache_diff_v, cos_t, sin_t, diff_subln_g[l][None, :],
        state_ret_fwd[:, l], state_ret_bwd[:, l], ret_norm_g[l][None, :],
        (moe_w_gate[l], moe_w_up[l], moe_w_down[l]))

    x1, h2, cw8, info, info_t, chunk_off, pc, steps = _outproj(
        ret_act, diff_act, proj, xp, xs, mod3, norm2_g[l][None, :],
        w_ret_o[l], w_diff_o[l], w_o[l], router_group_w[l].T, router_expert_w[l].T, brt)
    xs_sorted = _dispatch(chunk_off, pc, h2, info_t, cw8, xs0)
    y_sorted = _experts(steps, xs_sorted, wg_bf, wu_bf, wd_bf)
    yp, ys = _combine(chunk_off, pc, info, x1, mod3, final_norm_g[None, :], y_sorted)

    return (yp.reshape(16, T_P, D), ys.reshape(4, T_S, D),
            jnp.transpose(kt32.reshape(16, DIFF_H, 2, DIFF_HD, T_P), (0, 4, 1, 2, 3))[:, None],
            v32.reshape(16, 1, T_P, DIFF_H, 2 * DIFF_HD),
            s_f.reshape(16, 1, RET_H, RET_KD, RET_VD), s_b.reshape(16, 1, RET_H, RET_KD, RET_VD))
```

```python
import functools
import math

import jax
import jax.numpy as jnp
import numpy as np
from jax import lax
from jax.experimental import pallas as pl
from jax.experimental.pallas import tpu as pltpu

F32 = jnp.float32
BF16 = jnp.bfloat16

D = 1024
N_PROMPT = 16 * 256
N_SAMPLE = 4 * 1024
N_TOK = N_PROMPT + N_SAMPLE
T_P = 256
T_S = 1024
PAST = 512
GRID_W = 64
RET_H = 4
RET_KD = 128
RET_VD = 256
DIFF_H = 8
DIFF_HD = 64
ROPE_BASE = 10000.0
N_GROUPS = 4
EPG = 8
N_EXP = 32
FF = 256
EPS = 1e-6
IN_W = 8192
LANE = 128
N_CHUNK = IN_W // LANE
C_RQ, C_RK, C_RV, C_RG, C_DQ, C_DK, C_DV, C_GR, C_GD = 0, 4, 8, 16, 24, 32, 40, 48, 56
ROUTE_W = 128
SEQ_BLK = 1024
N_PBLK = N_PROMPT // SEQ_BLK
V7X_VMEM_BYTES = 64 * 1024 * 1024
VMEM_LIMIT = V7X_VMEM_BYTES - 8 * 1024 * 1024


def _cparams(sem):
    return pltpu.CompilerParams(dimension_semantics=sem, vmem_limit_bytes=VMEM_LIMIT)


def _mod_row(tile_rows):
    def f(i):
        start = i * tile_rows
        return jnp.where(start < N_PROMPT, 0, 1 + (start - N_PROMPT) // T_S)
    return f


def _rms(x):
    return x * lax.rsqrt(jnp.mean(x * x, axis=-1, keepdims=True) + EPS)


def _ordered_after(dst, src):
    m = None
    for a in range(src.shape[0] // 8):
        for b in range(src.shape[1] // LANE):
            part = src[a * 8:(a + 1) * 8, b * LANE:(b + 1) * LANE]
            m = part if m is None else jnp.maximum(m, part)
    never = jnp.logical_and(m != m, m == m)
    return jnp.where(jnp.tile(never, (dst.shape[0] // 8, 1)), 0.0, dst)


MOD_ROWS = 8


AUX_LAM = 0
AUX_LG = 1


def _mod_kernel(lam_init, ctx_ref, c_ref, w_ref, b_ref, lq1_ref, lk1_ref, lq2_ref, lk2_ref, df_ref, db_ref,
                gb_ref, eb_ref, o_ref, aux_ref, brt_ref):
    cond = jnp.concatenate([ctx_ref[...], c_ref[...],
                            jnp.zeros((MOD_ROWS - 1 - c_ref.shape[0], D), F32)], axis=0)
    s = jax.nn.silu(cond)
    out = jnp.dot(s.astype(BF16), w_ref[...].astype(BF16), preferred_element_type=F32) + b_ref[...]
    for r in range(MOD_ROWS):
        o_ref[r] = out[r:r + 1, :]

    def lane_row(parts):
        used = sum(p.shape[1] for p in parts)
        return jnp.concatenate(parts + [jnp.zeros((1, LANE - used), F32)], axis=1)

    lam = (jnp.exp(jnp.sum(lq1_ref[...] * lk1_ref[...], axis=1, keepdims=True))
           - jnp.exp(jnp.sum(lq2_ref[...] * lk2_ref[...], axis=1, keepdims=True)) + lam_init)
    rows = [lane_row([lam, jnp.full((1, 1), 1.0 - lam_init, F32)]),
            lane_row([jax.nn.log_sigmoid(df_ref[...])]),
            lane_row([jax.nn.log_sigmoid(db_ref[...])])]
    aux_ref[...] = jnp.concatenate(rows + [jnp.zeros((8 - len(rows), LANE), F32)], axis=0)
    bias = lane_row([gb_ref[...], jnp.zeros((1, ROUTE_E0 - N_GROUPS), F32), eb_ref[...]])
    brt_ref[...] = jnp.tile(bias, (8, 1)).T[:, 0:1]


def _modulation(lam_init, c_ctx, c, w_ada, b_ada, small):
    tn = 1536
    whole = lambda a: pl.BlockSpec(a.shape, lambda j: (0, 0))
    return pl.pallas_call(
        functools.partial(_mod_kernel, lam_init),
        out_shape=(jax.ShapeDtypeStruct((MOD_ROWS, 1, 6 * D), F32),
                   jax.ShapeDtypeStruct((8, LANE), F32),
                   jax.ShapeDtypeStruct((ROUTE_W, 1), F32)),
        grid=(6 * D // tn,),
        in_specs=[pl.BlockSpec((1, D), lambda j: (0, 0)),
                  pl.BlockSpec(c.shape, lambda j: (0, 0)),
                  pl.BlockSpec((D, tn), lambda j: (0, j)),
                  pl.BlockSpec((1, tn), lambda j: (0, j))] + [whole(a) for a in small],
        out_specs=(pl.BlockSpec((MOD_ROWS, 1, tn), lambda j: (0, 0, j)),
                   pl.BlockSpec((8, LANE), lambda j: (0, 0)),
                   pl.BlockSpec((ROUTE_W, 1), lambda j: (0, 0))),
        compiler_params=_cparams(("arbitrary",)),
        name="mod",
    )(c_ctx, c, w_ada, b_ada, *small)


IP_TM = 512
IP_TN = 2048
IP_NPT = N_PROMPT // IP_TM


IP_KV_TILE = C_DK * LANE // IP_TN
IP_SPT = IP_TM // T_P


def _inproj_kernel(xp_ref, xs_ref, mod_ref, n1_ref, w_ref, proj_ref, kt_ref, v32_ref, w_scr):
    j = pl.program_id(0)
    i = pl.program_id(1)

    @pl.when(i == 0)
    def _():
        w_scr[...] = w_ref[...].astype(BF16)

    x = jnp.where(i < IP_NPT, xp_ref[...], xs_ref[...])
    mod = mod_ref[...]
    h = (_rms(x) * n1_ref[...] * (1.0 + mod[:, D:2 * D]) + mod[:, 0:D]).astype(BF16)
    acc = jnp.dot(h, w_scr[...], preferred_element_type=F32)
    for c in range(IP_TN // LANE):
        proj_ref[c] = acc[:, c * LANE:(c + 1) * LANE].astype(BF16)

    @pl.when(jnp.logical_and(j == IP_KV_TILE, i < IP_NPT))
    def _():
        for s in range(IP_SPT):
            kt_ref[s] = acc[s * T_P:(s + 1) * T_P, :D].T
        v32_ref[...] = acc[:, D:]


def _inproj(xp, xs, mod3, n1, w_in):
    npt = IP_NPT
    cpt = IP_TN // LANE

    def kv_tile(j, i):
        return jnp.where(j < IP_KV_TILE, 0, jnp.where(j == IP_KV_TILE, jnp.minimum(i, npt - 1), npt - 1))

    return pl.pallas_call(
        _inproj_kernel,
        out_shape=(jax.ShapeDtypeStruct((N_CHUNK, N_TOK, LANE), BF16),
                   jax.ShapeDtypeStruct((N_PROMPT // T_P, D, T_P), F32),
                   jax.ShapeDtypeStruct((N_PROMPT, D), F32)),
        grid=(IN_W // IP_TN, N_TOK // IP_TM),
        in_specs=[pl.BlockSpec((IP_TM, D), lambda j, i: (jnp.minimum(i, npt - 1), 0)),
                  pl.BlockSpec((IP_TM, D), lambda j, i: (jnp.maximum(i - npt, 0), 0)),
                  pl.BlockSpec((None, 1, 6 * D), lambda j, i: (_mod_row(IP_TM)(i), 0, 0)),
                  pl.BlockSpec((1, D), lambda j, i: (0, 0)),
                  pl.BlockSpec((D, IP_TN), lambda j, i: (0, j))],
        out_specs=(pl.BlockSpec((cpt, IP_TM, LANE), lambda j, i: (j, i, 0)),
                   pl.BlockSpec((IP_SPT, D, T_P), lambda j, i: (kv_tile(j, i), 0, 0)),
                   pl.BlockSpec((IP_TM, D), lambda j, i: (kv_tile(j, i), 0))),
        scratch_shapes=[pltpu.VMEM((D, IP_TN), BF16)],
        compiler_params=_cparams(("arbitrary", "arbitrary")),
        name="inproj",
    )(xp, xs, mod3, n1, w_in)


def _decay_mask(t, lgf, lgb):
    ii = lax.broadcasted_iota(jnp.int32, (t, t), 0)
    jj = lax.broadcasted_iota(jnp.int32, (t, t), 1)
    rel = (ii - jj).astype(F32)
    e = jnp.exp(jnp.where(rel >= 0.0, lgf, -lgb) * rel)
    return jnp.where(rel == 0.0, 2.0, e) * (RET_KD ** -0.5)


def _ret_parts(lgf, lgb, q_ref, k_ref, v_ref, rg_ref, s0f_ref, s0b_ref, g_ref,
               o_ref, sf_ref, sb_ref, dm_ref):
    gain = g_ref[...]
    nt = (((1,), (1,)), ((), ()))
    tn = (((0,), (0,)), ((), ()))

    def finish(o, rg):
        d = o - jnp.mean(o, axis=-1, keepdims=True)
        y = d * lax.rsqrt(jnp.mean(d * d, axis=-1, keepdims=True) + EPS) * gain
        return (jax.nn.silu(rg.astype(F32)) * y).astype(BF16)

    def vcat(r):
        return jnp.concatenate([v_ref[0, r, :], v_ref[1, r, :]], axis=1)

    def gcat(r):
        return jnp.concatenate([rg_ref[0, r, :], rg_ref[1, r, :]], axis=1)

    def build_mask():
        dm_ref[...] = _decay_mask(T_P, lgf, lgb)

    n_chunk = SEQ_BLK // T_P
    chunks = [slice(c * T_P, (c + 1) * T_P) for c in range(n_chunk)]
    t = lax.broadcasted_iota(jnp.int32, (T_P, 1), 0).astype(F32)
    kdf = jnp.exp(lgf * (T_P - 1.0 - t)) * (RET_KD ** -0.5)
    kdb = jnp.exp(lgb * t) * (RET_KD ** -0.5)

    def intra(r):
        sc = lax.dot_general(q_ref[r, :], k_ref[r, :], nt, preferred_element_type=F32)
        return jnp.dot((sc * dm_ref[...]).astype(BF16), vcat(r), preferred_element_type=F32)

    def key_state(r, kd):
        kw = (k_ref[r, :].astype(F32) * kd).astype(BF16)
        return lax.dot_general(kw, vcat(r), tn, preferred_element_type=F32)

    def context_part():
        for s, r in enumerate(chunks):
            o_ref[r, :] = finish(intra(r), gcat(r))
            sf_ref[s] = key_state(r, kdf)
            sb_ref[s] = key_state(r, kdb)

    def latent_part():
        qdf = jnp.exp(lgf * (t + 1.0))
        qdb = jnp.exp(lgb * (T_P - t))
        span = jnp.full((1, 1), float(T_P), F32)
        cf = jnp.exp(lgf * span)
        cb = jnp.exp(lgb * span)
        before_b = [None] * n_chunk
        state = s0b_ref[...]
        for c in reversed(range(n_chunk)):
            before_b[c] = state
            if c > 0:
                state = cb * state + key_state(chunks[c], kdb)
        state = s0f_ref[...]
        for c, r in enumerate(chunks):
            qf = q_ref[r, :].astype(F32)
            o = (intra(r)
                 + jnp.dot((qf * qdf).astype(BF16), state.astype(BF16), preferred_element_type=F32)
                 + jnp.dot((qf * qdb).astype(BF16), before_b[c].astype(BF16), preferred_element_type=F32))
            o_ref[r, :] = finish(o, gcat(r))
            if c + 1 < n_chunk:
                state = cf * state + key_state(r, kdf)

    return build_mask, context_part, latent_part


ATT_TQ = 1024
ATT_HPS = 2
ATT_QSCALE = (DIFF_HD ** -0.5) * math.log2(math.e)


N_SIDE = 3


def _mixer_kernel(aux_ref, q_ref, k_ref, v_ref, ckt_ref, cv_ref, cos_ref, sin_ref, g_ref,
                  rq_ref, rk_ref, rv_ref, rg_ref, s0f_ref, s0b_ref, gn_ref, *rest):
    side_in = rest[:N_SIDE]
    o_ref, ro_ref, sf_ref, sb_ref = rest[N_SIDE:N_SIDE + 4]
    side_out = rest[N_SIDE + 4:2 * N_SIDE + 4]
    xs0_ref, q_scr, k_scr, v_scr, dm_scr = rest[2 * N_SIDE + 4:]
    i = pl.program_id(0)
    rh = pl.program_id(1)
    build_mask, ret_context, ret_latent = _ret_parts(
        aux_ref[AUX_LG, rh], aux_ref[AUX_LG + 1, rh], rq_ref, rk_ref, rv_ref, rg_ref, s0f_ref, s0b_ref, gn_ref,
        ro_ref, sf_ref, sb_ref, dm_scr.at[rh])

    @pl.when(i == 0)
    def _():
        build_mask()

    def side_jobs():
        for src, dst in zip(side_in, side_out):
            dst[...] = src[...].astype(BF16)
        xs0_ref[...] = jnp.zeros_like(xs0_ref)

    lam = aux_ref[AUX_LAM, 0]
    out_scale = aux_ref[AUX_LAM, 1]
    gain = g_ref[...] * out_scale
    lane = lax.broadcasted_iota(jnp.int32, (1, LANE), 1)
    first = lane < DIFF_HD
    nt = (((1,), (1,)), ((), ()))

    def halves(q):
        zero = jnp.zeros_like(q)
        return jnp.where(first, q, zero), jnp.where(first, zero, q)

    def weights(s):
        return jnp.exp2(s - jnp.max(s, axis=-1, keepdims=True)).astype(BF16)

    def finish(of0, of1):
        o = of0[:, :LANE] / of0[:, LANE:] - lam * (of1[:, :LANE] / of1[:, LANE:])
        return (_rms(o) * gain).astype(BF16)

    @pl.when(i < N_PBLK)
    def _():
        ret_context()
        n_seq = SEQ_BLK // T_P
        rows = [slice(s * T_P, (s + 1) * T_P) for s in range(n_seq)]
        ones = jnp.ones((T_P, LANE), BF16)
        for hh in range(ATT_HPS):
            cols = slice(hh * LANE, (hh + 1) * LANE)
            q0, q1 = halves((q_ref[hh].astype(F32) * ATT_QSCALE).astype(BF16))
            s0 = jnp.concatenate([lax.dot_general(q0[r], k_ref[hh, r, :], nt, preferred_element_type=F32)
                                  for r in rows], axis=0)
            s1 = jnp.concatenate([lax.dot_general(q1[r], k_ref[hh, r, :], nt, preferred_element_type=F32)
                                  for r in rows], axis=0)
            e0 = weights(s0)
            e1 = weights(s1)
            for r in rows:
                v1 = jnp.concatenate([v_ref[hh, r, :], ones], axis=1)
                o_ref[r, cols] = finish(jnp.dot(e0[r], v1, preferred_element_type=F32),
                                        jnp.dot(e1[r], v1, preferred_element_type=F32))

    @pl.when(i >= N_PBLK)
    def _():
        side_jobs()
        ret_latent()
        cos = cos_ref[...]
        sin = sin_ref[...]
        low = (lax.broadcasted_iota(jnp.int32, (T_S, LANE), 1) & 16) == 0

        def rope(x):
            xs = jnp.where(low, pltpu.roll(x, LANE - 16, 1), pltpu.roll(x, 16, 1))
            return x * cos + xs * sin

        for hh in range(ATT_HPS):
            cols = slice(hh * LANE, (hh + 1) * LANE)
            head = pl.program_id(1) * ATT_HPS + hh
            q_scr[hh] = (rope(q_ref[hh].astype(F32)) * ATT_QSCALE).astype(BF16)
            k_scr[hh] = rope(k_ref[hh].astype(F32)).astype(BF16)
            ckt = ckt_ref[cols, :].astype(BF16)
            v_scr[hh, 0:T_S, 0:LANE] = v_ref[hh]
            v_scr[hh, T_S:T_S + PAST, 0:LANE] = cv_ref[:, head, :].astype(BF16)
            v_scr[hh, :, LANE:2 * LANE] = jnp.ones((T_S + PAST, LANE), BF16)

            def scores(qh, hh=hh, ckt=ckt):
                return jnp.concatenate([lax.dot_general(qh, k_scr[hh], nt, preferred_element_type=F32),
                                        jnp.dot(qh, ckt, preferred_element_type=F32)], axis=1)

            for b in range(T_S // ATT_TQ):
                r = slice(b * ATT_TQ, (b + 1) * ATT_TQ)
                q0, q1 = halves(q_scr[hh, r, :])
                e0 = weights(scores(q0))
                e1 = weights(scores(q1))
                o_ref[r, cols] = finish(jnp.dot(e0, v_scr[hh], preferred_element_type=F32),
                                        jnp.dot(e1, v_scr[hh], preferred_element_type=F32))


def _mixers(aux, proj, cache_k, cache_v, cos_t, sin_t, subln_g, s0f, s0b, gnorm, side_weights):
    nb = N_TOK // SEQ_BLK
    spb = SEQ_BLK // T_P
    smp = lambda i: jnp.maximum(i - N_PBLK, 0)
    pmt = lambda i: jnp.minimum(i, N_PBLK - 1)
    pmh = lambda i, h: jnp.where(i < N_PBLK, h, RET_H - 1)
    state_in = pl.BlockSpec((None, None, RET_KD, RET_VD), lambda i, h: (smp(i), h, 0, 0))
    state_out = pl.BlockSpec((spb, None, RET_KD, RET_VD), lambda i, h: (pmt(i), pmh(i, h), 0, 0))

    n_hp = DIFF_H // ATT_HPS
    assert n_hp == RET_H
    n_steps = (nb - N_PBLK) * n_hp
    step = lambda i, h: jnp.maximum((i - N_PBLK) * n_hp + h, 0)
    exp_per_step = N_EXP // n_steps
    exp_slice = lambda i, h: (step(i, h), 0, 0)
    row_slice = lambda i, h: (step(i, h), 0)
    up_spec = pl.BlockSpec((exp_per_step, D, FF), exp_slice)
    down_spec = pl.BlockSpec((exp_per_step, FF, D), exp_slice)
    side_specs = [up_spec, up_spec, down_spec]
    side_shapes = [jax.ShapeDtypeStruct(a.shape, BF16) for a in side_weights]
    zrows = R_MAX // n_steps

    return pl.pallas_call(
        _mixer_kernel,
        out_shape=(jax.ShapeDtypeStruct((N_TOK, DIFF_H * 2 * DIFF_HD), BF16),
                   jax.ShapeDtypeStruct((N_TOK, RET_H * RET_VD), BF16),
                   jax.ShapeDtypeStruct((16, RET_H, RET_KD, RET_VD), F32),
                   jax.ShapeDtypeStruct((16, RET_H, RET_KD, RET_VD), F32),
                   *side_shapes,
                   jax.ShapeDtypeStruct((R_MAX, XW), BF16)),
        grid=(nb, n_hp),
        in_specs=[pl.BlockSpec(memory_space=pltpu.SMEM),
                  pl.BlockSpec((ATT_HPS, SEQ_BLK, LANE), lambda i, h: (C_DQ // ATT_HPS + h, i, 0)),
                  pl.BlockSpec((ATT_HPS, SEQ_BLK, LANE), lambda i, h: (C_DK // ATT_HPS + h, i, 0)),
                  pl.BlockSpec((ATT_HPS, SEQ_BLK, LANE), lambda i, h: (C_DV // ATT_HPS + h, i, 0)),
                  pl.BlockSpec((None, ATT_HPS * LANE, PAST), lambda i, h: (smp(i), h, 0)),
                  pl.BlockSpec((None, None, PAST, DIFF_H, LANE), lambda i, h: (smp(i), 0, 0, 0, 0)),
                  pl.BlockSpec((T_S, LANE), lambda i, h: (0, 0)),
                  pl.BlockSpec((T_S, LANE), lambda i, h: (0, 0)),
                  pl.BlockSpec((1, LANE), lambda i, h: (0, 0)),
                  pl.BlockSpec((None, SEQ_BLK, LANE), lambda i, h: (C_RQ + h, i, 0)),
                  pl.BlockSpec((None, SEQ_BLK, LANE), lambda i, h: (C_RK + h, i, 0)),
                  pl.BlockSpec((2, SEQ_BLK, LANE), lambda i, h: (C_RV // 2 + h, i, 0)),
                  pl.BlockSpec((2, SEQ_BLK, LANE), lambda i, h: (C_RG // 2 + h, i, 0)),
                  state_in, state_in,
                  pl.BlockSpec((1, RET_VD), lambda i, h: (0, h)),
                  *side_specs],
        out_specs=(pl.BlockSpec((SEQ_BLK, ATT_HPS * LANE), lambda i, h: (i, h)),
                   pl.BlockSpec((SEQ_BLK, RET_VD), lambda i, h: (i, h)),
                   state_out, state_out,
                   *side_specs,
                   pl.BlockSpec((zrows, XW), row_slice)),
        scratch_shapes=[pltpu.VMEM((ATT_HPS, T_S, LANE), BF16),
                        pltpu.VMEM((ATT_HPS, T_S, LANE), BF16),
                        pltpu.VMEM((ATT_HPS, T_S + PAST, 2 * LANE), BF16),
                        pltpu.VMEM((RET_H, T_P, T_P), F32)],
        compiler_params=_cparams(("arbitrary", "arbitrary")),
        name="mixers",
    )(aux, proj, proj, proj, cache_k, cache_v, cos_t, sin_t, subln_g,
      proj, proj, proj, proj, s0f, s0b, gnorm, *side_weights)


def _rope_tables():
    n_rows = T_S // GRID_W
    row = np.repeat(np.arange(n_rows), GRID_W).astype(np.float64)
    col = np.tile(np.arange(GRID_W), n_rows).astype(np.float64)
    n_freq = DIFF_HD // 4
    inv = ROPE_BASE ** (-np.arange(n_freq, dtype=np.float64) / n_freq)

    def axis_tables(pos):
        ang = pos[:, None] * inv[None, :]
        c = np.cos(ang)
        s = np.sin(ang)
        return np.concatenate([c, c], axis=-1), np.concatenate([-s, s], axis=-1)

    cr, sr = axis_tables(row)
    cc, sc = axis_tables(col)
    cos_h = np.concatenate([cr, cc], axis=-1)
    sin_h = np.concatenate([sr, sc], axis=-1)
    return (jnp.asarray(np.concatenate([cos_h, cos_h], axis=-1), F32),
            jnp.asarray(np.concatenate([sin_h, sin_h], axis=-1), F32))


OP_TM = 512
OP_NPT = N_PROMPT // OP_TM


def _outproj_kernel(ra_ref, da_ref, gr_ref, gd_ref, xp_ref, xs_ref, mod_ref, n2_ref,
                    wro32_ref, wdo32_ref, wo32_ref, rgw_ref, rew_ref, brt_ref,
                    x1_ref, h2_ref, cw8_ref, info_ref, infot_ref, off_ref, pctab_ref, steps_ref,
                    m_scr, wro_ref, wdo_ref, wo_ref, wrt_ref, pc_scr, lg_scr):
    i = pl.program_id(0)

    @pl.when(i == 0)
    def _():
        wro_ref[...] = wro32_ref[...].astype(BF16)
        wdo_ref[...] = wdo32_ref[...].astype(BF16)
        wo_ref[...] = wo32_ref[...].astype(BF16)
        gap = jnp.zeros((ROUTE_E0 - N_GROUPS, D), F32)
        pad = jnp.zeros((ROUTE_W - ROUTE_E0 - N_EXP, D), F32)
        wr = jnp.concatenate([rgw_ref[...], gap, rew_ref[...], pad], axis=0)
        wrt_ref[...] = wr.astype(BF16)

        lg_scr[...] = jnp.zeros_like(lg_scr)

    def route_previous():
        cw8, info, pc, info_t = _route_cols(lg_scr[...])
        cw8_ref[...] = cw8
        info_ref[...] = info
        infot_ref[...] = info_t
        pc_scr[pl.ds(jnp.maximum(i - 1, 0), 1), :] = pc
        return cw8, info

    @pl.when(i < RT_NT)
    def _():
        routed = route_previous()
        ret_out = jnp.dot(ra_ref[...], wro_ref[...], preferred_element_type=F32)
        diff_out = jnp.dot(da_ref[...], wdo_ref[...], preferred_element_type=F32)
        for c in range(D // LANE):
            sl = slice(c * LANE, (c + 1) * LANE)
            m = (jax.nn.sigmoid(gr_ref[c].astype(F32)) * ret_out[:, sl]
                 + jax.nn.sigmoid(gd_ref[c].astype(F32)) * diff_out[:, sl])
            if c == 0:
                m = _ordered_after(_ordered_after(m, routed[0]), routed[1])
            m_scr[:, sl] = m.astype(BF16)
        mix = jnp.dot(m_scr[...], wo_ref[...], preferred_element_type=F32)
        mod = mod_ref[...]

        x1 = jnp.where(i < OP_NPT, xp_ref[...], xs_ref[...]) + mod[:, 2 * D:3 * D] * mix
        x1_ref[...] = x1
        h2 = (_rms(x1) * n2_ref[...] * (1.0 + mod[:, 4 * D:5 * D]) + mod[:, 3 * D:4 * D]).astype(BF16)
        h2_ref[...] = h2
        lg_scr[...] = lax.dot_general(wrt_ref[...], h2, (((1,), (1,)), ((), ())),
                                      preferred_element_type=F32) + brt_ref[...]

    @pl.when(i == RT_NT)
    def _():
        route_previous()
        off_ref[...], pctab_ref[...], steps_ref[...] = _dispatch_plan(pc_scr[...])


def _outproj(ret_act, diff_act, proj, xp, xs, mod3, n2, wro, wdo, wo, rgw, rew, brt):
    assert OP_TM == RT_TM
    npt = OP_NPT
    nt = N_TOK // OP_TM
    cur = lambda i: jnp.minimum(i, nt - 1)
    prev = lambda i: jnp.maximum(i - 1, 0)
    full = lambda i: (0, 0)
    once = pl.Buffered(1)
    return pl.pallas_call(
        _outproj_kernel,
        out_shape=(jax.ShapeDtypeStruct((N_TOK, D), F32),
                   jax.ShapeDtypeStruct((N_TOK, D), BF16),
                   jax.ShapeDtypeStruct((N_TOK, ROUTE_W), F32),
                   jax.ShapeDtypeStruct((N_TOK, ROUTE_W), F32),
                   jax.ShapeDtypeStruct((RT_NT, EPG, RT_TM), F32),
                   jax.ShapeDtypeStruct((RT_NT, ROUTE_W), jnp.int32),
                   jax.ShapeDtypeStruct((RT_NT, ROUTE_W), jnp.int32),
                   jax.ShapeDtypeStruct((PLAN_ROWS, ROUTE_W), jnp.int32)),
        grid=(nt + 1,),
        in_specs=[pl.BlockSpec((OP_TM, D), lambda i: (cur(i), 0)),
                  pl.BlockSpec((OP_TM, D), lambda i: (cur(i), 0)),
                  pl.BlockSpec((8, OP_TM, LANE), lambda i: (C_GR // 8, cur(i), 0)),
                  pl.BlockSpec((8, OP_TM, LANE), lambda i: (C_GD // 8, cur(i), 0)),
                  pl.BlockSpec((OP_TM, D), lambda i: (jnp.minimum(i, npt - 1), 0)),
                  pl.BlockSpec((OP_TM, D), lambda i: (jnp.maximum(cur(i) - npt, 0), 0)),
                  pl.BlockSpec((None, 1, 6 * D), lambda i: (_mod_row(OP_TM)(cur(i)), 0, 0)),
                  pl.BlockSpec((1, D), full),
                  pl.BlockSpec((D, D), full, pipeline_mode=once),
                  pl.BlockSpec((D, D), full, pipeline_mode=once),
                  pl.BlockSpec((D, D), full, pipeline_mode=once),
                  pl.BlockSpec((N_GROUPS, D), full, pipeline_mode=once),
                  pl.BlockSpec((N_EXP, D), full, pipeline_mode=once),
                  pl.BlockSpec((ROUTE_W, 1), full)],
        out_specs=(pl.BlockSpec((OP_TM, D), lambda i: (cur(i), 0)),
                   pl.BlockSpec((OP_TM, D), lambda i: (cur(i), 0)),
                   pl.BlockSpec((OP_TM, ROUTE_W), lambda i: (prev(i), 0)),
                   pl.BlockSpec((OP_TM, ROUTE_W), lambda i: (prev(i), 0)),
                   pl.BlockSpec((None, EPG, RT_TM), lambda i: (prev(i), 0, 0)),
                   pl.BlockSpec((RT_NT, ROUTE_W), full),
                   pl.BlockSpec((RT_NT, ROUTE_W), full),
                   pl.BlockSpec((PLAN_ROWS, ROUTE_W), full)),
        scratch_shapes=[pltpu.VMEM((OP_TM, D), BF16),
                        pltpu.VMEM((D, D), BF16), pltpu.VMEM((D, D), BF16), pltpu.VMEM((D, D), BF16),
                        pltpu.VMEM((ROUTE_W, D), BF16),
                        pltpu.VMEM((RT_NT, ROUTE_W), F32),
                        pltpu.VMEM((ROUTE_W, RT_TM), F32)],
        compiler_params=_cparams(("arbitrary",)),
        name="outproj",
    )(ret_act, diff_act, proj, proj, xp, xs, mod3, n2, wro, wdo, wo, rgw, rew, brt)


RT_TM = 512
RT_NT = N_TOK // RT_TM
PIECE = 16
R_LOC = RT_TM + N_GROUPS * PIECE
R_STAGE = 640
EX_TM = 512
R_MAX = 11264
EX_NT = R_MAX // EX_TM


ROUTE_E0 = EPG


def _route_cols(lgt):
    row = lax.broadcasted_iota(jnp.int32, (EPG, RT_TM), 0)
    neg = jnp.float32(-jnp.inf)

    def first_row(cond):
        return jnp.min(jnp.where(cond, row, EPG), axis=0, keepdims=True)

    head = lgt[0:EPG]
    is_g = row < N_GROUPS
    gl = jnp.where(is_g, head, neg)
    gmax = jnp.max(gl, axis=0, keepdims=True)
    gsum = jnp.sum(jnp.where(is_g, jnp.exp(head - gmax), 0.0), axis=0, keepdims=True)
    p_top = 1.0 / gsum
    g_idx = first_row(gl == gmax)
    el = lgt[ROUTE_E0 + (N_GROUPS - 1) * EPG:ROUTE_E0 + N_GROUPS * EPG]
    for g in reversed(range(N_GROUPS - 1)):
        el = jnp.where(g_idx == g, lgt[ROUTE_E0 + g * EPG:ROUTE_E0 + (g + 1) * EPG], el)
    ee = jnp.exp(el - jnp.max(el, axis=0, keepdims=True))
    ep = ee / jnp.sum(ee, axis=0, keepdims=True)
    e1 = jnp.max(ep, axis=0, keepdims=True)
    i1 = first_row(ep == e1)
    ep2 = jnp.where(row == i1, -1.0, ep)
    e2 = jnp.max(ep2, axis=0, keepdims=True)
    i2 = first_row(ep2 == e2)
    den = e1 + e2
    cw8_t = (jnp.where(row == i1, p_top * e1 / den, 0.0)
             + jnp.where(row == i2, p_top * e2 / den, 0.0))

    onehot = (row == g_idx).astype(F32)
    ii = lax.broadcasted_iota(jnp.int32, (RT_TM, RT_TM), 0)
    jj = lax.broadcasted_iota(jnp.int32, (RT_TM, RT_TM), 1)
    earlier = (ii < jj).astype(BF16)
    prefix = jnp.dot(onehot.astype(BF16), earlier, preferred_element_type=F32)
    cnt = jnp.sum(onehot, axis=1, keepdims=True)
    pc_col = jnp.floor((cnt + (PIECE - 1.0)) * (1.0 / PIECE)) * PIECE
    row1 = lax.broadcasted_iota(jnp.int32, (EPG, 1), 0)
    lane1 = lax.broadcasted_iota(jnp.int32, (1, ROUTE_W), 1)
    lo = jnp.zeros((EPG, 1), F32)
    run = jnp.zeros((1, 1), F32)
    pc = jnp.zeros((1, ROUTE_W), F32)
    for g in range(N_GROUPS):
        pc_g = jnp.sum(jnp.where(row1 == g, pc_col, 0.0), axis=0, keepdims=True)
        lo = jnp.where(row1 == g, run, lo)
        pc = jnp.where(lane1 == g, pc_g, pc)
        run = run + pc_g
    dest = jnp.sum(onehot * (prefix + lo), axis=0, keepdims=True)
    info_t = jnp.where(row == 0, g_idx.astype(F32), jnp.where(row == 1, dest, 0.0))

    slab = jnp.concatenate([cw8_t, info_t, jnp.zeros((LANE - 2 * EPG, RT_TM), F32)], axis=0)
    cols = slab.T
    lane = lax.broadcasted_iota(jnp.int32, cols.shape, 1)
    cw8 = jnp.where(lane < EPG, cols, 0.0)
    info = jnp.where(lane < 2, pltpu.roll(cols, LANE - EPG, 1), 0.0)
    return cw8, info, pc, info_t


PLAN_ROWS = 32


def _dispatch_plan(pc_all):
    lane = lax.broadcasted_iota(jnp.int32, (1, ROUTE_W), 1)
    is_g = lane < N_GROUPS
    seg_len = jnp.sum(pc_all, axis=0, keepdims=True)
    seg_pad = jnp.floor((seg_len + (EX_TM - 1.0)) * (1.0 / EX_TM)) * EX_TM
    run = seg_pad + pltpu.roll(seg_pad, 1, 1)
    seg_end_pad = jnp.where(is_g, run + pltpu.roll(run, 2, 1), 0.0)
    seg_start = seg_end_pad - seg_pad
    ti = lax.broadcasted_iota(jnp.int32, (RT_NT, RT_NT), 0)
    tj = lax.broadcasted_iota(jnp.int32, (RT_NT, RT_NT), 1)
    earlier = (tj < ti).astype(BF16)
    within = jnp.dot(earlier, pc_all.astype(BF16), preferred_element_type=F32)
    used_begin = seg_end_pad - seg_len
    chunk_off = jnp.where(is_g, used_begin + within, 0.0)

    start = (lax.broadcasted_iota(jnp.int32, (PLAN_ROWS, 1), 0) * EX_TM).astype(F32)
    passed = jnp.sum(jnp.where(jnp.logical_and(is_g, start >= seg_end_pad), 1.0, 0.0), axis=1, keepdims=True)
    group = jnp.minimum(passed, N_GROUPS - 1.0)
    begin = jnp.sum(jnp.where(lane == group.astype(jnp.int32), used_begin, 0.0), axis=1, keepdims=True)
    used = jnp.where(passed >= N_GROUPS, 0.0, jnp.clip(start + EX_TM - begin, 0.0, float(EX_TM)))
    steps = jnp.where(lane == 0, group, jnp.where(lane == 1, used, 0.0))
    return chunk_off.astype(jnp.int32), pc_all.astype(jnp.int32), steps.astype(jnp.int32)


def _piece_copies(off_ref, pc_ref, tile, make):
    lo = 0
    for g in range(N_GROUPS):
        n = pc_ref[tile, g] // PIECE
        base = off_ref[tile, g]

        def body(j, carry, lo=lo, base=base):
            make(pl.multiple_of(lo + j * PIECE, PIECE), pl.multiple_of(base + j * PIECE, PIECE))
            return carry

        lax.fori_loop(0, n, body, 0)
        lo = lo + pc_ref[tile, g]


def _piece_count(pc_ref, tile):
    n = 0
    for g in range(N_GROUPS):
        n = n + pc_ref[tile, g] // PIECE
    return n


XW = D + ROUTE_W


def _dispatch_kernel(off_ref, pc_ref, h_ref, infot_ref, cw8_ref, xs_in, xs_out, x_scr, sem):
    del xs_in
    i = pl.program_id(0)
    slot = i % 2
    dest = infot_ref[1:2, :]
    row = lax.broadcasted_iota(jnp.int32, (R_LOC, RT_TM), 0).astype(F32)
    sel = (row == dest).astype(BF16)
    cw = cw8_ref[...]
    hi = cw.astype(BF16).astype(F32)
    mid = (cw - hi).astype(BF16).astype(F32)
    low = (cw - hi - mid).astype(BF16).astype(F32)
    pieces = (hi + pltpu.roll(mid, EPG, 1) + pltpu.roll(low, 2 * EPG, 1)).astype(BF16)
    rows = jnp.concatenate([h_ref[...], pieces], axis=1)
    x_scr[slot] = jnp.dot(sel, rows, preferred_element_type=F32).astype(BF16)

    def x_copy(s, src, dst):
        return pltpu.make_async_copy(x_scr.at[s, pl.ds(src, PIECE)], xs_out.at[pl.ds(dst, PIECE)],
                                     sem.at[s])

    _piece_copies(off_ref, pc_ref, i, lambda src, dst: x_copy(slot, src, dst).start())

    def wait_tile(tile, s):
        def wait(j, carry):
            x_copy(s, 0, 0).wait()
            return carry

        lax.fori_loop(0, _piece_count(pc_ref, tile), wait, 0)

    @pl.when(i > 0)
    def _():
        wait_tile(i - 1, 1 - slot)

    @pl.when(i == RT_NT - 1)
    def _():
        wait_tile(i, slot)


def _dispatch(chunk_off, pc, h2, info_t, cw8, xs0):
    grid_spec = pltpu.PrefetchScalarGridSpec(
        num_scalar_prefetch=2,
        grid=(RT_NT,),
        in_specs=[pl.BlockSpec((RT_TM, D), lambda i, o, p: (i, 0)),
                  pl.BlockSpec((None, EPG, RT_TM), lambda i, o, p: (i, 0, 0)),
                  pl.BlockSpec((RT_TM, ROUTE_W), lambda i, o, p: (i, 0)),
                  pl.BlockSpec(memory_space=pl.ANY)],
        out_specs=pl.BlockSpec(memory_space=pl.ANY),
        scratch_shapes=[pltpu.VMEM((2, R_LOC, XW), BF16),
                        pltpu.SemaphoreType.DMA((2,))])
    return pl.pallas_call(
        _dispatch_kernel,
        out_shape=jax.ShapeDtypeStruct((R_MAX, XW), BF16),
        grid_spec=grid_spec,
        input_output_aliases={5: 0},
        compiler_params=_cparams(("arbitrary",)),
        name="dispatch",
    )(chunk_off, pc, h2, info_t, cw8, xs0)


EX_PARTS = 4


def _expert_kernel(steps_ref, x_ref, wg_ref, wu_ref, wd_ref, y_ref):
    k = pl.program_id(0)
    used = steps_ref[k, 1]
    part = EX_TM // EX_PARTS
    n_parts = (used + (part - 1)) // part

    def run(rows):
        x = x_ref[rows, 0:D]
        cw = x_ref[rows, D:XW].astype(F32)
        lane = lax.broadcasted_iota(jnp.int32, cw.shape, 1)
        acc = jnp.zeros(x.shape, F32)
        for j in range(EPG):
            mine = jnp.logical_and((lane & (EPG - 1)) == j, lane < 3 * EPG)
            w = jnp.sum(jnp.where(mine, cw, 0.0), axis=-1, keepdims=True)
            a = (jax.nn.silu(jnp.dot(x, wg_ref[j], preferred_element_type=F32))
                 * jnp.dot(x, wu_ref[j], preferred_element_type=F32))
            acc = acc + jnp.dot((a * w).astype(BF16), wd_ref[j], preferred_element_type=F32)
        y_ref[rows, :] = acc.astype(BF16)

    for p in range(1, EX_PARTS + 1):
        @pl.when(n_parts == p)
        def _(p=p):
            run(slice(EX_TM - p * part, EX_TM))
            if p < EX_PARTS:
                y_ref[0:EX_TM - p * part, :] = jnp.zeros((EX_TM - p * part, D), BF16)

    @pl.when(n_parts == 0)
    def _():
        y_ref[...] = jnp.zeros_like(y_ref)


def _experts(steps, xs, wg, wu, wd):
    grid_spec = pltpu.PrefetchScalarGridSpec(
        num_scalar_prefetch=1,
        grid=(EX_NT,),
        in_specs=[pl.BlockSpec((EX_TM, XW), lambda k, st: (k, 0)),
                  pl.BlockSpec((EPG, D, FF), lambda k, st: (st[k, 0], 0, 0)),
                  pl.BlockSpec((EPG, D, FF), lambda k, st: (st[k, 0], 0, 0)),
                  pl.BlockSpec((EPG, FF, D), lambda k, st: (st[k, 0], 0, 0))],
        out_specs=pl.BlockSpec((EX_TM, D), lambda k, st: (k, 0)))
    return pl.pallas_call(
        _expert_kernel,
        out_shape=jax.ShapeDtypeStruct((R_MAX, D), BF16),
        grid_spec=grid_spec,
        compiler_params=_cparams(("arbitrary",)),
        name="experts",
    )(steps, xs, wg, wu, wd)


CB_NPT = N_PROMPT // RT_TM


def _combine_kernel(off_ref, pc_ref, info_ref, x1_ref, mod_ref, g_ref, ys_hbm,
                    yp_ref, ysm_ref, stage, sem):
    i = pl.program_id(0)
    slot = i % 2

    def copy(s, dst, src):
        return pltpu.make_async_copy(ys_hbm.at[pl.ds(src, PIECE)], stage.at[s, pl.ds(dst, PIECE)],
                                     sem.at[s])

    def fetch(tile, s):
        _piece_copies(off_ref, pc_ref, tile, lambda loc, glob: copy(s, loc, glob).start())

    @pl.when(i == 0)
    def _():
        stage[...] = jnp.zeros_like(stage)
        fetch(0, 0)

    @pl.when(i + 1 < RT_NT)
    def _():
        fetch(i + 1, 1 - slot)

    def wait(j, carry):
        copy(slot, 0, 0).wait()
        return carry

    lax.fori_loop(0, _piece_count(pc_ref, i), wait, 0)

    dest = info_ref[...][:, 1:2]
    col = lax.broadcasted_iota(jnp.int32, (RT_TM, R_STAGE), 1).astype(F32)
    moe = jnp.dot((col == dest).astype(BF16), stage[slot], preferred_element_type=F32)
    mod = mod_ref[...]
    out = _rms(x1_ref[...] + mod[:, 5 * D:6 * D] * moe) * g_ref[...]

    @pl.when(i < CB_NPT)
    def _():
        yp_ref[...] = out

    @pl.when(i >= CB_NPT)
    def _():
        ysm_ref[...] = out


def _combine(chunk_off, pc, info, x1, mod3, fg, ys):
    npt = CB_NPT
    grid_spec = pltpu.PrefetchScalarGridSpec(
        num_scalar_prefetch=2,
        grid=(RT_NT,),
        in_specs=[pl.BlockSpec((RT_TM, ROUTE_W), lambda i, o, p: (i, 0)),
                  pl.BlockSpec((RT_TM, D), lambda i, o, p: (i, 0)),
                  pl.BlockSpec((None, 1, 6 * D), lambda i, o, p: (_mod_row(RT_TM)(i), 0, 0)),
                  pl.BlockSpec((1, D), lambda i, o, p: (0, 0)),
                  pl.BlockSpec(memory_space=pl.ANY)],
        out_specs=(pl.BlockSpec((RT_TM, D), lambda i, o, p: (jnp.minimum(i, npt - 1), 0)),
                   pl.BlockSpec((RT_TM, D), lambda i, o, p: (jnp.maximum(i - npt, 0), 0))),
        scratch_shapes=[pltpu.VMEM((2, R_STAGE, D), BF16),
                        pltpu.SemaphoreType.DMA((2,))])
    return pl.pallas_call(
        _combine_kernel,
        out_shape=(jax.ShapeDtypeStruct((N_PROMPT, D), F32),
                   jax.ShapeDtypeStruct((N_SAMPLE, D), F32)),
        grid_spec=grid_spec,
        compiler_params=_cparams(("arbitrary",)),
        name="combine",
    )(chunk_off, pc, info, x1, mod3, fg, ys)


def kernel(x_prompt, x_sample, cache_diff_k, cache_diff_v, state_ret_fwd, state_ret_bwd, c, c_ctx,
           w_ada, b_ada, norm1_g, norm2_g, w_in, ret_decay_fwd, ret_decay_bwd, ret_norm_g,
           diff_lambda_q1, diff_lambda_k1, diff_lambda_q2, diff_lambda_k2, diff_subln_g,
           w_ret_o, w_diff_o, w_o, router_group_w, router_group_b, router_expert_w, router_expert_b,
           moe_w_gate, moe_w_up, moe_w_down, final_norm_g):
    l = 0
    lam_init = 0.8 - 0.6 * math.exp(-0.3 * l)
    row = lambda a: a[l].astype(F32)[None, :]

    xp = x_prompt.reshape(N_PROMPT, D)
    xs = x_sample.reshape(N_SAMPLE, D)
    mod3, aux, brt = _modulation(
        lam_init, c_ctx[None, :], c, w_ada[l], b_ada[l][None, :],
        [row(diff_lambda_q1), row(diff_lambda_k1), row(diff_lambda_q2), row(diff_lambda_k2),
         row(ret_decay_fwd), row(ret_decay_bwd), row(router_group_b), row(router_expert_b)])

    proj, kt32, v32 = _inproj(xp, xs, mod3, norm1_g[l][None, :], w_in[l])

    cos_t, sin_t = _rope_tables()
    cache_kt = jnp.transpose(cache_diff_k[:, l], (0, 2, 3, 4, 1)).reshape(4, D, PAST)
    (diff_act, ret_act, s_f, s_b, wg_bf, wu_bf, wd_bf, xs0) = _mixers(
        aux, proj, cache_kt, cache_diff_v, cos_t, sin_t, diff_subln_g[l][None, :],
        state_ret_fwd[:, l], state_ret_bwd[:, l], ret_norm_g[l][None, :],
        (moe_w_gate[l], moe_w_up[l], moe_w_down[l]))

    x1, h2, cw8, info, info_t, chunk_off, pc, steps = _outproj(
        ret_act, diff_act, proj, xp, xs, mod3, norm2_g[l][None, :],
        w_ret_o[l], w_diff_o[l], w_o[l], router_group_w[l].T, router_expert_w[l].T, brt)
    xs_sorted = _dispatch(chunk_off, pc, h2, info_t, cw8, xs0)
    y_sorted = _experts(steps, xs_sorted, wg_bf, wu_bf, wd_bf)
    yp, ys = _combine(chunk_off, pc, info, x1, mod3, final_norm_g[None, :], y_sorted)

    return (yp.reshape(16, T_P, D), ys.reshape(4, T_S, D),
            jnp.transpose(kt32.reshape(16, DIFF_H, 2, DIFF_HD, T_P), (0, 4, 1, 2, 3))[:, None],
            v32.reshape(16, 1, T_P, DIFF_H, 2 * DIFF_HD),
            s_f.reshape(16, 1, RET_H, RET_KD, RET_VD), s_b.reshape(16, 1, RET_H, RET_KD, RET_VD))
```

```python
import functools
import math

import jax
import jax.numpy as jnp
import numpy as np
from jax import lax
from jax.experimental import pallas as pl
from jax.experimental.pallas import tpu as pltpu

F32 = jnp.float32
BF16 = jnp.bfloat16

D = 1024
N_PROMPT = 16 * 256
N_SAMPLE = 4 * 1024
N_TOK = N_PROMPT + N_SAMPLE
T_P = 256
T_S = 1024
PAST = 512
GRID_W = 64
RET_H = 4
RET_KD = 128
RET_VD = 256
DIFF_H = 8
DIFF_HD = 64
ROPE_BASE = 10000.0
N_GROUPS = 4
EPG = 8
N_EXP = 32
FF = 256
EPS = 1e-6
IN_W = 8192
LANE = 128
N_CHUNK = IN_W // LANE
C_RQ, C_RK, C_RV, C_RG, C_DQ, C_DK, C_DV, C_GR, C_GD = 0, 4, 8, 16, 24, 32, 40, 48, 56
ROUTE_W = 128
SEQ_BLK = 1024
N_PBLK = N_PROMPT // SEQ_BLK
V7X_VMEM_BYTES = 64 * 1024 * 1024
VMEM_LIMIT = V7X_VMEM_BYTES - 8 * 1024 * 1024


def _cparams(sem):
    return pltpu.CompilerParams(dimension_semantics=sem, vmem_limit_bytes=VMEM_LIMIT)


def _mod_row(tile_rows):
    def f(i):
        start = i * tile_rows
        return jnp.where(start < N_PROMPT, 0, 1 + (start - N_PROMPT) // T_S)
    return f


def _rms(x):
    return x * lax.rsqrt(jnp.mean(x * x, axis=-1, keepdims=True) + EPS)


def _ordered_after(dst, src):
    m = None
    for a in range(src.shape[0] // 8):
        for b in range(src.shape[1] // LANE):
            part = src[a * 8:(a + 1) * 8, b * LANE:(b + 1) * LANE]
            m = part if m is None else jnp.maximum(m, part)
    never = jnp.logical_and(m != m, m == m)
    return jnp.where(jnp.tile(never, (dst.shape[0] // 8, 1)), 0.0, dst)


MOD_ROWS = 8


AUX_LAM = 0
AUX_LG = 1


def _mod_kernel(lam_init, ctx_ref, c_ref, w_ref, b_ref, lq1_ref, lk1_ref, lq2_ref, lk2_ref, df_ref, db_ref,
                gb_ref, eb_ref, o_ref, aux_ref, brt_ref):
    cond = jnp.concatenate([ctx_ref[...], c_ref[...],
                            jnp.zeros((MOD_ROWS - 1 - c_ref.shape[0], D), F32)], axis=0)
    s = jax.nn.silu(cond)
    out = jnp.dot(s.astype(BF16), w_ref[...].astype(BF16), preferred_element_type=F32) + b_ref[...]
    for r in range(MOD_ROWS):
        o_ref[r] = out[r:r + 1, :]

    def lane_row(parts):
        used = sum(p.shape[1] for p in parts)
        return jnp.concatenate(parts + [jnp.zeros((1, LANE - used), F32)], axis=1)

    lam = (jnp.exp(jnp.sum(lq1_ref[...] * lk1_ref[...], axis=1, keepdims=True))
           - jnp.exp(jnp.sum(lq2_ref[...] * lk2_ref[...], axis=1, keepdims=True)) + lam_init)
    rows = [lane_row([lam, jnp.full((1, 1), 1.0 - lam_init, F32)]),
            lane_row([jax.nn.log_sigmoid(df_ref[...])]),
            lane_row([jax.nn.log_sigmoid(db_ref[...])])]
    aux_ref[...] = jnp.concatenate(rows + [jnp.zeros((8 - len(rows), LANE), F32)], axis=0)
    bias = lane_row([gb_ref[...], jnp.zeros((1, ROUTE_E0 - N_GROUPS), F32), eb_ref[...]])
    brt_ref[...] = jnp.tile(bias, (8, 1)).T[:, 0:1]


def _modulation(lam_init, c_ctx, c, w_ada, b_ada, small):
    tn = 1536
    whole = lambda a: pl.BlockSpec(a.shape, lambda j: (0, 0))
    return pl.pallas_call(
        functools.partial(_mod_kernel, lam_init),
        out_shape=(jax.ShapeDtypeStruct((MOD_ROWS, 1, 6 * D), F32),
                   jax.ShapeDtypeStruct((8, LANE), F32),
                   jax.ShapeDtypeStruct((ROUTE_W, 1), F32)),
        grid=(6 * D // tn,),
        in_specs=[pl.BlockSpec((1, D), lambda j: (0, 0)),
                  pl.BlockSpec(c.shape, lambda j: (0, 0)),
                  pl.BlockSpec((D, tn), lambda j: (0, j)),
                  pl.BlockSpec((1, tn), lambda j: (0, j))] + [whole(a) for a in small],
        out_specs=(pl.BlockSpec((MOD_ROWS, 1, tn), lambda j: (0, 0, j)),
                   pl.BlockSpec((8, LANE), lambda j: (0, 0)),
                   pl.BlockSpec((ROUTE_W, 1), lambda j: (0, 0))),
        compiler_params=_cparams(("arbitrary",)),
        name="mod",
    )(c_ctx, c, w_ada, b_ada, *small)


IP_TM = 512
IP_TN = 2048
IP_NPT = N_PROMPT // IP_TM


IP_KV_TILE = C_DK * LANE // IP_TN
IP_SPT = IP_TM // T_P


def _inproj_kernel(xp_ref, xs_ref, mod_ref, n1_ref, w_ref, proj_ref, kt_ref, v32_ref, w_scr):
    j = pl.program_id(0)
    i = pl.program_id(1)

    @pl.when(i == 0)
    def _():
        w_scr[...] = w_ref[...].astype(BF16)

    x = jnp.where(i < IP_NPT, xp_ref[...], xs_ref[...])
    mod = mod_ref[...]
    h = (_rms(x) * n1_ref[...] * (1.0 + mod[:, D:2 * D]) + mod[:, 0:D]).astype(BF16)
    acc = jnp.dot(h, w_scr[...], preferred_element_type=F32)
    for c in range(IP_TN // LANE):
        proj_ref[c] = acc[:, c * LANE:(c + 1) * LANE].astype(BF16)

    @pl.when(jnp.logical_and(j == IP_KV_TILE, i < IP_NPT))
    def _():
        for s in range(IP_SPT):
            kt_ref[s] = acc[s * T_P:(s + 1) * T_P, :D].T
        v32_ref[...] = acc[:, D:]


def _inproj(xp, xs, mod3, n1, w_in):
    npt = IP_NPT
    cpt = IP_TN // LANE

    def kv_tile(j, i):
        return jnp.where(j < IP_KV_TILE, 0, jnp.where(j == IP_KV_TILE, jnp.minimum(i, npt - 1), npt - 1))

    return pl.pallas_call(
        _inproj_kernel,
        out_shape=(jax.ShapeDtypeStruct((N_CHUNK, N_TOK, LANE), BF16),
                   jax.ShapeDtypeStruct((N_PROMPT // T_P, D, T_P), F32),
                   jax.ShapeDtypeStruct((N_PROMPT, D), F32)),
        grid=(IN_W // IP_TN, N_TOK // IP_TM),
        in_specs=[pl.BlockSpec((IP_TM, D), lambda j, i: (jnp.minimum(i, npt - 1), 0)),
                  pl.BlockSpec((IP_TM, D), lambda j, i: (jnp.maximum(i - npt, 0), 0)),
                  pl.BlockSpec((None, 1, 6 * D), lambda j, i: (_mod_row(IP_TM)(i), 0, 0)),
                  pl.BlockSpec((1, D), lambda j, i: (0, 0)),
                  pl.BlockSpec((D, IP_TN), lambda j, i: (0, j))],
        out_specs=(pl.BlockSpec((cpt, IP_TM, LANE), lambda j, i: (j, i, 0)),
                   pl.BlockSpec((IP_SPT, D, T_P), lambda j, i: (kv_tile(j, i), 0, 0)),
                   pl.BlockSpec((IP_TM, D), lambda j, i: (kv_tile(j, i), 0))),
        scratch_shapes=[pltpu.VMEM((D, IP_TN), BF16)],
        compiler_params=_cparams(("arbitrary", "arbitrary")),
        name="inproj",
    )(xp, xs, mod3, n1, w_in)


def _decay_mask(t, lgf, lgb):
    ii = lax.broadcasted_iota(jnp.int32, (t, t), 0)
    jj = lax.broadcasted_iota(jnp.int32, (t, t), 1)
    rel = (ii - jj).astype(F32)
    e = jnp.exp(jnp.where(rel >= 0.0, lgf, -lgb) * rel)
    return jnp.where(rel == 0.0, 2.0, e) * (RET_KD ** -0.5)


def _ret_parts(lgf, lgb, q_ref, k_ref, v_ref, rg_ref, s0f_ref, s0b_ref, g_ref,
               o_ref, sf_ref, sb_ref, dm_ref):
    gain = g_ref[...]
    nt = (((1,), (1,)), ((), ()))
    tn = (((0,), (0,)), ((), ()))

    def finish(o, rg):
        d = o - jnp.mean(o, axis=-1, keepdims=True)
        y = d * lax.rsqrt(jnp.mean(d * d, axis=-1, keepdims=True) + EPS) * gain
        return (jax.nn.silu(rg.astype(F32)) * y).astype(BF16)

    def vcat(r):
        return jnp.concatenate([v_ref[0, r, :], v_ref[1, r, :]], axis=1)

    def gcat(r):
        return jnp.concatenate([rg_ref[0, r, :], rg_ref[1, r, :]], axis=1)

    def build_mask():
        dm_ref[...] = _decay_mask(T_P, lgf, lgb)

    n_chunk = SEQ_BLK // T_P
    chunks = [slice(c * T_P, (c + 1) * T_P) for c in range(n_chunk)]
    t = lax.broadcasted_iota(jnp.int32, (T_P, 1), 0).astype(F32)
    kdf = jnp.exp(lgf * (T_P - 1.0 - t)) * (RET_KD ** -0.5)
    kdb = jnp.exp(lgb * t) * (RET_KD ** -0.5)

    def intra(r):
        sc = lax.dot_general(q_ref[r, :], k_ref[r, :], nt, preferred_element_type=F32)
        return jnp.dot((sc * dm_ref[...]).astype(BF16), vcat(r), preferred_element_type=F32)

    def key_state(r, kd):
        kw = (k_ref[r, :].astype(F32) * kd).astype(BF16)
        return lax.dot_general(kw, vcat(r), tn, preferred_element_type=F32)

    def context_part():
        for s, r in enumerate(chunks):
            o_ref[r, :] = finish(intra(r), gcat(r))
            sf_ref[s] = key_state(r, kdf)
            sb_ref[s] = key_state(r, kdb)

    def latent_part():
        qdf = jnp.exp(lgf * (t + 1.0))
        qdb = jnp.exp(lgb * (T_P - t))
        span = jnp.full((1, 1), float(T_P), F32)
        cf = jnp.exp(lgf * span)
        cb = jnp.exp(lgb * span)
        before_b = [None] * n_chunk
        state = s0b_ref[...]
        for c in reversed(range(n_chunk)):
            before_b[c] = state
            if c > 0:
                state = cb * state + key_state(chunks[c], kdb)
        state = s0f_ref[...]
        for c, r in enumerate(chunks):
            qf = q_ref[r, :].astype(F32)
            o = (intra(r)
                 + jnp.dot((qf * qdf).astype(BF16), state.astype(BF16), preferred_element_type=F32)
                 + jnp.dot((qf * qdb).astype(BF16), before_b[c].astype(BF16), preferred_element_type=F32))
            o_ref[r, :] = finish(o, gcat(r))
            if c + 1 < n_chunk:
                state = cf * state + key_state(r, kdf)

    return build_mask, context_part, latent_part


ATT_TQ = 1024
ATT_HPS = 2
ATT_QSCALE = (DIFF_HD ** -0.5) * math.log2(math.e)


N_SIDE = 3


def _mixer_kernel(aux_ref, q_ref, k_ref, v_ref, ckt_ref, cv_ref, cos_ref, sin_ref, g_ref,
                  rq_ref, rk_ref, rv_ref, rg_ref, s0f_ref, s0b_ref, gn_ref, *rest):
    side_in = rest[:N_SIDE]
    o_ref, ro_ref, sf_ref, sb_ref = rest[N_SIDE:N_SIDE + 4]
    side_out = rest[N_SIDE + 4:2 * N_SIDE + 4]
    xs0_ref, q_scr, k_scr, v_scr, dm_scr = rest[2 * N_SIDE + 4:]
    i = pl.program_id(0)
    rh = pl.program_id(1)
    build_mask, ret_context, ret_latent = _ret_parts(
        aux_ref[AUX_LG, rh], aux_ref[AUX_LG + 1, rh], rq_ref, rk_ref, rv_ref, rg_ref, s0f_ref, s0b_ref, gn_ref,
        ro_ref, sf_ref, sb_ref, dm_scr.at[rh])

    @pl.when(i == 0)
    def _():
        build_mask()

    def side_jobs():
        for src, dst in zip(side_in, side_out):
            dst[...] = src[...].astype(BF16)
        xs0_ref[...] = jnp.zeros_like(xs0_ref)

    lam = aux_ref[AUX_LAM, 0]
    out_scale = aux_ref[AUX_LAM, 1]
    gain = g_ref[...] * out_scale
    lane = lax.broadcasted_iota(jnp.int32, (1, LANE), 1)
    first = lane < DIFF_HD
    nt = (((1,), (1,)), ((), ()))

    def halves(q):
        zero = jnp.zeros_like(q)
        return jnp.where(first, q, zero), jnp.where(first, zero, q)

    def weights(s):
        return jnp.exp2(s - jnp.max(s, axis=-1, keepdims=True)).astype(BF16)

    def finish(of0, of1):
        o = of0[:, :LANE] / of0[:, LANE:] - lam * (of1[:, :LANE] / of1[:, LANE:])
        return (_rms(o) * gain).astype(BF16)

    @pl.when(i < N_PBLK)
    def _():
        ret_context()
        n_seq = SEQ_BLK // T_P
        rows = [slice(s * T_P, (s + 1) * T_P) for s in range(n_seq)]
        ones = jnp.ones((T_P, LANE), BF16)
        for hh in range(ATT_HPS):
            cols = slice(hh * LANE, (hh + 1) * LANE)
            q0, q1 = halves((q_ref[hh].astype(F32) * ATT_QSCALE).astype(BF16))
            s0 = jnp.concatenate([lax.dot_general(q0[r], k_ref[hh, r, :], nt, preferred_element_type=F32)
                                  for r in rows], axis=0)
            s1 = jnp.concatenate([lax.dot_general(q1[r], k_ref[hh, r, :], nt, preferred_element_type=F32)
                                  for r in rows], axis=0)
            e0 = weights(s0)
            e1 = weights(s1)
            for r in rows:
                v1 = jnp.concatenate([v_ref[hh, r, :], ones], axis=1)
                o_ref[r, cols] = finish(jnp.dot(e0[r], v1, preferred_element_type=F32),
                                        jnp.dot(e1[r], v1, preferred_element_type=F32))

    @pl.when(i >= N_PBLK)
    def _():
        side_jobs()
        ret_latent()
        cos = cos_ref[...]
        sin = sin_ref[...]
        low = (lax.broadcasted_iota(jnp.int32, (T_S, LANE), 1) & 16) == 0

        def rope(x):
            xs = jnp.where(low, pltpu.roll(x, LANE - 16, 1), pltpu.roll(x, 16, 1))
            return x * cos + xs * sin

        for hh in range(ATT_HPS):
            cols = slice(hh * LANE, (hh + 1) * LANE)
            head = pl.program_id(1) * ATT_HPS + hh
            q_scr[hh] = (rope(q_ref[hh].astype(F32)) * ATT_QSCALE).astype(BF16)
            k_scr[hh] = rope(k_ref[hh].astype(F32)).astype(BF16)
            ckt = ckt_ref[cols, :].astype(BF16)
            v_scr[hh, 0:T_S, 0:LANE] = v_ref[hh]
            v_scr[hh, T_S:T_S + PAST, 0:LANE] = cv_ref[:, head, :].astype(BF16)
            v_scr[hh, :, LANE:2 * LANE] = jnp.ones((T_S + PAST, LANE), BF16)

            def scores(qh, hh=hh, ckt=ckt):
                return jnp.concatenate([lax.dot_general(qh, k_scr[hh], nt, preferred_element_type=F32),
                                        jnp.dot(qh, ckt, preferred_element_type=F32)], axis=1)

            for b in range(T_S // ATT_TQ):
                r = slice(b * ATT_TQ, (b + 1) * ATT_TQ)
                q0, q1 = halves(q_scr[hh, r, :])
                e0 = weights(scores(q0))
                e1 = weights(scores(q1))
                o_ref[r, cols] = finish(jnp.dot(e0, v_scr[hh], preferred_element_type=F32),
                                        jnp.dot(e1, v_scr[hh], preferred_element_type=F32))


def _mixers(aux, proj, cache_k, cache_v, cos_t, sin_t, subln_g, s0f, s0b, gnorm, side_weights):
    nb = N_TOK // SEQ_BLK
    spb = SEQ_BLK // T_P
    smp = lambda i: jnp.maximum(i - N_PBLK, 0)
    pmt = lambda i: jnp.minimum(i, N_PBLK - 1)
    pmh = lambda i, h: jnp.where(i < N_PBLK, h, RET_H - 1)
    state_in = pl.BlockSpec((None, None, RET_KD, RET_VD), lambda i, h: (smp(i), h, 0, 0))
    state_out = pl.BlockSpec((spb, None, RET_KD, RET_VD), lambda i, h: (pmt(i), pmh(i, h), 0, 0))

    n_hp = DIFF_H // ATT_HPS
    assert n_hp == RET_H
    n_steps = (nb - N_PBLK) * n_hp
    step = lambda i, h: jnp.maximum((i - N_PBLK) * n_hp + h, 0)
    exp_per_step = N_EXP // n_steps
    exp_slice = lambda i, h: (step(i, h), 0, 0)
    row_slice = lambda i, h: (step(i, h), 0)
    up_spec = pl.BlockSpec((exp_per_step, D, FF), exp_slice)
    down_spec = pl.BlockSpec((exp_per_step, FF, D), exp_slice)
    side_specs = [up_spec, up_spec, down_spec]
    side_shapes = [jax.ShapeDtypeStruct(a.shape, BF16) for a in side_weights]
    zrows = R_MAX // n_steps

    return pl.pallas_call(
        _mixer_kernel,
        out_shape=(jax.ShapeDtypeStruct((N_TOK, DIFF_H * 2 * DIFF_HD), BF16),
                   jax.ShapeDtypeStruct((N_TOK, RET_H * RET_VD), BF16),
                   jax.ShapeDtypeStruct((16, RET_H, RET_KD, RET_VD), F32),
                   jax.ShapeDtypeStruct((16, RET_H, RET_KD, RET_VD), F32),
                   *side_shapes,
                   jax.ShapeDtypeStruct((R_MAX, XW), BF16)),
        grid=(nb, n_hp),
        in_specs=[pl.BlockSpec(memory_space=pltpu.SMEM),
                  pl.BlockSpec((ATT_HPS, SEQ_BLK, LANE), lambda i, h: (C_DQ // ATT_HPS + h, i, 0)),
                  pl.BlockSpec((ATT_HPS, SEQ_BLK, LANE), lambda i, h: (C_DK // ATT_HPS + h, i, 0)),
                  pl.BlockSpec((ATT_HPS, SEQ_BLK, LANE), lambda i, h: (C_DV // ATT_HPS + h, i, 0)),
                  pl.BlockSpec((None, ATT_HPS * LANE, PAST), lambda i, h: (smp(i), h, 0)),
                  pl.BlockSpec((None, None, PAST, DIFF_H, LANE), lambda i, h: (smp(i), 0, 0, 0, 0)),
                  pl.BlockSpec((T_S, LANE), lambda i, h: (0, 0)),
                  pl.BlockSpec((T_S, LANE), lambda i, h: (0, 0)),
                  pl.BlockSpec((1, LANE), lambda i, h: (0, 0)),
                  pl.BlockSpec((None, SEQ_BLK, LANE), lambda i, h: (C_RQ + h, i, 0)),
                  pl.BlockSpec((None, SEQ_BLK, LANE), lambda i, h: (C_RK + h, i, 0)),
                  pl.BlockSpec((2, SEQ_BLK, LANE), lambda i, h: (C_RV // 2 + h, i, 0)),
                  pl.BlockSpec((2, SEQ_BLK, LANE), lambda i, h: (C_RG // 2 + h, i, 0)),
                  state_in, state_in,
                  pl.BlockSpec((1, RET_VD), lambda i, h: (0, h)),
                  *side_specs],
        out_specs=(pl.BlockSpec((SEQ_BLK, ATT_HPS * LANE), lambda i, h: (i, h)),
                   pl.BlockSpec((SEQ_BLK, RET_VD), lambda i, h: (i, h)),
                   state_out, state_out,
                   *side_specs,
                   pl.BlockSpec((zrows, XW), row_slice)),
        scratch_shapes=[pltpu.VMEM((ATT_HPS, T_S, LANE), BF16),
                        pltpu.VMEM((ATT_HPS, T_S, LANE), BF16),
                        pltpu.VMEM((ATT_HPS, T_S + PAST, 2 * LANE), BF16),
                        pltpu.VMEM((RET_H, T_P, T_P), F32)],
        compiler_params=_cparams(("arbitrary", "arbitrary")),
        name="mixers",
    )(aux, proj, proj, proj, cache_k, cache_v, cos_t, sin_t, subln_g,
      proj, proj, proj, proj, s0f, s0b, gnorm, *side_weights)


def _rope_tables():
    n_rows = T_S // GRID_W
    row = np.repeat(np.arange(n_rows), GRID_W).astype(np.float64)
    col = np.tile(np.arange(GRID_W), n_rows).astype(np.float64)
    n_freq = DIFF_HD // 4
    inv = ROPE_BASE ** (-np.arange(n_freq, dtype=np.float64) / n_freq)

    def axis_tables(pos):
        ang = pos[:, None] * inv[None, :]
        c = np.cos(ang)
        s = np.sin(ang)
        return np.concatenate([c, c], axis=-1), np.concatenate([-s, s], axis=-1)

    cr, sr = axis_tables(row)
    cc, sc = axis_tables(col)
    cos_h = np.concatenate([cr, cc], axis=-1)
    sin_h = np.concatenate([sr, sc], axis=-1)
    return (jnp.asarray(np.concatenate([cos_h, cos_h], axis=-1), F32),
            jnp.asarray(np.concatenate([sin_h, sin_h], axis=-1), F32))


OP_TM = 512
OP_NPT = N_PROMPT // OP_TM


def _outproj_kernel(ra_ref, da_ref, gr_ref, gd_ref, xp_ref, xs_ref, mod_ref, n2_ref,
                    wro32_ref, wdo32_ref, wo32_ref, rgw_ref, rew_ref, brt_ref,
                    x1_ref, h2_ref, cw8_ref, info_ref, infot_ref, off_ref, pctab_ref, steps_ref,
                    m_scr, wro_ref, wdo_ref, wo_ref, wrt_ref, pc_scr, lg_scr):
    i = pl.program_id(0)

    @pl.when(i == 0)
    def _():
        wro_ref[...] = wro32_ref[...].astype(BF16)
        wdo_ref[...] = wdo32_ref[...].astype(BF16)
        wo_ref[...] = wo32_ref[...].astype(BF16)
        gap = jnp.zeros((ROUTE_E0 - N_GROUPS, D), F32)
        pad = jnp.zeros((ROUTE_W - ROUTE_E0 - N_EXP, D), F32)
        wr = jnp.concatenate([rgw_ref[...], gap, rew_ref[...], pad], axis=0)
        wrt_ref[...] = wr.astype(BF16)

        lg_scr[...] = jnp.zeros_like(lg_scr)

    def route_previous():
        cw8, info, pc, info_t = _route_cols(lg_scr[...])
        cw8_ref[...] = cw8
        info_ref[...] = info
        infot_ref[...] = info_t
        pc_scr[pl.ds(jnp.maximum(i - 1, 0), 1), :] = pc
        return cw8, info

    @pl.when(i < RT_NT)
    def _():
        routed = route_previous()
        ret_out = jnp.dot(ra_ref[...], wro_ref[...], preferred_element_type=F32)
        diff_out = jnp.dot(da_ref[...], wdo_ref[...], preferred_element_type=F32)
        for c in range(D // LANE):
            sl = slice(c * LANE, (c + 1) * LANE)
            m = (jax.nn.sigmoid(gr_ref[c].astype(F32)) * ret_out[:, sl]
                 + jax.nn.sigmoid(gd_ref[c].astype(F32)) * diff_out[:, sl])
            if c == 0:
                m = _ordered_after(_ordered_after(m, routed[0]), routed[1])
            m_scr[:, sl] = m.astype(BF16)
        mix = jnp.dot(m_scr[...], wo_ref[...], preferred_element_type=F32)
        mod = mod_ref[...]

        x1 = jnp.where(i < OP_NPT, xp_ref[...], xs_ref[...]) + mod[:, 2 * D:3 * D] * mix
        x1_ref[...] = x1
        h2 = (_rms(x1) * n2_ref[...] * (1.0 + mod[:, 4 * D:5 * D]) + mod[:, 3 * D:4 * D]).astype(BF16)
        h2_ref[...] = h2
        lg_scr[...] = lax.dot_general(wrt_ref[...], h2, (((1,), (1,)), ((), ())),
                                      preferred_element_type=F32) + brt_ref[...]

    @pl.when(i == RT_NT)
    def _():
        route_previous()
        off_ref[...], pctab_ref[...], steps_ref[...] = _dispatch_plan(pc_scr[...])


def _outproj(ret_act, diff_act, proj, xp, xs, mod3, n2, wro, wdo, wo, rgw, rew, brt):
    assert OP_TM == RT_TM
    npt = OP_NPT
    nt = N_TOK // OP_TM
    cur = lambda i: jnp.minimum(i, nt - 1)
    prev = lambda i: jnp.maximum(i - 1, 0)
    full = lambda i: (0, 0)
    once = pl.Buffered(1)
    return pl.pallas_call(
        _outproj_kernel,
        out_shape=(jax.ShapeDtypeStruct((N_TOK, D), F32),
                   jax.ShapeDtypeStruct((N_TOK, D), BF16),
                   jax.ShapeDtypeStruct((N_TOK, ROUTE_W), F32),
                   jax.ShapeDtypeStruct((N_TOK, ROUTE_W), F32),
                   jax.ShapeDtypeStruct((RT_NT, EPG, RT_TM), F32),
                   jax.ShapeDtypeStruct((RT_NT, ROUTE_W), jnp.int32),
                   jax.ShapeDtypeStruct((RT_NT, ROUTE_W), jnp.int32),
                   jax.ShapeDtypeStruct((PLAN_ROWS, ROUTE_W), jnp.int32)),
        grid=(nt + 1,),
        in_specs=[pl.BlockSpec((OP_TM, D), lambda i: (cur(i), 0)),
                  pl.BlockSpec((OP_TM, D), lambda i: (cur(i), 0)),
                  pl.BlockSpec((8, OP_TM, LANE), lambda i: (C_GR // 8, cur(i), 0)),
                  pl.BlockSpec((8, OP_TM, LANE), lambda i: (C_GD // 8, cur(i), 0)),
                  pl.BlockSpec((OP_TM, D), lambda i: (jnp.minimum(i, npt - 1), 0)),
                  pl.BlockSpec((OP_TM, D), lambda i: (jnp.maximum(cur(i) - npt, 0), 0)),
                  pl.BlockSpec((None, 1, 6 * D), lambda i: (_mod_row(OP_TM)(cur(i)), 0, 0)),
                  pl.BlockSpec((1, D), full),
                  pl.BlockSpec((D, D), full, pipeline_mode=once),
                  pl.BlockSpec((D, D), full, pipeline_mode=once),
                  pl.BlockSpec((D, D), full, pipeline_mode=once),
                  pl.BlockSpec((N_GROUPS, D), full, pipeline_mode=once),
                  pl.BlockSpec((N_EXP, D), full, pipeline_mode=once),
                  pl.BlockSpec((ROUTE_W, 1), full)],
        out_specs=(pl.BlockSpec((OP_TM, D), lambda i: (cur(i), 0)),
                   pl.BlockSpec((OP_TM, D), lambda i: (cur(i), 0)),
                   pl.BlockSpec((OP_TM, ROUTE_W), lambda i: (prev(i), 0)),
                   pl.BlockSpec((OP_TM, ROUTE_W), lambda i: (prev(i), 0)),
                   pl.BlockSpec((None, EPG, RT_TM), lambda i: (prev(i), 0, 0)),
                   pl.BlockSpec((RT_NT, ROUTE_W), full),
                   pl.BlockSpec((RT_NT, ROUTE_W), full),
                   pl.BlockSpec((PLAN_ROWS, ROUTE_W), full)),
        scratch_shapes=[pltpu.VMEM((OP_TM, D), BF16),
                        pltpu.VMEM((D, D), BF16), pltpu.VMEM((D, D), BF16), pltpu.VMEM((D, D), BF16),
                        pltpu.VMEM((ROUTE_W, D), BF16),
                        pltpu.VMEM((RT_NT, ROUTE_W), F32),
                        pltpu.VMEM((ROUTE_W, RT_TM), F32)],
        compiler_params=_cparams(("arbitrary",)),
        name="outproj",
    )(ret_act, diff_act, proj, proj, xp, xs, mod3, n2, wro, wdo, wo, rgw, rew, brt)


RT_TM = 512
RT_NT = N_TOK // RT_TM
PIECE = 16
R_LOC = RT_TM + N_GROUPS * PIECE
R_STAGE = 640
EX_TM = 512
R_MAX = 11264
EX_NT = R_MAX // EX_TM


ROUTE_E0 = EPG


def _route_cols(lgt):
    row = lax.broadcasted_iota(jnp.int32, (EPG, RT_TM), 0)
    neg = jnp.float32(-jnp.inf)

    def first_row(cond):
        return jnp.min(jnp.where(cond, row, EPG), axis=0, keepdims=True)

    head = lgt[0:EPG]
    is_g = row < N_GROUPS
    gl = jnp.where(is_g, head, neg)
    gmax = jnp.max(gl, axis=0, keepdims=True)
    gsum = jnp.sum(jnp.where(is_g, jnp.exp(head - gmax), 0.0), axis=0, keepdims=True)
    p_top = 1.0 / gsum
    g_idx = first_row(gl == gmax)
    el = lgt[ROUTE_E0 + (N_GROUPS - 1) * EPG:ROUTE_E0 + N_GROUPS * EPG]
    for g in reversed(range(N_GROUPS - 1)):
        el = jnp.where(g_idx == g, lgt[ROUTE_E0 + g * EPG:ROUTE_E0 + (g + 1) * EPG], el)
    ee = jnp.exp(el - jnp.max(el, axis=0, keepdims=True))
    ep = ee / jnp.sum(ee, axis=0, keepdims=True)
    e1 = jnp.max(ep, axis=0, keepdims=True)
    i1 = first_row(ep == e1)
    ep2 = jnp.where(row == i1, -1.0, ep)
    e2 = jnp.max(ep2, axis=0, keepdims=True)
    i2 = first_row(ep2 == e2)
    den = e1 + e2
    cw8_t = (jnp.where(row == i1, p_top * e1 / den, 0.0)
             + jnp.where(row == i2, p_top * e2 / den, 0.0))

    onehot = (row == g_idx).astype(F32)
    ii = lax.broadcasted_iota(jnp.int32, (RT_TM, RT_TM), 0)
    jj = lax.broadcasted_iota(jnp.int32, (RT_TM, RT_TM), 1)
    earlier = (ii < jj).astype(BF16)
    prefix = jnp.dot(onehot.astype(BF16), earlier, preferred_element_type=F32)
    cnt = jnp.sum(onehot, axis=1, keepdims=True)
    pc_col = jnp.floor((cnt + (PIECE - 1.0)) * (1.0 / PIECE)) * PIECE
    row1 = lax.broadcasted_iota(jnp.int32, (EPG, 1), 0)
    lane1 = lax.broadcasted_iota(jnp.int32, (1, ROUTE_W), 1)
    lo = jnp.zeros((EPG, 1), F32)
    run = jnp.zeros((1, 1), F32)
    pc = jnp.zeros((1, ROUTE_W), F32)
    for g in range(N_GROUPS):
        pc_g = jnp.sum(jnp.where(row1 == g, pc_col, 0.0), axis=0, keepdims=True)
        lo = jnp.where(row1 == g, run, lo)
        pc = jnp.where(lane1 == g, pc_g, pc)
        run = run + pc_g
    dest = jnp.sum(onehot * (prefix + lo), axis=0, keepdims=True)
    info_t = jnp.where(row == 0, g_idx.astype(F32), jnp.where(row == 1, dest, 0.0))

    slab = jnp.concatenate([cw8_t, info_t, jnp.zeros((LANE - 2 * EPG, RT_TM), F32)], axis=0)
    cols = slab.T
    lane = lax.broadcasted_iota(jnp.int32, cols.shape, 1)
    cw8 = jnp.where(lane < EPG, cols, 0.0)
    info = jnp.where(lane < 2, pltpu.roll(cols, LANE - EPG, 1), 0.0)
    return cw8, info, pc, info_t


PLAN_ROWS = 32


def _dispatch_plan(pc_all):
    lane = lax.broadcasted_iota(jnp.int32, (1, ROUTE_W), 1)
    is_g = lane < N_GROUPS
    seg_len = jnp.sum(pc_all, axis=0, keepdims=True)
    seg_pad = jnp.floor((seg_len + (EX_TM - 1.0)) * (1.0 / EX_TM)) * EX_TM
    run = seg_pad + pltpu.roll(seg_pad, 1, 1)
    seg_end_pad = jnp.where(is_g, run + pltpu.roll(run, 2, 1), 0.0)
    seg_start = seg_end_pad - seg_pad
    ti = lax.broadcasted_iota(jnp.int32, (RT_NT, RT_NT), 0)
    tj = lax.broadcasted_iota(jnp.int32, (RT_NT, RT_NT), 1)
    earlier = (tj < ti).astype(BF16)
    within = jnp.dot(earlier, pc_all.astype(BF16), preferred_element_type=F32)
    used_begin = seg_end_pad - seg_len
    chunk_off = jnp.where(is_g, used_begin + within, 0.0)

    start = (lax.broadcasted_iota(jnp.int32, (PLAN_ROWS, 1), 0) * EX_TM).astype(F32)
    passed = jnp.sum(jnp.where(jnp.logical_and(is_g, start >= seg_end_pad), 1.0, 0.0), axis=1, keepdims=True)
    group = jnp.minimum(passed, N_GROUPS - 1.0)
    begin = jnp.sum(jnp.where(lane == group.astype(jnp.int32), used_begin, 0.0), axis=1, keepdims=True)
    used = jnp.where(passed >= N_GROUPS, 0.0, jnp.clip(start + EX_TM - begin, 0.0, float(EX_TM)))
    steps = jnp.where(lane == 0, group, jnp.where(lane == 1, used, 0.0))
    return chunk_off.astype(jnp.int32), pc_all.astype(jnp.int32), steps.astype(jnp.int32)


BIG = 64


def _piece_copies(off_ref, pc_ref, tile, make):
    lo = 0
    for g in range(N_GROUPS):
        rows = pc_ref[tile, g]
        n_big = rows // BIG
        for size, first, n in ((BIG, 0, n_big), (PIECE, n_big * BIG, (rows - n_big * BIG) // PIECE)):
            def body(j, carry, size=size, lo=lo + first, base=off_ref[tile, g] + first):
                make(pl.multiple_of(lo + j * size, PIECE), pl.multiple_of(base + j * size, PIECE), size)
                return carry

            lax.fori_loop(0, n, body, 0)
        lo = lo + rows


def _piece_waits(pc_ref, tile, wait):
    n_big = 0
    n_small = 0
    for g in range(N_GROUPS):
        rows = pc_ref[tile, g]
        n_big = n_big + rows // BIG
        n_small = n_small + (rows % BIG) // PIECE
    for size, n in ((BIG, n_big), (PIECE, n_small)):
        def body(j, carry, size=size):
            wait(size)
            return carry

        lax.fori_loop(0, n, body, 0)


XW = D + ROUTE_W


def _dispatch_kernel(off_ref, pc_ref, h_ref, infot_ref, cw8_ref, xs_in, xs_out, x_scr, sem):
    del xs_in
    i = pl.program_id(0)
    slot = i % 2
    dest = infot_ref[1:2, :]
    row = lax.broadcasted_iota(jnp.int32, (R_LOC, RT_TM), 0).astype(F32)
    sel = (row == dest).astype(BF16)
    cw = cw8_ref[...]
    hi = cw.astype(BF16).astype(F32)
    mid = (cw - hi).astype(BF16).astype(F32)
    low = (cw - hi - mid).astype(BF16).astype(F32)
    pieces = (hi + pltpu.roll(mid, EPG, 1) + pltpu.roll(low, 2 * EPG, 1)).astype(BF16)
    rows = jnp.concatenate([h_ref[...], pieces], axis=1)
    x_scr[slot] = jnp.dot(sel, rows, preferred_element_type=F32).astype(BF16)

    def x_copy(s, src, dst, rows):
        return pltpu.make_async_copy(x_scr.at[s, pl.ds(src, rows)], xs_out.at[pl.ds(dst, rows)],
                                     sem.at[s])

    _piece_copies(off_ref, pc_ref, i, lambda src, dst, rows: x_copy(slot, src, dst, rows).start())

    def wait_tile(tile, s):
        _piece_waits(pc_ref, tile, lambda rows: x_copy(s, 0, 0, rows).wait())

    @pl.when(i > 0)
    def _():
        wait_tile(i - 1, 1 - slot)

    @pl.when(i == RT_NT - 1)
    def _():
        wait_tile(i, slot)


def _dispatch(chunk_off, pc, h2, info_t, cw8, xs0):
    grid_spec = pltpu.PrefetchScalarGridSpec(
        num_scalar_prefetch=2,
        grid=(RT_NT,),
        in_specs=[pl.BlockSpec((RT_TM, D), lambda i, o, p: (i, 0)),
                  pl.BlockSpec((None, EPG, RT_TM), lambda i, o, p: (i, 0, 0)),
                  pl.BlockSpec((RT_TM, ROUTE_W), lambda i, o, p: (i, 0)),
                  pl.BlockSpec(memory_space=pl.ANY)],
        out_specs=pl.BlockSpec(memory_space=pl.ANY),
        scratch_shapes=[pltpu.VMEM((2, R_LOC, XW), BF16),
                        pltpu.SemaphoreType.DMA((2,))])
    return pl.pallas_call(
        _dispatch_kernel,
        out_shape=jax.ShapeDtypeStruct((R_MAX, XW), BF16),
        grid_spec=grid_spec,
        input_output_aliases={5: 0},
        compiler_params=_cparams(("arbitrary",)),
        name="dispatch",
    )(chunk_off, pc, h2, info_t, cw8, xs0)


EX_PARTS = 4


def _expert_kernel(steps_ref, x_ref, wg_ref, wu_ref, wd_ref, y_ref):
    k = pl.program_id(0)
    used = steps_ref[k, 1]
    part = EX_TM // EX_PARTS
    n_parts = (used + (part - 1)) // part

    def run(rows):
        x = x_ref[rows, 0:D]
        cw = x_ref[rows, D:XW].astype(F32)
        lane = lax.broadcasted_iota(jnp.int32, cw.shape, 1)
        acc = jnp.zeros(x.shape, F32)
        for j in range(EPG):
            mine = jnp.logical_and((lane & (EPG - 1)) == j, lane < 3 * EPG)
            w = jnp.sum(jnp.where(mine, cw, 0.0), axis=-1, keepdims=True)
            a = (jax.nn.silu(jnp.dot(x, wg_ref[j], preferred_element_type=F32))
                 * jnp.dot(x, wu_ref[j], preferred_element_type=F32))
            acc = acc + jnp.dot((a * w).astype(BF16), wd_ref[j], preferred_element_type=F32)
        y_ref[rows, :] = acc.astype(BF16)

    for p in range(1, EX_PARTS + 1):
        @pl.when(n_parts == p)
        def _(p=p):
            run(slice(EX_TM - p * part, EX_TM))
            if p < EX_PARTS:
                y_ref[0:EX_TM - p * part, :] = jnp.zeros((EX_TM - p * part, D), BF16)

    @pl.when(n_parts == 0)
    def _():
        y_ref[...] = jnp.zeros_like(y_ref)


def _experts(steps, xs, wg, wu, wd):
    grid_spec = pltpu.PrefetchScalarGridSpec(
        num_scalar_prefetch=1,
        grid=(EX_NT,),
        in_specs=[pl.BlockSpec((EX_TM, XW), lambda k, st: (k, 0)),
                  pl.BlockSpec((EPG, D, FF), lambda k, st: (st[k, 0], 0, 0)),
                  pl.BlockSpec((EPG, D, FF), lambda k, st: (st[k, 0], 0, 0)),
                  pl.BlockSpec((EPG, FF, D), lambda k, st: (st[k, 0], 0, 0))],
        out_specs=pl.BlockSpec((EX_TM, D), lambda k, st: (k, 0)))
    return pl.pallas_call(
        _expert_kernel,
        out_shape=jax.ShapeDtypeStruct((R_MAX, D), BF16),
        grid_spec=grid_spec,
        compiler_params=_cparams(("arbitrary",)),
        name="experts",
    )(steps, xs, wg, wu, wd)


CB_NPT = N_PROMPT // RT_TM


def _combine_kernel(off_ref, pc_ref, info_ref, x1_ref, mod_ref, g_ref, ys_hbm,
                    yp_ref, ysm_ref, stage, sem):
    i = pl.program_id(0)
    slot = i % 2

    def copy(s, dst, src, rows):
        return pltpu.make_async_copy(ys_hbm.at[pl.ds(src, rows)], stage.at[s, pl.ds(dst, rows)],
                                     sem.at[s])

    def fetch(tile, s):
        _piece_copies(off_ref, pc_ref, tile, lambda loc, glob, rows: copy(s, loc, glob, rows).start())

    @pl.when(i == 0)
    def _():
        stage[...] = jnp.zeros_like(stage)
        fetch(0, 0)

    @pl.when(i + 1 < RT_NT)
    def _():
        fetch(i + 1, 1 - slot)

    _piece_waits(pc_ref, i, lambda rows: copy(slot, 0, 0, rows).wait())

    dest = info_ref[...][:, 1:2]
    col = lax.broadcasted_iota(jnp.int32, (RT_TM, R_STAGE), 1).astype(F32)
    moe = jnp.dot((col == dest).astype(BF16), stage[slot], preferred_element_type=F32)
    mod = mod_ref[...]
    out = _rms(x1_ref[...] + mod[:, 5 * D:6 * D] * moe) * g_ref[...]

    @pl.when(i < CB_NPT)
    def _():
        yp_ref[...] = out

    @pl.when(i >= CB_NPT)
    def _():
        ysm_ref[...] = out


def _combine(chunk_off, pc, info, x1, mod3, fg, ys):
    npt = CB_NPT
    grid_spec = pltpu.PrefetchScalarGridSpec(
        num_scalar_prefetch=2,
        grid=(RT_NT,),
        in_specs=[pl.BlockSpec((RT_TM, ROUTE_W), lambda i, o, p: (i, 0)),
                  pl.BlockSpec((RT_TM, D), lambda i, o, p: (i, 0)),
                  pl.BlockSpec((None, 1, 6 * D), lambda i, o, p: (_mod_row(RT_TM)(i), 0, 0)),
                  pl.BlockSpec((1, D), lambda i, o, p: (0, 0)),
                  pl.BlockSpec(memory_space=pl.ANY)],
        out_specs=(pl.BlockSpec((RT_TM, D), lambda i, o, p: (jnp.minimum(i, npt - 1), 0)),
                   pl.BlockSpec((RT_TM, D), lambda i, o, p: (jnp.maximum(i - npt, 0), 0))),
        scratch_shapes=[pltpu.VMEM((2, R_STAGE, D), BF16),
                        pltpu.SemaphoreType.DMA((2,))])
    return pl.pallas_call(
        _combine_kernel,
        out_shape=(jax.ShapeDtypeStruct((N_PROMPT, D), F32),
                   jax.ShapeDtypeStruct((N_SAMPLE, D), F32)),
        grid_spec=grid_spec,
        compiler_params=_cparams(("arbitrary",)),
        name="combine",
    )(chunk_off, pc, info, x1, mod3, fg, ys)


def kernel(x_prompt, x_sample, cache_diff_k, cache_diff_v, state_ret_fwd, state_ret_bwd, c, c_ctx,
           w_ada, b_ada, norm1_g, norm2_g, w_in, ret_decay_fwd, ret_decay_bwd, ret_norm_g,
           diff_lambda_q1, diff_lambda_k1, diff_lambda_q2, diff_lambda_k2, diff_subln_g,
           w_ret_o, w_diff_o, w_o, router_group_w, router_group_b, router_expert_w, router_expert_b,
           moe_w_gate, moe_w_up, moe_w_down, final_norm_g):
    l = 0
    lam_init = 0.8 - 0.6 * math.exp(-0.3 * l)
    row = lambda a: a[l].astype(F32)[None, :]

    xp = x_prompt.reshape(N_PROMPT, D)
    xs = x_sample.reshape(N_SAMPLE, D)
    mod3, aux, brt = _modulation(
        lam_init, c_ctx[None, :], c, w_ada[l], b_ada[l][None, :],
        [row(diff_lambda_q1), row(diff_lambda_k1), row(diff_lambda_q2), row(diff_lambda_k2),
         row(ret_decay_fwd), row(ret_decay_bwd), row(router_group_b), row(router_expert_b)])

    proj, kt32, v32 = _inproj(xp, xs, mod3, norm1_g[l][None, :], w_in[l])

    cos_t, sin_t = _rope_tables()
    cache_kt = jnp.transpose(cache_diff_k[:, l], (0, 2, 3, 4, 1)).reshape(4, D, PAST)
    (diff_act, ret_act, s_f, s_b, wg_bf, wu_bf, wd_bf, xs0) = _mixers(
        aux, proj, cache_kt, cache_diff_v, cos_t, sin_t, diff_subln_g[l][None, :],
        state_ret_fwd[:, l], state_ret_bwd[:, l], ret_norm_g[l][None, :],
        (moe_w_gate[l], moe_w_up[l], moe_w_down[l]))

    x1, h2, cw8, info, info_t, chunk_off, pc, steps = _outproj(
        ret_act, diff_act, proj, xp, xs, mod3, norm2_g[l][None, :],
        w_ret_o[l], w_diff_o[l], w_o[l], router_group_w[l].T, router_expert_w[l].T, brt)
    xs_sorted = _dispatch(chunk_off, pc, h2, info_t, cw8, xs0)
    y_sorted = _experts(steps, xs_sorted, wg_bf, wu_bf, wd_bf)
    yp, ys = _combine(chunk_off, pc, info, x1, mod3, final_norm_g[None, :], y_sorted)

    return (yp.reshape(16, T_P, D), ys.reshape(4, T_S, D),
            jnp.transpose(kt32.reshape(16, DIFF_H, 2, DIFF_HD, T_P), (0, 4, 1, 2, 3))[:, None],
            v32.reshape(16, 1, T_P, DIFF_H, 2 * DIFF_HD),
            s_f.reshape(16, 1, RET_H, RET_KD, RET_VD), s_b.reshape(16, 1, RET_H, RET_KD, RET_VD))
```

```python
import functools
import math

import jax
import jax.numpy as jnp
import numpy as np
from jax import lax
from jax.experimental import pallas as pl
from jax.experimental.pallas import tpu as pltpu

F32 = jnp.float32
BF16 = jnp.bfloat16

D = 1024
N_PROMPT = 16 * 256
N_SAMPLE = 4 * 1024
N_TOK = N_PROMPT + N_SAMPLE
T_P = 256
T_S = 1024
PAST = 512
GRID_W = 64
RET_H = 4
RET_KD = 128
RET_VD = 256
DIFF_H = 8
DIFF_HD = 64
ROPE_BASE = 10000.0
N_GROUPS = 4
EPG = 8
N_EXP = 32
FF = 256
EPS = 1e-6
IN_W = 8192
LANE = 128
N_CHUNK = IN_W // LANE
C_RQ, C_RK, C_RV, C_RG, C_DQ, C_DK, C_DV, C_GR, C_GD = 0, 4, 8, 16, 24, 32, 40, 48, 56
ROUTE_W = 128
SEQ_BLK = 1024
N_PBLK = N_PROMPT // SEQ_BLK
V7X_VMEM_BYTES = 64 * 1024 * 1024
VMEM_LIMIT = V7X_VMEM_BYTES - 8 * 1024 * 1024


def _cparams(sem):
    return pltpu.CompilerParams(dimension_semantics=sem, vmem_limit_bytes=VMEM_LIMIT)


def _mod_row(tile_rows):
    def f(i):
        start = i * tile_rows
        return jnp.where(start < N_PROMPT, 0, 1 + (start - N_PROMPT) // T_S)
    return f


def _rms(x):
    return x * lax.rsqrt(jnp.mean(x * x, axis=-1, keepdims=True) + EPS)


def _ordered_after(dst, src):
    m = None
    for a in range(src.shape[0] // 8):
        for b in range(src.shape[1] // LANE):
            part = src[a * 8:(a + 1) * 8, b * LANE:(b + 1) * LANE]
            m = part if m is None else jnp.maximum(m, part)
    never = jnp.logical_and(m != m, m == m)
    return jnp.where(jnp.tile(never, (dst.shape[0] // 8, 1)), 0.0, dst)


MOD_ROWS = 8


AUX_LAM = 0
AUX_LG = 1


def _mod_kernel(lam_init, ctx_ref, c_ref, w_ref, b_ref, lq1_ref, lk1_ref, lq2_ref, lk2_ref, df_ref, db_ref,
                gb_ref, eb_ref, o_ref, aux_ref, brt_ref):
    cond = jnp.concatenate([ctx_ref[...], c_ref[...],
                            jnp.zeros((MOD_ROWS - 1 - c_ref.shape[0], D), F32)], axis=0)
    s = jax.nn.silu(cond)
    out = jnp.dot(s.astype(BF16), w_ref[...].astype(BF16), preferred_element_type=F32) + b_ref[...]
    for r in range(MOD_ROWS):
        o_ref[r] = out[r:r + 1, :]

    def lane_row(parts):
        used = sum(p.shape[1] for p in parts)
        return jnp.concatenate(parts + [jnp.zeros((1, LANE - used), F32)], axis=1)

    lam = (jnp.exp(jnp.sum(lq1_ref[...] * lk1_ref[...], axis=1, keepdims=True))
           - jnp.exp(jnp.sum(lq2_ref[...] * lk2_ref[...], axis=1, keepdims=True)) + lam_init)
    rows = [lane_row([lam, jnp.full((1, 1), 1.0 - lam_init, F32)]),
            lane_row([jax.nn.log_sigmoid(df_ref[...])]),
            lane_row([jax.nn.log_sigmoid(db_ref[...])])]
    aux_ref[...] = jnp.concatenate(rows + [jnp.zeros((8 - len(rows), LANE), F32)], axis=0)
    bias = lane_row([gb_ref[...], jnp.zeros((1, ROUTE_E0 - N_GROUPS), F32), eb_ref[...]])
    brt_ref[...] = jnp.tile(bias, (8, 1)).T[:, 0:1]


def _modulation(lam_init, c_ctx, c, w_ada, b_ada, small):
    tn = 1536
    whole = lambda a: pl.BlockSpec(a.shape, lambda j: (0, 0))
    return pl.pallas_call(
        functools.partial(_mod_kernel, lam_init),
        out_shape=(jax.ShapeDtypeStruct((MOD_ROWS, 1, 6 * D), F32),
                   jax.ShapeDtypeStruct((8, LANE), F32),
                   jax.ShapeDtypeStruct((ROUTE_W, 1), F32)),
        grid=(6 * D // tn,),
        in_specs=[pl.BlockSpec((1, D), lambda j: (0, 0)),
                  pl.BlockSpec(c.shape, lambda j: (0, 0)),
                  pl.BlockSpec((D, tn), lambda j: (0, j)),
                  pl.BlockSpec((1, tn), lambda j: (0, j))] + [whole(a) for a in small],
        out_specs=(pl.BlockSpec((MOD_ROWS, 1, tn), lambda j: (0, 0, j)),
                   pl.BlockSpec((8, LANE), lambda j: (0, 0)),
                   pl.BlockSpec((ROUTE_W, 1), lambda j: (0, 0))),
        compiler_params=_cparams(("arbitrary",)),
        name="mod",
    )(c_ctx, c, w_ada, b_ada, *small)


IP_TM = 512
IP_TN = 2048
IP_NPT = N_PROMPT // IP_TM


IP_KV_TILE = C_DK * LANE // IP_TN
IP_SPT = IP_TM // T_P


def _inproj_kernel(xp_ref, xs_ref, mod_ref, n1_ref, w_ref, proj_ref, kt_ref, v32_ref, w_scr):
    j = pl.program_id(0)
    i = pl.program_id(1)

    @pl.when(i == 0)
    def _():
        w_scr[...] = w_ref[...].astype(BF16)

    x = jnp.where(i < IP_NPT, xp_ref[...], xs_ref[...])
    mod = mod_ref[...]
    h = (_rms(x) * n1_ref[...] * (1.0 + mod[:, D:2 * D]) + mod[:, 0:D]).astype(BF16)
    acc = jnp.dot(h, w_scr[...], preferred_element_type=F32)
    for c in range(IP_TN // LANE):
        proj_ref[c] = acc[:, c * LANE:(c + 1) * LANE].astype(BF16)

    @pl.when(jnp.logical_and(j == IP_KV_TILE, i < IP_NPT))
    def _():
        for s in range(IP_SPT):
            kt_ref[s] = acc[s * T_P:(s + 1) * T_P, :D].T
        v32_ref[...] = acc[:, D:]


def _inproj(xp, xs, mod3, n1, w_in):
    npt = IP_NPT
    cpt = IP_TN // LANE

    def kv_tile(j, i):
        return jnp.where(j < IP_KV_TILE, 0, jnp.where(j == IP_KV_TILE, jnp.minimum(i, npt - 1), npt - 1))

    return pl.pallas_call(
        _inproj_kernel,
        out_shape=(jax.ShapeDtypeStruct((N_CHUNK, N_TOK, LANE), BF16),
                   jax.ShapeDtypeStruct((N_PROMPT // T_P, D, T_P), F32),
                   jax.ShapeDtypeStruct((N_PROMPT, D), F32)),
        grid=(IN_W // IP_TN, N_TOK // IP_TM),
        in_specs=[pl.BlockSpec((IP_TM, D), lambda j, i: (jnp.minimum(i, npt - 1), 0)),
                  pl.BlockSpec((IP_TM, D), lambda j, i: (jnp.maximum(i - npt, 0), 0)),
                  pl.BlockSpec((None, 1, 6 * D), lambda j, i: (_mod_row(IP_TM)(i), 0, 0)),
                  pl.BlockSpec((1, D), lambda j, i: (0, 0)),
                  pl.BlockSpec((D, IP_TN), lambda j, i: (0, j))],
        out_specs=(pl.BlockSpec((cpt, IP_TM, LANE), lambda j, i: (j, i, 0)),
                   pl.BlockSpec((IP_SPT, D, T_P), lambda j, i: (kv_tile(j, i), 0, 0)),
                   pl.BlockSpec((IP_TM, D), lambda j, i: (kv_tile(j, i), 0))),
        scratch_shapes=[pltpu.VMEM((D, IP_TN), BF16)],
        compiler_params=_cparams(("arbitrary", "arbitrary")),
        name="inproj",
    )(xp, xs, mod3, n1, w_in)


def _decay_mask(t, lgf, lgb):
    ii = lax.broadcasted_iota(jnp.int32, (t, t), 0)
    jj = lax.broadcasted_iota(jnp.int32, (t, t), 1)
    rel = (ii - jj).astype(F32)
    e = jnp.exp(jnp.where(rel >= 0.0, lgf, -lgb) * rel)
    return jnp.where(rel == 0.0, 2.0, e) * (RET_KD ** -0.5)


def _ret_parts(lgf, lgb, q_ref, k_ref, v_ref, rg_ref, s0f_ref, s0b_ref, g_ref,
               o_ref, sf_ref, sb_ref, dm_ref):
    gain = g_ref[...]
    nt = (((1,), (1,)), ((), ()))
    tn = (((0,), (0,)), ((), ()))

    def finish(o, rg):
        d = o - jnp.mean(o, axis=-1, keepdims=True)
        y = d * lax.rsqrt(jnp.mean(d * d, axis=-1, keepdims=True) + EPS) * gain
        return (jax.nn.silu(rg.astype(F32)) * y).astype(BF16)

    def vcat(r):
        return jnp.concatenate([v_ref[0, r, :], v_ref[1, r, :]], axis=1)

    def gcat(r):
        return jnp.concatenate([rg_ref[0, r, :], rg_ref[1, r, :]], axis=1)

    def build_mask():
        dm_ref[...] = _decay_mask(T_P, lgf, lgb)

    n_chunk = SEQ_BLK // T_P
    chunks = [slice(c * T_P, (c + 1) * T_P) for c in range(n_chunk)]
    t = lax.broadcasted_iota(jnp.int32, (T_P, 1), 0).astype(F32)
    kdf = jnp.exp(lgf * (T_P - 1.0 - t)) * (RET_KD ** -0.5)
    kdb = jnp.exp(lgb * t) * (RET_KD ** -0.5)

    def intra(r):
        sc = lax.dot_general(q_ref[r, :], k_ref[r, :], nt, preferred_element_type=F32)
        return jnp.dot((sc * dm_ref[...]).astype(BF16), vcat(r), preferred_element_type=F32)

    def key_state(r, kd):
        kw = (k_ref[r, :].astype(F32) * kd).astype(BF16)
        return lax.dot_general(kw, vcat(r), tn, preferred_element_type=F32)

    def context_part():
        for s, r in enumerate(chunks):
            o_ref[r, :] = finish(intra(r), gcat(r))
            sf_ref[s] = key_state(r, kdf)
            sb_ref[s] = key_state(r, kdb)

    def latent_part():
        qdf = jnp.exp(lgf * (t + 1.0))
        qdb = jnp.exp(lgb * (T_P - t))
        span = jnp.full((1, 1), float(T_P), F32)
        cf = jnp.exp(lgf * span)
        cb = jnp.exp(lgb * span)
        before_b = [None] * n_chunk
        state = s0b_ref[...]
        for c in reversed(range(n_chunk)):
            before_b[c] = state
            if c > 0:
                state = cb * state + key_state(chunks[c], kdb)
        state = s0f_ref[...]
        for c, r in enumerate(chunks):
            qf = q_ref[r, :].astype(F32)
            o = (intra(r)
                 + jnp.dot((qf * qdf).astype(BF16), state.astype(BF16), preferred_element_type=F32)
                 + jnp.dot((qf * qdb).astype(BF16), before_b[c].astype(BF16), preferred_element_type=F32))
            o_ref[r, :] = finish(o, gcat(r))
            if c + 1 < n_chunk:
                state = cf * state + key_state(r, kdf)

    return build_mask, context_part, latent_part


ATT_TQ = 1024
ATT_HPS = 2
ATT_QSCALE = (DIFF_HD ** -0.5) * math.log2(math.e)


N_SIDE = 3


def _mixer_kernel(aux_ref, q_ref, k_ref, v_ref, ckt_ref, cv_ref, cos_ref, sin_ref, g_ref,
                  rq_ref, rk_ref, rv_ref, rg_ref, s0f_ref, s0b_ref, gn_ref, *rest):
    side_in = rest[:N_SIDE]
    o_ref, ro_ref, sf_ref, sb_ref = rest[N_SIDE:N_SIDE + 4]
    side_out = rest[N_SIDE + 4:2 * N_SIDE + 4]
    xs0_ref, q_scr, k_scr, v_scr, dm_scr = rest[2 * N_SIDE + 4:]
    i = pl.program_id(0)
    rh = pl.program_id(1)
    build_mask, ret_context, ret_latent = _ret_parts(
        aux_ref[AUX_LG, rh], aux_ref[AUX_LG + 1, rh], rq_ref, rk_ref, rv_ref, rg_ref, s0f_ref, s0b_ref, gn_ref,
        ro_ref, sf_ref, sb_ref, dm_scr.at[rh])

    @pl.when(i == 0)
    def _():
        build_mask()

    def side_jobs():
        for src, dst in zip(side_in, side_out):
            dst[...] = src[...].astype(BF16)
        xs0_ref[...] = jnp.zeros_like(xs0_ref)

    lam = aux_ref[AUX_LAM, 0]
    out_scale = aux_ref[AUX_LAM, 1]
    gain = g_ref[...] * out_scale
    lane = lax.broadcasted_iota(jnp.int32, (1, LANE), 1)
    first = lane < DIFF_HD
    nt = (((1,), (1,)), ((), ()))

    def halves(q):
        zero = jnp.zeros_like(q)
        return jnp.where(first, q, zero), jnp.where(first, zero, q)

    def weights(s):
        return jnp.exp2(s - jnp.max(s, axis=-1, keepdims=True)).astype(BF16)

    def finish(of0, of1):
        o = of0[:, :LANE] / of0[:, LANE:] - lam * (of1[:, :LANE] / of1[:, LANE:])
        return (_rms(o) * gain).astype(BF16)

    @pl.when(i < N_PBLK)
    def _():
        ret_context()
        n_seq = SEQ_BLK // T_P
        rows = [slice(s * T_P, (s + 1) * T_P) for s in range(n_seq)]
        ones = jnp.ones((T_P, LANE), BF16)
        for hh in range(ATT_HPS):
            cols = slice(hh * LANE, (hh + 1) * LANE)
            q0, q1 = halves((q_ref[hh].astype(F32) * ATT_QSCALE).astype(BF16))
            s0 = jnp.concatenate([lax.dot_general(q0[r], k_ref[hh, r, :], nt, preferred_element_type=F32)
                                  for r in rows], axis=0)
            s1 = jnp.concatenate([lax.dot_general(q1[r], k_ref[hh, r, :], nt, preferred_element_type=F32)
                                  for r in rows], axis=0)
            e0 = weights(s0)
            e1 = weights(s1)
            for r in rows:
                v1 = jnp.concatenate([v_ref[hh, r, :], ones], axis=1)
                o_ref[r, cols] = finish(jnp.dot(e0[r], v1, preferred_element_type=F32),
                                        jnp.dot(e1[r], v1, preferred_element_type=F32))

    @pl.when(i >= N_PBLK)
    def _():
        side_jobs()
        ret_latent()
        cos = cos_ref[...]
        sin = sin_ref[...]
        low = (lax.broadcasted_iota(jnp.int32, (T_S, LANE), 1) & 16) == 0

        def rope(x):
            xs = jnp.where(low, pltpu.roll(x, LANE - 16, 1), pltpu.roll(x, 16, 1))
            return x * cos + xs * sin

        for hh in range(ATT_HPS):
            cols = slice(hh * LANE, (hh + 1) * LANE)
            head = pl.program_id(1) * ATT_HPS + hh
            q_scr[hh] = (rope(q_ref[hh].astype(F32)) * ATT_QSCALE).astype(BF16)
            k_scr[hh] = rope(k_ref[hh].astype(F32)).astype(BF16)
            ckt = ckt_ref[cols, :].astype(BF16)
            v_scr[hh, 0:T_S, 0:LANE] = v_ref[hh]
            v_scr[hh, T_S:T_S + PAST, 0:LANE] = cv_ref[:, head, :].astype(BF16)
            v_scr[hh, :, LANE:2 * LANE] = jnp.ones((T_S + PAST, LANE), BF16)

            def scores(qh, hh=hh, ckt=ckt):
                return jnp.concatenate([lax.dot_general(qh, k_scr[hh], nt, preferred_element_type=F32),
                                        jnp.dot(qh, ckt, preferred_element_type=F32)], axis=1)

            for b in range(T_S // ATT_TQ):
                r = slice(b * ATT_TQ, (b + 1) * ATT_TQ)
                q0, q1 = halves(q_scr[hh, r, :])
                e0 = weights(scores(q0))
                e1 = weights(scores(q1))
                o_ref[r, cols] = finish(jnp.dot(e0, v_scr[hh], preferred_element_type=F32),
                                        jnp.dot(e1, v_scr[hh], preferred_element_type=F32))


def _mixers(aux, proj, cache_k, cache_v, cos_t, sin_t, subln_g, s0f, s0b, gnorm, side_weights):
    nb = N_TOK // SEQ_BLK
    spb = SEQ_BLK // T_P
    smp = lambda i: jnp.maximum(i - N_PBLK, 0)
    pmt = lambda i: jnp.minimum(i, N_PBLK - 1)
    pmh = lambda i, h: jnp.where(i < N_PBLK, h, RET_H - 1)
    state_in = pl.BlockSpec((None, None, RET_KD, RET_VD), lambda i, h: (smp(i), h, 0, 0))
    state_out = pl.BlockSpec((spb, None, RET_KD, RET_VD), lambda i, h: (pmt(i), pmh(i, h), 0, 0))

    n_hp = DIFF_H // ATT_HPS
    assert n_hp == RET_H
    n_steps = (nb - N_PBLK) * n_hp
    step = lambda i, h: jnp.maximum((i - N_PBLK) * n_hp + h, 0)
    exp_per_step = N_EXP // n_steps
    exp_slice = lambda i, h: (step(i, h), 0, 0)
    row_slice = lambda i, h: (step(i, h), 0)
    up_spec = pl.BlockSpec((exp_per_step, D, FF), exp_slice)
    down_spec = pl.BlockSpec((exp_per_step, FF, D), exp_slice)
    side_specs = [up_spec, up_spec, down_spec]
    side_shapes = [jax.ShapeDtypeStruct(a.shape, BF16) for a in side_weights]
    zrows = R_MAX // n_steps

    return pl.pallas_call(
        _mixer_kernel,
        out_shape=(jax.ShapeDtypeStruct((N_TOK, DIFF_H * 2 * DIFF_HD), BF16),
                   jax.ShapeDtypeStruct((N_TOK, RET_H * RET_VD), BF16),
                   jax.ShapeDtypeStruct((16, RET_H, RET_KD, RET_VD), F32),
                   jax.ShapeDtypeStruct((16, RET_H, RET_KD, RET_VD), F32),
                   *side_shapes,
                   jax.ShapeDtypeStruct((R_MAX, XW), BF16)),
        grid=(nb, n_hp),
        in_specs=[pl.BlockSpec(memory_space=pltpu.SMEM),
                  pl.BlockSpec((ATT_HPS, SEQ_BLK, LANE), lambda i, h: (C_DQ // ATT_HPS + h, i, 0)),
                  pl.BlockSpec((ATT_HPS, SEQ_BLK, LANE), lambda i, h: (C_DK // ATT_HPS + h, i, 0)),
                  pl.BlockSpec((ATT_HPS, SEQ_BLK, LANE), lambda i, h: (C_DV // ATT_HPS + h, i, 0)),
                  pl.BlockSpec((None, ATT_HPS * LANE, PAST), lambda i, h: (smp(i), h, 0)),
                  pl.BlockSpec((None, None, PAST, DIFF_H, LANE), lambda i, h: (smp(i), 0, 0, 0, 0)),
                  pl.BlockSpec((T_S, LANE), lambda i, h: (0, 0)),
                  pl.BlockSpec((T_S, LANE), lambda i, h: (0, 0)),
                  pl.BlockSpec((1, LANE), lambda i, h: (0, 0)),
                  pl.BlockSpec((None, SEQ_BLK, LANE), lambda i, h: (C_RQ + h, i, 0)),
                  pl.BlockSpec((None, SEQ_BLK, LANE), lambda i, h: (C_RK + h, i, 0)),
                  pl.BlockSpec((2, SEQ_BLK, LANE), lambda i, h: (C_RV // 2 + h, i, 0)),
                  pl.BlockSpec((2, SEQ_BLK, LANE), lambda i, h: (C_RG // 2 + h, i, 0)),
                  state_in, state_in,
                  pl.BlockSpec((1, RET_VD), lambda i, h: (0, h)),
                  *side_specs],
        out_specs=(pl.BlockSpec((SEQ_BLK, ATT_HPS * LANE), lambda i, h: (i, h)),
                   pl.BlockSpec((SEQ_BLK, RET_VD), lambda i, h: (i, h)),
                   state_out, state_out,
                   *side_specs,
                   pl.BlockSpec((zrows, XW), row_slice)),
        scratch_shapes=[pltpu.VMEM((ATT_HPS, T_S, LANE), BF16),
                        pltpu.VMEM((ATT_HPS, T_S, LANE), BF16),
                        pltpu.VMEM((ATT_HPS, T_S + PAST, 2 * LANE), BF16),
                        pltpu.VMEM((RET_H, T_P, T_P), F32)],
        compiler_params=_cparams(("arbitrary", "arbitrary")),
        name="mixers",
    )(aux, proj, proj, proj, cache_k, cache_v, cos_t, sin_t, subln_g,
      proj, proj, proj, proj, s0f, s0b, gnorm, *side_weights)


def _rope_tables():
    n_rows = T_S // GRID_W
    row = np.repeat(np.arange(n_rows), GRID_W).astype(np.float64)
    col = np.tile(np.arange(GRID_W), n_rows).astype(np.float64)
    n_freq = DIFF_HD // 4
    inv = ROPE_BASE ** (-np.arange(n_freq, dtype=np.float64) / n_freq)

    def axis_tables(pos):
        ang = pos[:, None] * inv[None, :]
        c = np.cos(ang)
        s = np.sin(ang)
        return np.concatenate([c, c], axis=-1), np.concatenate([-s, s], axis=-1)

    cr, sr = axis_tables(row)
    cc, sc = axis_tables(col)
    cos_h = np.concatenate([cr, cc], axis=-1)
    sin_h = np.concatenate([sr, sc], axis=-1)
    return (jnp.asarray(np.concatenate([cos_h, cos_h], axis=-1), F32),
            jnp.asarray(np.concatenate([sin_h, sin_h], axis=-1), F32))


OP_TM = 512
OP_NPT = N_PROMPT // OP_TM


def _outproj_kernel(ra_ref, da_ref, gr_ref, gd_ref, xp_ref, xs_ref, mod_ref, n2_ref,
                    wro32_ref, wdo32_ref, wo32_ref, rgw_ref, rew_ref, brt_ref,
                    x1_ref, h2_ref, cw8_ref, info_ref, infot_ref, off_ref, pctab_ref, steps_ref,
                    m_scr, wro_ref, wdo_ref, wo_ref, wrt_ref, pc_scr, lg_scr):
    i = pl.program_id(0)

    @pl.when(i == 0)
    def _():
        wro_ref[...] = wro32_ref[...].astype(BF16)
        wdo_ref[...] = wdo32_ref[...].astype(BF16)
        wo_ref[...] = wo32_ref[...].astype(BF16)
        gap = jnp.zeros((ROUTE_E0 - N_GROUPS, D), F32)
        pad = jnp.zeros((ROUTE_W - ROUTE_E0 - N_EXP, D), F32)
        wr = jnp.concatenate([rgw_ref[...], gap, rew_ref[...], pad], axis=0)
        wrt_ref[...] = wr.astype(BF16)

        lg_scr[...] = jnp.zeros_like(lg_scr)

    def route_previous():
        cw8, info, pc, info_t = _route_cols(lg_scr[...])
        cw8_ref[...] = cw8
        info_ref[...] = info
        infot_ref[...] = info_t
        pc_scr[pl.ds(jnp.maximum(i - 1, 0), 1), :] = pc
        return cw8, info

    @pl.when(i < RT_NT)
    def _():
        routed = route_previous()
        ret_out = jnp.dot(ra_ref[...], wro_ref[...], preferred_element_type=F32)
        diff_out = jnp.dot(da_ref[...], wdo_ref[...], preferred_element_type=F32)
        for c in range(D // LANE):
            sl = slice(c * LANE, (c + 1) * LANE)
            m = (jax.nn.sigmoid(gr_ref[c].astype(F32)) * ret_out[:, sl]
                 + jax.nn.sigmoid(gd_ref[c].astype(F32)) * diff_out[:, sl])
            if c == 0:
                m = _ordered_after(_ordered_after(m, routed[0]), routed[1])
            m_scr[:, sl] = m.astype(BF16)
        mix = jnp.dot(m_scr[...], wo_ref[...], preferred_element_type=F32)
        mod = mod_ref[...]

        x1 = jnp.where(i < OP_NPT, xp_ref[...], xs_ref[...]) + mod[:, 2 * D:3 * D] * mix
        x1_ref[...] = x1
        h2 = (_rms(x1) * n2_ref[...] * (1.0 + mod[:, 4 * D:5 * D]) + mod[:, 3 * D:4 * D]).astype(BF16)
        h2_ref[...] = h2
        lg_scr[...] = lax.dot_general(wrt_ref[...], h2, (((1,), (1,)), ((), ())),
                                      preferred_element_type=F32) + brt_ref[...]

    @pl.when(i == RT_NT)
    def _():
        route_previous()
        off_ref[...], pctab_ref[...], steps_ref[...] = _dispatch_plan(pc_scr[...])


def _outproj(ret_act, diff_act, proj, xp, xs, mod3, n2, wro, wdo, wo, rgw, rew, brt):
    assert OP_TM == RT_TM
    npt = OP_NPT
    nt = N_TOK // OP_TM
    cur = lambda i: jnp.minimum(i, nt - 1)
    prev = lambda i: jnp.maximum(i - 1, 0)
    full = lambda i: (0, 0)
    once = pl.Buffered(1)
    return pl.pallas_call(
        _outproj_kernel,
        out_shape=(jax.ShapeDtypeStruct((N_TOK, D), F32),
                   jax.ShapeDtypeStruct((N_TOK, D), BF16),
                   jax.ShapeDtypeStruct((N_TOK, ROUTE_W), F32),
                   jax.ShapeDtypeStruct((N_TOK, ROUTE_W), F32),
                   jax.ShapeDtypeStruct((RT_NT, EPG, RT_TM), F32),
                   jax.ShapeDtypeStruct((RT_NT, ROUTE_W), jnp.int32),
                   jax.ShapeDtypeStruct((RT_NT, ROUTE_W), jnp.int32),
                   jax.ShapeDtypeStruct((PLAN_ROWS, ROUTE_W), jnp.int32)),
        grid=(nt + 1,),
        in_specs=[pl.BlockSpec((OP_TM, D), lambda i: (cur(i), 0)),
                  pl.BlockSpec((OP_TM, D), lambda i: (cur(i), 0)),
                  pl.BlockSpec((8, OP_TM, LANE), lambda i: (C_GR // 8, cur(i), 0)),
                  pl.BlockSpec((8, OP_TM, LANE), lambda i: (C_GD // 8, cur(i), 0)),
                  pl.BlockSpec((OP_TM, D), lambda i: (jnp.minimum(i, npt - 1), 0)),
                  pl.BlockSpec((OP_TM, D), lambda i: (jnp.maximum(cur(i) - npt, 0), 0)),
                  pl.BlockSpec((None, 1, 6 * D), lambda i: (_mod_row(OP_TM)(cur(i)), 0, 0)),
                  pl.BlockSpec((1, D), full),
                  pl.BlockSpec((D, D), full, pipeline_mode=once),
                  pl.BlockSpec((D, D), full, pipeline_mode=once),
                  pl.BlockSpec((D, D), full, pipeline_mode=once),
                  pl.BlockSpec((N_GROUPS, D), full, pipeline_mode=once),
                  pl.BlockSpec((N_EXP, D), full, pipeline_mode=once),
                  pl.BlockSpec((ROUTE_W, 1), full)],
        out_specs=(pl.BlockSpec((OP_TM, D), lambda i: (cur(i), 0)),
                   pl.BlockSpec((OP_TM, D), lambda i: (cur(i), 0)),
                   pl.BlockSpec((OP_TM, ROUTE_W), lambda i: (prev(i), 0)),
                   pl.BlockSpec((OP_TM, ROUTE_W), lambda i: (prev(i), 0)),
                   pl.BlockSpec((None, EPG, RT_TM), lambda i: (prev(i), 0, 0)),
                   pl.BlockSpec((RT_NT, ROUTE_W), full),
                   pl.BlockSpec((RT_NT, ROUTE_W), full),
                   pl.BlockSpec((PLAN_ROWS, ROUTE_W), full)),
        scratch_shapes=[pltpu.VMEM((OP_TM, D), BF16),
                        pltpu.VMEM((D, D), BF16), pltpu.VMEM((D, D), BF16), pltpu.VMEM((D, D), BF16),
                        pltpu.VMEM((ROUTE_W, D), BF16),
                        pltpu.VMEM((RT_NT, ROUTE_W), F32),
                        pltpu.VMEM((ROUTE_W, RT_TM), F32)],
        compiler_params=_cparams(("arbitrary",)),
        name="outproj",
    )(ret_act, diff_act, proj, proj, xp, xs, mod3, n2, wro, wdo, wo, rgw, rew, brt)


RT_TM = 512
RT_NT = N_TOK // RT_TM
PIECE = 16
R_LOC = RT_TM + N_GROUPS * PIECE
R_STAGE = 640
EX_TM = 512
R_MAX = 11264
EX_NT = R_MAX // EX_TM


ROUTE_E0 = EPG


def _route_cols(lgt):
    row = lax.broadcasted_iota(jnp.int32, (EPG, RT_TM), 0)
    neg = jnp.float32(-jnp.inf)

    def first_row(cond):
        return jnp.min(jnp.where(cond, row, EPG), axis=0, keepdims=True)

    head = lgt[0:EPG]
    is_g = row < N_GROUPS
    gl = jnp.where(is_g, head, neg)
    gmax = jnp.max(gl, axis=0, keepdims=True)
    gsum = jnp.sum(jnp.where(is_g, jnp.exp(head - gmax), 0.0), axis=0, keepdims=True)
    p_top = 1.0 / gsum
    g_idx = first_row(gl == gmax)
    el = lgt[ROUTE_E0 + (N_GROUPS - 1) * EPG:ROUTE_E0 + N_GROUPS * EPG]
    for g in reversed(range(N_GROUPS - 1)):
        el = jnp.where(g_idx == g, lgt[ROUTE_E0 + g * EPG:ROUTE_E0 + (g + 1) * EPG], el)
    ee = jnp.exp(el - jnp.max(el, axis=0, keepdims=True))
    ep = ee / jnp.sum(ee, axis=0, keepdims=True)
    e1 = jnp.max(ep, axis=0, keepdims=True)
    i1 = first_row(ep == e1)
    ep2 = jnp.where(row == i1, -1.0, ep)
    e2 = jnp.max(ep2, axis=0, keepdims=True)
    i2 = first_row(ep2 == e2)
    den = e1 + e2
    cw8_t = (jnp.where(row == i1, p_top * e1 / den, 0.0)
             + jnp.where(row == i2, p_top * e2 / den, 0.0))

    onehot = (row == g_idx).astype(F32)
    ii = lax.broadcasted_iota(jnp.int32, (RT_TM, RT_TM), 0)
    jj = lax.broadcasted_iota(jnp.int32, (RT_TM, RT_TM), 1)
    earlier = (ii < jj).astype(BF16)
    prefix = jnp.dot(onehot.astype(BF16), earlier, preferred_element_type=F32)
    cnt = jnp.sum(onehot, axis=1, keepdims=True)
    pc_col = jnp.floor((cnt + (PIECE - 1.0)) * (1.0 / PIECE)) * PIECE
    row1 = lax.broadcasted_iota(jnp.int32, (EPG, 1), 0)
    lane1 = lax.broadcasted_iota(jnp.int32, (1, ROUTE_W), 1)
    lo = jnp.zeros((EPG, 1), F32)
    run = jnp.zeros((1, 1), F32)
    pc = jnp.zeros((1, ROUTE_W), F32)
    for g in range(N_GROUPS):
        pc_g = jnp.sum(jnp.where(row1 == g, pc_col, 0.0), axis=0, keepdims=True)
        lo = jnp.where(row1 == g, run, lo)
        pc = jnp.where(lane1 == g, pc_g, pc)
        run = run + pc_g
    dest = jnp.sum(onehot * (prefix + lo), axis=0, keepdims=True)
    info_t = jnp.where(row == 0, g_idx.astype(F32), jnp.where(row == 1, dest, 0.0))

    slab = jnp.concatenate([cw8_t, info_t, jnp.zeros((LANE - 2 * EPG, RT_TM), F32)], axis=0)
    cols = slab.T
    lane = lax.broadcasted_iota(jnp.int32, cols.shape, 1)
    cw8 = jnp.where(lane < EPG, cols, 0.0)
    info = jnp.where(lane < 2, pltpu.roll(cols, LANE - EPG, 1), 0.0)
    return cw8, info, pc, info_t


PLAN_ROWS = 32


def _dispatch_plan(pc_all):
    lane = lax.broadcasted_iota(jnp.int32, (1, ROUTE_W), 1)
    is_g = lane < N_GROUPS
    seg_len = jnp.sum(pc_all, axis=0, keepdims=True)
    seg_pad = jnp.floor((seg_len + (EX_TM - 1.0)) * (1.0 / EX_TM)) * EX_TM
    run = seg_pad + pltpu.roll(seg_pad, 1, 1)
    seg_end_pad = jnp.where(is_g, run + pltpu.roll(run, 2, 1), 0.0)
    seg_start = seg_end_pad - seg_pad
    ti = lax.broadcasted_iota(jnp.int32, (RT_NT, RT_NT), 0)
    tj = lax.broadcasted_iota(jnp.int32, (RT_NT, RT_NT), 1)
    earlier = (tj < ti).astype(BF16)
    within = jnp.dot(earlier, pc_all.astype(BF16), preferred_element_type=F32)
    chunk_off = jnp.where(is_g, seg_start + within, 0.0)

    start = (lax.broadcasted_iota(jnp.int32, (PLAN_ROWS, 1), 0) * EX_TM).astype(F32)
    passed = jnp.where(jnp.logical_and(is_g, start >= seg_end_pad), 1.0, 0.0)
    group = jnp.minimum(jnp.sum(passed, axis=1, keepdims=True), N_GROUPS - 1.0)
    used_end = jnp.sum(jnp.where(lane == group.astype(jnp.int32), seg_start + seg_len, 0.0),
                       axis=1, keepdims=True)
    used = jnp.clip(used_end - start, 0.0, float(EX_TM))
    steps = jnp.where(lane == 0, group, jnp.where(lane == 1, used, 0.0))
    return chunk_off.astype(jnp.int32), pc_all.astype(jnp.int32), steps.astype(jnp.int32)


BIG = 64


def _piece_copies(off_ref, pc_ref, tile, make):
    lo = 0
    for g in range(N_GROUPS):
        rows = pc_ref[tile, g]
        n_big = rows // BIG
        for size, first, n in ((BIG, 0, n_big), (PIECE, n_big * BIG, (rows - n_big * BIG) // PIECE)):
            def body(j, carry, size=size, lo=lo + first, base=off_ref[tile, g] + first):
                make(pl.multiple_of(lo + j * size, PIECE), pl.multiple_of(base + j * size, PIECE), size)
                return carry

            lax.fori_loop(0, n, body, 0)
        lo = lo + rows


def _piece_waits(pc_ref, tile, wait):
    n_big = 0
    n_small = 0
    for g in range(N_GROUPS):
        rows = pc_ref[tile, g]
        n_big = n_big + rows // BIG
        n_small = n_small + (rows % BIG) // PIECE
    for size, n in ((BIG, n_big), (PIECE, n_small)):
        def body(j, carry, size=size):
            wait(size)
            return carry

        lax.fori_loop(0, n, body, 0)


XW = D + ROUTE_W


def _dispatch_kernel(off_ref, pc_ref, h_ref, infot_ref, cw8_ref, xs_in, xs_out, x_scr, sem):
    del xs_in
    i = pl.program_id(0)
    slot = i % 2
    dest = infot_ref[1:2, :]
    row = lax.broadcasted_iota(jnp.int32, (R_LOC, RT_TM), 0).astype(F32)
    sel = (row == dest).astype(BF16)
    cw = cw8_ref[...]
    hi = cw.astype(BF16).astype(F32)
    mid = (cw - hi).astype(BF16).astype(F32)
    low = (cw - hi - mid).astype(BF16).astype(F32)
    pieces = (hi + pltpu.roll(mid, EPG, 1) + pltpu.roll(low, 2 * EPG, 1)).astype(BF16)
    rows = jnp.concatenate([h_ref[...], pieces], axis=1)
    x_scr[slot] = jnp.dot(sel, rows, preferred_element_type=F32).astype(BF16)

    def x_copy(s, src, dst, rows):
        return pltpu.make_async_copy(x_scr.at[s, pl.ds(src, rows)], xs_out.at[pl.ds(dst, rows)],
                                     sem.at[s])

    _piece_copies(off_ref, pc_ref, i, lambda src, dst, rows: x_copy(slot, src, dst, rows).start())

    def wait_tile(tile, s):
        _piece_waits(pc_ref, tile, lambda rows: x_copy(s, 0, 0, rows).wait())

    @pl.when(i > 0)
    def _():
        wait_tile(i - 1, 1 - slot)

    @pl.when(i == RT_NT - 1)
    def _():
        wait_tile(i, slot)


def _dispatch(chunk_off, pc, h2, info_t, cw8, xs0):
    grid_spec = pltpu.PrefetchScalarGridSpec(
        num_scalar_prefetch=2,
        grid=(RT_NT,),
        in_specs=[pl.BlockSpec((RT_TM, D), lambda i, o, p: (i, 0)),
                  pl.BlockSpec((None, EPG, RT_TM), lambda i, o, p: (i, 0, 0)),
                  pl.BlockSpec((RT_TM, ROUTE_W), lambda i, o, p: (i, 0)),
                  pl.BlockSpec(memory_space=pl.ANY)],
        out_specs=pl.BlockSpec(memory_space=pl.ANY),
        scratch_shapes=[pltpu.VMEM((2, R_LOC, XW), BF16),
                        pltpu.SemaphoreType.DMA((2,))])
    return pl.pallas_call(
        _dispatch_kernel,
        out_shape=jax.ShapeDtypeStruct((R_MAX, XW), BF16),
        grid_spec=grid_spec,
        input_output_aliases={5: 0},
        compiler_params=_cparams(("arbitrary",)),
        name="dispatch",
    )(chunk_off, pc, h2, info_t, cw8, xs0)


def _expert_kernel(steps_ref, x_ref, wg_ref, wu_ref, wd_ref, y_ref):
    k = pl.program_id(0)
    used = steps_ref[k, 1]
    half = EX_TM // 2

    def run(rows):
        x = x_ref[rows, 0:D]
        cw = x_ref[rows, D:XW].astype(F32)
        lane = lax.broadcasted_iota(jnp.int32, cw.shape, 1)
        acc = jnp.zeros(x.shape, F32)
        for j in range(EPG):
            mine = jnp.logical_and((lane & (EPG - 1)) == j, lane < 3 * EPG)
            w = jnp.sum(jnp.where(mine, cw, 0.0), axis=-1, keepdims=True)
            a = (jax.nn.silu(jnp.dot(x, wg_ref[j], preferred_element_type=F32))
                 * jnp.dot(x, wu_ref[j], preferred_element_type=F32))
            acc = acc + jnp.dot((a * w).astype(BF16), wd_ref[j], preferred_element_type=F32)
        y_ref[rows, :] = acc.astype(BF16)

    @pl.when(used > half)
    def _():
        run(slice(0, EX_TM))

    @pl.when(jnp.logical_and(used > 0, used <= half))
    def _():
        run(slice(0, half))
        y_ref[half:EX_TM, :] = jnp.zeros((EX_TM - half, D), BF16)

    @pl.when(used == 0)
    def _():
        y_ref[...] = jnp.zeros_like(y_ref)


def _experts(steps, xs, wg, wu, wd):
    grid_spec = pltpu.PrefetchScalarGridSpec(
        num_scalar_prefetch=1,
        grid=(EX_NT,),
        in_specs=[pl.BlockSpec((EX_TM, XW), lambda k, st: (k, 0)),
                  pl.BlockSpec((EPG, D, FF), lambda k, st: (st[k, 0], 0, 0)),
                  pl.BlockSpec((EPG, D, FF), lambda k, st: (st[k, 0], 0, 0)),
                  pl.BlockSpec((EPG, FF, D), lambda k, st: (st[k, 0], 0, 0))],
        out_specs=pl.BlockSpec((EX_TM, D), lambda k, st: (k, 0)))
    return pl.pallas_call(
        _expert_kernel,
        out_shape=jax.ShapeDtypeStruct((R_MAX, D), BF16),
        grid_spec=grid_spec,
        compiler_params=_cparams(("arbitrary",)),
        name="experts",
    )(steps, xs, wg, wu, wd)


CB_NPT = N_PROMPT // RT_TM


def _combine_kernel(off_ref, pc_ref, info_ref, x1_ref, mod_ref, g_ref, ys_hbm,
                    yp_ref, ysm_ref, stage, sem):
    i = pl.program_id(0)
    slot = i % 2

    def copy(s, dst, src, rows):
        return pltpu.make_async_copy(ys_hbm.at[pl.ds(src, rows)], stage.at[s, pl.ds(dst, rows)],
                                     sem.at[s])

    def fetch(tile, s):
        _piece_copies(off_ref, pc_ref, tile, lambda loc, glob, rows: copy(s, loc, glob, rows).start())

    @pl.when(i == 0)
    def _():
        stage[...] = jnp.zeros_like(stage)
        fetch(0, 0)

    @pl.when(i + 1 < RT_NT)
    def _():
        fetch(i + 1, 1 - slot)

    _piece_waits(pc_ref, i, lambda rows: copy(slot, 0, 0, rows).wait())

    dest = info_ref[...][:, 1:2]
    col = lax.broadcasted_iota(jnp.int32, (RT_TM, R_STAGE), 1).astype(F32)
    moe = jnp.dot((col == dest).astype(BF16), stage[slot], preferred_element_type=F32)
    mod = mod_ref[...]
    out = _rms(x1_ref[...] + mod[:, 5 * D:6 * D] * moe) * g_ref[...]

    @pl.when(i < CB_NPT)
    def _():
        yp_ref[...] = out

    @pl.when(i >= CB_NPT)
    def _():
        ysm_ref[...] = out


def _combine(chunk_off, pc, info, x1, mod3, fg, ys):
    npt = CB_NPT
    grid_spec = pltpu.PrefetchScalarGridSpec(
        num_scalar_prefetch=2,
        grid=(RT_NT,),
        in_specs=[pl.BlockSpec((RT_TM, ROUTE_W), lambda i, o, p: (i, 0)),
                  pl.BlockSpec((RT_TM, D), lambda i, o, p: (i, 0)),
                  pl.BlockSpec((None, 1, 6 * D), lambda i, o, p: (_mod_row(RT_TM)(i), 0, 0)),
                  pl.BlockSpec((1, D), lambda i, o, p: (0, 0)),
                  pl.BlockSpec(memory_space=pl.ANY)],
        out_specs=(pl.BlockSpec((RT_TM, D), lambda i, o, p: (jnp.minimum(i, npt - 1), 0)),
                   pl.BlockSpec((RT_TM, D), lambda i, o, p: (jnp.maximum(i - npt, 0), 0))),
        scratch_shapes=[pltpu.VMEM((2, R_STAGE, D), BF16),
                        pltpu.SemaphoreType.DMA((2,))])
    return pl.pallas_call(
        _combine_kernel,
        out_shape=(jax.ShapeDtypeStruct((N_PROMPT, D), F32),
                   jax.ShapeDtypeStruct((N_SAMPLE, D), F32)),
        grid_spec=grid_spec,
        compiler_params=_cparams(("arbitrary",)),
        name="combine",
    )(chunk_off, pc, info, x1, mod3, fg, ys)


def kernel(x_prompt, x_sample, cache_diff_k, cache_diff_v, state_ret_fwd, state_ret_bwd, c, c_ctx,
           w_ada, b_ada, norm1_g, norm2_g, w_in, ret_decay_fwd, ret_decay_bwd, ret_norm_g,
           diff_lambda_q1, diff_lambda_k1, diff_lambda_q2, diff_lambda_k2, diff_subln_g,
           w_ret_o, w_diff_o, w_o, router_group_w, router_group_b, router_expert_w, router_expert_b,
           moe_w_gate, moe_w_up, moe_w_down, final_norm_g):
    l = 0
    lam_init = 0.8 - 0.6 * math.exp(-0.3 * l)
    row = lambda a: a[l].astype(F32)[None, :]

    xp = x_prompt.reshape(N_PROMPT, D)
    xs = x_sample.reshape(N_SAMPLE, D)
    mod3, aux, brt = _modulation(
        lam_init, c_ctx[None, :], c, w_ada[l], b_ada[l][None, :],
        [row(diff_lambda_q1), row(diff_lambda_k1), row(diff_lambda_q2), row(diff_lambda_k2),
         row(ret_decay_fwd), row(ret_decay_bwd), row(router_group_b), row(router_expert_b)])

    proj, kt32, v32 = _inproj(xp, xs, mod3, norm1_g[l][None, :], w_in[l])

    cos_t, sin_t = _rope_tables()
    cache_kt = jnp.transpose(cache_diff_k[:, l], (0, 2, 3, 4, 1)).reshape(4, D, PAST)
    (diff_act, ret_act, s_f, s_b, wg_bf, wu_bf, wd_bf, xs0) = _mixers(
        aux, proj, cache_kt, cache_diff_v, cos_t, sin_t, diff_subln_g[l][None, :],
        state_ret_fwd[:, l], state_ret_bwd[:, l], ret_norm_g[l][None, :],
        (moe_w_gate[l], moe_w_up[l], moe_w_down[l]))

    x1, h2, cw8, info, info_t, chunk_off, pc, steps = _outproj(
        ret_act, diff_act, proj, xp, xs, mod3, norm2_g[l][None, :],
        w_ret_o[l], w_diff_o[l], w_o[l], router_group_w[l].T, router_expert_w[l].T, brt)
    xs_sorted = _dispatch(chunk_off, pc, h2, info_t, cw8, xs0)
    y_sorted = _experts(steps, xs_sorted, wg_bf, wu_bf, wd_bf)
    yp, ys = _combine(chunk_off, pc, info, x1, mod3, final_norm_g[None, :], y_sorted)

    return (yp.reshape(16, T_P, D), ys.reshape(4, T_S, D),
            jnp.transpose(kt32.reshape(16, DIFF_H, 2, DIFF_HD, T_P), (0, 4, 1, 2, 3))[:, None],
            v32.reshape(16, 1, T_P, DIFF_H, 2 * DIFF_HD),
            s_f.reshape(16, 1, RET_H, RET_KD, RET_VD), s_b.reshape(16, 1, RET_H, RET_KD, RET_VD))
```

```python
import functools
import math

import jax
import jax.numpy as jnp
import numpy as np
from jax import lax
from jax.experimental import pallas as pl
from jax.experimental.pallas import tpu as pltpu

F32 = jnp.float32
BF16 = jnp.bfloat16

D = 1024
N_PROMPT = 16 * 256
N_SAMPLE = 4 * 1024
N_TOK = N_PROMPT + N_SAMPLE
T_P = 256
T_S = 1024
PAST = 512
GRID_W = 64
RET_H = 4
RET_KD = 128
RET_VD = 256
DIFF_H = 8
DIFF_HD = 64
ROPE_BASE = 10000.0
N_GROUPS = 4
EPG = 8
N_EXP = 32
FF = 256
EPS = 1e-6
IN_W = 8192
LANE = 128
N_CHUNK = IN_W // LANE
C_RQ, C_RK, C_RV, C_RG, C_DQ, C_DK, C_DV, C_GR, C_GD = 0, 4, 8, 16, 24, 32, 40, 48, 56
ROUTE_W = 128
SEQ_BLK = 1024
N_PBLK = N_PROMPT // SEQ_BLK
V7X_VMEM_BYTES = 64 * 1024 * 1024
VMEM_LIMIT = V7X_VMEM_BYTES - 8 * 1024 * 1024


def _cparams(sem):
    return pltpu.CompilerParams(dimension_semantics=sem, vmem_limit_bytes=VMEM_LIMIT)


def _mod_row(tile_rows):
    def f(i):
        start = i * tile_rows
        return jnp.where(start < N_PROMPT, 0, 1 + (start - N_PROMPT) // T_S)
    return f


def _rms(x):
    return x * lax.rsqrt(jnp.mean(x * x, axis=-1, keepdims=True) + EPS)


def _ordered_after(dst, src):
    m = None
    for a in range(src.shape[0] // 8):
        for b in range(src.shape[1] // LANE):
            part = src[a * 8:(a + 1) * 8, b * LANE:(b + 1) * LANE]
            m = part if m is None else jnp.maximum(m, part)
    never = jnp.logical_and(m != m, m == m)
    return jnp.where(jnp.tile(never, (dst.shape[0] // 8, 1)), 0.0, dst)


MOD_ROWS = 8


AUX_LAM = 0
AUX_LG = 1


def _mod_kernel(lam_init, ctx_ref, c_ref, w_ref, b_ref, lq1_ref, lk1_ref, lq2_ref, lk2_ref, df_ref, db_ref,
                gb_ref, eb_ref, o_ref, aux_ref, brt_ref):
    cond = jnp.concatenate([ctx_ref[...], c_ref[...],
                            jnp.zeros((MOD_ROWS - 1 - c_ref.shape[0], D), F32)], axis=0)
    s = jax.nn.silu(cond)
    out = jnp.dot(s.astype(BF16), w_ref[...].astype(BF16), preferred_element_type=F32) + b_ref[...]
    for r in range(MOD_ROWS):
        o_ref[r] = out[r:r + 1, :]

    def lane_row(parts):
        used = sum(p.shape[1] for p in parts)
        return jnp.concatenate(parts + [jnp.zeros((1, LANE - used), F32)], axis=1)

    lam = (jnp.exp(jnp.sum(lq1_ref[...] * lk1_ref[...], axis=1, keepdims=True))
           - jnp.exp(jnp.sum(lq2_ref[...] * lk2_ref[...], axis=1, keepdims=True)) + lam_init)
    rows = [lane_row([lam, jnp.full((1, 1), 1.0 - lam_init, F32)]),
            lane_row([jax.nn.log_sigmoid(df_ref[...])]),
            lane_row([jax.nn.log_sigmoid(db_ref[...])])]
    aux_ref[...] = jnp.concatenate(rows + [jnp.zeros((8 - len(rows), LANE), F32)], axis=0)
    bias = lane_row([gb_ref[...], jnp.zeros((1, ROUTE_E0 - N_GROUPS), F32), eb_ref[...]])
    brt_ref[...] = jnp.tile(bias, (8, 1)).T[:, 0:1]


def _modulation(lam_init, c_ctx, c, w_ada, b_ada, small):
    tn = 1536
    whole = lambda a: pl.BlockSpec(a.shape, lambda j: (0, 0))
    return pl.pallas_call(
        functools.partial(_mod_kernel, lam_init),
        out_shape=(jax.ShapeDtypeStruct((MOD_ROWS, 1, 6 * D), F32),
                   jax.ShapeDtypeStruct((8, LANE), F32),
                   jax.ShapeDtypeStruct((ROUTE_W, 1), F32)),
        grid=(6 * D // tn,),
        in_specs=[pl.BlockSpec((1, D), lambda j: (0, 0)),
                  pl.BlockSpec(c.shape, lambda j: (0, 0)),
                  pl.BlockSpec((D, tn), lambda j: (0, j)),
                  pl.BlockSpec((1, tn), lambda j: (0, j))] + [whole(a) for a in small],
        out_specs=(pl.BlockSpec((MOD_ROWS, 1, tn), lambda j: (0, 0, j)),
                   pl.BlockSpec((8, LANE), lambda j: (0, 0)),
                   pl.BlockSpec((ROUTE_W, 1), lambda j: (0, 0))),
        compiler_params=_cparams(("arbitrary",)),
        name="mod",
    )(c_ctx, c, w_ada, b_ada, *small)


IP_TM = 512
IP_TN = 2048
IP_NPT = N_PROMPT // IP_TM


IP_KV_TILE = C_DK * LANE // IP_TN
IP_SPT = IP_TM // T_P


def _inproj_kernel(xp_ref, xs_ref, mod_ref, n1_ref, w_ref, proj_ref, kt_ref, v32_ref, w_scr):
    j = pl.program_id(0)
    i = pl.program_id(1)

    @pl.when(i == 0)
    def _():
        w_scr[...] = w_ref[...].astype(BF16)

    x = jnp.where(i < IP_NPT, xp_ref[...], xs_ref[...])
    mod = mod_ref[...]
    h = (_rms(x) * n1_ref[...] * (1.0 + mod[:, D:2 * D]) + mod[:, 0:D]).astype(BF16)
    acc = jnp.dot(h, w_scr[...], preferred_element_type=F32)
    for c in range(IP_TN // LANE):
        proj_ref[c] = acc[:, c * LANE:(c + 1) * LANE].astype(BF16)

    @pl.when(jnp.logical_and(j == IP_KV_TILE, i < IP_NPT))
    def _():
        for s in range(IP_SPT):
            kt_ref[s] = acc[s * T_P:(s + 1) * T_P, :D].T
        v32_ref[...] = acc[:, D:]


def _inproj(xp, xs, mod3, n1, w_in):
    npt = IP_NPT
    cpt = IP_TN // LANE

    def kv_tile(j, i):
        return jnp.where(j < IP_KV_TILE, 0, jnp.where(j == IP_KV_TILE, jnp.minimum(i, npt - 1), npt - 1))

    return pl.pallas_call(
        _inproj_kernel,
        out_shape=(jax.ShapeDtypeStruct((N_CHUNK, N_TOK, LANE), BF16),
                   jax.ShapeDtypeStruct((N_PROMPT // T_P, D, T_P), F32),
                   jax.ShapeDtypeStruct((N_PROMPT, D), F32)),
        grid=(IN_W // IP_TN, N_TOK // IP_TM),
        in_specs=[pl.BlockSpec((IP_TM, D), lambda j, i: (jnp.minimum(i, npt - 1), 0)),
                  pl.BlockSpec((IP_TM, D), lambda j, i: (jnp.maximum(i - npt, 0), 0)),
                  pl.BlockSpec((None, 1, 6 * D), lambda j, i: (_mod_row(IP_TM)(i), 0, 0)),
                  pl.BlockSpec((1, D), lambda j, i: (0, 0)),
                  pl.BlockSpec((D, IP_TN), lambda j, i: (0, j))],
        out_specs=(pl.BlockSpec((cpt, IP_TM, LANE), lambda j, i: (j, i, 0)),
                   pl.BlockSpec((IP_SPT, D, T_P), lambda j, i: (kv_tile(j, i), 0, 0)),
                   pl.BlockSpec((IP_TM, D), lambda j, i: (kv_tile(j, i), 0))),
        scratch_shapes=[pltpu.VMEM((D, IP_TN), BF16)],
        compiler_params=_cparams(("arbitrary", "arbitrary")),
        name="inproj",
    )(xp, xs, mod3, n1, w_in)


def _decay_mask(t, lgf, lgb):
    ii = lax.broadcasted_iota(jnp.int32, (t, t), 0)
    jj = lax.broadcasted_iota(jnp.int32, (t, t), 1)
    rel = (ii - jj).astype(F32)
    e = jnp.exp(jnp.where(rel >= 0.0, lgf, -lgb) * rel)
    return jnp.where(rel == 0.0, 2.0, e) * (RET_KD ** -0.5)


def _ret_parts(lgf, lgb, q_ref, k_ref, v_ref, rg_ref, s0f_ref, s0b_ref, g_ref,
               o_ref, sf_ref, sb_ref, dm_ref):
    gain = g_ref[...]
    nt = (((1,), (1,)), ((), ()))
    tn = (((0,), (0,)), ((), ()))

    def finish(o, rg):
        d = o - jnp.mean(o, axis=-1, keepdims=True)
        y = d * lax.rsqrt(jnp.mean(d * d, axis=-1, keepdims=True) + EPS) * gain
        return (jax.nn.silu(rg.astype(F32)) * y).astype(BF16)

    def vcat(r):
        return jnp.concatenate([v_ref[0, r, :], v_ref[1, r, :]], axis=1)

    def gcat(r):
        return jnp.concatenate([rg_ref[0, r, :], rg_ref[1, r, :]], axis=1)

    def build_mask():
        dm_ref[...] = _decay_mask(T_P, lgf, lgb)

    n_chunk = SEQ_BLK // T_P
    chunks = [slice(c * T_P, (c + 1) * T_P) for c in range(n_chunk)]
    t = lax.broadcasted_iota(jnp.int32, (T_P, 1), 0).astype(F32)
    kdf = jnp.exp(lgf * (T_P - 1.0 - t)) * (RET_KD ** -0.5)
    kdb = jnp.exp(lgb * t) * (RET_KD ** -0.5)

    def intra(r):
        sc = lax.dot_general(q_ref[r, :], k_ref[r, :], nt, preferred_element_type=F32)
        return jnp.dot((sc * dm_ref[...]).astype(BF16), vcat(r), preferred_element_type=F32)

    def key_state(r, kd):
        kw = (k_ref[r, :].astype(F32) * kd).astype(BF16)
        return lax.dot_general(kw, vcat(r), tn, preferred_element_type=F32)

    def context_part():
        for s, r in enumerate(chunks):
            o_ref[r, :] = finish(intra(r), gcat(r))
            sf_ref[s] = key_state(r, kdf)
            sb_ref[s] = key_state(r, kdb)

    def latent_part():
        qdf = jnp.exp(lgf * (t + 1.0))
        qdb = jnp.exp(lgb * (T_P - t))
        span = jnp.full((1, 1), float(T_P), F32)
        cf = jnp.exp(lgf * span)
        cb = jnp.exp(lgb * span)
        before_b = [None] * n_chunk
        state = s0b_ref[...]
        for c in reversed(range(n_chunk)):
            before_b[c] = state
            if c > 0:
                state = cb * state + key_state(chunks[c], kdb)
        state = s0f_ref[...]
        for c, r in enumerate(chunks):
            qf = q_ref[r, :].astype(F32)
            o = (intra(r)
                 + jnp.dot((qf * qdf).astype(BF16), state.astype(BF16), preferred_element_type=F32)
                 + jnp.dot((qf * qdb).astype(BF16), before_b[c].astype(BF16), preferred_element_type=F32))
            o_ref[r, :] = finish(o, gcat(r))
            if c + 1 < n_chunk:
                state = cf * state + key_state(r, kdf)

    return build_mask, context_part, latent_part


ATT_TQ = 1024
ATT_HPS = 2
ATT_QSCALE = (DIFF_HD ** -0.5) * math.log2(math.e)


N_SIDE = 3


def _mixer_kernel(aux_ref, q_ref, k_ref, v_ref, ckt_ref, cv_ref, cos_ref, sin_ref, g_ref,
                  rq_ref, rk_ref, rv_ref, rg_ref, s0f_ref, s0b_ref, gn_ref, *rest):
    side_in = rest[:N_SIDE]
    o_ref, ro_ref, sf_ref, sb_ref = rest[N_SIDE:N_SIDE + 4]
    side_out = rest[N_SIDE + 4:2 * N_SIDE + 4]
    xs0_ref, q_scr, k_scr, v_scr, dm_scr = rest[2 * N_SIDE + 4:]
    i = pl.program_id(0)
    rh = pl.program_id(1)
    build_mask, ret_context, ret_latent = _ret_parts(
        aux_ref[AUX_LG, rh], aux_ref[AUX_LG + 1, rh], rq_ref, rk_ref, rv_ref, rg_ref, s0f_ref, s0b_ref, gn_ref,
        ro_ref, sf_ref, sb_ref, dm_scr.at[rh])

    @pl.when(i == 0)
    def _():
        build_mask()

    def side_jobs():
        for src, dst in zip(side_in, side_out):
            dst[...] = src[...].astype(BF16)
        xs0_ref[...] = jnp.zeros_like(xs0_ref)

    lam = aux_ref[AUX_LAM, 0]
    out_scale = aux_ref[AUX_LAM, 1]
    gain = g_ref[...] * out_scale
    lane = lax.broadcasted_iota(jnp.int32, (1, LANE), 1)
    first = lane < DIFF_HD
    nt = (((1,), (1,)), ((), ()))

    def halves(q):
        zero = jnp.zeros_like(q)
        return jnp.where(first, q, zero), jnp.where(first, zero, q)

    def weights(s):
        return jnp.exp2(s - jnp.max(s, axis=-1, keepdims=True)).astype(BF16)

    def finish(of0, of1):
        o = of0[:, :LANE] / of0[:, LANE:] - lam * (of1[:, :LANE] / of1[:, LANE:])
        return (_rms(o) * gain).astype(BF16)

    @pl.when(i < N_PBLK)
    def _():
        ret_context()
        n_seq = SEQ_BLK // T_P
        rows = [slice(s * T_P, (s + 1) * T_P) for s in range(n_seq)]
        ones = jnp.ones((T_P, LANE), BF16)
        for hh in range(ATT_HPS):
            cols = slice(hh * LANE, (hh + 1) * LANE)
            q0, q1 = halves((q_ref[hh].astype(F32) * ATT_QSCALE).astype(BF16))
            s0 = jnp.concatenate([lax.dot_general(q0[r], k_ref[hh, r, :], nt, preferred_element_type=F32)
                                  for r in rows], axis=0)
            s1 = jnp.concatenate([lax.dot_general(q1[r], k_ref[hh, r, :], nt, preferred_element_type=F32)
                                  for r in rows], axis=0)
            e0 = weights(s0)
            e1 = weights(s1)
            for r in rows:
                v1 = jnp.concatenate([v_ref[hh, r, :], ones], axis=1)
                o_ref[r, cols] = finish(jnp.dot(e0[r], v1, preferred_element_type=F32),
                                        jnp.dot(e1[r], v1, preferred_element_type=F32))

    @pl.when(i >= N_PBLK)
    def _():
        side_jobs()
        ret_latent()
        cos = cos_ref[...]
        sin = sin_ref[...]
        low = (lax.broadcasted_iota(jnp.int32, (T_S, LANE), 1) & 16) == 0

        def rope(x):
            xs = jnp.where(low, pltpu.roll(x, LANE - 16, 1), pltpu.roll(x, 16, 1))
            return x * cos + xs * sin

        for hh in range(ATT_HPS):
            cols = slice(hh * LANE, (hh + 1) * LANE)
            head = pl.program_id(1) * ATT_HPS + hh
            q_scr[hh] = (rope(q_ref[hh].astype(F32)) * ATT_QSCALE).astype(BF16)
            k_scr[hh] = rope(k_ref[hh].astype(F32)).astype(BF16)
            ckt = ckt_ref[cols, :].astype(BF16)
            v_scr[hh, 0:T_S, 0:LANE] = v_ref[hh]
            v_scr[hh, T_S:T_S + PAST, 0:LANE] = cv_ref[:, head, :].astype(BF16)
            v_scr[hh, :, LANE:2 * LANE] = jnp.ones((T_S + PAST, LANE), BF16)

            def scores(qh, hh=hh, ckt=ckt):
                return jnp.concatenate([lax.dot_general(qh, k_scr[hh], nt, preferred_element_type=F32),
                                        jnp.dot(qh, ckt, preferred_element_type=F32)], axis=1)

            for b in range(T_S // ATT_TQ):
                r = slice(b * ATT_TQ, (b + 1) * ATT_TQ)
                q0, q1 = halves(q_scr[hh, r, :])
                e0 = weights(scores(q0))
                e1 = weights(scores(q1))
                o_ref[r, cols] = finish(jnp.dot(e0, v_scr[hh], preferred_element_type=F32),
                                        jnp.dot(e1, v_scr[hh], preferred_element_type=F32))


def _mixers(aux, proj, cache_k, cache_v, cos_t, sin_t, subln_g, s0f, s0b, gnorm, side_weights):
    nb = N_TOK // SEQ_BLK
    spb = SEQ_BLK // T_P
    smp = lambda i: jnp.maximum(i - N_PBLK, 0)
    pmt = lambda i: jnp.minimum(i, N_PBLK - 1)
    pmh = lambda i, h: jnp.where(i < N_PBLK, h, RET_H - 1)
    state_in = pl.BlockSpec((None, None, RET_KD, RET_VD), lambda i, h: (smp(i), h, 0, 0))
    state_out = pl.BlockSpec((spb, None, RET_KD, RET_VD), lambda i, h: (pmt(i), pmh(i, h), 0, 0))

    n_hp = DIFF_H // ATT_HPS
    assert n_hp == RET_H
    n_steps = (nb - N_PBLK) * n_hp
    step = lambda i, h: jnp.maximum((i - N_PBLK) * n_hp + h, 0)
    exp_per_step = N_EXP // n_steps
    exp_slice = lambda i, h: (step(i, h), 0, 0)
    row_slice = lambda i, h: (step(i, h), 0)
    up_spec = pl.BlockSpec((exp_per_step, D, FF), exp_slice)
    down_spec = pl.BlockSpec((exp_per_step, FF, D), exp_slice)
    side_specs = [up_spec, up_spec, down_spec]
    side_shapes = [jax.ShapeDtypeStruct(a.shape, BF16) for a in side_weights]
    zrows = R_MAX // n_steps

    return pl.pallas_call(
        _mixer_kernel,
        out_shape=(jax.ShapeDtypeStruct((N_TOK, DIFF_H * 2 * DIFF_HD), BF16),
                   jax.ShapeDtypeStruct((N_TOK, RET_H * RET_VD), BF16),
                   jax.ShapeDtypeStruct((16, RET_H, RET_KD, RET_VD), F32),
                   jax.ShapeDtypeStruct((16, RET_H, RET_KD, RET_VD), F32),
                   *side_shapes,
                   jax.ShapeDtypeStruct((R_MAX, XW), BF16)),
        grid=(nb, n_hp),
        in_specs=[pl.BlockSpec(memory_space=pltpu.SMEM),
                  pl.BlockSpec((ATT_HPS, SEQ_BLK, LANE), lambda i, h: (C_DQ // ATT_HPS + h, i, 0)),
                  pl.BlockSpec((ATT_HPS, SEQ_BLK, LANE), lambda i, h: (C_DK // ATT_HPS + h, i, 0)),
                  pl.BlockSpec((ATT_HPS, SEQ_BLK, LANE), lambda i, h: (C_DV // ATT_HPS + h, i, 0)),
                  pl.BlockSpec((None, ATT_HPS * LANE, PAST), lambda i, h: (smp(i), h, 0)),
                  pl.BlockSpec((None, None, PAST, DIFF_H, LANE), lambda i, h: (smp(i), 0, 0, 0, 0)),
                  pl.BlockSpec((T_S, LANE), lambda i, h: (0, 0)),
                  pl.BlockSpec((T_S, LANE), lambda i, h: (0, 0)),
                  pl.BlockSpec((1, LANE), lambda i, h: (0, 0)),
                  pl.BlockSpec((None, SEQ_BLK, LANE), lambda i, h: (C_RQ + h, i, 0)),
                  pl.BlockSpec((None, SEQ_BLK, LANE), lambda i, h: (C_RK + h, i, 0)),
                  pl.BlockSpec((2, SEQ_BLK, LANE), lambda i, h: (C_RV // 2 + h, i, 0)),
                  pl.BlockSpec((2, SEQ_BLK, LANE), lambda i, h: (C_RG // 2 + h, i, 0)),
                  state_in, state_in,
                  pl.BlockSpec((1, RET_VD), lambda i, h: (0, h)),
                  *side_specs],
        out_specs=(pl.BlockSpec((SEQ_BLK, ATT_HPS * LANE), lambda i, h: (i, h)),
                   pl.BlockSpec((SEQ_BLK, RET_VD), lambda i, h: (i, h)),
                   state_out, state_out,
                   *side_specs,
                   pl.BlockSpec((zrows, XW), row_slice)),
        scratch_shapes=[pltpu.VMEM((ATT_HPS, T_S, LANE), BF16),
                        pltpu.VMEM((ATT_HPS, T_S, LANE), BF16),
                        pltpu.VMEM((ATT_HPS, T_S + PAST, 2 * LANE), BF16),
                        pltpu.VMEM((RET_H, T_P, T_P), F32)],
        compiler_params=_cparams(("arbitrary", "arbitrary")),
        name="mixers",
    )(aux, proj, proj, proj, cache_k, cache_v, cos_t, sin_t, subln_g,
      proj, proj, proj, proj, s0f, s0b, gnorm, *side_weights)


def _rope_tables():
    n_rows = T_S // GRID_W
    row = np.repeat(np.arange(n_rows), GRID_W).astype(np.float64)
    col = np.tile(np.arange(GRID_W), n_rows).astype(np.float64)
    n_freq = DIFF_HD // 4
    inv = ROPE_BASE ** (-np.arange(n_freq, dtype=np.float64) / n_freq)

    def axis_tables(pos):
        ang = pos[:, None] * inv[None, :]
        c = np.cos(ang)
        s = np.sin(ang)
        return np.concatenate([c, c], axis=-1), np.concatenate([-s, s], axis=-1)

    cr, sr = axis_tables(row)
    cc, sc = axis_tables(col)
    cos_h = np.concatenate([cr, cc], axis=-1)
    sin_h = np.concatenate([sr, sc], axis=-1)
    return (jnp.asarray(np.concatenate([cos_h, cos_h], axis=-1), F32),
            jnp.asarray(np.concatenate([sin_h, sin_h], axis=-1), F32))


OP_TM = 512
OP_NPT = N_PROMPT // OP_TM


def _outproj_kernel(ra_ref, da_ref, gr_ref, gd_ref, xp_ref, xs_ref, mod_ref, n2_ref,
                    wro32_ref, wdo32_ref, wo32_ref, rgw_ref, rew_ref, brt_ref,
                    x1_ref, h2_ref, cw8_ref, info_ref, infot_ref, off_ref, pctab_ref, steps_ref,
                    m_scr, wro_ref, wdo_ref, wo_ref, wrt_ref, pc_scr, lg_scr):
    i = pl.program_id(0)

    @pl.when(i == 0)
    def _():
        wro_ref[...] = wro32_ref[...].astype(BF16)
        wdo_ref[...] = wdo32_ref[...].astype(BF16)
        wo_ref[...] = wo32_ref[...].astype(BF16)
        gap = jnp.zeros((ROUTE_E0 - N_GROUPS, D), F32)
        pad = jnp.zeros((ROUTE_W - ROUTE_E0 - N_EXP, D), F32)
        wr = jnp.concatenate([rgw_ref[...], gap, rew_ref[...], pad], axis=0)
        wrt_ref[...] = wr.astype(BF16)

        lg_scr[...] = jnp.zeros_like(lg_scr)

    def route_previous():
        cw8, info, pc, info_t = _route_cols(lg_scr[...])
        cw8_ref[...] = cw8
        info_ref[...] = info
        infot_ref[...] = info_t
        pc_scr[pl.ds(jnp.maximum(i - 1, 0), 1), :] = pc
        return cw8, info

    @pl.when(i < RT_NT)
    def _():
        routed = route_previous()
        ret_out = jnp.dot(ra_ref[...], wro_ref[...], preferred_element_type=F32)
        diff_out = jnp.dot(da_ref[...], wdo_ref[...], preferred_element_type=F32)
        for c in range(D // LANE):
            sl = slice(c * LANE, (c + 1) * LANE)
            m = (jax.nn.sigmoid(gr_ref[c].astype(F32)) * ret_out[:, sl]
                 + jax.nn.sigmoid(gd_ref[c].astype(F32)) * diff_out[:, sl])
            if c == 0:
                m = _ordered_after(_ordered_after(m, routed[0]), routed[1])
            m_scr[:, sl] = m.astype(BF16)
        mix = jnp.dot(m_scr[...], wo_ref[...], preferred_element_type=F32)
        mod = mod_ref[...]

        x1 = jnp.where(i < OP_NPT, xp_ref[...], xs_ref[...]) + mod[:, 2 * D:3 * D] * mix
        x1_ref[...] = x1
        h2 = (_rms(x1) * n2_ref[...] * (1.0 + mod[:, 4 * D:5 * D]) + mod[:, 3 * D:4 * D]).astype(BF16)
        h2_ref[...] = h2
        lg_scr[...] = lax.dot_general(wrt_ref[...], h2, (((1,), (1,)), ((), ())),
                                      preferred_element_type=F32) + brt_ref[...]

    @pl.when(i == RT_NT)
    def _():
        route_previous()
        off_ref[...], pctab_ref[...], steps_ref[...] = _dispatch_plan(pc_scr[...])


def _outproj(ret_act, diff_act, proj, xp, xs, mod3, n2, wro, wdo, wo, rgw, rew, brt):
    assert OP_TM == RT_TM
    npt = OP_NPT
    nt = N_TOK // OP_TM
    cur = lambda i: jnp.minimum(i, nt - 1)
    prev = lambda i: jnp.maximum(i - 1, 0)
    full = lambda i: (0, 0)
    once = pl.Buffered(1)
    return pl.pallas_call(
        _outproj_kernel,
        out_shape=(jax.ShapeDtypeStruct((N_TOK, D), F32),
                   jax.ShapeDtypeStruct((N_TOK, D), BF16),
                   jax.ShapeDtypeStruct((N_TOK, ROUTE_W), F32),
                   jax.ShapeDtypeStruct((N_TOK, ROUTE_W), F32),
                   jax.ShapeDtypeStruct((RT_NT, EPG, RT_TM), F32),
                   jax.ShapeDtypeStruct((RT_NT, ROUTE_W), jnp.int32),
                   jax.ShapeDtypeStruct((RT_NT, ROUTE_W), jnp.int32),
                   jax.ShapeDtypeStruct((PLAN_ROWS, ROUTE_W), jnp.int32)),
        grid=(nt + 1,),
        in_specs=[pl.BlockSpec((OP_TM, D), lambda i: (cur(i), 0)),
                  pl.BlockSpec((OP_TM, D), lambda i: (cur(i), 0)),
                  pl.BlockSpec((8, OP_TM, LANE), lambda i: (C_GR // 8, cur(i), 0)),
                  pl.BlockSpec((8, OP_TM, LANE), lambda i: (C_GD // 8, cur(i), 0)),
                  pl.BlockSpec((OP_TM, D), lambda i: (jnp.minimum(i, npt - 1), 0)),
                  pl.BlockSpec((OP_TM, D), lambda i: (jnp.maximum(cur(i) - npt, 0), 0)),
                  pl.BlockSpec((None, 1, 6 * D), lambda i: (_mod_row(OP_TM)(cur(i)), 0, 0)),
                  pl.BlockSpec((1, D), full),
                  pl.BlockSpec((D, D), full, pipeline_mode=once),
                  pl.BlockSpec((D, D), full, pipeline_mode=once),
                  pl.BlockSpec((D, D), full, pipeline_mode=once),
                  pl.BlockSpec((N_GROUPS, D), full, pipeline_mode=once),
                  pl.BlockSpec((N_EXP, D), full, pipeline_mode=once),
                  pl.BlockSpec((ROUTE_W, 1), full)],
        out_specs=(pl.BlockSpec((OP_TM, D), lambda i: (cur(i), 0)),
                   pl.BlockSpec((OP_TM, D), lambda i: (cur(i), 0)),
                   pl.BlockSpec((OP_TM, ROUTE_W), lambda i: (prev(i), 0)),
                   pl.BlockSpec((OP_TM, ROUTE_W), lambda i: (prev(i), 0)),
                   pl.BlockSpec((None, EPG, RT_TM), lambda i: (prev(i), 0, 0)),
                   pl.BlockSpec((RT_NT, ROUTE_W), full),
                   pl.BlockSpec((RT_NT, ROUTE_W), full),
                   pl.BlockSpec((PLAN_ROWS, ROUTE_W), full)),
        scratch_shapes=[pltpu.VMEM((OP_TM, D), BF16),
                        pltpu.VMEM((D, D), BF16), pltpu.VMEM((D, D), BF16), pltpu.VMEM((D, D), BF16),
                        pltpu.VMEM((ROUTE_W, D), BF16),
                        pltpu.VMEM((RT_NT, ROUTE_W), F32),
                        pltpu.VMEM((ROUTE_W, RT_TM), F32)],
        compiler_params=_cparams(("arbitrary",)),
        name="outproj",
    )(ret_act, diff_act, proj, proj, xp, xs, mod3, n2, wro, wdo, wo, rgw, rew, brt)


RT_TM = 512
RT_NT = N_TOK // RT_TM
PIECE = 16
R_LOC = RT_TM + N_GROUPS * PIECE
R_STAGE = 640
EX_TM = 512
R_MAX = 11264
EX_NT = R_MAX // EX_TM


ROUTE_E0 = EPG


def _route_cols(lgt):
    row = lax.broadcasted_iota(jnp.int32, (EPG, RT_TM), 0)
    neg = jnp.float32(-jnp.inf)

    def first_row(cond):
        return jnp.min(jnp.where(cond, row, EPG), axis=0, keepdims=True)

    head = lgt[0:EPG]
    is_g = row < N_GROUPS
    gl = jnp.where(is_g, head, neg)
    gmax = jnp.max(gl, axis=0, keepdims=True)
    gsum = jnp.sum(jnp.where(is_g, jnp.exp(head - gmax), 0.0), axis=0, keepdims=True)
    p_top = 1.0 / gsum
    g_idx = first_row(gl == gmax)
    el = lgt[ROUTE_E0 + (N_GROUPS - 1) * EPG:ROUTE_E0 + N_GROUPS * EPG]
    for g in reversed(range(N_GROUPS - 1)):
        el = jnp.where(g_idx == g, lgt[ROUTE_E0 + g * EPG:ROUTE_E0 + (g + 1) * EPG], el)
    ee = jnp.exp(el - jnp.max(el, axis=0, keepdims=True))
    ep = ee / jnp.sum(ee, axis=0, keepdims=True)
    e1 = jnp.max(ep, axis=0, keepdims=True)
    i1 = first_row(ep == e1)
    ep2 = jnp.where(row == i1, -1.0, ep)
    e2 = jnp.max(ep2, axis=0, keepdims=True)
    i2 = first_row(ep2 == e2)
    den = e1 + e2
    cw8_t = (jnp.where(row == i1, p_top * e1 / den, 0.0)
             + jnp.where(row == i2, p_top * e2 / den, 0.0))

    onehot = (row == g_idx).astype(F32)
    ii = lax.broadcasted_iota(jnp.int32, (RT_TM, RT_TM), 0)
    jj = lax.broadcasted_iota(jnp.int32, (RT_TM, RT_TM), 1)
    earlier = (ii < jj).astype(BF16)
    prefix = jnp.dot(onehot.astype(BF16), earlier, preferred_element_type=F32)
    cnt = jnp.sum(onehot, axis=1, keepdims=True)
    pc_col = jnp.floor((cnt + (PIECE - 1.0)) * (1.0 / PIECE)) * PIECE
    row1 = lax.broadcasted_iota(jnp.int32, (EPG, 1), 0)
    lane1 = lax.broadcasted_iota(jnp.int32, (1, ROUTE_W), 1)
    lo = jnp.zeros((EPG, 1), F32)
    run = jnp.zeros((1, 1), F32)
    pc = jnp.zeros((1, ROUTE_W), F32)
    for g in range(N_GROUPS):
        pc_g = jnp.sum(jnp.where(row1 == g, pc_col, 0.0), axis=0, keepdims=True)
        lo = jnp.where(row1 == g, run, lo)
        pc = jnp.where(lane1 == g, pc_g, pc)
        run = run + pc_g
    dest = jnp.sum(onehot * (prefix + lo), axis=0, keepdims=True)
    info_t = jnp.where(row == 0, g_idx.astype(F32), jnp.where(row == 1, dest, 0.0))

    slab = jnp.concatenate([cw8_t, info_t, jnp.zeros((LANE - 2 * EPG, RT_TM), F32)], axis=0)
    cols = slab.T
    lane = lax.broadcasted_iota(jnp.int32, cols.shape, 1)
    cw8 = jnp.where(lane < EPG, cols, 0.0)
    info = jnp.where(lane < 2, pltpu.roll(cols, LANE - EPG, 1), 0.0)
    return cw8, info, pc, info_t


PLAN_ROWS = 32


def _dispatch_plan(pc_all):
    lane = lax.broadcasted_iota(jnp.int32, (1, ROUTE_W), 1)
    is_g = lane < N_GROUPS
    seg_len = jnp.sum(pc_all, axis=0, keepdims=True)
    seg_pad = jnp.floor((seg_len + (EX_TM - 1.0)) * (1.0 / EX_TM)) * EX_TM
    run = seg_pad + pltpu.roll(seg_pad, 1, 1)
    seg_end_pad = jnp.where(is_g, run + pltpu.roll(run, 2, 1), 0.0)
    seg_start = seg_end_pad - seg_pad
    ti = lax.broadcasted_iota(jnp.int32, (RT_NT, RT_NT), 0)
    tj = lax.broadcasted_iota(jnp.int32, (RT_NT, RT_NT), 1)
    earlier = (tj < ti).astype(BF16)
    within = jnp.dot(earlier, pc_all.astype(BF16), preferred_element_type=F32)
    chunk_off = jnp.where(is_g, seg_start + within, 0.0)

    start = (lax.broadcasted_iota(jnp.int32, (PLAN_ROWS, 1), 0) * EX_TM).astype(F32)
    passed = jnp.where(jnp.logical_and(is_g, start >= seg_end_pad), 1.0, 0.0)
    group = jnp.minimum(jnp.sum(passed, axis=1, keepdims=True), N_GROUPS - 1.0)
    used_end = jnp.sum(jnp.where(lane == group.astype(jnp.int32), seg_start + seg_len, 0.0),
                       axis=1, keepdims=True)
    used = jnp.clip(used_end - start, 0.0, float(EX_TM))
    steps = jnp.where(lane == 0, group, jnp.where(lane == 1, used, 0.0))
    return chunk_off.astype(jnp.int32), pc_all.astype(jnp.int32), steps.astype(jnp.int32)


BIG = 64


def _piece_copies(off_ref, pc_ref, tile, make):
    lo = 0
    for g in range(N_GROUPS):
        rows = pc_ref[tile, g]
        n_big = rows // BIG
        for size, first, n in ((BIG, 0, n_big), (PIECE, n_big * BIG, (rows - n_big * BIG) // PIECE)):
            def body(j, carry, size=size, lo=lo + first, base=off_ref[tile, g] + first):
                make(pl.multiple_of(lo + j * size, PIECE), pl.multiple_of(base + j * size, PIECE), size)
                return carry

            lax.fori_loop(0, n, body, 0)
        lo = lo + rows


def _piece_waits(pc_ref, tile, wait):
    n_big = 0
    n_small = 0
    for g in range(N_GROUPS):
        rows = pc_ref[tile, g]
        n_big = n_big + rows // BIG
        n_small = n_small + (rows % BIG) // PIECE
    for size, n in ((BIG, n_big), (PIECE, n_small)):
        def body(j, carry, size=size):
            wait(size)
            return carry

        lax.fori_loop(0, n, body, 0)


XW = D + ROUTE_W


def _dispatch_kernel(off_ref, pc_ref, h_ref, infot_ref, cw8_ref, xs_in, xs_out, x_scr, sem):
    del xs_in
    i = pl.program_id(0)
    slot = i % 2
    dest = infot_ref[1:2, :]
    row = lax.broadcasted_iota(jnp.int32, (R_LOC, RT_TM), 0).astype(F32)
    sel = (row == dest).astype(BF16)
    cw = cw8_ref[...]
    hi = cw.astype(BF16).astype(F32)
    mid = (cw - hi).astype(BF16).astype(F32)
    low = (cw - hi - mid).astype(BF16).astype(F32)
    pieces = (hi + pltpu.roll(mid, EPG, 1) + pltpu.roll(low, 2 * EPG, 1)).astype(BF16)
    rows = jnp.concatenate([h_ref[...], pieces], axis=1)
    x_scr[slot] = jnp.dot(sel, rows, preferred_element_type=F32).astype(BF16)

    def x_copy(s, src, dst, rows):
        return pltpu.make_async_copy(x_scr.at[s, pl.ds(src, rows)], xs_out.at[pl.ds(dst, rows)],
                                     sem.at[s])

    _piece_copies(off_ref, pc_ref, i, lambda src, dst, rows: x_copy(slot, src, dst, rows).start())

    def wait_tile(tile, s):
        _piece_waits(pc_ref, tile, lambda rows: x_copy(s, 0, 0, rows).wait())

    @pl.when(i > 0)
    def _():
        wait_tile(i - 1, 1 - slot)

    @pl.when(i == RT_NT - 1)
    def _():
        wait_tile(i, slot)


def _dispatch(chunk_off, pc, h2, info_t, cw8, xs0):
    grid_spec = pltpu.PrefetchScalarGridSpec(
        num_scalar_prefetch=2,
        grid=(RT_NT,),
        in_specs=[pl.BlockSpec((RT_TM, D), lambda i, o, p: (i, 0)),
                  pl.BlockSpec((None, EPG, RT_TM), lambda i, o, p: (i, 0, 0)),
                  pl.BlockSpec((RT_TM, ROUTE_W), lambda i, o, p: (i, 0)),
                  pl.BlockSpec(memory_space=pl.ANY)],
        out_specs=pl.BlockSpec(memory_space=pl.ANY),
        scratch_shapes=[pltpu.VMEM((2, R_LOC, XW), BF16),
                        pltpu.SemaphoreType.DMA((2,))])
    return pl.pallas_call(
        _dispatch_kernel,
        out_shape=jax.ShapeDtypeStruct((R_MAX, XW), BF16),
        grid_spec=grid_spec,
        input_output_aliases={5: 0},
        compiler_params=_cparams(("arbitrary",)),
        name="dispatch",
    )(chunk_off, pc, h2, info_t, cw8, xs0)


def _expert_kernel(steps_ref, x_ref, wg_ref, wu_ref, wd_ref, y_ref):
    k = pl.program_id(0)
    used = steps_ref[k, 1]

    def run(rows):
        x = x_ref[rows, 0:D]
        cw = x_ref[rows, D:XW].astype(F32)
        lane = lax.broadcasted_iota(jnp.int32, cw.shape, 1)
        acc = jnp.zeros(x.shape, F32)
        for j in range(EPG):
            mine = jnp.logical_and((lane & (EPG - 1)) == j, lane < 3 * EPG)
            w = jnp.sum(jnp.where(mine, cw, 0.0), axis=-1, keepdims=True)
            a = (jax.nn.silu(jnp.dot(x, wg_ref[j], preferred_element_type=F32))
                 * jnp.dot(x, wu_ref[j], preferred_element_type=F32))
            acc = acc + jnp.dot((a * w).astype(BF16), wd_ref[j], preferred_element_type=F32)
        y_ref[rows, :] = acc.astype(BF16)

    @pl.when(used > 0)
    def _():
        run(slice(0, EX_TM))

    @pl.when(used == 0)
    def _():
        y_ref[...] = jnp.zeros_like(y_ref)


def _experts(steps, xs, wg, wu, wd):
    grid_spec = pltpu.PrefetchScalarGridSpec(
        num_scalar_prefetch=1,
        grid=(EX_NT,),
        in_specs=[pl.BlockSpec((EX_TM, XW), lambda k, st: (k, 0)),
                  pl.BlockSpec((EPG, D, FF), lambda k, st: (st[k, 0], 0, 0)),
                  pl.BlockSpec((EPG, D, FF), lambda k, st: (st[k, 0], 0, 0)),
                  pl.BlockSpec((EPG, FF, D), lambda k, st: (st[k, 0], 0, 0))],
        out_specs=pl.BlockSpec((EX_TM, D), lambda k, st: (k, 0)))
    return pl.pallas_call(
        _expert_kernel,
        out_shape=jax.ShapeDtypeStruct((R_MAX, D), BF16),
        grid_spec=grid_spec,
        compiler_params=_cparams(("arbitrary",)),
        name="experts",
    )(steps, xs, wg, wu, wd)


CB_NPT = N_PROMPT // RT_TM


def _combine_kernel(off_ref, pc_ref, info_ref, x1_ref, mod_ref, g_ref, ys_hbm,
                    yp_ref, ysm_ref, stage, sem):
    i = pl.program_id(0)
    slot = i % 2

    def copy(s, dst, src, rows):
        return pltpu.make_async_copy(ys_hbm.at[pl.ds(src, rows)], stage.at[s, pl.ds(dst, rows)],
                                     sem.at[s])

    def fetch(tile, s):
        _piece_copies(off_ref, pc_ref, tile, lambda loc, glob, rows: copy(s, loc, glob, rows).start())

    @pl.when(i == 0)
    def _():
        stage[...] = jnp.zeros_like(stage)
        fetch(0, 0)

    @pl.when(i + 1 < RT_NT)
    def _():
        fetch(i + 1, 1 - slot)

    _piece_waits(pc_ref, i, lambda rows: copy(slot, 0, 0, rows).wait())

    dest = info_ref[...][:, 1:2]
    col = lax.broadcasted_iota(jnp.int32, (RT_TM, R_STAGE), 1).astype(F32)
    moe = jnp.dot((col == dest).astype(BF16), stage[slot], preferred_element_type=F32)
    mod = mod_ref[...]
    out = _rms(x1_ref[...] + mod[:, 5 * D:6 * D] * moe) * g_ref[...]

    @pl.when(i < CB_NPT)
    def _():
        yp_ref[...] = out

    @pl.when(i >= CB_NPT)
    def _():
        ysm_ref[...] = out


def _combine(chunk_off, pc, info, x1, mod3, fg, ys):
    npt = CB_NPT
    grid_spec = pltpu.PrefetchScalarGridSpec(
        num_scalar_prefetch=2,
        grid=(RT_NT,),
        in_specs=[pl.BlockSpec((RT_TM, ROUTE_W), lambda i, o, p: (i, 0)),
                  pl.BlockSpec((RT_TM, D), lambda i, o, p: (i, 0)),
                  pl.BlockSpec((None, 1, 6 * D), lambda i, o, p: (_mod_row(RT_TM)(i), 0, 0)),
                  pl.BlockSpec((1, D), lambda i, o, p: (0, 0)),
                  pl.BlockSpec(memory_space=pl.ANY)],
        out_specs=(pl.BlockSpec((RT_TM, D), lambda i, o, p: (jnp.minimum(i, npt - 1), 0)),
                   pl.BlockSpec((RT_TM, D), lambda i, o, p: (jnp.maximum(i - npt, 0), 0))),
        scratch_shapes=[pltpu.VMEM((2, R_STAGE, D), BF16),
                        pltpu.SemaphoreType.DMA((2,))])
    return pl.pallas_call(
        _combine_kernel,
        out_shape=(jax.ShapeDtypeStruct((N_PROMPT, D), F32),
                   jax.ShapeDtypeStruct((N_SAMPLE, D), F32)),
        grid_spec=grid_spec,
        compiler_params=_cparams(("arbitrary",)),
        name="combine",
    )(chunk_off, pc, info, x1, mod3, fg, ys)


def kernel(x_prompt, x_sample, cache_diff_k, cache_diff_v, state_ret_fwd, state_ret_bwd, c, c_ctx,
           w_ada, b_ada, norm1_g, norm2_g, w_in, ret_decay_fwd, ret_decay_bwd, ret_norm_g,
           diff_lambda_q1, diff_lambda_k1, diff_lambda_q2, diff_lambda_k2, diff_subln_g,
           w_ret_o, w_diff_o, w_o, router_group_w, router_group_b, router_expert_w, router_expert_b,
           moe_w_gate, moe_w_up, moe_w_down, final_norm_g):
    l = 0
    lam_init = 0.8 - 0.6 * math.exp(-0.3 * l)
    row = lambda a: a[l].astype(F32)[None, :]

    xp = x_prompt.reshape(N_PROMPT, D)
    xs = x_sample.reshape(N_SAMPLE, D)
    mod3, aux, brt = _modulation(
        lam_init, c_ctx[None, :], c, w_ada[l], b_ada[l][None, :],
        [row(diff_lambda_q1), row(diff_lambda_k1), row(diff_lambda_q2), row(diff_lambda_k2),
         row(ret_decay_fwd), row(ret_decay_bwd), row(router_group_b), row(router_expert_b)])

    proj, kt32, v32 = _inproj(xp, xs, mod3, norm1_g[l][None, :], w_in[l])

    cos_t, sin_t = _rope_tables()
    cache_kt = jnp.transpose(cache_diff_k[:, l], (0, 2, 3, 4, 1)).reshape(4, D, PAST)
    (diff_act, ret_act, s_f, s_b, wg_bf, wu_bf, wd_bf, xs0) = _mixers(
        aux, proj, cache_kt, cache_diff_v, cos_t, sin_t, diff_subln_g[l][None, :],
        state_ret_fwd[:, l], state_ret_bwd[:, l], ret_norm_g[l][None, :],
        (moe_w_gate[l], moe_w_up[l], moe_w_down[l]))

    x1, h2, cw8, info, info_t, chunk_off, pc, steps = _outproj(
        ret_act, diff_act, proj, xp, xs, mod3, norm2_g[l][None, :],
        w_ret_o[l], w_diff_o[l], w_o[l], router_group_w[l].T, router_expert_w[l].T, brt)
    xs_sorted = _dispatch(chunk_off, pc, h2, info_t, cw8, xs0)
    y_sorted = _experts(steps, xs_sorted, wg_bf, wu_bf, wd_bf)
    yp, ys = _combine(chunk_off, pc, info, x1, mod3, final_norm_g[None, :], y_sorted)

    return (yp.reshape(16, T_P, D), ys.reshape(4, T_S, D),
            jnp.transpose(kt32.reshape(16, DIFF_H, 2, DIFF_HD, T_P), (0, 4, 1, 2, 3))[:, None],
            v32.reshape(16, 1, T_P, DIFF_H, 2 * DIFF_HD),
            s_f.reshape(16, 1, RET_H, RET_KD, RET_VD), s_b.reshape(16, 1, RET_H, RET_KD, RET_VD))
```

```python
import functools
import math

import jax
import jax.numpy as jnp
import numpy as np
from jax import lax
from jax.experimental import pallas as pl
from jax.experimental.pallas import tpu as pltpu

F32 = jnp.float32
BF16 = jnp.bfloat16

D = 1024
N_PROMPT = 16 * 256
N_SAMPLE = 4 * 1024
N_TOK = N_PROMPT + N_SAMPLE
T_P = 256
T_S = 1024
PAST = 512
GRID_W = 64
RET_H = 4
RET_KD = 128
RET_VD = 256
DIFF_H = 8
DIFF_HD = 64
ROPE_BASE = 10000.0
N_GROUPS = 4
EPG = 8
N_EXP = 32
FF = 256
EPS = 1e-6
IN_W = 8192
LANE = 128
N_CHUNK = IN_W // LANE
C_RQ, C_RK, C_RV, C_RG, C_DQ, C_DK, C_DV, C_GR, C_GD = 0, 4, 8, 16, 24, 32, 40, 48, 56
ROUTE_W = 128
SEQ_BLK = 1024
N_PBLK = N_PROMPT // SEQ_BLK
V7X_VMEM_BYTES = 64 * 1024 * 1024
VMEM_LIMIT = V7X_VMEM_BYTES - 8 * 1024 * 1024


def _cparams(sem):
    return pltpu.CompilerParams(dimension_semantics=sem, vmem_limit_bytes=VMEM_LIMIT)


def _mod_row(tile_rows):
    def f(i):
        start = i * tile_rows
        return jnp.where(start < N_PROMPT, 0, 1 + (start - N_PROMPT) // T_S)
    return f


def _rms(x):
    return x * lax.rsqrt(jnp.mean(x * x, axis=-1, keepdims=True) + EPS)


def _ordered_after(dst, src):
    m = None
    for a in range(src.shape[0] // 8):
        for b in range(src.shape[1] // LANE):
            part = src[a * 8:(a + 1) * 8, b * LANE:(b + 1) * LANE]
            m = part if m is None else jnp.maximum(m, part)
    never = jnp.logical_and(m != m, m == m)
    return jnp.where(jnp.tile(never, (dst.shape[0] // 8, 1)), 0.0, dst)


MOD_ROWS = 8


AUX_LAM = 0
AUX_LG = 1


def _mod_kernel(lam_init, ctx_ref, c_ref, w_ref, b_ref, lq1_ref, lk1_ref, lq2_ref, lk2_ref, df_ref, db_ref,
                gb_ref, eb_ref, o_ref, aux_ref, brt_ref):
    cond = jnp.concatenate([ctx_ref[...], c_ref[...],
                            jnp.zeros((MOD_ROWS - 1 - c_ref.shape[0], D), F32)], axis=0)
    s = jax.nn.silu(cond)
    out = jnp.dot(s.astype(BF16), w_ref[...].astype(BF16), preferred_element_type=F32) + b_ref[...]
    for r in range(MOD_ROWS):
        o_ref[r] = out[r:r + 1, :]

    def lane_row(parts):
        used = sum(p.shape[1] for p in parts)
        return jnp.concatenate(parts + [jnp.zeros((1, LANE - used), F32)], axis=1)

    lam = (jnp.exp(jnp.sum(lq1_ref[...] * lk1_ref[...], axis=1, keepdims=True))
           - jnp.exp(jnp.sum(lq2_ref[...] * lk2_ref[...], axis=1, keepdims=True)) + lam_init)
    rows = [lane_row([lam, jnp.full((1, 1), 1.0 - lam_init, F32)]),
            lane_row([jax.nn.log_sigmoid(df_ref[...])]),
            lane_row([jax.nn.log_sigmoid(db_ref[...])])]
    aux_ref[...] = jnp.concatenate(rows + [jnp.zeros((8 - len(rows), LANE), F32)], axis=0)
    bias = lane_row([gb_ref[...], jnp.zeros((1, ROUTE_E0 - N_GROUPS), F32), eb_ref[...]])
    brt_ref[...] = jnp.tile(bias, (8, 1)).T[:, 0:1]


def _modulation(lam_init, c_ctx, c, w_ada, b_ada, small):
    tn = 1536
    whole = lambda a: pl.BlockSpec(a.shape, lambda j: (0, 0))
    return pl.pallas_call(
        functools.partial(_mod_kernel, lam_init),
        out_shape=(jax.ShapeDtypeStruct((MOD_ROWS, 1, 6 * D), F32),
                   jax.ShapeDtypeStruct((8, LANE), F32),
                   jax.ShapeDtypeStruct((ROUTE_W, 1), F32)),
        grid=(6 * D // tn,),
        in_specs=[pl.BlockSpec((1, D), lambda j: (0, 0)),
                  pl.BlockSpec(c.shape, lambda j: (0, 0)),
                  pl.BlockSpec((D, tn), lambda j: (0, j)),
                  pl.BlockSpec((1, tn), lambda j: (0, j))] + [whole(a) for a in small],
        out_specs=(pl.BlockSpec((MOD_ROWS, 1, tn), lambda j: (0, 0, j)),
                   pl.BlockSpec((8, LANE), lambda j: (0, 0)),
                   pl.BlockSpec((ROUTE_W, 1), lambda j: (0, 0))),
        compiler_params=_cparams(("arbitrary",)),
        name="mod",
    )(c_ctx, c, w_ada, b_ada, *small)


IP_TM = 512
IP_TN = 2048
IP_NPT = N_PROMPT // IP_TM


IP_KV_TILE = C_DK * LANE // IP_TN
IP_SPT = IP_TM // T_P


def _inproj_kernel(xp_ref, xs_ref, mod_ref, n1_ref, w_ref, proj_ref, kt_ref, v32_ref, w_scr):
    j = pl.program_id(0)
    i = pl.program_id(1)

    @pl.when(i == 0)
    def _():
        w_scr[...] = w_ref[...].astype(BF16)

    x = jnp.where(i < IP_NPT, xp_ref[...], xs_ref[...])
    mod = mod_ref[...]
    h = (_rms(x) * n1_ref[...] * (1.0 + mod[:, D:2 * D]) + mod[:, 0:D]).astype(BF16)
    acc = jnp.dot(h, w_scr[...], preferred_element_type=F32)
    for c in range(IP_TN // LANE):
        proj_ref[c] = acc[:, c * LANE:(c + 1) * LANE].astype(BF16)

    @pl.when(jnp.logical_and(j == IP_KV_TILE, i < IP_NPT))
    def _():
        for s in range(IP_SPT):
            kt_ref[s] = acc[s * T_P:(s + 1) * T_P, :D].T
        v32_ref[...] = acc[:, D:]


def _inproj(xp, xs, mod3, n1, w_in):
    npt = IP_NPT
    cpt = IP_TN // LANE

    def kv_tile(j, i):
        return jnp.where(j < IP_KV_TILE, 0, jnp.where(j == IP_KV_TILE, jnp.minimum(i, npt - 1), npt - 1))

    return pl.pallas_call(
        _inproj_kernel,
        out_shape=(jax.ShapeDtypeStruct((N_CHUNK, N_TOK, LANE), BF16),
                   jax.ShapeDtypeStruct((N_PROMPT // T_P, D, T_P), F32),
                   jax.ShapeDtypeStruct((N_PROMPT, D), F32)),
        grid=(IN_W // IP_TN, N_TOK // IP_TM),
        in_specs=[pl.BlockSpec((IP_TM, D), lambda j, i: (jnp.minimum(i, npt - 1), 0)),
                  pl.BlockSpec((IP_TM, D), lambda j, i: (jnp.maximum(i - npt, 0), 0)),
                  pl.BlockSpec((None, 1, 6 * D), lambda j, i: (_mod_row(IP_TM)(i), 0, 0)),
                  pl.BlockSpec((1, D), lambda j, i: (0, 0)),
                  pl.BlockSpec((D, IP_TN), lambda j, i: (0, j))],
        out_specs=(pl.BlockSpec((cpt, IP_TM, LANE), lambda j, i: (j, i, 0)),
                   pl.BlockSpec((IP_SPT, D, T_P), lambda j, i: (kv_tile(j, i), 0, 0)),
                   pl.BlockSpec((IP_TM, D), lambda j, i: (kv_tile(j, i), 0))),
        scratch_shapes=[pltpu.VMEM((D, IP_TN), BF16)],
        compiler_params=_cparams(("arbitrary", "arbitrary")),
        name="inproj",
    )(xp, xs, mod3, n1, w_in)


def _decay_mask(t, lgf, lgb):
    ii = lax.broadcasted_iota(jnp.int32, (t, t), 0)
    jj = lax.broadcasted_iota(jnp.int32, (t, t), 1)
    rel = (ii - jj).astype(F32)
    e = jnp.exp(jnp.where(rel >= 0.0, lgf, -lgb) * rel)
    return jnp.where(rel == 0.0, 2.0, e) * (RET_KD ** -0.5)


def _ret_parts(lgf, lgb, q_ref, k_ref, v_ref, rg_ref, s0f_ref, s0b_ref, g_ref,
               o_ref, sf_ref, sb_ref, dm_ref):
    gain = g_ref[...]
    nt = (((1,), (1,)), ((), ()))
    tn = (((0,), (0,)), ((), ()))

    def finish(o, rg):
        d = o - jnp.mean(o, axis=-1, keepdims=True)
        y = d * lax.rsqrt(jnp.mean(d * d, axis=-1, keepdims=True) + EPS) * gain
        return (jax.nn.silu(rg.astype(F32)) * y).astype(BF16)

    def vcat(r):
        return jnp.concatenate([v_ref[0, r, :], v_ref[1, r, :]], axis=1)

    def gcat(r):
        return jnp.concatenate([rg_ref[0, r, :], rg_ref[1, r, :]], axis=1)

    def build_mask():
        dm_ref[...] = _decay_mask(T_P, lgf, lgb)

    n_chunk = SEQ_BLK // T_P
    chunks = [slice(c * T_P, (c + 1) * T_P) for c in range(n_chunk)]
    t = lax.broadcasted_iota(jnp.int32, (T_P, 1), 0).astype(F32)
    kdf = jnp.exp(lgf * (T_P - 1.0 - t)) * (RET_KD ** -0.5)
    kdb = jnp.exp(lgb * t) * (RET_KD ** -0.5)

    def intra(r):
        sc = lax.dot_general(q_ref[r, :], k_ref[r, :], nt, preferred_element_type=F32)
        return jnp.dot((sc * dm_ref[...]).astype(BF16), vcat(r), preferred_element_type=F32)

    def key_state(r, kd):
        kw = (k_ref[r, :].astype(F32) * kd).astype(BF16)
        return lax.dot_general(kw, vcat(r), tn, preferred_element_type=F32)

    def context_part():
        for s, r in enumerate(chunks):
            o_ref[r, :] = finish(intra(r), gcat(r))
            sf_ref[s] = key_state(r, kdf)
            sb_ref[s] = key_state(r, kdb)

    def latent_part():
        qdf = jnp.exp(lgf * (t + 1.0))
        qdb = jnp.exp(lgb * (T_P - t))
        span = jnp.full((1, 1), float(T_P), F32)
        cf = jnp.exp(lgf * span)
        cb = jnp.exp(lgb * span)
        before_b = [None] * n_chunk
        state = s0b_ref[...]
        for c in reversed(range(n_chunk)):
            before_b[c] = state
            if c > 0:
                state = cb * state + key_state(chunks[c], kdb)
        state = s0f_ref[...]
        for c, r in enumerate(chunks):
            qf = q_ref[r, :].astype(F32)
            o = (intra(r)
                 + jnp.dot((qf * qdf).astype(BF16), state.astype(BF16), preferred_element_type=F32)
                 + jnp.dot((qf * qdb).astype(BF16), before_b[c].astype(BF16), preferred_element_type=F32))
            o_ref[r, :] = finish(o, gcat(r))
            if c + 1 < n_chunk:
                state = cf * state + key_state(r, kdf)

    return build_mask, context_part, latent_part


ATT_TQ = 1024
ATT_HPS = 2
ATT_QSCALE = (DIFF_HD ** -0.5) * math.log2(math.e)


N_SIDE = 3


def _mixer_kernel(aux_ref, q_ref, k_ref, v_ref, ckt_ref, cv_ref, cos_ref, sin_ref, g_ref,
                  rq_ref, rk_ref, rv_ref, rg_ref, s0f_ref, s0b_ref, gn_ref, *rest):
    side_in = rest[:N_SIDE]
    o_ref, ro_ref, sf_ref, sb_ref = rest[N_SIDE:N_SIDE + 4]
    side_out = rest[N_SIDE + 4:2 * N_SIDE + 4]
    xs0_ref, q_scr, k_scr, v_scr, dm_scr = rest[2 * N_SIDE + 4:]
    i = pl.program_id(0)
    rh = pl.program_id(1)
    build_mask, ret_context, ret_latent = _ret_parts(
        aux_ref[AUX_LG, rh], aux_ref[AUX_LG + 1, rh], rq_ref, rk_ref, rv_ref, rg_ref, s0f_ref, s0b_ref, gn_ref,
        ro_ref, sf_ref, sb_ref, dm_scr.at[rh])

    @pl.when(i == 0)
    def _():
        build_mask()

    def side_jobs():
        for src, dst in zip(side_in, side_out):
            dst[...] = src[...].astype(BF16)
        xs0_ref[...] = jnp.zeros_like(xs0_ref)

    lam = aux_ref[AUX_LAM, 0]
    out_scale = aux_ref[AUX_LAM, 1]
    gain = g_ref[...] * out_scale
    lane = lax.broadcasted_iota(jnp.int32, (1, LANE), 1)
    first = lane < DIFF_HD
    nt = (((1,), (1,)), ((), ()))

    def halves(q):
        zero = jnp.zeros_like(q)
        return jnp.where(first, q, zero), jnp.where(first, zero, q)

    def weights(s):
        return jnp.exp2(s - jnp.max(s, axis=-1, keepdims=True)).astype(BF16)

    def finish(of0, of1):
        o = of0[:, :LANE] / of0[:, LANE:] - lam * (of1[:, :LANE] / of1[:, LANE:])
        return (_rms(o) * gain).astype(BF16)

    @pl.when(i < N_PBLK)
    def _():
        ret_context()
        n_seq = SEQ_BLK // T_P
        rows = [slice(s * T_P, (s + 1) * T_P) for s in range(n_seq)]
        ones = jnp.ones((T_P, LANE), BF16)
        for hh in range(ATT_HPS):
            cols = slice(hh * LANE, (hh + 1) * LANE)
            q0, q1 = halves((q_ref[hh].astype(F32) * ATT_QSCALE).astype(BF16))
            s0 = jnp.concatenate([lax.dot_general(q0[r], k_ref[hh, r, :], nt, preferred_element_type=F32)
                                  for r in rows], axis=0)
            s1 = jnp.concatenate([lax.dot_general(q1[r], k_ref[hh, r, :], nt, preferred_element_type=F32)
                                  for r in rows], axis=0)
            e0 = weights(s0)
            e1 = weights(s1)
            for r in rows:
                v1 = jnp.concatenate([v_ref[hh, r, :], ones], axis=1)
                o_ref[r, cols] = finish(jnp.dot(e0[r], v1, preferred_element_type=F32),
                                        jnp.dot(e1[r], v1, preferred_element_type=F32))

    @pl.when(i >= N_PBLK)
    def _():
        side_jobs()
        ret_latent()
        cos = cos_ref[...]
        sin = sin_ref[...]
        low = (lax.broadcasted_iota(jnp.int32, (T_S, LANE), 1) & 16) == 0

        def rope(x):
            xs = jnp.where(low, pltpu.roll(x, LANE - 16, 1), pltpu.roll(x, 16, 1))
            return x * cos + xs * sin

        for hh in range(ATT_HPS):
            cols = slice(hh * LANE, (hh + 1) * LANE)
            head = pl.program_id(1) * ATT_HPS + hh
            q_scr[hh] = (rope(q_ref[hh].astype(F32)) * ATT_QSCALE).astype(BF16)
            k_scr[hh] = rope(k_ref[hh].astype(F32)).astype(BF16)
            ckt = ckt_ref[cols, :].astype(BF16)
            v_scr[hh, 0:T_S, 0:LANE] = v_ref[hh]
            v_scr[hh, T_S:T_S + PAST, 0:LANE] = cv_ref[:, head, :].astype(BF16)
            v_scr[hh, :, LANE:2 * LANE] = jnp.ones((T_S + PAST, LANE), BF16)

            def scores(qh, hh=hh, ckt=ckt):
                return jnp.concatenate([lax.dot_general(qh, k_scr[hh], nt, preferred_element_type=F32),
                                        jnp.dot(qh, ckt, preferred_element_type=F32)], axis=1)

            for b in range(T_S // ATT_TQ):
                r = slice(b * ATT_TQ, (b + 1) * ATT_TQ)
                q0, q1 = halves(q_scr[hh, r, :])
                e0 = weights(scores(q0))
                e1 = weights(scores(q1))
                o_ref[r, cols] = finish(jnp.dot(e0, v_scr[hh], preferred_element_type=F32),
                                        jnp.dot(e1, v_scr[hh], preferred_element_type=F32))


def _mixers(aux, proj, cache_k, cache_v, cos_t, sin_t, subln_g, s0f, s0b, gnorm, side_weights):
    nb = N_TOK // SEQ_BLK
    spb = SEQ_BLK // T_P
    smp = lambda i: jnp.maximum(i - N_PBLK, 0)
    pmt = lambda i: jnp.minimum(i, N_PBLK - 1)
    pmh = lambda i, h: jnp.where(i < N_PBLK, h, RET_H - 1)
    state_in = pl.BlockSpec((None, None, RET_KD, RET_VD), lambda i, h: (smp(i), h, 0, 0))
    state_out = pl.BlockSpec((spb, None, RET_KD, RET_VD), lambda i, h: (pmt(i), pmh(i, h), 0, 0))

    n_hp = DIFF_H // ATT_HPS
    assert n_hp == RET_H
    n_steps = (nb - N_PBLK) * n_hp
    step = lambda i, h: jnp.maximum((i - N_PBLK) * n_hp + h, 0)
    exp_per_step = N_EXP // n_steps
    exp_slice = lambda i, h: (step(i, h), 0, 0)
    row_slice = lambda i, h: (step(i, h), 0)
    up_spec = pl.BlockSpec((exp_per_step, D, FF), exp_slice)
    down_spec = pl.BlockSpec((exp_per_step, FF, D), exp_slice)
    side_specs = [up_spec, up_spec, down_spec]
    side_shapes = [jax.ShapeDtypeStruct(a.shape, BF16) for a in side_weights]
    zrows = R_MAX // n_steps

    return pl.pallas_call(
        _mixer_kernel,
        out_shape=(jax.ShapeDtypeStruct((N_TOK, DIFF_H * 2 * DIFF_HD), BF16),
                   jax.ShapeDtypeStruct((N_TOK, RET_H * RET_VD), BF16),
                   jax.ShapeDtypeStruct((16, RET_H, RET_KD, RET_VD), F32),
                   jax.ShapeDtypeStruct((16, RET_H, RET_KD, RET_VD), F32),
                   *side_shapes,
                   jax.ShapeDtypeStruct((R_MAX, XW), BF16)),
        grid=(nb, n_hp),
        in_specs=[pl.BlockSpec(memory_space=pltpu.SMEM),
                  pl.BlockSpec((ATT_HPS, SEQ_BLK, LANE), lambda i, h: (C_DQ // ATT_HPS + h, i, 0)),
                  pl.BlockSpec((ATT_HPS, SEQ_BLK, LANE), lambda i, h: (C_DK // ATT_HPS + h, i, 0)),
                  pl.BlockSpec((ATT_HPS, SEQ_BLK, LANE), lambda i, h: (C_DV // ATT_HPS + h, i, 0)),
                  pl.BlockSpec((None, ATT_HPS * LANE, PAST), lambda i, h: (smp(i), h, 0)),
                  pl.BlockSpec((None, None, PAST, DIFF_H, LANE), lambda i, h: (smp(i), 0, 0, 0, 0)),
                  pl.BlockSpec((T_S, LANE), lambda i, h: (0, 0)),
                  pl.BlockSpec((T_S, LANE), lambda i, h: (0, 0)),
                  pl.BlockSpec((1, LANE), lambda i, h: (0, 0)),
                  pl.BlockSpec((None, SEQ_BLK, LANE), lambda i, h: (C_RQ + h, i, 0)),
                  pl.BlockSpec((None, SEQ_BLK, LANE), lambda i, h: (C_RK + h, i, 0)),
                  pl.BlockSpec((2, SEQ_BLK, LANE), lambda i, h: (C_RV // 2 + h, i, 0)),
                  pl.BlockSpec((2, SEQ_BLK, LANE), lambda i, h: (C_RG // 2 + h, i, 0)),
                  state_in, state_in,
                  pl.BlockSpec((1, RET_VD), lambda i, h: (0, h)),
                  *side_specs],
        out_specs=(pl.BlockSpec((SEQ_BLK, ATT_HPS * LANE), lambda i, h: (i, h)),
                   pl.BlockSpec((SEQ_BLK, RET_VD), lambda i, h: (i, h)),
                   state_out, state_out,
                   *side_specs,
                   pl.BlockSpec((zrows, XW), row_slice)),
        scratch_shapes=[pltpu.VMEM((ATT_HPS, T_S, LANE), BF16),
                        pltpu.VMEM((ATT_HPS, T_S, LANE), BF16),
                        pltpu.VMEM((ATT_HPS, T_S + PAST, 2 * LANE), BF16),
                        pltpu.VMEM((RET_H, T_P, T_P), F32)],
        compiler_params=_cparams(("arbitrary", "arbitrary")),
        name="mixers",
    )(aux, proj, proj, proj, cache_k, cache_v, cos_t, sin_t, subln_g,
      proj, proj, proj, proj, s0f, s0b, gnorm, *side_weights)


def _rope_tables():
    n_rows = T_S // GRID_W
    row = np.repeat(np.arange(n_rows), GRID_W).astype(np.float64)
    col = np.tile(np.arange(GRID_W), n_rows).astype(np.float64)
    n_freq = DIFF_HD // 4
    inv = ROPE_BASE ** (-np.arange(n_freq, dtype=np.float64) / n_freq)

    def axis_tables(pos):
        ang = pos[:, None] * inv[None, :]
        c = np.cos(ang)
        s = np.sin(ang)
        return np.concatenate([c, c], axis=-1), np.concatenate([-s, s], axis=-1)

    cr, sr = axis_tables(row)
    cc, sc = axis_tables(col)
    cos_h = np.concatenate([cr, cc], axis=-1)
    sin_h = np.concatenate([sr, sc], axis=-1)
    return (jnp.asarray(np.concatenate([cos_h, cos_h], axis=-1), F32),
            jnp.asarray(np.concatenate([sin_h, sin_h], axis=-1), F32))


OP_TM = 512
OP_NPT = N_PROMPT // OP_TM


def _outproj_kernel(ra_ref, da_ref, gr_ref, gd_ref, xp_ref, xs_ref, mod_ref, n2_ref,
                    wro32_ref, wdo32_ref, wo32_ref, rgw_ref, rew_ref, brt_ref,
                    x1_ref, h2_ref, cw8_ref, info_ref, infot_ref, off_ref, pctab_ref, steps_ref,
                    m_scr, wro_ref, wdo_ref, wo_ref, wrt_ref, pc_scr, lg_scr):
    i = pl.program_id(0)

    @pl.when(i == 0)
    def _():
        wro_ref[...] = wro32_ref[...].astype(BF16)
        wdo_ref[...] = wdo32_ref[...].astype(BF16)
        wo_ref[...] = wo32_ref[...].astype(BF16)
        gap = jnp.zeros((ROUTE_E0 - N_GROUPS, D), F32)
        pad = jnp.zeros((ROUTE_W - ROUTE_E0 - N_EXP, D), F32)
        wr = jnp.concatenate([rgw_ref[...], gap, rew_ref[...], pad], axis=0)
        wrt_ref[...] = wr.astype(BF16)

        lg_scr[...] = jnp.zeros_like(lg_scr)

    def route_previous():
        cw8, info, pc, info_t = _route_cols(lg_scr[...])
        cw8_ref[...] = cw8
        info_ref[...] = info
        infot_ref[...] = info_t
        pc_scr[pl.ds(jnp.maximum(i - 1, 0), 1), :] = pc
        return cw8, info

    @pl.when(i < RT_NT)
    def _():
        routed = route_previous()
        ret_out = jnp.dot(ra_ref[...], wro_ref[...], preferred_element_type=F32)
        diff_out = jnp.dot(da_ref[...], wdo_ref[...], preferred_element_type=F32)
        for c in range(D // LANE):
            sl = slice(c * LANE, (c + 1) * LANE)
            m = (jax.nn.sigmoid(gr_ref[c].astype(F32)) * ret_out[:, sl]
                 + jax.nn.sigmoid(gd_ref[c].astype(F32)) * diff_out[:, sl])
            if c == 0:
                m = _ordered_after(_ordered_after(m, routed[0]), routed[1])
            m_scr[:, sl] = m.astype(BF16)
        mix = jnp.dot(m_scr[...], wo_ref[...], preferred_element_type=F32)
        mod = mod_ref[...]

        x1 = jnp.where(i < OP_NPT, xp_ref[...], xs_ref[...]) + mod[:, 2 * D:3 * D] * mix
        x1_ref[...] = x1
        h2 = (_rms(x1) * n2_ref[...] * (1.0 + mod[:, 4 * D:5 * D]) + mod[:, 3 * D:4 * D]).astype(BF16)
        h2_ref[...] = h2
        lg_scr[...] = lax.dot_general(wrt_ref[...], h2, (((1,), (1,)), ((), ())),
                                      preferred_element_type=F32) + brt_ref[...]

    @pl.when(i == RT_NT)
    def _():
        route_previous()
        off_ref[...], pctab_ref[...], steps_ref[...] = _dispatch_plan(pc_scr[...])


def _outproj(ret_act, diff_act, proj, xp, xs, mod3, n2, wro, wdo, wo, rgw, rew, brt):
    assert OP_TM == RT_TM
    npt = OP_NPT
    nt = N_TOK // OP_TM
    cur = lambda i: jnp.minimum(i, nt - 1)
    prev = lambda i: jnp.maximum(i - 1, 0)
    full = lambda i: (0, 0)
    once = pl.Buffered(1)
    return pl.pallas_call(
        _outproj_kernel,
        out_shape=(jax.ShapeDtypeStruct((N_TOK, D), F32),
                   jax.ShapeDtypeStruct((N_TOK, D), BF16),
                   jax.ShapeDtypeStruct((N_TOK, ROUTE_W), F32),
                   jax.ShapeDtypeStruct((N_TOK, ROUTE_W), F32),
                   jax.ShapeDtypeStruct((RT_NT, EPG, RT_TM), F32),
                   jax.ShapeDtypeStruct((RT_NT, ROUTE_W), jnp.int32),
                   jax.ShapeDtypeStruct((RT_NT, ROUTE_W), jnp.int32),
                   jax.ShapeDtypeStruct((PLAN_ROWS, ROUTE_W), jnp.int32)),
        grid=(nt + 1,),
        in_specs=[pl.BlockSpec((OP_TM, D), lambda i: (cur(i), 0)),
                  pl.BlockSpec((OP_TM, D), lambda i: (cur(i), 0)),
                  pl.BlockSpec((8, OP_TM, LANE), lambda i: (C_GR // 8, cur(i), 0)),
                  pl.BlockSpec((8, OP_TM, LANE), lambda i: (C_GD // 8, cur(i), 0)),
                  pl.BlockSpec((OP_TM, D), lambda i: (jnp.minimum(i, npt - 1), 0)),
                  pl.BlockSpec((OP_TM, D), lambda i: (jnp.maximum(cur(i) - npt, 0), 0)),
                  pl.BlockSpec((None, 1, 6 * D), lambda i: (_mod_row(OP_TM)(cur(i)), 0, 0)),
                  pl.BlockSpec((1, D), full),
                  pl.BlockSpec((D, D), full, pipeline_mode=once),
                  pl.BlockSpec((D, D), full, pipeline_mode=once),
                  pl.BlockSpec((D, D), full, pipeline_mode=once),
                  pl.BlockSpec((N_GROUPS, D), full, pipeline_mode=once),
                  pl.BlockSpec((N_EXP, D), full, pipeline_mode=once),
                  pl.BlockSpec((ROUTE_W, 1), full)],
        out_specs=(pl.BlockSpec((OP_TM, D), lambda i: (cur(i), 0)),
                   pl.BlockSpec((OP_TM, D), lambda i: (cur(i), 0)),
                   pl.BlockSpec((OP_TM, ROUTE_W), lambda i: (prev(i), 0)),
                   pl.BlockSpec((OP_TM, ROUTE_W), lambda i: (prev(i), 0)),
                   pl.BlockSpec((None, EPG, RT_TM), lambda i: (prev(i), 0, 0)),
                   pl.BlockSpec((RT_NT, ROUTE_W), full),
                   pl.BlockSpec((RT_NT, ROUTE_W), full),
                   pl.BlockSpec((PLAN_ROWS, ROUTE_W), full)),
        scratch_shapes=[pltpu.VMEM((OP_TM, D), BF16),
                        pltpu.VMEM((D, D), BF16), pltpu.VMEM((D, D), BF16), pltpu.VMEM((D, D), BF16),
                        pltpu.VMEM((ROUTE_W, D), BF16),
                        pltpu.VMEM((RT_NT, ROUTE_W), F32),
                        pltpu.VMEM((ROUTE_W, RT_TM), F32)],
        compiler_params=_cparams(("arbitrary",)),
        name="outproj",
    )(ret_act, diff_act, proj, proj, xp, xs, mod3, n2, wro, wdo, wo, rgw, rew, brt)


RT_TM = 512
RT_NT = N_TOK // RT_TM
PIECE = 16
R_LOC = RT_TM + N_GROUPS * PIECE
R_STAGE = 640
EX_TM = 512
R_MAX = 11264
EX_NT = R_MAX // EX_TM


ROUTE_E0 = EPG


def _route_cols(lgt):
    row = lax.broadcasted_iota(jnp.int32, (EPG, RT_TM), 0)
    neg = jnp.float32(-jnp.inf)

    def first_row(cond):
        return jnp.min(jnp.where(cond, row, EPG), axis=0, keepdims=True)

    head = lgt[0:EPG]
    is_g = row < N_GROUPS
    gl = jnp.where(is_g, head, neg)
    gmax = jnp.max(gl, axis=0, keepdims=True)
    gsum = jnp.sum(jnp.where(is_g, jnp.exp(head - gmax), 0.0), axis=0, keepdims=True)
    p_top = 1.0 / gsum
    g_idx = first_row(gl == gmax)
    el = lgt[ROUTE_E0 + (N_GROUPS - 1) * EPG:ROUTE_E0 + N_GROUPS * EPG]
    for g in reversed(range(N_GROUPS - 1)):
        el = jnp.where(g_idx == g, lgt[ROUTE_E0 + g * EPG:ROUTE_E0 + (g + 1) * EPG], el)
    ee = jnp.exp(el - jnp.max(el, axis=0, keepdims=True))
    ep = ee / jnp.sum(ee, axis=0, keepdims=True)
    e1 = jnp.max(ep, axis=0, keepdims=True)
    i1 = first_row(ep == e1)
    ep2 = jnp.where(row == i1, -1.0, ep)
    e2 = jnp.max(ep2, axis=0, keepdims=True)
    i2 = first_row(ep2 == e2)
    den = e1 + e2
    cw8_t = (jnp.where(row == i1, p_top * e1 / den, 0.0)
             + jnp.where(row == i2, p_top * e2 / den, 0.0))

    onehot = (row == g_idx).astype(F32)
    ii = lax.broadcasted_iota(jnp.int32, (RT_TM, RT_TM), 0)
    jj = lax.broadcasted_iota(jnp.int32, (RT_TM, RT_TM), 1)
    earlier = (ii < jj).astype(BF16)
    prefix = jnp.dot(onehot.astype(BF16), earlier, preferred_element_type=F32)
    cnt = jnp.sum(onehot, axis=1, keepdims=True)
    pc_col = jnp.floor((cnt + (PIECE - 1.0)) * (1.0 / PIECE)) * PIECE
    row1 = lax.broadcasted_iota(jnp.int32, (EPG, 1), 0)
    lane1 = lax.broadcasted_iota(jnp.int32, (1, ROUTE_W), 1)
    lo = jnp.zeros((EPG, 1), F32)
    run = jnp.zeros((1, 1), F32)
    pc = jnp.zeros((1, ROUTE_W), F32)
    for g in range(N_GROUPS):
        pc_g = jnp.sum(jnp.where(row1 == g, pc_col, 0.0), axis=0, keepdims=True)
        lo = jnp.where(row1 == g, run, lo)
        pc = jnp.where(lane1 == g, pc_g, pc)
        run = run + pc_g
    dest = jnp.sum(onehot * (prefix + lo), axis=0, keepdims=True)
    info_t = jnp.where(row == 0, g_idx.astype(F32), jnp.where(row == 1, dest, 0.0))

    slab = jnp.concatenate([cw8_t, info_t, jnp.zeros((LANE - 2 * EPG, RT_TM), F32)], axis=0)
    cols = slab.T
    lane = lax.broadcasted_iota(jnp.int32, cols.shape, 1)
    cw8 = jnp.where(lane < EPG, cols, 0.0)
    info = jnp.where(lane < 2, pltpu.roll(cols, LANE - EPG, 1), 0.0)
    return cw8, info, pc, info_t


PLAN_ROWS = 32


def _dispatch_plan(pc_all):
    lane = lax.broadcasted_iota(jnp.int32, (1, ROUTE_W), 1)
    is_g = lane < N_GROUPS
    seg_len = jnp.sum(pc_all, axis=0, keepdims=True)
    seg_pad = jnp.floor((seg_len + (EX_TM - 1.0)) * (1.0 / EX_TM)) * EX_TM
    run = seg_pad + pltpu.roll(seg_pad, 1, 1)
    seg_end_pad = jnp.where(is_g, run + pltpu.roll(run, 2, 1), 0.0)
    seg_start = seg_end_pad - seg_pad
    ti = lax.broadcasted_iota(jnp.int32, (RT_NT, RT_NT), 0)
    tj = lax.broadcasted_iota(jnp.int32, (RT_NT, RT_NT), 1)
    earlier = (tj < ti).astype(BF16)
    within = jnp.dot(earlier, pc_all.astype(BF16), preferred_element_type=F32)
    used_begin = seg_end_pad - seg_len
    chunk_off = jnp.where(is_g, used_begin + within, 0.0)

    start = (lax.broadcasted_iota(jnp.int32, (PLAN_ROWS, 1), 0) * EX_TM).astype(F32)
    passed = jnp.sum(jnp.where(jnp.logical_and(is_g, start >= seg_end_pad), 1.0, 0.0), axis=1, keepdims=True)
    group = jnp.minimum(passed, N_GROUPS - 1.0)
    begin = jnp.sum(jnp.where(lane == group.astype(jnp.int32), used_begin, 0.0), axis=1, keepdims=True)
    used = jnp.where(passed >= N_GROUPS, 0.0, jnp.clip(start + EX_TM - begin, 0.0, float(EX_TM)))
    steps = jnp.where(lane == 0, group, jnp.where(lane == 1, used, 0.0))
    return chunk_off.astype(jnp.int32), pc_all.astype(jnp.int32), steps.astype(jnp.int32)


BIG = 64


def _piece_copies(off_ref, pc_ref, tile, make):
    lo = 0
    for g in range(N_GROUPS):
        rows = pc_ref[tile, g]
        n_big = rows // BIG
        for size, first, n in ((BIG, 0, n_big), (PIECE, n_big * BIG, (rows - n_big * BIG) // PIECE)):
            def body(j, carry, size=size, lo=lo + first, base=off_ref[tile, g] + first):
                make(pl.multiple_of(lo + j * size, PIECE), pl.multiple_of(base + j * size, PIECE), size)
                return carry

            lax.fori_loop(0, n, body, 0)
        lo = lo + rows


def _piece_waits(pc_ref, tile, wait):
    n_big = 0
    n_small = 0
    for g in range(N_GROUPS):
        rows = pc_ref[tile, g]
        n_big = n_big + rows // BIG
        n_small = n_small + (rows % BIG) // PIECE
    for size, n in ((BIG, n_big), (PIECE, n_small)):
        def body(j, carry, size=size):
            wait(size)
            return carry

        lax.fori_loop(0, n, body, 0)


XW = D + ROUTE_W


def _dispatch_kernel(off_ref, pc_ref, h_ref, infot_ref, cw8_ref, xs_in, xs_out, x_scr, sem):
    del xs_in
    i = pl.program_id(0)
    slot = i % 2
    dest = infot_ref[1:2, :]
    row = lax.broadcasted_iota(jnp.int32, (R_LOC, RT_TM), 0).astype(F32)
    sel = (row == dest).astype(BF16)
    cw = cw8_ref[...]
    hi = cw.astype(BF16).astype(F32)
    mid = (cw - hi).astype(BF16).astype(F32)
    low = (cw - hi - mid).astype(BF16).astype(F32)
    pieces = (hi + pltpu.roll(mid, EPG, 1) + pltpu.roll(low, 2 * EPG, 1)).astype(BF16)
    rows = jnp.concatenate([h_ref[...], pieces], axis=1)
    x_scr[slot] = jnp.dot(sel, rows, preferred_element_type=F32).astype(BF16)

    def x_copy(s, src, dst, rows):
        return pltpu.make_async_copy(x_scr.at[s, pl.ds(src, rows)], xs_out.at[pl.ds(dst, rows)],
                                     sem.at[s])

    _piece_copies(off_ref, pc_ref, i, lambda src, dst, rows: x_copy(slot, src, dst, rows).start())

    def wait_tile(tile, s):
        _piece_waits(pc_ref, tile, lambda rows: x_copy(s, 0, 0, rows).wait())

    @pl.when(i > 0)
    def _():
        wait_tile(i - 1, 1 - slot)

    @pl.when(i == RT_NT - 1)
    def _():
        wait_tile(i, slot)


def _dispatch(chunk_off, pc, h2, info_t, cw8, xs0):
    grid_spec = pltpu.PrefetchScalarGridSpec(
        num_scalar_prefetch=2,
        grid=(RT_NT,),
        in_specs=[pl.BlockSpec((RT_TM, D), lambda i, o, p: (i, 0)),
                  pl.BlockSpec((None, EPG, RT_TM), lambda i, o, p: (i, 0, 0)),
                  pl.BlockSpec((RT_TM, ROUTE_W), lambda i, o, p: (i, 0)),
                  pl.BlockSpec(memory_space=pl.ANY)],
        out_specs=pl.BlockSpec(memory_space=pl.ANY),
        scratch_shapes=[pltpu.VMEM((2, R_LOC, XW), BF16),
                        pltpu.SemaphoreType.DMA((2,))])
    return pl.pallas_call(
        _dispatch_kernel,
        out_shape=jax.ShapeDtypeStruct((R_MAX, XW), BF16),
        grid_spec=grid_spec,
        input_output_aliases={5: 0},
        compiler_params=_cparams(("arbitrary",)),
        name="dispatch",
    )(chunk_off, pc, h2, info_t, cw8, xs0)


def _expert_kernel(steps_ref, x_ref, wg_ref, wu_ref, wd_ref, y_ref):
    k = pl.program_id(0)
    used = steps_ref[k, 1]

    def run(rows):
        x = x_ref[rows, 0:D]
        cw = x_ref[rows, D:XW].astype(F32)
        lane = lax.broadcasted_iota(jnp.int32, cw.shape, 1)
        acc = jnp.zeros(x.shape, F32)
        for j in range(EPG):
            mine = jnp.logical_and((lane & (EPG - 1)) == j, lane < 3 * EPG)
            w = jnp.sum(jnp.where(mine, cw, 0.0), axis=-1, keepdims=True)
            a = (jax.nn.silu(jnp.dot(x, wg_ref[j], preferred_element_type=F32))
                 * jnp.dot(x, wu_ref[j], preferred_element_type=F32))
            acc = acc + jnp.dot((a * w).astype(BF16), wd_ref[j], preferred_element_type=F32)
        y_ref[rows, :] = acc.astype(BF16)

    quarter = EX_TM // 4

    @pl.when(used > 2 * quarter)
    def _():
        run(slice(0, EX_TM))

    for rows, above in ((2 * quarter, quarter), (quarter, 0)):
        @pl.when(jnp.logical_and(used > above, used <= rows))
        def _(rows=rows):
            run(slice(EX_TM - rows, EX_TM))
            y_ref[0:EX_TM - rows, :] = jnp.zeros((EX_TM - rows, D), BF16)

    @pl.when(used == 0)
    def _():
        y_ref[...] = jnp.zeros_like(y_ref)


def _experts(steps, xs, wg, wu, wd):
    grid_spec = pltpu.PrefetchScalarGridSpec(
        num_scalar_prefetch=1,
        grid=(EX_NT,),
        in_specs=[pl.BlockSpec((EX_TM, XW), lambda k, st: (k, 0)),
                  pl.BlockSpec((EPG, D, FF), lambda k, st: (st[k, 0], 0, 0)),
                  pl.BlockSpec((EPG, D, FF), lambda k, st: (st[k, 0], 0, 0)),
                  pl.BlockSpec((EPG, FF, D), lambda k, st: (st[k, 0], 0, 0))],
        out_specs=pl.BlockSpec((EX_TM, D), lambda k, st: (k, 0)))
    return pl.pallas_call(
        _expert_kernel,
        out_shape=jax.ShapeDtypeStruct((R_MAX, D), BF16),
        grid_spec=grid_spec,
        compiler_params=_cparams(("arbitrary",)),
        name="experts",
    )(steps, xs, wg, wu, wd)


CB_NPT = N_PROMPT // RT_TM


def _combine_kernel(off_ref, pc_ref, info_ref, x1_ref, mod_ref, g_ref, ys_hbm,
                    yp_ref, ysm_ref, stage, sem):
    i = pl.program_id(0)
    slot = i % 2

    def copy(s, dst, src, rows):
        return pltpu.make_async_copy(ys_hbm.at[pl.ds(src, rows)], stage.at[s, pl.ds(dst, rows)],
                                     sem.at[s])

    def fetch(tile, s):
        _piece_copies(off_ref, pc_ref, tile, lambda loc, glob, rows: copy(s, loc, glob, rows).start())

    @pl.when(i == 0)
    def _():
        stage[...] = jnp.zeros_like(stage)
        fetch(0, 0)

    @pl.when(i + 1 < RT_NT)
    def _():
        fetch(i + 1, 1 - slot)

    _piece_waits(pc_ref, i, lambda rows: copy(slot, 0, 0, rows).wait())

    dest = info_ref[...][:, 1:2]
    col = lax.broadcasted_iota(jnp.int32, (RT_TM, R_STAGE), 1).astype(F32)
    moe = jnp.dot((col == dest).astype(BF16), stage[slot], preferred_element_type=F32)
    mod = mod_ref[...]
    out = _rms(x1_ref[...] + mod[:, 5 * D:6 * D] * moe) * g_ref[...]

    @pl.when(i < CB_NPT)
    def _():
        yp_ref[...] = out

    @pl.when(i >= CB_NPT)
    def _():
        ysm_ref[...] = out


def _combine(chunk_off, pc, info, x1, mod3, fg, ys):
    npt = CB_NPT
    grid_spec = pltpu.PrefetchScalarGridSpec(
        num_scalar_prefetch=2,
        grid=(RT_NT,),
        in_specs=[pl.BlockSpec((RT_TM, ROUTE_W), lambda i, o, p: (i, 0)),
                  pl.BlockSpec((RT_TM, D), lambda i, o, p: (i, 0)),
                  pl.BlockSpec((None, 1, 6 * D), lambda i, o, p: (_mod_row(RT_TM)(i), 0, 0)),
                  pl.BlockSpec((1, D), lambda i, o, p: (0, 0)),
                  pl.BlockSpec(memory_space=pl.ANY)],
        out_specs=(pl.BlockSpec((RT_TM, D), lambda i, o, p: (jnp.minimum(i, npt - 1), 0)),
                   pl.BlockSpec((RT_TM, D), lambda i, o, p: (jnp.maximum(i - npt, 0), 0))),
        scratch_shapes=[pltpu.VMEM((2, R_STAGE, D), BF16),
                        pltpu.SemaphoreType.DMA((2,))])
    return pl.pallas_call(
        _combine_kernel,
        out_shape=(jax.ShapeDtypeStruct((N_PROMPT, D), F32),
                   jax.ShapeDtypeStruct((N_SAMPLE, D), F32)),
        grid_spec=grid_spec,
        compiler_params=_cparams(("arbitrary",)),
        name="combine",
    )(chunk_off, pc, info, x1, mod3, fg, ys)


def kernel(x_prompt, x_sample, cache_diff_k, cache_diff_v, state_ret_fwd, state_ret_bwd, c, c_ctx,
           w_ada, b_ada, norm1_g, norm2_g, w_in, ret_decay_fwd, ret_decay_bwd, ret_norm_g,
           diff_lambda_q1, diff_lambda_k1, diff_lambda_q2, diff_lambda_k2, diff_subln_g,
           w_ret_o, w_diff_o, w_o, router_group_w, router_group_b, router_expert_w, router_expert_b,
           moe_w_gate, moe_w_up, moe_w_down, final_norm_g):
    l = 0
    lam_init = 0.8 - 0.6 * math.exp(-0.3 * l)
    row = lambda a: a[l].astype(F32)[None, :]

    xp = x_prompt.reshape(N_PROMPT, D)
    xs = x_sample.reshape(N_SAMPLE, D)
    mod3, aux, brt = _modulation(
        lam_init, c_ctx[None, :], c, w_ada[l], b_ada[l][None, :],
        [row(diff_lambda_q1), row(diff_lambda_k1), row(diff_lambda_q2), row(diff_lambda_k2),
         row(ret_decay_fwd), row(ret_decay_bwd), row(router_group_b), row(router_expert_b)])

    proj, kt32, v32 = _inproj(xp, xs, mod3, norm1_g[l][None, :], w_in[l])

    cos_t, sin_t = _rope_tables()
    cache_kt = jnp.transpose(cache_diff_k[:, l], (0, 2, 3, 4, 1)).reshape(4, D, PAST)
    (diff_act, ret_act, s_f, s_b, wg_bf, wu_bf, wd_bf, xs0) = _mixers(
        aux, proj, cache_kt, cache_diff_v, cos_t, sin_t, diff_subln_g[l][None, :],
        state_ret_fwd[:, l], state_ret_bwd[:, l], ret_norm_g[l][None, :],
        (moe_w_gate[l], moe_w_up[l], moe_w_down[l]))

    x1, h2, cw8, info, info_t, chunk_off, pc, steps = _outproj(
        ret_act, diff_act, proj, xp, xs, mod3, norm2_g[l][None, :],
        w_ret_o[l], w_diff_o[l], w_o[l], router_group_w[l].T, router_expert_w[l].T, brt)
    xs_sorted = _dispatch(chunk_off, pc, h2, info_t, cw8, xs0)
    y_sorted = _experts(steps, xs_sorted, wg_bf, wu_bf, wd_bf)
    yp, ys = _combine(chunk_off, pc, info, x1, mod3, final_norm_g[None, :], y_sorted)

    return (yp.reshape(16, T_P, D), ys.reshape(4, T_S, D),
            jnp.transpose(kt32.reshape(16, DIFF_H, 2, DIFF_HD, T_P), (0, 4, 1, 2, 3))[:, None],
            v32.reshape(16, 1, T_P, DIFF_H, 2 * DIFF_HD),
            s_f.reshape(16, 1, RET_H, RET_KD, RET_VD), s_b.reshape(16, 1, RET_H, RET_KD, RET_VD))
```

```python
import functools
import math

import jax
import jax.numpy as jnp
import numpy as np
from jax import lax
from jax.experimental import pallas as pl
from jax.experimental.pallas import tpu as pltpu

F32 = jnp.float32
BF16 = jnp.bfloat16

D = 1024
N_PROMPT = 16 * 256
N_SAMPLE = 4 * 1024
N_TOK = N_PROMPT + N_SAMPLE
T_P = 256
T_S = 1024
PAST = 512
GRID_W = 64
RET_H = 4
RET_KD = 128
RET_VD = 256
DIFF_H = 8
DIFF_HD = 64
ROPE_BASE = 10000.0
N_GROUPS = 4
EPG = 8
N_EXP = 32
FF = 256
EPS = 1e-6
IN_W = 8192
LANE = 128
N_CHUNK = IN_W // LANE
C_RQ, C_RK, C_RV, C_RG, C_DQ, C_DK, C_DV, C_GR, C_GD = 0, 4, 8, 16, 24, 32, 40, 48, 56
ROUTE_W = 128
SEQ_BLK = 1024
N_PBLK = N_PROMPT // SEQ_BLK
V7X_VMEM_BYTES = 64 * 1024 * 1024
VMEM_LIMIT = V7X_VMEM_BYTES - 8 * 1024 * 1024


def _cparams(sem):
    return pltpu.CompilerParams(dimension_semantics=sem, vmem_limit_bytes=VMEM_LIMIT)


def _mod_row(tile_rows):
    def f(i):
        start = i * tile_rows
        return jnp.where(start < N_PROMPT, 0, 1 + (start - N_PROMPT) // T_S)
    return f


def _rms(x):
    return x * lax.rsqrt(jnp.mean(x * x, axis=-1, keepdims=True) + EPS)


def _ordered_after(dst, src):
    m = None
    for a in range(src.shape[0] // 8):
        for b in range(src.shape[1] // LANE):
            part = src[a * 8:(a + 1) * 8, b * LANE:(b + 1) * LANE]
            m = part if m is None else jnp.maximum(m, part)
    never = jnp.logical_and(m != m, m == m)
    return jnp.where(jnp.tile(never, (dst.shape[0] // 8, 1)), 0.0, dst)


MOD_ROWS = 8


AUX_LAM = 0
AUX_LG = 1


def _mod_kernel(lam_init, ctx_ref, c_ref, w_ref, b_ref, lq1_ref, lk1_ref, lq2_ref, lk2_ref, df_ref, db_ref,
                gb_ref, eb_ref, o_ref, aux_ref, brt_ref):
    cond = jnp.concatenate([ctx_ref[...], c_ref[...],
                            jnp.zeros((MOD_ROWS - 1 - c_ref.shape[0], D), F32)], axis=0)
    s = jax.nn.silu(cond)
    out = jnp.dot(s.astype(BF16), w_ref[...].astype(BF16), preferred_element_type=F32) + b_ref[...]
    for r in range(MOD_ROWS):
        o_ref[r] = out[r:r + 1, :]

    def lane_row(parts):
        used = sum(p.shape[1] for p in parts)
        return jnp.concatenate(parts + [jnp.zeros((1, LANE - used), F32)], axis=1)

    lam = (jnp.exp(jnp.sum(lq1_ref[...] * lk1_ref[...], axis=1, keepdims=True))
           - jnp.exp(jnp.sum(lq2_ref[...] * lk2_ref[...], axis=1, keepdims=True)) + lam_init)
    rows = [lane_row([lam, jnp.full((1, 1), 1.0 - lam_init, F32)]),
            lane_row([jax.nn.log_sigmoid(df_ref[...])]),
            lane_row([jax.nn.log_sigmoid(db_ref[...])])]
    aux_ref[...] = jnp.concatenate(rows + [jnp.zeros((8 - len(rows), LANE), F32)], axis=0)
    bias = lane_row([gb_ref[...], jnp.zeros((1, ROUTE_E0 - N_GROUPS), F32), eb_ref[...]])
    brt_ref[...] = jnp.tile(bias, (8, 1)).T[:, 0:1]


def _modulation(lam_init, c_ctx, c, w_ada, b_ada, small):
    tn = 1536
    whole = lambda a: pl.BlockSpec(a.shape, lambda j: (0, 0))
    return pl.pallas_call(
        functools.partial(_mod_kernel, lam_init),
        out_shape=(jax.ShapeDtypeStruct((MOD_ROWS, 1, 6 * D), F32),
                   jax.ShapeDtypeStruct((8, LANE), F32),
                   jax.ShapeDtypeStruct((ROUTE_W, 1), F32)),
        grid=(6 * D // tn,),
        in_specs=[pl.BlockSpec((1, D), lambda j: (0, 0)),
                  pl.BlockSpec(c.shape, lambda j: (0, 0)),
                  pl.BlockSpec((D, tn), lambda j: (0, j)),
                  pl.BlockSpec((1, tn), lambda j: (0, j))] + [whole(a) for a in small],
        out_specs=(pl.BlockSpec((MOD_ROWS, 1, tn), lambda j: (0, 0, j)),
                   pl.BlockSpec((8, LANE), lambda j: (0, 0)),
                   pl.BlockSpec((ROUTE_W, 1), lambda j: (0, 0))),
        compiler_params=_cparams(("arbitrary",)),
        name="mod",
    )(c_ctx, c, w_ada, b_ada, *small)


IP_TM = 512
IP_TN = 2048
IP_NPT = N_PROMPT // IP_TM


IP_KV_TILE = C_DK * LANE // IP_TN
IP_SPT = IP_TM // T_P


def _inproj_kernel(xp_ref, xs_ref, mod_ref, n1_ref, w_ref, proj_ref, kt_ref, v32_ref, w_scr):
    j = pl.program_id(0)
    i = pl.program_id(1)

    @pl.when(i == 0)
    def _():
        w_scr[...] = w_ref[...].astype(BF16)

    x = jnp.where(i < IP_NPT, xp_ref[...], xs_ref[...])
    mod = mod_ref[...]
    h = (_rms(x) * n1_ref[...] * (1.0 + mod[:, D:2 * D]) + mod[:, 0:D]).astype(BF16)
    acc = jnp.dot(h, w_scr[...], preferred_element_type=F32)
    for c in range(IP_TN // LANE):
        proj_ref[c] = acc[:, c * LANE:(c + 1) * LANE].astype(BF16)

    @pl.when(jnp.logical_and(j == IP_KV_TILE, i < IP_NPT))
    def _():
        for s in range(IP_SPT):
            kt_ref[s] = acc[s * T_P:(s + 1) * T_P, :D].T
        v32_ref[...] = acc[:, D:]


def _inproj(xp, xs, mod3, n1, w_in):
    npt = IP_NPT
    cpt = IP_TN // LANE

    def kv_tile(j, i):
        return jnp.where(j < IP_KV_TILE, 0, jnp.where(j == IP_KV_TILE, jnp.minimum(i, npt - 1), npt - 1))

    return pl.pallas_call(
        _inproj_kernel,
        out_shape=(jax.ShapeDtypeStruct((N_CHUNK, N_TOK, LANE), BF16),
                   jax.ShapeDtypeStruct((N_PROMPT // T_P, D, T_P), F32),
                   jax.ShapeDtypeStruct((N_PROMPT, D), F32)),
        grid=(IN_W // IP_TN, N_TOK // IP_TM),
        in_specs=[pl.BlockSpec((IP_TM, D), lambda j, i: (jnp.minimum(i, npt - 1), 0)),
                  pl.BlockSpec((IP_TM, D), lambda j, i: (jnp.maximum(i - npt, 0), 0)),
                  pl.BlockSpec((None, 1, 6 * D), lambda j, i: (_mod_row(IP_TM)(i), 0, 0)),
                  pl.BlockSpec((1, D), lambda j, i: (0, 0)),
                  pl.BlockSpec((D, IP_TN), lambda j, i: (0, j))],
        out_specs=(pl.BlockSpec((cpt, IP_TM, LANE), lambda j, i: (j, i, 0)),
                   pl.BlockSpec((IP_SPT, D, T_P), lambda j, i: (kv_tile(j, i), 0, 0)),
                   pl.BlockSpec((IP_TM, D), lambda j, i: (kv_tile(j, i), 0))),
        scratch_shapes=[pltpu.VMEM((D, IP_TN), BF16)],
        compiler_params=_cparams(("arbitrary", "arbitrary")),
        name="inproj",
    )(xp, xs, mod3, n1, w_in)


def _decay_mask(t, lgf, lgb):
    ii = lax.broadcasted_iota(jnp.int32, (t, t), 0)
    jj = lax.broadcasted_iota(jnp.int32, (t, t), 1)
    rel = (ii - jj).astype(F32)
    e = jnp.exp(jnp.where(rel >= 0.0, lgf, -lgb) * rel)
    return jnp.where(rel == 0.0, 2.0, e) * (RET_KD ** -0.5)


def _ret_parts(lgf, lgb, q_ref, k_ref, v_ref, rg_ref, s0f_ref, s0b_ref, g_ref,
               o_ref, sf_ref, sb_ref, dm_ref):
    gain = g_ref[...]
    nt = (((1,), (1,)), ((), ()))
    tn = (((0,), (0,)), ((), ()))

    def finish(o, rg):
        d = o - jnp.mean(o, axis=-1, keepdims=True)
        y = d * lax.rsqrt(jnp.mean(d * d, axis=-1, keepdims=True) + EPS) * gain
        return (jax.nn.silu(rg.astype(F32)) * y).astype(BF16)

    def vcat(r):
        return jnp.concatenate([v_ref[0, r, :], v_ref[1, r, :]], axis=1)

    def gcat(r):
        return jnp.concatenate([rg_ref[0, r, :], rg_ref[1, r, :]], axis=1)

    def build_mask():
        dm_ref[...] = _decay_mask(T_P, lgf, lgb)

    n_chunk = SEQ_BLK // T_P
    chunks = [slice(c * T_P, (c + 1) * T_P) for c in range(n_chunk)]
    t = lax.broadcasted_iota(jnp.int32, (T_P, 1), 0).astype(F32)
    kdf = jnp.exp(lgf * (T_P - 1.0 - t)) * (RET_KD ** -0.5)
    kdb = jnp.exp(lgb * t) * (RET_KD ** -0.5)

    def intra(r):
        sc = lax.dot_general(q_ref[r, :], k_ref[r, :], nt, preferred_element_type=F32)
        return jnp.dot((sc * dm_ref[...]).astype(BF16), vcat(r), preferred_element_type=F32)

    def key_state(r, kd):
        kw = (k_ref[r, :].astype(F32) * kd).astype(BF16)
        return lax.dot_general(kw, vcat(r), tn, preferred_element_type=F32)

    def context_part():
        for s, r in enumerate(chunks):
            o_ref[r, :] = finish(intra(r), gcat(r))
            sf_ref[s] = key_state(r, kdf)
            sb_ref[s] = key_state(r, kdb)

    def latent_part():
        qdf = jnp.exp(lgf * (t + 1.0))
        qdb = jnp.exp(lgb * (T_P - t))
        span = jnp.full((1, 1), float(T_P), F32)
        cf = jnp.exp(lgf * span)
        cb = jnp.exp(lgb * span)
        before_b = [None] * n_chunk
        state = s0b_ref[...]
        for c in reversed(range(n_chunk)):
            before_b[c] = state
            if c > 0:
                state = cb * state + key_state(chunks[c], kdb)
        state = s0f_ref[...]
        for c, r in enumerate(chunks):
            qf = q_ref[r, :].astype(F32)
            o = (intra(r)
                 + jnp.dot((qf * qdf).astype(BF16), state.astype(BF16), preferred_element_type=F32)
                 + jnp.dot((qf * qdb).astype(BF16), before_b[c].astype(BF16), preferred_element_type=F32))
            o_ref[r, :] = finish(o, gcat(r))
            if c + 1 < n_chunk:
                state = cf * state + key_state(r, kdf)

    return build_mask, context_part, latent_part


ATT_TQ = 1024
ATT_HPS = 2
ATT_QSCALE = (DIFF_HD ** -0.5) * math.log2(math.e)


N_SIDE = 3


def _mixer_kernel(aux_ref, q_ref, k_ref, v_ref, ckt_ref, cv_ref, cos_ref, sin_ref, g_ref,
                  rq_ref, rk_ref, rv_ref, rg_ref, s0f_ref, s0b_ref, gn_ref, *rest):
    side_in = rest[:N_SIDE]
    o_ref, ro_ref, sf_ref, sb_ref = rest[N_SIDE:N_SIDE + 4]
    side_out = rest[N_SIDE + 4:2 * N_SIDE + 4]
    xs0_ref, q_scr, k_scr, v_scr, dm_scr = rest[2 * N_SIDE + 4:]
    i = pl.program_id(0)
    rh = pl.program_id(1)
    build_mask, ret_context, ret_latent = _ret_parts(
        aux_ref[AUX_LG, rh], aux_ref[AUX_LG + 1, rh], rq_ref, rk_ref, rv_ref, rg_ref, s0f_ref, s0b_ref, gn_ref,
        ro_ref, sf_ref, sb_ref, dm_scr.at[rh])

    @pl.when(i == 0)
    def _():
        build_mask()

    def side_jobs():
        for src, dst in zip(side_in, side_out):
            dst[...] = src[...].astype(BF16)
        xs0_ref[...] = jnp.zeros_like(xs0_ref)

    lam = aux_ref[AUX_LAM, 0]
    out_scale = aux_ref[AUX_LAM, 1]
    gain = g_ref[...] * out_scale
    lane = lax.broadcasted_iota(jnp.int32, (1, LANE), 1)
    first = lane < DIFF_HD
    nt = (((1,), (1,)), ((), ()))

    def halves(q):
        zero = jnp.zeros_like(q)
        return jnp.where(first, q, zero), jnp.where(first, zero, q)

    def weights(s):
        return jnp.exp2(s - jnp.max(s, axis=-1, keepdims=True)).astype(BF16)

    def finish(of0, of1):
        o = of0[:, :LANE] / of0[:, LANE:] - lam * (of1[:, :LANE] / of1[:, LANE:])
        return (_rms(o) * gain).astype(BF16)

    @pl.when(i < N_PBLK)
    def _():
        ret_context()
        n_seq = SEQ_BLK // T_P
        rows = [slice(s * T_P, (s + 1) * T_P) for s in range(n_seq)]
        ones = jnp.ones((T_P, LANE), BF16)
        for hh in range(ATT_HPS):
            cols = slice(hh * LANE, (hh + 1) * LANE)
            q0, q1 = halves((q_ref[hh].astype(F32) * ATT_QSCALE).astype(BF16))
            s0 = jnp.concatenate([lax.dot_general(q0[r], k_ref[hh, r, :], nt, preferred_element_type=F32)
                                  for r in rows], axis=0)
            s1 = jnp.concatenate([lax.dot_general(q1[r], k_ref[hh, r, :], nt, preferred_element_type=F32)
                                  for r in rows], axis=0)
            e0 = weights(s0)
            e1 = weights(s1)
            for r in rows:
                v1 = jnp.concatenate([v_ref[hh, r, :], ones], axis=1)
                o_ref[r, cols] = finish(jnp.dot(e0[r], v1, preferred_element_type=F32),
                                        jnp.dot(e1[r], v1, preferred_element_type=F32))

    @pl.when(i >= N_PBLK)
    def _():
        side_jobs()
        ret_latent()
        cos = cos_ref[...]
        sin = sin_ref[...]
        low = (lax.broadcasted_iota(jnp.int32, (T_S, LANE), 1) & 16) == 0

        def rope(x):
            xs = jnp.where(low, pltpu.roll(x, LANE - 16, 1), pltpu.roll(x, 16, 1))
            return x * cos + xs * sin

        for hh in range(ATT_HPS):
            cols = slice(hh * LANE, (hh + 1) * LANE)
            head = pl.program_id(1) * ATT_HPS + hh
            q_scr[hh] = (rope(q_ref[hh].astype(F32)) * ATT_QSCALE).astype(BF16)
            k_scr[hh] = rope(k_ref[hh].astype(F32)).astype(BF16)
            ckt = ckt_ref[cols, :].astype(BF16)
            v_scr[hh, 0:T_S, 0:LANE] = v_ref[hh]
            v_scr[hh, T_S:T_S + PAST, 0:LANE] = cv_ref[:, head, :].astype(BF16)
            v_scr[hh, :, LANE:2 * LANE] = jnp.ones((T_S + PAST, LANE), BF16)

            def scores(qh, hh=hh, ckt=ckt):
                return jnp.concatenate([lax.dot_general(qh, k_scr[hh], nt, preferred_element_type=F32),
                                        jnp.dot(qh, ckt, preferred_element_type=F32)], axis=1)

            for b in range(T_S // ATT_TQ):
                r = slice(b * ATT_TQ, (b + 1) * ATT_TQ)
                q0, q1 = halves(q_scr[hh, r, :])
                e0 = weights(scores(q0))
                e1 = weights(scores(q1))
                o_ref[r, cols] = finish(jnp.dot(e0, v_scr[hh], preferred_element_type=F32),
                                        jnp.dot(e1, v_scr[hh], preferred_element_type=F32))


def _mixers(aux, proj, cache_k, cache_v, cos_t, sin_t, subln_g, s0f, s0b, gnorm, side_weights):
    nb = N_TOK // SEQ_BLK
    spb = SEQ_BLK // T_P
    smp = lambda i: jnp.maximum(i - N_PBLK, 0)
    pmt = lambda i: jnp.minimum(i, N_PBLK - 1)
    pmh = lambda i, h: jnp.where(i < N_PBLK, h, RET_H - 1)
    state_in = pl.BlockSpec((None, None, RET_KD, RET_VD), lambda i, h: (smp(i), h, 0, 0))
    state_out = pl.BlockSpec((spb, None, RET_KD, RET_VD), lambda i, h: (pmt(i), pmh(i, h), 0, 0))

    n_hp = DIFF_H // ATT_HPS
    assert n_hp == RET_H
    n_steps = (nb - N_PBLK) * n_hp
    step = lambda i, h: jnp.maximum((i - N_PBLK) * n_hp + h, 0)
    exp_per_step = N_EXP // n_steps
    exp_slice = lambda i, h: (step(i, h), 0, 0)
    row_slice = lambda i, h: (step(i, h), 0)
    up_spec = pl.BlockSpec((exp_per_step, D, FF), exp_slice)
    down_spec = pl.BlockSpec((exp_per_step, FF, D), exp_slice)
    side_specs = [up_spec, up_spec, down_spec]
    side_shapes = [jax.ShapeDtypeStruct(a.shape, BF16) for a in side_weights]
    zrows = R_MAX // n_steps

    return pl.pallas_call(
        _mixer_kernel,
        out_shape=(jax.ShapeDtypeStruct((N_TOK, DIFF_H * 2 * DIFF_HD), BF16),
                   jax.ShapeDtypeStruct((N_TOK, RET_H * RET_VD), BF16),
                   jax.ShapeDtypeStruct((16, RET_H, RET_KD, RET_VD), F32),
                   jax.ShapeDtypeStruct((16, RET_H, RET_KD, RET_VD), F32),
                   *side_shapes,
                   jax.ShapeDtypeStruct((R_MAX, XW), BF16)),
        grid=(nb, n_hp),
        in_specs=[pl.BlockSpec(memory_space=pltpu.SMEM),
                  pl.BlockSpec((ATT_HPS, SEQ_BLK, LANE), lambda i, h: (C_DQ // ATT_HPS + h, i, 0)),
                  pl.BlockSpec((ATT_HPS, SEQ_BLK, LANE), lambda i, h: (C_DK // ATT_HPS + h, i, 0)),
                  pl.BlockSpec((ATT_HPS, SEQ_BLK, LANE), lambda i, h: (C_DV // ATT_HPS + h, i, 0)),
                  pl.BlockSpec((None, ATT_HPS * LANE, PAST), lambda i, h: (smp(i), h, 0)),
                  pl.BlockSpec((None, None, PAST, DIFF_H, LANE), lambda i, h: (smp(i), 0, 0, 0, 0)),
                  pl.BlockSpec((T_S, LANE), lambda i, h: (0, 0)),
                  pl.BlockSpec((T_S, LANE), lambda i, h: (0, 0)),
                  pl.BlockSpec((1, LANE), lambda i, h: (0, 0)),
                  pl.BlockSpec((None, SEQ_BLK, LANE), lambda i, h: (C_RQ + h, i, 0)),
                  pl.BlockSpec((None, SEQ_BLK, LANE), lambda i, h: (C_RK + h, i, 0)),
                  pl.BlockSpec((2, SEQ_BLK, LANE), lambda i, h: (C_RV // 2 + h, i, 0)),
                  pl.BlockSpec((2, SEQ_BLK, LANE), lambda i, h: (C_RG // 2 + h, i, 0)),
                  state_in, state_in,
                  pl.BlockSpec((1, RET_VD), lambda i, h: (0, h)),
                  *side_specs],
        out_specs=(pl.BlockSpec((SEQ_BLK, ATT_HPS * LANE), lambda i, h: (i, h)),
                   pl.BlockSpec((SEQ_BLK, RET_VD), lambda i, h: (i, h)),
                   state_out, state_out,
                   *side_specs,
                   pl.BlockSpec((zrows, XW), row_slice)),
        scratch_shapes=[pltpu.VMEM((ATT_HPS, T_S, LANE), BF16),
                        pltpu.VMEM((ATT_HPS, T_S, LANE), BF16),
                        pltpu.VMEM((ATT_HPS, T_S + PAST, 2 * LANE), BF16),
                        pltpu.VMEM((RET_H, T_P, T_P), F32)],
        compiler_params=_cparams(("arbitrary", "arbitrary")),
        name="mixers",
    )(aux, proj, proj, proj, cache_k, cache_v, cos_t, sin_t, subln_g,
      proj, proj, proj, proj, s0f, s0b, gnorm, *side_weights)


def _rope_tables():
    n_rows = T_S // GRID_W
    row = np.repeat(np.arange(n_rows), GRID_W).astype(np.float64)
    col = np.tile(np.arange(GRID_W), n_rows).astype(np.float64)
    n_freq = DIFF_HD // 4
    inv = ROPE_BASE ** (-np.arange(n_freq, dtype=np.float64) / n_freq)

    def axis_tables(pos):
        ang = pos[:, None] * inv[None, :]
        c = np.cos(ang)
        s = np.sin(ang)
        return np.concatenate([c, c], axis=-1), np.concatenate([-s, s], axis=-1)

    cr, sr = axis_tables(row)
    cc, sc = axis_tables(col)
    cos_h = np.concatenate([cr, cc], axis=-1)
    sin_h = np.concatenate([sr, sc], axis=-1)
    return (jnp.asarray(np.concatenate([cos_h, cos_h], axis=-1), F32),
            jnp.asarray(np.concatenate([sin_h, sin_h], axis=-1), F32))


OP_TM = 512
OP_NPT = N_PROMPT // OP_TM


def _outproj_kernel(ra_ref, da_ref, gr_ref, gd_ref, xp_ref, xs_ref, mod_ref, n2_ref,
                    wro32_ref, wdo32_ref, wo32_ref, rgw_ref, rew_ref, brt_ref,
                    x1_ref, h2_ref, cw8_ref, info_ref, infot_ref, off_ref, pctab_ref, steps_ref,
                    m_scr, wro_ref, wdo_ref, wo_ref, wrt_ref, pc_scr, lg_scr):
    i = pl.program_id(0)

    @pl.when(i == 0)
    def _():
        wro_ref[...] = wro32_ref[...].astype(BF16)
        wdo_ref[...] = wdo32_ref[...].astype(BF16)
        wo_ref[...] = wo32_ref[...].astype(BF16)
        gap = jnp.zeros((ROUTE_E0 - N_GROUPS, D), F32)
        pad = jnp.zeros((ROUTE_W - ROUTE_E0 - N_EXP, D), F32)
        wr = jnp.concatenate([rgw_ref[...], gap, rew_ref[...], pad], axis=0)
        wrt_ref[...] = wr.astype(BF16)

        lg_scr[...] = jnp.zeros_like(lg_scr)

    def route_previous():
        cw8, info, pc, info_t = _route_cols(lg_scr[...])
        cw8_ref[...] = cw8
        info_ref[...] = info
        infot_ref[...] = info_t
        pc_scr[pl.ds(jnp.maximum(i - 1, 0), 1), :] = pc
        return cw8, info

    @pl.when(i < RT_NT)
    def _():
        routed = route_previous()
        ret_out = jnp.dot(ra_ref[...], wro_ref[...], preferred_element_type=F32)
        diff_out = jnp.dot(da_ref[...], wdo_ref[...], preferred_element_type=F32)
        for c in range(D // LANE):
            sl = slice(c * LANE, (c + 1) * LANE)
            m = (jax.nn.sigmoid(gr_ref[c].astype(F32)) * ret_out[:, sl]
                 + jax.nn.sigmoid(gd_ref[c].astype(F32)) * diff_out[:, sl])
            if c == 0:
                m = _ordered_after(_ordered_after(m, routed[0]), routed[1])
            m_scr[:, sl] = m.astype(BF16)
        mix = jnp.dot(m_scr[...], wo_ref[...], preferred_element_type=F32)
        mod = mod_ref[...]

        x1 = jnp.where(i < OP_NPT, xp_ref[...], xs_ref[...]) + mod[:, 2 * D:3 * D] * mix
        x1_ref[...] = x1
        h2 = (_rms(x1) * n2_ref[...] * (1.0 + mod[:, 4 * D:5 * D]) + mod[:, 3 * D:4 * D]).astype(BF16)
        h2_ref[...] = h2
        lg_scr[...] = lax.dot_general(wrt_ref[...], h2, (((1,), (1,)), ((), ())),
                                      preferred_element_type=F32) + brt_ref[...]

    @pl.when(i == RT_NT)
    def _():
        route_previous()
        off_ref[...], pctab_ref[...], steps_ref[...] = _dispatch_plan(pc_scr[...])


def _outproj(ret_act, diff_act, proj, xp, xs, mod3, n2, wro, wdo, wo, rgw, rew, brt):
    assert OP_TM == RT_TM
    npt = OP_NPT
    nt = N_TOK // OP_TM
    cur = lambda i: jnp.minimum(i, nt - 1)
    prev = lambda i: jnp.maximum(i - 1, 0)
    full = lambda i: (0, 0)
    once = pl.Buffered(1)
    return pl.pallas_call(
        _outproj_kernel,
        out_shape=(jax.ShapeDtypeStruct((N_TOK, D), F32),
                   jax.ShapeDtypeStruct((N_TOK, D), BF16),
                   jax.ShapeDtypeStruct((N_TOK, ROUTE_W), F32),
                   jax.ShapeDtypeStruct((N_TOK, ROUTE_W), F32),
                   jax.ShapeDtypeStruct((RT_NT, EPG, RT_TM), F32),
                   jax.ShapeDtypeStruct((RT_NT, ROUTE_W), jnp.int32),
                   jax.ShapeDtypeStruct((RT_NT, ROUTE_W), jnp.int32),
                   jax.ShapeDtypeStruct((PLAN_ROWS, ROUTE_W), jnp.int32)),
        grid=(nt + 1,),
        in_specs=[pl.BlockSpec((OP_TM, D), lambda i: (cur(i), 0)),
                  pl.BlockSpec((OP_TM, D), lambda i: (cur(i), 0)),
                  pl.BlockSpec((8, OP_TM, LANE), lambda i: (C_GR // 8, cur(i), 0)),
                  pl.BlockSpec((8, OP_TM, LANE), lambda i: (C_GD // 8, cur(i), 0)),
                  pl.BlockSpec((OP_TM, D), lambda i: (jnp.minimum(i, npt - 1), 0)),
                  pl.BlockSpec((OP_TM, D), lambda i: (jnp.maximum(cur(i) - npt, 0), 0)),
                  pl.BlockSpec((None, 1, 6 * D), lambda i: (_mod_row(OP_TM)(cur(i)), 0, 0)),
                  pl.BlockSpec((1, D), full),
                  pl.BlockSpec((D, D), full, pipeline_mode=once),
                  pl.BlockSpec((D, D), full, pipeline_mode=once),
                  pl.BlockSpec((D, D), full, pipeline_mode=once),
                  pl.BlockSpec((N_GROUPS, D), full, pipeline_mode=once),
                  pl.BlockSpec((N_EXP, D), full, pipeline_mode=once),
                  pl.BlockSpec((ROUTE_W, 1), full)],
        out_specs=(pl.BlockSpec((OP_TM, D), lambda i: (cur(i), 0)),
                   pl.BlockSpec((OP_TM, D), lambda i: (cur(i), 0)),
                   pl.BlockSpec((OP_TM, ROUTE_W), lambda i: (prev(i), 0)),
                   pl.BlockSpec((OP_TM, ROUTE_W), lambda i: (prev(i), 0)),
                   pl.BlockSpec((None, EPG, RT_TM), lambda i: (prev(i), 0, 0)),
                   pl.BlockSpec((RT_NT, ROUTE_W), full),
                   pl.BlockSpec((RT_NT, ROUTE_W), full),
                   pl.BlockSpec((PLAN_ROWS, ROUTE_W), full)),
        scratch_shapes=[pltpu.VMEM((OP_TM, D), BF16),
                        pltpu.VMEM((D, D), BF16), pltpu.VMEM((D, D), BF16), pltpu.VMEM((D, D), BF16),
                        pltpu.VMEM((ROUTE_W, D), BF16),
                        pltpu.VMEM((RT_NT, ROUTE_W), F32),
                        pltpu.VMEM((ROUTE_W, RT_TM), F32)],
        compiler_params=_cparams(("arbitrary",)),
        name="outproj",
    )(ret_act, diff_act, proj, proj, xp, xs, mod3, n2, wro, wdo, wo, rgw, rew, brt)


RT_TM = 512
RT_NT = N_TOK // RT_TM
PIECE = 16
R_LOC = RT_TM + N_GROUPS * PIECE
R_STAGE = 640
EX_TM = 512
R_MAX = 11264
EX_NT = R_MAX // EX_TM


ROUTE_E0 = EPG


def _route_cols(lgt):
    row = lax.broadcasted_iota(jnp.int32, (EPG, RT_TM), 0)
    neg = jnp.float32(-jnp.inf)

    def first_row(cond):
        return jnp.min(jnp.where(cond, row, EPG), axis=0, keepdims=True)

    head = lgt[0:EPG]
    is_g = row < N_GROUPS
    gl = jnp.where(is_g, head, neg)
    gmax = jnp.max(gl, axis=0, keepdims=True)
    gsum = jnp.sum(jnp.where(is_g, jnp.exp(head - gmax), 0.0), axis=0, keepdims=True)
    p_top = 1.0 / gsum
    g_idx = first_row(gl == gmax)
    el = lgt[ROUTE_E0 + (N_GROUPS - 1) * EPG:ROUTE_E0 + N_GROUPS * EPG]
    for g in reversed(range(N_GROUPS - 1)):
        el = jnp.where(g_idx == g, lgt[ROUTE_E0 + g * EPG:ROUTE_E0 + (g + 1) * EPG], el)
    ee = jnp.exp(el - jnp.max(el, axis=0, keepdims=True))
    ep = ee / jnp.sum(ee, axis=0, keepdims=True)
    e1 = jnp.max(ep, axis=0, keepdims=True)
    i1 = first_row(ep == e1)
    ep2 = jnp.where(row == i1, -1.0, ep)
    e2 = jnp.max(ep2, axis=0, keepdims=True)
    i2 = first_row(ep2 == e2)
    den = e1 + e2
    cw8_t = (jnp.where(row == i1, p_top * e1 / den, 0.0)
             + jnp.where(row == i2, p_top * e2 / den, 0.0))

    onehot = (row == g_idx).astype(F32)
    ii = lax.broadcasted_iota(jnp.int32, (RT_TM, RT_TM), 0)
    jj = lax.broadcasted_iota(jnp.int32, (RT_TM, RT_TM), 1)
    earlier = (ii < jj).astype(BF16)
    prefix = jnp.dot(onehot.astype(BF16), earlier, preferred_element_type=F32)
    cnt = jnp.sum(onehot, axis=1, keepdims=True)
    pc_col = jnp.floor((cnt + (PIECE - 1.0)) * (1.0 / PIECE)) * PIECE
    row1 = lax.broadcasted_iota(jnp.int32, (EPG, 1), 0)
    lane1 = lax.broadcasted_iota(jnp.int32, (1, ROUTE_W), 1)
    lo = jnp.zeros((EPG, 1), F32)
    run = jnp.zeros((1, 1), F32)
    pc = jnp.zeros((1, ROUTE_W), F32)
    for g in range(N_GROUPS):
        pc_g = jnp.sum(jnp.where(row1 == g, pc_col, 0.0), axis=0, keepdims=True)
        lo = jnp.where(row1 == g, run, lo)
        pc = jnp.where(lane1 == g, pc_g, pc)
        run = run + pc_g
    dest = jnp.sum(onehot * (prefix + lo), axis=0, keepdims=True)
    info_t = jnp.where(row == 0, g_idx.astype(F32), jnp.where(row == 1, dest, 0.0))

    slab = jnp.concatenate([cw8_t, info_t, jnp.zeros((LANE - 2 * EPG, RT_TM), F32)], axis=0)
    cols = slab.T
    lane = lax.broadcasted_iota(jnp.int32, cols.shape, 1)
    cw8 = jnp.where(lane < EPG, cols, 0.0)
    info = jnp.where(lane < 2, pltpu.roll(cols, LANE - EPG, 1), 0.0)
    return cw8, info, pc, info_t


PLAN_ROWS = 32


def _dispatch_plan(pc_all):
    lane = lax.broadcasted_iota(jnp.int32, (1, ROUTE_W), 1)
    is_g = lane < N_GROUPS
    seg_len = jnp.sum(pc_all, axis=0, keepdims=True)
    seg_pad = jnp.floor((seg_len + (EX_TM - 1.0)) * (1.0 / EX_TM)) * EX_TM
    run = seg_pad + pltpu.roll(seg_pad, 1, 1)
    seg_end_pad = jnp.where(is_g, run + pltpu.roll(run, 2, 1), 0.0)
    seg_start = seg_end_pad - seg_pad
    ti = lax.broadcasted_iota(jnp.int32, (RT_NT, RT_NT), 0)
    tj = lax.broadcasted_iota(jnp.int32, (RT_NT, RT_NT), 1)
    earlier = (tj < ti).astype(BF16)
    within = jnp.dot(earlier, pc_all.astype(BF16), preferred_element_type=F32)
    used_begin = seg_end_pad - seg_len
    chunk_off = jnp.where(is_g, used_begin + within, 0.0)

    start = (lax.broadcasted_iota(jnp.int32, (PLAN_ROWS, 1), 0) * EX_TM).astype(F32)
    passed = jnp.sum(jnp.where(jnp.logical_and(is_g, start >= seg_end_pad), 1.0, 0.0), axis=1, keepdims=True)
    group = jnp.minimum(passed, N_GROUPS - 1.0)
    begin = jnp.sum(jnp.where(lane == group.astype(jnp.int32), used_begin, 0.0), axis=1, keepdims=True)
    used = jnp.where(passed >= N_GROUPS, 0.0, jnp.clip(start + EX_TM - begin, 0.0, float(EX_TM)))
    steps = jnp.where(lane == 0, group, jnp.where(lane == 1, used, 0.0))
    return chunk_off.astype(jnp.int32), pc_all.astype(jnp.int32), steps.astype(jnp.int32)


BIG = 64


def _piece_copies(off_ref, pc_ref, tile, make):
    lo = 0
    for g in range(N_GROUPS):
        rows = pc_ref[tile, g]
        n_big = rows // BIG
        for size, first, n in ((BIG, 0, n_big), (PIECE, n_big * BIG, (rows - n_big * BIG) // PIECE)):
            def body(j, carry, size=size, lo=lo + first, base=off_ref[tile, g] + first):
                make(pl.multiple_of(lo + j * size, PIECE), pl.multiple_of(base + j * size, PIECE), size)
                return carry

            lax.fori_loop(0, n, body, 0)
        lo = lo + rows


def _piece_waits(pc_ref, tile, wait):
    n_big = 0
    n_small = 0
    for g in range(N_GROUPS):
        rows = pc_ref[tile, g]
        n_big = n_big + rows // BIG
        n_small = n_small + (rows % BIG) // PIECE
    for size, n in ((BIG, n_big), (PIECE, n_small)):
        def body(j, carry, size=size):
            wait(size)
            return carry

        lax.fori_loop(0, n, body, 0)


XW = D + ROUTE_W


def _dispatch_kernel(off_ref, pc_ref, h_ref, infot_ref, cw8_ref, xs_in, xs_out, x_scr, sem):
    del xs_in
    i = pl.program_id(0)
    slot = i % 2
    dest = infot_ref[1:2, :]
    row = lax.broadcasted_iota(jnp.int32, (R_LOC, RT_TM), 0).astype(F32)
    sel = (row == dest).astype(BF16)
    cw = cw8_ref[...]
    hi = cw.astype(BF16).astype(F32)
    mid = (cw - hi).astype(BF16).astype(F32)
    low = (cw - hi - mid).astype(BF16).astype(F32)
    pieces = (hi + pltpu.roll(mid, EPG, 1) + pltpu.roll(low, 2 * EPG, 1)).astype(BF16)
    rows = jnp.concatenate([h_ref[...], pieces], axis=1)
    x_scr[slot] = jnp.dot(sel, rows, preferred_element_type=F32).astype(BF16)

    def x_copy(s, src, dst, rows):
        return pltpu.make_async_copy(x_scr.at[s, pl.ds(src, rows)], xs_out.at[pl.ds(dst, rows)],
                                     sem.at[s])

    _piece_copies(off_ref, pc_ref, i, lambda src, dst, rows: x_copy(slot, src, dst, rows).start())

    def wait_tile(tile, s):
        _piece_waits(pc_ref, tile, lambda rows: x_copy(s, 0, 0, rows).wait())

    @pl.when(i > 0)
    def _():
        wait_tile(i - 1, 1 - slot)

    @pl.when(i == RT_NT - 1)
    def _():
        wait_tile(i, slot)


def _dispatch(chunk_off, pc, h2, info_t, cw8, xs0):
    grid_spec = pltpu.PrefetchScalarGridSpec(
        num_scalar_prefetch=2,
        grid=(RT_NT,),
        in_specs=[pl.BlockSpec((RT_TM, D), lambda i, o, p: (i, 0)),
                  pl.BlockSpec((None, EPG, RT_TM), lambda i, o, p: (i, 0, 0)),
                  pl.BlockSpec((RT_TM, ROUTE_W), lambda i, o, p: (i, 0)),
                  pl.BlockSpec(memory_space=pl.ANY)],
        out_specs=pl.BlockSpec(memory_space=pl.ANY),
        scratch_shapes=[pltpu.VMEM((2, R_LOC, XW), BF16),
                        pltpu.SemaphoreType.DMA((2,))])
    return pl.pallas_call(
        _dispatch_kernel,
        out_shape=jax.ShapeDtypeStruct((R_MAX, XW), BF16),
        grid_spec=grid_spec,
        input_output_aliases={5: 0},
        compiler_params=_cparams(("arbitrary",)),
        name="dispatch",
    )(chunk_off, pc, h2, info_t, cw8, xs0)


def _expert_kernel(steps_ref, x_ref, wg_ref, wu_ref, wd_ref, y_ref, a_scr):
    k = pl.program_id(0)
    used = steps_ref[k, 1]

    def run(rows):
        x = x_ref[rows, 0:D]
        cw = x_ref[rows, D:XW].astype(F32)
        lane = lax.broadcasted_iota(jnp.int32, cw.shape, 1)
        for j in range(EPG):
            mine = jnp.logical_and((lane & (EPG - 1)) == j, lane < 3 * EPG)
            w = jnp.sum(jnp.where(mine, cw, 0.0), axis=-1, keepdims=True)
            a = (jax.nn.silu(jnp.dot(x, wg_ref[j], preferred_element_type=F32))
                 * jnp.dot(x, wu_ref[j], preferred_element_type=F32))
            a_scr[rows, j * FF:(j + 1) * FF] = (a * w).astype(BF16)
        y_ref[rows, :] = jnp.dot(a_scr[rows, :], wd_ref[...], preferred_element_type=F32).astype(BF16)

    quarter = EX_TM // 4

    @pl.when(used > 2 * quarter)
    def _():
        run(slice(0, EX_TM))

    for rows, above in ((2 * quarter, quarter), (quarter, 0)):
        @pl.when(jnp.logical_and(used > above, used <= rows))
        def _(rows=rows):
            run(slice(EX_TM - rows, EX_TM))
            y_ref[0:EX_TM - rows, :] = jnp.zeros((EX_TM - rows, D), BF16)

    @pl.when(used == 0)
    def _():
        y_ref[...] = jnp.zeros_like(y_ref)


def _experts(steps, xs, wg, wu, wd):
    grid_spec = pltpu.PrefetchScalarGridSpec(
        num_scalar_prefetch=1,
        grid=(EX_NT,),
        in_specs=[pl.BlockSpec((EX_TM, XW), lambda k, st: (k, 0)),
                  pl.BlockSpec((EPG, D, FF), lambda k, st: (st[k, 0], 0, 0)),
                  pl.BlockSpec((EPG, D, FF), lambda k, st: (st[k, 0], 0, 0)),
                  pl.BlockSpec((EPG * FF, D), lambda k, st: (st[k, 0], 0))],
        out_specs=pl.BlockSpec((EX_TM, D), lambda k, st: (k, 0)),
        scratch_shapes=[pltpu.VMEM((EX_TM, EPG * FF), BF16)])
    return pl.pallas_call(
        _expert_kernel,
        out_shape=jax.ShapeDtypeStruct((R_MAX, D), BF16),
        grid_spec=grid_spec,
        compiler_params=_cparams(("arbitrary",)),
        name="experts",
    )(steps, xs, wg, wu, wd.reshape(N_EXP * FF, D))


CB_NPT = N_PROMPT // RT_TM


def _combine_kernel(off_ref, pc_ref, info_ref, x1_ref, mod_ref, g_ref, ys_hbm,
                    yp_ref, ysm_ref, stage, sem):
    i = pl.program_id(0)
    slot = i % 2

    def copy(s, dst, src, rows):
        return pltpu.make_async_copy(ys_hbm.at[pl.ds(src, rows)], stage.at[s, pl.ds(dst, rows)],
                                     sem.at[s])

    def fetch(tile, s):
        _piece_copies(off_ref, pc_ref, tile, lambda loc, glob, rows: copy(s, loc, glob, rows).start())

    @pl.when(i == 0)
    def _():
        stage[...] = jnp.zeros_like(stage)
        fetch(0, 0)

    @pl.when(i + 1 < RT_NT)
    def _():
        fetch(i + 1, 1 - slot)

    _piece_waits(pc_ref, i, lambda rows: copy(slot, 0, 0, rows).wait())

    dest = info_ref[...][:, 1:2]
    col = lax.broadcasted_iota(jnp.int32, (RT_TM, R_STAGE), 1).astype(F32)
    moe = jnp.dot((col == dest).astype(BF16), stage[slot], preferred_element_type=F32)
    mod = mod_ref[...]
    out = _rms(x1_ref[...] + mod[:, 5 * D:6 * D] * moe) * g_ref[...]

    @pl.when(i < CB_NPT)
    def _():
        yp_ref[...] = out

    @pl.when(i >= CB_NPT)
    def _():
        ysm_ref[...] = out


def _combine(chunk_off, pc, info, x1, mod3, fg, ys):
    npt = CB_NPT
    grid_spec = pltpu.PrefetchScalarGridSpec(
        num_scalar_prefetch=2,
        grid=(RT_NT,),
        in_specs=[pl.BlockSpec((RT_TM, ROUTE_W), lambda i, o, p: (i, 0)),
                  pl.BlockSpec((RT_TM, D), lambda i, o, p: (i, 0)),
                  pl.BlockSpec((None, 1, 6 * D), lambda i, o, p: (_mod_row(RT_TM)(i), 0, 0)),
                  pl.BlockSpec((1, D), lambda i, o, p: (0, 0)),
                  pl.BlockSpec(memory_space=pl.ANY)],
        out_specs=(pl.BlockSpec((RT_TM, D), lambda i, o, p: (jnp.minimum(i, npt - 1), 0)),
                   pl.BlockSpec((RT_TM, D), lambda i, o, p: (jnp.maximum(i - npt, 0), 0))),
        scratch_shapes=[pltpu.VMEM((2, R_STAGE, D), BF16),
                        pltpu.SemaphoreType.DMA((2,))])
    return pl.pallas_call(
        _combine_kernel,
        out_shape=(jax.ShapeDtypeStruct((N_PROMPT, D), F32),
                   jax.ShapeDtypeStruct((N_SAMPLE, D), F32)),
        grid_spec=grid_spec,
        compiler_params=_cparams(("arbitrary",)),
        name="combine",
    )(chunk_off, pc, info, x1, mod3, fg, ys)


def kernel(x_prompt, x_sample, cache_diff_k, cache_diff_v, state_ret_fwd, state_ret_bwd, c, c_ctx,
           w_ada, b_ada, norm1_g, norm2_g, w_in, ret_decay_fwd, ret_decay_bwd, ret_norm_g,
           diff_lambda_q1, diff_lambda_k1, diff_lambda_q2, diff_lambda_k2, diff_subln_g,
           w_ret_o, w_diff_o, w_o, router_group_w, router_group_b, router_expert_w, router_expert_b,
           moe_w_gate, moe_w_up, moe_w_down, final_norm_g):
    l = 0
    lam_init = 0.8 - 0.6 * math.exp(-0.3 * l)
    row = lambda a: a[l].astype(F32)[None, :]

    xp = x_prompt.reshape(N_PROMPT, D)
    xs = x_sample.reshape(N_SAMPLE, D)
    mod3, aux, brt = _modulation(
        lam_init, c_ctx[None, :], c, w_ada[l], b_ada[l][None, :],
        [row(diff_lambda_q1), row(diff_lambda_k1), row(diff_lambda_q2), row(diff_lambda_k2),
         row(ret_decay_fwd), row(ret_decay_bwd), row(router_group_b), row(router_expert_b)])

    proj, kt32, v32 = _inproj(xp, xs, mod3, norm1_g[l][None, :], w_in[l])

    cos_t, sin_t = _rope_tables()
    cache_kt = jnp.transpose(cache_diff_k[:, l], (0, 2, 3, 4, 1)).reshape(4, D, PAST)
    (diff_act, ret_act, s_f, s_b, wg_bf, wu_bf, wd_bf, xs0) = _mixers(
        aux, proj, cache_kt, cache_diff_v, cos_t, sin_t, diff_subln_g[l][None, :],
        state_ret_fwd[:, l], state_ret_bwd[:, l], ret_norm_g[l][None, :],
        (moe_w_gate[l], moe_w_up[l], moe_w_down[l]))

    x1, h2, cw8, info, info_t, chunk_off, pc, steps = _outproj(
        ret_act, diff_act, proj, xp, xs, mod3, norm2_g[l][None, :],
        w_ret_o[l], w_diff_o[l], w_o[l], router_group_w[l].T, router_expert_w[l].T, brt)
    xs_sorted = _dispatch(chunk_off, pc, h2, info_t, cw8, xs0)
    y_sorted = _experts(steps, xs_sorted, wg_bf, wu_bf, wd_bf)
    yp, ys = _combine(chunk_off, pc, info, x1, mod3, final_norm_g[None, :], y_sorted)

    return (yp.reshape(16, T_P, D), ys.reshape(4, T_S, D),
            jnp.transpose(kt32.reshape(16, DIFF_H, 2, DIFF_HD, T_P), (0, 4, 1, 2, 3))[:, None],
            v32.reshape(16, 1, T_P, DIFF_H, 2 * DIFF_HD),
            s_f.reshape(16, 1, RET_H, RET_KD, RET_VD), s_b.reshape(16, 1, RET_H, RET_KD, RET_VD))
```

```python
import functools
import math

import jax
import jax.numpy as jnp
import numpy as np
from jax import lax
from jax.experimental import pallas as pl
from jax.experimental.pallas import tpu as pltpu

F32 = jnp.float32
BF16 = jnp.bfloat16

D = 1024
N_PROMPT = 16 * 256
N_SAMPLE = 4 * 1024
N_TOK = N_PROMPT + N_SAMPLE
T_P = 256
T_S = 1024
PAST = 512
GRID_W = 64
RET_H = 4
RET_KD = 128
RET_VD = 256
DIFF_H = 8
DIFF_HD = 64
ROPE_BASE = 10000.0
N_GROUPS = 4
EPG = 8
N_EXP = 32
FF = 256
EPS = 1e-6
IN_W = 8192
LANE = 128
N_CHUNK = IN_W // LANE
C_RQ, C_RK, C_RV, C_RG, C_DQ, C_DK, C_DV, C_GR, C_GD = 0, 4, 8, 16, 24, 32, 40, 48, 56
ROUTE_W = 128
SEQ_BLK = 1024
N_PBLK = N_PROMPT // SEQ_BLK
V7X_VMEM_BYTES = 64 * 1024 * 1024
VMEM_LIMIT = V7X_VMEM_BYTES - 8 * 1024 * 1024


def _cparams(sem):
    return pltpu.CompilerParams(dimension_semantics=sem, vmem_limit_bytes=VMEM_LIMIT)


def _mod_row(tile_rows):
    def f(i):
        start = i * tile_rows
        return jnp.where(start < N_PROMPT, 0, 1 + (start - N_PROMPT) // T_S)
    return f


def _rms(x):
    return x * lax.rsqrt(jnp.mean(x * x, axis=-1, keepdims=True) + EPS)


def _ordered_after(dst, src):
    m = None
    for a in range(src.shape[0] // 8):
        for b in range(src.shape[1] // LANE):
            part = src[a * 8:(a + 1) * 8, b * LANE:(b + 1) * LANE]
            m = part if m is None else jnp.maximum(m, part)
    never = jnp.logical_and(m != m, m == m)
    return jnp.where(jnp.tile(never, (dst.shape[0] // 8, 1)), 0.0, dst)


MOD_ROWS = 8


AUX_LAM = 0
AUX_LG = 1


def _mod_kernel(lam_init, ctx_ref, c_ref, w_ref, b_ref, lq1_ref, lk1_ref, lq2_ref, lk2_ref, df_ref, db_ref,
                gb_ref, eb_ref, o_ref, aux_ref, brt_ref):
    cond = jnp.concatenate([ctx_ref[...], c_ref[...],
                            jnp.zeros((MOD_ROWS - 1 - c_ref.shape[0], D), F32)], axis=0)
    s = jax.nn.silu(cond)
    out = jnp.dot(s.astype(BF16), w_ref[...].astype(BF16), preferred_element_type=F32) + b_ref[...]
    for r in range(MOD_ROWS):
        o_ref[r] = out[r:r + 1, :]

    def lane_row(parts):
        used = sum(p.shape[1] for p in parts)
        return jnp.concatenate(parts + [jnp.zeros((1, LANE - used), F32)], axis=1)

    lam = (jnp.exp(jnp.sum(lq1_ref[...] * lk1_ref[...], axis=1, keepdims=True))
           - jnp.exp(jnp.sum(lq2_ref[...] * lk2_ref[...], axis=1, keepdims=True)) + lam_init)
    rows = [lane_row([lam, jnp.full((1, 1), 1.0 - lam_init, F32)]),
            lane_row([jax.nn.log_sigmoid(df_ref[...])]),
            lane_row([jax.nn.log_sigmoid(db_ref[...])])]
    aux_ref[...] = jnp.concatenate(rows + [jnp.zeros((8 - len(rows), LANE), F32)], axis=0)
    bias = lane_row([gb_ref[...], jnp.zeros((1, ROUTE_E0 - N_GROUPS), F32), eb_ref[...]])
    brt_ref[...] = jnp.tile(bias, (8, 1)).T[:, 0:1]


def _modulation(lam_init, c_ctx, c, w_ada, b_ada, small):
    tn = 1536
    whole = lambda a: pl.BlockSpec(a.shape, lambda j: (0, 0))
    return pl.pallas_call(
        functools.partial(_mod_kernel, lam_init),
        out_shape=(jax.ShapeDtypeStruct((MOD_ROWS, 1, 6 * D), F32),
                   jax.ShapeDtypeStruct((8, LANE), F32),
                   jax.ShapeDtypeStruct((ROUTE_W, 1), F32)),
        grid=(6 * D // tn,),
        in_specs=[pl.BlockSpec((1, D), lambda j: (0, 0)),
                  pl.BlockSpec(c.shape, lambda j: (0, 0)),
                  pl.BlockSpec((D, tn), lambda j: (0, j)),
                  pl.BlockSpec((1, tn), lambda j: (0, j))] + [whole(a) for a in small],
        out_specs=(pl.BlockSpec((MOD_ROWS, 1, tn), lambda j: (0, 0, j)),
                   pl.BlockSpec((8, LANE), lambda j: (0, 0)),
                   pl.BlockSpec((ROUTE_W, 1), lambda j: (0, 0))),
        compiler_params=_cparams(("arbitrary",)),
        name="mod",
    )(c_ctx, c, w_ada, b_ada, *small)


IP_TM = 512
IP_TN = 2048
IP_NPT = N_PROMPT // IP_TM


IP_KV_TILE = C_DK * LANE // IP_TN
IP_SPT = IP_TM // T_P


def _inproj_kernel(xp_ref, xs_ref, mod_ref, n1_ref, w_ref, proj_ref, kt_ref, v32_ref, w_scr):
    j = pl.program_id(0)
    i = pl.program_id(1)

    @pl.when(i == 0)
    def _():
        w_scr[...] = w_ref[...].astype(BF16)

    x = jnp.where(i < IP_NPT, xp_ref[...], xs_ref[...])
    mod = mod_ref[...]
    h = (_rms(x) * n1_ref[...] * (1.0 + mod[:, D:2 * D]) + mod[:, 0:D]).astype(BF16)
    acc = jnp.dot(h, w_scr[...], preferred_element_type=F32)
    for c in range(IP_TN // LANE):
        proj_ref[c] = acc[:, c * LANE:(c + 1) * LANE].astype(BF16)

    @pl.when(jnp.logical_and(j == IP_KV_TILE, i < IP_NPT))
    def _():
        for s in range(IP_SPT):
            kt_ref[s] = acc[s * T_P:(s + 1) * T_P, :D].T
        v32_ref[...] = acc[:, D:]


def _inproj(xp, xs, mod3, n1, w_in):
    npt = IP_NPT
    cpt = IP_TN // LANE

    def kv_tile(j, i):
        return jnp.where(j < IP_KV_TILE, 0, jnp.where(j == IP_KV_TILE, jnp.minimum(i, npt - 1), npt - 1))

    return pl.pallas_call(
        _inproj_kernel,
        out_shape=(jax.ShapeDtypeStruct((N_CHUNK, N_TOK, LANE), BF16),
                   jax.ShapeDtypeStruct((N_PROMPT // T_P, D, T_P), F32),
                   jax.ShapeDtypeStruct((N_PROMPT, D), F32)),
        grid=(IN_W // IP_TN, N_TOK // IP_TM),
        in_specs=[pl.BlockSpec((IP_TM, D), lambda j, i: (jnp.minimum(i, npt - 1), 0)),
                  pl.BlockSpec((IP_TM, D), lambda j, i: (jnp.maximum(i - npt, 0), 0)),
                  pl.BlockSpec((None, 1, 6 * D), lambda j, i: (_mod_row(IP_TM)(i), 0, 0)),
                  pl.BlockSpec((1, D), lambda j, i: (0, 0)),
                  pl.BlockSpec((D, IP_TN), lambda j, i: (0, j))],
        out_specs=(pl.BlockSpec((cpt, IP_TM, LANE), lambda j, i: (j, i, 0)),
                   pl.BlockSpec((IP_SPT, D, T_P), lambda j, i: (kv_tile(j, i), 0, 0)),
                   pl.BlockSpec((IP_TM, D), lambda j, i: (kv_tile(j, i), 0))),
        scratch_shapes=[pltpu.VMEM((D, IP_TN), BF16)],
        compiler_params=_cparams(("arbitrary", "arbitrary")),
        name="inproj",
    )(xp, xs, mod3, n1, w_in)


def _decay_mask(t, lgf, lgb):
    ii = lax.broadcasted_iota(jnp.int32, (t, t), 0)
    jj = lax.broadcasted_iota(jnp.int32, (t, t), 1)
    rel = (ii - jj).astype(F32)
    e = jnp.exp(jnp.where(rel >= 0.0, lgf, -lgb) * rel)
    return jnp.where(rel == 0.0, 2.0, e) * (RET_KD ** -0.5)


def _ret_parts(lgf, lgb, q_ref, k_ref, v_ref, rg_ref, s0f_ref, s0b_ref, g_ref,
               o_ref, sf_ref, sb_ref, dm_ref):
    gain = g_ref[...]
    nt = (((1,), (1,)), ((), ()))
    tn = (((0,), (0,)), ((), ()))

    def finish(o, rg):
        d = o - jnp.mean(o, axis=-1, keepdims=True)
        y = d * lax.rsqrt(jnp.mean(d * d, axis=-1, keepdims=True) + EPS) * gain
        return (jax.nn.silu(rg.astype(F32)) * y).astype(BF16)

    def vcat(r):
        return jnp.concatenate([v_ref[0, r, :], v_ref[1, r, :]], axis=1)

    def gcat(r):
        return jnp.concatenate([rg_ref[0, r, :], rg_ref[1, r, :]], axis=1)

    def build_mask():
        dm_ref[...] = _decay_mask(T_P, lgf, lgb)

    n_chunk = SEQ_BLK // T_P
    chunks = [slice(c * T_P, (c + 1) * T_P) for c in range(n_chunk)]
    t = lax.broadcasted_iota(jnp.int32, (T_P, 1), 0).astype(F32)
    kdf = jnp.exp(lgf * (T_P - 1.0 - t)) * (RET_KD ** -0.5)
    kdb = jnp.exp(lgb * t) * (RET_KD ** -0.5)

    def intra(r):
        sc = lax.dot_general(q_ref[r, :], k_ref[r, :], nt, preferred_element_type=F32)
        return jnp.dot((sc * dm_ref[...]).astype(BF16), vcat(r), preferred_element_type=F32)

    def key_state(r, kd):
        kw = (k_ref[r, :].astype(F32) * kd).astype(BF16)
        return lax.dot_general(kw, vcat(r), tn, preferred_element_type=F32)

    def context_part():
        for s, r in enumerate(chunks):
            o_ref[r, :] = finish(intra(r), gcat(r))
            sf_ref[s] = key_state(r, kdf)
            sb_ref[s] = key_state(r, kdb)

    def latent_part():
        qdf = jnp.exp(lgf * (t + 1.0))
        qdb = jnp.exp(lgb * (T_P - t))
        span = jnp.full((1, 1), float(T_P), F32)
        cf = jnp.exp(lgf * span)
        cb = jnp.exp(lgb * span)
        before_b = [None] * n_chunk
        state = s0b_ref[...]
        for c in reversed(range(n_chunk)):
            before_b[c] = state
            if c > 0:
                state = cb * state + key_state(chunks[c], kdb)
        state = s0f_ref[...]
        for c, r in enumerate(chunks):
            qf = q_ref[r, :].astype(F32)
            o = (intra(r)
                 + jnp.dot((qf * qdf).astype(BF16), state.astype(BF16), preferred_element_type=F32)
                 + jnp.dot((qf * qdb).astype(BF16), before_b[c].astype(BF16), preferred_element_type=F32))
            o_ref[r, :] = finish(o, gcat(r))
            if c + 1 < n_chunk:
                state = cf * state + key_state(r, kdf)

    return build_mask, context_part, latent_part


ATT_TQ = 1024
ATT_HPS = 2
ATT_QSCALE = (DIFF_HD ** -0.5) * math.log2(math.e)


N_SIDE = 3


def _mixer_kernel(aux_ref, q_ref, k_ref, v_ref, ckt_ref, cv_ref, cos_ref, sin_ref, g_ref,
                  rq_ref, rk_ref, rv_ref, rg_ref, s0f_ref, s0b_ref, gn_ref, *rest):
    side_in = rest[:N_SIDE]
    o_ref, ro_ref, sf_ref, sb_ref = rest[N_SIDE:N_SIDE + 4]
    side_out = rest[N_SIDE + 4:2 * N_SIDE + 4]
    xs0_ref, q_scr, k_scr, v_scr, dm_scr = rest[2 * N_SIDE + 4:]
    i = pl.program_id(0)
    rh = pl.program_id(1)
    build_mask, ret_context, ret_latent = _ret_parts(
        aux_ref[AUX_LG, rh], aux_ref[AUX_LG + 1, rh], rq_ref, rk_ref, rv_ref, rg_ref, s0f_ref, s0b_ref, gn_ref,
        ro_ref, sf_ref, sb_ref, dm_scr.at[rh])

    @pl.when(i == 0)
    def _():
        build_mask()

    def side_jobs():
        for src, dst in zip(side_in, side_out):
            dst[...] = src[...].astype(BF16)
        xs0_ref[...] = jnp.zeros_like(xs0_ref)

    lam = aux_ref[AUX_LAM, 0]
    out_scale = aux_ref[AUX_LAM, 1]
    gain = g_ref[...] * out_scale
    lane = lax.broadcasted_iota(jnp.int32, (1, LANE), 1)
    first = lane < DIFF_HD
    nt = (((1,), (1,)), ((), ()))

    def halves(q):
        zero = jnp.zeros_like(q)
        return jnp.where(first, q, zero), jnp.where(first, zero, q)

    def weights(s):
        return jnp.exp2(s - jnp.max(s, axis=-1, keepdims=True)).astype(BF16)

    def finish(of0, of1):
        o = of0[:, :LANE] / of0[:, LANE:] - lam * (of1[:, :LANE] / of1[:, LANE:])
        return (_rms(o) * gain).astype(BF16)

    @pl.when(i < N_PBLK)
    def _():
        ret_context()
        n_seq = SEQ_BLK // T_P
        rows = [slice(s * T_P, (s + 1) * T_P) for s in range(n_seq)]
        ones = jnp.ones((T_P, LANE), BF16)
        for hh in range(ATT_HPS):
            cols = slice(hh * LANE, (hh + 1) * LANE)
            q0, q1 = halves((q_ref[hh].astype(F32) * ATT_QSCALE).astype(BF16))
            s0 = jnp.concatenate([lax.dot_general(q0[r], k_ref[hh, r, :], nt, preferred_element_type=F32)
                                  for r in rows], axis=0)
            s1 = jnp.concatenate([lax.dot_general(q1[r], k_ref[hh, r, :], nt, preferred_element_type=F32)
                                  for r in rows], axis=0)
            e0 = weights(s0)
            e1 = weights(s1)
            for r in rows:
                v1 = jnp.concatenate([v_ref[hh, r, :], ones], axis=1)
                o_ref[r, cols] = finish(jnp.dot(e0[r], v1, preferred_element_type=F32),
                                        jnp.dot(e1[r], v1, preferred_element_type=F32))

    @pl.when(i >= N_PBLK)
    def _():
        side_jobs()
        ret_latent()
        cos = cos_ref[...]
        sin = sin_ref[...]
        low = (lax.broadcasted_iota(jnp.int32, (T_S, LANE), 1) & 16) == 0

        def rope(x):
            xs = jnp.where(low, pltpu.roll(x, LANE - 16, 1), pltpu.roll(x, 16, 1))
            return x * cos + xs * sin

        for hh in range(ATT_HPS):
            cols = slice(hh * LANE, (hh + 1) * LANE)
            head = pl.program_id(1) * ATT_HPS + hh
            q_scr[hh] = (rope(q_ref[hh].astype(F32)) * ATT_QSCALE).astype(BF16)
            k_scr[hh] = rope(k_ref[hh].astype(F32)).astype(BF16)
            ckt = ckt_ref[cols, :].astype(BF16)
            v_scr[hh, 0:T_S, 0:LANE] = v_ref[hh]
            v_scr[hh, T_S:T_S + PAST, 0:LANE] = cv_ref[:, head, :].astype(BF16)
            v_scr[hh, :, LANE:2 * LANE] = jnp.ones((T_S + PAST, LANE), BF16)

            def scores(qh, hh=hh, ckt=ckt):
                return jnp.concatenate([lax.dot_general(qh, k_scr[hh], nt, preferred_element_type=F32),
                                        jnp.dot(qh, ckt, preferred_element_type=F32)], axis=1)

            for b in range(T_S // ATT_TQ):
                r = slice(b * ATT_TQ, (b + 1) * ATT_TQ)
                q0, q1 = halves(q_scr[hh, r, :])
                e0 = weights(scores(q0))
                e1 = weights(scores(q1))
                o_ref[r, cols] = finish(jnp.dot(e0, v_scr[hh], preferred_element_type=F32),
                                        jnp.dot(e1, v_scr[hh], preferred_element_type=F32))


def _mixers(aux, proj, cache_k, cache_v, cos_t, sin_t, subln_g, s0f, s0b, gnorm, side_weights):
    nb = N_TOK // SEQ_BLK
    spb = SEQ_BLK // T_P
    smp = lambda i: jnp.maximum(i - N_PBLK, 0)
    pmt = lambda i: jnp.minimum(i, N_PBLK - 1)
    pmh = lambda i, h: jnp.where(i < N_PBLK, h, RET_H - 1)
    state_in = pl.BlockSpec((None, None, RET_KD, RET_VD), lambda i, h: (smp(i), h, 0, 0))
    state_out = pl.BlockSpec((spb, None, RET_KD, RET_VD), lambda i, h: (pmt(i), pmh(i, h), 0, 0))

    n_hp = DIFF_H // ATT_HPS
    assert n_hp == RET_H
    n_steps = (nb - N_PBLK) * n_hp
    step = lambda i, h: jnp.maximum((i - N_PBLK) * n_hp + h, 0)
    exp_per_step = N_EXP // n_steps
    exp_slice = lambda i, h: (step(i, h), 0, 0)
    row_slice = lambda i, h: (step(i, h), 0)
    up_spec = pl.BlockSpec((exp_per_step, D, FF), exp_slice)
    down_spec = pl.BlockSpec((exp_per_step, FF, D), exp_slice)
    side_specs = [up_spec, up_spec, down_spec]
    side_shapes = [jax.ShapeDtypeStruct(a.shape, BF16) for a in side_weights]
    zrows = R_MAX // n_steps

    return pl.pallas_call(
        _mixer_kernel,
        out_shape=(jax.ShapeDtypeStruct((N_TOK, DIFF_H * 2 * DIFF_HD), BF16),
                   jax.ShapeDtypeStruct((N_TOK, RET_H * RET_VD), BF16),
                   jax.ShapeDtypeStruct((16, RET_H, RET_KD, RET_VD), F32),
                   jax.ShapeDtypeStruct((16, RET_H, RET_KD, RET_VD), F32),
                   *side_shapes,
                   jax.ShapeDtypeStruct((R_MAX, XW), BF16)),
        grid=(nb, n_hp),
        in_specs=[pl.BlockSpec(memory_space=pltpu.SMEM),
                  pl.BlockSpec((ATT_HPS, SEQ_BLK, LANE), lambda i, h: (C_DQ // ATT_HPS + h, i, 0)),
                  pl.BlockSpec((ATT_HPS, SEQ_BLK, LANE), lambda i, h: (C_DK // ATT_HPS + h, i, 0)),
                  pl.BlockSpec((ATT_HPS, SEQ_BLK, LANE), lambda i, h: (C_DV // ATT_HPS + h, i, 0)),
                  pl.BlockSpec((None, ATT_HPS * LANE, PAST), lambda i, h: (smp(i), h, 0)),
                  pl.BlockSpec((None, None, PAST, DIFF_H, LANE), lambda i, h: (smp(i), 0, 0, 0, 0)),
                  pl.BlockSpec((T_S, LANE), lambda i, h: (0, 0)),
                  pl.BlockSpec((T_S, LANE), lambda i, h: (0, 0)),
                  pl.BlockSpec((1, LANE), lambda i, h: (0, 0)),
                  pl.BlockSpec((None, SEQ_BLK, LANE), lambda i, h: (C_RQ + h, i, 0)),
                  pl.BlockSpec((None, SEQ_BLK, LANE), lambda i, h: (C_RK + h, i, 0)),
                  pl.BlockSpec((2, SEQ_BLK, LANE), lambda i, h: (C_RV // 2 + h, i, 0)),
                  pl.BlockSpec((2, SEQ_BLK, LANE), lambda i, h: (C_RG // 2 + h, i, 0)),
                  state_in, state_in,
                  pl.BlockSpec((1, RET_VD), lambda i, h: (0, h)),
                  *side_specs],
        out_specs=(pl.BlockSpec((SEQ_BLK, ATT_HPS * LANE), lambda i, h: (i, h)),
                   pl.BlockSpec((SEQ_BLK, RET_VD), lambda i, h: (i, h)),
                   state_out, state_out,
                   *side_specs,
                   pl.BlockSpec((zrows, XW), row_slice)),
        scratch_shapes=[pltpu.VMEM((ATT_HPS, T_S, LANE), BF16),
                        pltpu.VMEM((ATT_HPS, T_S, LANE), BF16),
                        pltpu.VMEM((ATT_HPS, T_S + PAST, 2 * LANE), BF16),
                        pltpu.VMEM((RET_H, T_P, T_P), F32)],
        compiler_params=_cparams(("arbitrary", "arbitrary")),
        name="mixers",
    )(aux, proj, proj, proj, cache_k, cache_v, cos_t, sin_t, subln_g,
      proj, proj, proj, proj, s0f, s0b, gnorm, *side_weights)


def _rope_tables():
    n_rows = T_S // GRID_W
    row = np.repeat(np.arange(n_rows), GRID_W).astype(np.float64)
    col = np.tile(np.arange(GRID_W), n_rows).astype(np.float64)
    n_freq = DIFF_HD // 4
    inv = ROPE_BASE ** (-np.arange(n_freq, dtype=np.float64) / n_freq)

    def axis_tables(pos):
        ang = pos[:, None] * inv[None, :]
        c = np.cos(ang)
        s = np.sin(ang)
        return np.concatenate([c, c], axis=-1), np.concatenate([-s, s], axis=-1)

    cr, sr = axis_tables(row)
    cc, sc = axis_tables(col)
    cos_h = np.concatenate([cr, cc], axis=-1)
    sin_h = np.concatenate([sr, sc], axis=-1)
    return (jnp.asarray(np.concatenate([cos_h, cos_h], axis=-1), F32),
            jnp.asarray(np.concatenate([sin_h, sin_h], axis=-1), F32))


OP_TM = 512
OP_NPT = N_PROMPT // OP_TM


def _outproj_kernel(ra_ref, da_ref, gr_ref, gd_ref, xp_ref, xs_ref, mod_ref, n2_ref,
                    wro32_ref, wdo32_ref, wo32_ref, rgw_ref, rew_ref, brt_ref,
                    x1_ref, h2_ref, cw8_ref, info_ref, infot_ref, off_ref, pctab_ref, steps_ref,
                    m_scr, wro_ref, wdo_ref, wo_ref, wrt_ref, pc_scr, lg_scr):
    i = pl.program_id(0)

    @pl.when(i == 0)
    def _():
        wro_ref[...] = wro32_ref[...].astype(BF16)
        wdo_ref[...] = wdo32_ref[...].astype(BF16)
        wo_ref[...] = wo32_ref[...].astype(BF16)
        gap = jnp.zeros((ROUTE_E0 - N_GROUPS, D), F32)
        pad = jnp.zeros((ROUTE_W - ROUTE_E0 - N_EXP, D), F32)
        wr = jnp.concatenate([rgw_ref[...], gap, rew_ref[...], pad], axis=0)
        wrt_ref[...] = wr.astype(BF16)

        lg_scr[...] = jnp.zeros_like(lg_scr)

    def route_previous():
        cw8, info, pc, info_t = _route_cols(lg_scr[...])
        cw8_ref[...] = cw8
        info_ref[...] = info
        infot_ref[...] = info_t
        pc_scr[pl.ds(jnp.maximum(i - 1, 0), 1), :] = pc
        return cw8, info

    @pl.when(i < RT_NT)
    def _():
        routed = route_previous()
        ret_out = jnp.dot(ra_ref[...], wro_ref[...], preferred_element_type=F32)
        diff_out = jnp.dot(da_ref[...], wdo_ref[...], preferred_element_type=F32)
        for c in range(D // LANE):
            sl = slice(c * LANE, (c + 1) * LANE)
            m = (jax.nn.sigmoid(gr_ref[c].astype(F32)) * ret_out[:, sl]
                 + jax.nn.sigmoid(gd_ref[c].astype(F32)) * diff_out[:, sl])
            if c == 0:
                m = _ordered_after(_ordered_after(m, routed[0]), routed[1])
            m_scr[:, sl] = m.astype(BF16)
        mix = jnp.dot(m_scr[...], wo_ref[...], preferred_element_type=F32)
        mod = mod_ref[...]

        x1 = jnp.where(i < OP_NPT, xp_ref[...], xs_ref[...]) + mod[:, 2 * D:3 * D] * mix
        x1_ref[...] = x1
        h2 = (_rms(x1) * n2_ref[...] * (1.0 + mod[:, 4 * D:5 * D]) + mod[:, 3 * D:4 * D]).astype(BF16)
        h2_ref[...] = h2
        lg_scr[...] = lax.dot_general(wrt_ref[...], h2, (((1,), (1,)), ((), ())),
                                      preferred_element_type=F32) + brt_ref[...]

    @pl.when(i == RT_NT)
    def _():
        route_previous()
        off_ref[...], pctab_ref[...], steps_ref[...] = _dispatch_plan(pc_scr[...])


def _outproj(ret_act, diff_act, proj, xp, xs, mod3, n2, wro, wdo, wo, rgw, rew, brt):
    assert OP_TM == RT_TM
    npt = OP_NPT
    nt = N_TOK // OP_TM
    cur = lambda i: jnp.minimum(i, nt - 1)
    prev = lambda i: jnp.maximum(i - 1, 0)
    full = lambda i: (0, 0)
    once = pl.Buffered(1)
    return pl.pallas_call(
        _outproj_kernel,
        out_shape=(jax.ShapeDtypeStruct((N_TOK, D), F32),
                   jax.ShapeDtypeStruct((N_TOK, D), BF16),
                   jax.ShapeDtypeStruct((N_TOK, ROUTE_W), F32),
                   jax.ShapeDtypeStruct((N_TOK, ROUTE_W), F32),
                   jax.ShapeDtypeStruct((RT_NT, EPG, RT_TM), F32),
                   jax.ShapeDtypeStruct((RT_NT, ROUTE_W), jnp.int32),
                   jax.ShapeDtypeStruct((RT_NT, ROUTE_W), jnp.int32),
                   jax.ShapeDtypeStruct((PLAN_ROWS, ROUTE_W), jnp.int32)),
        grid=(nt + 1,),
        in_specs=[pl.BlockSpec((OP_TM, D), lambda i: (cur(i), 0)),
                  pl.BlockSpec((OP_TM, D), lambda i: (cur(i), 0)),
                  pl.BlockSpec((8, OP_TM, LANE), lambda i: (C_GR // 8, cur(i), 0)),
                  pl.BlockSpec((8, OP_TM, LANE), lambda i: (C_GD // 8, cur(i), 0)),
                  pl.BlockSpec((OP_TM, D), lambda i: (jnp.minimum(i, npt - 1), 0)),
                  pl.BlockSpec((OP_TM, D), lambda i: (jnp.maximum(cur(i) - npt, 0), 0)),
                  pl.BlockSpec((None, 1, 6 * D), lambda i: (_mod_row(OP_TM)(cur(i)), 0, 0)),
                  pl.BlockSpec((1, D), full),
                  pl.BlockSpec((D, D), full, pipeline_mode=once),
                  pl.BlockSpec((D, D), full, pipeline_mode=once),
                  pl.BlockSpec((D, D), full, pipeline_mode=once),
                  pl.BlockSpec((N_GROUPS, D), full, pipeline_mode=once),
                  pl.BlockSpec((N_EXP, D), full, pipeline_mode=once),
                  pl.BlockSpec((ROUTE_W, 1), full)],
        out_specs=(pl.BlockSpec((OP_TM, D), lambda i: (cur(i), 0)),
                   pl.BlockSpec((OP_TM, D), lambda i: (cur(i), 0)),
                   pl.BlockSpec((OP_TM, ROUTE_W), lambda i: (prev(i), 0)),
                   pl.BlockSpec((OP_TM, ROUTE_W), lambda i: (prev(i), 0)),
                   pl.BlockSpec((None, EPG, RT_TM), lambda i: (prev(i), 0, 0)),
                   pl.BlockSpec((RT_NT, ROUTE_W), full),
                   pl.BlockSpec((RT_NT, ROUTE_W), full),
                   pl.BlockSpec((PLAN_ROWS, ROUTE_W), full)),
        scratch_shapes=[pltpu.VMEM((OP_TM, D), BF16),
                        pltpu.VMEM((D, D), BF16), pltpu.VMEM((D, D), BF16), pltpu.VMEM((D, D), BF16),
                        pltpu.VMEM((ROUTE_W, D), BF16),
                        pltpu.VMEM((RT_NT, ROUTE_W), F32),
                        pltpu.VMEM((ROUTE_W, RT_TM), F32)],
        compiler_params=_cparams(("arbitrary",)),
        name="outproj",
    )(ret_act, diff_act, proj, proj, xp, xs, mod3, n2, wro, wdo, wo, rgw, rew, brt)


RT_TM = 512
RT_NT = N_TOK // RT_TM
PIECE = 16
R_LOC = RT_TM + N_GROUPS * PIECE
R_STAGE = 640
EX_TM = 512
R_MAX = 11264
EX_NT = R_MAX // EX_TM


ROUTE_E0 = EPG


def _route_cols(lgt):
    row = lax.broadcasted_iota(jnp.int32, (EPG, RT_TM), 0)
    neg = jnp.float32(-jnp.inf)

    def first_row(cond):
        return jnp.min(jnp.where(cond, row, EPG), axis=0, keepdims=True)

    head = lgt[0:EPG]
    is_g = row < N_GROUPS
    gl = jnp.where(is_g, head, neg)
    gmax = jnp.max(gl, axis=0, keepdims=True)
    gsum = jnp.sum(jnp.where(is_g, jnp.exp(head - gmax), 0.0), axis=0, keepdims=True)
    p_top = 1.0 / gsum
    g_idx = first_row(gl == gmax)
    el = lgt[ROUTE_E0 + (N_GROUPS - 1) * EPG:ROUTE_E0 + N_GROUPS * EPG]
    for g in reversed(range(N_GROUPS - 1)):
        el = jnp.where(g_idx == g, lgt[ROUTE_E0 + g * EPG:ROUTE_E0 + (g + 1) * EPG], el)
    ee = jnp.exp(el - jnp.max(el, axis=0, keepdims=True))
    ep = ee / jnp.sum(ee, axis=0, keepdims=True)
    e1 = jnp.max(ep, axis=0, keepdims=True)
    i1 = first_row(ep == e1)
    ep2 = jnp.where(row == i1, -1.0, ep)
    e2 = jnp.max(ep2, axis=0, keepdims=True)
    i2 = first_row(ep2 == e2)
    den = e1 + e2
    cw8_t = (jnp.where(row == i1, p_top * e1 / den, 0.0)
             + jnp.where(row == i2, p_top * e2 / den, 0.0))

    onehot = (row == g_idx).astype(F32)
    ii = lax.broadcasted_iota(jnp.int32, (RT_TM, RT_TM), 0)
    jj = lax.broadcasted_iota(jnp.int32, (RT_TM, RT_TM), 1)
    earlier = (ii < jj).astype(BF16)
    prefix = jnp.dot(onehot.astype(BF16), earlier, preferred_element_type=F32)
    cnt = jnp.sum(onehot, axis=1, keepdims=True)
    pc_col = jnp.floor((cnt + (PIECE - 1.0)) * (1.0 / PIECE)) * PIECE
    row1 = lax.broadcasted_iota(jnp.int32, (EPG, 1), 0)
    lane1 = lax.broadcasted_iota(jnp.int32, (1, ROUTE_W), 1)
    lo = jnp.zeros((EPG, 1), F32)
    run = jnp.zeros((1, 1), F32)
    pc = jnp.zeros((1, ROUTE_W), F32)
    for g in range(N_GROUPS):
        pc_g = jnp.sum(jnp.where(row1 == g, pc_col, 0.0), axis=0, keepdims=True)
        lo = jnp.where(row1 == g, run, lo)
        pc = jnp.where(lane1 == g, pc_g, pc)
        run = run + pc_g
    dest = jnp.sum(onehot * (prefix + lo), axis=0, keepdims=True)
    info_t = jnp.where(row == 0, g_idx.astype(F32), jnp.where(row == 1, dest, 0.0))

    slab = jnp.concatenate([cw8_t, info_t, jnp.zeros((LANE - 2 * EPG, RT_TM), F32)], axis=0)
    cols = slab.T
    lane = lax.broadcasted_iota(jnp.int32, cols.shape, 1)
    cw8 = jnp.where(lane < EPG, cols, 0.0)
    info = jnp.where(lane < 2, pltpu.roll(cols, LANE - EPG, 1), 0.0)
    return cw8, info, pc, info_t


PLAN_ROWS = 32


def _dispatch_plan(pc_all):
    lane = lax.broadcasted_iota(jnp.int32, (1, ROUTE_W), 1)
    is_g = lane < N_GROUPS
    seg_len = jnp.sum(pc_all, axis=0, keepdims=True)
    seg_pad = jnp.floor((seg_len + (EX_TM - 1.0)) * (1.0 / EX_TM)) * EX_TM
    run = seg_pad + pltpu.roll(seg_pad, 1, 1)
    seg_end_pad = jnp.where(is_g, run + pltpu.roll(run, 2, 1), 0.0)
    seg_start = seg_end_pad - seg_pad
    ti = lax.broadcasted_iota(jnp.int32, (RT_NT, RT_NT), 0)
    tj = lax.broadcasted_iota(jnp.int32, (RT_NT, RT_NT), 1)
    earlier = (tj < ti).astype(BF16)
    within = jnp.dot(earlier, pc_all.astype(BF16), preferred_element_type=F32)
    used_begin = seg_end_pad - seg_len
    chunk_off = jnp.where(is_g, used_begin + within, 0.0)

    start = (lax.broadcasted_iota(jnp.int32, (PLAN_ROWS, 1), 0) * EX_TM).astype(F32)
    passed = jnp.sum(jnp.where(jnp.logical_and(is_g, start >= seg_end_pad), 1.0, 0.0), axis=1, keepdims=True)
    group = jnp.minimum(passed, N_GROUPS - 1.0)
    begin = jnp.sum(jnp.where(lane == group.astype(jnp.int32), used_begin, 0.0), axis=1, keepdims=True)
    used = jnp.where(passed >= N_GROUPS, 0.0, jnp.clip(start + EX_TM - begin, 0.0, float(EX_TM)))
    steps = jnp.where(lane == 0, group, jnp.where(lane == 1, used, 0.0))
    return chunk_off.astype(jnp.int32), pc_all.astype(jnp.int32), steps.astype(jnp.int32)


BIG = 64


def _piece_copies(off_ref, pc_ref, tile, make):
    lo = 0
    for g in range(N_GROUPS):
        rows = pc_ref[tile, g]
        n_big = rows // BIG
        for size, first, n in ((BIG, 0, n_big), (PIECE, n_big * BIG, (rows - n_big * BIG) // PIECE)):
            def body(j, carry, size=size, lo=lo + first, base=off_ref[tile, g] + first):
                make(pl.multiple_of(lo + j * size, PIECE), pl.multiple_of(base + j * size, PIECE), size)
                return carry

            lax.fori_loop(0, n, body, 0)
        lo = lo + rows


def _piece_waits(pc_ref, tile, wait):
    n_big = 0
    n_small = 0
    for g in range(N_GROUPS):
        rows = pc_ref[tile, g]
        n_big = n_big + rows // BIG
        n_small = n_small + (rows % BIG) // PIECE
    for size, n in ((BIG, n_big), (PIECE, n_small)):
        def body(j, carry, size=size):
            wait(size)
            return carry

        lax.fori_loop(0, n, body, 0)


XW = D + ROUTE_W


def _dispatch_kernel(off_ref, pc_ref, h_ref, infot_ref, cw8_ref, xs_in, xs_out, x_scr, sem):
    del xs_in
    i = pl.program_id(0)
    slot = i % 2
    dest = infot_ref[1:2, :]
    row = lax.broadcasted_iota(jnp.int32, (R_LOC, RT_TM), 0).astype(F32)
    sel = (row == dest).astype(BF16)
    cw = cw8_ref[...]
    hi = cw.astype(BF16).astype(F32)
    mid = (cw - hi).astype(BF16).astype(F32)
    low = (cw - hi - mid).astype(BF16).astype(F32)
    pieces = (hi + pltpu.roll(mid, EPG, 1) + pltpu.roll(low, 2 * EPG, 1)).astype(BF16)
    rows = jnp.concatenate([h_ref[...], pieces], axis=1)
    x_scr[slot] = jnp.dot(sel, rows, preferred_element_type=F32).astype(BF16)

    def x_copy(s, src, dst, rows):
        return pltpu.make_async_copy(x_scr.at[s, pl.ds(src, rows)], xs_out.at[pl.ds(dst, rows)],
                                     sem.at[s])

    _piece_copies(off_ref, pc_ref, i,
                  lambda src, dst, rows: x_copy(slot, src, dst, rows).start(priority=int(rows != BIG)))

    def wait_tile(tile, s):
        _piece_waits(pc_ref, tile, lambda rows: x_copy(s, 0, 0, rows).wait())

    @pl.when(i > 0)
    def _():
        wait_tile(i - 1, 1 - slot)

    @pl.when(i == RT_NT - 1)
    def _():
        wait_tile(i, slot)


def _dispatch(chunk_off, pc, h2, info_t, cw8, xs0):
    grid_spec = pltpu.PrefetchScalarGridSpec(
        num_scalar_prefetch=2,
        grid=(RT_NT,),
        in_specs=[pl.BlockSpec((RT_TM, D), lambda i, o, p: (i, 0)),
                  pl.BlockSpec((None, EPG, RT_TM), lambda i, o, p: (i, 0, 0)),
                  pl.BlockSpec((RT_TM, ROUTE_W), lambda i, o, p: (i, 0)),
                  pl.BlockSpec(memory_space=pl.ANY)],
        out_specs=pl.BlockSpec(memory_space=pl.ANY),
        scratch_shapes=[pltpu.VMEM((2, R_LOC, XW), BF16),
                        pltpu.SemaphoreType.DMA((2,))])
    return pl.pallas_call(
        _dispatch_kernel,
        out_shape=jax.ShapeDtypeStruct((R_MAX, XW), BF16),
        grid_spec=grid_spec,
        input_output_aliases={5: 0},
        compiler_params=_cparams(("arbitrary",)),
        name="dispatch",
    )(chunk_off, pc, h2, info_t, cw8, xs0)


def _expert_kernel(steps_ref, x_ref, wg_ref, wu_ref, wd_ref, y_ref, a_scr):
    k = pl.program_id(0)
    used = steps_ref[k, 1]

    def run(rows):
        x = x_ref[rows, 0:D]
        cw = x_ref[rows, D:XW].astype(F32)
        lane = lax.broadcasted_iota(jnp.int32, cw.shape, 1)
        for j in range(EPG):
            mine = jnp.logical_and((lane & (EPG - 1)) == j, lane < 3 * EPG)
            w = jnp.sum(jnp.where(mine, cw, 0.0), axis=-1, keepdims=True)
            a = (jax.nn.silu(jnp.dot(x, wg_ref[j], preferred_element_type=F32))
                 * jnp.dot(x, wu_ref[j], preferred_element_type=F32))
            a_scr[rows, j * FF:(j + 1) * FF] = (a * w).astype(BF16)
        y_ref[rows, :] = jnp.dot(a_scr[rows, :], wd_ref[...], preferred_element_type=F32).astype(BF16)

    quarter = EX_TM // 4

    @pl.when(used > 2 * quarter)
    def _():
        run(slice(0, EX_TM))

    for rows, above in ((2 * quarter, quarter), (quarter, 0)):
        @pl.when(jnp.logical_and(used > above, used <= rows))
        def _(rows=rows):
            run(slice(EX_TM - rows, EX_TM))
            y_ref[0:EX_TM - rows, :] = jnp.zeros((EX_TM - rows, D), BF16)

    @pl.when(used == 0)
    def _():
        y_ref[...] = jnp.zeros_like(y_ref)


def _experts(steps, xs, wg, wu, wd):
    grid_spec = pltpu.PrefetchScalarGridSpec(
        num_scalar_prefetch=1,
        grid=(EX_NT,),
        in_specs=[pl.BlockSpec((EX_TM, XW), lambda k, st: (k, 0)),
                  pl.BlockSpec((EPG, D, FF), lambda k, st: (st[k, 0], 0, 0)),
                  pl.BlockSpec((EPG, D, FF), lambda k, st: (st[k, 0], 0, 0)),
                  pl.BlockSpec((EPG * FF, D), lambda k, st: (st[k, 0], 0))],
        out_specs=pl.BlockSpec((EX_TM, D), lambda k, st: (k, 0)),
        scratch_shapes=[pltpu.VMEM((EX_TM, EPG * FF), BF16)])
    return pl.pallas_call(
        _expert_kernel,
        out_shape=jax.ShapeDtypeStruct((R_MAX, D), BF16),
        grid_spec=grid_spec,
        compiler_params=_cparams(("arbitrary",)),
        name="experts",
    )(steps, xs, wg, wu, wd.reshape(N_EXP * FF, D))


CB_NPT = N_PROMPT // RT_TM


def _combine_kernel(off_ref, pc_ref, info_ref, x1_ref, mod_ref, g_ref, ys_hbm,
                    yp_ref, ysm_ref, stage, sem):
    i = pl.program_id(0)
    slot = i % 2

    def copy(s, dst, src, rows):
        return pltpu.make_async_copy(ys_hbm.at[pl.ds(src, rows)], stage.at[s, pl.ds(dst, rows)],
                                     sem.at[s])

    def fetch(tile, s):
        _piece_copies(off_ref, pc_ref, tile,
                      lambda loc, glob, rows: copy(s, loc, glob, rows).start(priority=int(rows != BIG)))

    @pl.when(i == 0)
    def _():
        stage[...] = jnp.zeros_like(stage)
        fetch(0, 0)

    @pl.when(i + 1 < RT_NT)
    def _():
        fetch(i + 1, 1 - slot)

    _piece_waits(pc_ref, i, lambda rows: copy(slot, 0, 0, rows).wait())

    dest = info_ref[...][:, 1:2]
    col = lax.broadcasted_iota(jnp.int32, (RT_TM, R_STAGE), 1).astype(F32)
    moe = jnp.dot((col == dest).astype(BF16), stage[slot], preferred_element_type=F32)
    mod = mod_ref[...]
    out = _rms(x1_ref[...] + mod[:, 5 * D:6 * D] * moe) * g_ref[...]

    @pl.when(i < CB_NPT)
    def _():
        yp_ref[...] = out

    @pl.when(i >= CB_NPT)
    def _():
        ysm_ref[...] = out


def _combine(chunk_off, pc, info, x1, mod3, fg, ys):
    npt = CB_NPT
    grid_spec = pltpu.PrefetchScalarGridSpec(
        num_scalar_prefetch=2,
        grid=(RT_NT,),
        in_specs=[pl.BlockSpec((RT_TM, ROUTE_W), lambda i, o, p: (i, 0)),
                  pl.BlockSpec((RT_TM, D), lambda i, o, p: (i, 0)),
                  pl.BlockSpec((None, 1, 6 * D), lambda i, o, p: (_mod_row(RT_TM)(i), 0, 0)),
                  pl.BlockSpec((1, D), lambda i, o, p: (0, 0)),
                  pl.BlockSpec(memory_space=pl.ANY)],
        out_specs=(pl.BlockSpec((RT_TM, D), lambda i, o, p: (jnp.minimum(i, npt - 1), 0)),
                   pl.BlockSpec((RT_TM, D), lambda i, o, p: (jnp.maximum(i - npt, 0), 0))),
        scratch_shapes=[pltpu.VMEM((2, R_STAGE, D), BF16),
                        pltpu.SemaphoreType.DMA((2,))])
    return pl.pallas_call(
        _combine_kernel,
        out_shape=(jax.ShapeDtypeStruct((N_PROMPT, D), F32),
                   jax.ShapeDtypeStruct((N_SAMPLE, D), F32)),
        grid_spec=grid_spec,
        compiler_params=_cparams(("arbitrary",)),
        name="combine",
    )(chunk_off, pc, info, x1, mod3, fg, ys)


def kernel(x_prompt, x_sample, cache_diff_k, cache_diff_v, state_ret_fwd, state_ret_bwd, c, c_ctx,
           w_ada, b_ada, norm1_g, norm2_g, w_in, ret_decay_fwd, ret_decay_bwd, ret_norm_g,
           diff_lambda_q1, diff_lambda_k1, diff_lambda_q2, diff_lambda_k2, diff_subln_g,
           w_ret_o, w_diff_o, w_o, router_group_w, router_group_b, router_expert_w, router_expert_b,
           moe_w_gate, moe_w_up, moe_w_down, final_norm_g):
    l = 0
    lam_init = 0.8 - 0.6 * math.exp(-0.3 * l)
    row = lambda a: a[l].astype(F32)[None, :]

    xp = x_prompt.reshape(N_PROMPT, D)
    xs = x_sample.reshape(N_SAMPLE, D)
    mod3, aux, brt = _modulation(
        lam_init, c_ctx[None, :], c, w_ada[l], b_ada[l][None, :],
        [row(diff_lambda_q1), row(diff_lambda_k1), row(diff_lambda_q2), row(diff_lambda_k2),
         row(ret_decay_fwd), row(ret_decay_bwd), row(router_group_b), row(router_expert_b)])

    proj, kt32, v32 = _inproj(xp, xs, mod3, norm1_g[l][None, :], w_in[l])

    cos_t, sin_t = _rope_tables()
    cache_kt = jnp.transpose(cache_diff_k[:, l], (0, 2, 3, 4, 1)).reshape(4, D, PAST)
    (diff_act, ret_act, s_f, s_b, wg_bf, wu_bf, wd_bf, xs0) = _mixers(
        aux, proj, cache_kt, cache_diff_v, cos_t, sin_t, diff_subln_g[l][None, :],
        state_ret_fwd[:, l], state_ret_bwd[:, l], ret_norm_g[l][None, :],
        (moe_w_gate[l], moe_w_up[l], moe_w_down[l]))

    x1, h2, cw8, info, info_t, chunk_off, pc, steps = _outproj(
        ret_act, diff_act, proj, xp, xs, mod3, norm2_g[l][None, :],
        w_ret_o[l], w_diff_o[l], w_o[l], router_group_w[l].T, router_expert_w[l].T, brt)
    xs_sorted = _dispatch(chunk_off, pc, h2, info_t, cw8, xs0)
    y_sorted = _experts(steps, xs_sorted, wg_bf, wu_bf, wd_bf)
    yp, ys = _combine(chunk_off, pc, info, x1, mod3, final_norm_g[None, :], y_sorted)

    return (yp.reshape(16, T_P, D), ys.reshape(4, T_S, D),
            jnp.transpose(kt32.reshape(16, DIFF_H, 2, DIFF_HD, T_P), (0, 4, 1, 2, 3))[:, None],
            v32.reshape(16, 1, T_P, DIFF_H, 2 * DIFF_HD),
            s_f.reshape(16, 1, RET_H, RET_KD, RET_VD), s_b.reshape(16, 1, RET_H, RET_KD, RET_VD))
```
